```python
import math
import jax, jax.numpy as jnp
from jax import lax
import numpy as np

D_MODEL = 1024
BATCH = 16
SEQ = 2048
DEPTH = 2
DEC_BATCH = 128
DEC_SEQ = 4
PAST_LEN = 16384
PAGE_SIZE = 128

A_HEADS = 8
A_KV_HEADS = 2
A_GROUP = A_HEADS // A_KV_HEADS
A_HEAD_DIM = 64
A_WIDTH = A_HEADS * A_HEAD_DIM
A_KV_WIDTH = A_KV_HEADS * A_HEAD_DIM
WINDOW = 128
ATT_SCALE = A_HEAD_DIM ** -0.5
N_BUCKETS = 32
MAX_DISTANCE = WINDOW
M_HEADS = 4
M_HEAD_DIM = 128
M_WIDTH = M_HEADS * M_HEAD_DIM
CONV_W = 4
M_CHUNK = 64
MIX_WIDTH = A_WIDTH + M_WIDTH
IN_COLS = A_WIDTH + 2 * A_KV_WIDTH + 4 * M_WIDTH + 2 * M_HEADS
N_KEYS = 128
N_EXPERTS = N_KEYS * N_KEYS
P_HEADS = 8
P_TOPK = 16
P_KEY_DIM = 256
P_HALF = P_KEY_DIM // 2
P_TOKEN_BLOCK = 128
EPS = 1e-6
NEG_INF = -1e30

kernel_name = "hymba_swa_mlstm_peer_adaln_step"


def rmsnorm(x, g):
    x32 = x.astype(jnp.float32)
    y = x32 * lax.rsqrt(jnp.mean(x32 * x32, axis=-1, keepdims=True) + EPS)
    return (y * g.astype(jnp.float32)).astype(x.dtype)


def t5_bucket(dist):
    n = jnp.maximum(dist, 0)
    max_exact = N_BUCKETS // 2
    nf = jnp.maximum(n, 1).astype(jnp.float32)
    large = max_exact + (jnp.log(nf / max_exact) / math.log(MAX_DISTANCE / max_exact)
                         * (N_BUCKETS - max_exact)).astype(jnp.int32)
    return jnp.where(n < max_exact, n, jnp.minimum(large, N_BUCKETS - 1))


def rel_bias_heads(rel_bias, dist):
    b = jnp.take(rel_bias, t5_bucket(dist), axis=0).astype(jnp.float32)
    nq, nk = dist.shape
    return b.reshape(nq, nk, A_KV_HEADS, A_GROUP).transpose(2, 3, 0, 1)


def sink_softmax(s, sinks):
    sk = sinks.astype(jnp.float32).reshape(A_KV_HEADS, A_GROUP, 1, 1)
    mx = jnp.maximum(jnp.max(s, axis=-1, keepdims=True), sk)
    e = jnp.exp(s - mx)
    return e / (jnp.sum(e, axis=-1, keepdims=True) + jnp.exp(sk - mx))


def swa_prompt(q, k, v, sinks, rel_bias):
    B, T = q.shape[:2]
    nb = T // WINDOW
    qb = q.reshape(B, nb, WINDOW, A_KV_HEADS, A_GROUP, A_HEAD_DIM)

    def band(t):
        t = t.reshape(B, nb, WINDOW, A_KV_HEADS, A_HEAD_DIM)
        prev = jnp.concatenate([jnp.zeros_like(t[:, :1]), t[:, :-1]], axis=1)
        return jnp.concatenate([prev, t], axis=2)

    kb, vb = band(k), band(v)
    s = jnp.einsum('bnqhgd,bnkhd->bnhgqk', qb, kb, preferred_element_type=jnp.float32) * ATT_SCALE
    qi = jnp.arange(WINDOW)[:, None]
    kj = jnp.arange(2 * WINDOW)[None, :]
    dist = qi + WINDOW - kj
    valid = (dist >= 0) & (dist < WINDOW)
    mask = valid[None] & ((jnp.arange(nb)[:, None, None] > 0) | (kj >= WINDOW)[None])
    s = s + rel_bias_heads(rel_bias, dist)
    s = jnp.where(mask[None, :, None, None], s, NEG_INF)
    p = sink_softmax(s, sinks)
    o = jnp.einsum('bnhgqk,bnkhd->bnqhgd', p.astype(vb.dtype), vb)
    return o.reshape(B, T, A_WIDTH)


def swa_sample(q, k, v, k_buf, v_buf, sinks, rel_bias):
    B, S = q.shape[:2]
    kk = jnp.concatenate([k_buf.astype(k.dtype), k], axis=1)
    vv = jnp.concatenate([v_buf.astype(v.dtype), v], axis=1)
    qg = q.reshape(B, S, A_KV_HEADS, A_GROUP, A_HEAD_DIM)
    s = jnp.einsum('bqhgd,bkhd->bhgqk', qg, kk, preferred_element_type=jnp.float32) * ATT_SCALE
    dist = jnp.arange(S)[:, None] + WINDOW - jnp.arange(WINDOW + S)[None, :]
    valid = (dist >= 0) & (dist < WINDOW)
    s = s + rel_bias_heads(rel_bias, dist)
    s = jnp.where(valid, s, NEG_INF)
    p = sink_softmax(s, sinks)
    o = jnp.einsum('bhgqk,bkhd->bqhgd', p.astype(vv.dtype), vv)
    return o.reshape(B, S, A_WIDTH), kk[:, -WINDOW:], vv[:, -WINDOW:]


def causal_conv(x, buf, w, b):
    T = x.shape[1]
    xp = jnp.concatenate([buf.astype(x.dtype), x], axis=1)
    y = b + sum(xp[:, i:i + T] * w[i] for i in range(CONV_W))
    return jax.nn.silu(y), xp[:, -(CONV_W - 1):]


def mlstm_chunk(carry, xs):
    C, n, m = carry
    q, k, v, ig, lf = xs
    L = q.shape[1]
    b = jnp.cumsum(lf, axis=1)
    causal = jnp.tril(jnp.ones((L, L), dtype=bool))
    logw = b[:, :, None, :] - b[:, None, :, :] + ig[:, None, :, :]
    logw = jnp.where(causal[None, :, :, None], logw, -jnp.inf)
    inter = b + m[:, None, :]
    m_t = jnp.maximum(inter, jnp.max(logw, axis=2))
    w = jnp.exp(logw - m_t[:, :, None, :])
    a = jnp.exp(inter - m_t)
    qk = jnp.einsum('bthd,bshd->btsh', q, k)
    wqk = w * qk
    num = jnp.einsum('btsh,bshd->bthd', wqk, v) + a[..., None] * jnp.einsum('bhvk,bthk->bthv', C, q)
    den = jnp.sum(wqk, axis=2) + a * jnp.einsum('bhk,bthk->bth', n, q)
    h = num / jnp.maximum(jnp.abs(den), jnp.exp(-m_t))[..., None]
    m_new = m_t[:, -1]
    wl = jnp.exp(b[:, -1:, :] - b + ig - m_new[:, None, :])
    al = jnp.exp(b[:, -1] + m - m_new)
    C_new = al[..., None, None] * C + jnp.einsum('bsh,bshv,bshk->bhvk', wl, v, k)
    n_new = al[..., None] * n + jnp.einsum('bsh,bshk->bhk', wl, k)
    return (C_new, n_new, m_new), h


def mlstm_run(q, k, v, ig, lf, state, chunk):
    B, T = q.shape[:2]
    nc = T // chunk

    def to_chunks(t):
        return t.reshape((B, nc, chunk) + t.shape[2:]).swapaxes(0, 1)

    state, h = lax.scan(mlstm_chunk, state, (to_chunks(q), to_chunks(k), to_chunks(v),
                                             to_chunks(ig), to_chunks(lf)))
    return state, h.swapaxes(0, 1).reshape(B, T, M_HEADS, M_HEAD_DIM)


def peer(h, w_query, sub_keys, u, v):
    B, T, D = h.shape
    xf = h.reshape(-1, D)
    N = xf.shape[0]
    pad = (-N) % P_TOKEN_BLOCK
    xb = jnp.pad(xf, ((0, pad), (0, 0))).reshape(-1, P_TOKEN_BLOCK, D)

    def block(xt):
        q = (xt @ w_query).reshape(P_TOKEN_BLOCK, P_HEADS, 2, P_HALF)
        s = jnp.einsum('thcd,hckd->thck', q, sub_keys, preferred_element_type=jnp.float32)
        s_top, i_top = lax.top_k(s, P_TOPK)
        cand = s_top[:, :, 0, :, None] + s_top[:, :, 1, None, :]
        cand_idx = i_top[:, :, 0, :, None] * N_KEYS + i_top[:, :, 1, None, :]
        best, pos = lax.top_k(cand.reshape(P_TOKEN_BLOCK, P_HEADS, P_TOPK * P_TOPK), P_TOPK)
        idx = jnp.take_along_axis(cand_idx.reshape(P_TOKEN_BLOCK, P_HEADS, P_TOPK * P_TOPK), pos, axis=-1)
        g = jax.nn.softmax(best, axis=-1)
        ue = jnp.take(u, idx, axis=0)
        ve = jnp.take(v, idx, axis=0)
        act = jax.nn.gelu(jnp.einsum('thkd,td->thk', ue, xt, preferred_element_type=jnp.float32))
        return jnp.einsum('thk,thkd->td', (g * act).astype(ve.dtype), ve)

    out = lax.map(block, xb)
    return out.reshape(-1, D)[:N].reshape(B, T, D).astype(h.dtype)


def decoder_layer(x, c, rel_bias, w_ada, b_ada, norm_mix, norm_ffn, w_in, conv_w, conv_b, gate_b,
                  sinks, m_norm, w_out, peer_query, peer_keys, peer_u, peer_v,
                  kv_buf, conv_buf, m_state):
    B, T, _ = x.shape
    mod = jax.nn.silu(c) @ w_ada + b_ada
    sh1, sc1, g1, sh2, sc2, g2 = jnp.split(mod[:, None, :], 6, axis=-1)

    h = rmsnorm(x, norm_mix) * (1 + sc1) + sh1
    proj = h @ w_in
    cuts = np.cumsum([A_WIDTH, A_KV_WIDTH, A_KV_WIDTH, M_WIDTH, M_WIDTH, M_WIDTH, M_WIDTH]).tolist()
    q_a, k_a, v_a, q_m, k_m, v_m, o_m, gates = jnp.split(proj, cuts, axis=-1)

    q_a = q_a.reshape(B, T, A_HEADS, A_HEAD_DIM)
    k_a = k_a.reshape(B, T, A_KV_HEADS, A_HEAD_DIM)
    v_a = v_a.reshape(B, T, A_KV_HEADS, A_HEAD_DIM)
    if kv_buf is None:
        att = swa_prompt(q_a, k_a, v_a, sinks, rel_bias)
        new_k, new_v = k_a[:, -WINDOW:], v_a[:, -WINDOW:]
    else:
        att, new_k, new_v = swa_sample(q_a, k_a, v_a, kv_buf[0], kv_buf[1], sinks, rel_bias)

    if conv_buf is None:
        conv_buf = jnp.zeros((B, CONV_W - 1, 2 * M_WIDTH), x.dtype)
    qk_conv, new_conv = causal_conv(jnp.concatenate([q_m, k_m], axis=-1), conv_buf, conv_w, conv_b)
    q_m, k_m = jnp.split(qk_conv, 2, axis=-1)
    gates = gates.astype(jnp.float32) + gate_b.astype(jnp.float32)
    ig = gates[..., :M_HEADS]
    lf = jax.nn.log_sigmoid(gates[..., M_HEADS:])
    if m_state is None:
        state = (jnp.zeros((B, M_HEADS, M_HEAD_DIM, M_HEAD_DIM), jnp.float32),
                 jnp.zeros((B, M_HEADS, M_HEAD_DIM), jnp.float32),
                 jnp.zeros((B, M_HEADS), jnp.float32))
        chunk = M_CHUNK
    else:
        state = tuple(s.astype(jnp.float32) for s in m_state)
        chunk = T
    qf = q_m.reshape(B, T, M_HEADS, M_HEAD_DIM).astype(jnp.float32)
    kf = k_m.reshape(B, T, M_HEADS, M_HEAD_DIM).astype(jnp.float32) * (M_HEAD_DIM ** -0.5)
    vf = v_m.reshape(B, T, M_HEADS, M_HEAD_DIM).astype(jnp.float32)
    (C_new, n_new, m_new), hm = mlstm_run(qf, kf, vf, ig, lf, state, chunk)
    hm = hm * lax.rsqrt(jnp.mean(hm * hm, axis=-1, keepdims=True) + EPS)
    hm = hm * m_norm.astype(jnp.float32).reshape(M_HEADS, M_HEAD_DIM)
    m_out = (jax.nn.sigmoid(o_m.astype(jnp.float32)) * hm.reshape(B, T, M_WIDTH)).astype(x.dtype)

    mix = jnp.concatenate([att, m_out], axis=-1) @ w_out
    x = x + g1 * mix

    h2 = rmsnorm(x, norm_ffn) * (1 + sc2) + sh2
    x = x + g2 * peer(h2, peer_query, peer_keys, peer_u, peer_v)
    return x, new_k, new_v, new_conv, C_new, n_new, m_new


def setup_inputs(seed: int = 0) -> dict:
    key = jax.random.key(seed)
    ks = jax.random.split(key, 32)
    D = D_MODEL

    def nrm(k, shape, s):
        return jax.random.normal(k, shape, jnp.float32) * s

    gate_b = jnp.concatenate([nrm(ks[20], (DEPTH, M_HEADS), 0.1),
                              3.0 + nrm(ks[21], (DEPTH, M_HEADS), 0.5)], axis=-1)
    return {
        "x_prompt": nrm(ks[0], (BATCH, SEQ, D), 1.0),
        "x_sample": nrm(ks[1], (DEC_BATCH, DEC_SEQ, D), 1.0),
        "c_prompt": nrm(ks[2], (BATCH, D), 1.0),
        "c_sample": nrm(ks[3], (DEC_BATCH, D), 1.0),
        "cache_k": nrm(ks[4], (DEPTH, DEC_BATCH, WINDOW, A_KV_HEADS, A_HEAD_DIM), 1.0),
        "cache_v": nrm(ks[5], (DEPTH, DEC_BATCH, WINDOW, A_KV_HEADS, A_HEAD_DIM), 1.0),
        "state_conv": nrm(ks[6], (DEPTH, DEC_BATCH, CONV_W - 1, 2 * M_WIDTH), 1.0),
        "state_C": nrm(ks[7], (DEPTH, DEC_BATCH, M_HEADS, M_HEAD_DIM, M_HEAD_DIM), 0.05),
        "state_n": nrm(ks[8], (DEPTH, DEC_BATCH, M_HEADS, M_HEAD_DIM), 0.5),
        "state_m": nrm(ks[9], (DEPTH, DEC_BATCH, M_HEADS), 0.5),
        "rel_bias": nrm(ks[10], (N_BUCKETS, A_HEADS), 0.5),
        "w_ada": nrm(ks[11], (DEPTH, D, 6 * D), 0.5 * D ** -0.5),
        "b_ada": nrm(ks[12], (DEPTH, 6 * D), 0.1),
        "norm_mix": 1.0 + nrm(ks[13], (DEPTH, D), 0.05),
        "norm_ffn": 1.0 + nrm(ks[14], (DEPTH, D), 0.05),
        "w_in": nrm(ks[15], (DEPTH, D, IN_COLS), D ** -0.5),
        "conv_w": nrm(ks[16], (DEPTH, CONV_W, 2 * M_WIDTH), CONV_W ** -0.5),
        "conv_b": nrm(ks[17], (DEPTH, 2 * M_WIDTH), 0.05),
        "gate_b": gate_b,
        "attn_sinks": nrm(ks[18], (DEPTH, A_HEADS), 1.0),
        "m_norm": 1.0 + nrm(ks[19], (DEPTH, M_WIDTH), 0.05),
        "w_out": nrm(ks[22], (DEPTH, MIX_WIDTH, D), MIX_WIDTH ** -0.5),
        "peer_query": nrm(ks[23], (DEPTH, D, P_HEADS * P_KEY_DIM), D ** -0.5),
        "peer_keys": nrm(ks[24], (DEPTH, P_HEADS, 2, N_KEYS, P_HALF), P_HALF ** -0.5),
        "peer_u": nrm(ks[25], (DEPTH, N_EXPERTS, D), D ** -0.5),
        "peer_v": nrm(ks[26], (DEPTH, N_EXPERTS, D), P_HEADS ** -0.5),
        "norm_final": 1.0 + nrm(ks[27], (D,), 0.05),
    }


def reference(x_prompt, x_sample, c_prompt, c_sample, cache_k, cache_v, state_conv, state_C, state_n,
              state_m, rel_bias, w_ada, b_ada, norm_mix, norm_ffn, w_in, conv_w, conv_b, gate_b,
              attn_sinks, m_norm, w_out, peer_query, peer_keys, peer_u, peer_v, norm_final):
    xp, xs = x_prompt, x_sample
    st_p, st_s = [], []
    for l in range(DEPTH):
        lw = (w_ada[l], b_ada[l], norm_mix[l], norm_ffn[l], w_in[l], conv_w[l], conv_b[l], gate_b[l],
              attn_sinks[l], m_norm[l], w_out[l], peer_query[l], peer_keys[l], peer_u[l], peer_v[l])
        xp, *sp = decoder_layer(xp, c_prompt, rel_bias, *lw, None, None, None)
        xs, *ss = decoder_layer(xs, c_sample, rel_bias, *lw, (cache_k[l], cache_v[l]), state_conv[l],
                                (state_C[l], state_n[l], state_m[l]))
        st_p.append(sp)
        st_s.append(ss)
    y_prompt = rmsnorm(xp, norm_final)
    y_sample = rmsnorm(xs, norm_final)
    k_p, v_p, conv_p, C_p, n_p, m_p = [jnp.stack([s[i] for s in st_p]) for i in range(6)]
    k_s, v_s, conv_s, C_s, n_s, m_s = [jnp.stack([s[i] for s in st_s]) for i in range(6)]
    return (y_prompt, y_sample, k_p, v_p, conv_p, C_p, n_p, m_p, k_s, v_s, conv_s, C_s, n_s, m_s)
```

```python
import functools
import math

import numpy as np
import jax
import jax.numpy as jnp
from jax import lax
from jax.experimental import pallas as pl
from jax.experimental.pallas import tpu as pltpu

F32 = jnp.float32
BF16 = jnp.bfloat16
HIGHEST = lax.Precision.HIGHEST

D_MODEL = 1024
A_HEADS = 8
A_KV_HEADS = 2
A_GROUP = A_HEADS // A_KV_HEADS
A_HEAD_DIM = 64
A_WIDTH = A_HEADS * A_HEAD_DIM
A_KV_WIDTH = A_KV_HEADS * A_HEAD_DIM
WINDOW = 128
ATT_SCALE = A_HEAD_DIM ** -0.5
N_BUCKETS = 32
MAX_DISTANCE = WINDOW
M_HEADS = 4
M_HEAD_DIM = 128
M_WIDTH = M_HEADS * M_HEAD_DIM
CONV_W = 4
M_CHUNK = 64
N_KEYS = 128
P_HEADS = 8
P_TOPK = 16
P_KEY_DIM = 256
P_HALF = P_KEY_DIM // 2
EPS = 1e-6
NEG_INF = -1e30

LANES = 128
SUBLANES = 8
SAMPLE_PAD = SUBLANES
VMEM_LIMIT = 48 * 1024 * 1024

NT_DIMS = (((1,), (1,)), ((), ()))
TN_DIMS = (((0,), (0,)), ((), ()))


def _cparams(*sem):
    return pltpu.CompilerParams(dimension_semantics=sem, vmem_limit_bytes=VMEM_LIMIT)


def _bdot(a, b):
    return jnp.dot(a.astype(BF16), b.astype(BF16), preferred_element_type=F32)


def _bdot_nt(a, b):
    return lax.dot_general(a.astype(BF16), b.astype(BF16), NT_DIMS, preferred_element_type=F32)


def _sigmoid(x):
    return 1.0 / (1.0 + jnp.exp(-x))


def _log_sigmoid(x):
    return jnp.minimum(x, 0.0) - jnp.log1p(jnp.exp(-jnp.abs(x)))


def _gelu_tanh(x):
    c = math.sqrt(2.0 / math.pi)
    return x * (0.5 * (1.0 + jnp.tanh(c * (x + 0.044715 * (x * x * x)))))


def _ada_kernel(c_ref, w_ref, b_ref, o_ref):
    c = c_ref[...]
    s = c * _sigmoid(c)
    o_ref[...] = jnp.dot(s, w_ref[...], precision=HIGHEST, preferred_element_type=F32) + b_ref[...]


def _ada_call(c_all, w_ada, b_ada):
    depth, d, n6 = w_ada.shape
    rows = c_all.shape[0]
    bn = 1024
    return pl.pallas_call(
        _ada_kernel,
        grid=(depth, n6 // bn),
        in_specs=[
            pl.BlockSpec((rows, d), lambda l, j: (0, 0)),
            pl.BlockSpec((None, d, bn), lambda l, j: (l, 0, j)),
            pl.BlockSpec((None, 1, bn), lambda l, j: (l, 0, j)),
        ],
        out_specs=pl.BlockSpec((None, rows, bn), lambda l, j: (l, 0, j)),
        out_shape=jax.ShapeDtypeStruct((depth, rows, n6), F32),
        compiler_params=_cparams("parallel", "parallel"),
        name="ada_mod",
    )(c_all, w_ada, b_ada.reshape(depth, 1, n6))


def _mod_spec(per_token, tm, tokens_per_req):
    if per_token:
        return pl.BlockSpec((tm, D_MODEL), lambda i: (i, 0))
    tiles = tokens_per_req // tm
    return pl.BlockSpec((None, 1, D_MODEL), lambda i: (i // tiles, 0, 0))


def _in_kernel(x_ref, sc_ref, sh_ref, g_ref, wa_ref, wqk_ref, wv_ref, wo_ref, wg_ref, gb_ref,
               qkv_ref, qkm_ref, v_ref, o_ref, gc_ref):
    x = x_ref[...]
    y = x * lax.rsqrt(jnp.mean(x * x, axis=-1, keepdims=True) + EPS) * g_ref[...]
    h = (y * (1.0 + sc_ref[...]) + sh_ref[...]).astype(BF16)
    qkv_ref[...] = jnp.dot(h, wa_ref[...], preferred_element_type=F32)
    qkm_ref[...] = jnp.dot(h, wqk_ref[...], preferred_element_type=F32)
    v_ref[...] = jnp.dot(h, wv_ref[...], preferred_element_type=F32)
    o_ref[...] = jnp.dot(h, wo_ref[...], preferred_element_type=F32)
    g = jnp.dot(h, wg_ref[...], preferred_element_type=F32) + gb_ref[...]
    lane = lax.broadcasted_iota(jnp.int32, g.shape, 1)
    gc_ref[...] = jnp.where(lane < M_HEADS, g, jnp.where(lane < 2 * M_HEADS, _log_sigmoid(g), 0.0))


def _in_call(x, sc, sh, gnorm, wa, wqk, wv, wo, wg, gb, per_token, tokens_per_req):
    n = x.shape[0]
    tm = min(512, n if per_token else tokens_per_req)
    mod = _mod_spec(per_token, tm, tokens_per_req)
    full = lambda shape: pl.BlockSpec(shape, lambda i: (0,) * len(shape))
    row = lambda w: pl.BlockSpec((tm, w), lambda i: (i, 0))
    return pl.pallas_call(
        _in_kernel,
        grid=(n // tm,),
        in_specs=[row(D_MODEL), mod, mod, full((1, D_MODEL)), full(wa.shape), full(wqk.shape),
                  full(wv.shape), full(wo.shape), full(wg.shape), full((1, LANES))],
        out_specs=[row(wa.shape[1]), row(wqk.shape[1]), row(wv.shape[1]), row(wo.shape[1]), row(LANES)],
        out_shape=[jax.ShapeDtypeStruct((n, w), F32)
                   for w in (wa.shape[1], wqk.shape[1], wv.shape[1], wo.shape[1], LANES)],
        compiler_params=_cparams("parallel"),
        name="in_proj",
    )(x, sc, sh, gnorm, wa, wqk, wv, wo, wg, gb)


def _t5_bucket_np(dist):
    n = np.maximum(dist, 0)
    max_exact = N_BUCKETS // 2
    nf = np.maximum(n, 1).astype(np.float64)
    large = max_exact + (np.log(nf / max_exact) / math.log(MAX_DISTANCE / max_exact)
                         * (N_BUCKETS - max_exact)).astype(np.int32)
    return np.where(n < max_exact, n, np.minimum(large, N_BUCKETS - 1)).astype(np.int32)


def _bias_kernel(bucket_ref, rel_ref, o_ref):
    bucket = bucket_ref[...]
    for h in range(A_HEADS):
        acc = jnp.zeros(bucket.shape, F32)
        for b in range(N_BUCKETS):
            acc = jnp.where(bucket == b, rel_ref[b, h], acc)
        o_ref[h] = acc


def _bias_call(rel_bias, dist):
    bucket = jnp.asarray(_t5_bucket_np(dist))
    nq, nk = dist.shape
    return pl.pallas_call(
        _bias_kernel,
        in_specs=[pl.BlockSpec((nq, nk), lambda: (0, 0)),
                  pl.BlockSpec(memory_space=pltpu.SMEM)],
        out_specs=pl.BlockSpec((A_HEADS, nq, nk), lambda: (0, 0, 0)),
        out_shape=jax.ShapeDtypeStruct((A_HEADS, nq, nk), F32),
        name="t5_bias",
    )(bucket, rel_bias)


def _attn_p_kernel(q_ref, kp_ref, kc_ref, vp_ref, vc_ref, bias_ref, sink_ref, o_ref):
    i = pl.program_id(1)
    qi = lax.broadcasted_iota(jnp.int32, (WINDOW, WINDOW), 0)
    kj = lax.broadcasted_iota(jnp.int32, (WINDOW, WINDOW), 1)
    valid_prev = jnp.logical_and(kj > qi, i > 0)
    valid_cur = kj <= qi
    q = q_ref[...]
    for h in range(A_HEADS):
        kv = h // A_GROUP
        qh = q[:, h * A_HEAD_DIM:(h + 1) * A_HEAD_DIM]
        sl = slice(kv * A_HEAD_DIM, (kv + 1) * A_HEAD_DIM)
        bias = bias_ref[h]
        sp = _bdot_nt(qh, kp_ref[:, sl]) * ATT_SCALE + bias[:, :WINDOW]
        sc = _bdot_nt(qh, kc_ref[:, sl]) * ATT_SCALE + bias[:, WINDOW:]
        sp = jnp.where(valid_prev, sp, NEG_INF)
        sc = jnp.where(valid_cur, sc, NEG_INF)
        sink = sink_ref[0, h]
        mx = jnp.maximum(jnp.maximum(jnp.max(sp, axis=-1, keepdims=True),
                                     jnp.max(sc, axis=-1, keepdims=True)), sink)
        ep = jnp.exp(sp - mx)
        ec = jnp.exp(sc - mx)
        den = (jnp.sum(ep, axis=-1, keepdims=True) + jnp.sum(ec, axis=-1, keepdims=True)
               + jnp.exp(sink - mx))
        o = _bdot(ep / den, vp_ref[:, sl]) + _bdot(ec / den, vc_ref[:, sl])
        o_ref[:, h * A_HEAD_DIM:(h + 1) * A_HEAD_DIM] = o


def _attn_p_call(qkv, bias, sinks, batch, seq):
    nb = seq // WINDOW
    n = batch * seq
    kcol = A_WIDTH // A_KV_WIDTH
    vcol = kcol + 1
    cur = lambda col: pl.BlockSpec((WINDOW, A_KV_WIDTH), lambda b, i: (b * nb + i, col))
    prev = lambda col: pl.BlockSpec((WINDOW, A_KV_WIDTH),
                                    lambda b, i: (b * nb + jnp.maximum(i - 1, 0), col))
    return pl.pallas_call(
        _attn_p_kernel,
        grid=(batch, nb),
        in_specs=[pl.BlockSpec((WINDOW, A_WIDTH), lambda b, i: (b * nb + i, 0)),
                  prev(kcol), cur(kcol), prev(vcol), cur(vcol),
                  pl.BlockSpec((A_HEADS, WINDOW, 2 * WINDOW), lambda b, i: (0, 0, 0)),
                  pl.BlockSpec(memory_space=pltpu.SMEM)],
        out_specs=pl.BlockSpec((WINDOW, A_WIDTH), lambda b, i: (b * nb + i, 0)),
        out_shape=jax.ShapeDtypeStruct((n, A_WIDTH), F32),
        compiler_params=_cparams("parallel", "parallel"),
        name="swa_prompt",
    )(qkv, qkv, qkv, qkv, qkv, bias, sinks.reshape(1, A_HEADS))


def _attn_s_kernel(n_new, qkv_ref, ck_ref, cv_ref, bc_ref, bn_ref, sink_ref,
                   o_ref, nk_ref, nv_ref, kk_s, vv_s):
    qkv = qkv_ref[...]
    knew = qkv[:, A_WIDTH:A_WIDTH + A_KV_WIDTH]
    vnew = qkv[:, A_WIDTH + A_KV_WIDTH:A_WIDTH + 2 * A_KV_WIDTH]
    ck = ck_ref[...]
    cv = cv_ref[...]
    qi = lax.broadcasted_iota(jnp.int32, (SAMPLE_PAD, WINDOW), 0)
    kj = lax.broadcasted_iota(jnp.int32, (SAMPLE_PAD, WINDOW), 1)
    valid_c = kj > qi
    qcol = lax.broadcasted_iota(jnp.int32, (SAMPLE_PAD, 1), 0)
    for h in range(A_HEADS):
        kv = h // A_GROUP
        sl = slice(kv * A_HEAD_DIM, (kv + 1) * A_HEAD_DIM)
        qh = qkv[:, h * A_HEAD_DIM:(h + 1) * A_HEAD_DIM]
        s_c = lax.dot_general(qh, ck[:, sl], NT_DIMS, precision=HIGHEST,
                              preferred_element_type=F32) * ATT_SCALE + bc_ref[h]
        s_c = jnp.where(valid_c, s_c, NEG_INF)
        bn = bn_ref[h]
        s_n = []
        for j in range(n_new):
            sj = jnp.sum(qh * knew[j:j + 1, sl], axis=-1, keepdims=True) * ATT_SCALE + bn[:, j:j + 1]
            s_n.append(jnp.where(qcol >= j, sj, NEG_INF))
        sink = sink_ref[0, h]
        mx = jnp.maximum(jnp.max(s_c, axis=-1, keepdims=True), sink)
        for sj in s_n:
            mx = jnp.maximum(mx, sj)
        e_c = jnp.exp(s_c - mx)
        den = jnp.sum(e_c, axis=-1, keepdims=True) + jnp.exp(sink - mx)
        o = jnp.dot(e_c, cv[:, sl], precision=HIGHEST, preferred_element_type=F32)
        for j, sj in enumerate(s_n):
            ej = jnp.exp(sj - mx)
            den = den + ej
            o = o + ej * vnew[j:j + 1, sl]
        o_ref[:, h * A_HEAD_DIM:(h + 1) * A_HEAD_DIM] = o / den
    kk_s[0:WINDOW, :] = ck
    kk_s[WINDOW:WINDOW + SAMPLE_PAD, :] = knew
    vv_s[0:WINDOW, :] = cv
    vv_s[WINDOW:WINDOW + SAMPLE_PAD, :] = vnew
    nk_ref[...] = kk_s[n_new:n_new + WINDOW, :]
    nv_ref[...] = vv_s[n_new:n_new + WINDOW, :]


def _attn_s_call(qkv, ck, cv, bias_c, bias_n, sinks, n_new):
    nreq = ck.shape[0]
    wq = qkv.shape[1]
    full3 = lambda shape: pl.BlockSpec(shape, lambda b: (0, 0, 0))
    cache = pl.BlockSpec((None, WINDOW, A_KV_WIDTH), lambda b: (b, 0, 0))
    return pl.pallas_call(
        functools.partial(_attn_s_kernel, n_new),
        grid=(nreq,),
        in_specs=[pl.BlockSpec((SAMPLE_PAD, wq), lambda b: (b, 0)), cache, cache,
                  full3(bias_c.shape), full3(bias_n.shape),
                  pl.BlockSpec(memory_space=pltpu.SMEM)],
        out_specs=[pl.BlockSpec((SAMPLE_PAD, A_WIDTH), lambda b: (b, 0)), cache, cache],
        out_shape=[jax.ShapeDtypeStruct((nreq * SAMPLE_PAD, A_WIDTH), F32),
                   jax.ShapeDtypeStruct(ck.shape, F32), jax.ShapeDtypeStruct(cv.shape, F32)],
        scratch_shapes=[pltpu.VMEM((WINDOW + SAMPLE_PAD, A_KV_WIDTH), F32),
                        pltpu.VMEM((WINDOW + SAMPLE_PAD, A_KV_WIDTH), F32)],
        compiler_params=_cparams("parallel"),
        name="swa_sample",
    )(qkv, ck, cv, bias_c, bias_n, sinks.reshape(1, A_HEADS))


def _mlstm_kernel(chunk, t_valid, qk_ref, v_ref, og_ref, gc_ref, cw_ref, cb_ref, mn_ref,
                  conv0_ref, c0_ref, n0_ref, m0_ref,
                  out_ref, cout_ref, nout_ref, mout_ref,
                  xp_s, c_s, n_s, m_s):
    step = pl.program_id(1)
    halo = SUBLANES

    @pl.when(step == 0)
    def _():
        xp_s[0:halo, :] = conv0_ref[...]
        c_s[...] = c0_ref[...]
        n_s[...] = n0_ref[...]
        m_s[...] = m0_ref[...]

    xp_s[halo:halo + chunk, :] = qk_ref[...]
    cw = cw_ref[...]
    y = cb_ref[...]
    for i in range(CONV_W):
        off = halo - (CONV_W - 1) + i
        y = y + xp_s[off:off + chunk, :] * cw[i:i + 1, :]
    xp_s[0:halo, :] = xp_s[chunk:chunk + halo, :]
    y = y * _sigmoid(y)
    q_all = y[:, :M_WIDTH]
    k_all = y[:, M_WIDTH:] * (M_HEAD_DIM ** -0.5)

    g = gc_ref[...]
    if t_valid < chunk:
        row = lax.broadcasted_iota(jnp.int32, g.shape, 0)
        lane = lax.broadcasted_iota(jnp.int32, g.shape, 1)
        g = jnp.where(row < t_valid, g, jnp.where(lane < M_HEADS, NEG_INF, 0.0))
    tr = lax.broadcasted_iota(jnp.int32, (chunk, chunk), 0)
    tc = lax.broadcasted_iota(jnp.int32, (chunk, chunk), 1)
    causal = tr >= tc
    tri = causal.astype(F32)
    bcol = jnp.dot(tri, g, precision=HIGHEST, preferred_element_type=F32)
    er = lax.broadcasted_iota(jnp.int32, (SUBLANES, LANES), 0)
    ec = lax.broadcasted_iota(jnp.int32, (SUBLANES, LANES), 1)
    eye = (er == ec).astype(F32)
    g_rows = lax.dot_general(eye, g, NT_DIMS, precision=HIGHEST, preferred_element_type=F32)
    b_rows = lax.dot_general(eye, bcol, NT_DIMS, precision=HIGHEST, preferred_element_type=F32)

    for h in range(M_HEADS):
        hs = slice(h * M_HEAD_DIM, (h + 1) * M_HEAD_DIM)
        b_c = bcol[:, M_HEADS + h:M_HEADS + h + 1]
        ig_c = g[:, h:h + 1]
        b_r = b_rows[M_HEADS + h:M_HEADS + h + 1, :]
        ig_r = g_rows[h:h + 1, :]
        m_prev = m_s[h:h + 1, 0:1]
        logw = jnp.where(causal, b_c - b_r + ig_r, -jnp.inf)
        inter = b_c + m_prev
        m_t = jnp.maximum(inter, jnp.max(logw, axis=-1, keepdims=True))
        w = jnp.exp(logw - m_t)
        a = jnp.exp(inter - m_t)
        q = q_all[:, hs]
        k = k_all[:, hs]
        v = v_ref[:, hs]
        cmat = c_s[h]
        nvec = n_s[h:h + 1, :]
        wqk = w * _bdot_nt(q, k)
        num = _bdot(wqk, v) + a * _bdot_nt(q, cmat)
        den = jnp.sum(wqk, axis=-1, keepdims=True) + a * jnp.sum(q * nvec, axis=-1, keepdims=True)
        hh = num / jnp.maximum(jnp.abs(den), jnp.exp(-m_t))
        m_new = m_t[chunk - 1:chunk, :]
        b_last = b_c[chunk - 1:chunk, :]
        wl = jnp.exp(b_last - b_c + ig_c - m_new)
        al = jnp.exp(b_last + m_prev - m_new)
        c_s[h] = al * cmat + lax.dot_general((v * wl).astype(BF16), k.astype(BF16), TN_DIMS,
                                             preferred_element_type=F32)
        n_s[h:h + 1, :] = al * nvec + jnp.sum(wl * k, axis=0, keepdims=True)
        m_s[h:h + 1, :] = jnp.broadcast_to(m_new, (1, LANES))
        hn = hh * lax.rsqrt(jnp.mean(hh * hh, axis=-1, keepdims=True) + EPS) * mn_ref[:, hs]
        out_ref[:, hs] = _sigmoid(og_ref[:, hs]) * hn

    @pl.when(step == pl.num_programs(1) - 1)
    def _():
        cout_ref[...] = c_s[...]
        nout_ref[...] = n_s[...]
        mout_ref[...] = m_s[...]


def _mlstm_call(qk, v, og, gc, conv_w, conv_b, m_norm, conv0, c0, n0, m0, batch, seq, chunk, t_valid):
    nc = seq // chunk
    n = batch * seq
    row = lambda w: pl.BlockSpec((chunk, w), lambda b, c: (b * nc + c, 0))
    full2 = lambda shape: pl.BlockSpec(shape, lambda b, c: (0, 0))
    per_b = lambda shape: pl.BlockSpec((None,) + shape, lambda b, c: (b,) + (0,) * len(shape))
    dh = M_HEAD_DIM
    return pl.pallas_call(
        functools.partial(_mlstm_kernel, chunk, t_valid),
        grid=(batch, nc),
        in_specs=[row(2 * M_WIDTH), row(M_WIDTH), row(M_WIDTH), row(LANES),
                  full2((CONV_W, 2 * M_WIDTH)), full2((1, 2 * M_WIDTH)), full2((1, M_WIDTH)),
                  per_b((SUBLANES, 2 * M_WIDTH)), per_b((M_HEADS, dh, dh)), per_b((M_HEADS, dh)),
                  per_b((SUBLANES, LANES))],
        out_specs=[row(M_WIDTH), per_b((M_HEADS, dh, dh)), per_b((M_HEADS, dh)), per_b((SUBLANES, LANES))],
        out_shape=[jax.ShapeDtypeStruct((n, M_WIDTH), F32),
                   jax.ShapeDtypeStruct((batch, M_HEADS, dh, dh), F32),
                   jax.ShapeDtypeStruct((batch, M_HEADS, dh), F32),
                   jax.ShapeDtypeStruct((batch, SUBLANES, LANES), F32)],
        scratch_shapes=[pltpu.VMEM((SUBLANES + chunk, 2 * M_WIDTH), F32),
                        pltpu.VMEM((M_HEADS, dh, dh), F32),
                        pltpu.VMEM((M_HEADS, dh), F32),
                        pltpu.VMEM((SUBLANES, LANES), F32)],
        compiler_params=_cparams("parallel", "arbitrary"),
        name="mlstm",
    )(qk, v, og, gc, conv_w, conv_b, m_norm, conv0, c0, n0, m0)


def _out_kernel(att_ref, mo_ref, x_ref, g1_ref, sc_ref, sh_ref, gn_ref, wa_ref, wm_ref, wq_ref,
                xo_ref, h2_ref, qp_ref):
    mix = (jnp.dot(att_ref[...].astype(BF16), wa_ref[...], preferred_element_type=F32)
           + jnp.dot(mo_ref[...].astype(BF16), wm_ref[...], preferred_element_type=F32))
    x = x_ref[...] + g1_ref[...] * mix
    xo_ref[...] = x
    y = x * lax.rsqrt(jnp.mean(x * x, axis=-1, keepdims=True) + EPS) * gn_ref[...]
    h2 = y * (1.0 + sc_ref[...]) + sh_ref[...]
    h2_ref[...] = h2
    qp_ref[...] = jnp.dot(h2.astype(BF16), wq_ref[...], preferred_element_type=F32)


def _out_call(att, mo, x, g1, sc, sh, gnorm, wa, wm, wq, per_token, tokens_per_req):
    n = x.shape[0]
    tm = min(256, n if per_token else tokens_per_req)
    mod = _mod_spec(per_token, tm, tokens_per_req)
    full = lambda shape: pl.BlockSpec(shape, lambda i: (0,) * len(shape))
    row = lambda w: pl.BlockSpec((tm, w), lambda i: (i, 0))
    nq = wq.shape[1]
    return pl.pallas_call(
        _out_kernel,
        grid=(n // tm,),
        in_specs=[row(A_WIDTH), row(M_WIDTH), row(D_MODEL), mod, mod, mod, full((1, D_MODEL)),
                  full(wa.shape), full(wm.shape), full(wq.shape)],
        out_specs=[row(D_MODEL), row(D_MODEL), row(nq)],
        out_shape=[jax.ShapeDtypeStruct((n, D_MODEL), F32), jax.ShapeDtypeStruct((n, D_MODEL), F32),
                   jax.ShapeDtypeStruct((n, nq), F32)],
        compiler_params=_cparams("parallel"),
        name="out_proj",
    )(att, mo, x, g1, sc, sh, gnorm, wa, wm, wq)


def _pk_cells():
    return [(a, b) for a in range(P_TOPK) for b in range(P_TOPK) if (a + 1) * (b + 1) <= P_TOPK]


def _pk_expand_mats():
    cells = _pk_cells()
    e0 = np.zeros((LANES, LANES), np.float32)
    e1 = np.zeros((LANES, LANES), np.float32)
    for j, (a, b) in enumerate(cells):
        e0[a, j] = 1.0
        e1[b, j] = 1.0
    return e0, e1, len(cells)


def _top_rounds(s, lanef, rounds):
    vals = jnp.full(s.shape, -jnp.inf, F32)
    idxs = jnp.zeros(s.shape, F32)
    for r in range(rounds):
        m = jnp.max(s, axis=-1, keepdims=True)
        i = jnp.min(jnp.where(s == m, lanef, float(LANES)), axis=-1, keepdims=True)
        vals = jnp.where(lanef == r, m, vals)
        idxs = jnp.where(lanef == r, i, idxs)
        s = jnp.where(lanef == i, -jnp.inf, s)
    return vals, idxs


def _select_kernel(n_cells, qp_ref, keys_ref, e0_ref, e1_ref, idx_ref, gw_ref):
    tm = qp_ref.shape[0]
    lanef = lax.broadcasted_iota(jnp.int32, (tm, LANES), 1).astype(F32)
    e0 = e0_ref[...]
    e1 = e1_ref[...]

    def head(h, carry):
        idx_all, best_all = carry
        h = jnp.asarray(h, jnp.int32)
        sub = []
        for c in range(2):
            col = pl.multiple_of((h * 2 + c) * P_HALF, P_HALF)
            s = _bdot_nt(qp_ref[:, pl.ds(col, P_HALF)], keys_ref[h, c])
            v, i = _top_rounds(s, lanef, P_TOPK)
            sub.append((jnp.where(lanef < P_TOPK, v, 0.0), i))
        (v0, i0), (v1, i1) = sub
        base = (h * P_TOPK).astype(F32)
        cand = (jnp.dot(v0, e0, precision=HIGHEST, preferred_element_type=F32)
                + jnp.dot(v1, e1, precision=HIGHEST, preferred_element_type=F32))
        cidx = (jnp.dot(i0 * float(N_KEYS), e0, precision=HIGHEST, preferred_element_type=F32)
                + jnp.dot(i1, e1, precision=HIGHEST, preferred_element_type=F32))
        cand = jnp.where(lanef < n_cells, cand, -jnp.inf)
        for r in range(P_TOPK):
            m = jnp.max(cand, axis=-1, keepdims=True)
            j = jnp.min(jnp.where(cand == m, lanef, float(LANES)), axis=-1, keepdims=True)
            hit = lanef == j
            e = jnp.max(jnp.where(hit, cidx, -1.0), axis=-1, keepdims=True)
            slot = lanef == (base + r)
            best_all = jnp.where(slot, m, best_all)
            idx_all = jnp.where(slot, e, idx_all)
            cand = jnp.where(hit, -jnp.inf, cand)
        return idx_all, best_all

    idx_all, best_all = lax.fori_loop(
        0, P_HEADS, head, (jnp.zeros((tm, LANES), F32), jnp.full((tm, LANES), -jnp.inf, F32)))
    gw = jnp.zeros((tm, LANES), F32)
    for h in range(P_HEADS):
        mask = jnp.logical_and(lanef >= h * P_TOPK, lanef < (h + 1) * P_TOPK)
        bh = jnp.where(mask, best_all, -jnp.inf)
        mx = jnp.max(bh, axis=-1, keepdims=True)
        e = jnp.exp(bh - mx)
        gw = gw + e / jnp.sum(e, axis=-1, keepdims=True)
    idx_ref[...] = idx_all.astype(jnp.int32)
    gw_ref[...] = gw


def _select_call(qp, keys_bf16):
    n = qp.shape[0]
    tm = min(128, n)
    e0, e1, n_cells = _pk_expand_mats()
    full = lambda shape: pl.BlockSpec(shape, lambda i: (0,) * len(shape))
    return pl.pallas_call(
        functools.partial(_select_kernel, n_cells),
        grid=(n // tm,),
        in_specs=[pl.BlockSpec((tm, qp.shape[1]), lambda i: (i, 0)), full(keys_bf16.shape),
                  full((LANES, LANES)), full((LANES, LANES))],
        out_specs=[pl.BlockSpec((tm, LANES), lambda i: (i, 0)), pl.BlockSpec((tm, LANES), lambda i: (i, 0))],
        out_shape=[jax.ShapeDtypeStruct((n, LANES), jnp.int32), jax.ShapeDtypeStruct((n, LANES), F32)],
        compiler_params=_cparams("parallel"),
        name="peer_select",
    )(qp, keys_bf16, jnp.asarray(e0), jnp.asarray(e1))


N_SEL = P_HEADS * P_TOPK
EXPERT_TB = 32


def _expert_kernel(idx_ref, gw_ref, h2_ref, x_ref, g2_ref, u_hbm, v_hbm, o_ref, ubuf, vbuf, sem):
    tb = gw_ref.shape[0]

    def row_copy(table, buf, which, slot, t, k):
        e = idx_ref[t, k]
        return pltpu.make_async_copy(table.at[pl.ds(e, 1)], buf.at[slot, pl.ds(k, 1)], sem.at[which, slot])

    def issue(t, slot):
        def body(k, carry):
            row_copy(u_hbm, ubuf, 0, slot, t, k).start()
            row_copy(v_hbm, vbuf, 1, slot, t, k).start()
            return carry
        lax.fori_loop(0, N_SEL, body, 0, unroll=8)

    def wait(slot):
        pltpu.make_async_copy(u_hbm.at[pl.ds(0, N_SEL)], ubuf.at[slot], sem.at[0, slot]).wait()
        pltpu.make_async_copy(v_hbm.at[pl.ds(0, N_SEL)], vbuf.at[slot], sem.at[1, slot]).wait()

    def compute(t, slot):
        x = h2_ref[pl.ds(t, 1), :]
        x8 = jnp.broadcast_to(x, (SUBLANES, D_MODEL))
        act = lax.dot_general(x8, ubuf[slot], NT_DIMS, preferred_element_type=F32)
        coef = _gelu_tanh(act) * gw_ref[pl.ds(t, 1), :]
        y = jnp.dot(coef, vbuf[slot], preferred_element_type=F32)
        g2 = g2_ref[...] if g2_ref.shape[0] == 1 else g2_ref[pl.ds(t, 1), :]
        o_ref[pl.ds(t, 1), :] = x_ref[pl.ds(t, 1), :] + g2 * y[0:1, :]

    issue(0, 0)

    def pair(p, carry):
        t0 = 2 * p
        issue(t0 + 1, 1)
        wait(0)
        compute(t0, 0)

        @pl.when(t0 + 2 < tb)
        def _():
            issue(t0 + 2, 0)
        wait(1)
        compute(t0 + 1, 1)
        return carry

    lax.fori_loop(0, tb // 2, pair, 0)


def _expert_call(idx, gw, h2, x, g2, u, v, per_token, tokens_per_req):
    n = x.shape[0]
    tb = EXPERT_TB
    mod = _mod_spec(per_token, tb, tokens_per_req)
    row = lambda w: pl.BlockSpec((tb, w), lambda i: (i, 0))
    return pl.pallas_call(
        _expert_kernel,
        grid=(n // tb,),
        in_specs=[pl.BlockSpec((tb, N_SEL), lambda i: (i, 0), memory_space=pltpu.SMEM),
                  row(N_SEL), row(D_MODEL), row(D_MODEL), mod,
                  pl.BlockSpec(memory_space=pl.ANY), pl.BlockSpec(memory_space=pl.ANY)],
        out_specs=row(D_MODEL),
        out_shape=jax.ShapeDtypeStruct((n, D_MODEL), F32),
        scratch_shapes=[pltpu.VMEM((2, N_SEL, D_MODEL), F32), pltpu.VMEM((2, N_SEL, D_MODEL), F32),
                        pltpu.SemaphoreType.DMA((2, 2))],
        compiler_params=_cparams("arbitrary"),
        name="peer_experts",
    )(idx, gw, h2, x, g2, u, v)


def _final_kernel(x_ref, g_ref, o_ref):
    x = x_ref[...]
    o_ref[...] = x * lax.rsqrt(jnp.mean(x * x, axis=-1, keepdims=True) + EPS) * g_ref[...]


def _final_call(x, g):
    n = x.shape[0]
    tm = min(512, n)
    return pl.pallas_call(
        _final_kernel,
        grid=(n // tm,),
        in_specs=[pl.BlockSpec((tm, D_MODEL), lambda i: (i, 0)), pl.BlockSpec((1, D_MODEL), lambda i: (0, 0))],
        out_specs=pl.BlockSpec((tm, D_MODEL), lambda i: (i, 0)),
        out_shape=jax.ShapeDtypeStruct((n, D_MODEL), F32),
        compiler_params=_cparams("parallel"),
        name="final_norm",
    )(x, g)


def _split_w_in(w_in_l, gate_b_l):
    cuts = np.cumsum([A_WIDTH + 2 * A_KV_WIDTH, 2 * M_WIDTH, M_WIDTH, M_WIDTH]).tolist()
    wa = w_in_l[:, :cuts[0]].astype(BF16)
    wqk = w_in_l[:, cuts[0]:cuts[1]].astype(BF16)
    wv = w_in_l[:, cuts[1]:cuts[2]].astype(BF16)
    wo = w_in_l[:, cuts[2]:cuts[3]].astype(BF16)
    ng = 2 * M_HEADS
    wg = jnp.pad(w_in_l[:, cuts[3]:], ((0, 0), (0, LANES - ng))).astype(BF16)
    gb = jnp.pad(gate_b_l.astype(F32), (0, LANES - ng)).reshape(1, LANES)
    return wa, wqk, wv, wo, wg, gb


def _layer(x, mods, per_token, batch, seq, t_valid, lw, bias_p, bias_c, bias_n, kv_cache, conv0, state):
    (norm_mix, norm_ffn, w_in, conv_w, conv_b, gate_b, sinks, m_norm, w_out, peer_query, peer_keys,
     peer_u, peer_v) = lw
    sh1, sc1, g1, sh2, sc2, g2 = mods
    wa, wqk, wv, wo, wg, gb = _split_w_in(w_in, gate_b)
    qkv, qkm, vm, om, gc = _in_call(x, sc1, sh1, norm_mix.reshape(1, -1), wa, wqk, wv, wo, wg, gb,
                                    per_token, seq)
    if kv_cache is None:
        att = _attn_p_call(qkv, bias_p, sinks, batch, seq)
        kv3 = qkv.reshape(batch, seq, -1)
        new_k = kv3[:, seq - WINDOW:, A_WIDTH:A_WIDTH + A_KV_WIDTH]
        new_v = kv3[:, seq - WINDOW:, A_WIDTH + A_KV_WIDTH:]
        chunk = M_CHUNK
    else:
        att, new_k, new_v = _attn_s_call(qkv, kv_cache[0], kv_cache[1], bias_c, bias_n, sinks, t_valid)
        chunk = seq
    c0, n0, m0 = state
    mo, c_new, n_new, m_new = _mlstm_call(qkm, vm, om, gc, conv_w, conv_b.reshape(1, -1),
                                          m_norm.reshape(1, -1), conv0, c0, n0, m0,
                                          batch, seq, chunk, min(t_valid, chunk))
    new_conv = qkm.reshape(batch, seq, -1)[:, t_valid - (CONV_W - 1):t_valid]
    x_mid, h2, qp = _out_call(att, mo, x, g1, sc2, sh2, norm_ffn.reshape(1, -1),
                              w_out[:A_WIDTH].astype(BF16), w_out[A_WIDTH:].astype(BF16),
                              peer_query.astype(BF16), per_token, seq)
    idx, gw = _select_call(qp, peer_keys.astype(BF16))
    x_new = _expert_call(idx, gw, h2, x_mid, g2, peer_u, peer_v, per_token, seq)
    new_k = new_k.reshape(batch, WINDOW, A_KV_HEADS, A_HEAD_DIM)
    new_v = new_v.reshape(batch, WINDOW, A_KV_HEADS, A_HEAD_DIM)
    return x_new, (new_k, new_v, new_conv, c_new, n_new, m_new[:, :M_HEADS, 0])


def kernel(x_prompt, x_sample, c_prompt, c_sample, cache_k, cache_v, state_conv, state_C, state_n, state_m, rel_bias, w_ada, b_ada, norm_mix, norm_ffn, w_in, conv_w, conv_b, gate_b, attn_sinks, m_norm, w_out, peer_query, peer_keys, peer_u, peer_v, norm_final):
    depth = w_ada.shape[0]
    bp, tp, d = x_prompt.shape
    bs, ts, _ = x_sample.shape
    assert tp % WINDOW == 0 and tp % M_CHUNK == 0 and ts <= SAMPLE_PAD and ts >= CONV_W - 1

    mod_all = _ada_call(jnp.concatenate([c_prompt, c_sample], axis=0), w_ada, b_ada)

    qi = np.arange(WINDOW)[:, None]
    bias_p = _bias_call(rel_bias, qi + WINDOW - np.arange(2 * WINDOW)[None, :])
    qs = np.arange(SAMPLE_PAD)[:, None]
    bias_c = _bias_call(rel_bias, qs + WINDOW - np.arange(WINDOW)[None, :])
    bias_n = _bias_call(rel_bias, qs - np.arange(SAMPLE_PAD)[None, :])

    xp = x_prompt.reshape(bp * tp, d)
    xs = jnp.pad(x_sample, ((0, 0), (0, SAMPLE_PAD - ts), (0, 0))).reshape(bs * SAMPLE_PAD, d)
    halo_pad = ((0, 0), (SUBLANES - (CONV_W - 1), 0), (0, 0))
    zero_state = (jnp.zeros((bp, M_HEADS, M_HEAD_DIM, M_HEAD_DIM), F32),
                  jnp.zeros((bp, M_HEADS, M_HEAD_DIM), F32),
                  jnp.zeros((bp, SUBLANES, LANES), F32))
    zero_conv = jnp.zeros((bp, SUBLANES, 2 * M_WIDTH), F32)

    st_p, st_s = [], []
    for l in range(depth):
        lw = (norm_mix[l], norm_ffn[l], w_in[l], conv_w[l], conv_b[l], gate_b[l], attn_sinks[l], m_norm[l],
              w_out[l], peer_query[l], peer_keys[l], peer_u[l], peer_v[l])
        mod_p = [m.reshape(bp, 1, d) for m in jnp.split(mod_all[l, :bp], 6, axis=-1)]
        mod_s = [jnp.repeat(m, SAMPLE_PAD, axis=0) for m in jnp.split(mod_all[l, bp:], 6, axis=-1)]
        xp, sp = _layer(xp, mod_p, False, bp, tp, tp, lw, bias_p, None, None, None, zero_conv, zero_state)
        state_s = (state_C[l].astype(F32), state_n[l].astype(F32),
                   jnp.broadcast_to(jnp.pad(state_m[l].astype(F32), ((0, 0), (0, SUBLANES - M_HEADS)))[:, :, None],
                                    (bs, SUBLANES, LANES)))
        kv_cache = (cache_k[l].reshape(bs, WINDOW, A_KV_WIDTH), cache_v[l].reshape(bs, WINDOW, A_KV_WIDTH))
        xs, ss = _layer(xs, mod_s, True, bs, SAMPLE_PAD, ts, lw, None, bias_c, bias_n, kv_cache,
                        jnp.pad(state_conv[l].astype(F32), halo_pad), state_s)
        st_p.append(sp)
        st_s.append(ss)

    gfin = norm_final.reshape(1, d)
    y_prompt = _final_call(xp, gfin).reshape(bp, tp, d)
    y_sample = _final_call(xs, gfin).reshape(bs, SAMPLE_PAD, d)[:, :ts]
    outs_p = [jnp.stack([s[i] for s in st_p]) for i in range(6)]
    outs_s = [jnp.stack([s[i] for s in st_s]) for i in range(6)]
    return (y_prompt, y_sample, *outs_p, *outs_s)
```

```python
import functools
import math

import numpy as np
import jax
import jax.numpy as jnp
from jax import lax
from jax.experimental import pallas as pl
from jax.experimental.pallas import tpu as pltpu
from jax.experimental.pallas import tpu_sc as plsc

F32 = jnp.float32
BF16 = jnp.bfloat16
HIGHEST = lax.Precision.HIGHEST

D_MODEL = 1024
A_HEADS = 8
A_KV_HEADS = 2
A_GROUP = A_HEADS // A_KV_HEADS
A_HEAD_DIM = 64
A_WIDTH = A_HEADS * A_HEAD_DIM
A_KV_WIDTH = A_KV_HEADS * A_HEAD_DIM
WINDOW = 128
ATT_SCALE = A_HEAD_DIM ** -0.5
N_BUCKETS = 32
MAX_DISTANCE = WINDOW
M_HEADS = 4
M_HEAD_DIM = 128
M_WIDTH = M_HEADS * M_HEAD_DIM
CONV_W = 4
M_CHUNK = 64
N_KEYS = 128
P_HEADS = 8
P_TOPK = 16
P_KEY_DIM = 256
P_HALF = P_KEY_DIM // 2
EPS = 1e-6
NEG_INF = -1e30

LANES = 128
SUBLANES = 8
SAMPLE_PAD = SUBLANES
VMEM_LIMIT = 48 * 1024 * 1024

NT_DIMS = (((1,), (1,)), ((), ()))
TN_DIMS = (((0,), (0,)), ((), ()))


def _cparams(*sem):
    return pltpu.CompilerParams(dimension_semantics=sem, vmem_limit_bytes=VMEM_LIMIT)


def _bdot(a, b):
    return jnp.dot(a.astype(BF16), b.astype(BF16), preferred_element_type=F32)


def _bdot_nt(a, b):
    return lax.dot_general(a.astype(BF16), b.astype(BF16), NT_DIMS, preferred_element_type=F32)


def _sigmoid(x):
    return 1.0 / (1.0 + jnp.exp(-x))


def _log_sigmoid(x):
    return jnp.minimum(x, 0.0) - jnp.log1p(jnp.exp(-jnp.abs(x)))


def _gelu_tanh(x):
    c = math.sqrt(2.0 / math.pi)
    return x * (0.5 * (1.0 + jnp.tanh(c * (x + 0.044715 * (x * x * x)))))


def _ada_kernel(c_ref, w_ref, b_ref, o_ref):
    c = c_ref[...]
    s = c * _sigmoid(c)
    o_ref[...] = jnp.dot(s, w_ref[...], precision=HIGHEST, preferred_element_type=F32) + b_ref[...]


def _ada_call(c_all, w_ada, b_ada):
    depth, d, n6 = w_ada.shape
    rows = c_all.shape[0]
    bn = 1024
    return pl.pallas_call(
        _ada_kernel,
        grid=(depth, n6 // bn),
        in_specs=[
            pl.BlockSpec((rows, d), lambda l, j: (0, 0)),
            pl.BlockSpec((None, d, bn), lambda l, j: (l, 0, j)),
            pl.BlockSpec((None, 1, bn), lambda l, j: (l, 0, j)),
        ],
        out_specs=pl.BlockSpec((None, rows, bn), lambda l, j: (l, 0, j)),
        out_shape=jax.ShapeDtypeStruct((depth, rows, n6), F32),
        compiler_params=_cparams("parallel", "parallel"),
        name="ada_mod",
    )(c_all, w_ada, b_ada.reshape(depth, 1, n6))


def _mod_spec(per_token, tm, tokens_per_req):
    if per_token:
        return pl.BlockSpec((tm, D_MODEL), lambda i: (i, 0))
    tiles = tokens_per_req // tm
    return pl.BlockSpec((None, 1, D_MODEL), lambda i: (i // tiles, 0, 0))


def _in_kernel(x_ref, sc_ref, sh_ref, g_ref, wa_ref, wqk_ref, wv_ref, wo_ref, wg_ref, gb_ref,
               qkv_ref, qkm_ref, v_ref, o_ref, gc_ref):
    x = x_ref[...]
    y = x * lax.rsqrt(jnp.mean(x * x, axis=-1, keepdims=True) + EPS) * g_ref[...]
    h = (y * (1.0 + sc_ref[...]) + sh_ref[...]).astype(BF16)
    qkv_ref[...] = jnp.dot(h, wa_ref[...], preferred_element_type=F32)
    qkm_ref[...] = jnp.dot(h, wqk_ref[...], preferred_element_type=F32)
    v_ref[...] = jnp.dot(h, wv_ref[...], preferred_element_type=F32)
    o_ref[...] = jnp.dot(h, wo_ref[...], preferred_element_type=F32)
    g = jnp.dot(h, wg_ref[...], preferred_element_type=F32) + gb_ref[...]
    lane = lax.broadcasted_iota(jnp.int32, g.shape, 1)
    gc_ref[...] = jnp.where(lane < M_HEADS, g, jnp.where(lane < 2 * M_HEADS, _log_sigmoid(g), 0.0))


def _in_call(x, sc, sh, gnorm, wa, wqk, wv, wo, wg, gb, per_token, tokens_per_req):
    n = x.shape[0]
    tm = min(512, n if per_token else tokens_per_req)
    mod = _mod_spec(per_token, tm, tokens_per_req)
    full = lambda shape: pl.BlockSpec(shape, lambda i: (0,) * len(shape))
    row = lambda w: pl.BlockSpec((tm, w), lambda i: (i, 0))
    return pl.pallas_call(
        _in_kernel,
        grid=(n // tm,),
        in_specs=[row(D_MODEL), mod, mod, full((1, D_MODEL)), full(wa.shape), full(wqk.shape),
                  full(wv.shape), full(wo.shape), full(wg.shape), full((1, LANES))],
        out_specs=[row(wa.shape[1]), row(wqk.shape[1]), row(wv.shape[1]), row(wo.shape[1]), row(LANES)],
        out_shape=[jax.ShapeDtypeStruct((n, w), F32)
                   for w in (wa.shape[1], wqk.shape[1], wv.shape[1], wo.shape[1], LANES)],
        compiler_params=_cparams("parallel"),
        name="in_proj",
    )(x, sc, sh, gnorm, wa, wqk, wv, wo, wg, gb)


def _t5_bucket_np(dist):
    n = np.maximum(dist, 0)
    max_exact = N_BUCKETS // 2
    nf = np.maximum(n, 1).astype(np.float64)
    large = max_exact + (np.log(nf / max_exact) / math.log(MAX_DISTANCE / max_exact)
                         * (N_BUCKETS - max_exact)).astype(np.int32)
    return np.where(n < max_exact, n, np.minimum(large, N_BUCKETS - 1)).astype(np.int32)


def _bias_kernel(bucket_ref, rel_ref, o_ref):
    bucket = bucket_ref[...]
    for h in range(A_HEADS):
        acc = jnp.zeros(bucket.shape, F32)
        for b in range(N_BUCKETS):
            acc = jnp.where(bucket == b, rel_ref[b, h], acc)
        o_ref[h] = acc


def _bias_call(rel_bias, dist):
    bucket = jnp.asarray(_t5_bucket_np(dist))
    nq, nk = dist.shape
    return pl.pallas_call(
        _bias_kernel,
        in_specs=[pl.BlockSpec((nq, nk), lambda: (0, 0)),
                  pl.BlockSpec(memory_space=pltpu.SMEM)],
        out_specs=pl.BlockSpec((A_HEADS, nq, nk), lambda: (0, 0, 0)),
        out_shape=jax.ShapeDtypeStruct((A_HEADS, nq, nk), F32),
        name="t5_bias",
    )(bucket, rel_bias)


def _attn_p_kernel(q_ref, kp_ref, kc_ref, vp_ref, vc_ref, bias_ref, sink_ref, o_ref):
    i = pl.program_id(1)
    qi = lax.broadcasted_iota(jnp.int32, (WINDOW, WINDOW), 0)
    kj = lax.broadcasted_iota(jnp.int32, (WINDOW, WINDOW), 1)
    valid_prev = jnp.logical_and(kj > qi, i > 0)
    valid_cur = kj <= qi
    q = q_ref[...]
    for h in range(A_HEADS):
        kv = h // A_GROUP
        qh = q[:, h * A_HEAD_DIM:(h + 1) * A_HEAD_DIM]
        sl = slice(kv * A_HEAD_DIM, (kv + 1) * A_HEAD_DIM)
        bias = bias_ref[h]
        sp = _bdot_nt(qh, kp_ref[:, sl]) * ATT_SCALE + bias[:, :WINDOW]
        sc = _bdot_nt(qh, kc_ref[:, sl]) * ATT_SCALE + bias[:, WINDOW:]
        sp = jnp.where(valid_prev, sp, NEG_INF)
        sc = jnp.where(valid_cur, sc, NEG_INF)
        sink = sink_ref[0, h]
        mx = jnp.maximum(jnp.maximum(jnp.max(sp, axis=-1, keepdims=True),
                                     jnp.max(sc, axis=-1, keepdims=True)), sink)
        ep = jnp.exp(sp - mx)
        ec = jnp.exp(sc - mx)
        den = (jnp.sum(ep, axis=-1, keepdims=True) + jnp.sum(ec, axis=-1, keepdims=True)
               + jnp.exp(sink - mx))
        o = _bdot(ep / den, vp_ref[:, sl]) + _bdot(ec / den, vc_ref[:, sl])
        o_ref[:, h * A_HEAD_DIM:(h + 1) * A_HEAD_DIM] = o


def _attn_p_call(qkv, bias, sinks, batch, seq):
    nb = seq // WINDOW
    n = batch * seq
    kcol = A_WIDTH // A_KV_WIDTH
    vcol = kcol + 1
    cur = lambda col: pl.BlockSpec((WINDOW, A_KV_WIDTH), lambda b, i: (b * nb + i, col))
    prev = lambda col: pl.BlockSpec((WINDOW, A_KV_WIDTH),
                                    lambda b, i: (b * nb + jnp.maximum(i - 1, 0), col))
    return pl.pallas_call(
        _attn_p_kernel,
        grid=(batch, nb),
        in_specs=[pl.BlockSpec((WINDOW, A_WIDTH), lambda b, i: (b * nb + i, 0)),
                  prev(kcol), cur(kcol), prev(vcol), cur(vcol),
                  pl.BlockSpec((A_HEADS, WINDOW, 2 * WINDOW), lambda b, i: (0, 0, 0)),
                  pl.BlockSpec(memory_space=pltpu.SMEM)],
        out_specs=pl.BlockSpec((WINDOW, A_WIDTH), lambda b, i: (b * nb + i, 0)),
        out_shape=jax.ShapeDtypeStruct((n, A_WIDTH), F32),
        compiler_params=_cparams("parallel", "parallel"),
        name="swa_prompt",
    )(qkv, qkv, qkv, qkv, qkv, bias, sinks.reshape(1, A_HEADS))


def _attn_s_kernel(n_new, qkv_ref, ck_ref, cv_ref, bc_ref, bn_ref, sink_ref,
                   o_ref, nk_ref, nv_ref, kk_s, vv_s):
    qkv = qkv_ref[...]
    knew = qkv[:, A_WIDTH:A_WIDTH + A_KV_WIDTH]
    vnew = qkv[:, A_WIDTH + A_KV_WIDTH:A_WIDTH + 2 * A_KV_WIDTH]
    ck = ck_ref[...]
    cv = cv_ref[...]
    qi = lax.broadcasted_iota(jnp.int32, (SAMPLE_PAD, WINDOW), 0)
    kj = lax.broadcasted_iota(jnp.int32, (SAMPLE_PAD, WINDOW), 1)
    valid_c = kj > qi
    qcol = lax.broadcasted_iota(jnp.int32, (SAMPLE_PAD, 1), 0)
    for h in range(A_HEADS):
        kv = h // A_GROUP
        sl = slice(kv * A_HEAD_DIM, (kv + 1) * A_HEAD_DIM)
        qh = qkv[:, h * A_HEAD_DIM:(h + 1) * A_HEAD_DIM]
        s_c = lax.dot_general(qh, ck[:, sl], NT_DIMS, precision=HIGHEST,
                              preferred_element_type=F32) * ATT_SCALE + bc_ref[h]
        s_c = jnp.where(valid_c, s_c, NEG_INF)
        bn = bn_ref[h]
        s_n = []
        for j in range(n_new):
            sj = jnp.sum(qh * knew[j:j + 1, sl], axis=-1, keepdims=True) * ATT_SCALE + bn[:, j:j + 1]
            s_n.append(jnp.where(qcol >= j, sj, NEG_INF))
        sink = sink_ref[0, h]
        mx = jnp.maximum(jnp.max(s_c, axis=-1, keepdims=True), sink)
        for sj in s_n:
            mx = jnp.maximum(mx, sj)
        e_c = jnp.exp(s_c - mx)
        den = jnp.sum(e_c, axis=-1, keepdims=True) + jnp.exp(sink - mx)
        o = jnp.dot(e_c, cv[:, sl], precision=HIGHEST, preferred_element_type=F32)
        for j, sj in enumerate(s_n):
            ej = jnp.exp(sj - mx)
            den = den + ej
            o = o + ej * vnew[j:j + 1, sl]
        o_ref[:, h * A_HEAD_DIM:(h + 1) * A_HEAD_DIM] = o / den
    kk_s[0:WINDOW, :] = ck
    kk_s[WINDOW:WINDOW + SAMPLE_PAD, :] = knew
    vv_s[0:WINDOW, :] = cv
    vv_s[WINDOW:WINDOW + SAMPLE_PAD, :] = vnew
    nk_ref[...] = kk_s[n_new:n_new + WINDOW, :]
    nv_ref[...] = vv_s[n_new:n_new + WINDOW, :]


def _attn_s_call(qkv, ck, cv, bias_c, bias_n, sinks, n_new):
    nreq = ck.shape[0]
    wq = qkv.shape[1]
    full3 = lambda shape: pl.BlockSpec(shape, lambda b: (0, 0, 0))
    cache = pl.BlockSpec((None, WINDOW, A_KV_WIDTH), lambda b: (b, 0, 0))
    return pl.pallas_call(
        functools.partial(_attn_s_kernel, n_new),
        grid=(nreq,),
        in_specs=[pl.BlockSpec((SAMPLE_PAD, wq), lambda b: (b, 0)), cache, cache,
                  full3(bias_c.shape), full3(bias_n.shape),
                  pl.BlockSpec(memory_space=pltpu.SMEM)],
        out_specs=[pl.BlockSpec((SAMPLE_PAD, A_WIDTH), lambda b: (b, 0)), cache, cache],
        out_shape=[jax.ShapeDtypeStruct((nreq * SAMPLE_PAD, A_WIDTH), F32),
                   jax.ShapeDtypeStruct(ck.shape, F32), jax.ShapeDtypeStruct(cv.shape, F32)],
        scratch_shapes=[pltpu.VMEM((WINDOW + SAMPLE_PAD, A_KV_WIDTH), F32),
                        pltpu.VMEM((WINDOW + SAMPLE_PAD, A_KV_WIDTH), F32)],
        compiler_params=_cparams("parallel"),
        name="swa_sample",
    )(qkv, ck, cv, bias_c, bias_n, sinks.reshape(1, A_HEADS))


def _mlstm_kernel(chunk, t_valid, qk_ref, v_ref, og_ref, gc_ref, cw_ref, cb_ref, mn_ref,
                  conv0_ref, c0_ref, n0_ref, m0_ref,
                  out_ref, cout_ref, nout_ref, mout_ref,
                  xp_s, c_s, n_s, m_s):
    step = pl.program_id(1)
    halo = SUBLANES

    @pl.when(step == 0)
    def _():
        xp_s[0:halo, :] = conv0_ref[...]
        c_s[...] = c0_ref[...]
        n_s[...] = n0_ref[...]
        m_s[...] = m0_ref[...]

    xp_s[halo:halo + chunk, :] = qk_ref[...]
    cw = cw_ref[...]
    y = cb_ref[...]
    for i in range(CONV_W):
        off = halo - (CONV_W - 1) + i
        y = y + xp_s[off:off + chunk, :] * cw[i:i + 1, :]
    xp_s[0:halo, :] = xp_s[chunk:chunk + halo, :]
    y = y * _sigmoid(y)
    q_all = y[:, :M_WIDTH]
    k_all = y[:, M_WIDTH:] * (M_HEAD_DIM ** -0.5)

    g = gc_ref[...]
    if t_valid < chunk:
        row = lax.broadcasted_iota(jnp.int32, g.shape, 0)
        lane = lax.broadcasted_iota(jnp.int32, g.shape, 1)
        g = jnp.where(row < t_valid, g, jnp.where(lane < M_HEADS, NEG_INF, 0.0))
    tr = lax.broadcasted_iota(jnp.int32, (chunk, chunk), 0)
    tc = lax.broadcasted_iota(jnp.int32, (chunk, chunk), 1)
    causal = tr >= tc
    tri = causal.astype(F32)
    bcol = jnp.dot(tri, g, precision=HIGHEST, preferred_element_type=F32)
    er = lax.broadcasted_iota(jnp.int32, (SUBLANES, LANES), 0)
    ec = lax.broadcasted_iota(jnp.int32, (SUBLANES, LANES), 1)
    eye = (er == ec).astype(F32)
    g_rows = lax.dot_general(eye, g, NT_DIMS, precision=HIGHEST, preferred_element_type=F32)
    b_rows = lax.dot_general(eye, bcol, NT_DIMS, precision=HIGHEST, preferred_element_type=F32)

    for h in range(M_HEADS):
        hs = slice(h * M_HEAD_DIM, (h + 1) * M_HEAD_DIM)
        b_c = bcol[:, M_HEADS + h:M_HEADS + h + 1]
        ig_c = g[:, h:h + 1]
        b_r = b_rows[M_HEADS + h:M_HEADS + h + 1, :]
        ig_r = g_rows[h:h + 1, :]
        m_prev = m_s[h:h + 1, 0:1]
        logw = jnp.where(causal, b_c - b_r + ig_r, -jnp.inf)
        inter = b_c + m_prev
        m_t = jnp.maximum(inter, jnp.max(logw, axis=-1, keepdims=True))
        w = jnp.exp(logw - m_t)
        a = jnp.exp(inter - m_t)
        q = q_all[:, hs]
        k = k_all[:, hs]
        v = v_ref[:, hs]
        cmat = c_s[h]
        nvec = n_s[h:h + 1, :]
        wqk = w * _bdot_nt(q, k)
        num = _bdot(wqk, v) + a * _bdot_nt(q, cmat)
        den = jnp.sum(wqk, axis=-1, keepdims=True) + a * jnp.sum(q * nvec, axis=-1, keepdims=True)
        hh = num / jnp.maximum(jnp.abs(den), jnp.exp(-m_t))
        m_new = m_t[chunk - 1:chunk, :]
        b_last = b_c[chunk - 1:chunk, :]
        wl = jnp.exp(b_last - b_c + ig_c - m_new)
        al = jnp.exp(b_last + m_prev - m_new)
        c_s[h] = al * cmat + lax.dot_general((v * wl).astype(BF16), k.astype(BF16), TN_DIMS,
                                             preferred_element_type=F32)
        n_s[h:h + 1, :] = al * nvec + jnp.sum(wl * k, axis=0, keepdims=True)
        m_s[h:h + 1, :] = jnp.broadcast_to(m_new, (1, LANES))
        hn = hh * lax.rsqrt(jnp.mean(hh * hh, axis=-1, keepdims=True) + EPS) * mn_ref[:, hs]
        out_ref[:, hs] = _sigmoid(og_ref[:, hs]) * hn

    @pl.when(step == pl.num_programs(1) - 1)
    def _():
        cout_ref[...] = c_s[...]
        nout_ref[...] = n_s[...]
        mout_ref[...] = m_s[...]


def _mlstm_call(qk, v, og, gc, conv_w, conv_b, m_norm, conv0, c0, n0, m0, batch, seq, chunk, t_valid):
    nc = seq // chunk
    n = batch * seq
    row = lambda w: pl.BlockSpec((chunk, w), lambda b, c: (b * nc + c, 0))
    full2 = lambda shape: pl.BlockSpec(shape, lambda b, c: (0, 0))
    per_b = lambda shape: pl.BlockSpec((None,) + shape, lambda b, c: (b,) + (0,) * len(shape))
    dh = M_HEAD_DIM
    return pl.pallas_call(
        functools.partial(_mlstm_kernel, chunk, t_valid),
        grid=(batch, nc),
        in_specs=[row(2 * M_WIDTH), row(M_WIDTH), row(M_WIDTH), row(LANES),
                  full2((CONV_W, 2 * M_WIDTH)), full2((1, 2 * M_WIDTH)), full2((1, M_WIDTH)),
                  per_b((SUBLANES, 2 * M_WIDTH)), per_b((M_HEADS, dh, dh)), per_b((M_HEADS, dh)),
                  per_b((SUBLANES, LANES))],
        out_specs=[row(M_WIDTH), per_b((M_HEADS, dh, dh)), per_b((M_HEADS, dh)), per_b((SUBLANES, LANES))],
        out_shape=[jax.ShapeDtypeStruct((n, M_WIDTH), F32),
                   jax.ShapeDtypeStruct((batch, M_HEADS, dh, dh), F32),
                   jax.ShapeDtypeStruct((batch, M_HEADS, dh), F32),
                   jax.ShapeDtypeStruct((batch, SUBLANES, LANES), F32)],
        scratch_shapes=[pltpu.VMEM((SUBLANES + chunk, 2 * M_WIDTH), F32),
                        pltpu.VMEM((M_HEADS, dh, dh), F32),
                        pltpu.VMEM((M_HEADS, dh), F32),
                        pltpu.VMEM((SUBLANES, LANES), F32)],
        compiler_params=_cparams("parallel", "arbitrary"),
        name="mlstm",
    )(qk, v, og, gc, conv_w, conv_b, m_norm, conv0, c0, n0, m0)


def _out_kernel(att_ref, mo_ref, x_ref, g1_ref, sc_ref, sh_ref, gn_ref, wa_ref, wm_ref, wq_ref,
                xo_ref, h2_ref, qp_ref):
    mix = (jnp.dot(att_ref[...].astype(BF16), wa_ref[...], preferred_element_type=F32)
           + jnp.dot(mo_ref[...].astype(BF16), wm_ref[...], preferred_element_type=F32))
    x = x_ref[...] + g1_ref[...] * mix
    xo_ref[...] = x
    y = x * lax.rsqrt(jnp.mean(x * x, axis=-1, keepdims=True) + EPS) * gn_ref[...]
    h2 = y * (1.0 + sc_ref[...]) + sh_ref[...]
    h2_ref[...] = h2
    qp_ref[...] = jnp.dot(h2.astype(BF16), wq_ref[...], preferred_element_type=F32)


def _out_call(att, mo, x, g1, sc, sh, gnorm, wa, wm, wq, per_token, tokens_per_req):
    n = x.shape[0]
    tm = min(256, n if per_token else tokens_per_req)
    mod = _mod_spec(per_token, tm, tokens_per_req)
    full = lambda shape: pl.BlockSpec(shape, lambda i: (0,) * len(shape))
    row = lambda w: pl.BlockSpec((tm, w), lambda i: (i, 0))
    nq = wq.shape[1]
    return pl.pallas_call(
        _out_kernel,
        grid=(n // tm,),
        in_specs=[row(A_WIDTH), row(M_WIDTH), row(D_MODEL), mod, mod, mod, full((1, D_MODEL)),
                  full(wa.shape), full(wm.shape), full(wq.shape)],
        out_specs=[row(D_MODEL), row(D_MODEL), row(nq)],
        out_shape=[jax.ShapeDtypeStruct((n, D_MODEL), F32), jax.ShapeDtypeStruct((n, D_MODEL), F32),
                   jax.ShapeDtypeStruct((n, nq), F32)],
        compiler_params=_cparams("parallel"),
        name="out_proj",
    )(att, mo, x, g1, sc, sh, gnorm, wa, wm, wq)


def _pk_cells():
    return [(a, b) for a in range(P_TOPK) for b in range(P_TOPK) if (a + 1) * (b + 1) <= P_TOPK]


def _pk_expand_mats():
    cells = _pk_cells()
    e0 = np.zeros((LANES, LANES), np.float32)
    e1 = np.zeros((LANES, LANES), np.float32)
    for j, (a, b) in enumerate(cells):
        e0[a, j] = 1.0
        e1[b, j] = 1.0
    return e0, e1, len(cells)


def _top_rounds(s, lanef, rounds):
    vals = jnp.full(s.shape, -jnp.inf, F32)
    idxs = jnp.zeros(s.shape, F32)
    for r in range(rounds):
        m = jnp.max(s, axis=-1, keepdims=True)
        i = jnp.min(jnp.where(s == m, lanef, float(LANES)), axis=-1, keepdims=True)
        vals = jnp.where(lanef == r, m, vals)
        idxs = jnp.where(lanef == r, i, idxs)
        s = jnp.where(lanef == i, -jnp.inf, s)
    return vals, idxs


def _select_kernel(n_cells, qp_ref, keys_ref, e0_ref, e1_ref, idx_ref, gw_ref):
    tm = qp_ref.shape[0]
    lanef = lax.broadcasted_iota(jnp.int32, (tm, LANES), 1).astype(F32)
    e0 = e0_ref[...]
    e1 = e1_ref[...]

    def head(h, carry):
        idx_all, best_all = carry
        h = jnp.asarray(h, jnp.int32)
        sub = []
        for c in range(2):
            col = pl.multiple_of((h * 2 + c) * P_HALF, P_HALF)
            s = _bdot_nt(qp_ref[:, pl.ds(col, P_HALF)], keys_ref[h, c])
            v, i = _top_rounds(s, lanef, P_TOPK)
            sub.append((jnp.where(lanef < P_TOPK, v, 0.0), i))
        (v0, i0), (v1, i1) = sub
        base = (h * P_TOPK).astype(F32)
        cand = (jnp.dot(v0, e0, precision=HIGHEST, preferred_element_type=F32)
                + jnp.dot(v1, e1, precision=HIGHEST, preferred_element_type=F32))
        cidx = (jnp.dot(i0 * float(N_KEYS), e0, precision=HIGHEST, preferred_element_type=F32)
                + jnp.dot(i1, e1, precision=HIGHEST, preferred_element_type=F32))
        cand = jnp.where(lanef < n_cells, cand, -jnp.inf)
        for r in range(P_TOPK):
            m = jnp.max(cand, axis=-1, keepdims=True)
            j = jnp.min(jnp.where(cand == m, lanef, float(LANES)), axis=-1, keepdims=True)
            hit = lanef == j
            e = jnp.max(jnp.where(hit, cidx, -1.0), axis=-1, keepdims=True)
            slot = lanef == (base + r)
            best_all = jnp.where(slot, m, best_all)
            idx_all = jnp.where(slot, e, idx_all)
            cand = jnp.where(hit, -jnp.inf, cand)
        return idx_all, best_all

    idx_all, best_all = lax.fori_loop(
        0, P_HEADS, head, (jnp.zeros((tm, LANES), F32), jnp.full((tm, LANES), -jnp.inf, F32)))
    gw = jnp.zeros((tm, LANES), F32)
    for h in range(P_HEADS):
        mask = jnp.logical_and(lanef >= h * P_TOPK, lanef < (h + 1) * P_TOPK)
        bh = jnp.where(mask, best_all, -jnp.inf)
        mx = jnp.max(bh, axis=-1, keepdims=True)
        e = jnp.exp(bh - mx)
        gw = gw + e / jnp.sum(e, axis=-1, keepdims=True)
    idx_ref[...] = idx_all.astype(jnp.int32)
    gw_ref[...] = gw


def _select_call(qp, keys_bf16):
    n = qp.shape[0]
    tm = min(128, n)
    e0, e1, n_cells = _pk_expand_mats()
    full = lambda shape: pl.BlockSpec(shape, lambda i: (0,) * len(shape))
    return pl.pallas_call(
        functools.partial(_select_kernel, n_cells),
        grid=(n // tm,),
        in_specs=[pl.BlockSpec((tm, qp.shape[1]), lambda i: (i, 0)), full(keys_bf16.shape),
                  full((LANES, LANES)), full((LANES, LANES))],
        out_specs=[pl.BlockSpec((tm, LANES), lambda i: (i, 0)), pl.BlockSpec((tm, LANES), lambda i: (i, 0))],
        out_shape=[jax.ShapeDtypeStruct((n, LANES), jnp.int32), jax.ShapeDtypeStruct((n, LANES), F32)],
        compiler_params=_cparams("parallel"),
        name="peer_select",
    )(qp, keys_bf16, jnp.asarray(e0), jnp.asarray(e1))


N_SEL = P_HEADS * P_TOPK

SC_CORES = 2
SC_SUBCORES = 16
SC_LANES = 16
SC_TOK_BLOCK = 8
SC_ROWS = 32
SC_ROW_BLOCK = 8
SC_ACC_CHAINS = 4
N_COL = D_MODEL // SC_LANES


def _sc_gelu(x):
    z = math.sqrt(2.0 / math.pi) * (x + 0.044715 * (x * x * x))
    t = 1.0 - 2.0 / (jnp.exp(2.0 * z) + 1.0)
    return x * (0.5 * (1.0 + t))


def _sc_expert_body(tokens_per_worker, idx_hbm, gw_hbm, h2_hbm, u_hbm, v_hbm, y_hbm,
                    idx_v, gw_v, x_v, o_v, buf, coef_v, tr_v, sem):
    wid = lax.axis_index("s") * SC_CORES + lax.axis_index("c")
    base = wid * tokens_per_worker
    lane = lax.iota(jnp.int32, SC_LANES)
    n_gather = N_SEL // SC_ROWS
    zero = jnp.zeros((SC_LANES,), F32)

    def gather(tt, i, slot):
        table = u_hbm if i < n_gather else v_hbm
        j = i % n_gather
        return pltpu.async_copy(table.at[idx_v.at[tt, pl.ds(j * SC_ROWS, SC_ROWS)]], buf.at[slot], sem.at[slot])

    def act_chunk(tt, j, slot):
        @pl.loop(0, SC_ROWS // SC_LANES)
        def _(half):
            for rb in range(SC_LANES // SC_ROW_BLOCK):
                r0 = half * SC_LANES + rb * SC_ROW_BLOCK

                def col(c, accs):
                    off = pl.multiple_of(c * SC_LANES, SC_LANES)
                    xv = x_v[tt, pl.ds(off, SC_LANES)]
                    return tuple(a + buf[slot, r0 + r, pl.ds(off, SC_LANES)] * xv for r, a in enumerate(accs))

                accs = lax.fori_loop(0, N_COL, col, (zero,) * SC_ROW_BLOCK)
                for r in range(SC_ROW_BLOCK):
                    tr_v[pl.ds((rb * SC_ROW_BLOCK + r) * SC_LANES, SC_LANES)] = accs[r]
            tot = zero
            for jj in range(SC_LANES):
                tot = tot + plsc.load_gather(tr_v, [lane * SC_LANES + jj])
            k0 = pl.multiple_of(j * SC_ROWS + half * SC_LANES, SC_LANES)
            coef_v[pl.ds(k0, SC_LANES)] = gw_v[tt, pl.ds(k0, SC_LANES)] * _sc_gelu(tot)

    def acc_chunk(tt, j, slot, first):
        splat = [plsc.load_gather(coef_v, [jnp.full((SC_LANES,), j * SC_ROWS + r, jnp.int32)])
                 for r in range(SC_ROWS)]

        @pl.loop(0, N_COL)
        def _(c):
            off = pl.multiple_of(c * SC_LANES, SC_LANES)
            part = [splat[r] * buf[slot, r, pl.ds(off, SC_LANES)] for r in range(SC_ACC_CHAINS)]
            for r in range(SC_ACC_CHAINS, SC_ROWS):
                part[r % SC_ACC_CHAINS] = part[r % SC_ACC_CHAINS] + splat[r] * buf[slot, r, pl.ds(off, SC_LANES)]
            while len(part) > 1:
                part = [part[i] + part[i + 1] for i in range(0, len(part), 2)]
            acc = part[0] if first else part[0] + o_v[tt, pl.ds(off, SC_LANES)]
            o_v[tt, pl.ds(off, SC_LANES)] = acc

    @pl.loop(0, tokens_per_worker // SC_TOK_BLOCK)
    def _(blk):
        t0 = pl.multiple_of(base + blk * SC_TOK_BLOCK, SC_TOK_BLOCK)
        pltpu.sync_copy(idx_hbm.at[pl.ds(t0, SC_TOK_BLOCK)], idx_v)
        pltpu.sync_copy(gw_hbm.at[pl.ds(t0, SC_TOK_BLOCK)], gw_v)
        pltpu.sync_copy(h2_hbm.at[pl.ds(t0, SC_TOK_BLOCK)], x_v)

        @pl.loop(0, SC_TOK_BLOCK)
        def _(tt):
            cur = gather(tt, 0, 0)
            for i in range(2 * n_gather):
                nxt = gather(tt, i + 1, (i + 1) % 2) if i + 1 < 2 * n_gather else None
                cur.wait()
                if i < n_gather:
                    act_chunk(tt, i, i % 2)
                else:
                    acc_chunk(tt, i - n_gather, i % 2, i == n_gather)
                cur = nxt

        pltpu.sync_copy(o_v, y_hbm.at[pl.ds(t0, SC_TOK_BLOCK)])


def _sc_expert_call(idx, gw, h2, u, v):
    n = idx.shape[0]
    workers = SC_CORES * SC_SUBCORES
    assert n % (workers * SC_TOK_BLOCK) == 0
    mesh = plsc.VectorSubcoreMesh(core_axis_name="c", subcore_axis_name="s")
    return pl.kernel(
        functools.partial(_sc_expert_body, n // workers),
        out_type=jax.ShapeDtypeStruct((n, D_MODEL), F32),
        mesh=mesh,
        scratch_types=[pltpu.VMEM((SC_TOK_BLOCK, N_SEL), jnp.int32),
                       pltpu.VMEM((SC_TOK_BLOCK, N_SEL), F32),
                       pltpu.VMEM((SC_TOK_BLOCK, D_MODEL), F32),
                       pltpu.VMEM((SC_TOK_BLOCK, D_MODEL), F32),
                       pltpu.VMEM((2, SC_ROWS, D_MODEL), F32),
                       pltpu.VMEM((N_SEL,), F32),
                       pltpu.VMEM((SC_LANES * SC_LANES,), F32),
                       pltpu.SemaphoreType.DMA((2,))],
        compiler_params=pltpu.CompilerParams(needs_layout_passes=False),
        name="peer_experts_sc",
    )(idx, gw, h2, u, v)


def _resid_kernel(x_ref, y_ref, g_ref, o_ref):
    o_ref[...] = x_ref[...] + g_ref[...] * y_ref[...]


def _resid_call(x, y, g2, per_token, tokens_per_req):
    n = x.shape[0]
    tm = min(512, n if per_token else tokens_per_req)
    row = pl.BlockSpec((tm, D_MODEL), lambda i: (i, 0))
    return pl.pallas_call(
        _resid_kernel,
        grid=(n // tm,),
        in_specs=[row, row, _mod_spec(per_token, tm, tokens_per_req)],
        out_specs=row,
        out_shape=jax.ShapeDtypeStruct((n, D_MODEL), F32),
        compiler_params=_cparams("parallel"),
        name="peer_residual",
    )(x, y, g2)


def _final_kernel(x_ref, g_ref, o_ref):
    x = x_ref[...]
    o_ref[...] = x * lax.rsqrt(jnp.mean(x * x, axis=-1, keepdims=True) + EPS) * g_ref[...]


def _final_call(x, g):
    n = x.shape[0]
    tm = min(512, n)
    return pl.pallas_call(
        _final_kernel,
        grid=(n // tm,),
        in_specs=[pl.BlockSpec((tm, D_MODEL), lambda i: (i, 0)), pl.BlockSpec((1, D_MODEL), lambda i: (0, 0))],
        out_specs=pl.BlockSpec((tm, D_MODEL), lambda i: (i, 0)),
        out_shape=jax.ShapeDtypeStruct((n, D_MODEL), F32),
        compiler_params=_cparams("parallel"),
        name="final_norm",
    )(x, g)


def _split_w_in(w_in_l, gate_b_l):
    cuts = np.cumsum([A_WIDTH + 2 * A_KV_WIDTH, 2 * M_WIDTH, M_WIDTH, M_WIDTH]).tolist()
    wa = w_in_l[:, :cuts[0]].astype(BF16)
    wqk = w_in_l[:, cuts[0]:cuts[1]].astype(BF16)
    wv = w_in_l[:, cuts[1]:cuts[2]].astype(BF16)
    wo = w_in_l[:, cuts[2]:cuts[3]].astype(BF16)
    ng = 2 * M_HEADS
    wg = jnp.pad(w_in_l[:, cuts[3]:], ((0, 0), (0, LANES - ng))).astype(BF16)
    gb = jnp.pad(gate_b_l.astype(F32), (0, LANES - ng)).reshape(1, LANES)
    return wa, wqk, wv, wo, wg, gb


def _layer(x, mods, per_token, batch, seq, t_valid, lw, bias_p, bias_c, bias_n, kv_cache, conv0, state):
    (norm_mix, norm_ffn, w_in, conv_w, conv_b, gate_b, sinks, m_norm, w_out, peer_query, peer_keys,
     peer_u, peer_v) = lw
    sh1, sc1, g1, sh2, sc2, g2 = mods
    wa, wqk, wv, wo, wg, gb = _split_w_in(w_in, gate_b)
    qkv, qkm, vm, om, gc = _in_call(x, sc1, sh1, norm_mix.reshape(1, -1), wa, wqk, wv, wo, wg, gb,
                                    per_token, seq)
    if kv_cache is None:
        att = _attn_p_call(qkv, bias_p, sinks, batch, seq)
        kv3 = qkv.reshape(batch, seq, -1)
        new_k = kv3[:, seq - WINDOW:, A_WIDTH:A_WIDTH + A_KV_WIDTH]
        new_v = kv3[:, seq - WINDOW:, A_WIDTH + A_KV_WIDTH:]
        chunk = M_CHUNK
    else:
        att, new_k, new_v = _attn_s_call(qkv, kv_cache[0], kv_cache[1], bias_c, bias_n, sinks, t_valid)
        chunk = seq
    c0, n0, m0 = state
    mo, c_new, n_new, m_new = _mlstm_call(qkm, vm, om, gc, conv_w, conv_b.reshape(1, -1),
                                          m_norm.reshape(1, -1), conv0, c0, n0, m0,
                                          batch, seq, chunk, min(t_valid, chunk))
    new_conv = qkm.reshape(batch, seq, -1)[:, t_valid - (CONV_W - 1):t_valid]
    x_mid, h2, qp = _out_call(att, mo, x, g1, sc2, sh2, norm_ffn.reshape(1, -1),
                              w_out[:A_WIDTH].astype(BF16), w_out[A_WIDTH:].astype(BF16),
                              peer_query.astype(BF16), per_token, seq)
    idx, gw = _select_call(qp, peer_keys.astype(BF16))
    y = _sc_expert_call(idx, gw, h2, peer_u, peer_v)
    x_new = _resid_call(x_mid, y, g2, per_token, seq)
    new_k = new_k.reshape(batch, WINDOW, A_KV_HEADS, A_HEAD_DIM)
    new_v = new_v.reshape(batch, WINDOW, A_KV_HEADS, A_HEAD_DIM)
    return x_new, (new_k, new_v, new_conv, c_new, n_new, m_new[:, :M_HEADS, 0])


def kernel(x_prompt, x_sample, c_prompt, c_sample, cache_k, cache_v, state_conv, state_C, state_n, state_m, rel_bias, w_ada, b_ada, norm_mix, norm_ffn, w_in, conv_w, conv_b, gate_b, attn_sinks, m_norm, w_out, peer_query, peer_keys, peer_u, peer_v, norm_final):
    depth = w_ada.shape[0]
    bp, tp, d = x_prompt.shape
    bs, ts, _ = x_sample.shape
    assert tp % WINDOW == 0 and tp % M_CHUNK == 0 and ts <= SAMPLE_PAD and ts >= CONV_W - 1

    mod_all = _ada_call(jnp.concatenate([c_prompt, c_sample], axis=0), w_ada, b_ada)

    qi = np.arange(WINDOW)[:, None]
    bias_p = _bias_call(rel_bias, qi + WINDOW - np.arange(2 * WINDOW)[None, :])
    qs = np.arange(SAMPLE_PAD)[:, None]
    bias_c = _bias_call(rel_bias, qs + WINDOW - np.arange(WINDOW)[None, :])
    bias_n = _bias_call(rel_bias, qs - np.arange(SAMPLE_PAD)[None, :])

    xp = x_prompt.reshape(bp * tp, d)
    xs = jnp.pad(x_sample, ((0, 0), (0, SAMPLE_PAD - ts), (0, 0))).reshape(bs * SAMPLE_PAD, d)
    halo_pad = ((0, 0), (SUBLANES - (CONV_W - 1), 0), (0, 0))
    zero_state = (jnp.zeros((bp, M_HEADS, M_HEAD_DIM, M_HEAD_DIM), F32),
                  jnp.zeros((bp, M_HEADS, M_HEAD_DIM), F32),
                  jnp.zeros((bp, SUBLANES, LANES), F32))
    zero_conv = jnp.zeros((bp, SUBLANES, 2 * M_WIDTH), F32)

    st_p, st_s = [], []
    for l in range(depth):
        lw = (norm_mix[l], norm_ffn[l], w_in[l], conv_w[l], conv_b[l], gate_b[l], attn_sinks[l], m_norm[l],
              w_out[l], peer_query[l], peer_keys[l], peer_u[l], peer_v[l])
        mod_p = [m.reshape(bp, 1, d) for m in jnp.split(mod_all[l, :bp], 6, axis=-1)]
        mod_s = [jnp.repeat(m, SAMPLE_PAD, axis=0) for m in jnp.split(mod_all[l, bp:], 6, axis=-1)]
        xp, sp = _layer(xp, mod_p, False, bp, tp, tp, lw, bias_p, None, None, None, zero_conv, zero_state)
        state_s = (state_C[l].astype(F32), state_n[l].astype(F32),
                   jnp.broadcast_to(jnp.pad(state_m[l].astype(F32), ((0, 0), (0, SUBLANES - M_HEADS)))[:, :, None],
                                    (bs, SUBLANES, LANES)))
        kv_cache = (cache_k[l].reshape(bs, WINDOW, A_KV_WIDTH), cache_v[l].reshape(bs, WINDOW, A_KV_WIDTH))
        xs, ss = _layer(xs, mod_s, True, bs, SAMPLE_PAD, ts, lw, None, bias_c, bias_n, kv_cache,
                        jnp.pad(state_conv[l].astype(F32), halo_pad), state_s)
        st_p.append(sp)
        st_s.append(ss)

    gfin = norm_final.reshape(1, d)
    y_prompt = _final_call(xp, gfin).reshape(bp, tp, d)
    y_sample = _final_call(xs, gfin).reshape(bs, SAMPLE_PAD, d)[:, :ts]
    outs_p = [jnp.stack([s[i] for s in st_p]) for i in range(6)]
    outs_s = [jnp.stack([s[i] for s in st_s]) for i in range(6)]
    return (y_prompt, y_sample, *outs_p, *outs_s)
```

```python
import functools
import math

import numpy as np
import jax
import jax.numpy as jnp
from jax import lax
from jax.experimental import pallas as pl
from jax.experimental.pallas import tpu as pltpu
from jax.experimental.pallas import tpu_sc as plsc

F32 = jnp.float32
BF16 = jnp.bfloat16
HIGHEST = lax.Precision.HIGHEST

D_MODEL = 1024
A_HEADS = 8
A_KV_HEADS = 2
A_GROUP = A_HEADS // A_KV_HEADS
A_HEAD_DIM = 64
A_WIDTH = A_HEADS * A_HEAD_DIM
A_KV_WIDTH = A_KV_HEADS * A_HEAD_DIM
WINDOW = 128
ATT_SCALE = A_HEAD_DIM ** -0.5
N_BUCKETS = 32
MAX_DISTANCE = WINDOW
M_HEADS = 4
M_HEAD_DIM = 128
M_WIDTH = M_HEADS * M_HEAD_DIM
CONV_W = 4
M_CHUNK = 64
N_KEYS = 128
P_HEADS = 8
P_TOPK = 16
P_KEY_DIM = 256
P_HALF = P_KEY_DIM // 2
EPS = 1e-6
NEG_INF = -1e30

LANES = 128
SUBLANES = 8
SAMPLE_PAD = SUBLANES
VMEM_LIMIT = 48 * 1024 * 1024

NT_DIMS = (((1,), (1,)), ((), ()))
TN_DIMS = (((0,), (0,)), ((), ()))


def _cparams(*sem):
    return pltpu.CompilerParams(dimension_semantics=sem, vmem_limit_bytes=VMEM_LIMIT)


def _bdot(a, b):
    return jnp.dot(a.astype(BF16), b.astype(BF16), preferred_element_type=F32)


def _bdot_nt(a, b):
    return lax.dot_general(a.astype(BF16), b.astype(BF16), NT_DIMS, preferred_element_type=F32)


def _sigmoid(x):
    return 1.0 / (1.0 + jnp.exp(-x))


def _log_sigmoid(x):
    return jnp.minimum(x, 0.0) - jnp.log1p(jnp.exp(-jnp.abs(x)))


def _gelu_tanh(x):
    c = math.sqrt(2.0 / math.pi)
    return x * (0.5 * (1.0 + jnp.tanh(c * (x + 0.044715 * (x * x * x)))))


def _ada_kernel(c_ref, w_ref, b_ref, o_ref):
    c = c_ref[...]
    s = c * _sigmoid(c)
    o_ref[...] = jnp.dot(s, w_ref[...], precision=HIGHEST, preferred_element_type=F32) + b_ref[...]


def _ada_call(c_all, w_ada, b_ada):
    depth, d, n6 = w_ada.shape
    rows = c_all.shape[0]
    bn = 1024
    return pl.pallas_call(
        _ada_kernel,
        grid=(depth, n6 // bn),
        in_specs=[
            pl.BlockSpec((rows, d), lambda l, j: (0, 0)),
            pl.BlockSpec((None, d, bn), lambda l, j: (l, 0, j)),
            pl.BlockSpec((None, 1, bn), lambda l, j: (l, 0, j)),
        ],
        out_specs=pl.BlockSpec((None, rows, bn), lambda l, j: (l, 0, j)),
        out_shape=jax.ShapeDtypeStruct((depth, rows, n6), F32),
        compiler_params=_cparams("parallel", "parallel"),
        name="ada_mod",
    )(c_all, w_ada, b_ada.reshape(depth, 1, n6))


def _mod_spec(per_token, tm, tokens_per_req):
    if per_token:
        return pl.BlockSpec((tm, D_MODEL), lambda i: (i, 0))
    tiles = tokens_per_req // tm
    return pl.BlockSpec((None, 1, D_MODEL), lambda i: (i // tiles, 0, 0))


def _in_kernel(x_ref, sc_ref, sh_ref, g_ref, wa_ref, wqk_ref, wv_ref, wo_ref, wg_ref, gb_ref,
               qkv_ref, qkm_ref, v_ref, o_ref, gc_ref):
    x = x_ref[...]
    y = x * lax.rsqrt(jnp.mean(x * x, axis=-1, keepdims=True) + EPS) * g_ref[...]
    h = (y * (1.0 + sc_ref[...]) + sh_ref[...]).astype(BF16)
    qkv_ref[...] = jnp.dot(h, wa_ref[...], preferred_element_type=F32)
    qkm_ref[...] = jnp.dot(h, wqk_ref[...], preferred_element_type=F32)
    v_ref[...] = jnp.dot(h, wv_ref[...], preferred_element_type=F32)
    o_ref[...] = jnp.dot(h, wo_ref[...], preferred_element_type=F32)
    g = jnp.dot(h, wg_ref[...], preferred_element_type=F32) + gb_ref[...]
    lane = lax.broadcasted_iota(jnp.int32, g.shape, 1)
    gc_ref[...] = jnp.where(lane < M_HEADS, g, jnp.where(lane < 2 * M_HEADS, _log_sigmoid(g), 0.0))


def _in_call(x, sc, sh, gnorm, wa, wqk, wv, wo, wg, gb, per_token, tokens_per_req):
    n = x.shape[0]
    tm = min(512, n if per_token else tokens_per_req)
    mod = _mod_spec(per_token, tm, tokens_per_req)
    full = lambda shape: pl.BlockSpec(shape, lambda i: (0,) * len(shape))
    row = lambda w: pl.BlockSpec((tm, w), lambda i: (i, 0))
    return pl.pallas_call(
        _in_kernel,
        grid=(n // tm,),
        in_specs=[row(D_MODEL), mod, mod, full((1, D_MODEL)), full(wa.shape), full(wqk.shape),
                  full(wv.shape), full(wo.shape), full(wg.shape), full((1, LANES))],
        out_specs=[row(wa.shape[1]), row(wqk.shape[1]), row(wv.shape[1]), row(wo.shape[1]), row(LANES)],
        out_shape=[jax.ShapeDtypeStruct((n, w), F32)
                   for w in (wa.shape[1], wqk.shape[1], wv.shape[1], wo.shape[1], LANES)],
        compiler_params=_cparams("parallel"),
        name="in_proj",
    )(x, sc, sh, gnorm, wa, wqk, wv, wo, wg, gb)


def _t5_bucket_np(dist):
    n = np.maximum(dist, 0)
    max_exact = N_BUCKETS // 2
    nf = np.maximum(n, 1).astype(np.float64)
    large = max_exact + (np.log(nf / max_exact) / math.log(MAX_DISTANCE / max_exact)
                         * (N_BUCKETS - max_exact)).astype(np.int32)
    return np.where(n < max_exact, n, np.minimum(large, N_BUCKETS - 1)).astype(np.int32)


def _bias_kernel(bucket_ref, rel_ref, o_ref):
    bucket = bucket_ref[...]
    for h in range(A_HEADS):
        acc = jnp.zeros(bucket.shape, F32)
        for b in range(N_BUCKETS):
            acc = jnp.where(bucket == b, rel_ref[b, h], acc)
        o_ref[h] = acc


def _bias_call(rel_bias, dist):
    bucket = jnp.asarray(_t5_bucket_np(dist))
    nq, nk = dist.shape
    return pl.pallas_call(
        _bias_kernel,
        in_specs=[pl.BlockSpec((nq, nk), lambda: (0, 0)),
                  pl.BlockSpec(memory_space=pltpu.SMEM)],
        out_specs=pl.BlockSpec((A_HEADS, nq, nk), lambda: (0, 0, 0)),
        out_shape=jax.ShapeDtypeStruct((A_HEADS, nq, nk), F32),
        name="t5_bias",
    )(bucket, rel_bias)


def _attn_p_kernel(q_ref, kp_ref, kc_ref, vp_ref, vc_ref, bias_ref, sink_ref, o_ref):
    i = pl.program_id(1)
    qi = lax.broadcasted_iota(jnp.int32, (WINDOW, WINDOW), 0)
    kj = lax.broadcasted_iota(jnp.int32, (WINDOW, WINDOW), 1)
    valid_prev = jnp.logical_and(kj > qi, i > 0)
    valid_cur = kj <= qi
    q = q_ref[...]
    for h in range(A_HEADS):
        kv = h // A_GROUP
        qh = q[:, h * A_HEAD_DIM:(h + 1) * A_HEAD_DIM]
        sl = slice(kv * A_HEAD_DIM, (kv + 1) * A_HEAD_DIM)
        bias = bias_ref[h]
        sp = _bdot_nt(qh, kp_ref[:, sl]) * ATT_SCALE + bias[:, :WINDOW]
        sc = _bdot_nt(qh, kc_ref[:, sl]) * ATT_SCALE + bias[:, WINDOW:]
        sp = jnp.where(valid_prev, sp, NEG_INF)
        sc = jnp.where(valid_cur, sc, NEG_INF)
        sink = sink_ref[0, h]
        mx = jnp.maximum(jnp.maximum(jnp.max(sp, axis=-1, keepdims=True),
                                     jnp.max(sc, axis=-1, keepdims=True)), sink)
        ep = jnp.exp(sp - mx)
        ec = jnp.exp(sc - mx)
        den = (jnp.sum(ep, axis=-1, keepdims=True) + jnp.sum(ec, axis=-1, keepdims=True)
               + jnp.exp(sink - mx))
        o = _bdot(ep / den, vp_ref[:, sl]) + _bdot(ec / den, vc_ref[:, sl])
        o_ref[:, h * A_HEAD_DIM:(h + 1) * A_HEAD_DIM] = o


def _attn_p_call(qkv, bias, sinks, batch, seq):
    nb = seq // WINDOW
    n = batch * seq
    kcol = A_WIDTH // A_KV_WIDTH
    vcol = kcol + 1
    cur = lambda col: pl.BlockSpec((WINDOW, A_KV_WIDTH), lambda b, i: (b * nb + i, col))
    prev = lambda col: pl.BlockSpec((WINDOW, A_KV_WIDTH),
                                    lambda b, i: (b * nb + jnp.maximum(i - 1, 0), col))
    return pl.pallas_call(
        _attn_p_kernel,
        grid=(batch, nb),
        in_specs=[pl.BlockSpec((WINDOW, A_WIDTH), lambda b, i: (b * nb + i, 0)),
                  prev(kcol), cur(kcol), prev(vcol), cur(vcol),
                  pl.BlockSpec((A_HEADS, WINDOW, 2 * WINDOW), lambda b, i: (0, 0, 0)),
                  pl.BlockSpec(memory_space=pltpu.SMEM)],
        out_specs=pl.BlockSpec((WINDOW, A_WIDTH), lambda b, i: (b * nb + i, 0)),
        out_shape=jax.ShapeDtypeStruct((n, A_WIDTH), F32),
        compiler_params=_cparams("parallel", "parallel"),
        name="swa_prompt",
    )(qkv, qkv, qkv, qkv, qkv, bias, sinks.reshape(1, A_HEADS))


def _attn_s_kernel(n_new, qkv_ref, ck_ref, cv_ref, bc_ref, bn_ref, sink_ref,
                   o_ref, nk_ref, nv_ref, kk_s, vv_s):
    qkv = qkv_ref[...]
    knew = qkv[:, A_WIDTH:A_WIDTH + A_KV_WIDTH]
    vnew = qkv[:, A_WIDTH + A_KV_WIDTH:A_WIDTH + 2 * A_KV_WIDTH]
    ck = ck_ref[...]
    cv = cv_ref[...]
    qi = lax.broadcasted_iota(jnp.int32, (SAMPLE_PAD, WINDOW), 0)
    kj = lax.broadcasted_iota(jnp.int32, (SAMPLE_PAD, WINDOW), 1)
    valid_c = kj > qi
    qcol = lax.broadcasted_iota(jnp.int32, (SAMPLE_PAD, 1), 0)
    for h in range(A_HEADS):
        kv = h // A_GROUP
        sl = slice(kv * A_HEAD_DIM, (kv + 1) * A_HEAD_DIM)
        qh = qkv[:, h * A_HEAD_DIM:(h + 1) * A_HEAD_DIM]
        s_c = lax.dot_general(qh, ck[:, sl], NT_DIMS, precision=HIGHEST,
                              preferred_element_type=F32) * ATT_SCALE + bc_ref[h]
        s_c = jnp.where(valid_c, s_c, NEG_INF)
        bn = bn_ref[h]
        s_n = []
        for j in range(n_new):
            sj = jnp.sum(qh * knew[j:j + 1, sl], axis=-1, keepdims=True) * ATT_SCALE + bn[:, j:j + 1]
            s_n.append(jnp.where(qcol >= j, sj, NEG_INF))
        sink = sink_ref[0, h]
        mx = jnp.maximum(jnp.max(s_c, axis=-1, keepdims=True), sink)
        for sj in s_n:
            mx = jnp.maximum(mx, sj)
        e_c = jnp.exp(s_c - mx)
        den = jnp.sum(e_c, axis=-1, keepdims=True) + jnp.exp(sink - mx)
        o = jnp.dot(e_c, cv[:, sl], precision=HIGHEST, preferred_element_type=F32)
        for j, sj in enumerate(s_n):
            ej = jnp.exp(sj - mx)
            den = den + ej
            o = o + ej * vnew[j:j + 1, sl]
        o_ref[:, h * A_HEAD_DIM:(h + 1) * A_HEAD_DIM] = o / den
    kk_s[0:WINDOW, :] = ck
    kk_s[WINDOW:WINDOW + SAMPLE_PAD, :] = knew
    vv_s[0:WINDOW, :] = cv
    vv_s[WINDOW:WINDOW + SAMPLE_PAD, :] = vnew
    nk_ref[...] = kk_s[n_new:n_new + WINDOW, :]
    nv_ref[...] = vv_s[n_new:n_new + WINDOW, :]


def _attn_s_call(qkv, ck, cv, bias_c, bias_n, sinks, n_new):
    nreq = ck.shape[0]
    wq = qkv.shape[1]
    full3 = lambda shape: pl.BlockSpec(shape, lambda b: (0, 0, 0))
    cache = pl.BlockSpec((None, WINDOW, A_KV_WIDTH), lambda b: (b, 0, 0))
    return pl.pallas_call(
        functools.partial(_attn_s_kernel, n_new),
        grid=(nreq,),
        in_specs=[pl.BlockSpec((SAMPLE_PAD, wq), lambda b: (b, 0)), cache, cache,
                  full3(bias_c.shape), full3(bias_n.shape),
                  pl.BlockSpec(memory_space=pltpu.SMEM)],
        out_specs=[pl.BlockSpec((SAMPLE_PAD, A_WIDTH), lambda b: (b, 0)), cache, cache],
        out_shape=[jax.ShapeDtypeStruct((nreq * SAMPLE_PAD, A_WIDTH), F32),
                   jax.ShapeDtypeStruct(ck.shape, F32), jax.ShapeDtypeStruct(cv.shape, F32)],
        scratch_shapes=[pltpu.VMEM((WINDOW + SAMPLE_PAD, A_KV_WIDTH), F32),
                        pltpu.VMEM((WINDOW + SAMPLE_PAD, A_KV_WIDTH), F32)],
        compiler_params=_cparams("parallel"),
        name="swa_sample",
    )(qkv, ck, cv, bias_c, bias_n, sinks.reshape(1, A_HEADS))


def _mlstm_kernel(chunk, t_valid, qk_ref, v_ref, og_ref, gc_ref, cw_ref, cb_ref, mn_ref,
                  conv0_ref, c0_ref, n0_ref, m0_ref,
                  out_ref, cout_ref, nout_ref, mout_ref,
                  xp_s, c_s, n_s, m_s):
    step = pl.program_id(1)
    halo = SUBLANES

    @pl.when(step == 0)
    def _():
        xp_s[0:halo, :] = conv0_ref[...]
        c_s[...] = c0_ref[...]
        n_s[...] = n0_ref[...]
        m_s[...] = m0_ref[...]

    xp_s[halo:halo + chunk, :] = qk_ref[...]
    cw = cw_ref[...]
    y = cb_ref[...]
    for i in range(CONV_W):
        off = halo - (CONV_W - 1) + i
        y = y + xp_s[off:off + chunk, :] * cw[i:i + 1, :]
    xp_s[0:halo, :] = xp_s[chunk:chunk + halo, :]
    y = y * _sigmoid(y)
    q_all = y[:, :M_WIDTH]
    k_all = y[:, M_WIDTH:] * (M_HEAD_DIM ** -0.5)

    g = gc_ref[...]
    if t_valid < chunk:
        row = lax.broadcasted_iota(jnp.int32, g.shape, 0)
        lane = lax.broadcasted_iota(jnp.int32, g.shape, 1)
        g = jnp.where(row < t_valid, g, jnp.where(lane < M_HEADS, NEG_INF, 0.0))
    tr = lax.broadcasted_iota(jnp.int32, (chunk, chunk), 0)
    tc = lax.broadcasted_iota(jnp.int32, (chunk, chunk), 1)
    causal = tr >= tc
    tri = causal.astype(F32)
    bcol = jnp.dot(tri, g, precision=HIGHEST, preferred_element_type=F32)
    er = lax.broadcasted_iota(jnp.int32, (SUBLANES, LANES), 0)
    ec = lax.broadcasted_iota(jnp.int32, (SUBLANES, LANES), 1)
    eye = (er == ec).astype(F32)
    g_rows = lax.dot_general(eye, g, NT_DIMS, precision=HIGHEST, preferred_element_type=F32)
    b_rows = lax.dot_general(eye, bcol, NT_DIMS, precision=HIGHEST, preferred_element_type=F32)

    for h in range(M_HEADS):
        hs = slice(h * M_HEAD_DIM, (h + 1) * M_HEAD_DIM)
        b_c = bcol[:, M_HEADS + h:M_HEADS + h + 1]
        ig_c = g[:, h:h + 1]
        b_r = b_rows[M_HEADS + h:M_HEADS + h + 1, :]
        ig_r = g_rows[h:h + 1, :]
        m_prev = m_s[h:h + 1, 0:1]
        logw = jnp.where(causal, b_c - b_r + ig_r, -jnp.inf)
        inter = b_c + m_prev
        m_t = jnp.maximum(inter, jnp.max(logw, axis=-1, keepdims=True))
        w = jnp.exp(logw - m_t)
        a = jnp.exp(inter - m_t)
        q = q_all[:, hs]
        k = k_all[:, hs]
        v = v_ref[:, hs]
        cmat = c_s[h]
        nvec = n_s[h:h + 1, :]
        wqk = w * _bdot_nt(q, k)
        num = _bdot(wqk, v) + a * _bdot_nt(q, cmat)
        den = jnp.sum(wqk, axis=-1, keepdims=True) + a * jnp.sum(q * nvec, axis=-1, keepdims=True)
        hh = num / jnp.maximum(jnp.abs(den), jnp.exp(-m_t))
        m_new = m_t[chunk - 1:chunk, :]
        b_last = b_c[chunk - 1:chunk, :]
        wl = jnp.exp(b_last - b_c + ig_c - m_new)
        al = jnp.exp(b_last + m_prev - m_new)
        c_s[h] = al * cmat + lax.dot_general((v * wl).astype(BF16), k.astype(BF16), TN_DIMS,
                                             preferred_element_type=F32)
        n_s[h:h + 1, :] = al * nvec + jnp.sum(wl * k, axis=0, keepdims=True)
        m_s[h:h + 1, :] = jnp.broadcast_to(m_new, (1, LANES))
        hn = hh * lax.rsqrt(jnp.mean(hh * hh, axis=-1, keepdims=True) + EPS) * mn_ref[:, hs]
        out_ref[:, hs] = _sigmoid(og_ref[:, hs]) * hn

    @pl.when(step == pl.num_programs(1) - 1)
    def _():
        cout_ref[...] = c_s[...]
        nout_ref[...] = n_s[...]
        mout_ref[...] = m_s[...]


def _mlstm_call(qk, v, og, gc, conv_w, conv_b, m_norm, conv0, c0, n0, m0, batch, seq, chunk, t_valid):
    nc = seq // chunk
    n = batch * seq
    row = lambda w: pl.BlockSpec((chunk, w), lambda b, c: (b * nc + c, 0))
    full2 = lambda shape: pl.BlockSpec(shape, lambda b, c: (0, 0))
    per_b = lambda shape: pl.BlockSpec((None,) + shape, lambda b, c: (b,) + (0,) * len(shape))
    dh = M_HEAD_DIM
    return pl.pallas_call(
        functools.partial(_mlstm_kernel, chunk, t_valid),
        grid=(batch, nc),
        in_specs=[row(2 * M_WIDTH), row(M_WIDTH), row(M_WIDTH), row(LANES),
                  full2((CONV_W, 2 * M_WIDTH)), full2((1, 2 * M_WIDTH)), full2((1, M_WIDTH)),
                  per_b((SUBLANES, 2 * M_WIDTH)), per_b((M_HEADS, dh, dh)), per_b((M_HEADS, dh)),
                  per_b((SUBLANES, LANES))],
        out_specs=[row(M_WIDTH), per_b((M_HEADS, dh, dh)), per_b((M_HEADS, dh)), per_b((SUBLANES, LANES))],
        out_shape=[jax.ShapeDtypeStruct((n, M_WIDTH), F32),
                   jax.ShapeDtypeStruct((batch, M_HEADS, dh, dh), F32),
                   jax.ShapeDtypeStruct((batch, M_HEADS, dh), F32),
                   jax.ShapeDtypeStruct((batch, SUBLANES, LANES), F32)],
        scratch_shapes=[pltpu.VMEM((SUBLANES + chunk, 2 * M_WIDTH), F32),
                        pltpu.VMEM((M_HEADS, dh, dh), F32),
                        pltpu.VMEM((M_HEADS, dh), F32),
                        pltpu.VMEM((SUBLANES, LANES), F32)],
        compiler_params=_cparams("parallel", "arbitrary"),
        name="mlstm",
    )(qk, v, og, gc, conv_w, conv_b, m_norm, conv0, c0, n0, m0)


def _out_kernel(att_ref, mo_ref, x_ref, g1_ref, sc_ref, sh_ref, gn_ref, wa_ref, wm_ref, wq_ref,
                xo_ref, h2_ref, qp_ref):
    mix = (jnp.dot(att_ref[...].astype(BF16), wa_ref[...], preferred_element_type=F32)
           + jnp.dot(mo_ref[...].astype(BF16), wm_ref[...], preferred_element_type=F32))
    x = x_ref[...] + g1_ref[...] * mix
    xo_ref[...] = x
    y = x * lax.rsqrt(jnp.mean(x * x, axis=-1, keepdims=True) + EPS) * gn_ref[...]
    h2 = y * (1.0 + sc_ref[...]) + sh_ref[...]
    h2_ref[...] = h2
    qp_ref[...] = jnp.dot(h2.astype(BF16), wq_ref[...], preferred_element_type=F32)


def _out_call(att, mo, x, g1, sc, sh, gnorm, wa, wm, wq, per_token, tokens_per_req):
    n = x.shape[0]
    tm = min(256, n if per_token else tokens_per_req)
    mod = _mod_spec(per_token, tm, tokens_per_req)
    full = lambda shape: pl.BlockSpec(shape, lambda i: (0,) * len(shape))
    row = lambda w: pl.BlockSpec((tm, w), lambda i: (i, 0))
    nq = wq.shape[1]
    return pl.pallas_call(
        _out_kernel,
        grid=(n // tm,),
        in_specs=[row(A_WIDTH), row(M_WIDTH), row(D_MODEL), mod, mod, mod, full((1, D_MODEL)),
                  full(wa.shape), full(wm.shape), full(wq.shape)],
        out_specs=[row(D_MODEL), row(D_MODEL), row(nq)],
        out_shape=[jax.ShapeDtypeStruct((n, D_MODEL), F32), jax.ShapeDtypeStruct((n, D_MODEL), F32),
                   jax.ShapeDtypeStruct((n, nq), F32)],
        compiler_params=_cparams("parallel"),
        name="out_proj",
    )(att, mo, x, g1, sc, sh, gnorm, wa, wm, wq)


def _pk_cells():
    return [(a, b) for a in range(P_TOPK) for b in range(P_TOPK) if (a + 1) * (b + 1) <= P_TOPK]


PK_CELL_ROWS = 64


def _pk_expand_mats():
    cells = _pk_cells()
    e0 = np.zeros((PK_CELL_ROWS, LANES), np.float32)
    e1 = np.zeros((PK_CELL_ROWS, LANES), np.float32)
    for j, (a, b) in enumerate(cells):
        e0[j, a] = 1.0
        e1[j, b] = 1.0
    return e0, e1, len(cells)


def _top_rows(s, rowf, rounds):
    n_rows = s.shape[0]
    vals, idxs = [], []
    for _ in range(rounds):
        m = jnp.max(s, axis=0, keepdims=True)
        i = jnp.min(jnp.where(s == m, rowf, float(n_rows)), axis=0, keepdims=True)
        vals.append(m)
        idxs.append(i)
        s = jnp.where(rowf == i, -jnp.inf, s)
    return jnp.concatenate(vals, axis=0), jnp.concatenate(idxs, axis=0)


def _select_kernel(n_cells, qp_ref, keys_ref, e0_ref, e1_ref, idx_ref, gw_ref, idx_s, gw_s):
    tm = qp_ref.shape[0]
    keyf = lax.broadcasted_iota(jnp.int32, (N_KEYS, tm), 0).astype(F32)
    cellf = lax.broadcasted_iota(jnp.int32, (PK_CELL_ROWS, tm), 0).astype(F32)
    e0 = e0_ref[...]
    e1 = e1_ref[...]
    pad = jnp.zeros((LANES - P_TOPK, tm), F32)

    def head(h, carry):
        h = jnp.asarray(h, jnp.int32)
        sub = []
        for c in range(2):
            col = pl.multiple_of((h * 2 + c) * P_HALF, P_HALF)
            s = _bdot_nt(keys_ref[h, c], qp_ref[:, pl.ds(col, P_HALF)])
            sub.append(_top_rows(s, keyf, P_TOPK))
        (v0, i0), (v1, i1) = sub
        expand = lambda e, x: jnp.dot(e, jnp.concatenate([x, pad], axis=0), precision=HIGHEST,
                                      preferred_element_type=F32)
        cand = expand(e0, v0) + expand(e1, v1)
        cidx = expand(e0, i0 * float(N_KEYS)) + expand(e1, i1)
        cand = jnp.where(cellf < n_cells, cand, -jnp.inf)
        best, eidx = [], []
        for _ in range(P_TOPK):
            m = jnp.max(cand, axis=0, keepdims=True)
            j = jnp.min(jnp.where(cand == m, cellf, float(PK_CELL_ROWS)), axis=0, keepdims=True)
            hit = cellf == j
            eidx.append(jnp.max(jnp.where(hit, cidx, -1.0), axis=0, keepdims=True))
            best.append(m)
            cand = jnp.where(hit, -jnp.inf, cand)
        best = jnp.concatenate(best, axis=0)
        e = jnp.exp(best - best[0:1, :])
        row0 = pl.multiple_of(h * P_TOPK, P_TOPK)
        gw_s[pl.ds(row0, P_TOPK), :] = e / jnp.sum(e, axis=0, keepdims=True)
        idx_s[pl.ds(row0, P_TOPK), :] = jnp.concatenate(eidx, axis=0)
        return carry

    lax.fori_loop(0, P_HEADS, head, 0)
    idx_ref[...] = idx_s[...].T.astype(jnp.int32)
    gw_ref[...] = gw_s[...].T


def _select_call(qp, keys_bf16):
    n = qp.shape[0]
    tm = min(LANES, n)
    e0, e1, n_cells = _pk_expand_mats()
    full = lambda shape: pl.BlockSpec(shape, lambda i: (0,) * len(shape))
    return pl.pallas_call(
        functools.partial(_select_kernel, n_cells),
        grid=(n // tm,),
        in_specs=[pl.BlockSpec((tm, qp.shape[1]), lambda i: (i, 0)), full(keys_bf16.shape),
                  full((PK_CELL_ROWS, LANES)), full((PK_CELL_ROWS, LANES))],
        out_specs=[pl.BlockSpec((tm, LANES), lambda i: (i, 0)), pl.BlockSpec((tm, LANES), lambda i: (i, 0))],
        out_shape=[jax.ShapeDtypeStruct((n, LANES), jnp.int32), jax.ShapeDtypeStruct((n, LANES), F32)],
        scratch_shapes=[pltpu.VMEM((P_HEADS * P_TOPK, tm), F32), pltpu.VMEM((P_HEADS * P_TOPK, tm), F32)],
        compiler_params=_cparams("parallel"),
        name="peer_select",
    )(qp, keys_bf16, jnp.asarray(e0), jnp.asarray(e1))


N_SEL = P_HEADS * P_TOPK

SC_CORES = 2
SC_SUBCORES = 16
SC_LANES = 16
SC_TOK_BLOCK = 8
SC_ROWS = 32
SC_ROW_BLOCK = 16
SC_ACC_CHAINS = 4
N_COL = D_MODEL // SC_LANES


def _sc_gelu(x):
    z = math.sqrt(2.0 / math.pi) * (x + 0.044715 * (x * x * x))
    t = 1.0 - 2.0 / (jnp.exp(2.0 * z) + 1.0)
    return x * (0.5 * (1.0 + t))


def _sc_expert_body(tokens_per_worker, idx_hbm, gw_hbm, h2_hbm, u_hbm, v_hbm, y_hbm,
                    idx_v, gw_v, x_v, o_v, buf, coef_v, tr_v, sem):
    wid = lax.axis_index("s") * SC_CORES + lax.axis_index("c")
    base = wid * tokens_per_worker
    lane = lax.iota(jnp.int32, SC_LANES)
    n_gather = N_SEL // SC_ROWS
    zero = jnp.zeros((SC_LANES,), F32)

    def gather(tt, i):
        table = u_hbm if i < n_gather else v_hbm
        j = i % n_gather
        slot = i % 2
        return pltpu.make_async_copy(table.at[idx_v.at[tt, pl.ds(j * SC_ROWS, SC_ROWS)]], buf.at[slot],
                                     sem.at[slot])

    def act_chunk(tt, j, slot):
        @pl.loop(0, SC_ROWS // SC_LANES)
        def _(half):
            for rb in range(SC_LANES // SC_ROW_BLOCK):
                r0 = half * SC_LANES + rb * SC_ROW_BLOCK

                def col(c, accs):
                    off = pl.multiple_of(c * SC_LANES, SC_LANES)
                    xv = x_v[tt, pl.ds(off, SC_LANES)]
                    return tuple(a + buf[slot, r0 + r, pl.ds(off, SC_LANES)] * xv for r, a in enumerate(accs))

                accs = lax.fori_loop(0, N_COL, col, (zero,) * SC_ROW_BLOCK)
                for r in range(SC_ROW_BLOCK):
                    tr_v[pl.ds((rb * SC_ROW_BLOCK + r) * SC_LANES, SC_LANES)] = accs[r]
            tot = zero
            for jj in range(SC_LANES):
                tot = tot + plsc.load_gather(tr_v, [lane * SC_LANES + jj])
            k0 = pl.multiple_of(j * SC_ROWS + half * SC_LANES, SC_LANES)
            coef_v[pl.ds(k0, SC_LANES)] = gw_v[tt, pl.ds(k0, SC_LANES)] * _sc_gelu(tot)

    def acc_chunk(tt, j, slot, first):
        splat = [plsc.load_gather(coef_v, [jnp.full((SC_LANES,), j * SC_ROWS + r, jnp.int32)])
                 for r in range(SC_ROWS)]

        @plsc.parallel_loop(0, N_COL, unroll=2)
        def _(c):
            off = pl.multiple_of(c * SC_LANES, SC_LANES)
            part = [splat[r] * buf[slot, r, pl.ds(off, SC_LANES)] for r in range(SC_ACC_CHAINS)]
            for r in range(SC_ACC_CHAINS, SC_ROWS):
                part[r % SC_ACC_CHAINS] = part[r % SC_ACC_CHAINS] + splat[r] * buf[slot, r, pl.ds(off, SC_LANES)]
            while len(part) > 1:
                part = [part[i] + part[i + 1] for i in range(0, len(part), 2)]
            acc = part[0] if first else part[0] + o_v[tt, pl.ds(off, SC_LANES)]
            o_v[tt, pl.ds(off, SC_LANES)] = acc

    @pl.loop(0, tokens_per_worker // SC_TOK_BLOCK)
    def _(blk):
        t0 = pl.multiple_of(base + blk * SC_TOK_BLOCK, SC_TOK_BLOCK)
        pltpu.sync_copy(idx_hbm.at[pl.ds(t0, SC_TOK_BLOCK)], idx_v)
        pltpu.sync_copy(gw_hbm.at[pl.ds(t0, SC_TOK_BLOCK)], gw_v)
        pltpu.sync_copy(h2_hbm.at[pl.ds(t0, SC_TOK_BLOCK)], x_v)

        gather(0, 0).start()

        @pl.loop(0, SC_TOK_BLOCK)
        def _(tt):
            n_steps = 2 * n_gather
            for i in range(n_steps):
                if i + 1 < n_steps:
                    gather(tt, i + 1).start()
                else:
                    @pl.when(tt + 1 < SC_TOK_BLOCK)
                    def _():
                        gather(tt + 1, 0).start()
                gather(tt, i).wait()
                if i < n_gather:
                    act_chunk(tt, i, i % 2)
                else:
                    acc_chunk(tt, i - n_gather, i % 2, i == n_gather)

        pltpu.sync_copy(o_v, y_hbm.at[pl.ds(t0, SC_TOK_BLOCK)])


def _sc_expert_call(idx, gw, h2, u, v):
    n = idx.shape[0]
    workers = SC_CORES * SC_SUBCORES
    assert n % (workers * SC_TOK_BLOCK) == 0
    mesh = plsc.VectorSubcoreMesh(core_axis_name="c", subcore_axis_name="s")
    return pl.kernel(
        functools.partial(_sc_expert_body, n // workers),
        out_type=jax.ShapeDtypeStruct((n, D_MODEL), F32),
        mesh=mesh,
        scratch_types=[pltpu.VMEM((SC_TOK_BLOCK, N_SEL), jnp.int32),
                       pltpu.VMEM((SC_TOK_BLOCK, N_SEL), F32),
                       pltpu.VMEM((SC_TOK_BLOCK, D_MODEL), F32),
                       pltpu.VMEM((SC_TOK_BLOCK, D_MODEL), F32),
                       pltpu.VMEM((2, SC_ROWS, D_MODEL), F32),
                       pltpu.VMEM((N_SEL,), F32),
                       pltpu.VMEM((SC_LANES * SC_LANES,), F32),
                       pltpu.SemaphoreType.DMA((2,))],
        compiler_params=pltpu.CompilerParams(needs_layout_passes=False),
        name="peer_experts_sc",
    )(idx, gw, h2, u, v)


def _resid_kernel(x_ref, y_ref, g_ref, o_ref):
    o_ref[...] = x_ref[...] + g_ref[...] * y_ref[...]


def _resid_call(x, y, g2, per_token, tokens_per_req):
    n = x.shape[0]
    tm = min(512, n if per_token else tokens_per_req)
    row = pl.BlockSpec((tm, D_MODEL), lambda i: (i, 0))
    return pl.pallas_call(
        _resid_kernel,
        grid=(n // tm,),
        in_specs=[row, row, _mod_spec(per_token, tm, tokens_per_req)],
        out_specs=row,
        out_shape=jax.ShapeDtypeStruct((n, D_MODEL), F32),
        compiler_params=_cparams("parallel"),
        name="peer_residual",
    )(x, y, g2)


def _final_kernel(x_ref, g_ref, o_ref):
    x = x_ref[...]
    o_ref[...] = x * lax.rsqrt(jnp.mean(x * x, axis=-1, keepdims=True) + EPS) * g_ref[...]


def _final_call(x, g):
    n = x.shape[0]
    tm = min(512, n)
    return pl.pallas_call(
        _final_kernel,
        grid=(n // tm,),
        in_specs=[pl.BlockSpec((tm, D_MODEL), lambda i: (i, 0)), pl.BlockSpec((1, D_MODEL), lambda i: (0, 0))],
        out_specs=pl.BlockSpec((tm, D_MODEL), lambda i: (i, 0)),
        out_shape=jax.ShapeDtypeStruct((n, D_MODEL), F32),
        compiler_params=_cparams("parallel"),
        name="final_norm",
    )(x, g)


def _split_w_in(w_in_l, gate_b_l):
    cuts = np.cumsum([A_WIDTH + 2 * A_KV_WIDTH, 2 * M_WIDTH, M_WIDTH, M_WIDTH]).tolist()
    wa = w_in_l[:, :cuts[0]].astype(BF16)
    wqk = w_in_l[:, cuts[0]:cuts[1]].astype(BF16)
    wv = w_in_l[:, cuts[1]:cuts[2]].astype(BF16)
    wo = w_in_l[:, cuts[2]:cuts[3]].astype(BF16)
    ng = 2 * M_HEADS
    wg = jnp.pad(w_in_l[:, cuts[3]:], ((0, 0), (0, LANES - ng))).astype(BF16)
    gb = jnp.pad(gate_b_l.astype(F32), (0, LANES - ng)).reshape(1, LANES)
    return wa, wqk, wv, wo, wg, gb


def _layer(x, mods, per_token, batch, seq, t_valid, lw, bias_p, bias_c, bias_n, kv_cache, conv0, state):
    (norm_mix, norm_ffn, w_in, conv_w, conv_b, gate_b, sinks, m_norm, w_out, peer_query, peer_keys,
     peer_u, peer_v) = lw
    sh1, sc1, g1, sh2, sc2, g2 = mods
    wa, wqk, wv, wo, wg, gb = _split_w_in(w_in, gate_b)
    qkv, qkm, vm, om, gc = _in_call(x, sc1, sh1, norm_mix.reshape(1, -1), wa, wqk, wv, wo, wg, gb,
                                    per_token, seq)
    if kv_cache is None:
        att = _attn_p_call(qkv, bias_p, sinks, batch, seq)
        kv3 = qkv.reshape(batch, seq, -1)
        new_k = kv3[:, seq - WINDOW:, A_WIDTH:A_WIDTH + A_KV_WIDTH]
        new_v = kv3[:, seq - WINDOW:, A_WIDTH + A_KV_WIDTH:]
        chunk = M_CHUNK
    else:
        att, new_k, new_v = _attn_s_call(qkv, kv_cache[0], kv_cache[1], bias_c, bias_n, sinks, t_valid)
        chunk = seq
    c0, n0, m0 = state
    mo, c_new, n_new, m_new = _mlstm_call(qkm, vm, om, gc, conv_w, conv_b.reshape(1, -1),
                                          m_norm.reshape(1, -1), conv0, c0, n0, m0,
                                          batch, seq, chunk, min(t_valid, chunk))
    new_conv = qkm.reshape(batch, seq, -1)[:, t_valid - (CONV_W - 1):t_valid]
    x_mid, h2, qp = _out_call(att, mo, x, g1, sc2, sh2, norm_ffn.reshape(1, -1),
                              w_out[:A_WIDTH].astype(BF16), w_out[A_WIDTH:].astype(BF16),
                              peer_query.astype(BF16), per_token, seq)
    idx, gw = _select_call(qp, peer_keys.astype(BF16))
    y = _sc_expert_call(idx, gw, h2, peer_u, peer_v)
    x_new = _resid_call(x_mid, y, g2, per_token, seq)
    new_k = new_k.reshape(batch, WINDOW, A_KV_HEADS, A_HEAD_DIM)
    new_v = new_v.reshape(batch, WINDOW, A_KV_HEADS, A_HEAD_DIM)
    return x_new, (new_k, new_v, new_conv, c_new, n_new, m_new[:, :M_HEADS, 0])


def kernel(x_prompt, x_sample, c_prompt, c_sample, cache_k, cache_v, state_conv, state_C, state_n, state_m, rel_bias, w_ada, b_ada, norm_mix, norm_ffn, w_in, conv_w, conv_b, gate_b, attn_sinks, m_norm, w_out, peer_query, peer_keys, peer_u, peer_v, norm_final):
    depth = w_ada.shape[0]
    bp, tp, d = x_prompt.shape
    bs, ts, _ = x_sample.shape
    assert tp % WINDOW == 0 and tp % M_CHUNK == 0 and ts <= SAMPLE_PAD and ts >= CONV_W - 1

    mod_all = _ada_call(jnp.concatenate([c_prompt, c_sample], axis=0), w_ada, b_ada)

    qi = np.arange(WINDOW)[:, None]
    bias_p = _bias_call(rel_bias, qi + WINDOW - np.arange(2 * WINDOW)[None, :])
    qs = np.arange(SAMPLE_PAD)[:, None]
    bias_c = _bias_call(rel_bias, qs + WINDOW - np.arange(WINDOW)[None, :])
    bias_n = _bias_call(rel_bias, qs - np.arange(SAMPLE_PAD)[None, :])

    xp = x_prompt.reshape(bp * tp, d)
    xs = jnp.pad(x_sample, ((0, 0), (0, SAMPLE_PAD - ts), (0, 0))).reshape(bs * SAMPLE_PAD, d)
    halo_pad = ((0, 0), (SUBLANES - (CONV_W - 1), 0), (0, 0))
    zero_state = (jnp.zeros((bp, M_HEADS, M_HEAD_DIM, M_HEAD_DIM), F32),
                  jnp.zeros((bp, M_HEADS, M_HEAD_DIM), F32),
                  jnp.zeros((bp, SUBLANES, LANES), F32))
    zero_conv = jnp.zeros((bp, SUBLANES, 2 * M_WIDTH), F32)

    st_p, st_s = [], []
    for l in range(depth):
        lw = (norm_mix[l], norm_ffn[l], w_in[l], conv_w[l], conv_b[l], gate_b[l], attn_sinks[l], m_norm[l],
              w_out[l], peer_query[l], peer_keys[l], peer_u[l], peer_v[l])
        mod_p = [m.reshape(bp, 1, d) for m in jnp.split(mod_all[l, :bp], 6, axis=-1)]
        mod_s = [jnp.repeat(m, SAMPLE_PAD, axis=0) for m in jnp.split(mod_all[l, bp:], 6, axis=-1)]
        xp, sp = _layer(xp, mod_p, False, bp, tp, tp, lw, bias_p, None, None, None, zero_conv, zero_state)
        state_s = (state_C[l].astype(F32), state_n[l].astype(F32),
                   jnp.broadcast_to(jnp.pad(state_m[l].astype(F32), ((0, 0), (0, SUBLANES - M_HEADS)))[:, :, None],
                                    (bs, SUBLANES, LANES)))
        kv_cache = (cache_k[l].reshape(bs, WINDOW, A_KV_WIDTH), cache_v[l].reshape(bs, WINDOW, A_KV_WIDTH))
        xs, ss = _layer(xs, mod_s, True, bs, SAMPLE_PAD, ts, lw, None, bias_c, bias_n, kv_cache,
                        jnp.pad(state_conv[l].astype(F32), halo_pad), state_s)
        st_p.append(sp)
        st_s.append(ss)

    gfin = norm_final.reshape(1, d)
    y_prompt = _final_call(xp, gfin).reshape(bp, tp, d)
    y_sample = _final_call(xs, gfin).reshape(bs, SAMPLE_PAD, d)[:, :ts]
    outs_p = [jnp.stack([s[i] for s in st_p]) for i in range(6)]
    outs_s = [jnp.stack([s[i] for s in st_s]) for i in range(6)]
    return (y_prompt, y_sample, *outs_p, *outs_s)
```

```python
import functools
import math

import numpy as np
import jax
import jax.numpy as jnp
from jax import lax
from jax.experimental import pallas as pl
from jax.experimental.pallas import tpu as pltpu
from jax.experimental.pallas import tpu_sc as plsc

F32 = jnp.float32
BF16 = jnp.bfloat16
HIGHEST = lax.Precision.HIGHEST

D_MODEL = 1024
A_HEADS = 8
A_KV_HEADS = 2
A_GROUP = A_HEADS // A_KV_HEADS
A_HEAD_DIM = 64
A_WIDTH = A_HEADS * A_HEAD_DIM
A_KV_WIDTH = A_KV_HEADS * A_HEAD_DIM
WINDOW = 128
ATT_SCALE = A_HEAD_DIM ** -0.5
N_BUCKETS = 32
MAX_DISTANCE = WINDOW
M_HEADS = 4
M_HEAD_DIM = 128
M_WIDTH = M_HEADS * M_HEAD_DIM
CONV_W = 4
M_CHUNK = 64
N_KEYS = 128
P_HEADS = 8
P_TOPK = 16
P_KEY_DIM = 256
P_HALF = P_KEY_DIM // 2
EPS = 1e-6
NEG_INF = -1e30

LANES = 128
SUBLANES = 8
SAMPLE_PAD = SUBLANES
VMEM_LIMIT = 48 * 1024 * 1024
PROMPT_GROUPS = 2

NT_DIMS = (((1,), (1,)), ((), ()))
TN_DIMS = (((0,), (0,)), ((), ()))


def _cparams(*sem):
    return pltpu.CompilerParams(dimension_semantics=sem, vmem_limit_bytes=VMEM_LIMIT)


def _bdot(a, b):
    return jnp.dot(a.astype(BF16), b.astype(BF16), preferred_element_type=F32)


def _bdot_nt(a, b):
    return lax.dot_general(a.astype(BF16), b.astype(BF16), NT_DIMS, preferred_element_type=F32)


def _sigmoid(x):
    return 1.0 / (1.0 + jnp.exp(-x))


def _log_sigmoid(x):
    return jnp.minimum(x, 0.0) - jnp.log1p(jnp.exp(-jnp.abs(x)))


def _gelu_tanh(x):
    c = math.sqrt(2.0 / math.pi)
    return x * (0.5 * (1.0 + jnp.tanh(c * (x + 0.044715 * (x * x * x)))))


def _ada_kernel(c_ref, w_ref, b_ref, o_ref):
    c = c_ref[...]
    s = c * _sigmoid(c)
    o_ref[...] = jnp.dot(s, w_ref[...], precision=HIGHEST, preferred_element_type=F32) + b_ref[...]


def _ada_call(c_all, w_ada, b_ada):
    depth, d, n6 = w_ada.shape
    rows = c_all.shape[0]
    bn = 1024
    return pl.pallas_call(
        _ada_kernel,
        grid=(depth, n6 // bn),
        in_specs=[
            pl.BlockSpec((rows, d), lambda l, j: (0, 0)),
            pl.BlockSpec((None, d, bn), lambda l, j: (l, 0, j)),
            pl.BlockSpec((None, 1, bn), lambda l, j: (l, 0, j)),
        ],
        out_specs=pl.BlockSpec((None, rows, bn), lambda l, j: (l, 0, j)),
        out_shape=jax.ShapeDtypeStruct((depth, rows, n6), F32),
        compiler_params=_cparams("parallel", "parallel"),
        name="ada_mod",
    )(c_all, w_ada, b_ada.reshape(depth, 1, n6))


def _mod_spec(per_token, tm, tokens_per_req):
    if per_token:
        return pl.BlockSpec((tm, D_MODEL), lambda i: (i, 0))
    tiles = tokens_per_req // tm
    return pl.BlockSpec((None, 1, D_MODEL), lambda i: (i // tiles, 0, 0))


def _in_kernel(x_ref, sc_ref, sh_ref, g_ref, wa_ref, wqk_ref, wv_ref, wo_ref, wg_ref, gb_ref,
               qkv_ref, qkm_ref, v_ref, o_ref, gc_ref):
    x = x_ref[...]
    y = x * lax.rsqrt(jnp.mean(x * x, axis=-1, keepdims=True) + EPS) * g_ref[...]
    h = (y * (1.0 + sc_ref[...]) + sh_ref[...]).astype(BF16)
    qkv_ref[...] = jnp.dot(h, wa_ref[...], preferred_element_type=F32)
    qkm_ref[...] = jnp.dot(h, wqk_ref[...], preferred_element_type=F32)
    v_ref[...] = jnp.dot(h, wv_ref[...], preferred_element_type=F32)
    o_ref[...] = jnp.dot(h, wo_ref[...], preferred_element_type=F32)
    g = jnp.dot(h, wg_ref[...], preferred_element_type=F32) + gb_ref[...]
    lane = lax.broadcasted_iota(jnp.int32, g.shape, 1)
    gc_ref[...] = jnp.where(lane < M_HEADS, g, jnp.where(lane < 2 * M_HEADS, _log_sigmoid(g), 0.0))


def _in_call(x, sc, sh, gnorm, wa, wqk, wv, wo, wg, gb, per_token, tokens_per_req):
    n = x.shape[0]
    tm = min(512, n if per_token else tokens_per_req)
    mod = _mod_spec(per_token, tm, tokens_per_req)
    full = lambda shape: pl.BlockSpec(shape, lambda i: (0,) * len(shape))
    row = lambda w: pl.BlockSpec((tm, w), lambda i: (i, 0))
    return pl.pallas_call(
        _in_kernel,
        grid=(n // tm,),
        in_specs=[row(D_MODEL), mod, mod, full((1, D_MODEL)), full(wa.shape), full(wqk.shape),
                  full(wv.shape), full(wo.shape), full(wg.shape), full((1, LANES))],
        out_specs=[row(wa.shape[1]), row(wqk.shape[1]), row(wv.shape[1]), row(wo.shape[1]), row(LANES)],
        out_shape=[jax.ShapeDtypeStruct((n, w), F32)
                   for w in (wa.shape[1], wqk.shape[1], wv.shape[1], wo.shape[1], LANES)],
        compiler_params=_cparams("parallel"),
        name="in_proj",
    )(x, sc, sh, gnorm, wa, wqk, wv, wo, wg, gb)


def _t5_bucket_np(dist):
    n = np.maximum(dist, 0)
    max_exact = N_BUCKETS // 2
    nf = np.maximum(n, 1).astype(np.float64)
    large = max_exact + (np.log(nf / max_exact) / math.log(MAX_DISTANCE / max_exact)
                         * (N_BUCKETS - max_exact)).astype(np.int32)
    return np.where(n < max_exact, n, np.minimum(large, N_BUCKETS - 1)).astype(np.int32)


def _bias_kernel(bucket_ref, rel_ref, o_ref):
    bucket = bucket_ref[...]
    for h in range(A_HEADS):
        acc = jnp.zeros(bucket.shape, F32)
        for b in range(N_BUCKETS):
            acc = jnp.where(bucket == b, rel_ref[b, h], acc)
        o_ref[h] = acc


def _bias_call(rel_bias, dist):
    bucket = jnp.asarray(_t5_bucket_np(dist))
    nq, nk = dist.shape
    return pl.pallas_call(
        _bias_kernel,
        in_specs=[pl.BlockSpec((nq, nk), lambda: (0, 0)),
                  pl.BlockSpec(memory_space=pltpu.SMEM)],
        out_specs=pl.BlockSpec((A_HEADS, nq, nk), lambda: (0, 0, 0)),
        out_shape=jax.ShapeDtypeStruct((A_HEADS, nq, nk), F32),
        name="t5_bias",
    )(bucket, rel_bias)


def _attn_p_kernel(q_ref, kp_ref, kc_ref, vp_ref, vc_ref, bias_ref, sink_ref, o_ref):
    i = pl.program_id(1)
    qi = lax.broadcasted_iota(jnp.int32, (WINDOW, WINDOW), 0)
    kj = lax.broadcasted_iota(jnp.int32, (WINDOW, WINDOW), 1)
    valid_prev = jnp.logical_and(kj > qi, i > 0)
    valid_cur = kj <= qi
    q = q_ref[...]
    for h in range(A_HEADS):
        kv = h // A_GROUP
        qh = q[:, h * A_HEAD_DIM:(h + 1) * A_HEAD_DIM]
        sl = slice(kv * A_HEAD_DIM, (kv + 1) * A_HEAD_DIM)
        bias = bias_ref[h]
        sp = _bdot_nt(qh, kp_ref[:, sl]) * ATT_SCALE + bias[:, :WINDOW]
        sc = _bdot_nt(qh, kc_ref[:, sl]) * ATT_SCALE + bias[:, WINDOW:]
        sp = jnp.where(valid_prev, sp, NEG_INF)
        sc = jnp.where(valid_cur, sc, NEG_INF)
        sink = sink_ref[0, h]
        mx = jnp.maximum(jnp.maximum(jnp.max(sp, axis=-1, keepdims=True),
                                     jnp.max(sc, axis=-1, keepdims=True)), sink)
        ep = jnp.exp(sp - mx)
        ec = jnp.exp(sc - mx)
        den = (jnp.sum(ep, axis=-1, keepdims=True) + jnp.sum(ec, axis=-1, keepdims=True)
               + jnp.exp(sink - mx))
        o = _bdot(ep / den, vp_ref[:, sl]) + _bdot(ec / den, vc_ref[:, sl])
        o_ref[:, h * A_HEAD_DIM:(h + 1) * A_HEAD_DIM] = o


def _attn_p_call(qkv, bias, sinks, batch, seq):
    nb = seq // WINDOW
    n = batch * seq
    kcol = A_WIDTH // A_KV_WIDTH
    vcol = kcol + 1
    cur = lambda col: pl.BlockSpec((WINDOW, A_KV_WIDTH), lambda b, i: (b * nb + i, col))
    prev = lambda col: pl.BlockSpec((WINDOW, A_KV_WIDTH),
                                    lambda b, i: (b * nb + jnp.maximum(i - 1, 0), col))
    return pl.pallas_call(
        _attn_p_kernel,
        grid=(batch, nb),
        in_specs=[pl.BlockSpec((WINDOW, A_WIDTH), lambda b, i: (b * nb + i, 0)),
                  prev(kcol), cur(kcol), prev(vcol), cur(vcol),
                  pl.BlockSpec((A_HEADS, WINDOW, 2 * WINDOW), lambda b, i: (0, 0, 0)),
                  pl.BlockSpec(memory_space=pltpu.SMEM)],
        out_specs=pl.BlockSpec((WINDOW, A_WIDTH), lambda b, i: (b * nb + i, 0)),
        out_shape=jax.ShapeDtypeStruct((n, A_WIDTH), F32),
        compiler_params=_cparams("parallel", "parallel"),
        name="swa_prompt",
    )(qkv, qkv, qkv, qkv, qkv, bias, sinks.reshape(1, A_HEADS))


def _attn_s_kernel(n_new, qkv_ref, ck_ref, cv_ref, bc_ref, bn_ref, sink_ref,
                   o_ref, nk_ref, nv_ref, kk_s, vv_s):
    qkv = qkv_ref[...]
    knew = qkv[:, A_WIDTH:A_WIDTH + A_KV_WIDTH]
    vnew = qkv[:, A_WIDTH + A_KV_WIDTH:A_WIDTH + 2 * A_KV_WIDTH]
    ck = ck_ref[...]
    cv = cv_ref[...]
    qi = lax.broadcasted_iota(jnp.int32, (SAMPLE_PAD, WINDOW), 0)
    kj = lax.broadcasted_iota(jnp.int32, (SAMPLE_PAD, WINDOW), 1)
    valid_c = kj > qi
    qcol = lax.broadcasted_iota(jnp.int32, (SAMPLE_PAD, 1), 0)
    for h in range(A_HEADS):
        kv = h // A_GROUP
        sl = slice(kv * A_HEAD_DIM, (kv + 1) * A_HEAD_DIM)
        qh = qkv[:, h * A_HEAD_DIM:(h + 1) * A_HEAD_DIM]
        s_c = lax.dot_general(qh, ck[:, sl], NT_DIMS, precision=HIGHEST,
                              preferred_element_type=F32) * ATT_SCALE + bc_ref[h]
        s_c = jnp.where(valid_c, s_c, NEG_INF)
        bn = bn_ref[h]
        s_n = []
        for j in range(n_new):
            sj = jnp.sum(qh * knew[j:j + 1, sl], axis=-1, keepdims=True) * ATT_SCALE + bn[:, j:j + 1]
            s_n.append(jnp.where(qcol >= j, sj, NEG_INF))
        sink = sink_ref[0, h]
        mx = jnp.maximum(jnp.max(s_c, axis=-1, keepdims=True), sink)
        for sj in s_n:
            mx = jnp.maximum(mx, sj)
        e_c = jnp.exp(s_c - mx)
        den = jnp.sum(e_c, axis=-1, keepdims=True) + jnp.exp(sink - mx)
        o = jnp.dot(e_c, cv[:, sl], precision=HIGHEST, preferred_element_type=F32)
        for j, sj in enumerate(s_n):
            ej = jnp.exp(sj - mx)
            den = den + ej
            o = o + ej * vnew[j:j + 1, sl]
        o_ref[:, h * A_HEAD_DIM:(h + 1) * A_HEAD_DIM] = o / den
    kk_s[0:WINDOW, :] = ck
    kk_s[WINDOW:WINDOW + SAMPLE_PAD, :] = knew
    vv_s[0:WINDOW, :] = cv
    vv_s[WINDOW:WINDOW + SAMPLE_PAD, :] = vnew
    nk_ref[...] = kk_s[n_new:n_new + WINDOW, :]
    nv_ref[...] = vv_s[n_new:n_new + WINDOW, :]


def _attn_s_call(qkv, ck, cv, bias_c, bias_n, sinks, n_new):
    nreq = ck.shape[0]
    wq = qkv.shape[1]
    full3 = lambda shape: pl.BlockSpec(shape, lambda b: (0, 0, 0))
    cache = pl.BlockSpec((None, WINDOW, A_KV_WIDTH), lambda b: (b, 0, 0))
    return pl.pallas_call(
        functools.partial(_attn_s_kernel, n_new),
        grid=(nreq,),
        in_specs=[pl.BlockSpec((SAMPLE_PAD, wq), lambda b: (b, 0)), cache, cache,
                  full3(bias_c.shape), full3(bias_n.shape),
                  pl.BlockSpec(memory_space=pltpu.SMEM)],
        out_specs=[pl.BlockSpec((SAMPLE_PAD, A_WIDTH), lambda b: (b, 0)), cache, cache],
        out_shape=[jax.ShapeDtypeStruct((nreq * SAMPLE_PAD, A_WIDTH), F32),
                   jax.ShapeDtypeStruct(ck.shape, F32), jax.ShapeDtypeStruct(cv.shape, F32)],
        scratch_shapes=[pltpu.VMEM((WINDOW + SAMPLE_PAD, A_KV_WIDTH), F32),
                        pltpu.VMEM((WINDOW + SAMPLE_PAD, A_KV_WIDTH), F32)],
        compiler_params=_cparams("parallel"),
        name="swa_sample",
    )(qkv, ck, cv, bias_c, bias_n, sinks.reshape(1, A_HEADS))


def _mlstm_kernel(chunk, t_valid, qk_ref, v_ref, og_ref, gc_ref, cw_ref, cb_ref, mn_ref,
                  conv0_ref, c0_ref, n0_ref, m0_ref,
                  out_ref, cout_ref, nout_ref, mout_ref,
                  xp_s, c_s, n_s, m_s):
    step = pl.program_id(1)
    halo = SUBLANES

    @pl.when(step == 0)
    def _():
        xp_s[0:halo, :] = conv0_ref[...]
        c_s[...] = c0_ref[...]
        n_s[...] = n0_ref[...]
        m_s[...] = m0_ref[...]

    xp_s[halo:halo + chunk, :] = qk_ref[...]
    cw = cw_ref[...]
    y = cb_ref[...]
    for i in range(CONV_W):
        off = halo - (CONV_W - 1) + i
        y = y + xp_s[off:off + chunk, :] * cw[i:i + 1, :]
    xp_s[0:halo, :] = xp_s[chunk:chunk + halo, :]
    y = y * _sigmoid(y)
    q_all = y[:, :M_WIDTH]
    k_all = y[:, M_WIDTH:] * (M_HEAD_DIM ** -0.5)

    g = gc_ref[...]
    if t_valid < chunk:
        row = lax.broadcasted_iota(jnp.int32, g.shape, 0)
        lane = lax.broadcasted_iota(jnp.int32, g.shape, 1)
        g = jnp.where(row < t_valid, g, jnp.where(lane < M_HEADS, NEG_INF, 0.0))
    tr = lax.broadcasted_iota(jnp.int32, (chunk, chunk), 0)
    tc = lax.broadcasted_iota(jnp.int32, (chunk, chunk), 1)
    causal = tr >= tc
    tri = causal.astype(F32)
    bcol = jnp.dot(tri, g, precision=HIGHEST, preferred_element_type=F32)
    er = lax.broadcasted_iota(jnp.int32, (SUBLANES, LANES), 0)
    ec = lax.broadcasted_iota(jnp.int32, (SUBLANES, LANES), 1)
    eye = (er == ec).astype(F32)
    g_rows = lax.dot_general(eye, g, NT_DIMS, precision=HIGHEST, preferred_element_type=F32)
    b_rows = lax.dot_general(eye, bcol, NT_DIMS, precision=HIGHEST, preferred_element_type=F32)

    for h in range(M_HEADS):
        hs = slice(h * M_HEAD_DIM, (h + 1) * M_HEAD_DIM)
        b_c = bcol[:, M_HEADS + h:M_HEADS + h + 1]
        ig_c = g[:, h:h + 1]
        b_r = b_rows[M_HEADS + h:M_HEADS + h + 1, :]
        ig_r = g_rows[h:h + 1, :]
        m_prev = m_s[h:h + 1, 0:1]
        logw = jnp.where(causal, b_c - b_r + ig_r, -jnp.inf)
        inter = b_c + m_prev
        m_t = jnp.maximum(inter, jnp.max(logw, axis=-1, keepdims=True))
        w = jnp.exp(logw - m_t)
        a = jnp.exp(inter - m_t)
        q = q_all[:, hs]
        k = k_all[:, hs]
        v = v_ref[:, hs]
        cmat = c_s[h]
        nvec = n_s[h:h + 1, :]
        wqk = w * _bdot_nt(q, k)
        num = _bdot(wqk, v) + a * _bdot_nt(q, cmat)
        den = jnp.sum(wqk, axis=-1, keepdims=True) + a * jnp.sum(q * nvec, axis=-1, keepdims=True)
        hh = num / jnp.maximum(jnp.abs(den), jnp.exp(-m_t))
        m_new = m_t[chunk - 1:chunk, :]
        b_last = b_c[chunk - 1:chunk, :]
        wl = jnp.exp(b_last - b_c + ig_c - m_new)
        al = jnp.exp(b_last + m_prev - m_new)
        c_s[h] = al * cmat + lax.dot_general((v * wl).astype(BF16), k.astype(BF16), TN_DIMS,
                                             preferred_element_type=F32)
        n_s[h:h + 1, :] = al * nvec + jnp.sum(wl * k, axis=0, keepdims=True)
        m_s[h:h + 1, :] = jnp.broadcast_to(m_new, (1, LANES))
        hn = hh * lax.rsqrt(jnp.mean(hh * hh, axis=-1, keepdims=True) + EPS) * mn_ref[:, hs]
        out_ref[:, hs] = _sigmoid(og_ref[:, hs]) * hn

    @pl.when(step == pl.num_programs(1) - 1)
    def _():
        cout_ref[...] = c_s[...]
        nout_ref[...] = n_s[...]
        mout_ref[...] = m_s[...]


def _mlstm_call(qk, v, og, gc, conv_w, conv_b, m_norm, conv0, c0, n0, m0, batch, seq, chunk, t_valid):
    nc = seq // chunk
    n = batch * seq
    row = lambda w: pl.BlockSpec((chunk, w), lambda b, c: (b * nc + c, 0))
    full2 = lambda shape: pl.BlockSpec(shape, lambda b, c: (0, 0))
    per_b = lambda shape: pl.BlockSpec((None,) + shape, lambda b, c: (b,) + (0,) * len(shape))
    dh = M_HEAD_DIM
    return pl.pallas_call(
        functools.partial(_mlstm_kernel, chunk, t_valid),
        grid=(batch, nc),
        in_specs=[row(2 * M_WIDTH), row(M_WIDTH), row(M_WIDTH), row(LANES),
                  full2((CONV_W, 2 * M_WIDTH)), full2((1, 2 * M_WIDTH)), full2((1, M_WIDTH)),
                  per_b((SUBLANES, 2 * M_WIDTH)), per_b((M_HEADS, dh, dh)), per_b((M_HEADS, dh)),
                  per_b((SUBLANES, LANES))],
        out_specs=[row(M_WIDTH), per_b((M_HEADS, dh, dh)), per_b((M_HEADS, dh)), per_b((SUBLANES, LANES))],
        out_shape=[jax.ShapeDtypeStruct((n, M_WIDTH), F32),
                   jax.ShapeDtypeStruct((batch, M_HEADS, dh, dh), F32),
                   jax.ShapeDtypeStruct((batch, M_HEADS, dh), F32),
                   jax.ShapeDtypeStruct((batch, SUBLANES, LANES), F32)],
        scratch_shapes=[pltpu.VMEM((SUBLANES + chunk, 2 * M_WIDTH), F32),
                        pltpu.VMEM((M_HEADS, dh, dh), F32),
                        pltpu.VMEM((M_HEADS, dh), F32),
                        pltpu.VMEM((SUBLANES, LANES), F32)],
        compiler_params=_cparams("parallel", "arbitrary"),
        name="mlstm",
    )(qk, v, og, gc, conv_w, conv_b, m_norm, conv0, c0, n0, m0)


def _out_kernel(att_ref, mo_ref, x_ref, g1_ref, sc_ref, sh_ref, gn_ref, wa_ref, wm_ref, wq_ref,
                xo_ref, h2_ref, qp_ref):
    mix = (jnp.dot(att_ref[...].astype(BF16), wa_ref[...], preferred_element_type=F32)
           + jnp.dot(mo_ref[...].astype(BF16), wm_ref[...], preferred_element_type=F32))
    x = x_ref[...] + g1_ref[...] * mix
    xo_ref[...] = x
    y = x * lax.rsqrt(jnp.mean(x * x, axis=-1, keepdims=True) + EPS) * gn_ref[...]
    h2 = y * (1.0 + sc_ref[...]) + sh_ref[...]
    h2_ref[...] = h2
    qp_ref[...] = jnp.dot(h2.astype(BF16), wq_ref[...], preferred_element_type=F32)


def _out_call(att, mo, x, g1, sc, sh, gnorm, wa, wm, wq, per_token, tokens_per_req):
    n = x.shape[0]
    tm = min(256, n if per_token else tokens_per_req)
    mod = _mod_spec(per_token, tm, tokens_per_req)
    full = lambda shape: pl.BlockSpec(shape, lambda i: (0,) * len(shape))
    row = lambda w: pl.BlockSpec((tm, w), lambda i: (i, 0))
    nq = wq.shape[1]
    return pl.pallas_call(
        _out_kernel,
        grid=(n // tm,),
        in_specs=[row(A_WIDTH), row(M_WIDTH), row(D_MODEL), mod, mod, mod, full((1, D_MODEL)),
                  full(wa.shape), full(wm.shape), full(wq.shape)],
        out_specs=[row(D_MODEL), row(D_MODEL), row(nq)],
        out_shape=[jax.ShapeDtypeStruct((n, D_MODEL), F32), jax.ShapeDtypeStruct((n, D_MODEL), F32),
                   jax.ShapeDtypeStruct((n, nq), F32)],
        compiler_params=_cparams("parallel"),
        name="out_proj",
    )(att, mo, x, g1, sc, sh, gnorm, wa, wm, wq)


def _pk_cells():
    return [(a, b) for a in range(P_TOPK) for b in range(P_TOPK) if (a + 1) * (b + 1) <= P_TOPK]


PK_CELL_ROWS = 64


def _pk_expand_mats():
    cells = _pk_cells()
    e0 = np.zeros((PK_CELL_ROWS, LANES), np.float32)
    e1 = np.zeros((PK_CELL_ROWS, LANES), np.float32)
    for j, (a, b) in enumerate(cells):
        e0[j, a] = 1.0
        e1[j, b] = 1.0
    return e0, e1, len(cells)


def _top_rows(s, rowf, rounds):
    n_rows = s.shape[0]
    vals, idxs = [], []
    for _ in range(rounds):
        m = jnp.max(s, axis=0, keepdims=True)
        i = jnp.min(jnp.where(s == m, rowf, float(n_rows)), axis=0, keepdims=True)
        vals.append(m)
        idxs.append(i)
        s = jnp.where(rowf == i, -jnp.inf, s)
    return jnp.concatenate(vals, axis=0), jnp.concatenate(idxs, axis=0)


def _select_kernel(n_cells, qp_ref, keys_ref, e0_ref, e1_ref, idx_ref, gw_ref, idx_s, gw_s):
    tm = qp_ref.shape[0]
    keyf = lax.broadcasted_iota(jnp.int32, (N_KEYS, tm), 0).astype(F32)
    cellf = lax.broadcasted_iota(jnp.int32, (PK_CELL_ROWS, tm), 0).astype(F32)
    e0 = e0_ref[...]
    e1 = e1_ref[...]
    pad = jnp.zeros((LANES - P_TOPK, tm), F32)

    def head(h, carry):
        h = jnp.asarray(h, jnp.int32)
        sub = []
        for c in range(2):
            col = pl.multiple_of((h * 2 + c) * P_HALF, P_HALF)
            s = _bdot_nt(keys_ref[h, c], qp_ref[:, pl.ds(col, P_HALF)])
            sub.append(_top_rows(s, keyf, P_TOPK))
        (v0, i0), (v1, i1) = sub
        expand = lambda e, x: jnp.dot(e, jnp.concatenate([x, pad], axis=0), precision=HIGHEST,
                                      preferred_element_type=F32)
        cand = expand(e0, v0) + expand(e1, v1)
        cidx = expand(e0, i0 * float(N_KEYS)) + expand(e1, i1)
        cand = jnp.where(cellf < n_cells, cand, -jnp.inf)
        best, eidx = [], []
        for _ in range(P_TOPK):
            m = jnp.max(cand, axis=0, keepdims=True)
            j = jnp.min(jnp.where(cand == m, cellf, float(PK_CELL_ROWS)), axis=0, keepdims=True)
            hit = cellf == j
            eidx.append(jnp.max(jnp.where(hit, cidx, -1.0), axis=0, keepdims=True))
            best.append(m)
            cand = jnp.where(hit, -jnp.inf, cand)
        best = jnp.concatenate(best, axis=0)
        e = jnp.exp(best - best[0:1, :])
        row0 = pl.multiple_of(h * P_TOPK, P_TOPK)
        gw_s[pl.ds(row0, P_TOPK), :] = e / jnp.sum(e, axis=0, keepdims=True)
        idx_s[pl.ds(row0, P_TOPK), :] = jnp.concatenate(eidx, axis=0)
        return carry

    lax.fori_loop(0, P_HEADS, head, 0)
    idx_ref[...] = idx_s[...].T.astype(jnp.int32)
    gw_ref[...] = gw_s[...].T


def _select_call(qp, keys_bf16):
    n = qp.shape[0]
    tm = min(LANES, n)
    e0, e1, n_cells = _pk_expand_mats()
    full = lambda shape: pl.BlockSpec(shape, lambda i: (0,) * len(shape))
    return pl.pallas_call(
        functools.partial(_select_kernel, n_cells),
        grid=(n // tm,),
        in_specs=[pl.BlockSpec((tm, qp.shape[1]), lambda i: (i, 0)), full(keys_bf16.shape),
                  full((PK_CELL_ROWS, LANES)), full((PK_CELL_ROWS, LANES))],
        out_specs=[pl.BlockSpec((tm, LANES), lambda i: (i, 0)), pl.BlockSpec((tm, LANES), lambda i: (i, 0))],
        out_shape=[jax.ShapeDtypeStruct((n, LANES), jnp.int32), jax.ShapeDtypeStruct((n, LANES), F32)],
        scratch_shapes=[pltpu.VMEM((P_HEADS * P_TOPK, tm), F32), pltpu.VMEM((P_HEADS * P_TOPK, tm), F32)],
        compiler_params=_cparams("parallel"),
        name="peer_select",
    )(qp, keys_bf16, jnp.asarray(e0), jnp.asarray(e1))


N_SEL = P_HEADS * P_TOPK

SC_CORES = 2
SC_SUBCORES = 16
SC_LANES = 16
SC_TOK_BLOCK = 8
SC_ROWS = 32
SC_ROW_BLOCK = 16
SC_ACC_CHAINS = 4
N_COL = D_MODEL // SC_LANES


def _sc_gelu(x):
    z = math.sqrt(2.0 / math.pi) * (x + 0.044715 * (x * x * x))
    t = 1.0 - 2.0 / (jnp.exp(2.0 * z) + 1.0)
    return x * (0.5 * (1.0 + t))


def _sc_expert_body(tokens_per_worker, idx_hbm, gw_hbm, h2_hbm, u_hbm, v_hbm, y_hbm,
                    idx_v, gw_v, x_v, o_v, buf, coef_v, tr_v, sem):
    wid = lax.axis_index("s") * SC_CORES + lax.axis_index("c")
    base = wid * tokens_per_worker
    lane = lax.iota(jnp.int32, SC_LANES)
    n_gather = N_SEL // SC_ROWS
    zero = jnp.zeros((SC_LANES,), F32)

    def gather(tt, i):
        table = u_hbm if i < n_gather else v_hbm
        j = i % n_gather
        slot = i % 2
        return pltpu.make_async_copy(table.at[idx_v.at[tt, pl.ds(j * SC_ROWS, SC_ROWS)]], buf.at[slot],
                                     sem.at[slot])

    def act_chunk(tt, j, slot):
        @pl.loop(0, SC_ROWS // SC_LANES)
        def _(half):
            for rb in range(SC_LANES // SC_ROW_BLOCK):
                r0 = half * SC_LANES + rb * SC_ROW_BLOCK

                def col(c, accs):
                    off = pl.multiple_of(c * SC_LANES, SC_LANES)
                    xv = x_v[tt, pl.ds(off, SC_LANES)]
                    return tuple(a + buf[slot, r0 + r, pl.ds(off, SC_LANES)] * xv for r, a in enumerate(accs))

                accs = lax.fori_loop(0, N_COL, col, (zero,) * SC_ROW_BLOCK)
                for r in range(SC_ROW_BLOCK):
                    tr_v[pl.ds((rb * SC_ROW_BLOCK + r) * SC_LANES, SC_LANES)] = accs[r]
            tot = zero
            for jj in range(SC_LANES):
                tot = tot + plsc.load_gather(tr_v, [lane * SC_LANES + jj])
            k0 = pl.multiple_of(j * SC_ROWS + half * SC_LANES, SC_LANES)
            coef_v[pl.ds(k0, SC_LANES)] = gw_v[tt, pl.ds(k0, SC_LANES)] * _sc_gelu(tot)

    def acc_chunk(tt, j, slot, first):
        splat = [plsc.load_gather(coef_v, [jnp.full((SC_LANES,), j * SC_ROWS + r, jnp.int32)])
                 for r in range(SC_ROWS)]

        @plsc.parallel_loop(0, N_COL, unroll=2)
        def _(c):
            off = pl.multiple_of(c * SC_LANES, SC_LANES)
            part = [splat[r] * buf[slot, r, pl.ds(off, SC_LANES)] for r in range(SC_ACC_CHAINS)]
            for r in range(SC_ACC_CHAINS, SC_ROWS):
                part[r % SC_ACC_CHAINS] = part[r % SC_ACC_CHAINS] + splat[r] * buf[slot, r, pl.ds(off, SC_LANES)]
            while len(part) > 1:
                part = [part[i] + part[i + 1] for i in range(0, len(part), 2)]
            acc = part[0] if first else part[0] + o_v[tt, pl.ds(off, SC_LANES)]
            o_v[tt, pl.ds(off, SC_LANES)] = acc

    @pl.loop(0, tokens_per_worker // SC_TOK_BLOCK)
    def _(blk):
        t0 = pl.multiple_of(base + blk * SC_TOK_BLOCK, SC_TOK_BLOCK)
        pltpu.sync_copy(idx_hbm.at[pl.ds(t0, SC_TOK_BLOCK)], idx_v)
        pltpu.sync_copy(gw_hbm.at[pl.ds(t0, SC_TOK_BLOCK)], gw_v)
        pltpu.sync_copy(h2_hbm.at[pl.ds(t0, SC_TOK_BLOCK)], x_v)

        gather(0, 0).start()

        @pl.loop(0, SC_TOK_BLOCK)
        def _(tt):
            n_steps = 2 * n_gather
            for i in range(n_steps):
                if i + 1 < n_steps:
                    gather(tt, i + 1).start()
                else:
                    @pl.when(tt + 1 < SC_TOK_BLOCK)
                    def _():
                        gather(tt + 1, 0).start()
                gather(tt, i).wait()
                if i < n_gather:
                    act_chunk(tt, i, i % 2)
                else:
                    acc_chunk(tt, i - n_gather, i % 2, i == n_gather)

        pltpu.sync_copy(o_v, y_hbm.at[pl.ds(t0, SC_TOK_BLOCK)])


def _sc_expert_call(idx, gw, h2, u, v):
    n = idx.shape[0]
    workers = SC_CORES * SC_SUBCORES
    assert n % (workers * SC_TOK_BLOCK) == 0
    mesh = plsc.VectorSubcoreMesh(core_axis_name="c", subcore_axis_name="s")
    return pl.kernel(
        functools.partial(_sc_expert_body, n // workers),
        out_type=jax.ShapeDtypeStruct((n, D_MODEL), F32),
        mesh=mesh,
        scratch_types=[pltpu.VMEM((SC_TOK_BLOCK, N_SEL), jnp.int32),
                       pltpu.VMEM((SC_TOK_BLOCK, N_SEL), F32),
                       pltpu.VMEM((SC_TOK_BLOCK, D_MODEL), F32),
                       pltpu.VMEM((SC_TOK_BLOCK, D_MODEL), F32),
                       pltpu.VMEM((2, SC_ROWS, D_MODEL), F32),
                       pltpu.VMEM((N_SEL,), F32),
                       pltpu.VMEM((SC_LANES * SC_LANES,), F32),
                       pltpu.SemaphoreType.DMA((2,))],
        compiler_params=pltpu.CompilerParams(needs_layout_passes=False),
        name="peer_experts_sc",
    )(idx, gw, h2, u, v)


def _resid_kernel(x_ref, y_ref, g_ref, o_ref):
    o_ref[...] = x_ref[...] + g_ref[...] * y_ref[...]


def _resid_call(x, y, g2, per_token, tokens_per_req):
    n = x.shape[0]
    tm = min(512, n if per_token else tokens_per_req)
    row = pl.BlockSpec((tm, D_MODEL), lambda i: (i, 0))
    return pl.pallas_call(
        _resid_kernel,
        grid=(n // tm,),
        in_specs=[row, row, _mod_spec(per_token, tm, tokens_per_req)],
        out_specs=row,
        out_shape=jax.ShapeDtypeStruct((n, D_MODEL), F32),
        compiler_params=_cparams("parallel"),
        name="peer_residual",
    )(x, y, g2)


def _final_kernel(x_ref, g_ref, o_ref):
    x = x_ref[...]
    o_ref[...] = x * lax.rsqrt(jnp.mean(x * x, axis=-1, keepdims=True) + EPS) * g_ref[...]


def _final_call(x, g):
    n = x.shape[0]
    tm = min(512, n)
    return pl.pallas_call(
        _final_kernel,
        grid=(n // tm,),
        in_specs=[pl.BlockSpec((tm, D_MODEL), lambda i: (i, 0)), pl.BlockSpec((1, D_MODEL), lambda i: (0, 0))],
        out_specs=pl.BlockSpec((tm, D_MODEL), lambda i: (i, 0)),
        out_shape=jax.ShapeDtypeStruct((n, D_MODEL), F32),
        compiler_params=_cparams("parallel"),
        name="final_norm",
    )(x, g)


def _split_w_in(w_in_l, gate_b_l):
    cuts = np.cumsum([A_WIDTH + 2 * A_KV_WIDTH, 2 * M_WIDTH, M_WIDTH, M_WIDTH]).tolist()
    wa = w_in_l[:, :cuts[0]].astype(BF16)
    wqk = w_in_l[:, cuts[0]:cuts[1]].astype(BF16)
    wv = w_in_l[:, cuts[1]:cuts[2]].astype(BF16)
    wo = w_in_l[:, cuts[2]:cuts[3]].astype(BF16)
    ng = 2 * M_HEADS
    wg = jnp.pad(w_in_l[:, cuts[3]:], ((0, 0), (0, LANES - ng))).astype(BF16)
    gb = jnp.pad(gate_b_l.astype(F32), (0, LANES - ng)).reshape(1, LANES)
    return wa, wqk, wv, wo, wg, gb


def _layer(x, mods, per_token, batch, seq, t_valid, lw, bias_p, bias_c, bias_n, kv_cache, conv0, state):
    (norm_mix, norm_ffn, w_in, conv_w, conv_b, gate_b, sinks, m_norm, w_out, peer_query, peer_keys,
     peer_u, peer_v) = lw
    sh1, sc1, g1, sh2, sc2, g2 = mods
    wa, wqk, wv, wo, wg, gb = _split_w_in(w_in, gate_b)
    qkv, qkm, vm, om, gc = _in_call(x, sc1, sh1, norm_mix.reshape(1, -1), wa, wqk, wv, wo, wg, gb,
                                    per_token, seq)
    if kv_cache is None:
        att = _attn_p_call(qkv, bias_p, sinks, batch, seq)
        kv3 = qkv.reshape(batch, seq, -1)
        new_k = kv3[:, seq - WINDOW:, A_WIDTH:A_WIDTH + A_KV_WIDTH]
        new_v = kv3[:, seq - WINDOW:, A_WIDTH + A_KV_WIDTH:]
        chunk = M_CHUNK
    else:
        att, new_k, new_v = _attn_s_call(qkv, kv_cache[0], kv_cache[1], bias_c, bias_n, sinks, t_valid)
        chunk = seq
    c0, n0, m0 = state
    mo, c_new, n_new, m_new = _mlstm_call(qkm, vm, om, gc, conv_w, conv_b.reshape(1, -1),
                                          m_norm.reshape(1, -1), conv0, c0, n0, m0,
                                          batch, seq, chunk, min(t_valid, chunk))
    new_conv = qkm.reshape(batch, seq, -1)[:, t_valid - (CONV_W - 1):t_valid]
    x_mid, h2, qp = _out_call(att, mo, x, g1, sc2, sh2, norm_ffn.reshape(1, -1),
                              w_out[:A_WIDTH].astype(BF16), w_out[A_WIDTH:].astype(BF16),
                              peer_query.astype(BF16), per_token, seq)
    idx, gw = _select_call(qp, peer_keys.astype(BF16))
    y = _sc_expert_call(idx, gw, h2, peer_u, peer_v)
    x_new = _resid_call(x_mid, y, g2, per_token, seq)
    new_k = new_k.reshape(batch, WINDOW, A_KV_HEADS, A_HEAD_DIM)
    new_v = new_v.reshape(batch, WINDOW, A_KV_HEADS, A_HEAD_DIM)
    return x_new, (new_k, new_v, new_conv, c_new, n_new, m_new[:, :M_HEADS, 0])


def kernel(x_prompt, x_sample, c_prompt, c_sample, cache_k, cache_v, state_conv, state_C, state_n, state_m, rel_bias, w_ada, b_ada, norm_mix, norm_ffn, w_in, conv_w, conv_b, gate_b, attn_sinks, m_norm, w_out, peer_query, peer_keys, peer_u, peer_v, norm_final):
    depth = w_ada.shape[0]
    bp, tp, d = x_prompt.shape
    bs, ts, _ = x_sample.shape
    assert tp % WINDOW == 0 and tp % M_CHUNK == 0 and ts <= SAMPLE_PAD and ts >= CONV_W - 1

    mod_all = _ada_call(jnp.concatenate([c_prompt, c_sample], axis=0), w_ada, b_ada)

    qi = np.arange(WINDOW)[:, None]
    bias_p = _bias_call(rel_bias, qi + WINDOW - np.arange(2 * WINDOW)[None, :])
    qs = np.arange(SAMPLE_PAD)[:, None]
    bias_c = _bias_call(rel_bias, qs + WINDOW - np.arange(WINDOW)[None, :])
    bias_n = _bias_call(rel_bias, qs - np.arange(SAMPLE_PAD)[None, :])

    groups = PROMPT_GROUPS if bp % PROMPT_GROUPS == 0 else 1
    bg = bp // groups
    xg = [x_prompt[g * bg:(g + 1) * bg].reshape(bg * tp, d) for g in range(groups)]
    xs = jnp.pad(x_sample, ((0, 0), (0, SAMPLE_PAD - ts), (0, 0))).reshape(bs * SAMPLE_PAD, d)
    halo_pad = ((0, 0), (SUBLANES - (CONV_W - 1), 0), (0, 0))
    zero_state = (jnp.zeros((bg, M_HEADS, M_HEAD_DIM, M_HEAD_DIM), F32),
                  jnp.zeros((bg, M_HEADS, M_HEAD_DIM), F32),
                  jnp.zeros((bg, SUBLANES, LANES), F32))
    zero_conv = jnp.zeros((bg, SUBLANES, 2 * M_WIDTH), F32)

    st_p, st_s = [], []
    for l in range(depth):
        lw = (norm_mix[l], norm_ffn[l], w_in[l], conv_w[l], conv_b[l], gate_b[l], attn_sinks[l], m_norm[l],
              w_out[l], peer_query[l], peer_keys[l], peer_u[l], peer_v[l])
        mod_s = [jnp.repeat(m, SAMPLE_PAD, axis=0) for m in jnp.split(mod_all[l, bp:], 6, axis=-1)]
        sp_groups = []
        for g in range(groups):
            mod_g = [m.reshape(bg, 1, d) for m in jnp.split(mod_all[l, g * bg:(g + 1) * bg], 6, axis=-1)]
            xg[g], sp = _layer(xg[g], mod_g, False, bg, tp, tp, lw, bias_p, None, None, None, zero_conv, zero_state)
            sp_groups.append(sp)
        st_p.append([jnp.concatenate([sp[i] for sp in sp_groups], axis=0) for i in range(6)])
        state_s = (state_C[l].astype(F32), state_n[l].astype(F32),
                   jnp.broadcast_to(jnp.pad(state_m[l].astype(F32), ((0, 0), (0, SUBLANES - M_HEADS)))[:, :, None],
                                    (bs, SUBLANES, LANES)))
        kv_cache = (cache_k[l].reshape(bs, WINDOW, A_KV_WIDTH), cache_v[l].reshape(bs, WINDOW, A_KV_WIDTH))
        xs, ss = _layer(xs, mod_s, True, bs, SAMPLE_PAD, ts, lw, None, bias_c, bias_n, kv_cache,
                        jnp.pad(state_conv[l].astype(F32), halo_pad), state_s)
        st_s.append(ss)

    gfin = norm_final.reshape(1, d)
    y_prompt = jnp.concatenate([_final_call(x, gfin).reshape(bg, tp, d) for x in xg], axis=0)
    y_sample = _final_call(xs, gfin).reshape(bs, SAMPLE_PAD, d)[:, :ts]
    outs_p = [jnp.stack([s[i] for s in st_p]) for i in range(6)]
    outs_s = [jnp.stack([s[i] for s in st_s]) for i in range(6)]
    return (y_prompt, y_sample, *outs_p, *outs_s)
```

```python
import functools
import math

import numpy as np
import jax
import jax.numpy as jnp
from jax import lax
from jax.experimental import pallas as pl
from jax.experimental.pallas import tpu as pltpu
from jax.experimental.pallas import tpu_sc as plsc

F32 = jnp.float32
BF16 = jnp.bfloat16
HIGHEST = lax.Precision.HIGHEST

D_MODEL = 1024
A_HEADS = 8
A_KV_HEADS = 2
A_GROUP = A_HEADS // A_KV_HEADS
A_HEAD_DIM = 64
A_WIDTH = A_HEADS * A_HEAD_DIM
A_KV_WIDTH = A_KV_HEADS * A_HEAD_DIM
WINDOW = 128
ATT_SCALE = A_HEAD_DIM ** -0.5
N_BUCKETS = 32
MAX_DISTANCE = WINDOW
M_HEADS = 4
M_HEAD_DIM = 128
M_WIDTH = M_HEADS * M_HEAD_DIM
CONV_W = 4
M_CHUNK = 64
N_KEYS = 128
P_HEADS = 8
P_TOPK = 16
P_KEY_DIM = 256
P_HALF = P_KEY_DIM // 2
EPS = 1e-6
NEG_INF = -1e30

LANES = 128
SUBLANES = 8
SAMPLE_PAD = SUBLANES
VMEM_LIMIT = 48 * 1024 * 1024
PROMPT_GROUPS = 2

NT_DIMS = (((1,), (1,)), ((), ()))
TN_DIMS = (((0,), (0,)), ((), ()))


def _cparams(*sem):
    return pltpu.CompilerParams(dimension_semantics=sem, vmem_limit_bytes=VMEM_LIMIT)


def _bdot(a, b):
    return jnp.dot(a.astype(BF16), b.astype(BF16), preferred_element_type=F32)


def _bdot_nt(a, b):
    return lax.dot_general(a.astype(BF16), b.astype(BF16), NT_DIMS, preferred_element_type=F32)


def _sigmoid(x):
    return 1.0 / (1.0 + jnp.exp(-x))


def _log_sigmoid(x):
    return jnp.minimum(x, 0.0) - jnp.log1p(jnp.exp(-jnp.abs(x)))


def _gelu_tanh(x):
    c = math.sqrt(2.0 / math.pi)
    return x * (0.5 * (1.0 + jnp.tanh(c * (x + 0.044715 * (x * x * x)))))


def _ada_kernel(c_ref, w_ref, b_ref, o_ref):
    c = c_ref[...]
    s = c * _sigmoid(c)
    o_ref[...] = jnp.dot(s, w_ref[...], precision=HIGHEST, preferred_element_type=F32) + b_ref[...]


def _ada_call(c_all, w_ada, b_ada):
    depth, d, n6 = w_ada.shape
    rows = c_all.shape[0]
    bn = 1024
    return pl.pallas_call(
        _ada_kernel,
        grid=(depth, n6 // bn),
        in_specs=[
            pl.BlockSpec((rows, d), lambda l, j: (0, 0)),
            pl.BlockSpec((None, d, bn), lambda l, j: (l, 0, j)),
            pl.BlockSpec((None, 1, bn), lambda l, j: (l, 0, j)),
        ],
        out_specs=pl.BlockSpec((None, rows, bn), lambda l, j: (l, 0, j)),
        out_shape=jax.ShapeDtypeStruct((depth, rows, n6), F32),
        compiler_params=_cparams("parallel", "parallel"),
        name="ada_mod",
    )(c_all, w_ada, b_ada.reshape(depth, 1, n6))


def _mod_spec(per_token, tm, tokens_per_req):
    if per_token:
        return pl.BlockSpec((tm, D_MODEL), lambda i: (i, 0))
    tiles = tokens_per_req // tm
    return pl.BlockSpec((None, 1, D_MODEL), lambda i: (i // tiles, 0, 0))


def _in_kernel(x_ref, sc_ref, sh_ref, g_ref, wa_ref, wqk_ref, wv_ref, wo_ref, wg_ref, gb_ref,
               qkv_ref, qkm_ref, v_ref, o_ref, gc_ref):
    x = x_ref[...]
    y = x * lax.rsqrt(jnp.mean(x * x, axis=-1, keepdims=True) + EPS) * g_ref[...]
    h = (y * (1.0 + sc_ref[...]) + sh_ref[...]).astype(BF16)
    qkv_ref[...] = jnp.dot(h, wa_ref[...], preferred_element_type=F32)
    qkm_ref[...] = jnp.dot(h, wqk_ref[...], preferred_element_type=F32)
    v_ref[...] = jnp.dot(h, wv_ref[...], preferred_element_type=F32)
    o_ref[...] = jnp.dot(h, wo_ref[...], preferred_element_type=F32)
    g = jnp.dot(h, wg_ref[...], preferred_element_type=F32) + gb_ref[...]
    lane = lax.broadcasted_iota(jnp.int32, g.shape, 1)
    gc_ref[...] = jnp.where(lane < M_HEADS, g, jnp.where(lane < 2 * M_HEADS, _log_sigmoid(g), 0.0))


def _in_call(x, sc, sh, gnorm, wa, wqk, wv, wo, wg, gb, per_token, tokens_per_req):
    n = x.shape[0]
    tm = min(512, n if per_token else tokens_per_req)
    mod = _mod_spec(per_token, tm, tokens_per_req)
    full = lambda shape: pl.BlockSpec(shape, lambda i: (0,) * len(shape))
    row = lambda w: pl.BlockSpec((tm, w), lambda i: (i, 0))
    return pl.pallas_call(
        _in_kernel,
        grid=(n // tm,),
        in_specs=[row(D_MODEL), mod, mod, full((1, D_MODEL)), full(wa.shape), full(wqk.shape),
                  full(wv.shape), full(wo.shape), full(wg.shape), full((1, LANES))],
        out_specs=[row(wa.shape[1]), row(wqk.shape[1]), row(wv.shape[1]), row(wo.shape[1]), row(LANES)],
        out_shape=[jax.ShapeDtypeStruct((n, w), F32)
                   for w in (wa.shape[1], wqk.shape[1], wv.shape[1], wo.shape[1], LANES)],
        compiler_params=_cparams("parallel"),
        name="in_proj",
    )(x, sc, sh, gnorm, wa, wqk, wv, wo, wg, gb)


def _t5_bucket_np(dist):
    n = np.maximum(dist, 0)
    max_exact = N_BUCKETS // 2
    nf = np.maximum(n, 1).astype(np.float64)
    large = max_exact + (np.log(nf / max_exact) / math.log(MAX_DISTANCE / max_exact)
                         * (N_BUCKETS - max_exact)).astype(np.int32)
    return np.where(n < max_exact, n, np.minimum(large, N_BUCKETS - 1)).astype(np.int32)


def _bias_kernel(bucket_ref, rel_ref, o_ref):
    bucket = bucket_ref[...]
    for h in range(A_HEADS):
        acc = jnp.zeros(bucket.shape, F32)
        for b in range(N_BUCKETS):
            acc = jnp.where(bucket == b, rel_ref[b, h], acc)
        o_ref[h] = acc


def _bias_call(rel_bias, dist):
    bucket = jnp.asarray(_t5_bucket_np(dist))
    nq, nk = dist.shape
    return pl.pallas_call(
        _bias_kernel,
        in_specs=[pl.BlockSpec((nq, nk), lambda: (0, 0)),
                  pl.BlockSpec(memory_space=pltpu.SMEM)],
        out_specs=pl.BlockSpec((A_HEADS, nq, nk), lambda: (0, 0, 0)),
        out_shape=jax.ShapeDtypeStruct((A_HEADS, nq, nk), F32),
        name="t5_bias",
    )(bucket, rel_bias)


def _attn_p_kernel(q_ref, kp_ref, kc_ref, vp_ref, vc_ref, bias_ref, sink_ref, o_ref):
    i = pl.program_id(1)
    qi = lax.broadcasted_iota(jnp.int32, (WINDOW, WINDOW), 0)
    kj = lax.broadcasted_iota(jnp.int32, (WINDOW, WINDOW), 1)
    valid_prev = jnp.logical_and(kj > qi, i > 0)
    valid_cur = kj <= qi
    q = q_ref[...]
    for h in range(A_HEADS):
        kv = h // A_GROUP
        qh = q[:, h * A_HEAD_DIM:(h + 1) * A_HEAD_DIM]
        sl = slice(kv * A_HEAD_DIM, (kv + 1) * A_HEAD_DIM)
        bias = bias_ref[h]
        sp = _bdot_nt(qh, kp_ref[:, sl]) * ATT_SCALE + bias[:, :WINDOW]
        sc = _bdot_nt(qh, kc_ref[:, sl]) * ATT_SCALE + bias[:, WINDOW:]
        sp = jnp.where(valid_prev, sp, NEG_INF)
        sc = jnp.where(valid_cur, sc, NEG_INF)
        sink = sink_ref[0, h]
        mx = jnp.maximum(jnp.maximum(jnp.max(sp, axis=-1, keepdims=True),
                                     jnp.max(sc, axis=-1, keepdims=True)), sink)
        ep = jnp.exp(sp - mx)
        ec = jnp.exp(sc - mx)
        den = (jnp.sum(ep, axis=-1, keepdims=True) + jnp.sum(ec, axis=-1, keepdims=True)
               + jnp.exp(sink - mx))
        o = _bdot(ep / den, vp_ref[:, sl]) + _bdot(ec / den, vc_ref[:, sl])
        o_ref[:, h * A_HEAD_DIM:(h + 1) * A_HEAD_DIM] = o


def _attn_p_call(qkv, bias, sinks, batch, seq):
    nb = seq // WINDOW
    n = batch * seq
    kcol = A_WIDTH // A_KV_WIDTH
    vcol = kcol + 1
    cur = lambda col: pl.BlockSpec((WINDOW, A_KV_WIDTH), lambda b, i: (b * nb + i, col))
    prev = lambda col: pl.BlockSpec((WINDOW, A_KV_WIDTH),
                                    lambda b, i: (b * nb + jnp.maximum(i - 1, 0), col))
    return pl.pallas_call(
        _attn_p_kernel,
        grid=(batch, nb),
        in_specs=[pl.BlockSpec((WINDOW, A_WIDTH), lambda b, i: (b * nb + i, 0)),
                  prev(kcol), cur(kcol), prev(vcol), cur(vcol),
                  pl.BlockSpec((A_HEADS, WINDOW, 2 * WINDOW), lambda b, i: (0, 0, 0)),
                  pl.BlockSpec(memory_space=pltpu.SMEM)],
        out_specs=pl.BlockSpec((WINDOW, A_WIDTH), lambda b, i: (b * nb + i, 0)),
        out_shape=jax.ShapeDtypeStruct((n, A_WIDTH), F32),
        compiler_params=_cparams("parallel", "parallel"),
        name="swa_prompt",
    )(qkv, qkv, qkv, qkv, qkv, bias, sinks.reshape(1, A_HEADS))


def _attn_s_kernel(n_new, qkv_ref, ck_ref, cv_ref, bc_ref, bn_ref, sink_ref,
                   o_ref, nk_ref, nv_ref, kk_s, vv_s):
    qkv = qkv_ref[...]
    knew = qkv[:, A_WIDTH:A_WIDTH + A_KV_WIDTH]
    vnew = qkv[:, A_WIDTH + A_KV_WIDTH:A_WIDTH + 2 * A_KV_WIDTH]
    ck = ck_ref[...]
    cv = cv_ref[...]
    qi = lax.broadcasted_iota(jnp.int32, (SAMPLE_PAD, WINDOW), 0)
    kj = lax.broadcasted_iota(jnp.int32, (SAMPLE_PAD, WINDOW), 1)
    valid_c = kj > qi
    qcol = lax.broadcasted_iota(jnp.int32, (SAMPLE_PAD, 1), 0)
    for h in range(A_HEADS):
        kv = h // A_GROUP
        sl = slice(kv * A_HEAD_DIM, (kv + 1) * A_HEAD_DIM)
        qh = qkv[:, h * A_HEAD_DIM:(h + 1) * A_HEAD_DIM]
        s_c = lax.dot_general(qh, ck[:, sl], NT_DIMS, precision=HIGHEST,
                              preferred_element_type=F32) * ATT_SCALE + bc_ref[h]
        s_c = jnp.where(valid_c, s_c, NEG_INF)
        bn = bn_ref[h]
        s_n = []
        for j in range(n_new):
            sj = jnp.sum(qh * knew[j:j + 1, sl], axis=-1, keepdims=True) * ATT_SCALE + bn[:, j:j + 1]
            s_n.append(jnp.where(qcol >= j, sj, NEG_INF))
        sink = sink_ref[0, h]
        mx = jnp.maximum(jnp.max(s_c, axis=-1, keepdims=True), sink)
        for sj in s_n:
            mx = jnp.maximum(mx, sj)
        e_c = jnp.exp(s_c - mx)
        den = jnp.sum(e_c, axis=-1, keepdims=True) + jnp.exp(sink - mx)
        o = jnp.dot(e_c, cv[:, sl], precision=HIGHEST, preferred_element_type=F32)
        for j, sj in enumerate(s_n):
            ej = jnp.exp(sj - mx)
            den = den + ej
            o = o + ej * vnew[j:j + 1, sl]
        o_ref[:, h * A_HEAD_DIM:(h + 1) * A_HEAD_DIM] = o / den
    kk_s[0:WINDOW, :] = ck
    kk_s[WINDOW:WINDOW + SAMPLE_PAD, :] = knew
    vv_s[0:WINDOW, :] = cv
    vv_s[WINDOW:WINDOW + SAMPLE_PAD, :] = vnew
    nk_ref[...] = kk_s[n_new:n_new + WINDOW, :]
    nv_ref[...] = vv_s[n_new:n_new + WINDOW, :]


def _attn_s_call(qkv, ck, cv, bias_c, bias_n, sinks, n_new):
    nreq = ck.shape[0]
    wq = qkv.shape[1]
    full3 = lambda shape: pl.BlockSpec(shape, lambda b: (0, 0, 0))
    cache = pl.BlockSpec((None, WINDOW, A_KV_WIDTH), lambda b: (b, 0, 0))
    return pl.pallas_call(
        functools.partial(_attn_s_kernel, n_new),
        grid=(nreq,),
        in_specs=[pl.BlockSpec((SAMPLE_PAD, wq), lambda b: (b, 0)), cache, cache,
                  full3(bias_c.shape), full3(bias_n.shape),
                  pl.BlockSpec(memory_space=pltpu.SMEM)],
        out_specs=[pl.BlockSpec((SAMPLE_PAD, A_WIDTH), lambda b: (b, 0)), cache, cache],
        out_shape=[jax.ShapeDtypeStruct((nreq * SAMPLE_PAD, A_WIDTH), F32),
                   jax.ShapeDtypeStruct(ck.shape, F32), jax.ShapeDtypeStruct(cv.shape, F32)],
        scratch_shapes=[pltpu.VMEM((WINDOW + SAMPLE_PAD, A_KV_WIDTH), F32),
                        pltpu.VMEM((WINDOW + SAMPLE_PAD, A_KV_WIDTH), F32)],
        compiler_params=_cparams("parallel"),
        name="swa_sample",
    )(qkv, ck, cv, bias_c, bias_n, sinks.reshape(1, A_HEADS))


def _mlstm_kernel(chunk, t_valid, qk_ref, v_ref, og_ref, gc_ref, cw_ref, cb_ref, mn_ref,
                  conv0_ref, c0_ref, n0_ref, m0_ref,
                  out_ref, cout_ref, nout_ref, mout_ref,
                  xp_s, c_s, n_s, m_s):
    step = pl.program_id(1)
    halo = SUBLANES

    @pl.when(step == 0)
    def _():
        xp_s[0:halo, :] = conv0_ref[...]
        c_s[...] = c0_ref[...]
        n_s[...] = n0_ref[...]
        m_s[...] = m0_ref[...]

    xp_s[halo:halo + chunk, :] = qk_ref[...]
    cw = cw_ref[...]
    y = cb_ref[...]
    for i in range(CONV_W):
        off = halo - (CONV_W - 1) + i
        y = y + xp_s[off:off + chunk, :] * cw[i:i + 1, :]
    xp_s[0:halo, :] = xp_s[chunk:chunk + halo, :]
    y = y * _sigmoid(y)
    q_all = y[:, :M_WIDTH]
    k_all = y[:, M_WIDTH:] * (M_HEAD_DIM ** -0.5)

    g = gc_ref[...]
    if t_valid < chunk:
        row = lax.broadcasted_iota(jnp.int32, g.shape, 0)
        lane = lax.broadcasted_iota(jnp.int32, g.shape, 1)
        g = jnp.where(row < t_valid, g, jnp.where(lane < M_HEADS, NEG_INF, 0.0))
    tr = lax.broadcasted_iota(jnp.int32, (chunk, chunk), 0)
    tc = lax.broadcasted_iota(jnp.int32, (chunk, chunk), 1)
    causal = tr >= tc
    tri = causal.astype(F32)
    bcol = jnp.dot(tri, g, precision=HIGHEST, preferred_element_type=F32)
    er = lax.broadcasted_iota(jnp.int32, (SUBLANES, LANES), 0)
    ec = lax.broadcasted_iota(jnp.int32, (SUBLANES, LANES), 1)
    eye = (er == ec).astype(F32)
    g_rows = lax.dot_general(eye, g, NT_DIMS, precision=HIGHEST, preferred_element_type=F32)
    b_rows = lax.dot_general(eye, bcol, NT_DIMS, precision=HIGHEST, preferred_element_type=F32)

    for h in range(M_HEADS):
        hs = slice(h * M_HEAD_DIM, (h + 1) * M_HEAD_DIM)
        b_c = bcol[:, M_HEADS + h:M_HEADS + h + 1]
        ig_c = g[:, h:h + 1]
        b_r = b_rows[M_HEADS + h:M_HEADS + h + 1, :]
        ig_r = g_rows[h:h + 1, :]
        m_prev = m_s[h:h + 1, 0:1]
        logw = jnp.where(causal, b_c - b_r + ig_r, -jnp.inf)
        inter = b_c + m_prev
        m_t = jnp.maximum(inter, jnp.max(logw, axis=-1, keepdims=True))
        w = jnp.exp(logw - m_t)
        a = jnp.exp(inter - m_t)
        q = q_all[:, hs]
        k = k_all[:, hs]
        v = v_ref[:, hs]
        cmat = c_s[h]
        nvec = n_s[h:h + 1, :]
        wqk = w * _bdot_nt(q, k)
        num = _bdot(wqk, v) + a * _bdot_nt(q, cmat)
        den = jnp.sum(wqk, axis=-1, keepdims=True) + a * jnp.sum(q * nvec, axis=-1, keepdims=True)
        hh = num / jnp.maximum(jnp.abs(den), jnp.exp(-m_t))
        m_new = m_t[chunk - 1:chunk, :]
        b_last = b_c[chunk - 1:chunk, :]
        wl = jnp.exp(b_last - b_c + ig_c - m_new)
        al = jnp.exp(b_last + m_prev - m_new)
        c_s[h] = al * cmat + lax.dot_general((v * wl).astype(BF16), k.astype(BF16), TN_DIMS,
                                             preferred_element_type=F32)
        n_s[h:h + 1, :] = al * nvec + jnp.sum(wl * k, axis=0, keepdims=True)
        m_s[h:h + 1, :] = jnp.broadcast_to(m_new, (1, LANES))
        hn = hh * lax.rsqrt(jnp.mean(hh * hh, axis=-1, keepdims=True) + EPS) * mn_ref[:, hs]
        out_ref[:, hs] = _sigmoid(og_ref[:, hs]) * hn

    @pl.when(step == pl.num_programs(1) - 1)
    def _():
        cout_ref[...] = c_s[...]
        nout_ref[...] = n_s[...]
        mout_ref[...] = m_s[...]


def _mlstm_call(qk, v, og, gc, conv_w, conv_b, m_norm, conv0, c0, n0, m0, batch, seq, chunk, t_valid):
    nc = seq // chunk
    n = batch * seq
    row = lambda w: pl.BlockSpec((chunk, w), lambda b, c: (b * nc + c, 0))
    full2 = lambda shape: pl.BlockSpec(shape, lambda b, c: (0, 0))
    per_b = lambda shape: pl.BlockSpec((None,) + shape, lambda b, c: (b,) + (0,) * len(shape))
    dh = M_HEAD_DIM
    return pl.pallas_call(
        functools.partial(_mlstm_kernel, chunk, t_valid),
        grid=(batch, nc),
        in_specs=[row(2 * M_WIDTH), row(M_WIDTH), row(M_WIDTH), row(LANES),
                  full2((CONV_W, 2 * M_WIDTH)), full2((1, 2 * M_WIDTH)), full2((1, M_WIDTH)),
                  per_b((SUBLANES, 2 * M_WIDTH)), per_b((M_HEADS, dh, dh)), per_b((M_HEADS, dh)),
                  per_b((SUBLANES, LANES))],
        out_specs=[row(M_WIDTH), per_b((M_HEADS, dh, dh)), per_b((M_HEADS, dh)), per_b((SUBLANES, LANES))],
        out_shape=[jax.ShapeDtypeStruct((n, M_WIDTH), F32),
                   jax.ShapeDtypeStruct((batch, M_HEADS, dh, dh), F32),
                   jax.ShapeDtypeStruct((batch, M_HEADS, dh), F32),
                   jax.ShapeDtypeStruct((batch, SUBLANES, LANES), F32)],
        scratch_shapes=[pltpu.VMEM((SUBLANES + chunk, 2 * M_WIDTH), F32),
                        pltpu.VMEM((M_HEADS, dh, dh), F32),
                        pltpu.VMEM((M_HEADS, dh), F32),
                        pltpu.VMEM((SUBLANES, LANES), F32)],
        compiler_params=_cparams("parallel", "arbitrary"),
        name="mlstm",
    )(qk, v, og, gc, conv_w, conv_b, m_norm, conv0, c0, n0, m0)


def _out_kernel(att_ref, mo_ref, x_ref, g1_ref, sc_ref, sh_ref, gn_ref, wa_ref, wm_ref, wq_ref,
                xo_ref, h2_ref, qp_ref):
    mix = (jnp.dot(att_ref[...].astype(BF16), wa_ref[...], preferred_element_type=F32)
           + jnp.dot(mo_ref[...].astype(BF16), wm_ref[...], preferred_element_type=F32))
    x = x_ref[...] + g1_ref[...] * mix
    xo_ref[...] = x
    y = x * lax.rsqrt(jnp.mean(x * x, axis=-1, keepdims=True) + EPS) * gn_ref[...]
    h2 = y * (1.0 + sc_ref[...]) + sh_ref[...]
    h2_ref[...] = h2
    qp_ref[...] = jnp.dot(h2.astype(BF16), wq_ref[...], preferred_element_type=F32)


def _out_call(att, mo, x, g1, sc, sh, gnorm, wa, wm, wq, per_token, tokens_per_req):
    n = x.shape[0]
    tm = min(256, n if per_token else tokens_per_req)
    mod = _mod_spec(per_token, tm, tokens_per_req)
    full = lambda shape: pl.BlockSpec(shape, lambda i: (0,) * len(shape))
    row = lambda w: pl.BlockSpec((tm, w), lambda i: (i, 0))
    nq = wq.shape[1]
    return pl.pallas_call(
        _out_kernel,
        grid=(n // tm,),
        in_specs=[row(A_WIDTH), row(M_WIDTH), row(D_MODEL), mod, mod, mod, full((1, D_MODEL)),
                  full(wa.shape), full(wm.shape), full(wq.shape)],
        out_specs=[row(D_MODEL), row(D_MODEL), row(nq)],
        out_shape=[jax.ShapeDtypeStruct((n, D_MODEL), F32), jax.ShapeDtypeStruct((n, D_MODEL), F32),
                   jax.ShapeDtypeStruct((n, nq), F32)],
        compiler_params=_cparams("parallel"),
        name="out_proj",
    )(att, mo, x, g1, sc, sh, gnorm, wa, wm, wq)


def _pk_cells():
    return [(a, b) for a in range(P_TOPK) for b in range(P_TOPK) if (a + 1) * (b + 1) <= P_TOPK]


PK_CELL_ROWS = 64


def _pk_expand_mats():
    cells = _pk_cells()
    e0 = np.zeros((PK_CELL_ROWS, LANES), np.float32)
    e1 = np.zeros((PK_CELL_ROWS, LANES), np.float32)
    for j, (a, b) in enumerate(cells):
        e0[j, a] = 1.0
        e1[j, b] = 1.0
    return e0, e1, len(cells)


def _top_rows(s, rowf, rounds):
    n_rows = s.shape[0]
    vals, idxs = [], []
    for _ in range(rounds):
        m = jnp.max(s, axis=0, keepdims=True)
        i = jnp.min(jnp.where(s == m, rowf, float(n_rows)), axis=0, keepdims=True)
        vals.append(m)
        idxs.append(i)
        s = jnp.where(rowf == i, -jnp.inf, s)
    return jnp.concatenate(vals, axis=0), jnp.concatenate(idxs, axis=0)


def _select_kernel(n_cells, qp_ref, keys_ref, e0_ref, e1_ref, idx_ref, gw_ref, idx_s, gw_s):
    tm = qp_ref.shape[0]
    keyf = lax.broadcasted_iota(jnp.int32, (N_KEYS, tm), 0).astype(F32)
    cellf = lax.broadcasted_iota(jnp.int32, (PK_CELL_ROWS, tm), 0).astype(F32)
    e0 = e0_ref[...]
    e1 = e1_ref[...]
    pad = jnp.zeros((LANES - P_TOPK, tm), F32)

    def head(h, carry):
        h = jnp.asarray(h, jnp.int32)
        sub = []
        for c in range(2):
            col = pl.multiple_of((h * 2 + c) * P_HALF, P_HALF)
            s = _bdot_nt(keys_ref[h, c], qp_ref[:, pl.ds(col, P_HALF)])
            sub.append(_top_rows(s, keyf, P_TOPK))
        (v0, i0), (v1, i1) = sub
        expand = lambda e, x: jnp.dot(e, jnp.concatenate([x, pad], axis=0), precision=HIGHEST,
                                      preferred_element_type=F32)
        cand = expand(e0, v0) + expand(e1, v1)
        cidx = expand(e0, i0 * float(N_KEYS)) + expand(e1, i1)
        cand = jnp.where(cellf < n_cells, cand, -jnp.inf)
        best, eidx = [], []
        for _ in range(P_TOPK):
            m = jnp.max(cand, axis=0, keepdims=True)
            j = jnp.min(jnp.where(cand == m, cellf, float(PK_CELL_ROWS)), axis=0, keepdims=True)
            hit = cellf == j
            eidx.append(jnp.max(jnp.where(hit, cidx, -1.0), axis=0, keepdims=True))
            best.append(m)
            cand = jnp.where(hit, -jnp.inf, cand)
        best = jnp.concatenate(best, axis=0)
        e = jnp.exp(best - best[0:1, :])
        row0 = pl.multiple_of(h * P_TOPK, P_TOPK)
        gw_s[pl.ds(row0, P_TOPK), :] = e / jnp.sum(e, axis=0, keepdims=True)
        idx_s[pl.ds(row0, P_TOPK), :] = jnp.concatenate(eidx, axis=0)
        return carry

    lax.fori_loop(0, P_HEADS, head, 0)
    idx_ref[...] = idx_s[...].T.astype(jnp.int32)
    gw_ref[...] = gw_s[...].T


def _select_call(qp, keys_bf16):
    n = qp.shape[0]
    tm = min(LANES, n)
    e0, e1, n_cells = _pk_expand_mats()
    full = lambda shape: pl.BlockSpec(shape, lambda i: (0,) * len(shape))
    return pl.pallas_call(
        functools.partial(_select_kernel, n_cells),
        grid=(n // tm,),
        in_specs=[pl.BlockSpec((tm, qp.shape[1]), lambda i: (i, 0)), full(keys_bf16.shape),
                  full((PK_CELL_ROWS, LANES)), full((PK_CELL_ROWS, LANES))],
        out_specs=[pl.BlockSpec((tm, LANES), lambda i: (i, 0)), pl.BlockSpec((tm, LANES), lambda i: (i, 0))],
        out_shape=[jax.ShapeDtypeStruct((n, LANES), jnp.int32), jax.ShapeDtypeStruct((n, LANES), F32)],
        scratch_shapes=[pltpu.VMEM((P_HEADS * P_TOPK, tm), F32), pltpu.VMEM((P_HEADS * P_TOPK, tm), F32)],
        compiler_params=_cparams("parallel"),
        name="peer_select",
    )(qp, keys_bf16, jnp.asarray(e0), jnp.asarray(e1))


N_SEL = P_HEADS * P_TOPK

SC_CORES = 2
SC_SUBCORES = 16
SC_LANES = 16
SC_TOK_BLOCK = 8
SC_ROWS = 32
SC_ROW_BLOCK = 16
SC_ACC_CHAINS = 4
SC_ACC_ROWS = 16
N_WCOL = D_MODEL // (2 * SC_LANES)


def _pack_table(t):
    e, dcol = t.shape
    tb = t.astype(BF16).reshape(e, dcol // (2 * SC_LANES), 2, SC_LANES)
    words = lax.bitcast_convert_type(jnp.swapaxes(tb, 2, 3), jnp.uint32)
    return lax.bitcast_convert_type(words, jnp.int32).reshape(e, dcol // 2)


def _sc_gelu(x):
    z = math.sqrt(2.0 / math.pi) * (x + 0.044715 * (x * x * x))
    t = 1.0 - 2.0 / (jnp.exp(2.0 * z) + 1.0)
    return x * (0.5 * (1.0 + t))


def _sc_expert_body(tokens_per_worker, idx_hbm, gw_hbm, h2_hbm, u_hbm, v_hbm, y_hbm,
                    idx_v, gw_v, x_v, o_v, buf, coef_v, tr_v, sem):
    wid = lax.axis_index("s") * SC_CORES + lax.axis_index("c")
    base = wid * tokens_per_worker
    lane = lax.iota(jnp.int32, SC_LANES)
    n_gather = N_SEL // SC_ROWS
    zero = jnp.zeros((SC_LANES,), F32)

    def gather(tt, i):
        table = u_hbm if i < n_gather else v_hbm
        j = i % n_gather
        slot = i % 2
        return pltpu.make_async_copy(table.at[idx_v.at[tt, pl.ds(j * SC_ROWS, SC_ROWS)]], buf.at[slot],
                                     sem.at[slot])

    def unpack(w):
        lo = lax.bitcast_convert_type(lax.shift_left(w, jnp.full((SC_LANES,), 16, jnp.int32)), F32)
        hi = lax.bitcast_convert_type(w & jnp.full((SC_LANES,), -65536, jnp.int32), F32)
        return lo, hi

    def act_chunk(tt, j, slot):
        @pl.loop(0, SC_ROWS // SC_LANES)
        def _(half):
            for rb in range(SC_LANES // SC_ROW_BLOCK):
                r0 = half * SC_LANES + rb * SC_ROW_BLOCK

                def col(c, accs):
                    woff = pl.multiple_of(c * SC_LANES, SC_LANES)
                    xoff = pl.multiple_of(c * (2 * SC_LANES), 2 * SC_LANES)
                    xa = x_v[tt, pl.ds(xoff, SC_LANES)]
                    xb = x_v[tt, pl.ds(xoff + SC_LANES, SC_LANES)]
                    out = []
                    for r, a in enumerate(accs):
                        lo, hi = unpack(buf[slot, r0 + r, pl.ds(woff, SC_LANES)])
                        out.append(a + lo * xa + hi * xb)
                    return tuple(out)

                accs = lax.fori_loop(0, N_WCOL, col, (zero,) * SC_ROW_BLOCK)
                for r in range(SC_ROW_BLOCK):
                    tr_v[pl.ds((rb * SC_ROW_BLOCK + r) * SC_LANES, SC_LANES)] = accs[r]
            tot = zero
            for jj in range(SC_LANES):
                tot = tot + plsc.load_gather(tr_v, [lane * SC_LANES + jj])
            k0 = pl.multiple_of(j * SC_ROWS + half * SC_LANES, SC_LANES)
            coef_v[pl.ds(k0, SC_LANES)] = gw_v[tt, pl.ds(k0, SC_LANES)] * _sc_gelu(tot)

    def acc_chunk(tt, j, slot, first):
        def tree_sum(parts):
            while len(parts) > 1:
                parts = [parts[i] + parts[i + 1] for i in range(0, len(parts), 2)]
            return parts[0]

        for rb in range(SC_ROWS // SC_ACC_ROWS):
            rows = range(rb * SC_ACC_ROWS, (rb + 1) * SC_ACC_ROWS)
            splat = {r: plsc.load_gather(coef_v, [jnp.full((SC_LANES,), j * SC_ROWS + r, jnp.int32)])
                     for r in rows}
            fresh = first and rb == 0

            @plsc.parallel_loop(0, N_WCOL)
            def _(c):
                woff = pl.multiple_of(c * SC_LANES, SC_LANES)
                xoff = pl.multiple_of(c * (2 * SC_LANES), 2 * SC_LANES)
                pa, pb = [], []
                for n, r in enumerate(rows):
                    lo, hi = unpack(buf[slot, r, pl.ds(woff, SC_LANES)])
                    if n < SC_ACC_CHAINS:
                        pa.append(splat[r] * lo)
                        pb.append(splat[r] * hi)
                    else:
                        pa[n % SC_ACC_CHAINS] = pa[n % SC_ACC_CHAINS] + splat[r] * lo
                        pb[n % SC_ACC_CHAINS] = pb[n % SC_ACC_CHAINS] + splat[r] * hi
                sa, sb = tree_sum(pa), tree_sum(pb)
                if not fresh:
                    sa = sa + o_v[tt, pl.ds(xoff, SC_LANES)]
                    sb = sb + o_v[tt, pl.ds(xoff + SC_LANES, SC_LANES)]
                o_v[tt, pl.ds(xoff, SC_LANES)] = sa
                o_v[tt, pl.ds(xoff + SC_LANES, SC_LANES)] = sb

    @pl.loop(0, tokens_per_worker // SC_TOK_BLOCK)
    def _(blk):
        t0 = pl.multiple_of(base + blk * SC_TOK_BLOCK, SC_TOK_BLOCK)
        pltpu.sync_copy(idx_hbm.at[pl.ds(t0, SC_TOK_BLOCK)], idx_v)
        pltpu.sync_copy(gw_hbm.at[pl.ds(t0, SC_TOK_BLOCK)], gw_v)
        pltpu.sync_copy(h2_hbm.at[pl.ds(t0, SC_TOK_BLOCK)], x_v)

        gather(0, 0).start()

        @pl.loop(0, SC_TOK_BLOCK)
        def _(tt):
            n_steps = 2 * n_gather
            for i in range(n_steps):
                if i + 1 < n_steps:
                    gather(tt, i + 1).start()
                else:
                    @pl.when(tt + 1 < SC_TOK_BLOCK)
                    def _():
                        gather(tt + 1, 0).start()
                gather(tt, i).wait()
                if i < n_gather:
                    act_chunk(tt, i, i % 2)
                else:
                    acc_chunk(tt, i - n_gather, i % 2, i == n_gather)

        pltpu.sync_copy(o_v, y_hbm.at[pl.ds(t0, SC_TOK_BLOCK)])


def _sc_expert_call(idx, gw, h2, u, v):
    n = idx.shape[0]
    workers = SC_CORES * SC_SUBCORES
    assert n % (workers * SC_TOK_BLOCK) == 0
    mesh = plsc.VectorSubcoreMesh(core_axis_name="c", subcore_axis_name="s")
    return pl.kernel(
        functools.partial(_sc_expert_body, n // workers),
        out_type=jax.ShapeDtypeStruct((n, D_MODEL), F32),
        mesh=mesh,
        scratch_types=[pltpu.VMEM((SC_TOK_BLOCK, N_SEL), jnp.int32),
                       pltpu.VMEM((SC_TOK_BLOCK, N_SEL), F32),
                       pltpu.VMEM((SC_TOK_BLOCK, D_MODEL), F32),
                       pltpu.VMEM((SC_TOK_BLOCK, D_MODEL), F32),
                       pltpu.VMEM((2, SC_ROWS, D_MODEL // 2), jnp.int32),
                       pltpu.VMEM((N_SEL,), F32),
                       pltpu.VMEM((SC_LANES * SC_LANES,), F32),
                       pltpu.SemaphoreType.DMA((2,))],
        compiler_params=pltpu.CompilerParams(needs_layout_passes=False),
        name="peer_experts_sc",
    )(idx, gw, h2, u, v)


def _resid_kernel(x_ref, y_ref, g_ref, o_ref):
    o_ref[...] = x_ref[...] + g_ref[...] * y_ref[...]


def _resid_call(x, y, g2, per_token, tokens_per_req):
    n = x.shape[0]
    tm = min(512, n if per_token else tokens_per_req)
    row = pl.BlockSpec((tm, D_MODEL), lambda i: (i, 0))
    return pl.pallas_call(
        _resid_kernel,
        grid=(n // tm,),
        in_specs=[row, row, _mod_spec(per_token, tm, tokens_per_req)],
        out_specs=row,
        out_shape=jax.ShapeDtypeStruct((n, D_MODEL), F32),
        compiler_params=_cparams("parallel"),
        name="peer_residual",
    )(x, y, g2)


def _final_kernel(x_ref, g_ref, o_ref):
    x = x_ref[...]
    o_ref[...] = x * lax.rsqrt(jnp.mean(x * x, axis=-1, keepdims=True) + EPS) * g_ref[...]


def _final_call(x, g):
    n = x.shape[0]
    tm = min(512, n)
    return pl.pallas_call(
        _final_kernel,
        grid=(n // tm,),
        in_specs=[pl.BlockSpec((tm, D_MODEL), lambda i: (i, 0)), pl.BlockSpec((1, D_MODEL), lambda i: (0, 0))],
        out_specs=pl.BlockSpec((tm, D_MODEL), lambda i: (i, 0)),
        out_shape=jax.ShapeDtypeStruct((n, D_MODEL), F32),
        compiler_params=_cparams("parallel"),
        name="final_norm",
    )(x, g)


def _split_w_in(w_in_l, gate_b_l):
    cuts = np.cumsum([A_WIDTH + 2 * A_KV_WIDTH, 2 * M_WIDTH, M_WIDTH, M_WIDTH]).tolist()
    wa = w_in_l[:, :cuts[0]].astype(BF16)
    wqk = w_in_l[:, cuts[0]:cuts[1]].astype(BF16)
    wv = w_in_l[:, cuts[1]:cuts[2]].astype(BF16)
    wo = w_in_l[:, cuts[2]:cuts[3]].astype(BF16)
    ng = 2 * M_HEADS
    wg = jnp.pad(w_in_l[:, cuts[3]:], ((0, 0), (0, LANES - ng))).astype(BF16)
    gb = jnp.pad(gate_b_l.astype(F32), (0, LANES - ng)).reshape(1, LANES)
    return wa, wqk, wv, wo, wg, gb


def _layer(x, mods, per_token, batch, seq, t_valid, lw, bias_p, bias_c, bias_n, kv_cache, conv0, state):
    (norm_mix, norm_ffn, w_in, conv_w, conv_b, gate_b, sinks, m_norm, w_out, peer_query, peer_keys,
     peer_u, peer_v) = lw
    sh1, sc1, g1, sh2, sc2, g2 = mods
    wa, wqk, wv, wo, wg, gb = _split_w_in(w_in, gate_b)
    qkv, qkm, vm, om, gc = _in_call(x, sc1, sh1, norm_mix.reshape(1, -1), wa, wqk, wv, wo, wg, gb,
                                    per_token, seq)
    if kv_cache is None:
        att = _attn_p_call(qkv, bias_p, sinks, batch, seq)
        kv3 = qkv.reshape(batch, seq, -1)
        new_k = kv3[:, seq - WINDOW:, A_WIDTH:A_WIDTH + A_KV_WIDTH]
        new_v = kv3[:, seq - WINDOW:, A_WIDTH + A_KV_WIDTH:]
        chunk = M_CHUNK
    else:
        att, new_k, new_v = _attn_s_call(qkv, kv_cache[0], kv_cache[1], bias_c, bias_n, sinks, t_valid)
        chunk = seq
    c0, n0, m0 = state
    mo, c_new, n_new, m_new = _mlstm_call(qkm, vm, om, gc, conv_w, conv_b.reshape(1, -1),
                                          m_norm.reshape(1, -1), conv0, c0, n0, m0,
                                          batch, seq, chunk, min(t_valid, chunk))
    new_conv = qkm.reshape(batch, seq, -1)[:, t_valid - (CONV_W - 1):t_valid]
    x_mid, h2, qp = _out_call(att, mo, x, g1, sc2, sh2, norm_ffn.reshape(1, -1),
                              w_out[:A_WIDTH].astype(BF16), w_out[A_WIDTH:].astype(BF16),
                              peer_query.astype(BF16), per_token, seq)
    idx, gw = _select_call(qp, peer_keys.astype(BF16))
    y = _sc_expert_call(idx, gw, h2, peer_u, peer_v)
    x_new = _resid_call(x_mid, y, g2, per_token, seq)
    new_k = new_k.reshape(batch, WINDOW, A_KV_HEADS, A_HEAD_DIM)
    new_v = new_v.reshape(batch, WINDOW, A_KV_HEADS, A_HEAD_DIM)
    return x_new, (new_k, new_v, new_conv, c_new, n_new, m_new[:, :M_HEADS, 0])


def kernel(x_prompt, x_sample, c_prompt, c_sample, cache_k, cache_v, state_conv, state_C, state_n, state_m, rel_bias, w_ada, b_ada, norm_mix, norm_ffn, w_in, conv_w, conv_b, gate_b, attn_sinks, m_norm, w_out, peer_query, peer_keys, peer_u, peer_v, norm_final):
    depth = w_ada.shape[0]
    bp, tp, d = x_prompt.shape
    bs, ts, _ = x_sample.shape
    assert tp % WINDOW == 0 and tp % M_CHUNK == 0 and ts <= SAMPLE_PAD and ts >= CONV_W - 1

    mod_all = _ada_call(jnp.concatenate([c_prompt, c_sample], axis=0), w_ada, b_ada)

    qi = np.arange(WINDOW)[:, None]
    bias_p = _bias_call(rel_bias, qi + WINDOW - np.arange(2 * WINDOW)[None, :])
    qs = np.arange(SAMPLE_PAD)[:, None]
    bias_c = _bias_call(rel_bias, qs + WINDOW - np.arange(WINDOW)[None, :])
    bias_n = _bias_call(rel_bias, qs - np.arange(SAMPLE_PAD)[None, :])

    groups = PROMPT_GROUPS if bp % PROMPT_GROUPS == 0 else 1
    bg = bp // groups
    xg = [x_prompt[g * bg:(g + 1) * bg].reshape(bg * tp, d) for g in range(groups)]
    xs = jnp.pad(x_sample, ((0, 0), (0, SAMPLE_PAD - ts), (0, 0))).reshape(bs * SAMPLE_PAD, d)
    halo_pad = ((0, 0), (SUBLANES - (CONV_W - 1), 0), (0, 0))
    zero_state = (jnp.zeros((bg, M_HEADS, M_HEAD_DIM, M_HEAD_DIM), F32),
                  jnp.zeros((bg, M_HEADS, M_HEAD_DIM), F32),
                  jnp.zeros((bg, SUBLANES, LANES), F32))
    zero_conv = jnp.zeros((bg, SUBLANES, 2 * M_WIDTH), F32)

    st_p, st_s = [], []
    for l in range(depth):
        lw = (norm_mix[l], norm_ffn[l], w_in[l], conv_w[l], conv_b[l], gate_b[l], attn_sinks[l], m_norm[l],
              w_out[l], peer_query[l], peer_keys[l], _pack_table(peer_u[l]), _pack_table(peer_v[l]))
        mod_s = [jnp.repeat(m, SAMPLE_PAD, axis=0) for m in jnp.split(mod_all[l, bp:], 6, axis=-1)]
        sp_groups = []
        for g in range(groups):
            mod_g = [m.reshape(bg, 1, d) for m in jnp.split(mod_all[l, g * bg:(g + 1) * bg], 6, axis=-1)]
            xg[g], sp = _layer(xg[g], mod_g, False, bg, tp, tp, lw, bias_p, None, None, None, zero_conv, zero_state)
            sp_groups.append(sp)
        st_p.append([jnp.concatenate([sp[i] for sp in sp_groups], axis=0) for i in range(6)])
        state_s = (state_C[l].astype(F32), state_n[l].astype(F32),
                   jnp.broadcast_to(jnp.pad(state_m[l].astype(F32), ((0, 0), (0, SUBLANES - M_HEADS)))[:, :, None],
                                    (bs, SUBLANES, LANES)))
        kv_cache = (cache_k[l].reshape(bs, WINDOW, A_KV_WIDTH), cache_v[l].reshape(bs, WINDOW, A_KV_WIDTH))
        xs, ss = _layer(xs, mod_s, True, bs, SAMPLE_PAD, ts, lw, None, bias_c, bias_n, kv_cache,
                        jnp.pad(state_conv[l].astype(F32), halo_pad), state_s)
        st_s.append(ss)

    gfin = norm_final.reshape(1, d)
    y_prompt = jnp.concatenate([_final_call(x, gfin).reshape(bg, tp, d) for x in xg], axis=0)
    y_sample = _final_call(xs, gfin).reshape(bs, SAMPLE_PAD, d)[:, :ts]
    outs_p = [jnp.stack([s[i] for s in st_p]) for i in range(6)]
    outs_s = [jnp.stack([s[i] for s in st_s]) for i in range(6)]
    return (y_prompt, y_sample, *outs_p, *outs_s)
```

```python
import functools
import math

import numpy as np
import jax
import jax.numpy as jnp
from jax import lax
from jax.experimental import pallas as pl
from jax.experimental.pallas import tpu as pltpu
from jax.experimental.pallas import tpu_sc as plsc

F32 = jnp.float32
BF16 = jnp.bfloat16
HIGHEST = lax.Precision.HIGHEST

D_MODEL = 1024
A_HEADS = 8
A_KV_HEADS = 2
A_GROUP = A_HEADS // A_KV_HEADS
A_HEAD_DIM = 64
A_WIDTH = A_HEADS * A_HEAD_DIM
A_KV_WIDTH = A_KV_HEADS * A_HEAD_DIM
WINDOW = 128
ATT_SCALE = A_HEAD_DIM ** -0.5
N_BUCKETS = 32
MAX_DISTANCE = WINDOW
M_HEADS = 4
M_HEAD_DIM = 128
M_WIDTH = M_HEADS * M_HEAD_DIM
CONV_W = 4
M_CHUNK = 64
N_KEYS = 128
P_HEADS = 8
P_TOPK = 16
P_KEY_DIM = 256
P_HALF = P_KEY_DIM // 2
EPS = 1e-6
NEG_INF = -1e30

LANES = 128
SUBLANES = 8
SAMPLE_PAD = SUBLANES
VMEM_LIMIT = 48 * 1024 * 1024
PROMPT_GROUPS = 4

NT_DIMS = (((1,), (1,)), ((), ()))
TN_DIMS = (((0,), (0,)), ((), ()))


def _cparams(*sem):
    return pltpu.CompilerParams(dimension_semantics=sem, vmem_limit_bytes=VMEM_LIMIT)


def _bdot(a, b):
    return jnp.dot(a.astype(BF16), b.astype(BF16), preferred_element_type=F32)


def _bdot_nt(a, b):
    return lax.dot_general(a.astype(BF16), b.astype(BF16), NT_DIMS, preferred_element_type=F32)


def _sigmoid(x):
    return 1.0 / (1.0 + jnp.exp(-x))


def _log_sigmoid(x):
    return jnp.minimum(x, 0.0) - jnp.log1p(jnp.exp(-jnp.abs(x)))


def _gelu_tanh(x):
    c = math.sqrt(2.0 / math.pi)
    return x * (0.5 * (1.0 + jnp.tanh(c * (x + 0.044715 * (x * x * x)))))


def _ada_kernel(c_ref, w_ref, b_ref, o_ref):
    c = c_ref[...]
    s = c * _sigmoid(c)
    o_ref[...] = jnp.dot(s, w_ref[...], precision=HIGHEST, preferred_element_type=F32) + b_ref[...]


def _ada_call(c_all, w_ada, b_ada):
    depth, d, n6 = w_ada.shape
    rows = c_all.shape[0]
    bn = 1024
    return pl.pallas_call(
        _ada_kernel,
        grid=(depth, n6 // bn),
        in_specs=[
            pl.BlockSpec((rows, d), lambda l, j: (0, 0)),
            pl.BlockSpec((None, d, bn), lambda l, j: (l, 0, j)),
            pl.BlockSpec((None, 1, bn), lambda l, j: (l, 0, j)),
        ],
        out_specs=pl.BlockSpec((None, rows, bn), lambda l, j: (l, 0, j)),
        out_shape=jax.ShapeDtypeStruct((depth, rows, n6), F32),
        compiler_params=_cparams("parallel", "parallel"),
        name="ada_mod",
    )(c_all, w_ada, b_ada.reshape(depth, 1, n6))


def _mod_spec(per_token, tm, tokens_per_req):
    if per_token:
        return pl.BlockSpec((tm, D_MODEL), lambda i: (i, 0))
    tiles = tokens_per_req // tm
    return pl.BlockSpec((None, 1, D_MODEL), lambda i: (i // tiles, 0, 0))


def _in_kernel(x_ref, sc_ref, sh_ref, g_ref, wa_ref, wqk_ref, wv_ref, wo_ref, wg_ref, gb_ref,
               qkv_ref, qkm_ref, v_ref, o_ref, gc_ref):
    x = x_ref[...]
    y = x * lax.rsqrt(jnp.mean(x * x, axis=-1, keepdims=True) + EPS) * g_ref[...]
    h = (y * (1.0 + sc_ref[...]) + sh_ref[...]).astype(BF16)
    qkv_ref[...] = jnp.dot(h, wa_ref[...], preferred_element_type=F32)
    qkm_ref[...] = jnp.dot(h, wqk_ref[...], preferred_element_type=F32)
    v_ref[...] = jnp.dot(h, wv_ref[...], preferred_element_type=F32)
    o_ref[...] = jnp.dot(h, wo_ref[...], preferred_element_type=F32)
    g = jnp.dot(h, wg_ref[...], preferred_element_type=F32) + gb_ref[...]
    lane = lax.broadcasted_iota(jnp.int32, g.shape, 1)
    gc_ref[...] = jnp.where(lane < M_HEADS, g, jnp.where(lane < 2 * M_HEADS, _log_sigmoid(g), 0.0))


def _in_call(x, sc, sh, gnorm, wa, wqk, wv, wo, wg, gb, per_token, tokens_per_req):
    n = x.shape[0]
    tm = min(512, n if per_token else tokens_per_req)
    mod = _mod_spec(per_token, tm, tokens_per_req)
    full = lambda shape: pl.BlockSpec(shape, lambda i: (0,) * len(shape))
    row = lambda w: pl.BlockSpec((tm, w), lambda i: (i, 0))
    return pl.pallas_call(
        _in_kernel,
        grid=(n // tm,),
        in_specs=[row(D_MODEL), mod, mod, full((1, D_MODEL)), full(wa.shape), full(wqk.shape),
                  full(wv.shape), full(wo.shape), full(wg.shape), full((1, LANES))],
        out_specs=[row(wa.shape[1]), row(wqk.shape[1]), row(wv.shape[1]), row(wo.shape[1]), row(LANES)],
        out_shape=[jax.ShapeDtypeStruct((n, w), F32)
                   for w in (wa.shape[1], wqk.shape[1], wv.shape[1], wo.shape[1], LANES)],
        compiler_params=_cparams("parallel"),
        name="in_proj",
    )(x, sc, sh, gnorm, wa, wqk, wv, wo, wg, gb)


def _t5_bucket_np(dist):
    n = np.maximum(dist, 0)
    max_exact = N_BUCKETS // 2
    nf = np.maximum(n, 1).astype(np.float64)
    large = max_exact + (np.log(nf / max_exact) / math.log(MAX_DISTANCE / max_exact)
                         * (N_BUCKETS - max_exact)).astype(np.int32)
    return np.where(n < max_exact, n, np.minimum(large, N_BUCKETS - 1)).astype(np.int32)


def _bias_kernel(bucket_ref, rel_ref, o_ref):
    bucket = bucket_ref[...]
    for h in range(A_HEADS):
        acc = jnp.zeros(bucket.shape, F32)
        for b in range(N_BUCKETS):
            acc = jnp.where(bucket == b, rel_ref[b, h], acc)
        o_ref[h] = acc


def _bias_call(rel_bias, dist):
    bucket = jnp.asarray(_t5_bucket_np(dist))
    nq, nk = dist.shape
    return pl.pallas_call(
        _bias_kernel,
        in_specs=[pl.BlockSpec((nq, nk), lambda: (0, 0)),
                  pl.BlockSpec(memory_space=pltpu.SMEM)],
        out_specs=pl.BlockSpec((A_HEADS, nq, nk), lambda: (0, 0, 0)),
        out_shape=jax.ShapeDtypeStruct((A_HEADS, nq, nk), F32),
        name="t5_bias",
    )(bucket, rel_bias)


def _attn_p_kernel(q_ref, kp_ref, kc_ref, vp_ref, vc_ref, bias_ref, sink_ref, o_ref):
    i = pl.program_id(1)
    qi = lax.broadcasted_iota(jnp.int32, (WINDOW, WINDOW), 0)
    kj = lax.broadcasted_iota(jnp.int32, (WINDOW, WINDOW), 1)
    valid_prev = jnp.logical_and(kj > qi, i > 0)
    valid_cur = kj <= qi
    q = q_ref[...]
    for h in range(A_HEADS):
        kv = h // A_GROUP
        qh = q[:, h * A_HEAD_DIM:(h + 1) * A_HEAD_DIM]
        sl = slice(kv * A_HEAD_DIM, (kv + 1) * A_HEAD_DIM)
        bias = bias_ref[h]
        sp = _bdot_nt(qh, kp_ref[:, sl]) * ATT_SCALE + bias[:, :WINDOW]
        sc = _bdot_nt(qh, kc_ref[:, sl]) * ATT_SCALE + bias[:, WINDOW:]
        sp = jnp.where(valid_prev, sp, NEG_INF)
        sc = jnp.where(valid_cur, sc, NEG_INF)
        sink = sink_ref[0, h]
        mx = jnp.maximum(jnp.maximum(jnp.max(sp, axis=-1, keepdims=True),
                                     jnp.max(sc, axis=-1, keepdims=True)), sink)
        ep = jnp.exp(sp - mx)
        ec = jnp.exp(sc - mx)
        den = (jnp.sum(ep, axis=-1, keepdims=True) + jnp.sum(ec, axis=-1, keepdims=True)
               + jnp.exp(sink - mx))
        o = _bdot(ep / den, vp_ref[:, sl]) + _bdot(ec / den, vc_ref[:, sl])
        o_ref[:, h * A_HEAD_DIM:(h + 1) * A_HEAD_DIM] = o


def _attn_p_call(qkv, bias, sinks, batch, seq):
    nb = seq // WINDOW
    n = batch * seq
    kcol = A_WIDTH // A_KV_WIDTH
    vcol = kcol + 1
    cur = lambda col: pl.BlockSpec((WINDOW, A_KV_WIDTH), lambda b, i: (b * nb + i, col))
    prev = lambda col: pl.BlockSpec((WINDOW, A_KV_WIDTH),
                                    lambda b, i: (b * nb + jnp.maximum(i - 1, 0), col))
    return pl.pallas_call(
        _attn_p_kernel,
        grid=(batch, nb),
        in_specs=[pl.BlockSpec((WINDOW, A_WIDTH), lambda b, i: (b * nb + i, 0)),
                  prev(kcol), cur(kcol), prev(vcol), cur(vcol),
                  pl.BlockSpec((A_HEADS, WINDOW, 2 * WINDOW), lambda b, i: (0, 0, 0)),
                  pl.BlockSpec(memory_space=pltpu.SMEM)],
        out_specs=pl.BlockSpec((WINDOW, A_WIDTH), lambda b, i: (b * nb + i, 0)),
        out_shape=jax.ShapeDtypeStruct((n, A_WIDTH), F32),
        compiler_params=_cparams("parallel", "parallel"),
        name="swa_prompt",
    )(qkv, qkv, qkv, qkv, qkv, bias, sinks.reshape(1, A_HEADS))


def _attn_s_kernel(n_new, qkv_ref, ck_ref, cv_ref, bc_ref, bn_ref, sink_ref,
                   o_ref, nk_ref, nv_ref, kk_s, vv_s):
    qkv = qkv_ref[...]
    knew = qkv[:, A_WIDTH:A_WIDTH + A_KV_WIDTH]
    vnew = qkv[:, A_WIDTH + A_KV_WIDTH:A_WIDTH + 2 * A_KV_WIDTH]
    ck = ck_ref[...]
    cv = cv_ref[...]
    qi = lax.broadcasted_iota(jnp.int32, (SAMPLE_PAD, WINDOW), 0)
    kj = lax.broadcasted_iota(jnp.int32, (SAMPLE_PAD, WINDOW), 1)
    valid_c = kj > qi
    qcol = lax.broadcasted_iota(jnp.int32, (SAMPLE_PAD, 1), 0)
    for h in range(A_HEADS):
        kv = h // A_GROUP
        sl = slice(kv * A_HEAD_DIM, (kv + 1) * A_HEAD_DIM)
        qh = qkv[:, h * A_HEAD_DIM:(h + 1) * A_HEAD_DIM]
        s_c = lax.dot_general(qh, ck[:, sl], NT_DIMS, precision=HIGHEST,
                              preferred_element_type=F32) * ATT_SCALE + bc_ref[h]
        s_c = jnp.where(valid_c, s_c, NEG_INF)
        bn = bn_ref[h]
        s_n = []
        for j in range(n_new):
            sj = jnp.sum(qh * knew[j:j + 1, sl], axis=-1, keepdims=True) * ATT_SCALE + bn[:, j:j + 1]
            s_n.append(jnp.where(qcol >= j, sj, NEG_INF))
        sink = sink_ref[0, h]
        mx = jnp.maximum(jnp.max(s_c, axis=-1, keepdims=True), sink)
        for sj in s_n:
            mx = jnp.maximum(mx, sj)
        e_c = jnp.exp(s_c - mx)
        den = jnp.sum(e_c, axis=-1, keepdims=True) + jnp.exp(sink - mx)
        o = jnp.dot(e_c, cv[:, sl], precision=HIGHEST, preferred_element_type=F32)
        for j, sj in enumerate(s_n):
            ej = jnp.exp(sj - mx)
            den = den + ej
            o = o + ej * vnew[j:j + 1, sl]
        o_ref[:, h * A_HEAD_DIM:(h + 1) * A_HEAD_DIM] = o / den
    kk_s[0:WINDOW, :] = ck
    kk_s[WINDOW:WINDOW + SAMPLE_PAD, :] = knew
    vv_s[0:WINDOW, :] = cv
    vv_s[WINDOW:WINDOW + SAMPLE_PAD, :] = vnew
    nk_ref[...] = kk_s[n_new:n_new + WINDOW, :]
    nv_ref[...] = vv_s[n_new:n_new + WINDOW, :]


def _attn_s_call(qkv, ck, cv, bias_c, bias_n, sinks, n_new):
    nreq = ck.shape[0]
    wq = qkv.shape[1]
    full3 = lambda shape: pl.BlockSpec(shape, lambda b: (0, 0, 0))
    cache = pl.BlockSpec((None, WINDOW, A_KV_WIDTH), lambda b: (b, 0, 0))
    return pl.pallas_call(
        functools.partial(_attn_s_kernel, n_new),
        grid=(nreq,),
        in_specs=[pl.BlockSpec((SAMPLE_PAD, wq), lambda b: (b, 0)), cache, cache,
                  full3(bias_c.shape), full3(bias_n.shape),
                  pl.BlockSpec(memory_space=pltpu.SMEM)],
        out_specs=[pl.BlockSpec((SAMPLE_PAD, A_WIDTH), lambda b: (b, 0)), cache, cache],
        out_shape=[jax.ShapeDtypeStruct((nreq * SAMPLE_PAD, A_WIDTH), F32),
                   jax.ShapeDtypeStruct(ck.shape, F32), jax.ShapeDtypeStruct(cv.shape, F32)],
        scratch_shapes=[pltpu.VMEM((WINDOW + SAMPLE_PAD, A_KV_WIDTH), F32),
                        pltpu.VMEM((WINDOW + SAMPLE_PAD, A_KV_WIDTH), F32)],
        compiler_params=_cparams("parallel"),
        name="swa_sample",
    )(qkv, ck, cv, bias_c, bias_n, sinks.reshape(1, A_HEADS))


def _mlstm_kernel(chunk, t_valid, qk_ref, v_ref, og_ref, gc_ref, cw_ref, cb_ref, mn_ref,
                  conv0_ref, c0_ref, n0_ref, m0_ref,
                  out_ref, cout_ref, nout_ref, mout_ref,
                  xp_s, c_s, n_s, m_s):
    step = pl.program_id(1)
    halo = SUBLANES

    @pl.when(step == 0)
    def _():
        xp_s[0:halo, :] = conv0_ref[...]
        c_s[...] = c0_ref[...]
        n_s[...] = n0_ref[...]
        m_s[...] = m0_ref[...]

    xp_s[halo:halo + chunk, :] = qk_ref[...]
    cw = cw_ref[...]
    y = cb_ref[...]
    for i in range(CONV_W):
        off = halo - (CONV_W - 1) + i
        y = y + xp_s[off:off + chunk, :] * cw[i:i + 1, :]
    xp_s[0:halo, :] = xp_s[chunk:chunk + halo, :]
    y = y * _sigmoid(y)
    q_all = y[:, :M_WIDTH]
    k_all = y[:, M_WIDTH:] * (M_HEAD_DIM ** -0.5)

    g = gc_ref[...]
    if t_valid < chunk:
        row = lax.broadcasted_iota(jnp.int32, g.shape, 0)
        lane = lax.broadcasted_iota(jnp.int32, g.shape, 1)
        g = jnp.where(row < t_valid, g, jnp.where(lane < M_HEADS, NEG_INF, 0.0))
    tr = lax.broadcasted_iota(jnp.int32, (chunk, chunk), 0)
    tc = lax.broadcasted_iota(jnp.int32, (chunk, chunk), 1)
    causal = tr >= tc
    tri = causal.astype(F32)
    bcol = jnp.dot(tri, g, precision=HIGHEST, preferred_element_type=F32)
    er = lax.broadcasted_iota(jnp.int32, (SUBLANES, LANES), 0)
    ec = lax.broadcasted_iota(jnp.int32, (SUBLANES, LANES), 1)
    eye = (er == ec).astype(F32)
    g_rows = lax.dot_general(eye, g, NT_DIMS, precision=HIGHEST, preferred_element_type=F32)
    b_rows = lax.dot_general(eye, bcol, NT_DIMS, precision=HIGHEST, preferred_element_type=F32)

    for h in range(M_HEADS):
        hs = slice(h * M_HEAD_DIM, (h + 1) * M_HEAD_DIM)
        b_c = bcol[:, M_HEADS + h:M_HEADS + h + 1]
        ig_c = g[:, h:h + 1]
        b_r = b_rows[M_HEADS + h:M_HEADS + h + 1, :]
        ig_r = g_rows[h:h + 1, :]
        m_prev = m_s[h:h + 1, 0:1]
        logw = jnp.where(causal, b_c - b_r + ig_r, -jnp.inf)
        inter = b_c + m_prev
        m_t = jnp.maximum(inter, jnp.max(logw, axis=-1, keepdims=True))
        w = jnp.exp(logw - m_t)
        a = jnp.exp(inter - m_t)
        q = q_all[:, hs]
        k = k_all[:, hs]
        v = v_ref[:, hs]
        cmat = c_s[h]
        nvec = n_s[h:h + 1, :]
        wqk = w * _bdot_nt(q, k)
        num = _bdot(wqk, v) + a * _bdot_nt(q, cmat)
        den = jnp.sum(wqk, axis=-1, keepdims=True) + a * jnp.sum(q * nvec, axis=-1, keepdims=True)
        hh = num / jnp.maximum(jnp.abs(den), jnp.exp(-m_t))
        m_new = m_t[chunk - 1:chunk, :]
        b_last = b_c[chunk - 1:chunk, :]
        wl = jnp.exp(b_last - b_c + ig_c - m_new)
        al = jnp.exp(b_last + m_prev - m_new)
        c_s[h] = al * cmat + lax.dot_general((v * wl).astype(BF16), k.astype(BF16), TN_DIMS,
                                             preferred_element_type=F32)
        n_s[h:h + 1, :] = al * nvec + jnp.sum(wl * k, axis=0, keepdims=True)
        m_s[h:h + 1, :] = jnp.broadcast_to(m_new, (1, LANES))
        hn = hh * lax.rsqrt(jnp.mean(hh * hh, axis=-1, keepdims=True) + EPS) * mn_ref[:, hs]
        out_ref[:, hs] = _sigmoid(og_ref[:, hs]) * hn

    @pl.when(step == pl.num_programs(1) - 1)
    def _():
        cout_ref[...] = c_s[...]
        nout_ref[...] = n_s[...]
        mout_ref[...] = m_s[...]


def _mlstm_call(qk, v, og, gc, conv_w, conv_b, m_norm, conv0, c0, n0, m0, batch, seq, chunk, t_valid):
    nc = seq // chunk
    n = batch * seq
    row = lambda w: pl.BlockSpec((chunk, w), lambda b, c: (b * nc + c, 0))
    full2 = lambda shape: pl.BlockSpec(shape, lambda b, c: (0, 0))
    per_b = lambda shape: pl.BlockSpec((None,) + shape, lambda b, c: (b,) + (0,) * len(shape))
    dh = M_HEAD_DIM
    return pl.pallas_call(
        functools.partial(_mlstm_kernel, chunk, t_valid),
        grid=(batch, nc),
        in_specs=[row(2 * M_WIDTH), row(M_WIDTH), row(M_WIDTH), row(LANES),
                  full2((CONV_W, 2 * M_WIDTH)), full2((1, 2 * M_WIDTH)), full2((1, M_WIDTH)),
                  per_b((SUBLANES, 2 * M_WIDTH)), per_b((M_HEADS, dh, dh)), per_b((M_HEADS, dh)),
                  per_b((SUBLANES, LANES))],
        out_specs=[row(M_WIDTH), per_b((M_HEADS, dh, dh)), per_b((M_HEADS, dh)), per_b((SUBLANES, LANES))],
        out_shape=[jax.ShapeDtypeStruct((n, M_WIDTH), F32),
                   jax.ShapeDtypeStruct((batch, M_HEADS, dh, dh), F32),
                   jax.ShapeDtypeStruct((batch, M_HEADS, dh), F32),
                   jax.ShapeDtypeStruct((batch, SUBLANES, LANES), F32)],
        scratch_shapes=[pltpu.VMEM((SUBLANES + chunk, 2 * M_WIDTH), F32),
                        pltpu.VMEM((M_HEADS, dh, dh), F32),
                        pltpu.VMEM((M_HEADS, dh), F32),
                        pltpu.VMEM((SUBLANES, LANES), F32)],
        compiler_params=_cparams("parallel", "arbitrary"),
        name="mlstm",
    )(qk, v, og, gc, conv_w, conv_b, m_norm, conv0, c0, n0, m0)


def _out_kernel(att_ref, mo_ref, x_ref, g1_ref, sc_ref, sh_ref, gn_ref, wa_ref, wm_ref, wq_ref,
                xo_ref, h2_ref, qp_ref):
    mix = (jnp.dot(att_ref[...].astype(BF16), wa_ref[...], preferred_element_type=F32)
           + jnp.dot(mo_ref[...].astype(BF16), wm_ref[...], preferred_element_type=F32))
    x = x_ref[...] + g1_ref[...] * mix
    xo_ref[...] = x
    y = x * lax.rsqrt(jnp.mean(x * x, axis=-1, keepdims=True) + EPS) * gn_ref[...]
    h2 = y * (1.0 + sc_ref[...]) + sh_ref[...]
    h2_ref[...] = h2
    qp_ref[...] = jnp.dot(h2.astype(BF16), wq_ref[...], preferred_element_type=F32)


def _out_call(att, mo, x, g1, sc, sh, gnorm, wa, wm, wq, per_token, tokens_per_req):
    n = x.shape[0]
    tm = min(256, n if per_token else tokens_per_req)
    mod = _mod_spec(per_token, tm, tokens_per_req)
    full = lambda shape: pl.BlockSpec(shape, lambda i: (0,) * len(shape))
    row = lambda w: pl.BlockSpec((tm, w), lambda i: (i, 0))
    nq = wq.shape[1]
    return pl.pallas_call(
        _out_kernel,
        grid=(n // tm,),
        in_specs=[row(A_WIDTH), row(M_WIDTH), row(D_MODEL), mod, mod, mod, full((1, D_MODEL)),
                  full(wa.shape), full(wm.shape), full(wq.shape)],
        out_specs=[row(D_MODEL), row(D_MODEL), row(nq)],
        out_shape=[jax.ShapeDtypeStruct((n, D_MODEL), F32), jax.ShapeDtypeStruct((n, D_MODEL), F32),
                   jax.ShapeDtypeStruct((n, nq), F32)],
        compiler_params=_cparams("parallel"),
        name="out_proj",
    )(att, mo, x, g1, sc, sh, gnorm, wa, wm, wq)


def _pk_cells():
    return [(a, b) for a in range(P_TOPK) for b in range(P_TOPK) if (a + 1) * (b + 1) <= P_TOPK]


PK_CELL_ROWS = 64


def _pk_expand_mats():
    cells = _pk_cells()
    e0 = np.zeros((PK_CELL_ROWS, LANES), np.float32)
    e1 = np.zeros((PK_CELL_ROWS, LANES), np.float32)
    for j, (a, b) in enumerate(cells):
        e0[j, a] = 1.0
        e1[j, b] = 1.0
    return e0, e1, len(cells)


def _top_rows(s, rowf, rounds):
    n_rows = s.shape[0]
    vals, idxs = [], []
    for _ in range(rounds):
        m = jnp.max(s, axis=0, keepdims=True)
        i = jnp.min(jnp.where(s == m, rowf, float(n_rows)), axis=0, keepdims=True)
        vals.append(m)
        idxs.append(i)
        s = jnp.where(rowf == i, -jnp.inf, s)
    return jnp.concatenate(vals, axis=0), jnp.concatenate(idxs, axis=0)


def _select_kernel(n_cells, qp_ref, keys_ref, e0_ref, e1_ref, idx_ref, gw_ref, idx_s, gw_s):
    tm = qp_ref.shape[0]
    keyf = lax.broadcasted_iota(jnp.int32, (N_KEYS, tm), 0).astype(F32)
    cellf = lax.broadcasted_iota(jnp.int32, (PK_CELL_ROWS, tm), 0).astype(F32)
    e0 = e0_ref[...]
    e1 = e1_ref[...]
    pad = jnp.zeros((LANES - P_TOPK, tm), F32)

    def head(h, carry):
        h = jnp.asarray(h, jnp.int32)
        sub = []
        for c in range(2):
            col = pl.multiple_of((h * 2 + c) * P_HALF, P_HALF)
            s = _bdot_nt(keys_ref[h, c], qp_ref[:, pl.ds(col, P_HALF)])
            sub.append(_top_rows(s, keyf, P_TOPK))
        (v0, i0), (v1, i1) = sub
        expand = lambda e, x: jnp.dot(e, jnp.concatenate([x, pad], axis=0), precision=HIGHEST,
                                      preferred_element_type=F32)
        cand = expand(e0, v0) + expand(e1, v1)
        cidx = expand(e0, i0 * float(N_KEYS)) + expand(e1, i1)
        cand = jnp.where(cellf < n_cells, cand, -jnp.inf)
        best, eidx = [], []
        for _ in range(P_TOPK):
            m = jnp.max(cand, axis=0, keepdims=True)
            j = jnp.min(jnp.where(cand == m, cellf, float(PK_CELL_ROWS)), axis=0, keepdims=True)
            hit = cellf == j
            eidx.append(jnp.max(jnp.where(hit, cidx, -1.0), axis=0, keepdims=True))
            best.append(m)
            cand = jnp.where(hit, -jnp.inf, cand)
        best = jnp.concatenate(best, axis=0)
        e = jnp.exp(best - best[0:1, :])
        row0 = pl.multiple_of(h * P_TOPK, P_TOPK)
        gw_s[pl.ds(row0, P_TOPK), :] = e / jnp.sum(e, axis=0, keepdims=True)
        idx_s[pl.ds(row0, P_TOPK), :] = jnp.concatenate(eidx, axis=0)
        return carry

    lax.fori_loop(0, P_HEADS, head, 0)
    idx_ref[...] = idx_s[...].T.astype(jnp.int32)
    gw_ref[...] = gw_s[...].T


def _select_call(qp, keys_bf16):
    n = qp.shape[0]
    tm = min(LANES, n)
    e0, e1, n_cells = _pk_expand_mats()
    full = lambda shape: pl.BlockSpec(shape, lambda i: (0,) * len(shape))
    return pl.pallas_call(
        functools.partial(_select_kernel, n_cells),
        grid=(n // tm,),
        in_specs=[pl.BlockSpec((tm, qp.shape[1]), lambda i: (i, 0)), full(keys_bf16.shape),
                  full((PK_CELL_ROWS, LANES)), full((PK_CELL_ROWS, LANES))],
        out_specs=[pl.BlockSpec((tm, LANES), lambda i: (i, 0)), pl.BlockSpec((tm, LANES), lambda i: (i, 0))],
        out_shape=[jax.ShapeDtypeStruct((n, LANES), jnp.int32), jax.ShapeDtypeStruct((n, LANES), F32)],
        scratch_shapes=[pltpu.VMEM((P_HEADS * P_TOPK, tm), F32), pltpu.VMEM((P_HEADS * P_TOPK, tm), F32)],
        compiler_params=_cparams("parallel"),
        name="peer_select",
    )(qp, keys_bf16, jnp.asarray(e0), jnp.asarray(e1))


N_SEL = P_HEADS * P_TOPK

SC_CORES = 2
SC_SUBCORES = 16
SC_LANES = 16
SC_TOK_BLOCK = 8
SC_ROWS = 32
SC_ROW_BLOCK = 16
SC_ACC_CHAINS = 4
SC_ACC_ROWS = 16
N_WCOL = D_MODEL // (2 * SC_LANES)


def _pack_table(t):
    e, dcol = t.shape
    tb = t.astype(BF16).reshape(e, dcol // (2 * SC_LANES), 2, SC_LANES)
    words = lax.bitcast_convert_type(jnp.swapaxes(tb, 2, 3), jnp.uint32)
    return lax.bitcast_convert_type(words, jnp.int32).reshape(e, dcol // 2)


def _sc_gelu(x):
    z = math.sqrt(2.0 / math.pi) * (x + 0.044715 * (x * x * x))
    t = 1.0 - 2.0 / (jnp.exp(2.0 * z) + 1.0)
    return x * (0.5 * (1.0 + t))


def _sc_expert_body(tokens_per_worker, idx_hbm, gw_hbm, h2_hbm, u_hbm, v_hbm, y_hbm,
                    idx_v, gw_v, x_v, o_v, buf, coef_v, tr_v, sem, in_sem, out_sem):
    wid = lax.axis_index("s") * SC_CORES + lax.axis_index("c")
    base = wid * tokens_per_worker
    lane = lax.iota(jnp.int32, SC_LANES)
    n_gather = N_SEL // SC_ROWS
    zero = jnp.zeros((SC_LANES,), F32)

    def gather(p, tt, i):
        table = u_hbm if i < n_gather else v_hbm
        j = i % n_gather
        slot = i % 2
        return pltpu.make_async_copy(table.at[idx_v.at[p, tt, pl.ds(j * SC_ROWS, SC_ROWS)]], buf.at[slot],
                                     sem.at[slot])

    def unpack(w):
        lo = lax.bitcast_convert_type(lax.shift_left(w, jnp.full((SC_LANES,), 16, jnp.int32)), F32)
        hi = lax.bitcast_convert_type(w & jnp.full((SC_LANES,), -65536, jnp.int32), F32)
        return lo, hi

    def act_chunk(p, tt, j, slot):
        @pl.loop(0, SC_ROWS // SC_LANES)
        def _(half):
            for rb in range(SC_LANES // SC_ROW_BLOCK):
                r0 = half * SC_LANES + rb * SC_ROW_BLOCK

                def col(c, accs):
                    woff = pl.multiple_of(c * SC_LANES, SC_LANES)
                    xoff = pl.multiple_of(c * (2 * SC_LANES), 2 * SC_LANES)
                    xa = x_v[p, tt, pl.ds(xoff, SC_LANES)]
                    xb = x_v[p, tt, pl.ds(xoff + SC_LANES, SC_LANES)]
                    out = []
                    for r, a in enumerate(accs):
                        lo, hi = unpack(buf[slot, r0 + r, pl.ds(woff, SC_LANES)])
                        out.append(a + lo * xa + hi * xb)
                    return tuple(out)

                accs = lax.fori_loop(0, N_WCOL, col, (zero,) * SC_ROW_BLOCK)
                for r in range(SC_ROW_BLOCK):
                    tr_v[pl.ds((rb * SC_ROW_BLOCK + r) * SC_LANES, SC_LANES)] = accs[r]
            tot = zero
            for jj in range(SC_LANES):
                tot = tot + plsc.load_gather(tr_v, [lane * SC_LANES + jj])
            k0 = pl.multiple_of(j * SC_ROWS + half * SC_LANES, SC_LANES)
            coef_v[pl.ds(k0, SC_LANES)] = gw_v[p, tt, pl.ds(k0, SC_LANES)] * _sc_gelu(tot)

    def acc_chunk(p, tt, j, slot, first):
        def tree_sum(parts):
            while len(parts) > 1:
                parts = [parts[i] + parts[i + 1] for i in range(0, len(parts), 2)]
            return parts[0]

        for rb in range(SC_ROWS // SC_ACC_ROWS):
            rows = range(rb * SC_ACC_ROWS, (rb + 1) * SC_ACC_ROWS)
            splat = {r: plsc.load_gather(coef_v, [jnp.full((SC_LANES,), j * SC_ROWS + r, jnp.int32)])
                     for r in rows}
            fresh = first and rb == 0

            @plsc.parallel_loop(0, N_WCOL)
            def _(c):
                woff = pl.multiple_of(c * SC_LANES, SC_LANES)
                xoff = pl.multiple_of(c * (2 * SC_LANES), 2 * SC_LANES)
                pa, pb = [], []
                for n, r in enumerate(rows):
                    lo, hi = unpack(buf[slot, r, pl.ds(woff, SC_LANES)])
                    if n < SC_ACC_CHAINS:
                        pa.append(splat[r] * lo)
                        pb.append(splat[r] * hi)
                    else:
                        pa[n % SC_ACC_CHAINS] = pa[n % SC_ACC_CHAINS] + splat[r] * lo
                        pb[n % SC_ACC_CHAINS] = pb[n % SC_ACC_CHAINS] + splat[r] * hi
                sa, sb = tree_sum(pa), tree_sum(pb)
                if not fresh:
                    sa = sa + o_v[p, tt, pl.ds(xoff, SC_LANES)]
                    sb = sb + o_v[p, tt, pl.ds(xoff + SC_LANES, SC_LANES)]
                o_v[p, tt, pl.ds(xoff, SC_LANES)] = sa
                o_v[p, tt, pl.ds(xoff + SC_LANES, SC_LANES)] = sb

    n_blocks = tokens_per_worker // SC_TOK_BLOCK

    def block_start(blk):
        return pl.multiple_of(base + blk * SC_TOK_BLOCK, SC_TOK_BLOCK)

    def in_copies(blk, p):
        rows = pl.ds(block_start(blk), SC_TOK_BLOCK)
        return [pltpu.make_async_copy(idx_hbm.at[rows], idx_v.at[p], in_sem.at[p]),
                pltpu.make_async_copy(gw_hbm.at[rows], gw_v.at[p], in_sem.at[p]),
                pltpu.make_async_copy(h2_hbm.at[rows], x_v.at[p], in_sem.at[p])]

    def out_copy(blk, p):
        return pltpu.make_async_copy(o_v.at[p], y_hbm.at[pl.ds(block_start(blk), SC_TOK_BLOCK)], out_sem.at[p])

    for cp in in_copies(0, 0):
        cp.start()

    @pl.loop(0, n_blocks)
    def _(blk):
        p = lax.rem(blk, 2)
        for cp in in_copies(blk, p):
            cp.wait()

        @pl.when(blk + 1 < n_blocks)
        def _():
            for cp in in_copies(blk + 1, 1 - p):
                cp.start()

        @pl.when(blk >= 2)
        def _():
            out_copy(blk - 2, p).wait()

        gather(p, 0, 0).start()

        @pl.loop(0, SC_TOK_BLOCK)
        def _(tt):
            n_steps = 2 * n_gather
            for i in range(n_steps):
                if i + 1 < n_steps:
                    gather(p, tt, i + 1).start()
                else:
                    @pl.when(tt + 1 < SC_TOK_BLOCK)
                    def _():
                        gather(p, tt + 1, 0).start()
                gather(p, tt, i).wait()
                if i < n_gather:
                    act_chunk(p, tt, i, i % 2)
                else:
                    acc_chunk(p, tt, i - n_gather, i % 2, i == n_gather)

        out_copy(blk, p).start()

    for blk in (n_blocks - 2, n_blocks - 1):
        out_copy(blk, blk % 2).wait()


def _sc_expert_call(idx, gw, h2, u, v):
    n = idx.shape[0]
    workers = SC_CORES * SC_SUBCORES
    assert n % (workers * SC_TOK_BLOCK) == 0 and n // (workers * SC_TOK_BLOCK) >= 2
    mesh = plsc.VectorSubcoreMesh(core_axis_name="c", subcore_axis_name="s")
    return pl.kernel(
        functools.partial(_sc_expert_body, n // workers),
        out_type=jax.ShapeDtypeStruct((n, D_MODEL), F32),
        mesh=mesh,
        scratch_types=[pltpu.VMEM((2, SC_TOK_BLOCK, N_SEL), jnp.int32),
                       pltpu.VMEM((2, SC_TOK_BLOCK, N_SEL), F32),
                       pltpu.VMEM((2, SC_TOK_BLOCK, D_MODEL), F32),
                       pltpu.VMEM((2, SC_TOK_BLOCK, D_MODEL), F32),
                       pltpu.VMEM((2, SC_ROWS, D_MODEL // 2), jnp.int32),
                       pltpu.VMEM((N_SEL,), F32),
                       pltpu.VMEM((SC_LANES * SC_LANES,), F32),
                       pltpu.SemaphoreType.DMA((2,)), pltpu.SemaphoreType.DMA((2,)),
                       pltpu.SemaphoreType.DMA((2,))],
        compiler_params=pltpu.CompilerParams(needs_layout_passes=False),
        name="peer_experts_sc",
    )(idx, gw, h2, u, v)


def _resid_kernel(x_ref, y_ref, g_ref, o_ref):
    o_ref[...] = x_ref[...] + g_ref[...] * y_ref[...]


def _resid_call(x, y, g2, per_token, tokens_per_req):
    n = x.shape[0]
    tm = min(512, n if per_token else tokens_per_req)
    row = pl.BlockSpec((tm, D_MODEL), lambda i: (i, 0))
    return pl.pallas_call(
        _resid_kernel,
        grid=(n // tm,),
        in_specs=[row, row, _mod_spec(per_token, tm, tokens_per_req)],
        out_specs=row,
        out_shape=jax.ShapeDtypeStruct((n, D_MODEL), F32),
        compiler_params=_cparams("parallel"),
        name="peer_residual",
    )(x, y, g2)


def _final_kernel(x_ref, g_ref, o_ref):
    x = x_ref[...]
    o_ref[...] = x * lax.rsqrt(jnp.mean(x * x, axis=-1, keepdims=True) + EPS) * g_ref[...]


def _final_call(x, g):
    n = x.shape[0]
    tm = min(512, n)
    return pl.pallas_call(
        _final_kernel,
        grid=(n // tm,),
        in_specs=[pl.BlockSpec((tm, D_MODEL), lambda i: (i, 0)), pl.BlockSpec((1, D_MODEL), lambda i: (0, 0))],
        out_specs=pl.BlockSpec((tm, D_MODEL), lambda i: (i, 0)),
        out_shape=jax.ShapeDtypeStruct((n, D_MODEL), F32),
        compiler_params=_cparams("parallel"),
        name="final_norm",
    )(x, g)


def _split_w_in(w_in_l, gate_b_l):
    cuts = np.cumsum([A_WIDTH + 2 * A_KV_WIDTH, 2 * M_WIDTH, M_WIDTH, M_WIDTH]).tolist()
    wa = w_in_l[:, :cuts[0]].astype(BF16)
    wqk = w_in_l[:, cuts[0]:cuts[1]].astype(BF16)
    wv = w_in_l[:, cuts[1]:cuts[2]].astype(BF16)
    wo = w_in_l[:, cuts[2]:cuts[3]].astype(BF16)
    ng = 2 * M_HEADS
    wg = jnp.pad(w_in_l[:, cuts[3]:], ((0, 0), (0, LANES - ng))).astype(BF16)
    gb = jnp.pad(gate_b_l.astype(F32), (0, LANES - ng)).reshape(1, LANES)
    return wa, wqk, wv, wo, wg, gb


def _layer(x, mods, per_token, batch, seq, t_valid, lw, bias_p, bias_c, bias_n, kv_cache, conv0, state,
           after=None):
    (norm_mix, norm_ffn, w_in, conv_w, conv_b, gate_b, sinks, m_norm, w_out, peer_query, peer_keys,
     peer_u, peer_v) = lw
    if after is not None:
        x, _ = lax.optimization_barrier((x, after))
    sh1, sc1, g1, sh2, sc2, g2 = mods
    wa, wqk, wv, wo, wg, gb = _split_w_in(w_in, gate_b)
    qkv, qkm, vm, om, gc = _in_call(x, sc1, sh1, norm_mix.reshape(1, -1), wa, wqk, wv, wo, wg, gb,
                                    per_token, seq)
    if kv_cache is None:
        att = _attn_p_call(qkv, bias_p, sinks, batch, seq)
        kv3 = qkv.reshape(batch, seq, -1)
        new_k = kv3[:, seq - WINDOW:, A_WIDTH:A_WIDTH + A_KV_WIDTH]
        new_v = kv3[:, seq - WINDOW:, A_WIDTH + A_KV_WIDTH:]
        chunk = M_CHUNK
    else:
        att, new_k, new_v = _attn_s_call(qkv, kv_cache[0], kv_cache[1], bias_c, bias_n, sinks, t_valid)
        chunk = seq
    c0, n0, m0 = state
    mo, c_new, n_new, m_new = _mlstm_call(qkm, vm, om, gc, conv_w, conv_b.reshape(1, -1),
                                          m_norm.reshape(1, -1), conv0, c0, n0, m0,
                                          batch, seq, chunk, min(t_valid, chunk))
    new_conv = qkm.reshape(batch, seq, -1)[:, t_valid - (CONV_W - 1):t_valid]
    x_mid, h2, qp = _out_call(att, mo, x, g1, sc2, sh2, norm_ffn.reshape(1, -1),
                              w_out[:A_WIDTH].astype(BF16), w_out[A_WIDTH:].astype(BF16),
                              peer_query.astype(BF16), per_token, seq)
    idx, gw = _select_call(qp, peer_keys.astype(BF16))
    y = _sc_expert_call(idx, gw, h2, peer_u, peer_v)
    x_new = _resid_call(x_mid, y, g2, per_token, seq)
    new_k = new_k.reshape(batch, WINDOW, A_KV_HEADS, A_HEAD_DIM)
    new_v = new_v.reshape(batch, WINDOW, A_KV_HEADS, A_HEAD_DIM)
    return x_new, (new_k, new_v, new_conv, c_new, n_new, m_new[:, :M_HEADS, 0]), idx


def kernel(x_prompt, x_sample, c_prompt, c_sample, cache_k, cache_v, state_conv, state_C, state_n, state_m, rel_bias, w_ada, b_ada, norm_mix, norm_ffn, w_in, conv_w, conv_b, gate_b, attn_sinks, m_norm, w_out, peer_query, peer_keys, peer_u, peer_v, norm_final):
    depth = w_ada.shape[0]
    bp, tp, d = x_prompt.shape
    bs, ts, _ = x_sample.shape
    assert tp % WINDOW == 0 and tp % M_CHUNK == 0 and ts <= SAMPLE_PAD and ts >= CONV_W - 1

    mod_all = _ada_call(jnp.concatenate([c_prompt, c_sample], axis=0), w_ada, b_ada)

    qi = np.arange(WINDOW)[:, None]
    bias_p = _bias_call(rel_bias, qi + WINDOW - np.arange(2 * WINDOW)[None, :])
    qs = np.arange(SAMPLE_PAD)[:, None]
    bias_c = _bias_call(rel_bias, qs + WINDOW - np.arange(WINDOW)[None, :])
    bias_n = _bias_call(rel_bias, qs - np.arange(SAMPLE_PAD)[None, :])

    groups = PROMPT_GROUPS if bp % PROMPT_GROUPS == 0 else 1
    bg = bp // groups
    xg = [x_prompt[g * bg:(g + 1) * bg].reshape(bg * tp, d) for g in range(groups)]
    xs = jnp.pad(x_sample, ((0, 0), (0, SAMPLE_PAD - ts), (0, 0))).reshape(bs * SAMPLE_PAD, d)
    halo_pad = ((0, 0), (SUBLANES - (CONV_W - 1), 0), (0, 0))
    zero_state = (jnp.zeros((bg, M_HEADS, M_HEAD_DIM, M_HEAD_DIM), F32),
                  jnp.zeros((bg, M_HEADS, M_HEAD_DIM), F32),
                  jnp.zeros((bg, SUBLANES, LANES), F32))
    zero_conv = jnp.zeros((bg, SUBLANES, 2 * M_WIDTH), F32)

    st_p, st_s = [], []
    for l in range(depth):
        lw = (norm_mix[l], norm_ffn[l], w_in[l], conv_w[l], conv_b[l], gate_b[l], attn_sinks[l], m_norm[l],
              w_out[l], peer_query[l], peer_keys[l], _pack_table(peer_u[l]), _pack_table(peer_v[l]))
        mod_s = [jnp.repeat(m, SAMPLE_PAD, axis=0) for m in jnp.split(mod_all[l, bp:], 6, axis=-1)]
        sp_groups = []
        for g in range(groups):
            mod_g = [m.reshape(bg, 1, d) for m in jnp.split(mod_all[l, g * bg:(g + 1) * bg], 6, axis=-1)]
            xg[g], sp, last_idx = _layer(xg[g], mod_g, False, bg, tp, tp, lw, bias_p, None, None, None,
                                         zero_conv, zero_state)
            sp_groups.append(sp)
        st_p.append([jnp.concatenate([sp[i] for sp in sp_groups], axis=0) for i in range(6)])
        state_s = (state_C[l].astype(F32), state_n[l].astype(F32),
                   jnp.broadcast_to(jnp.pad(state_m[l].astype(F32), ((0, 0), (0, SUBLANES - M_HEADS)))[:, :, None],
                                    (bs, SUBLANES, LANES)))
        kv_cache = (cache_k[l].reshape(bs, WINDOW, A_KV_WIDTH), cache_v[l].reshape(bs, WINDOW, A_KV_WIDTH))
        xs, ss, _ = _layer(xs, mod_s, True, bs, SAMPLE_PAD, ts, lw, None, bias_c, bias_n, kv_cache,
                           jnp.pad(state_conv[l].astype(F32), halo_pad), state_s, after=last_idx)
        st_s.append(ss)

    gfin = norm_final.reshape(1, d)
    y_prompt = jnp.concatenate([_final_call(x, gfin).reshape(bg, tp, d) for x in xg], axis=0)
    y_sample = _final_call(xs, gfin).reshape(bs, SAMPLE_PAD, d)[:, :ts]
    outs_p = [jnp.stack([s[i] for s in st_p]) for i in range(6)]
    outs_s = [jnp.stack([s[i] for s in st_s]) for i in range(6)]
    return (y_prompt, y_sample, *outs_p, *outs_s)
```

```python
import functools
import math

import numpy as np
import jax
import jax.numpy as jnp
from jax import lax
from jax.experimental import pallas as pl
from jax.experimental.pallas import tpu as pltpu
from jax.experimental.pallas import tpu_sc as plsc

F32 = jnp.float32
BF16 = jnp.bfloat16
HIGHEST = lax.Precision.HIGHEST

D_MODEL = 1024
A_HEADS = 8
A_KV_HEADS = 2
A_GROUP = A_HEADS // A_KV_HEADS
A_HEAD_DIM = 64
A_WIDTH = A_HEADS * A_HEAD_DIM
A_KV_WIDTH = A_KV_HEADS * A_HEAD_DIM
WINDOW = 128
ATT_SCALE = A_HEAD_DIM ** -0.5
N_BUCKETS = 32
MAX_DISTANCE = WINDOW
M_HEADS = 4
M_HEAD_DIM = 128
M_WIDTH = M_HEADS * M_HEAD_DIM
CONV_W = 4
M_CHUNK = 64
N_KEYS = 128
P_HEADS = 8
P_TOPK = 16
P_KEY_DIM = 256
P_HALF = P_KEY_DIM // 2
EPS = 1e-6
NEG_INF = -1e30

LANES = 128
SUBLANES = 8
SAMPLE_PAD = SUBLANES
VMEM_LIMIT = 48 * 1024 * 1024
PROMPT_GROUPS = 4

NT_DIMS = (((1,), (1,)), ((), ()))
TN_DIMS = (((0,), (0,)), ((), ()))


def _cparams(*sem):
    return pltpu.CompilerParams(dimension_semantics=sem, vmem_limit_bytes=VMEM_LIMIT)


def _bdot(a, b):
    return jnp.dot(a.astype(BF16), b.astype(BF16), preferred_element_type=F32)


def _bdot_nt(a, b):
    return lax.dot_general(a.astype(BF16), b.astype(BF16), NT_DIMS, preferred_element_type=F32)


def _sigmoid(x):
    return 1.0 / (1.0 + jnp.exp(-x))


def _log_sigmoid(x):
    return jnp.minimum(x, 0.0) - jnp.log1p(jnp.exp(-jnp.abs(x)))


def _gelu_tanh(x):
    c = math.sqrt(2.0 / math.pi)
    return x * (0.5 * (1.0 + jnp.tanh(c * (x + 0.044715 * (x * x * x)))))


def _ada_kernel(c_ref, w_ref, b_ref, o_ref):
    c = c_ref[...]
    s = c * _sigmoid(c)
    o_ref[...] = jnp.dot(s, w_ref[...], precision=HIGHEST, preferred_element_type=F32) + b_ref[...]


def _ada_call(c_all, w_ada, b_ada):
    depth, d, n6 = w_ada.shape
    rows = c_all.shape[0]
    bn = 1024
    return pl.pallas_call(
        _ada_kernel,
        grid=(depth, n6 // bn),
        in_specs=[
            pl.BlockSpec((rows, d), lambda l, j: (0, 0)),
            pl.BlockSpec((None, d, bn), lambda l, j: (l, 0, j)),
            pl.BlockSpec((None, 1, bn), lambda l, j: (l, 0, j)),
        ],
        out_specs=pl.BlockSpec((None, rows, bn), lambda l, j: (l, 0, j)),
        out_shape=jax.ShapeDtypeStruct((depth, rows, n6), F32),
        compiler_params=_cparams("parallel", "parallel"),
        name="ada_mod",
    )(c_all, w_ada, b_ada.reshape(depth, 1, n6))


def _mod_spec(per_token, tm, tokens_per_req):
    if per_token:
        return pl.BlockSpec((tm, D_MODEL), lambda i: (i, 0))
    tiles = tokens_per_req // tm
    return pl.BlockSpec((None, 1, D_MODEL), lambda i: (i // tiles, 0, 0))


def _in_kernel(x_ref, sc_ref, sh_ref, g_ref, wa_ref, wqk_ref, wv_ref, wo_ref, wg_ref, gb_ref,
               qkv_ref, qkm_ref, v_ref, o_ref, gc_ref):
    x = x_ref[...]
    y = x * lax.rsqrt(jnp.mean(x * x, axis=-1, keepdims=True) + EPS) * g_ref[...]
    h = (y * (1.0 + sc_ref[...]) + sh_ref[...]).astype(BF16)
    qkv_ref[...] = jnp.dot(h, wa_ref[...], preferred_element_type=F32)
    qkm_ref[...] = jnp.dot(h, wqk_ref[...], preferred_element_type=F32)
    v_ref[...] = jnp.dot(h, wv_ref[...], preferred_element_type=F32)
    o_ref[...] = jnp.dot(h, wo_ref[...], preferred_element_type=F32)
    g = jnp.dot(h, wg_ref[...], preferred_element_type=F32) + gb_ref[...]
    lane = lax.broadcasted_iota(jnp.int32, g.shape, 1)
    gc_ref[...] = jnp.where(lane < M_HEADS, g, jnp.where(lane < 2 * M_HEADS, _log_sigmoid(g), 0.0))


def _in_call(x, sc, sh, gnorm, wa, wqk, wv, wo, wg, gb, per_token, tokens_per_req):
    n = x.shape[0]
    tm = min(512, n if per_token else tokens_per_req)
    mod = _mod_spec(per_token, tm, tokens_per_req)
    full = lambda shape: pl.BlockSpec(shape, lambda i: (0,) * len(shape))
    row = lambda w: pl.BlockSpec((tm, w), lambda i: (i, 0))
    return pl.pallas_call(
        _in_kernel,
        grid=(n // tm,),
        in_specs=[row(D_MODEL), mod, mod, full((1, D_MODEL)), full(wa.shape), full(wqk.shape),
                  full(wv.shape), full(wo.shape), full(wg.shape), full((1, LANES))],
        out_specs=[row(wa.shape[1]), row(wqk.shape[1]), row(wv.shape[1]), row(wo.shape[1]), row(LANES)],
        out_shape=[jax.ShapeDtypeStruct((n, w), F32)
                   for w in (wa.shape[1], wqk.shape[1], wv.shape[1], wo.shape[1], LANES)],
        compiler_params=_cparams("parallel"),
        name="in_proj",
    )(x, sc, sh, gnorm, wa, wqk, wv, wo, wg, gb)


def _t5_bucket_np(dist):
    n = np.maximum(dist, 0)
    max_exact = N_BUCKETS // 2
    nf = np.maximum(n, 1).astype(np.float64)
    large = max_exact + (np.log(nf / max_exact) / math.log(MAX_DISTANCE / max_exact)
                         * (N_BUCKETS - max_exact)).astype(np.int32)
    return np.where(n < max_exact, n, np.minimum(large, N_BUCKETS - 1)).astype(np.int32)


def _bias_kernel(bucket_ref, rel_ref, o_ref):
    bucket = bucket_ref[...]
    for h in range(A_HEADS):
        acc = jnp.zeros(bucket.shape, F32)
        for b in range(N_BUCKETS):
            acc = jnp.where(bucket == b, rel_ref[b, h], acc)
        o_ref[h] = acc


def _bias_call(rel_bias, dist):
    bucket = jnp.asarray(_t5_bucket_np(dist))
    nq, nk = dist.shape
    return pl.pallas_call(
        _bias_kernel,
        in_specs=[pl.BlockSpec((nq, nk), lambda: (0, 0)),
                  pl.BlockSpec(memory_space=pltpu.SMEM)],
        out_specs=pl.BlockSpec((A_HEADS, nq, nk), lambda: (0, 0, 0)),
        out_shape=jax.ShapeDtypeStruct((A_HEADS, nq, nk), F32),
        name="t5_bias",
    )(bucket, rel_bias)


def _attn_p_kernel(q_ref, kp_ref, kc_ref, vp_ref, vc_ref, bias_ref, sink_ref, o_ref):
    i = pl.program_id(1)
    qi = lax.broadcasted_iota(jnp.int32, (WINDOW, WINDOW), 0)
    kj = lax.broadcasted_iota(jnp.int32, (WINDOW, WINDOW), 1)
    valid_prev = jnp.logical_and(kj > qi, i > 0)
    valid_cur = kj <= qi
    q = q_ref[...]
    for h in range(A_HEADS):
        kv = h // A_GROUP
        qh = q[:, h * A_HEAD_DIM:(h + 1) * A_HEAD_DIM]
        sl = slice(kv * A_HEAD_DIM, (kv + 1) * A_HEAD_DIM)
        bias = bias_ref[h]
        sp = _bdot_nt(qh, kp_ref[:, sl]) * ATT_SCALE + bias[:, :WINDOW]
        sc = _bdot_nt(qh, kc_ref[:, sl]) * ATT_SCALE + bias[:, WINDOW:]
        sp = jnp.where(valid_prev, sp, NEG_INF)
        sc = jnp.where(valid_cur, sc, NEG_INF)
        sink = sink_ref[0, h]
        mx = jnp.maximum(jnp.maximum(jnp.max(sp, axis=-1, keepdims=True),
                                     jnp.max(sc, axis=-1, keepdims=True)), sink)
        ep = jnp.exp(sp - mx)
        ec = jnp.exp(sc - mx)
        den = (jnp.sum(ep, axis=-1, keepdims=True) + jnp.sum(ec, axis=-1, keepdims=True)
               + jnp.exp(sink - mx))
        o = _bdot(ep / den, vp_ref[:, sl]) + _bdot(ec / den, vc_ref[:, sl])
        o_ref[:, h * A_HEAD_DIM:(h + 1) * A_HEAD_DIM] = o


def _attn_p_call(qkv, bias, sinks, batch, seq):
    nb = seq // WINDOW
    n = batch * seq
    kcol = A_WIDTH // A_KV_WIDTH
    vcol = kcol + 1
    cur = lambda col: pl.BlockSpec((WINDOW, A_KV_WIDTH), lambda b, i: (b * nb + i, col))
    prev = lambda col: pl.BlockSpec((WINDOW, A_KV_WIDTH),
                                    lambda b, i: (b * nb + jnp.maximum(i - 1, 0), col))
    return pl.pallas_call(
        _attn_p_kernel,
        grid=(batch, nb),
        in_specs=[pl.BlockSpec((WINDOW, A_WIDTH), lambda b, i: (b * nb + i, 0)),
                  prev(kcol), cur(kcol), prev(vcol), cur(vcol),
                  pl.BlockSpec((A_HEADS, WINDOW, 2 * WINDOW), lambda b, i: (0, 0, 0)),
                  pl.BlockSpec(memory_space=pltpu.SMEM)],
        out_specs=pl.BlockSpec((WINDOW, A_WIDTH), lambda b, i: (b * nb + i, 0)),
        out_shape=jax.ShapeDtypeStruct((n, A_WIDTH), F32),
        compiler_params=_cparams("parallel", "parallel"),
        name="swa_prompt",
    )(qkv, qkv, qkv, qkv, qkv, bias, sinks.reshape(1, A_HEADS))


def _attn_s_kernel(n_new, qkv_ref, ck_ref, cv_ref, bc_ref, bn_ref, sink_ref,
                   o_ref, nk_ref, nv_ref, kk_s, vv_s):
    qkv = qkv_ref[...]
    knew = qkv[:, A_WIDTH:A_WIDTH + A_KV_WIDTH]
    vnew = qkv[:, A_WIDTH + A_KV_WIDTH:A_WIDTH + 2 * A_KV_WIDTH]
    ck = ck_ref[...]
    cv = cv_ref[...]
    qi = lax.broadcasted_iota(jnp.int32, (SAMPLE_PAD, WINDOW), 0)
    kj = lax.broadcasted_iota(jnp.int32, (SAMPLE_PAD, WINDOW), 1)
    valid_c = kj > qi
    qcol = lax.broadcasted_iota(jnp.int32, (SAMPLE_PAD, 1), 0)
    for h in range(A_HEADS):
        kv = h // A_GROUP
        sl = slice(kv * A_HEAD_DIM, (kv + 1) * A_HEAD_DIM)
        qh = qkv[:, h * A_HEAD_DIM:(h + 1) * A_HEAD_DIM]
        s_c = lax.dot_general(qh, ck[:, sl], NT_DIMS, precision=HIGHEST,
                              preferred_element_type=F32) * ATT_SCALE + bc_ref[h]
        s_c = jnp.where(valid_c, s_c, NEG_INF)
        bn = bn_ref[h]
        s_n = []
        for j in range(n_new):
            sj = jnp.sum(qh * knew[j:j + 1, sl], axis=-1, keepdims=True) * ATT_SCALE + bn[:, j:j + 1]
            s_n.append(jnp.where(qcol >= j, sj, NEG_INF))
        sink = sink_ref[0, h]
        mx = jnp.maximum(jnp.max(s_c, axis=-1, keepdims=True), sink)
        for sj in s_n:
            mx = jnp.maximum(mx, sj)
        e_c = jnp.exp(s_c - mx)
        den = jnp.sum(e_c, axis=-1, keepdims=True) + jnp.exp(sink - mx)
        o = jnp.dot(e_c, cv[:, sl], precision=HIGHEST, preferred_element_type=F32)
        for j, sj in enumerate(s_n):
            ej = jnp.exp(sj - mx)
            den = den + ej
            o = o + ej * vnew[j:j + 1, sl]
        o_ref[:, h * A_HEAD_DIM:(h + 1) * A_HEAD_DIM] = o / den
    kk_s[0:WINDOW, :] = ck
    kk_s[WINDOW:WINDOW + SAMPLE_PAD, :] = knew
    vv_s[0:WINDOW, :] = cv
    vv_s[WINDOW:WINDOW + SAMPLE_PAD, :] = vnew
    nk_ref[...] = kk_s[n_new:n_new + WINDOW, :]
    nv_ref[...] = vv_s[n_new:n_new + WINDOW, :]


def _attn_s_call(qkv, ck, cv, bias_c, bias_n, sinks, n_new):
    nreq = ck.shape[0]
    wq = qkv.shape[1]
    full3 = lambda shape: pl.BlockSpec(shape, lambda b: (0, 0, 0))
    cache = pl.BlockSpec((None, WINDOW, A_KV_WIDTH), lambda b: (b, 0, 0))
    return pl.pallas_call(
        functools.partial(_attn_s_kernel, n_new),
        grid=(nreq,),
        in_specs=[pl.BlockSpec((SAMPLE_PAD, wq), lambda b: (b, 0)), cache, cache,
                  full3(bias_c.shape), full3(bias_n.shape),
                  pl.BlockSpec(memory_space=pltpu.SMEM)],
        out_specs=[pl.BlockSpec((SAMPLE_PAD, A_WIDTH), lambda b: (b, 0)), cache, cache],
        out_shape=[jax.ShapeDtypeStruct((nreq * SAMPLE_PAD, A_WIDTH), F32),
                   jax.ShapeDtypeStruct(ck.shape, F32), jax.ShapeDtypeStruct(cv.shape, F32)],
        scratch_shapes=[pltpu.VMEM((WINDOW + SAMPLE_PAD, A_KV_WIDTH), F32),
                        pltpu.VMEM((WINDOW + SAMPLE_PAD, A_KV_WIDTH), F32)],
        compiler_params=_cparams("parallel"),
        name="swa_sample",
    )(qkv, ck, cv, bias_c, bias_n, sinks.reshape(1, A_HEADS))


def _mlstm_kernel(chunk, t_valid, qk_ref, v_ref, og_ref, gc_ref, cw_ref, cb_ref, mn_ref,
                  conv0_ref, c0_ref, n0_ref, m0_ref,
                  out_ref, cout_ref, nout_ref, mout_ref,
                  xp_s, c_s, n_s, m_s):
    step = pl.program_id(1)
    halo = SUBLANES

    @pl.when(step == 0)
    def _():
        xp_s[0:halo, :] = conv0_ref[...]
        c_s[...] = c0_ref[...]
        n_s[...] = n0_ref[...]
        m_s[...] = m0_ref[...]

    xp_s[halo:halo + chunk, :] = qk_ref[...]
    cw = cw_ref[...]
    y = cb_ref[...]
    for i in range(CONV_W):
        off = halo - (CONV_W - 1) + i
        y = y + xp_s[off:off + chunk, :] * cw[i:i + 1, :]
    xp_s[0:halo, :] = xp_s[chunk:chunk + halo, :]
    y = y * _sigmoid(y)
    q_all = y[:, :M_WIDTH]
    k_all = y[:, M_WIDTH:] * (M_HEAD_DIM ** -0.5)

    g = gc_ref[...]
    if t_valid < chunk:
        row = lax.broadcasted_iota(jnp.int32, g.shape, 0)
        lane = lax.broadcasted_iota(jnp.int32, g.shape, 1)
        g = jnp.where(row < t_valid, g, jnp.where(lane < M_HEADS, NEG_INF, 0.0))
    tr = lax.broadcasted_iota(jnp.int32, (chunk, chunk), 0)
    tc = lax.broadcasted_iota(jnp.int32, (chunk, chunk), 1)
    causal = tr >= tc
    tri = causal.astype(F32)
    bcol = jnp.dot(tri, g, precision=HIGHEST, preferred_element_type=F32)
    er = lax.broadcasted_iota(jnp.int32, (SUBLANES, LANES), 0)
    ec = lax.broadcasted_iota(jnp.int32, (SUBLANES, LANES), 1)
    eye = (er == ec).astype(F32)
    g_rows = lax.dot_general(eye, g, NT_DIMS, precision=HIGHEST, preferred_element_type=F32)
    b_rows = lax.dot_general(eye, bcol, NT_DIMS, precision=HIGHEST, preferred_element_type=F32)

    for h in range(M_HEADS):
        hs = slice(h * M_HEAD_DIM, (h + 1) * M_HEAD_DIM)
        b_c = bcol[:, M_HEADS + h:M_HEADS + h + 1]
        ig_c = g[:, h:h + 1]
        b_r = b_rows[M_HEADS + h:M_HEADS + h + 1, :]
        ig_r = g_rows[h:h + 1, :]
        m_prev = m_s[h:h + 1, 0:1]
        logw = jnp.where(causal, b_c - b_r + ig_r, -jnp.inf)
        inter = b_c + m_prev
        m_t = jnp.maximum(inter, jnp.max(logw, axis=-1, keepdims=True))
        w = jnp.exp(logw - m_t)
        a = jnp.exp(inter - m_t)
        q = q_all[:, hs]
        k = k_all[:, hs]
        v = v_ref[:, hs]
        cmat = c_s[h]
        nvec = n_s[h:h + 1, :]
        wqk = w * _bdot_nt(q, k)
        num = _bdot(wqk, v) + a * _bdot_nt(q, cmat)
        den = jnp.sum(wqk, axis=-1, keepdims=True) + a * jnp.sum(q * nvec, axis=-1, keepdims=True)
        hh = num / jnp.maximum(jnp.abs(den), jnp.exp(-m_t))
        m_new = m_t[chunk - 1:chunk, :]
        b_last = b_c[chunk - 1:chunk, :]
        wl = jnp.exp(b_last - b_c + ig_c - m_new)
        al = jnp.exp(b_last + m_prev - m_new)
        c_s[h] = al * cmat + lax.dot_general((v * wl).astype(BF16), k.astype(BF16), TN_DIMS,
                                             preferred_element_type=F32)
        n_s[h:h + 1, :] = al * nvec + jnp.sum(wl * k, axis=0, keepdims=True)
        m_s[h:h + 1, :] = jnp.broadcast_to(m_new, (1, LANES))
        hn = hh * lax.rsqrt(jnp.mean(hh * hh, axis=-1, keepdims=True) + EPS) * mn_ref[:, hs]
        out_ref[:, hs] = _sigmoid(og_ref[:, hs]) * hn

    @pl.when(step == pl.num_programs(1) - 1)
    def _():
        cout_ref[...] = c_s[...]
        nout_ref[...] = n_s[...]
        mout_ref[...] = m_s[...]


def _mlstm_call(qk, v, og, gc, conv_w, conv_b, m_norm, conv0, c0, n0, m0, batch, seq, chunk, t_valid):
    nc = seq // chunk
    n = batch * seq
    row = lambda w: pl.BlockSpec((chunk, w), lambda b, c: (b * nc + c, 0))
    full2 = lambda shape: pl.BlockSpec(shape, lambda b, c: (0, 0))
    per_b = lambda shape: pl.BlockSpec((None,) + shape, lambda b, c: (b,) + (0,) * len(shape))
    dh = M_HEAD_DIM
    return pl.pallas_call(
        functools.partial(_mlstm_kernel, chunk, t_valid),
        grid=(batch, nc),
        in_specs=[row(2 * M_WIDTH), row(M_WIDTH), row(M_WIDTH), row(LANES),
                  full2((CONV_W, 2 * M_WIDTH)), full2((1, 2 * M_WIDTH)), full2((1, M_WIDTH)),
                  per_b((SUBLANES, 2 * M_WIDTH)), per_b((M_HEADS, dh, dh)), per_b((M_HEADS, dh)),
                  per_b((SUBLANES, LANES))],
        out_specs=[row(M_WIDTH), per_b((M_HEADS, dh, dh)), per_b((M_HEADS, dh)), per_b((SUBLANES, LANES))],
        out_shape=[jax.ShapeDtypeStruct((n, M_WIDTH), F32),
                   jax.ShapeDtypeStruct((batch, M_HEADS, dh, dh), F32),
                   jax.ShapeDtypeStruct((batch, M_HEADS, dh), F32),
                   jax.ShapeDtypeStruct((batch, SUBLANES, LANES), F32)],
        scratch_shapes=[pltpu.VMEM((SUBLANES + chunk, 2 * M_WIDTH), F32),
                        pltpu.VMEM((M_HEADS, dh, dh), F32),
                        pltpu.VMEM((M_HEADS, dh), F32),
                        pltpu.VMEM((SUBLANES, LANES), F32)],
        compiler_params=_cparams("parallel", "arbitrary"),
        name="mlstm",
    )(qk, v, og, gc, conv_w, conv_b, m_norm, conv0, c0, n0, m0)


def _out_kernel(att_ref, mo_ref, x_ref, g1_ref, sc_ref, sh_ref, gn_ref, wa_ref, wm_ref, wq_ref,
                xo_ref, h2_ref, qp_ref):
    mix = (jnp.dot(att_ref[...].astype(BF16), wa_ref[...], preferred_element_type=F32)
           + jnp.dot(mo_ref[...].astype(BF16), wm_ref[...], preferred_element_type=F32))
    x = x_ref[...] + g1_ref[...] * mix
    xo_ref[...] = x
    y = x * lax.rsqrt(jnp.mean(x * x, axis=-1, keepdims=True) + EPS) * gn_ref[...]
    h2 = y * (1.0 + sc_ref[...]) + sh_ref[...]
    h2_ref[...] = h2
    qp_ref[...] = jnp.dot(h2.astype(BF16), wq_ref[...], preferred_element_type=F32)


def _out_call(att, mo, x, g1, sc, sh, gnorm, wa, wm, wq, per_token, tokens_per_req):
    n = x.shape[0]
    tm = min(256, n if per_token else tokens_per_req)
    mod = _mod_spec(per_token, tm, tokens_per_req)
    full = lambda shape: pl.BlockSpec(shape, lambda i: (0,) * len(shape))
    row = lambda w: pl.BlockSpec((tm, w), lambda i: (i, 0))
    nq = wq.shape[1]
    return pl.pallas_call(
        _out_kernel,
        grid=(n // tm,),
        in_specs=[row(A_WIDTH), row(M_WIDTH), row(D_MODEL), mod, mod, mod, full((1, D_MODEL)),
                  full(wa.shape), full(wm.shape), full(wq.shape)],
        out_specs=[row(D_MODEL), row(D_MODEL), row(nq)],
        out_shape=[jax.ShapeDtypeStruct((n, D_MODEL), F32), jax.ShapeDtypeStruct((n, D_MODEL), F32),
                   jax.ShapeDtypeStruct((n, nq), F32)],
        compiler_params=_cparams("parallel"),
        name="out_proj",
    )(att, mo, x, g1, sc, sh, gnorm, wa, wm, wq)


def _pk_cells():
    return [(a, b) for a in range(P_TOPK) for b in range(P_TOPK) if (a + 1) * (b + 1) <= P_TOPK]


PK_CELL_ROWS = 64


def _pk_expand_mats():
    cells = _pk_cells()
    e0 = np.zeros((PK_CELL_ROWS, LANES), np.float32)
    e1 = np.zeros((PK_CELL_ROWS, LANES), np.float32)
    for j, (a, b) in enumerate(cells):
        e0[j, a] = 1.0
        e1[j, b] = 1.0
    return e0, e1, len(cells)


def _top_rows(s, rowf, rounds):
    n_rows = s.shape[0]
    vals, idxs = [], []
    for _ in range(rounds):
        m = jnp.max(s, axis=0, keepdims=True)
        i = jnp.min(jnp.where(s == m, rowf, float(n_rows)), axis=0, keepdims=True)
        vals.append(m)
        idxs.append(i)
        s = jnp.where(rowf == i, -jnp.inf, s)
    return jnp.concatenate(vals, axis=0), jnp.concatenate(idxs, axis=0)


def _select_kernel(n_cells, qp_ref, keys_ref, e0_ref, e1_ref, idx_ref, gw_ref, idx_s, gw_s):
    tm = qp_ref.shape[0]
    keyf = lax.broadcasted_iota(jnp.int32, (N_KEYS, tm), 0).astype(F32)
    cellf = lax.broadcasted_iota(jnp.int32, (PK_CELL_ROWS, tm), 0).astype(F32)
    e0 = e0_ref[...]
    e1 = e1_ref[...]
    pad = jnp.zeros((LANES - P_TOPK, tm), F32)

    def head(h, carry):
        h = jnp.asarray(h, jnp.int32)
        sub = []
        for c in range(2):
            col = pl.multiple_of((h * 2 + c) * P_HALF, P_HALF)
            s = _bdot_nt(keys_ref[h, c], qp_ref[:, pl.ds(col, P_HALF)])
            sub.append(_top_rows(s, keyf, P_TOPK))
        (v0, i0), (v1, i1) = sub
        expand = lambda e, x: jnp.dot(e, jnp.concatenate([x, pad], axis=0), precision=HIGHEST,
                                      preferred_element_type=F32)
        cand = expand(e0, v0) + expand(e1, v1)
        cidx = expand(e0, i0 * float(N_KEYS)) + expand(e1, i1)
        cand = jnp.where(cellf < n_cells, cand, -jnp.inf)
        best, eidx = [], []
        for _ in range(P_TOPK):
            m = jnp.max(cand, axis=0, keepdims=True)
            j = jnp.min(jnp.where(cand == m, cellf, float(PK_CELL_ROWS)), axis=0, keepdims=True)
            hit = cellf == j
            eidx.append(jnp.max(jnp.where(hit, cidx, -1.0), axis=0, keepdims=True))
            best.append(m)
            cand = jnp.where(hit, -jnp.inf, cand)
        best = jnp.concatenate(best, axis=0)
        e = jnp.exp(best - best[0:1, :])
        row0 = pl.multiple_of(h * P_TOPK, P_TOPK)
        gw_s[pl.ds(row0, P_TOPK), :] = e / jnp.sum(e, axis=0, keepdims=True)
        idx_s[pl.ds(row0, P_TOPK), :] = jnp.concatenate(eidx, axis=0)
        return carry

    lax.fori_loop(0, P_HEADS, head, 0)
    idx_ref[...] = idx_s[...].T.astype(jnp.int32)
    gw_ref[...] = gw_s[...].T


def _select_call(qp, keys_bf16):
    n = qp.shape[0]
    tm = min(LANES, n)
    e0, e1, n_cells = _pk_expand_mats()
    full = lambda shape: pl.BlockSpec(shape, lambda i: (0,) * len(shape))
    return pl.pallas_call(
        functools.partial(_select_kernel, n_cells),
        grid=(n // tm,),
        in_specs=[pl.BlockSpec((tm, qp.shape[1]), lambda i: (i, 0)), full(keys_bf16.shape),
                  full((PK_CELL_ROWS, LANES)), full((PK_CELL_ROWS, LANES))],
        out_specs=[pl.BlockSpec((tm, LANES), lambda i: (i, 0)), pl.BlockSpec((tm, LANES), lambda i: (i, 0))],
        out_shape=[jax.ShapeDtypeStruct((n, LANES), jnp.int32), jax.ShapeDtypeStruct((n, LANES), F32)],
        scratch_shapes=[pltpu.VMEM((P_HEADS * P_TOPK, tm), F32), pltpu.VMEM((P_HEADS * P_TOPK, tm), F32)],
        compiler_params=_cparams("parallel"),
        name="peer_select",
    )(qp, keys_bf16, jnp.asarray(e0), jnp.asarray(e1))


N_SEL = P_HEADS * P_TOPK

SC_CORES = 2
SC_SUBCORES = 16
SC_LANES = 16
SC_TOK_BLOCK = 8
SC_ROWS = 32
SC_ROW_BLOCK = 16
SC_ACC_CHAINS = 4
SC_ACC_ROWS = 16
SC_NBUF = 4
N_WCOL = D_MODEL // (2 * SC_LANES)


def _pack_table(t):
    e, dcol = t.shape
    tb = t.astype(BF16).reshape(e, dcol // (2 * SC_LANES), 2, SC_LANES)
    words = lax.bitcast_convert_type(jnp.swapaxes(tb, 2, 3), jnp.uint32)
    return lax.bitcast_convert_type(words, jnp.int32).reshape(e, dcol // 2)


def _sc_gelu(x):
    z = math.sqrt(2.0 / math.pi) * (x + 0.044715 * (x * x * x))
    t = 1.0 - 2.0 / (jnp.exp(2.0 * z) + 1.0)
    return x * (0.5 * (1.0 + t))


def _sc_expert_body(tokens_per_worker, idx_hbm, gw_hbm, h2_hbm, u_hbm, v_hbm, y_hbm,
                    idx_v, gw_v, x_v, o_v, buf, coef_v, tr_v, sem, in_sem, out_sem):
    wid = lax.axis_index("s") * SC_CORES + lax.axis_index("c")
    base = wid * tokens_per_worker
    lane = lax.iota(jnp.int32, SC_LANES)
    n_gather = N_SEL // SC_ROWS
    zero = jnp.zeros((SC_LANES,), F32)

    n_steps = 2 * n_gather
    assert n_steps % SC_NBUF == 0

    def gather(p, tt, i):
        table = u_hbm if i < n_gather else v_hbm
        j = i % n_gather
        slot = i % SC_NBUF
        return pltpu.make_async_copy(table.at[idx_v.at[p, tt, pl.ds(j * SC_ROWS, SC_ROWS)]], buf.at[slot],
                                     sem.at[slot])

    def unpack(w):
        lo = lax.bitcast_convert_type(lax.shift_left(w, jnp.full((SC_LANES,), 16, jnp.int32)), F32)
        hi = lax.bitcast_convert_type(w & jnp.full((SC_LANES,), -65536, jnp.int32), F32)
        return lo, hi

    def act_chunk(p, tt, j, slot):
        @pl.loop(0, SC_ROWS // SC_LANES)
        def _(half):
            for rb in range(SC_LANES // SC_ROW_BLOCK):
                r0 = half * SC_LANES + rb * SC_ROW_BLOCK

                def col(c, accs):
                    woff = pl.multiple_of(c * SC_LANES, SC_LANES)
                    xoff = pl.multiple_of(c * (2 * SC_LANES), 2 * SC_LANES)
                    xa = x_v[p, tt, pl.ds(xoff, SC_LANES)]
                    xb = x_v[p, tt, pl.ds(xoff + SC_LANES, SC_LANES)]
                    out = []
                    for r, a in enumerate(accs):
                        lo, hi = unpack(buf[slot, r0 + r, pl.ds(woff, SC_LANES)])
                        out.append(a + lo * xa + hi * xb)
                    return tuple(out)

                accs = lax.fori_loop(0, N_WCOL, col, (zero,) * SC_ROW_BLOCK)
                for r in range(SC_ROW_BLOCK):
                    tr_v[pl.ds((rb * SC_ROW_BLOCK + r) * SC_LANES, SC_LANES)] = accs[r]
            tot = zero
            for jj in range(SC_LANES):
                tot = tot + plsc.load_gather(tr_v, [lane * SC_LANES + jj])
            k0 = pl.multiple_of(j * SC_ROWS + half * SC_LANES, SC_LANES)
            coef_v[pl.ds(k0, SC_LANES)] = gw_v[p, tt, pl.ds(k0, SC_LANES)] * _sc_gelu(tot)

    def acc_chunk(p, tt, j, slot, first):
        def tree_sum(parts):
            while len(parts) > 1:
                parts = [parts[i] + parts[i + 1] for i in range(0, len(parts), 2)]
            return parts[0]

        for rb in range(SC_ROWS // SC_ACC_ROWS):
            rows = range(rb * SC_ACC_ROWS, (rb + 1) * SC_ACC_ROWS)
            splat = {r: plsc.load_gather(coef_v, [jnp.full((SC_LANES,), j * SC_ROWS + r, jnp.int32)])
                     for r in rows}
            fresh = first and rb == 0

            @plsc.parallel_loop(0, N_WCOL)
            def _(c):
                woff = pl.multiple_of(c * SC_LANES, SC_LANES)
                xoff = pl.multiple_of(c * (2 * SC_LANES), 2 * SC_LANES)
                pa, pb = [], []
                for n, r in enumerate(rows):
                    lo, hi = unpack(buf[slot, r, pl.ds(woff, SC_LANES)])
                    if n < SC_ACC_CHAINS:
                        pa.append(splat[r] * lo)
                        pb.append(splat[r] * hi)
                    else:
                        pa[n % SC_ACC_CHAINS] = pa[n % SC_ACC_CHAINS] + splat[r] * lo
                        pb[n % SC_ACC_CHAINS] = pb[n % SC_ACC_CHAINS] + splat[r] * hi
                sa, sb = tree_sum(pa), tree_sum(pb)
                if not fresh:
                    sa = sa + o_v[p, tt, pl.ds(xoff, SC_LANES)]
                    sb = sb + o_v[p, tt, pl.ds(xoff + SC_LANES, SC_LANES)]
                o_v[p, tt, pl.ds(xoff, SC_LANES)] = sa
                o_v[p, tt, pl.ds(xoff + SC_LANES, SC_LANES)] = sb

    n_blocks = tokens_per_worker // SC_TOK_BLOCK

    def block_start(blk):
        return pl.multiple_of(base + blk * SC_TOK_BLOCK, SC_TOK_BLOCK)

    def in_copies(blk, p):
        rows = pl.ds(block_start(blk), SC_TOK_BLOCK)
        return [pltpu.make_async_copy(idx_hbm.at[rows], idx_v.at[p], in_sem.at[p]),
                pltpu.make_async_copy(gw_hbm.at[rows], gw_v.at[p], in_sem.at[p]),
                pltpu.make_async_copy(h2_hbm.at[rows], x_v.at[p], in_sem.at[p])]

    def out_copy(blk, p):
        return pltpu.make_async_copy(o_v.at[p], y_hbm.at[pl.ds(block_start(blk), SC_TOK_BLOCK)], out_sem.at[p])

    for cp in in_copies(0, 0):
        cp.start()

    @pl.loop(0, n_blocks)
    def _(blk):
        p = lax.rem(blk, 2)
        for cp in in_copies(blk, p):
            cp.wait()

        @pl.when(blk + 1 < n_blocks)
        def _():
            for cp in in_copies(blk + 1, 1 - p):
                cp.start()

        @pl.when(blk >= 2)
        def _():
            out_copy(blk - 2, p).wait()

        ahead = SC_NBUF - 1
        for i in range(ahead):
            gather(p, 0, i).start()

        @pl.loop(0, SC_TOK_BLOCK)
        def _(tt):
            for i in range(n_steps):
                if i + ahead < n_steps:
                    gather(p, tt, i + ahead).start()
                else:
                    @pl.when(tt + 1 < SC_TOK_BLOCK)
                    def _():
                        gather(p, tt + 1, i + ahead - n_steps).start()
                gather(p, tt, i).wait()
                if i < n_gather:
                    act_chunk(p, tt, i, i % SC_NBUF)
                else:
                    acc_chunk(p, tt, i - n_gather, i % SC_NBUF, i == n_gather)

        out_copy(blk, p).start()

    for blk in (n_blocks - 2, n_blocks - 1):
        out_copy(blk, blk % 2).wait()


def _sc_expert_call(idx, gw, h2, u, v):
    n = idx.shape[0]
    workers = SC_CORES * SC_SUBCORES
    assert n % (workers * SC_TOK_BLOCK) == 0 and n // (workers * SC_TOK_BLOCK) >= 2
    mesh = plsc.VectorSubcoreMesh(core_axis_name="c", subcore_axis_name="s")
    return pl.kernel(
        functools.partial(_sc_expert_body, n // workers),
        out_type=jax.ShapeDtypeStruct((n, D_MODEL), F32),
        mesh=mesh,
        scratch_types=[pltpu.VMEM((2, SC_TOK_BLOCK, N_SEL), jnp.int32),
                       pltpu.VMEM((2, SC_TOK_BLOCK, N_SEL), F32),
                       pltpu.VMEM((2, SC_TOK_BLOCK, D_MODEL), F32),
                       pltpu.VMEM((2, SC_TOK_BLOCK, D_MODEL), F32),
                       pltpu.VMEM((SC_NBUF, SC_ROWS, D_MODEL // 2), jnp.int32),
                       pltpu.VMEM((N_SEL,), F32),
                       pltpu.VMEM((SC_LANES * SC_LANES,), F32),
                       pltpu.SemaphoreType.DMA((SC_NBUF,)), pltpu.SemaphoreType.DMA((2,)),
                       pltpu.SemaphoreType.DMA((2,))],
        compiler_params=pltpu.CompilerParams(needs_layout_passes=False),
        name="peer_experts_sc",
    )(idx, gw, h2, u, v)


def _resid_kernel(x_ref, y_ref, g_ref, o_ref):
    o_ref[...] = x_ref[...] + g_ref[...] * y_ref[...]


def _resid_call(x, y, g2, per_token, tokens_per_req):
    n = x.shape[0]
    tm = min(512, n if per_token else tokens_per_req)
    row = pl.BlockSpec((tm, D_MODEL), lambda i: (i, 0))
    return pl.pallas_call(
        _resid_kernel,
        grid=(n // tm,),
        in_specs=[row, row, _mod_spec(per_token, tm, tokens_per_req)],
        out_specs=row,
        out_shape=jax.ShapeDtypeStruct((n, D_MODEL), F32),
        compiler_params=_cparams("parallel"),
        name="peer_residual",
    )(x, y, g2)


def _final_kernel(x_ref, g_ref, o_ref):
    x = x_ref[...]
    o_ref[...] = x * lax.rsqrt(jnp.mean(x * x, axis=-1, keepdims=True) + EPS) * g_ref[...]


def _final_call(x, g):
    n = x.shape[0]
    tm = min(512, n)
    return pl.pallas_call(
        _final_kernel,
        grid=(n // tm,),
        in_specs=[pl.BlockSpec((tm, D_MODEL), lambda i: (i, 0)), pl.BlockSpec((1, D_MODEL), lambda i: (0, 0))],
        out_specs=pl.BlockSpec((tm, D_MODEL), lambda i: (i, 0)),
        out_shape=jax.ShapeDtypeStruct((n, D_MODEL), F32),
        compiler_params=_cparams("parallel"),
        name="final_norm",
    )(x, g)


def _split_w_in(w_in_l, gate_b_l):
    cuts = np.cumsum([A_WIDTH + 2 * A_KV_WIDTH, 2 * M_WIDTH, M_WIDTH, M_WIDTH]).tolist()
    wa = w_in_l[:, :cuts[0]].astype(BF16)
    wqk = w_in_l[:, cuts[0]:cuts[1]].astype(BF16)
    wv = w_in_l[:, cuts[1]:cuts[2]].astype(BF16)
    wo = w_in_l[:, cuts[2]:cuts[3]].astype(BF16)
    ng = 2 * M_HEADS
    wg = jnp.pad(w_in_l[:, cuts[3]:], ((0, 0), (0, LANES - ng))).astype(BF16)
    gb = jnp.pad(gate_b_l.astype(F32), (0, LANES - ng)).reshape(1, LANES)
    return wa, wqk, wv, wo, wg, gb


def _layer(x, mods, per_token, batch, seq, t_valid, lw, bias_p, bias_c, bias_n, kv_cache, conv0, state,
           after=None):
    (norm_mix, norm_ffn, w_in, conv_w, conv_b, gate_b, sinks, m_norm, w_out, peer_query, peer_keys,
     peer_u, peer_v) = lw
    if after is not None:
        x, _ = lax.optimization_barrier((x, after))
    sh1, sc1, g1, sh2, sc2, g2 = mods
    wa, wqk, wv, wo, wg, gb = _split_w_in(w_in, gate_b)
    qkv, qkm, vm, om, gc = _in_call(x, sc1, sh1, norm_mix.reshape(1, -1), wa, wqk, wv, wo, wg, gb,
                                    per_token, seq)
    if kv_cache is None:
        att = _attn_p_call(qkv, bias_p, sinks, batch, seq)
        kv3 = qkv.reshape(batch, seq, -1)
        new_k = kv3[:, seq - WINDOW:, A_WIDTH:A_WIDTH + A_KV_WIDTH]
        new_v = kv3[:, seq - WINDOW:, A_WIDTH + A_KV_WIDTH:]
        chunk = M_CHUNK
    else:
        att, new_k, new_v = _attn_s_call(qkv, kv_cache[0], kv_cache[1], bias_c, bias_n, sinks, t_valid)
        chunk = seq
    c0, n0, m0 = state
    mo, c_new, n_new, m_new = _mlstm_call(qkm, vm, om, gc, conv_w, conv_b.reshape(1, -1),
                                          m_norm.reshape(1, -1), conv0, c0, n0, m0,
                                          batch, seq, chunk, min(t_valid, chunk))
    new_conv = qkm.reshape(batch, seq, -1)[:, t_valid - (CONV_W - 1):t_valid]
    x_mid, h2, qp = _out_call(att, mo, x, g1, sc2, sh2, norm_ffn.reshape(1, -1),
                              w_out[:A_WIDTH].astype(BF16), w_out[A_WIDTH:].astype(BF16),
                              peer_query.astype(BF16), per_token, seq)
    idx, gw = _select_call(qp, peer_keys.astype(BF16))
    y = _sc_expert_call(idx, gw, h2, peer_u, peer_v)
    x_new = _resid_call(x_mid, y, g2, per_token, seq)
    new_k = new_k.reshape(batch, WINDOW, A_KV_HEADS, A_HEAD_DIM)
    new_v = new_v.reshape(batch, WINDOW, A_KV_HEADS, A_HEAD_DIM)
    return x_new, (new_k, new_v, new_conv, c_new, n_new, m_new[:, :M_HEADS, 0]), idx


def _prompt_group_sizes(n_req):
    if n_req <= 1:
        return [n_req]
    rest = n_req - 1
    n_rest = min(PROMPT_GROUPS - 1, rest)
    sizes = [rest // n_rest + (1 if i < rest % n_rest else 0) for i in range(n_rest)]
    return [1] + sizes


def kernel(x_prompt, x_sample, c_prompt, c_sample, cache_k, cache_v, state_conv, state_C, state_n, state_m, rel_bias, w_ada, b_ada, norm_mix, norm_ffn, w_in, conv_w, conv_b, gate_b, attn_sinks, m_norm, w_out, peer_query, peer_keys, peer_u, peer_v, norm_final):
    depth = w_ada.shape[0]
    bp, tp, d = x_prompt.shape
    bs, ts, _ = x_sample.shape
    assert tp % WINDOW == 0 and tp % M_CHUNK == 0 and ts <= SAMPLE_PAD and ts >= CONV_W - 1

    mod_all = _ada_call(jnp.concatenate([c_prompt, c_sample], axis=0), w_ada, b_ada)

    qi = np.arange(WINDOW)[:, None]
    bias_p = _bias_call(rel_bias, qi + WINDOW - np.arange(2 * WINDOW)[None, :])
    qs = np.arange(SAMPLE_PAD)[:, None]
    bias_c = _bias_call(rel_bias, qs + WINDOW - np.arange(WINDOW)[None, :])
    bias_n = _bias_call(rel_bias, qs - np.arange(SAMPLE_PAD)[None, :])

    sizes = _prompt_group_sizes(bp)
    starts = np.cumsum([0] + sizes).tolist()
    xg = [x_prompt[starts[g]:starts[g + 1]].reshape(sizes[g] * tp, d) for g in range(len(sizes))]
    xs = jnp.pad(x_sample, ((0, 0), (0, SAMPLE_PAD - ts), (0, 0))).reshape(bs * SAMPLE_PAD, d)
    halo_pad = ((0, 0), (SUBLANES - (CONV_W - 1), 0), (0, 0))

    st_p, st_s = [], []
    for l in range(depth):
        lw = (norm_mix[l], norm_ffn[l], w_in[l], conv_w[l], conv_b[l], gate_b[l], attn_sinks[l], m_norm[l],
              w_out[l], peer_query[l], peer_keys[l], _pack_table(peer_u[l]), _pack_table(peer_v[l]))
        mod_s = [jnp.repeat(m, SAMPLE_PAD, axis=0) for m in jnp.split(mod_all[l, bp:], 6, axis=-1)]
        sp_groups = []
        for g, bg in enumerate(sizes):
            mod_g = [m.reshape(bg, 1, d) for m in jnp.split(mod_all[l, starts[g]:starts[g + 1]], 6, axis=-1)]
            zero_state = (jnp.zeros((bg, M_HEADS, M_HEAD_DIM, M_HEAD_DIM), F32),
                          jnp.zeros((bg, M_HEADS, M_HEAD_DIM), F32),
                          jnp.zeros((bg, SUBLANES, LANES), F32))
            zero_conv = jnp.zeros((bg, SUBLANES, 2 * M_WIDTH), F32)
            xg[g], sp, last_idx = _layer(xg[g], mod_g, False, bg, tp, tp, lw, bias_p, None, None, None,
                                         zero_conv, zero_state)
            sp_groups.append(sp)
        st_p.append([jnp.concatenate([sp[i] for sp in sp_groups], axis=0) for i in range(6)])
        state_s = (state_C[l].astype(F32), state_n[l].astype(F32),
                   jnp.broadcast_to(jnp.pad(state_m[l].astype(F32), ((0, 0), (0, SUBLANES - M_HEADS)))[:, :, None],
                                    (bs, SUBLANES, LANES)))
        kv_cache = (cache_k[l].reshape(bs, WINDOW, A_KV_WIDTH), cache_v[l].reshape(bs, WINDOW, A_KV_WIDTH))
        xs, ss, _ = _layer(xs, mod_s, True, bs, SAMPLE_PAD, ts, lw, None, bias_c, bias_n, kv_cache,
                           jnp.pad(state_conv[l].astype(F32), halo_pad), state_s, after=last_idx)
        st_s.append(ss)

    gfin = norm_final.reshape(1, d)
    y_prompt = jnp.concatenate([_final_call(x, gfin).reshape(bg, tp, d) for x, bg in zip(xg, sizes)], axis=0)
    y_sample = _final_call(xs, gfin).reshape(bs, SAMPLE_PAD, d)[:, :ts]
    outs_p = [jnp.stack([s[i] for s in st_p]) for i in range(6)]
    outs_s = [jnp.stack([s[i] for s in st_s]) for i in range(6)]
    return (y_prompt, y_sample, *outs_p, *outs_s)
```

```python
import functools
import math

import numpy as np
import jax
import jax.numpy as jnp
from jax import lax
from jax.experimental import pallas as pl
from jax.experimental.pallas import tpu as pltpu
from jax.experimental.pallas import tpu_sc as plsc

F32 = jnp.float32
BF16 = jnp.bfloat16
HIGHEST = lax.Precision.HIGHEST

D_MODEL = 1024
A_HEADS = 8
A_KV_HEADS = 2
A_GROUP = A_HEADS // A_KV_HEADS
A_HEAD_DIM = 64
A_WIDTH = A_HEADS * A_HEAD_DIM
A_KV_WIDTH = A_KV_HEADS * A_HEAD_DIM
WINDOW = 128
ATT_SCALE = A_HEAD_DIM ** -0.5
N_BUCKETS = 32
MAX_DISTANCE = WINDOW
M_HEADS = 4
M_HEAD_DIM = 128
M_WIDTH = M_HEADS * M_HEAD_DIM
CONV_W = 4
M_CHUNK = 64
N_KEYS = 128
P_HEADS = 8
P_TOPK = 16
P_KEY_DIM = 256
P_HALF = P_KEY_DIM // 2
EPS = 1e-6
NEG_INF = -1e30

LANES = 128
SUBLANES = 8
SAMPLE_PAD = SUBLANES
VMEM_LIMIT = 48 * 1024 * 1024
PROMPT_GROUPS = 4

NT_DIMS = (((1,), (1,)), ((), ()))
TN_DIMS = (((0,), (0,)), ((), ()))


def _cparams(*sem):
    return pltpu.CompilerParams(dimension_semantics=sem, vmem_limit_bytes=VMEM_LIMIT)


def _bdot(a, b):
    return jnp.dot(a.astype(BF16), b.astype(BF16), preferred_element_type=F32)


def _bdot_nt(a, b):
    return lax.dot_general(a.astype(BF16), b.astype(BF16), NT_DIMS, preferred_element_type=F32)


def _sigmoid(x):
    return 1.0 / (1.0 + jnp.exp(-x))


def _log_sigmoid(x):
    return jnp.minimum(x, 0.0) - jnp.log1p(jnp.exp(-jnp.abs(x)))


def _gelu_tanh(x):
    c = math.sqrt(2.0 / math.pi)
    return x * (0.5 * (1.0 + jnp.tanh(c * (x + 0.044715 * (x * x * x)))))


def _ada_kernel(c_ref, w_ref, b_ref, o_ref):
    c = c_ref[...]
    s = c * _sigmoid(c)
    o_ref[...] = jnp.dot(s, w_ref[...], precision=HIGHEST, preferred_element_type=F32) + b_ref[...]


def _ada_call(c_all, w_ada, b_ada):
    depth, d, n6 = w_ada.shape
    rows = c_all.shape[0]
    bn = 1024
    return pl.pallas_call(
        _ada_kernel,
        grid=(depth, n6 // bn),
        in_specs=[
            pl.BlockSpec((rows, d), lambda l, j: (0, 0)),
            pl.BlockSpec((None, d, bn), lambda l, j: (l, 0, j)),
            pl.BlockSpec((None, 1, bn), lambda l, j: (l, 0, j)),
        ],
        out_specs=pl.BlockSpec((None, rows, bn), lambda l, j: (l, 0, j)),
        out_shape=jax.ShapeDtypeStruct((depth, rows, n6), F32),
        compiler_params=_cparams("parallel", "parallel"),
        name="ada_mod",
    )(c_all, w_ada, b_ada.reshape(depth, 1, n6))


def _mod_spec(per_token, tm, tokens_per_req):
    if per_token:
        return pl.BlockSpec((tm, D_MODEL), lambda i: (i, 0))
    tiles = tokens_per_req // tm
    return pl.BlockSpec((None, 1, D_MODEL), lambda i: (i // tiles, 0, 0))


def _in_kernel(x_ref, sc_ref, sh_ref, g_ref, wa_ref, wqk_ref, wv_ref, wo_ref, wg_ref, gb_ref,
               qkv_ref, qkm_ref, v_ref, o_ref, gc_ref):
    x = x_ref[...]
    y = x * lax.rsqrt(jnp.mean(x * x, axis=-1, keepdims=True) + EPS) * g_ref[...]
    h = (y * (1.0 + sc_ref[...]) + sh_ref[...]).astype(BF16)
    qkv_ref[...] = jnp.dot(h, wa_ref[...], preferred_element_type=F32)
    qkm_ref[...] = jnp.dot(h, wqk_ref[...], preferred_element_type=F32)
    v_ref[...] = jnp.dot(h, wv_ref[...], preferred_element_type=F32)
    o_ref[...] = jnp.dot(h, wo_ref[...], preferred_element_type=F32)
    g = jnp.dot(h, wg_ref[...], preferred_element_type=F32) + gb_ref[...]
    lane = lax.broadcasted_iota(jnp.int32, g.shape, 1)
    gc_ref[...] = jnp.where(lane < M_HEADS, g, jnp.where(lane < 2 * M_HEADS, _log_sigmoid(g), 0.0))


def _in_call(x, sc, sh, gnorm, wa, wqk, wv, wo, wg, gb, per_token, tokens_per_req):
    n = x.shape[0]
    tm = min(512, n if per_token else tokens_per_req)
    mod = _mod_spec(per_token, tm, tokens_per_req)
    full = lambda shape: pl.BlockSpec(shape, lambda i: (0,) * len(shape))
    row = lambda w: pl.BlockSpec((tm, w), lambda i: (i, 0))
    return pl.pallas_call(
        _in_kernel,
        grid=(n // tm,),
        in_specs=[row(D_MODEL), mod, mod, full((1, D_MODEL)), full(wa.shape), full(wqk.shape),
                  full(wv.shape), full(wo.shape), full(wg.shape), full((1, LANES))],
        out_specs=[row(wa.shape[1]), row(wqk.shape[1]), row(wv.shape[1]), row(wo.shape[1]), row(LANES)],
        out_shape=[jax.ShapeDtypeStruct((n, w), F32)
                   for w in (wa.shape[1], wqk.shape[1], wv.shape[1], wo.shape[1], LANES)],
        compiler_params=_cparams("parallel"),
        name="in_proj",
    )(x, sc, sh, gnorm, wa, wqk, wv, wo, wg, gb)


def _t5_bucket_np(dist):
    n = np.maximum(dist, 0)
    max_exact = N_BUCKETS // 2
    nf = np.maximum(n, 1).astype(np.float64)
    large = max_exact + (np.log(nf / max_exact) / math.log(MAX_DISTANCE / max_exact)
                         * (N_BUCKETS - max_exact)).astype(np.int32)
    return np.where(n < max_exact, n, np.minimum(large, N_BUCKETS - 1)).astype(np.int32)


def _bias_kernel(bucket_ref, rel_ref, o_ref):
    bucket = bucket_ref[...]
    for h in range(A_HEADS):
        acc = jnp.zeros(bucket.shape, F32)
        for b in range(N_BUCKETS):
            acc = jnp.where(bucket == b, rel_ref[b, h], acc)
        o_ref[h] = acc


def _bias_call(rel_bias, dist):
    bucket = jnp.asarray(_t5_bucket_np(dist))
    nq, nk = dist.shape
    return pl.pallas_call(
        _bias_kernel,
        in_specs=[pl.BlockSpec((nq, nk), lambda: (0, 0)),
                  pl.BlockSpec(memory_space=pltpu.SMEM)],
        out_specs=pl.BlockSpec((A_HEADS, nq, nk), lambda: (0, 0, 0)),
        out_shape=jax.ShapeDtypeStruct((A_HEADS, nq, nk), F32),
        name="t5_bias",
    )(bucket, rel_bias)


def _attn_p_kernel(q_ref, kp_ref, kc_ref, vp_ref, vc_ref, bias_ref, sink_ref, o_ref):
    i = pl.program_id(1)
    qi = lax.broadcasted_iota(jnp.int32, (WINDOW, WINDOW), 0)
    kj = lax.broadcasted_iota(jnp.int32, (WINDOW, WINDOW), 1)
    valid_prev = jnp.logical_and(kj > qi, i > 0)
    valid_cur = kj <= qi
    q = q_ref[...]
    for h in range(A_HEADS):
        kv = h // A_GROUP
        qh = q[:, h * A_HEAD_DIM:(h + 1) * A_HEAD_DIM]
        sl = slice(kv * A_HEAD_DIM, (kv + 1) * A_HEAD_DIM)
        bias = bias_ref[h]
        sp = _bdot_nt(qh, kp_ref[:, sl]) * ATT_SCALE + bias[:, :WINDOW]
        sc = _bdot_nt(qh, kc_ref[:, sl]) * ATT_SCALE + bias[:, WINDOW:]
        sp = jnp.where(valid_prev, sp, NEG_INF)
        sc = jnp.where(valid_cur, sc, NEG_INF)
        sink = sink_ref[0, h]
        mx = jnp.maximum(jnp.maximum(jnp.max(sp, axis=-1, keepdims=True),
                                     jnp.max(sc, axis=-1, keepdims=True)), sink)
        ep = jnp.exp(sp - mx)
        ec = jnp.exp(sc - mx)
        den = (jnp.sum(ep, axis=-1, keepdims=True) + jnp.sum(ec, axis=-1, keepdims=True)
               + jnp.exp(sink - mx))
        o = _bdot(ep / den, vp_ref[:, sl]) + _bdot(ec / den, vc_ref[:, sl])
        o_ref[:, h * A_HEAD_DIM:(h + 1) * A_HEAD_DIM] = o


def _attn_p_call(qkv, bias, sinks, batch, seq):
    nb = seq // WINDOW
    n = batch * seq
    kcol = A_WIDTH // A_KV_WIDTH
    vcol = kcol + 1
    cur = lambda col: pl.BlockSpec((WINDOW, A_KV_WIDTH), lambda b, i: (b * nb + i, col))
    prev = lambda col: pl.BlockSpec((WINDOW, A_KV_WIDTH),
                                    lambda b, i: (b * nb + jnp.maximum(i - 1, 0), col))
    return pl.pallas_call(
        _attn_p_kernel,
        grid=(batch, nb),
        in_specs=[pl.BlockSpec((WINDOW, A_WIDTH), lambda b, i: (b * nb + i, 0)),
                  prev(kcol), cur(kcol), prev(vcol), cur(vcol),
                  pl.BlockSpec((A_HEADS, WINDOW, 2 * WINDOW), lambda b, i: (0, 0, 0)),
                  pl.BlockSpec(memory_space=pltpu.SMEM)],
        out_specs=pl.BlockSpec((WINDOW, A_WIDTH), lambda b, i: (b * nb + i, 0)),
        out_shape=jax.ShapeDtypeStruct((n, A_WIDTH), F32),
        compiler_params=_cparams("parallel", "parallel"),
        name="swa_prompt",
    )(qkv, qkv, qkv, qkv, qkv, bias, sinks.reshape(1, A_HEADS))


def _attn_s_kernel(n_new, qkv_ref, ck_ref, cv_ref, bc_ref, bn_ref, sink_ref,
                   o_ref, nk_ref, nv_ref, kk_s, vv_s):
    qkv = qkv_ref[...]
    knew = qkv[:, A_WIDTH:A_WIDTH + A_KV_WIDTH]
    vnew = qkv[:, A_WIDTH + A_KV_WIDTH:A_WIDTH + 2 * A_KV_WIDTH]
    ck = ck_ref[...]
    cv = cv_ref[...]
    qi = lax.broadcasted_iota(jnp.int32, (SAMPLE_PAD, WINDOW), 0)
    kj = lax.broadcasted_iota(jnp.int32, (SAMPLE_PAD, WINDOW), 1)
    valid_c = kj > qi
    qcol = lax.broadcasted_iota(jnp.int32, (SAMPLE_PAD, 1), 0)
    for h in range(A_HEADS):
        kv = h // A_GROUP
        sl = slice(kv * A_HEAD_DIM, (kv + 1) * A_HEAD_DIM)
        qh = qkv[:, h * A_HEAD_DIM:(h + 1) * A_HEAD_DIM]
        s_c = lax.dot_general(qh, ck[:, sl], NT_DIMS, precision=HIGHEST,
                              preferred_element_type=F32) * ATT_SCALE + bc_ref[h]
        s_c = jnp.where(valid_c, s_c, NEG_INF)
        bn = bn_ref[h]
        s_n = []
        for j in range(n_new):
            sj = jnp.sum(qh * knew[j:j + 1, sl], axis=-1, keepdims=True) * ATT_SCALE + bn[:, j:j + 1]
            s_n.append(jnp.where(qcol >= j, sj, NEG_INF))
        sink = sink_ref[0, h]
        mx = jnp.maximum(jnp.max(s_c, axis=-1, keepdims=True), sink)
        for sj in s_n:
            mx = jnp.maximum(mx, sj)
        e_c = jnp.exp(s_c - mx)
        den = jnp.sum(e_c, axis=-1, keepdims=True) + jnp.exp(sink - mx)
        o = jnp.dot(e_c, cv[:, sl], precision=HIGHEST, preferred_element_type=F32)
        for j, sj in enumerate(s_n):
            ej = jnp.exp(sj - mx)
            den = den + ej
            o = o + ej * vnew[j:j + 1, sl]
        o_ref[:, h * A_HEAD_DIM:(h + 1) * A_HEAD_DIM] = o / den
    kk_s[0:WINDOW, :] = ck
    kk_s[WINDOW:WINDOW + SAMPLE_PAD, :] = knew
    vv_s[0:WINDOW, :] = cv
    vv_s[WINDOW:WINDOW + SAMPLE_PAD, :] = vnew
    nk_ref[...] = kk_s[n_new:n_new + WINDOW, :]
    nv_ref[...] = vv_s[n_new:n_new + WINDOW, :]


def _attn_s_call(qkv, ck, cv, bias_c, bias_n, sinks, n_new):
    nreq = ck.shape[0]
    wq = qkv.shape[1]
    full3 = lambda shape: pl.BlockSpec(shape, lambda b: (0, 0, 0))
    cache = pl.BlockSpec((None, WINDOW, A_KV_WIDTH), lambda b: (b, 0, 0))
    return pl.pallas_call(
        functools.partial(_attn_s_kernel, n_new),
        grid=(nreq,),
        in_specs=[pl.BlockSpec((SAMPLE_PAD, wq), lambda b: (b, 0)), cache, cache,
                  full3(bias_c.shape), full3(bias_n.shape),
                  pl.BlockSpec(memory_space=pltpu.SMEM)],
        out_specs=[pl.BlockSpec((SAMPLE_PAD, A_WIDTH), lambda b: (b, 0)), cache, cache],
        out_shape=[jax.ShapeDtypeStruct((nreq * SAMPLE_PAD, A_WIDTH), F32),
                   jax.ShapeDtypeStruct(ck.shape, F32), jax.ShapeDtypeStruct(cv.shape, F32)],
        scratch_shapes=[pltpu.VMEM((WINDOW + SAMPLE_PAD, A_KV_WIDTH), F32),
                        pltpu.VMEM((WINDOW + SAMPLE_PAD, A_KV_WIDTH), F32)],
        compiler_params=_cparams("parallel"),
        name="swa_sample",
    )(qkv, ck, cv, bias_c, bias_n, sinks.reshape(1, A_HEADS))


def _mlstm_kernel(chunk, t_valid, qk_ref, v_ref, og_ref, gc_ref, cw_ref, cb_ref, mn_ref,
                  conv0_ref, c0_ref, n0_ref, m0_ref,
                  out_ref, cout_ref, nout_ref, mout_ref,
                  xp_s, c_s, n_s, m_s):
    step = pl.program_id(1)
    halo = SUBLANES

    @pl.when(step == 0)
    def _():
        xp_s[0:halo, :] = conv0_ref[...]
        c_s[...] = c0_ref[...]
        n_s[...] = n0_ref[...]
        m_s[...] = m0_ref[...]

    xp_s[halo:halo + chunk, :] = qk_ref[...]
    cw = cw_ref[...]
    y = cb_ref[...]
    for i in range(CONV_W):
        off = halo - (CONV_W - 1) + i
        y = y + xp_s[off:off + chunk, :] * cw[i:i + 1, :]
    xp_s[0:halo, :] = xp_s[chunk:chunk + halo, :]
    y = y * _sigmoid(y)
    q_all = y[:, :M_WIDTH]
    k_all = y[:, M_WIDTH:] * (M_HEAD_DIM ** -0.5)

    g = gc_ref[...]
    if t_valid < chunk:
        row = lax.broadcasted_iota(jnp.int32, g.shape, 0)
        lane = lax.broadcasted_iota(jnp.int32, g.shape, 1)
        g = jnp.where(row < t_valid, g, jnp.where(lane < M_HEADS, NEG_INF, 0.0))
    tr = lax.broadcasted_iota(jnp.int32, (chunk, chunk), 0)
    tc = lax.broadcasted_iota(jnp.int32, (chunk, chunk), 1)
    causal = tr >= tc
    tri = causal.astype(F32)
    bcol = jnp.dot(tri, g, precision=HIGHEST, preferred_element_type=F32)
    er = lax.broadcasted_iota(jnp.int32, (SUBLANES, LANES), 0)
    ec = lax.broadcasted_iota(jnp.int32, (SUBLANES, LANES), 1)
    eye = (er == ec).astype(F32)
    g_rows = lax.dot_general(eye, g, NT_DIMS, precision=HIGHEST, preferred_element_type=F32)
    b_rows = lax.dot_general(eye, bcol, NT_DIMS, precision=HIGHEST, preferred_element_type=F32)

    for h in range(M_HEADS):
        hs = slice(h * M_HEAD_DIM, (h + 1) * M_HEAD_DIM)
        b_c = bcol[:, M_HEADS + h:M_HEADS + h + 1]
        ig_c = g[:, h:h + 1]
        b_r = b_rows[M_HEADS + h:M_HEADS + h + 1, :]
        ig_r = g_rows[h:h + 1, :]
        m_prev = m_s[h:h + 1, 0:1]
        logw = jnp.where(causal, b_c - b_r + ig_r, -jnp.inf)
        inter = b_c + m_prev
        m_t = jnp.maximum(inter, jnp.max(logw, axis=-1, keepdims=True))
        w = jnp.exp(logw - m_t)
        a = jnp.exp(inter - m_t)
        q = q_all[:, hs]
        k = k_all[:, hs]
        v = v_ref[:, hs]
        cmat = c_s[h]
        nvec = n_s[h:h + 1, :]
        wqk = w * _bdot_nt(q, k)
        num = _bdot(wqk, v) + a * _bdot_nt(q, cmat)
        den = jnp.sum(wqk, axis=-1, keepdims=True) + a * jnp.sum(q * nvec, axis=-1, keepdims=True)
        hh = num / jnp.maximum(jnp.abs(den), jnp.exp(-m_t))
        m_new = m_t[chunk - 1:chunk, :]
        b_last = b_c[chunk - 1:chunk, :]
        wl = jnp.exp(b_last - b_c + ig_c - m_new)
        al = jnp.exp(b_last + m_prev - m_new)
        c_s[h] = al * cmat + lax.dot_general((v * wl).astype(BF16), k.astype(BF16), TN_DIMS,
                                             preferred_element_type=F32)
        n_s[h:h + 1, :] = al * nvec + jnp.sum(wl * k, axis=0, keepdims=True)
        m_s[h:h + 1, :] = jnp.broadcast_to(m_new, (1, LANES))
        hn = hh * lax.rsqrt(jnp.mean(hh * hh, axis=-1, keepdims=True) + EPS) * mn_ref[:, hs]
        out_ref[:, hs] = _sigmoid(og_ref[:, hs]) * hn

    @pl.when(step == pl.num_programs(1) - 1)
    def _():
        cout_ref[...] = c_s[...]
        nout_ref[...] = n_s[...]
        mout_ref[...] = m_s[...]


def _mlstm_call(qk, v, og, gc, conv_w, conv_b, m_norm, conv0, c0, n0, m0, batch, seq, chunk, t_valid):
    nc = seq // chunk
    n = batch * seq
    row = lambda w: pl.BlockSpec((chunk, w), lambda b, c: (b * nc + c, 0))
    full2 = lambda shape: pl.BlockSpec(shape, lambda b, c: (0, 0))
    per_b = lambda shape: pl.BlockSpec((None,) + shape, lambda b, c: (b,) + (0,) * len(shape))
    dh = M_HEAD_DIM
    return pl.pallas_call(
        functools.partial(_mlstm_kernel, chunk, t_valid),
        grid=(batch, nc),
        in_specs=[row(2 * M_WIDTH), row(M_WIDTH), row(M_WIDTH), row(LANES),
                  full2((CONV_W, 2 * M_WIDTH)), full2((1, 2 * M_WIDTH)), full2((1, M_WIDTH)),
                  per_b((SUBLANES, 2 * M_WIDTH)), per_b((M_HEADS, dh, dh)), per_b((M_HEADS, dh)),
                  per_b((SUBLANES, LANES))],
        out_specs=[row(M_WIDTH), per_b((M_HEADS, dh, dh)), per_b((M_HEADS, dh)), per_b((SUBLANES, LANES))],
        out_shape=[jax.ShapeDtypeStruct((n, M_WIDTH), F32),
                   jax.ShapeDtypeStruct((batch, M_HEADS, dh, dh), F32),
                   jax.ShapeDtypeStruct((batch, M_HEADS, dh), F32),
                   jax.ShapeDtypeStruct((batch, SUBLANES, LANES), F32)],
        scratch_shapes=[pltpu.VMEM((SUBLANES + chunk, 2 * M_WIDTH), F32),
                        pltpu.VMEM((M_HEADS, dh, dh), F32),
                        pltpu.VMEM((M_HEADS, dh), F32),
                        pltpu.VMEM((SUBLANES, LANES), F32)],
        compiler_params=_cparams("parallel", "arbitrary"),
        name="mlstm",
    )(qk, v, og, gc, conv_w, conv_b, m_norm, conv0, c0, n0, m0)


def _out_kernel(att_ref, mo_ref, x_ref, g1_ref, sc_ref, sh_ref, gn_ref, wa_ref, wm_ref, wq_ref,
                xo_ref, h2_ref, qp_ref):
    mix = (jnp.dot(att_ref[...].astype(BF16), wa_ref[...], preferred_element_type=F32)
           + jnp.dot(mo_ref[...].astype(BF16), wm_ref[...], preferred_element_type=F32))
    x = x_ref[...] + g1_ref[...] * mix
    xo_ref[...] = x
    y = x * lax.rsqrt(jnp.mean(x * x, axis=-1, keepdims=True) + EPS) * gn_ref[...]
    h2 = y * (1.0 + sc_ref[...]) + sh_ref[...]
    h2_ref[...] = h2
    qp_ref[...] = jnp.dot(h2.astype(BF16), wq_ref[...], preferred_element_type=F32)


def _out_call(att, mo, x, g1, sc, sh, gnorm, wa, wm, wq, per_token, tokens_per_req):
    n = x.shape[0]
    tm = min(256, n if per_token else tokens_per_req)
    mod = _mod_spec(per_token, tm, tokens_per_req)
    full = lambda shape: pl.BlockSpec(shape, lambda i: (0,) * len(shape))
    row = lambda w: pl.BlockSpec((tm, w), lambda i: (i, 0))
    nq = wq.shape[1]
    return pl.pallas_call(
        _out_kernel,
        grid=(n // tm,),
        in_specs=[row(A_WIDTH), row(M_WIDTH), row(D_MODEL), mod, mod, mod, full((1, D_MODEL)),
                  full(wa.shape), full(wm.shape), full(wq.shape)],
        out_specs=[row(D_MODEL), row(D_MODEL), row(nq)],
        out_shape=[jax.ShapeDtypeStruct((n, D_MODEL), F32), jax.ShapeDtypeStruct((n, D_MODEL), F32),
                   jax.ShapeDtypeStruct((n, nq), F32)],
        compiler_params=_cparams("parallel"),
        name="out_proj",
    )(att, mo, x, g1, sc, sh, gnorm, wa, wm, wq)


def _pk_cells():
    return [(a, b) for a in range(P_TOPK) for b in range(P_TOPK) if (a + 1) * (b + 1) <= P_TOPK]


PK_CELL_ROWS = 64


def _pk_expand_mats():
    cells = _pk_cells()
    e0 = np.zeros((PK_CELL_ROWS, LANES), np.float32)
    e1 = np.zeros((PK_CELL_ROWS, LANES), np.float32)
    for j, (a, b) in enumerate(cells):
        e0[j, a] = 1.0
        e1[j, b] = 1.0
    return e0, e1, len(cells)


def _top_rows(s, rowf, rounds):
    n_rows = s.shape[0]
    vals, idxs = [], []
    for _ in range(rounds):
        m = jnp.max(s, axis=0, keepdims=True)
        i = jnp.min(jnp.where(s == m, rowf, float(n_rows)), axis=0, keepdims=True)
        vals.append(m)
        idxs.append(i)
        s = jnp.where(rowf == i, -jnp.inf, s)
    return jnp.concatenate(vals, axis=0), jnp.concatenate(idxs, axis=0)


def _select_kernel(n_cells, qp_ref, keys_ref, e0_ref, e1_ref, idx_ref, gw_ref, idx_s, gw_s):
    tm = qp_ref.shape[0]
    keyf = lax.broadcasted_iota(jnp.int32, (N_KEYS, tm), 0).astype(F32)
    cellf = lax.broadcasted_iota(jnp.int32, (PK_CELL_ROWS, tm), 0).astype(F32)
    e0 = e0_ref[...]
    e1 = e1_ref[...]
    pad = jnp.zeros((LANES - P_TOPK, tm), F32)

    def head(h, carry):
        h = jnp.asarray(h, jnp.int32)
        sub = []
        for c in range(2):
            col = pl.multiple_of((h * 2 + c) * P_HALF, P_HALF)
            s = _bdot_nt(keys_ref[h, c], qp_ref[:, pl.ds(col, P_HALF)])
            sub.append(_top_rows(s, keyf, P_TOPK))
        (v0, i0), (v1, i1) = sub
        expand = lambda e, x: jnp.dot(e, jnp.concatenate([x, pad], axis=0), precision=HIGHEST,
                                      preferred_element_type=F32)
        cand = expand(e0, v0) + expand(e1, v1)
        cidx = expand(e0, i0 * float(N_KEYS)) + expand(e1, i1)
        cand = jnp.where(cellf < n_cells, cand, -jnp.inf)
        best, eidx = [], []
        for _ in range(P_TOPK):
            m = jnp.max(cand, axis=0, keepdims=True)
            j = jnp.min(jnp.where(cand == m, cellf, float(PK_CELL_ROWS)), axis=0, keepdims=True)
            hit = cellf == j
            eidx.append(jnp.max(jnp.where(hit, cidx, -1.0), axis=0, keepdims=True))
            best.append(m)
            cand = jnp.where(hit, -jnp.inf, cand)
        best = jnp.concatenate(best, axis=0)
        e = jnp.exp(best - best[0:1, :])
        row0 = pl.multiple_of(h * P_TOPK, P_TOPK)
        gw_s[pl.ds(row0, P_TOPK), :] = e / jnp.sum(e, axis=0, keepdims=True)
        idx_s[pl.ds(row0, P_TOPK), :] = jnp.concatenate(eidx, axis=0)
        return carry

    lax.fori_loop(0, P_HEADS, head, 0)
    idx_ref[...] = idx_s[...].T.astype(jnp.int32)
    gw_ref[...] = gw_s[...].T


def _select_call(qp, keys_bf16):
    n = qp.shape[0]
    tm = min(LANES, n)
    e0, e1, n_cells = _pk_expand_mats()
    full = lambda shape: pl.BlockSpec(shape, lambda i: (0,) * len(shape))
    return pl.pallas_call(
        functools.partial(_select_kernel, n_cells),
        grid=(n // tm,),
        in_specs=[pl.BlockSpec((tm, qp.shape[1]), lambda i: (i, 0)), full(keys_bf16.shape),
                  full((PK_CELL_ROWS, LANES)), full((PK_CELL_ROWS, LANES))],
        out_specs=[pl.BlockSpec((tm, LANES), lambda i: (i, 0)), pl.BlockSpec((tm, LANES), lambda i: (i, 0))],
        out_shape=[jax.ShapeDtypeStruct((n, LANES), jnp.int32), jax.ShapeDtypeStruct((n, LANES), F32)],
        scratch_shapes=[pltpu.VMEM((P_HEADS * P_TOPK, tm), F32), pltpu.VMEM((P_HEADS * P_TOPK, tm), F32)],
        compiler_params=_cparams("parallel"),
        name="peer_select",
    )(qp, keys_bf16, jnp.asarray(e0), jnp.asarray(e1))


N_SEL = P_HEADS * P_TOPK

SC_CORES = 2
SC_SUBCORES = 16
SC_LANES = 16
SC_TOK_BLOCK = 8
SC_ROWS = 32
SC_ROW_BLOCK = 16
SC_ACC_CHAINS = 4
SC_ACC_ROWS = 16
SC_NBUF = 4
N_WCOL = D_MODEL // (2 * SC_LANES)


def _pack_table(t):
    e, dcol = t.shape
    tb = t.astype(BF16).reshape(e, dcol // (2 * SC_LANES), 2, SC_LANES)
    words = lax.bitcast_convert_type(jnp.swapaxes(tb, 2, 3), jnp.uint32)
    return lax.bitcast_convert_type(words, jnp.int32).reshape(e, dcol // 2)


def _sc_gelu(x):
    z = math.sqrt(2.0 / math.pi) * (x + 0.044715 * (x * x * x))
    t = 1.0 - 2.0 / (jnp.exp(2.0 * z) + 1.0)
    return x * (0.5 * (1.0 + t))


def _sc_expert_body(tokens_per_worker, idx_hbm, gw_hbm, h2_hbm, u_hbm, v_hbm, y_hbm,
                    idx_v, gw_v, x_v, o_v, buf, coef_v, tr_v, sem, in_sem, out_sem):
    wid = lax.axis_index("s") * SC_CORES + lax.axis_index("c")
    base = wid * tokens_per_worker
    lane = lax.iota(jnp.int32, SC_LANES)
    n_gather = N_SEL // SC_ROWS
    zero = jnp.zeros((SC_LANES,), F32)

    n_steps = 2 * n_gather
    assert n_steps % SC_NBUF == 0

    def gather(p, tt, i):
        table = u_hbm if i < n_gather else v_hbm
        j = i % n_gather
        slot = i % SC_NBUF
        return pltpu.make_async_copy(table.at[idx_v.at[p, tt, pl.ds(j * SC_ROWS, SC_ROWS)]], buf.at[slot],
                                     sem.at[slot])

    def unpack(w):
        lo = lax.bitcast_convert_type(lax.shift_left(w, jnp.full((SC_LANES,), 16, jnp.int32)), F32)
        hi = lax.bitcast_convert_type(w & jnp.full((SC_LANES,), -65536, jnp.int32), F32)
        return lo, hi

    def packed(w):
        return plsc.bitcast(w, BF16)

    def unpack_sum(s):
        return unpack(plsc.bitcast(s, jnp.int32))

    def act_chunk(p, tt, j, slot):
        @pl.loop(0, SC_ROWS // SC_LANES)
        def _(half):
            for rb in range(SC_LANES // SC_ROW_BLOCK):
                r0 = half * SC_LANES + rb * SC_ROW_BLOCK

                def col(c, accs):
                    w0 = pl.multiple_of(c * (2 * SC_LANES), 2 * SC_LANES)
                    xa = packed(x_v[p, tt, pl.ds(w0, SC_LANES)])
                    xb = packed(x_v[p, tt, pl.ds(w0 + SC_LANES, SC_LANES)])
                    out = []
                    for r, a in enumerate(accs):
                        ua = packed(buf[slot, r0 + r, pl.ds(w0, SC_LANES)])
                        ub = packed(buf[slot, r0 + r, pl.ds(w0 + SC_LANES, SC_LANES)])
                        lo, hi = unpack_sum(ua * xa + ub * xb)
                        out.append(a + lo + hi)
                    return tuple(out)

                accs = lax.fori_loop(0, N_WCOL // 2, col, (zero,) * SC_ROW_BLOCK)
                for r in range(SC_ROW_BLOCK):
                    tr_v[pl.ds((rb * SC_ROW_BLOCK + r) * SC_LANES, SC_LANES)] = accs[r]
            tot = zero
            for jj in range(SC_LANES):
                tot = tot + plsc.load_gather(tr_v, [lane * SC_LANES + jj])
            k0 = pl.multiple_of(j * SC_ROWS + half * SC_LANES, SC_LANES)
            coef_v[pl.ds(k0, SC_LANES)] = gw_v[p, tt, pl.ds(k0, SC_LANES)] * _sc_gelu(tot)

    def acc_chunk(p, tt, j, slot, first):
        def tree_sum(parts):
            while len(parts) > 1:
                parts = [parts[i] + parts[i + 1] for i in range(0, len(parts), 2)]
            return parts[0]

        for rb in range(SC_ROWS // SC_ACC_ROWS):
            rows = list(range(rb * SC_ACC_ROWS, (rb + 1) * SC_ACC_ROWS))
            splat = {}
            for r in rows:
                c16 = plsc.load_gather(coef_v, [jnp.full((SC_LANES,), j * SC_ROWS + r, jnp.int32)])
                splat[r] = plsc.pack(c16, c16, format=plsc.PackFormat.INTERLEAVED)
            fresh = first and rb == 0

            @plsc.parallel_loop(0, N_WCOL)
            def _(c):
                woff = pl.multiple_of(c * SC_LANES, SC_LANES)
                xoff = pl.multiple_of(c * (2 * SC_LANES), 2 * SC_LANES)
                pa, pb = [], []
                for n in range(0, SC_ACC_ROWS, 2):
                    r0, r1 = rows[n], rows[n + 1]
                    s = (splat[r0] * packed(buf[slot, r0, pl.ds(woff, SC_LANES)])
                         + splat[r1] * packed(buf[slot, r1, pl.ds(woff, SC_LANES)]))
                    lo, hi = unpack_sum(s)
                    if n // 2 < SC_ACC_CHAINS:
                        pa.append(lo)
                        pb.append(hi)
                    else:
                        pa[(n // 2) % SC_ACC_CHAINS] = pa[(n // 2) % SC_ACC_CHAINS] + lo
                        pb[(n // 2) % SC_ACC_CHAINS] = pb[(n // 2) % SC_ACC_CHAINS] + hi
                sa, sb = tree_sum(pa), tree_sum(pb)
                if not fresh:
                    sa = sa + o_v[p, tt, pl.ds(xoff, SC_LANES)]
                    sb = sb + o_v[p, tt, pl.ds(xoff + SC_LANES, SC_LANES)]
                o_v[p, tt, pl.ds(xoff, SC_LANES)] = sa
                o_v[p, tt, pl.ds(xoff + SC_LANES, SC_LANES)] = sb

    n_blocks = tokens_per_worker // SC_TOK_BLOCK

    def block_start(blk):
        return pl.multiple_of(base + blk * SC_TOK_BLOCK, SC_TOK_BLOCK)

    def in_copies(blk, p):
        rows = pl.ds(block_start(blk), SC_TOK_BLOCK)
        return [pltpu.make_async_copy(idx_hbm.at[rows], idx_v.at[p], in_sem.at[p]),
                pltpu.make_async_copy(gw_hbm.at[rows], gw_v.at[p], in_sem.at[p]),
                pltpu.make_async_copy(h2_hbm.at[rows], x_v.at[p], in_sem.at[p])]

    def out_copy(blk, p):
        return pltpu.make_async_copy(o_v.at[p], y_hbm.at[pl.ds(block_start(blk), SC_TOK_BLOCK)], out_sem.at[p])

    for cp in in_copies(0, 0):
        cp.start()

    @pl.loop(0, n_blocks)
    def _(blk):
        p = lax.rem(blk, 2)
        for cp in in_copies(blk, p):
            cp.wait()

        @pl.when(blk + 1 < n_blocks)
        def _():
            for cp in in_copies(blk + 1, 1 - p):
                cp.start()

        @pl.when(blk >= 2)
        def _():
            out_copy(blk - 2, p).wait()

        ahead = SC_NBUF - 1
        for i in range(ahead):
            gather(p, 0, i).start()

        @pl.loop(0, SC_TOK_BLOCK)
        def _(tt):
            for i in range(n_steps):
                if i + ahead < n_steps:
                    gather(p, tt, i + ahead).start()
                else:
                    @pl.when(tt + 1 < SC_TOK_BLOCK)
                    def _():
                        gather(p, tt + 1, i + ahead - n_steps).start()
                gather(p, tt, i).wait()
                if i < n_gather:
                    act_chunk(p, tt, i, i % SC_NBUF)
                else:
                    acc_chunk(p, tt, i - n_gather, i % SC_NBUF, i == n_gather)

        out_copy(blk, p).start()

    for blk in (n_blocks - 2, n_blocks - 1):
        out_copy(blk, blk % 2).wait()


def _sc_expert_call(idx, gw, h2, u, v):
    n = idx.shape[0]
    workers = SC_CORES * SC_SUBCORES
    assert n % (workers * SC_TOK_BLOCK) == 0 and n // (workers * SC_TOK_BLOCK) >= 2
    mesh = plsc.VectorSubcoreMesh(core_axis_name="c", subcore_axis_name="s")
    return pl.kernel(
        functools.partial(_sc_expert_body, n // workers),
        out_type=jax.ShapeDtypeStruct((n, D_MODEL), F32),
        mesh=mesh,
        scratch_types=[pltpu.VMEM((2, SC_TOK_BLOCK, N_SEL), jnp.int32),
                       pltpu.VMEM((2, SC_TOK_BLOCK, N_SEL), F32),
                       pltpu.VMEM((2, SC_TOK_BLOCK, D_MODEL // 2), jnp.int32),
                       pltpu.VMEM((2, SC_TOK_BLOCK, D_MODEL), F32),
                       pltpu.VMEM((SC_NBUF, SC_ROWS, D_MODEL // 2), jnp.int32),
                       pltpu.VMEM((N_SEL,), F32),
                       pltpu.VMEM((SC_LANES * SC_LANES,), F32),
                       pltpu.SemaphoreType.DMA((SC_NBUF,)), pltpu.SemaphoreType.DMA((2,)),
                       pltpu.SemaphoreType.DMA((2,))],
        compiler_params=pltpu.CompilerParams(needs_layout_passes=False),
        name="peer_experts_sc",
    )(idx, gw, _pack_table(h2), u, v)


def _resid_kernel(x_ref, y_ref, g_ref, o_ref):
    o_ref[...] = x_ref[...] + g_ref[...] * y_ref[...]


def _resid_call(x, y, g2, per_token, tokens_per_req):
    n = x.shape[0]
    tm = min(512, n if per_token else tokens_per_req)
    row = pl.BlockSpec((tm, D_MODEL), lambda i: (i, 0))
    return pl.pallas_call(
        _resid_kernel,
        grid=(n // tm,),
        in_specs=[row, row, _mod_spec(per_token, tm, tokens_per_req)],
        out_specs=row,
        out_shape=jax.ShapeDtypeStruct((n, D_MODEL), F32),
        compiler_params=_cparams("parallel"),
        name="peer_residual",
    )(x, y, g2)


def _final_kernel(x_ref, g_ref, o_ref):
    x = x_ref[...]
    o_ref[...] = x * lax.rsqrt(jnp.mean(x * x, axis=-1, keepdims=True) + EPS) * g_ref[...]


def _final_call(x, g):
    n = x.shape[0]
    tm = min(512, n)
    return pl.pallas_call(
        _final_kernel,
        grid=(n // tm,),
        in_specs=[pl.BlockSpec((tm, D_MODEL), lambda i: (i, 0)), pl.BlockSpec((1, D_MODEL), lambda i: (0, 0))],
        out_specs=pl.BlockSpec((tm, D_MODEL), lambda i: (i, 0)),
        out_shape=jax.ShapeDtypeStruct((n, D_MODEL), F32),
        compiler_params=_cparams("parallel"),
        name="final_norm",
    )(x, g)


def _split_w_in(w_in_l, gate_b_l):
    cuts = np.cumsum([A_WIDTH + 2 * A_KV_WIDTH, 2 * M_WIDTH, M_WIDTH, M_WIDTH]).tolist()
    wa = w_in_l[:, :cuts[0]].astype(BF16)
    wqk = w_in_l[:, cuts[0]:cuts[1]].astype(BF16)
    wv = w_in_l[:, cuts[1]:cuts[2]].astype(BF16)
    wo = w_in_l[:, cuts[2]:cuts[3]].astype(BF16)
    ng = 2 * M_HEADS
    wg = jnp.pad(w_in_l[:, cuts[3]:], ((0, 0), (0, LANES - ng))).astype(BF16)
    gb = jnp.pad(gate_b_l.astype(F32), (0, LANES - ng)).reshape(1, LANES)
    return wa, wqk, wv, wo, wg, gb


def _layer(x, mods, per_token, batch, seq, t_valid, lw, bias_p, bias_c, bias_n, kv_cache, conv0, state,
           after=None):
    (norm_mix, norm_ffn, w_in, conv_w, conv_b, gate_b, sinks, m_norm, w_out, peer_query, peer_keys,
     peer_u, peer_v) = lw
    if after is not None:
        x, _ = lax.optimization_barrier((x, after))
    sh1, sc1, g1, sh2, sc2, g2 = mods
    wa, wqk, wv, wo, wg, gb = _split_w_in(w_in, gate_b)
    qkv, qkm, vm, om, gc = _in_call(x, sc1, sh1, norm_mix.reshape(1, -1), wa, wqk, wv, wo, wg, gb,
                                    per_token, seq)
    if kv_cache is None:
        att = _attn_p_call(qkv, bias_p, sinks, batch, seq)
        kv3 = qkv.reshape(batch, seq, -1)
        new_k = kv3[:, seq - WINDOW:, A_WIDTH:A_WIDTH + A_KV_WIDTH]
        new_v = kv3[:, seq - WINDOW:, A_WIDTH + A_KV_WIDTH:]
        chunk = M_CHUNK
    else:
        att, new_k, new_v = _attn_s_call(qkv, kv_cache[0], kv_cache[1], bias_c, bias_n, sinks, t_valid)
        chunk = seq
    c0, n0, m0 = state
    mo, c_new, n_new, m_new = _mlstm_call(qkm, vm, om, gc, conv_w, conv_b.reshape(1, -1),
                                          m_norm.reshape(1, -1), conv0, c0, n0, m0,
                                          batch, seq, chunk, min(t_valid, chunk))
    new_conv = qkm.reshape(batch, seq, -1)[:, t_valid - (CONV_W - 1):t_valid]
    x_mid, h2, qp = _out_call(att, mo, x, g1, sc2, sh2, norm_ffn.reshape(1, -1),
                              w_out[:A_WIDTH].astype(BF16), w_out[A_WIDTH:].astype(BF16),
                              peer_query.astype(BF16), per_token, seq)
    idx, gw = _select_call(qp, peer_keys.astype(BF16))
    y = _sc_expert_call(idx, gw, h2, peer_u, peer_v)
    x_new = _resid_call(x_mid, y, g2, per_token, seq)
    new_k = new_k.reshape(batch, WINDOW, A_KV_HEADS, A_HEAD_DIM)
    new_v = new_v.reshape(batch, WINDOW, A_KV_HEADS, A_HEAD_DIM)
    return x_new, (new_k, new_v, new_conv, c_new, n_new, m_new[:, :M_HEADS, 0]), idx


def _prompt_group_sizes(n_req):
    sizes, left = [], n_req
    for g in range(PROMPT_GROUPS):
        take = left if g == PROMPT_GROUPS - 1 else min(left, 2 ** g)
        if take:
            sizes.append(take)
        left -= take
    return sizes


def kernel(x_prompt, x_sample, c_prompt, c_sample, cache_k, cache_v, state_conv, state_C, state_n, state_m, rel_bias, w_ada, b_ada, norm_mix, norm_ffn, w_in, conv_w, conv_b, gate_b, attn_sinks, m_norm, w_out, peer_query, peer_keys, peer_u, peer_v, norm_final):
    depth = w_ada.shape[0]
    bp, tp, d = x_prompt.shape
    bs, ts, _ = x_sample.shape
    assert tp % WINDOW == 0 and tp % M_CHUNK == 0 and ts <= SAMPLE_PAD and ts >= CONV_W - 1

    mod_all = _ada_call(jnp.concatenate([c_prompt, c_sample], axis=0), w_ada, b_ada)

    qi = np.arange(WINDOW)[:, None]
    bias_p = _bias_call(rel_bias, qi + WINDOW - np.arange(2 * WINDOW)[None, :])
    qs = np.arange(SAMPLE_PAD)[:, None]
    bias_c = _bias_call(rel_bias, qs + WINDOW - np.arange(WINDOW)[None, :])
    bias_n = _bias_call(rel_bias, qs - np.arange(SAMPLE_PAD)[None, :])

    sizes = _prompt_group_sizes(bp)
    starts = np.cumsum([0] + sizes).tolist()
    xg = [x_prompt[starts[g]:starts[g + 1]].reshape(sizes[g] * tp, d) for g in range(len(sizes))]
    xs = jnp.pad(x_sample, ((0, 0), (0, SAMPLE_PAD - ts), (0, 0))).reshape(bs * SAMPLE_PAD, d)
    halo_pad = ((0, 0), (SUBLANES - (CONV_W - 1), 0), (0, 0))

    st_p, st_s = [], []
    for l in range(depth):
        lw = (norm_mix[l], norm_ffn[l], w_in[l], conv_w[l], conv_b[l], gate_b[l], attn_sinks[l], m_norm[l],
              w_out[l], peer_query[l], peer_keys[l], _pack_table(peer_u[l]), _pack_table(peer_v[l]))
        mod_s = [jnp.repeat(m, SAMPLE_PAD, axis=0) for m in jnp.split(mod_all[l, bp:], 6, axis=-1)]
        sp_groups = []
        for g, bg in enumerate(sizes):
            mod_g = [m.reshape(bg, 1, d) for m in jnp.split(mod_all[l, starts[g]:starts[g + 1]], 6, axis=-1)]
            zero_state = (jnp.zeros((bg, M_HEADS, M_HEAD_DIM, M_HEAD_DIM), F32),
                          jnp.zeros((bg, M_HEADS, M_HEAD_DIM), F32),
                          jnp.zeros((bg, SUBLANES, LANES), F32))
            zero_conv = jnp.zeros((bg, SUBLANES, 2 * M_WIDTH), F32)
            xg[g], sp, last_idx = _layer(xg[g], mod_g, False, bg, tp, tp, lw, bias_p, None, None, None,
                                         zero_conv, zero_state)
            sp_groups.append(sp)
        st_p.append([jnp.concatenate([sp[i] for sp in sp_groups], axis=0) for i in range(6)])
        state_s = (state_C[l].astype(F32), state_n[l].astype(F32),
                   jnp.broadcast_to(jnp.pad(state_m[l].astype(F32), ((0, 0), (0, SUBLANES - M_HEADS)))[:, :, None],
                                    (bs, SUBLANES, LANES)))
        kv_cache = (cache_k[l].reshape(bs, WINDOW, A_KV_WIDTH), cache_v[l].reshape(bs, WINDOW, A_KV_WIDTH))
        xs, ss, _ = _layer(xs, mod_s, True, bs, SAMPLE_PAD, ts, lw, None, bias_c, bias_n, kv_cache,
                           jnp.pad(state_conv[l].astype(F32), halo_pad), state_s, after=last_idx)
        st_s.append(ss)

    gfin = norm_final.reshape(1, d)
    y_prompt = jnp.concatenate([_final_call(x, gfin).reshape(bg, tp, d) for x, bg in zip(xg, sizes)], axis=0)
    y_sample = _final_call(xs, gfin).reshape(bs, SAMPLE_PAD, d)[:, :ts]
    outs_p = [jnp.stack([s[i] for s in st_p]) for i in range(6)]
    outs_s = [jnp.stack([s[i] for s in st_s]) for i in range(6)]
    return (y_prompt, y_sample, *outs_p, *outs_s)
```

```python
import functools
import math

import numpy as np
import jax
import jax.numpy as jnp
from jax import lax
from jax.experimental import pallas as pl
from jax.experimental.pallas import tpu as pltpu
from jax.experimental.pallas import tpu_sc as plsc

F32 = jnp.float32
BF16 = jnp.bfloat16
HIGHEST = lax.Precision.HIGHEST

D_MODEL = 1024
A_HEADS = 8
A_KV_HEADS = 2
A_GROUP = A_HEADS // A_KV_HEADS
A_HEAD_DIM = 64
A_WIDTH = A_HEADS * A_HEAD_DIM
A_KV_WIDTH = A_KV_HEADS * A_HEAD_DIM
WINDOW = 128
ATT_SCALE = A_HEAD_DIM ** -0.5
N_BUCKETS = 32
MAX_DISTANCE = WINDOW
M_HEADS = 4
M_HEAD_DIM = 128
M_WIDTH = M_HEADS * M_HEAD_DIM
CONV_W = 4
M_CHUNK = 64
N_KEYS = 128
P_HEADS = 8
P_TOPK = 16
P_KEY_DIM = 256
P_HALF = P_KEY_DIM // 2
EPS = 1e-6
NEG_INF = -1e30

LANES = 128
SUBLANES = 8
SAMPLE_PAD = SUBLANES
VMEM_LIMIT = 48 * 1024 * 1024
PROMPT_GROUPS = 6

NT_DIMS = (((1,), (1,)), ((), ()))
TN_DIMS = (((0,), (0,)), ((), ()))


def _cparams(*sem):
    return pltpu.CompilerParams(dimension_semantics=sem, vmem_limit_bytes=VMEM_LIMIT)


def _bdot(a, b):
    return jnp.dot(a.astype(BF16), b.astype(BF16), preferred_element_type=F32)


def _bdot_nt(a, b):
    return lax.dot_general(a.astype(BF16), b.astype(BF16), NT_DIMS, preferred_element_type=F32)


def _sigmoid(x):
    return 1.0 / (1.0 + jnp.exp(-x))


def _log_sigmoid(x):
    return jnp.minimum(x, 0.0) - jnp.log1p(jnp.exp(-jnp.abs(x)))


def _gelu_tanh(x):
    c = math.sqrt(2.0 / math.pi)
    return x * (0.5 * (1.0 + jnp.tanh(c * (x + 0.044715 * (x * x * x)))))


def _ada_kernel(c_ref, w_ref, b_ref, o_ref):
    c = c_ref[...]
    s = c * _sigmoid(c)
    o_ref[...] = jnp.dot(s, w_ref[...], precision=HIGHEST, preferred_element_type=F32) + b_ref[...]


def _ada_call(c_all, w_ada, b_ada):
    depth, d, n6 = w_ada.shape
    rows = c_all.shape[0]
    bn = 1024
    return pl.pallas_call(
        _ada_kernel,
        grid=(depth, n6 // bn),
        in_specs=[
            pl.BlockSpec((rows, d), lambda l, j: (0, 0)),
            pl.BlockSpec((None, d, bn), lambda l, j: (l, 0, j)),
            pl.BlockSpec((None, 1, bn), lambda l, j: (l, 0, j)),
        ],
        out_specs=pl.BlockSpec((None, rows, bn), lambda l, j: (l, 0, j)),
        out_shape=jax.ShapeDtypeStruct((depth, rows, n6), F32),
        compiler_params=_cparams("parallel", "parallel"),
        name="ada_mod",
    )(c_all, w_ada, b_ada.reshape(depth, 1, n6))


def _mod_spec(per_token, tm, tokens_per_req):
    if per_token:
        return pl.BlockSpec((tm, D_MODEL), lambda i: (i, 0))
    tiles = tokens_per_req // tm
    return pl.BlockSpec((None, 1, D_MODEL), lambda i: (i // tiles, 0, 0))


def _in_kernel(x_ref, sc_ref, sh_ref, g_ref, wa_ref, wqk_ref, wv_ref, wo_ref, wg_ref, gb_ref,
               qkv_ref, qkm_ref, v_ref, o_ref, gc_ref):
    x = x_ref[...]
    y = x * lax.rsqrt(jnp.mean(x * x, axis=-1, keepdims=True) + EPS) * g_ref[...]
    h = (y * (1.0 + sc_ref[...]) + sh_ref[...]).astype(BF16)
    qkv_ref[...] = jnp.dot(h, wa_ref[...], preferred_element_type=F32)
    qkm_ref[...] = jnp.dot(h, wqk_ref[...], preferred_element_type=F32)
    v_ref[...] = jnp.dot(h, wv_ref[...], preferred_element_type=F32)
    o_ref[...] = jnp.dot(h, wo_ref[...], preferred_element_type=F32)
    g = jnp.dot(h, wg_ref[...], preferred_element_type=F32) + gb_ref[...]
    lane = lax.broadcasted_iota(jnp.int32, g.shape, 1)
    gc_ref[...] = jnp.where(lane < M_HEADS, g, jnp.where(lane < 2 * M_HEADS, _log_sigmoid(g), 0.0))


def _in_call(x, sc, sh, gnorm, wa, wqk, wv, wo, wg, gb, per_token, tokens_per_req):
    n = x.shape[0]
    tm = min(512, n if per_token else tokens_per_req)
    mod = _mod_spec(per_token, tm, tokens_per_req)
    full = lambda shape: pl.BlockSpec(shape, lambda i: (0,) * len(shape))
    row = lambda w: pl.BlockSpec((tm, w), lambda i: (i, 0))
    return pl.pallas_call(
        _in_kernel,
        grid=(n // tm,),
        in_specs=[row(D_MODEL), mod, mod, full((1, D_MODEL)), full(wa.shape), full(wqk.shape),
                  full(wv.shape), full(wo.shape), full(wg.shape), full((1, LANES))],
        out_specs=[row(wa.shape[1]), row(wqk.shape[1]), row(wv.shape[1]), row(wo.shape[1]), row(LANES)],
        out_shape=[jax.ShapeDtypeStruct((n, w), F32)
                   for w in (wa.shape[1], wqk.shape[1], wv.shape[1], wo.shape[1], LANES)],
        compiler_params=_cparams("parallel"),
        name="in_proj",
    )(x, sc, sh, gnorm, wa, wqk, wv, wo, wg, gb)


def _t5_bucket_np(dist):
    n = np.maximum(dist, 0)
    max_exact = N_BUCKETS // 2
    nf = np.maximum(n, 1).astype(np.float64)
    large = max_exact + (np.log(nf / max_exact) / math.log(MAX_DISTANCE / max_exact)
                         * (N_BUCKETS - max_exact)).astype(np.int32)
    return np.where(n < max_exact, n, np.minimum(large, N_BUCKETS - 1)).astype(np.int32)


def _bias_kernel(bucket_ref, rel_ref, o_ref):
    bucket = bucket_ref[...]
    for h in range(A_HEADS):
        acc = jnp.zeros(bucket.shape, F32)
        for b in range(N_BUCKETS):
            acc = jnp.where(bucket == b, rel_ref[b, h], acc)
        o_ref[h] = acc


def _bias_call(rel_bias, dist):
    bucket = jnp.asarray(_t5_bucket_np(dist))
    nq, nk = dist.shape
    return pl.pallas_call(
        _bias_kernel,
        in_specs=[pl.BlockSpec((nq, nk), lambda: (0, 0)),
                  pl.BlockSpec(memory_space=pltpu.SMEM)],
        out_specs=pl.BlockSpec((A_HEADS, nq, nk), lambda: (0, 0, 0)),
        out_shape=jax.ShapeDtypeStruct((A_HEADS, nq, nk), F32),
        name="t5_bias",
    )(bucket, rel_bias)


def _attn_p_kernel(q_ref, kp_ref, kc_ref, vp_ref, vc_ref, bias_ref, sink_ref, o_ref):
    i = pl.program_id(1)
    qi = lax.broadcasted_iota(jnp.int32, (WINDOW, WINDOW), 0)
    kj = lax.broadcasted_iota(jnp.int32, (WINDOW, WINDOW), 1)
    valid_prev = jnp.logical_and(kj > qi, i > 0)
    valid_cur = kj <= qi
    q = q_ref[...]
    for h in range(A_HEADS):
        kv = h // A_GROUP
        qh = q[:, h * A_HEAD_DIM:(h + 1) * A_HEAD_DIM]
        sl = slice(kv * A_HEAD_DIM, (kv + 1) * A_HEAD_DIM)
        bias = bias_ref[h]
        sp = _bdot_nt(qh, kp_ref[:, sl]) * ATT_SCALE + bias[:, :WINDOW]
        sc = _bdot_nt(qh, kc_ref[:, sl]) * ATT_SCALE + bias[:, WINDOW:]
        sp = jnp.where(valid_prev, sp, NEG_INF)
        sc = jnp.where(valid_cur, sc, NEG_INF)
        sink = sink_ref[0, h]
        mx = jnp.maximum(jnp.maximum(jnp.max(sp, axis=-1, keepdims=True),
                                     jnp.max(sc, axis=-1, keepdims=True)), sink)
        ep = jnp.exp(sp - mx)
        ec = jnp.exp(sc - mx)
        den = (jnp.sum(ep, axis=-1, keepdims=True) + jnp.sum(ec, axis=-1, keepdims=True)
               + jnp.exp(sink - mx))
        o = _bdot(ep / den, vp_ref[:, sl]) + _bdot(ec / den, vc_ref[:, sl])
        o_ref[:, h * A_HEAD_DIM:(h + 1) * A_HEAD_DIM] = o


def _attn_p_call(qkv, bias, sinks, batch, seq):
    nb = seq // WINDOW
    n = batch * seq
    kcol = A_WIDTH // A_KV_WIDTH
    vcol = kcol + 1
    cur = lambda col: pl.BlockSpec((WINDOW, A_KV_WIDTH), lambda b, i: (b * nb + i, col))
    prev = lambda col: pl.BlockSpec((WINDOW, A_KV_WIDTH),
                                    lambda b, i: (b * nb + jnp.maximum(i - 1, 0), col))
    return pl.pallas_call(
        _attn_p_kernel,
        grid=(batch, nb),
        in_specs=[pl.BlockSpec((WINDOW, A_WIDTH), lambda b, i: (b * nb + i, 0)),
                  prev(kcol), cur(kcol), prev(vcol), cur(vcol),
                  pl.BlockSpec((A_HEADS, WINDOW, 2 * WINDOW), lambda b, i: (0, 0, 0)),
                  pl.BlockSpec(memory_space=pltpu.SMEM)],
        out_specs=pl.BlockSpec((WINDOW, A_WIDTH), lambda b, i: (b * nb + i, 0)),
        out_shape=jax.ShapeDtypeStruct((n, A_WIDTH), F32),
        compiler_params=_cparams("parallel", "parallel"),
        name="swa_prompt",
    )(qkv, qkv, qkv, qkv, qkv, bias, sinks.reshape(1, A_HEADS))


def _attn_s_kernel(n_new, qkv_ref, ck_ref, cv_ref, bc_ref, bn_ref, sink_ref,
                   o_ref, nk_ref, nv_ref, kk_s, vv_s):
    qkv = qkv_ref[...]
    knew = qkv[:, A_WIDTH:A_WIDTH + A_KV_WIDTH]
    vnew = qkv[:, A_WIDTH + A_KV_WIDTH:A_WIDTH + 2 * A_KV_WIDTH]
    ck = ck_ref[...]
    cv = cv_ref[...]
    rows = A_GROUP * SAMPLE_PAD
    qi = lax.broadcasted_iota(jnp.int32, (rows, WINDOW), 0) % SAMPLE_PAD
    kj = lax.broadcasted_iota(jnp.int32, (rows, WINDOW), 1)
    valid_c = kj > qi
    rcol = lax.broadcasted_iota(jnp.int32, (rows, 1), 0)
    qcol = rcol % SAMPLE_PAD
    for kv in range(A_KV_HEADS):
        heads = range(kv * A_GROUP, (kv + 1) * A_GROUP)
        sl = slice(kv * A_HEAD_DIM, (kv + 1) * A_HEAD_DIM)
        qs = jnp.concatenate([qkv[:, h * A_HEAD_DIM:(h + 1) * A_HEAD_DIM] for h in heads], axis=0)
        bias_c = jnp.concatenate([bc_ref[h] for h in heads], axis=0)
        bias_n = jnp.concatenate([bn_ref[h] for h in heads], axis=0)
        sink = jnp.zeros((rows, 1), F32)
        for g, h in enumerate(heads):
            sink = jnp.where(rcol // SAMPLE_PAD == g, sink_ref[0, h], sink)
        s_c = _bdot_nt(qs, ck[:, sl]) * ATT_SCALE + bias_c
        s_c = jnp.where(valid_c, s_c, NEG_INF)
        s_n = []
        for j in range(n_new):
            sj = jnp.sum(qs * knew[j:j + 1, sl], axis=-1, keepdims=True) * ATT_SCALE + bias_n[:, j:j + 1]
            s_n.append(jnp.where(qcol >= j, sj, NEG_INF))
        mx = jnp.maximum(jnp.max(s_c, axis=-1, keepdims=True), sink)
        for sj in s_n:
            mx = jnp.maximum(mx, sj)
        e_c = jnp.exp(s_c - mx)
        den = jnp.sum(e_c, axis=-1, keepdims=True) + jnp.exp(sink - mx)
        o = _bdot(e_c, cv[:, sl])
        for j, sj in enumerate(s_n):
            ej = jnp.exp(sj - mx)
            den = den + ej
            o = o + ej * vnew[j:j + 1, sl]
        o = o / den
        for g, h in enumerate(heads):
            o_ref[:, h * A_HEAD_DIM:(h + 1) * A_HEAD_DIM] = o[g * SAMPLE_PAD:(g + 1) * SAMPLE_PAD, :]
    kk_s[0:WINDOW, :] = ck
    kk_s[WINDOW:WINDOW + SAMPLE_PAD, :] = knew
    vv_s[0:WINDOW, :] = cv
    vv_s[WINDOW:WINDOW + SAMPLE_PAD, :] = vnew
    nk_ref[...] = kk_s[n_new:n_new + WINDOW, :]
    nv_ref[...] = vv_s[n_new:n_new + WINDOW, :]


def _attn_s_call(qkv, ck, cv, bias_c, bias_n, sinks, n_new):
    nreq = ck.shape[0]
    wq = qkv.shape[1]
    full3 = lambda shape: pl.BlockSpec(shape, lambda b: (0, 0, 0))
    cache = pl.BlockSpec((None, WINDOW, A_KV_WIDTH), lambda b: (b, 0, 0))
    return pl.pallas_call(
        functools.partial(_attn_s_kernel, n_new),
        grid=(nreq,),
        in_specs=[pl.BlockSpec((SAMPLE_PAD, wq), lambda b: (b, 0)), cache, cache,
                  full3(bias_c.shape), full3(bias_n.shape),
                  pl.BlockSpec(memory_space=pltpu.SMEM)],
        out_specs=[pl.BlockSpec((SAMPLE_PAD, A_WIDTH), lambda b: (b, 0)), cache, cache],
        out_shape=[jax.ShapeDtypeStruct((nreq * SAMPLE_PAD, A_WIDTH), F32),
                   jax.ShapeDtypeStruct(ck.shape, F32), jax.ShapeDtypeStruct(cv.shape, F32)],
        scratch_shapes=[pltpu.VMEM((WINDOW + SAMPLE_PAD, A_KV_WIDTH), F32),
                        pltpu.VMEM((WINDOW + SAMPLE_PAD, A_KV_WIDTH), F32)],
        compiler_params=_cparams("parallel"),
        name="swa_sample",
    )(qkv, ck, cv, bias_c, bias_n, sinks.reshape(1, A_HEADS))


def _mlstm_kernel(chunk, t_valid, qk_ref, v_ref, og_ref, gc_ref, cw_ref, cb_ref, mn_ref,
                  conv0_ref, c0_ref, n0_ref, m0_ref,
                  out_ref, cout_ref, nout_ref, mout_ref,
                  xp_s, c_s, n_s, m_s):
    step = pl.program_id(1)
    halo = SUBLANES

    @pl.when(step == 0)
    def _():
        xp_s[0:halo, :] = conv0_ref[...]
        c_s[...] = c0_ref[...]
        n_s[...] = n0_ref[...]
        m_s[...] = m0_ref[...]

    xp_s[halo:halo + chunk, :] = qk_ref[...]
    cw = cw_ref[...]
    y = cb_ref[...]
    for i in range(CONV_W):
        off = halo - (CONV_W - 1) + i
        y = y + xp_s[off:off + chunk, :] * cw[i:i + 1, :]
    xp_s[0:halo, :] = xp_s[chunk:chunk + halo, :]
    y = y * _sigmoid(y)
    q_all = y[:, :M_WIDTH]
    k_all = y[:, M_WIDTH:] * (M_HEAD_DIM ** -0.5)

    g = gc_ref[...]
    if t_valid < chunk:
        row = lax.broadcasted_iota(jnp.int32, g.shape, 0)
        lane = lax.broadcasted_iota(jnp.int32, g.shape, 1)
        g = jnp.where(row < t_valid, g, jnp.where(lane < M_HEADS, NEG_INF, 0.0))
    tr = lax.broadcasted_iota(jnp.int32, (chunk, chunk), 0)
    tc = lax.broadcasted_iota(jnp.int32, (chunk, chunk), 1)
    causal = tr >= tc
    tri = causal.astype(F32)
    bcol = jnp.dot(tri, g, precision=HIGHEST, preferred_element_type=F32)
    er = lax.broadcasted_iota(jnp.int32, (SUBLANES, LANES), 0)
    ec = lax.broadcasted_iota(jnp.int32, (SUBLANES, LANES), 1)
    eye = (er == ec).astype(F32)
    g_rows = lax.dot_general(eye, g, NT_DIMS, precision=HIGHEST, preferred_element_type=F32)
    b_rows = lax.dot_general(eye, bcol, NT_DIMS, precision=HIGHEST, preferred_element_type=F32)

    for h in range(M_HEADS):
        hs = slice(h * M_HEAD_DIM, (h + 1) * M_HEAD_DIM)
        b_c = bcol[:, M_HEADS + h:M_HEADS + h + 1]
        ig_c = g[:, h:h + 1]
        b_r = b_rows[M_HEADS + h:M_HEADS + h + 1, :]
        ig_r = g_rows[h:h + 1, :]
        m_prev = m_s[h:h + 1, 0:1]
        logw = jnp.where(causal, b_c - b_r + ig_r, -jnp.inf)
        inter = b_c + m_prev
        m_t = jnp.maximum(inter, jnp.max(logw, axis=-1, keepdims=True))
        w = jnp.exp(logw - m_t)
        a = jnp.exp(inter - m_t)
        q = q_all[:, hs]
        k = k_all[:, hs]
        v = v_ref[:, hs]
        cmat = c_s[h]
        nvec = n_s[h:h + 1, :]
        wqk = w * _bdot_nt(q, k)
        num = _bdot(wqk, v) + a * _bdot_nt(q, cmat)
        den = jnp.sum(wqk, axis=-1, keepdims=True) + a * jnp.sum(q * nvec, axis=-1, keepdims=True)
        hh = num / jnp.maximum(jnp.abs(den), jnp.exp(-m_t))
        m_new = m_t[chunk - 1:chunk, :]
        b_last = b_c[chunk - 1:chunk, :]
        wl = jnp.exp(b_last - b_c + ig_c - m_new)
        al = jnp.exp(b_last + m_prev - m_new)
        c_s[h] = al * cmat + lax.dot_general((v * wl).astype(BF16), k.astype(BF16), TN_DIMS,
                                             preferred_element_type=F32)
        n_s[h:h + 1, :] = al * nvec + jnp.sum(wl * k, axis=0, keepdims=True)
        m_s[h:h + 1, :] = jnp.broadcast_to(m_new, (1, LANES))
        hn = hh * lax.rsqrt(jnp.mean(hh * hh, axis=-1, keepdims=True) + EPS) * mn_ref[:, hs]
        out_ref[:, hs] = _sigmoid(og_ref[:, hs]) * hn

    @pl.when(step == pl.num_programs(1) - 1)
    def _():
        cout_ref[...] = c_s[...]
        nout_ref[...] = n_s[...]
        mout_ref[...] = m_s[...]


def _mlstm_call(qk, v, og, gc, conv_w, conv_b, m_norm, conv0, c0, n0, m0, batch, seq, chunk, t_valid):
    nc = seq // chunk
    n = batch * seq
    row = lambda w: pl.BlockSpec((chunk, w), lambda b, c: (b * nc + c, 0))
    full2 = lambda shape: pl.BlockSpec(shape, lambda b, c: (0, 0))
    per_b = lambda shape: pl.BlockSpec((None,) + shape, lambda b, c: (b,) + (0,) * len(shape))
    dh = M_HEAD_DIM
    return pl.pallas_call(
        functools.partial(_mlstm_kernel, chunk, t_valid),
        grid=(batch, nc),
        in_specs=[row(2 * M_WIDTH), row(M_WIDTH), row(M_WIDTH), row(LANES),
                  full2((CONV_W, 2 * M_WIDTH)), full2((1, 2 * M_WIDTH)), full2((1, M_WIDTH)),
                  per_b((SUBLANES, 2 * M_WIDTH)), per_b((M_HEADS, dh, dh)), per_b((M_HEADS, dh)),
                  per_b((SUBLANES, LANES))],
        out_specs=[row(M_WIDTH), per_b((M_HEADS, dh, dh)), per_b((M_HEADS, dh)), per_b((SUBLANES, LANES))],
        out_shape=[jax.ShapeDtypeStruct((n, M_WIDTH), F32),
                   jax.ShapeDtypeStruct((batch, M_HEADS, dh, dh), F32),
                   jax.ShapeDtypeStruct((batch, M_HEADS, dh), F32),
                   jax.ShapeDtypeStruct((batch, SUBLANES, LANES), F32)],
        scratch_shapes=[pltpu.VMEM((SUBLANES + chunk, 2 * M_WIDTH), F32),
                        pltpu.VMEM((M_HEADS, dh, dh), F32),
                        pltpu.VMEM((M_HEADS, dh), F32),
                        pltpu.VMEM((SUBLANES, LANES), F32)],
        compiler_params=_cparams("parallel", "arbitrary"),
        name="mlstm",
    )(qk, v, og, gc, conv_w, conv_b, m_norm, conv0, c0, n0, m0)


def _out_kernel(att_ref, mo_ref, x_ref, g1_ref, sc_ref, sh_ref, gn_ref, wa_ref, wm_ref, wq_ref,
                xo_ref, h2_ref, qp_ref):
    mix = (jnp.dot(att_ref[...].astype(BF16), wa_ref[...], preferred_element_type=F32)
           + jnp.dot(mo_ref[...].astype(BF16), wm_ref[...], preferred_element_type=F32))
    x = x_ref[...] + g1_ref[...] * mix
    xo_ref[...] = x
    y = x * lax.rsqrt(jnp.mean(x * x, axis=-1, keepdims=True) + EPS) * gn_ref[...]
    h2 = (y * (1.0 + sc_ref[...]) + sh_ref[...]).astype(BF16)
    qp_ref[...] = jnp.dot(h2, wq_ref[...], preferred_element_type=F32)
    bits = pltpu.bitcast(h2.astype(F32), jnp.uint32)
    half = D_MODEL // 2
    lo = lax.shift_right_logical(bits[:, :half], jnp.uint32(16))
    hi = bits[:, half:] & jnp.uint32(0xFFFF0000)
    h2_ref[...] = pltpu.bitcast(lo | hi, jnp.int32)


def _out_call(att, mo, x, g1, sc, sh, gnorm, wa, wm, wq, per_token, tokens_per_req):
    n = x.shape[0]
    tm = min(256, n if per_token else tokens_per_req)
    mod = _mod_spec(per_token, tm, tokens_per_req)
    full = lambda shape: pl.BlockSpec(shape, lambda i: (0,) * len(shape))
    row = lambda w: pl.BlockSpec((tm, w), lambda i: (i, 0))
    nq = wq.shape[1]
    return pl.pallas_call(
        _out_kernel,
        grid=(n // tm,),
        in_specs=[row(A_WIDTH), row(M_WIDTH), row(D_MODEL), mod, mod, mod, full((1, D_MODEL)),
                  full(wa.shape), full(wm.shape), full(wq.shape)],
        out_specs=[row(D_MODEL), row(D_MODEL // 2), row(nq)],
        out_shape=[jax.ShapeDtypeStruct((n, D_MODEL), F32), jax.ShapeDtypeStruct((n, D_MODEL // 2), jnp.int32),
                   jax.ShapeDtypeStruct((n, nq), F32)],
        compiler_params=_cparams("parallel"),
        name="out_proj",
    )(att, mo, x, g1, sc, sh, gnorm, wa, wm, wq)


def _pk_cells():
    return [(a, b) for a in range(P_TOPK) for b in range(P_TOPK) if (a + 1) * (b + 1) <= P_TOPK]


PK_CELL_ROWS = 64


def _pk_expand_mats():
    cells = _pk_cells()
    e0 = np.zeros((PK_CELL_ROWS, LANES), np.float32)
    e1 = np.zeros((PK_CELL_ROWS, LANES), np.float32)
    for j, (a, b) in enumerate(cells):
        e0[j, a] = 1.0
        e1[j, b] = 1.0
    return e0, e1, len(cells)


def _top_rows(s, rowf, rounds):
    n_rows = s.shape[0]
    vals, idxs = [], []
    for _ in range(rounds):
        m = jnp.max(s, axis=0, keepdims=True)
        i = jnp.min(jnp.where(s == m, rowf, float(n_rows)), axis=0, keepdims=True)
        vals.append(m)
        idxs.append(i)
        s = jnp.where(rowf == i, -jnp.inf, s)
    return jnp.concatenate(vals, axis=0), jnp.concatenate(idxs, axis=0)


def _select_kernel(n_cells, qp_ref, keys_ref, e0_ref, e1_ref, idx_ref, gw_ref, idx_s, gw_s):
    tm = qp_ref.shape[0]
    keyf = lax.broadcasted_iota(jnp.int32, (N_KEYS, tm), 0).astype(F32)
    cellf = lax.broadcasted_iota(jnp.int32, (PK_CELL_ROWS, tm), 0).astype(F32)
    e0 = e0_ref[...]
    e1 = e1_ref[...]
    pad = jnp.zeros((LANES - P_TOPK, tm), F32)

    def head(h, carry):
        h = jnp.asarray(h, jnp.int32)
        sub = []
        for c in range(2):
            col = pl.multiple_of((h * 2 + c) * P_HALF, P_HALF)
            s = _bdot_nt(keys_ref[h, c], qp_ref[:, pl.ds(col, P_HALF)])
            sub.append(_top_rows(s, keyf, P_TOPK))
        (v0, i0), (v1, i1) = sub
        expand = lambda e, x: jnp.dot(e, jnp.concatenate([x, pad], axis=0), precision=HIGHEST,
                                      preferred_element_type=F32)
        cand = expand(e0, v0) + expand(e1, v1)
        cidx = expand(e0, i0 * float(N_KEYS)) + expand(e1, i1)
        cand = jnp.where(cellf < n_cells, cand, -jnp.inf)
        best, eidx = [], []
        for _ in range(P_TOPK):
            m = jnp.max(cand, axis=0, keepdims=True)
            j = jnp.min(jnp.where(cand == m, cellf, float(PK_CELL_ROWS)), axis=0, keepdims=True)
            hit = cellf == j
            eidx.append(jnp.max(jnp.where(hit, cidx, -1.0), axis=0, keepdims=True))
            best.append(m)
            cand = jnp.where(hit, -jnp.inf, cand)
        best = jnp.concatenate(best, axis=0)
        e = jnp.exp(best - best[0:1, :])
        row0 = pl.multiple_of(h * P_TOPK, P_TOPK)
        gw_s[pl.ds(row0, P_TOPK), :] = e / jnp.sum(e, axis=0, keepdims=True)
        idx_s[pl.ds(row0, P_TOPK), :] = jnp.concatenate(eidx, axis=0)
        return carry

    lax.fori_loop(0, P_HEADS, head, 0)
    idx_ref[...] = idx_s[...].T.astype(jnp.int32)
    gw_ref[...] = gw_s[...].T


def _select_call(qp, keys_bf16):
    n = qp.shape[0]
    tm = min(LANES, n)
    e0, e1, n_cells = _pk_expand_mats()
    full = lambda shape: pl.BlockSpec(shape, lambda i: (0,) * len(shape))
    return pl.pallas_call(
        functools.partial(_select_kernel, n_cells),
        grid=(n // tm,),
        in_specs=[pl.BlockSpec((tm, qp.shape[1]), lambda i: (i, 0)), full(keys_bf16.shape),
                  full((PK_CELL_ROWS, LANES)), full((PK_CELL_ROWS, LANES))],
        out_specs=[pl.BlockSpec((tm, LANES), lambda i: (i, 0)), pl.BlockSpec((tm, LANES), lambda i: (i, 0))],
        out_shape=[jax.ShapeDtypeStruct((n, LANES), jnp.int32), jax.ShapeDtypeStruct((n, LANES), F32)],
        scratch_shapes=[pltpu.VMEM((P_HEADS * P_TOPK, tm), F32), pltpu.VMEM((P_HEADS * P_TOPK, tm), F32)],
        compiler_params=_cparams("parallel"),
        name="peer_select",
    )(qp, keys_bf16, jnp.asarray(e0), jnp.asarray(e1))


N_SEL = P_HEADS * P_TOPK

SC_CORES = 2
SC_SUBCORES = 16
SC_LANES = 16
SC_TOK_BLOCK = 8
SC_ROWS = 32
SC_ROW_BLOCK = 16
SC_ACC_CHAINS = 4
SC_ACC_ROWS = 16
SC_NBUF = 4
N_WCOL = D_MODEL // (2 * SC_LANES)


def _pack_table(t):
    e, dcol = t.shape
    tb = t.astype(BF16).reshape(e, 2, dcol // 2)
    words = lax.bitcast_convert_type(jnp.swapaxes(tb, 1, 2), jnp.uint32)
    return lax.bitcast_convert_type(words, jnp.int32)


def _sc_gelu(x):
    z = math.sqrt(2.0 / math.pi) * (x + 0.044715 * (x * x * x))
    t = 1.0 - 2.0 / (jnp.exp(2.0 * z) + 1.0)
    return x * (0.5 * (1.0 + t))


def _sc_expert_body(tokens_per_worker, idx_hbm, gw_hbm, h2_hbm, u_hbm, v_hbm, y_hbm,
                    idx_v, gw_v, x_v, o_v, buf, coef_v, tr_v, sem, in_sem, out_sem):
    wid = lax.axis_index("s") * SC_CORES + lax.axis_index("c")
    base = wid * tokens_per_worker
    lane = lax.iota(jnp.int32, SC_LANES)
    n_gather = N_SEL // SC_ROWS
    zero = jnp.zeros((SC_LANES,), F32)

    n_steps = 2 * n_gather
    assert n_steps % SC_NBUF == 0

    def gather(p, tt, i):
        table = u_hbm if i < n_gather else v_hbm
        j = i % n_gather
        slot = i % SC_NBUF
        return pltpu.make_async_copy(table.at[idx_v.at[p, tt, pl.ds(j * SC_ROWS, SC_ROWS)]], buf.at[slot],
                                     sem.at[slot])

    def unpack(w):
        lo = lax.bitcast_convert_type(lax.shift_left(w, jnp.full((SC_LANES,), 16, jnp.int32)), F32)
        hi = lax.bitcast_convert_type(w & jnp.full((SC_LANES,), -65536, jnp.int32), F32)
        return lo, hi

    def packed(w):
        return plsc.bitcast(w, BF16)

    def unpack_sum(s):
        return unpack(plsc.bitcast(s, jnp.int32))

    def act_chunk(p, tt, j, slot):
        @pl.loop(0, SC_ROWS // SC_LANES)
        def _(half):
            for rb in range(SC_LANES // SC_ROW_BLOCK):
                r0 = half * SC_LANES + rb * SC_ROW_BLOCK

                def col(c, accs):
                    w0 = pl.multiple_of(c * (2 * SC_LANES), 2 * SC_LANES)
                    xa = packed(x_v[p, tt, pl.ds(w0, SC_LANES)])
                    xb = packed(x_v[p, tt, pl.ds(w0 + SC_LANES, SC_LANES)])
                    out = []
                    for r, a in enumerate(accs):
                        ua = packed(buf[slot, r0 + r, pl.ds(w0, SC_LANES)])
                        ub = packed(buf[slot, r0 + r, pl.ds(w0 + SC_LANES, SC_LANES)])
                        lo, hi = unpack_sum(ua * xa + ub * xb)
                        out.append(a + lo + hi)
                    return tuple(out)

                accs = lax.fori_loop(0, N_WCOL // 2, col, (zero,) * SC_ROW_BLOCK)
                for r in range(SC_ROW_BLOCK):
                    tr_v[pl.ds((rb * SC_ROW_BLOCK + r) * SC_LANES, SC_LANES)] = accs[r]
            tot = zero
            for jj in range(SC_LANES):
                tot = tot + plsc.load_gather(tr_v, [lane * SC_LANES + jj])
            k0 = pl.multiple_of(j * SC_ROWS + half * SC_LANES, SC_LANES)
            coef_v[pl.ds(k0, SC_LANES)] = gw_v[p, tt, pl.ds(k0, SC_LANES)] * _sc_gelu(tot)

    def acc_chunk(p, tt, j, slot, first):
        def tree_sum(parts):
            while len(parts) > 1:
                parts = [parts[i] + parts[i + 1] for i in range(0, len(parts), 2)]
            return parts[0]

        for rb in range(SC_ROWS // SC_ACC_ROWS):
            rows = list(range(rb * SC_ACC_ROWS, (rb + 1) * SC_ACC_ROWS))
            splat = {}
            for r in rows:
                c16 = plsc.load_gather(coef_v, [jnp.full((SC_LANES,), j * SC_ROWS + r, jnp.int32)])
                splat[r] = plsc.pack(c16, c16, format=plsc.PackFormat.INTERLEAVED)
            fresh = first and rb == 0

            @plsc.parallel_loop(0, N_WCOL)
            def _(c):
                woff = pl.multiple_of(c * SC_LANES, SC_LANES)
                hoff = pl.multiple_of(c * SC_LANES + D_MODEL // 2, SC_LANES)
                pa, pb = [], []
                for n in range(0, SC_ACC_ROWS, 2):
                    r0, r1 = rows[n], rows[n + 1]
                    s = (splat[r0] * packed(buf[slot, r0, pl.ds(woff, SC_LANES)])
                         + splat[r1] * packed(buf[slot, r1, pl.ds(woff, SC_LANES)]))
                    lo, hi = unpack_sum(s)
                    if n // 2 < SC_ACC_CHAINS:
                        pa.append(lo)
                        pb.append(hi)
                    else:
                        pa[(n // 2) % SC_ACC_CHAINS] = pa[(n // 2) % SC_ACC_CHAINS] + lo
                        pb[(n // 2) % SC_ACC_CHAINS] = pb[(n // 2) % SC_ACC_CHAINS] + hi
                sa, sb = tree_sum(pa), tree_sum(pb)
                if not fresh:
                    sa = sa + o_v[p, tt, pl.ds(woff, SC_LANES)]
                    sb = sb + o_v[p, tt, pl.ds(hoff, SC_LANES)]
                o_v[p, tt, pl.ds(woff, SC_LANES)] = sa
                o_v[p, tt, pl.ds(hoff, SC_LANES)] = sb

    n_blocks = tokens_per_worker // SC_TOK_BLOCK

    def block_start(blk):
        return pl.multiple_of(base + blk * SC_TOK_BLOCK, SC_TOK_BLOCK)

    def in_copies(blk, p):
        rows = pl.ds(block_start(blk), SC_TOK_BLOCK)
        return [pltpu.make_async_copy(idx_hbm.at[rows], idx_v.at[p], in_sem.at[p]),
                pltpu.make_async_copy(gw_hbm.at[rows], gw_v.at[p], in_sem.at[p]),
                pltpu.make_async_copy(h2_hbm.at[rows], x_v.at[p], in_sem.at[p])]

    def out_copy(blk, p):
        return pltpu.make_async_copy(o_v.at[p], y_hbm.at[pl.ds(block_start(blk), SC_TOK_BLOCK)], out_sem.at[p])

    for cp in in_copies(0, 0):
        cp.start()

    @pl.loop(0, n_blocks)
    def _(blk):
        p = lax.rem(blk, 2)
        for cp in in_copies(blk, p):
            cp.wait()

        @pl.when(blk + 1 < n_blocks)
        def _():
            for cp in in_copies(blk + 1, 1 - p):
                cp.start()

        @pl.when(blk >= 2)
        def _():
            out_copy(blk - 2, p).wait()

        ahead = SC_NBUF - 1
        for i in range(ahead):
            gather(p, 0, i).start()

        @pl.loop(0, SC_TOK_BLOCK)
        def _(tt):
            for i in range(n_steps):
                if i + ahead < n_steps:
                    gather(p, tt, i + ahead).start()
                else:
                    @pl.when(tt + 1 < SC_TOK_BLOCK)
                    def _():
                        gather(p, tt + 1, i + ahead - n_steps).start()
                gather(p, tt, i).wait()
                if i < n_gather:
                    act_chunk(p, tt, i, i % SC_NBUF)
                else:
                    acc_chunk(p, tt, i - n_gather, i % SC_NBUF, i == n_gather)

        out_copy(blk, p).start()

    for blk in (n_blocks - 2, n_blocks - 1):
        out_copy(blk, blk % 2).wait()


def _sc_expert_call(idx, gw, xw, u, v):
    n = idx.shape[0]
    workers = SC_CORES * SC_SUBCORES
    assert n % (workers * SC_TOK_BLOCK) == 0 and n // (workers * SC_TOK_BLOCK) >= 2
    mesh = plsc.VectorSubcoreMesh(core_axis_name="c", subcore_axis_name="s")
    return pl.kernel(
        functools.partial(_sc_expert_body, n // workers),
        out_type=jax.ShapeDtypeStruct((n, D_MODEL), F32),
        mesh=mesh,
        scratch_types=[pltpu.VMEM((2, SC_TOK_BLOCK, N_SEL), jnp.int32),
                       pltpu.VMEM((2, SC_TOK_BLOCK, N_SEL), F32),
                       pltpu.VMEM((2, SC_TOK_BLOCK, D_MODEL // 2), jnp.int32),
                       pltpu.VMEM((2, SC_TOK_BLOCK, D_MODEL), F32),
                       pltpu.VMEM((SC_NBUF, SC_ROWS, D_MODEL // 2), jnp.int32),
                       pltpu.VMEM((N_SEL,), F32),
                       pltpu.VMEM((SC_LANES * SC_LANES,), F32),
                       pltpu.SemaphoreType.DMA((SC_NBUF,)), pltpu.SemaphoreType.DMA((2,)),
                       pltpu.SemaphoreType.DMA((2,))],
        compiler_params=pltpu.CompilerParams(needs_layout_passes=False),
        name="peer_experts_sc",
    )(idx, gw, xw, u, v)


def _resid_kernel(x_ref, y_ref, g_ref, o_ref):
    o_ref[...] = x_ref[...] + g_ref[...] * y_ref[...]


def _resid_call(x, y, g2, per_token, tokens_per_req):
    n = x.shape[0]
    tm = min(512, n if per_token else tokens_per_req)
    row = pl.BlockSpec((tm, D_MODEL), lambda i: (i, 0))
    return pl.pallas_call(
        _resid_kernel,
        grid=(n // tm,),
        in_specs=[row, row, _mod_spec(per_token, tm, tokens_per_req)],
        out_specs=row,
        out_shape=jax.ShapeDtypeStruct((n, D_MODEL), F32),
        compiler_params=_cparams("parallel"),
        name="peer_residual",
    )(x, y, g2)


def _final_kernel(x_ref, g_ref, o_ref):
    x = x_ref[...]
    o_ref[...] = x * lax.rsqrt(jnp.mean(x * x, axis=-1, keepdims=True) + EPS) * g_ref[...]


def _final_call(x, g):
    n = x.shape[0]
    tm = min(512, n)
    return pl.pallas_call(
        _final_kernel,
        grid=(n // tm,),
        in_specs=[pl.BlockSpec((tm, D_MODEL), lambda i: (i, 0)), pl.BlockSpec((1, D_MODEL), lambda i: (0, 0))],
        out_specs=pl.BlockSpec((tm, D_MODEL), lambda i: (i, 0)),
        out_shape=jax.ShapeDtypeStruct((n, D_MODEL), F32),
        compiler_params=_cparams("parallel"),
        name="final_norm",
    )(x, g)


def _split_w_in(w_in_l, gate_b_l):
    cuts = np.cumsum([A_WIDTH + 2 * A_KV_WIDTH, 2 * M_WIDTH, M_WIDTH, M_WIDTH]).tolist()
    wa = w_in_l[:, :cuts[0]].astype(BF16)
    wqk = w_in_l[:, cuts[0]:cuts[1]].astype(BF16)
    wv = w_in_l[:, cuts[1]:cuts[2]].astype(BF16)
    wo = w_in_l[:, cuts[2]:cuts[3]].astype(BF16)
    ng = 2 * M_HEADS
    wg = jnp.pad(w_in_l[:, cuts[3]:], ((0, 0), (0, LANES - ng))).astype(BF16)
    gb = jnp.pad(gate_b_l.astype(F32), (0, LANES - ng)).reshape(1, LANES)
    return wa, wqk, wv, wo, wg, gb


def _layer(x, mods, per_token, batch, seq, t_valid, lw, bias_p, bias_c, bias_n, kv_cache, conv0, state,
           after=None):
    (norm_mix, norm_ffn, w_in, conv_w, conv_b, gate_b, sinks, m_norm, w_out, peer_query, peer_keys,
     peer_u, peer_v) = lw
    if after is not None:
        x, _ = lax.optimization_barrier((x, after))
    sh1, sc1, g1, sh2, sc2, g2 = mods
    wa, wqk, wv, wo, wg, gb = _split_w_in(w_in, gate_b)
    qkv, qkm, vm, om, gc = _in_call(x, sc1, sh1, norm_mix.reshape(1, -1), wa, wqk, wv, wo, wg, gb,
                                    per_token, seq)
    if kv_cache is None:
        att = _attn_p_call(qkv, bias_p, sinks, batch, seq)
        kv3 = qkv.reshape(batch, seq, -1)
        new_k = kv3[:, seq - WINDOW:, A_WIDTH:A_WIDTH + A_KV_WIDTH]
        new_v = kv3[:, seq - WINDOW:, A_WIDTH + A_KV_WIDTH:]
        chunk = M_CHUNK
    else:
        att, new_k, new_v = _attn_s_call(qkv, kv_cache[0], kv_cache[1], bias_c, bias_n, sinks, t_valid)
        chunk = seq
    c0, n0, m0 = state
    mo, c_new, n_new, m_new = _mlstm_call(qkm, vm, om, gc, conv_w, conv_b.reshape(1, -1),
                                          m_norm.reshape(1, -1), conv0, c0, n0, m0,
                                          batch, seq, chunk, min(t_valid, chunk))
    new_conv = qkm.reshape(batch, seq, -1)[:, t_valid - (CONV_W - 1):t_valid]
    x_mid, xw, qp = _out_call(att, mo, x, g1, sc2, sh2, norm_ffn.reshape(1, -1),
                              w_out[:A_WIDTH].astype(BF16), w_out[A_WIDTH:].astype(BF16),
                              peer_query.astype(BF16), per_token, seq)
    idx, gw = _select_call(qp, peer_keys.astype(BF16))
    y = _sc_expert_call(idx, gw, xw, peer_u, peer_v)
    x_new = _resid_call(x_mid, y, g2, per_token, seq)
    new_k = new_k.reshape(batch, WINDOW, A_KV_HEADS, A_HEAD_DIM)
    new_v = new_v.reshape(batch, WINDOW, A_KV_HEADS, A_HEAD_DIM)
    return x_new, (new_k, new_v, new_conv, c_new, n_new, m_new[:, :M_HEADS, 0]), idx


def _prompt_group_sizes(n_req):
    if n_req < PROMPT_GROUPS:
        return [1] * n_req
    mid, n_mid = n_req - 2, PROMPT_GROUPS - 2
    return [1] + [mid // n_mid + (1 if i < mid % n_mid else 0) for i in range(n_mid)] + [1]


def kernel(x_prompt, x_sample, c_prompt, c_sample, cache_k, cache_v, state_conv, state_C, state_n, state_m, rel_bias, w_ada, b_ada, norm_mix, norm_ffn, w_in, conv_w, conv_b, gate_b, attn_sinks, m_norm, w_out, peer_query, peer_keys, peer_u, peer_v, norm_final):
    depth = w_ada.shape[0]
    bp, tp, d = x_prompt.shape
    bs, ts, _ = x_sample.shape
    assert tp % WINDOW == 0 and tp % M_CHUNK == 0 and ts <= SAMPLE_PAD and ts >= CONV_W - 1

    mod_all = _ada_call(jnp.concatenate([c_prompt, c_sample], axis=0), w_ada, b_ada)

    qi = np.arange(WINDOW)[:, None]
    bias_p = _bias_call(rel_bias, qi + WINDOW - np.arange(2 * WINDOW)[None, :])
    qs = np.arange(SAMPLE_PAD)[:, None]
    bias_c = _bias_call(rel_bias, qs + WINDOW - np.arange(WINDOW)[None, :])
    bias_n = _bias_call(rel_bias, qs - np.arange(SAMPLE_PAD)[None, :])

    sizes = _prompt_group_sizes(bp)
    starts = np.cumsum([0] + sizes).tolist()
    xg = [x_prompt[starts[g]:starts[g + 1]].reshape(sizes[g] * tp, d) for g in range(len(sizes))]
    xs = jnp.pad(x_sample, ((0, 0), (0, SAMPLE_PAD - ts), (0, 0))).reshape(bs * SAMPLE_PAD, d)
    halo_pad = ((0, 0), (SUBLANES - (CONV_W - 1), 0), (0, 0))

    st_p, st_s = [], []
    for l in range(depth):
        lw = (norm_mix[l], norm_ffn[l], w_in[l], conv_w[l], conv_b[l], gate_b[l], attn_sinks[l], m_norm[l],
              w_out[l], peer_query[l], peer_keys[l], _pack_table(peer_u[l]), _pack_table(peer_v[l]))
        mod_s = [jnp.repeat(m, SAMPLE_PAD, axis=0) for m in jnp.split(mod_all[l, bp:], 6, axis=-1)]
        sp_groups = []
        for g, bg in enumerate(sizes):
            mod_g = [m.reshape(bg, 1, d) for m in jnp.split(mod_all[l, starts[g]:starts[g + 1]], 6, axis=-1)]
            zero_state = (jnp.zeros((bg, M_HEADS, M_HEAD_DIM, M_HEAD_DIM), F32),
                          jnp.zeros((bg, M_HEADS, M_HEAD_DIM), F32),
                          jnp.zeros((bg, SUBLANES, LANES), F32))
            zero_conv = jnp.zeros((bg, SUBLANES, 2 * M_WIDTH), F32)
            xg[g], sp, last_idx = _layer(xg[g], mod_g, False, bg, tp, tp, lw, bias_p, None, None, None,
                                         zero_conv, zero_state)
            sp_groups.append(sp)
        st_p.append([jnp.concatenate([sp[i] for sp in sp_groups], axis=0) for i in range(6)])
        state_s = (state_C[l].astype(F32), state_n[l].astype(F32),
                   jnp.broadcast_to(jnp.pad(state_m[l].astype(F32), ((0, 0), (0, SUBLANES - M_HEADS)))[:, :, None],
                                    (bs, SUBLANES, LANES)))
        kv_cache = (cache_k[l].reshape(bs, WINDOW, A_KV_WIDTH), cache_v[l].reshape(bs, WINDOW, A_KV_WIDTH))
        xs, ss, _ = _layer(xs, mod_s, True, bs, SAMPLE_PAD, ts, lw, None, bias_c, bias_n, kv_cache,
                           jnp.pad(state_conv[l].astype(F32), halo_pad), state_s, after=last_idx)
        st_s.append(ss)

    gfin = norm_final.reshape(1, d)
    y_prompt = jnp.concatenate([_final_call(x, gfin).reshape(bg, tp, d) for x, bg in zip(xg, sizes)], axis=0)
    y_sample = _final_call(xs, gfin).reshape(bs, SAMPLE_PAD, d)[:, :ts]
    outs_p = [jnp.stack([s[i] for s in st_p]) for i in range(6)]
    outs_s = [jnp.stack([s[i] for s in st_s]) for i in range(6)]
    return (y_prompt, y_sample, *outs_p, *outs_s)
```

```python
import functools
import math

import numpy as np
import jax
import jax.numpy as jnp
from jax import lax
from jax.experimental import pallas as pl
from jax.experimental.pallas import tpu as pltpu
from jax.experimental.pallas import tpu_sc as plsc

F32 = jnp.float32
BF16 = jnp.bfloat16
HIGHEST = lax.Precision.HIGHEST

D_MODEL = 1024
A_HEADS = 8
A_KV_HEADS = 2
A_GROUP = A_HEADS // A_KV_HEADS
A_HEAD_DIM = 64
A_WIDTH = A_HEADS * A_HEAD_DIM
A_KV_WIDTH = A_KV_HEADS * A_HEAD_DIM
WINDOW = 128
ATT_SCALE = A_HEAD_DIM ** -0.5
N_BUCKETS = 32
MAX_DISTANCE = WINDOW
M_HEADS = 4
M_HEAD_DIM = 128
M_WIDTH = M_HEADS * M_HEAD_DIM
CONV_W = 4
M_CHUNK = 64
N_KEYS = 128
P_HEADS = 8
P_TOPK = 16
P_KEY_DIM = 256
P_HALF = P_KEY_DIM // 2
EPS = 1e-6
NEG_INF = -1e30

LANES = 128
SUBLANES = 8
SAMPLE_PAD = SUBLANES
VMEM_LIMIT = 48 * 1024 * 1024
PROMPT_GROUPS = 6

NT_DIMS = (((1,), (1,)), ((), ()))
TN_DIMS = (((0,), (0,)), ((), ()))


def _cparams(*sem):
    return pltpu.CompilerParams(dimension_semantics=sem, vmem_limit_bytes=VMEM_LIMIT)


def _bdot(a, b):
    return jnp.dot(a.astype(BF16), b.astype(BF16), preferred_element_type=F32)


def _bdot_nt(a, b):
    return lax.dot_general(a.astype(BF16), b.astype(BF16), NT_DIMS, preferred_element_type=F32)


def _sigmoid(x):
    return 1.0 / (1.0 + jnp.exp(-x))


def _log_sigmoid(x):
    return jnp.minimum(x, 0.0) - jnp.log1p(jnp.exp(-jnp.abs(x)))


def _gelu_tanh(x):
    c = math.sqrt(2.0 / math.pi)
    return x * (0.5 * (1.0 + jnp.tanh(c * (x + 0.044715 * (x * x * x)))))


def _ada_kernel(c_ref, w_ref, b_ref, o_ref):
    c = c_ref[...]
    s = c * _sigmoid(c)
    o_ref[...] = jnp.dot(s, w_ref[...], precision=HIGHEST, preferred_element_type=F32) + b_ref[...]


def _ada_call(c_all, w_ada, b_ada):
    depth, d, n6 = w_ada.shape
    rows = c_all.shape[0]
    bn = 1024
    return pl.pallas_call(
        _ada_kernel,
        grid=(depth, n6 // bn),
        in_specs=[
            pl.BlockSpec((rows, d), lambda l, j: (0, 0)),
            pl.BlockSpec((None, d, bn), lambda l, j: (l, 0, j)),
            pl.BlockSpec((None, 1, bn), lambda l, j: (l, 0, j)),
        ],
        out_specs=pl.BlockSpec((None, rows, bn), lambda l, j: (l, 0, j)),
        out_shape=jax.ShapeDtypeStruct((depth, rows, n6), F32),
        compiler_params=_cparams("parallel", "parallel"),
        name="ada_mod",
    )(c_all, w_ada, b_ada.reshape(depth, 1, n6))


def _mod_spec(per_token, tm, tokens_per_req):
    if per_token:
        return pl.BlockSpec((tm, D_MODEL), lambda i: (i, 0))
    tiles = tokens_per_req // tm
    return pl.BlockSpec((None, 1, D_MODEL), lambda i: (i // tiles, 0, 0))


def _in_kernel(x_ref, sc_ref, sh_ref, g_ref, wa_ref, wqk_ref, wv_ref, wo_ref, wg_ref, gb_ref,
               qkv_ref, qkm_ref, v_ref, o_ref, gc_ref):
    x = x_ref[...]
    y = x * lax.rsqrt(jnp.mean(x * x, axis=-1, keepdims=True) + EPS) * g_ref[...]
    h = (y * (1.0 + sc_ref[...]) + sh_ref[...]).astype(BF16)
    qkv_ref[...] = jnp.dot(h, wa_ref[...], preferred_element_type=F32)
    qkm_ref[...] = jnp.dot(h, wqk_ref[...], preferred_element_type=F32)
    v_ref[...] = jnp.dot(h, wv_ref[...], preferred_element_type=F32)
    o_ref[...] = jnp.dot(h, wo_ref[...], preferred_element_type=F32)
    g = jnp.dot(h, wg_ref[...], preferred_element_type=F32) + gb_ref[...]
    lane = lax.broadcasted_iota(jnp.int32, g.shape, 1)
    gc_ref[...] = jnp.where(lane < M_HEADS, g, jnp.where(lane < 2 * M_HEADS, _log_sigmoid(g), 0.0))


def _in_call(x, sc, sh, gnorm, wa, wqk, wv, wo, wg, gb, per_token, tokens_per_req):
    n = x.shape[0]
    tm = min(512, n if per_token else tokens_per_req)
    mod = _mod_spec(per_token, tm, tokens_per_req)
    full = lambda shape: pl.BlockSpec(shape, lambda i: (0,) * len(shape))
    row = lambda w: pl.BlockSpec((tm, w), lambda i: (i, 0))
    return pl.pallas_call(
        _in_kernel,
        grid=(n // tm,),
        in_specs=[row(D_MODEL), mod, mod, full((1, D_MODEL)), full(wa.shape), full(wqk.shape),
                  full(wv.shape), full(wo.shape), full(wg.shape), full((1, LANES))],
        out_specs=[row(wa.shape[1]), row(wqk.shape[1]), row(wv.shape[1]), row(wo.shape[1]), row(LANES)],
        out_shape=[jax.ShapeDtypeStruct((n, w), F32)
                   for w in (wa.shape[1], wqk.shape[1], wv.shape[1], wo.shape[1], LANES)],
        compiler_params=_cparams("parallel"),
        name="in_proj",
    )(x, sc, sh, gnorm, wa, wqk, wv, wo, wg, gb)


def _t5_bucket_np(dist):
    n = np.maximum(dist, 0)
    max_exact = N_BUCKETS // 2
    nf = np.maximum(n, 1).astype(np.float64)
    large = max_exact + (np.log(nf / max_exact) / math.log(MAX_DISTANCE / max_exact)
                         * (N_BUCKETS - max_exact)).astype(np.int32)
    return np.where(n < max_exact, n, np.minimum(large, N_BUCKETS - 1)).astype(np.int32)


def _bias_kernel(bucket_ref, rel_ref, o_ref):
    bucket = bucket_ref[...]
    for h in range(A_HEADS):
        acc = jnp.zeros(bucket.shape, F32)
        for b in range(N_BUCKETS):
            acc = jnp.where(bucket == b, rel_ref[b, h], acc)
        o_ref[h] = acc


def _bias_call(rel_bias, dist):
    bucket = jnp.asarray(_t5_bucket_np(dist))
    nq, nk = dist.shape
    return pl.pallas_call(
        _bias_kernel,
        in_specs=[pl.BlockSpec((nq, nk), lambda: (0, 0)),
                  pl.BlockSpec(memory_space=pltpu.SMEM)],
        out_specs=pl.BlockSpec((A_HEADS, nq, nk), lambda: (0, 0, 0)),
        out_shape=jax.ShapeDtypeStruct((A_HEADS, nq, nk), F32),
        name="t5_bias",
    )(bucket, rel_bias)


def _attn_p_kernel(q_ref, kp_ref, kc_ref, vp_ref, vc_ref, bias_ref, sink_ref, o_ref):
    i = pl.program_id(1)
    qi = lax.broadcasted_iota(jnp.int32, (WINDOW, WINDOW), 0)
    kj = lax.broadcasted_iota(jnp.int32, (WINDOW, WINDOW), 1)
    valid_prev = jnp.logical_and(kj > qi, i > 0)
    valid_cur = kj <= qi
    q = q_ref[...]
    for h in range(A_HEADS):
        kv = h // A_GROUP
        qh = q[:, h * A_HEAD_DIM:(h + 1) * A_HEAD_DIM]
        sl = slice(kv * A_HEAD_DIM, (kv + 1) * A_HEAD_DIM)
        bias = bias_ref[h]
        sp = _bdot_nt(qh, kp_ref[:, sl]) * ATT_SCALE + bias[:, :WINDOW]
        sc = _bdot_nt(qh, kc_ref[:, sl]) * ATT_SCALE + bias[:, WINDOW:]
        sp = jnp.where(valid_prev, sp, NEG_INF)
        sc = jnp.where(valid_cur, sc, NEG_INF)
        sink = sink_ref[0, h]
        mx = jnp.maximum(jnp.maximum(jnp.max(sp, axis=-1, keepdims=True),
                                     jnp.max(sc, axis=-1, keepdims=True)), sink)
        ep = jnp.exp(sp - mx)
        ec = jnp.exp(sc - mx)
        den = (jnp.sum(ep, axis=-1, keepdims=True) + jnp.sum(ec, axis=-1, keepdims=True)
               + jnp.exp(sink - mx))
        o = _bdot(ep / den, vp_ref[:, sl]) + _bdot(ec / den, vc_ref[:, sl])
        o_ref[:, h * A_HEAD_DIM:(h + 1) * A_HEAD_DIM] = o


def _attn_p_call(qkv, bias, sinks, batch, seq):
    nb = seq // WINDOW
    n = batch * seq
    kcol = A_WIDTH // A_KV_WIDTH
    vcol = kcol + 1
    cur = lambda col: pl.BlockSpec((WINDOW, A_KV_WIDTH), lambda b, i: (b * nb + i, col))
    prev = lambda col: pl.BlockSpec((WINDOW, A_KV_WIDTH),
                                    lambda b, i: (b * nb + jnp.maximum(i - 1, 0), col))
    return pl.pallas_call(
        _attn_p_kernel,
        grid=(batch, nb),
        in_specs=[pl.BlockSpec((WINDOW, A_WIDTH), lambda b, i: (b * nb + i, 0)),
                  prev(kcol), cur(kcol), prev(vcol), cur(vcol),
                  pl.BlockSpec((A_HEADS, WINDOW, 2 * WINDOW), lambda b, i: (0, 0, 0)),
                  pl.BlockSpec(memory_space=pltpu.SMEM)],
        out_specs=pl.BlockSpec((WINDOW, A_WIDTH), lambda b, i: (b * nb + i, 0)),
        out_shape=jax.ShapeDtypeStruct((n, A_WIDTH), F32),
        compiler_params=_cparams("parallel", "parallel"),
        name="swa_prompt",
    )(qkv, qkv, qkv, qkv, qkv, bias, sinks.reshape(1, A_HEADS))


def _attn_s_kernel(n_new, qkv_ref, ck_ref, cv_ref, bc_ref, bn_ref, sink_ref,
                   o_ref, nk_ref, nv_ref, kk_s, vv_s):
    qkv = qkv_ref[...]
    knew = qkv[:, A_WIDTH:A_WIDTH + A_KV_WIDTH]
    vnew = qkv[:, A_WIDTH + A_KV_WIDTH:A_WIDTH + 2 * A_KV_WIDTH]
    ck = ck_ref[...]
    cv = cv_ref[...]
    rows = A_GROUP * SAMPLE_PAD
    qi = lax.broadcasted_iota(jnp.int32, (rows, WINDOW), 0) % SAMPLE_PAD
    kj = lax.broadcasted_iota(jnp.int32, (rows, WINDOW), 1)
    valid_c = kj > qi
    rcol = lax.broadcasted_iota(jnp.int32, (rows, 1), 0)
    qcol = rcol % SAMPLE_PAD
    for kv in range(A_KV_HEADS):
        heads = range(kv * A_GROUP, (kv + 1) * A_GROUP)
        sl = slice(kv * A_HEAD_DIM, (kv + 1) * A_HEAD_DIM)
        qs = jnp.concatenate([qkv[:, h * A_HEAD_DIM:(h + 1) * A_HEAD_DIM] for h in heads], axis=0)
        bias_c = jnp.concatenate([bc_ref[h] for h in heads], axis=0)
        bias_n = jnp.concatenate([bn_ref[h] for h in heads], axis=0)
        sink = jnp.zeros((rows, 1), F32)
        for g, h in enumerate(heads):
            sink = jnp.where(rcol // SAMPLE_PAD == g, sink_ref[0, h], sink)
        s_c = _bdot_nt(qs, ck[:, sl]) * ATT_SCALE + bias_c
        s_c = jnp.where(valid_c, s_c, NEG_INF)
        s_n = []
        for j in range(n_new):
            sj = jnp.sum(qs * knew[j:j + 1, sl], axis=-1, keepdims=True) * ATT_SCALE + bias_n[:, j:j + 1]
            s_n.append(jnp.where(qcol >= j, sj, NEG_INF))
        mx = jnp.maximum(jnp.max(s_c, axis=-1, keepdims=True), sink)
        for sj in s_n:
            mx = jnp.maximum(mx, sj)
        e_c = jnp.exp(s_c - mx)
        den = jnp.sum(e_c, axis=-1, keepdims=True) + jnp.exp(sink - mx)
        o = _bdot(e_c, cv[:, sl])
        for j, sj in enumerate(s_n):
            ej = jnp.exp(sj - mx)
            den = den + ej
            o = o + ej * vnew[j:j + 1, sl]
        o = o / den
        for g, h in enumerate(heads):
            o_ref[:, h * A_HEAD_DIM:(h + 1) * A_HEAD_DIM] = o[g * SAMPLE_PAD:(g + 1) * SAMPLE_PAD, :]
    kk_s[0:WINDOW, :] = ck
    kk_s[WINDOW:WINDOW + SAMPLE_PAD, :] = knew
    vv_s[0:WINDOW, :] = cv
    vv_s[WINDOW:WINDOW + SAMPLE_PAD, :] = vnew
    nk_ref[...] = kk_s[n_new:n_new + WINDOW, :]
    nv_ref[...] = vv_s[n_new:n_new + WINDOW, :]


def _attn_s_call(qkv, ck, cv, bias_c, bias_n, sinks, n_new):
    nreq = ck.shape[0]
    wq = qkv.shape[1]
    full3 = lambda shape: pl.BlockSpec(shape, lambda b: (0, 0, 0))
    cache = pl.BlockSpec((None, WINDOW, A_KV_WIDTH), lambda b: (b, 0, 0))
    return pl.pallas_call(
        functools.partial(_attn_s_kernel, n_new),
        grid=(nreq,),
        in_specs=[pl.BlockSpec((SAMPLE_PAD, wq), lambda b: (b, 0)), cache, cache,
                  full3(bias_c.shape), full3(bias_n.shape),
                  pl.BlockSpec(memory_space=pltpu.SMEM)],
        out_specs=[pl.BlockSpec((SAMPLE_PAD, A_WIDTH), lambda b: (b, 0)), cache, cache],
        out_shape=[jax.ShapeDtypeStruct((nreq * SAMPLE_PAD, A_WIDTH), F32),
                   jax.ShapeDtypeStruct(ck.shape, F32), jax.ShapeDtypeStruct(cv.shape, F32)],
        scratch_shapes=[pltpu.VMEM((WINDOW + SAMPLE_PAD, A_KV_WIDTH), F32),
                        pltpu.VMEM((WINDOW + SAMPLE_PAD, A_KV_WIDTH), F32)],
        compiler_params=_cparams("parallel"),
        name="swa_sample",
    )(qkv, ck, cv, bias_c, bias_n, sinks.reshape(1, A_HEADS))


def _mlstm_kernel(chunk, t_valid, qk_ref, v_ref, og_ref, gc_ref, cw_ref, cb_ref, mn_ref,
                  conv0_ref, c0_ref, n0_ref, m0_ref,
                  out_ref, cout_ref, nout_ref, mout_ref,
                  xp_s, c_s, n_s, m_s):
    step = pl.program_id(1)
    halo = SUBLANES

    @pl.when(step == 0)
    def _():
        xp_s[0:halo, :] = conv0_ref[...]
        c_s[...] = c0_ref[...]
        n_s[...] = n0_ref[...]
        m_s[...] = m0_ref[...]

    xp_s[halo:halo + chunk, :] = qk_ref[...]
    cw = cw_ref[...]
    y = cb_ref[...]
    for i in range(CONV_W):
        off = halo - (CONV_W - 1) + i
        y = y + xp_s[off:off + chunk, :] * cw[i:i + 1, :]
    xp_s[0:halo, :] = xp_s[chunk:chunk + halo, :]
    y = y * _sigmoid(y)
    q_all = y[:, :M_WIDTH]
    k_all = y[:, M_WIDTH:] * (M_HEAD_DIM ** -0.5)

    g = gc_ref[...]
    if t_valid < chunk:
        row = lax.broadcasted_iota(jnp.int32, g.shape, 0)
        lane = lax.broadcasted_iota(jnp.int32, g.shape, 1)
        g = jnp.where(row < t_valid, g, jnp.where(lane < M_HEADS, NEG_INF, 0.0))
    tr = lax.broadcasted_iota(jnp.int32, (chunk, chunk), 0)
    tc = lax.broadcasted_iota(jnp.int32, (chunk, chunk), 1)
    causal = tr >= tc
    tri = causal.astype(F32)
    bcol = jnp.dot(tri, g, precision=HIGHEST, preferred_element_type=F32)
    er = lax.broadcasted_iota(jnp.int32, (SUBLANES, LANES), 0)
    ec = lax.broadcasted_iota(jnp.int32, (SUBLANES, LANES), 1)
    eye = (er == ec).astype(F32)
    g_rows = lax.dot_general(eye, g, NT_DIMS, precision=HIGHEST, preferred_element_type=F32)
    b_rows = lax.dot_general(eye, bcol, NT_DIMS, precision=HIGHEST, preferred_element_type=F32)

    for h in range(M_HEADS):
        hs = slice(h * M_HEAD_DIM, (h + 1) * M_HEAD_DIM)
        b_c = bcol[:, M_HEADS + h:M_HEADS + h + 1]
        ig_c = g[:, h:h + 1]
        b_r = b_rows[M_HEADS + h:M_HEADS + h + 1, :]
        ig_r = g_rows[h:h + 1, :]
        m_prev = m_s[h:h + 1, 0:1]
        logw = jnp.where(causal, b_c - b_r + ig_r, -jnp.inf)
        inter = b_c + m_prev
        m_t = jnp.maximum(inter, jnp.max(logw, axis=-1, keepdims=True))
        w = jnp.exp(logw - m_t)
        a = jnp.exp(inter - m_t)
        q = q_all[:, hs]
        k = k_all[:, hs]
        v = v_ref[:, hs]
        cmat = c_s[h]
        nvec = n_s[h:h + 1, :]
        wqk = w * _bdot_nt(q, k)
        num = _bdot(wqk, v) + a * _bdot_nt(q, cmat)
        den = jnp.sum(wqk, axis=-1, keepdims=True) + a * jnp.sum(q * nvec, axis=-1, keepdims=True)
        hh = num / jnp.maximum(jnp.abs(den), jnp.exp(-m_t))
        m_new = m_t[chunk - 1:chunk, :]
        b_last = b_c[chunk - 1:chunk, :]
        wl = jnp.exp(b_last - b_c + ig_c - m_new)
        al = jnp.exp(b_last + m_prev - m_new)
        c_s[h] = al * cmat + lax.dot_general((v * wl).astype(BF16), k.astype(BF16), TN_DIMS,
                                             preferred_element_type=F32)
        n_s[h:h + 1, :] = al * nvec + jnp.sum(wl * k, axis=0, keepdims=True)
        m_s[h:h + 1, :] = jnp.broadcast_to(m_new, (1, LANES))
        hn = hh * lax.rsqrt(jnp.mean(hh * hh, axis=-1, keepdims=True) + EPS) * mn_ref[:, hs]
        out_ref[:, hs] = _sigmoid(og_ref[:, hs]) * hn

    @pl.when(step == pl.num_programs(1) - 1)
    def _():
        cout_ref[...] = c_s[...]
        nout_ref[...] = n_s[...]
        mout_ref[...] = m_s[...]


def _mlstm_call(qk, v, og, gc, conv_w, conv_b, m_norm, conv0, c0, n0, m0, batch, seq, chunk, t_valid):
    nc = seq // chunk
    n = batch * seq
    row = lambda w: pl.BlockSpec((chunk, w), lambda b, c: (b * nc + c, 0))
    full2 = lambda shape: pl.BlockSpec(shape, lambda b, c: (0, 0))
    per_b = lambda shape: pl.BlockSpec((None,) + shape, lambda b, c: (b,) + (0,) * len(shape))
    dh = M_HEAD_DIM
    return pl.pallas_call(
        functools.partial(_mlstm_kernel, chunk, t_valid),
        grid=(batch, nc),
        in_specs=[row(2 * M_WIDTH), row(M_WIDTH), row(M_WIDTH), row(LANES),
                  full2((CONV_W, 2 * M_WIDTH)), full2((1, 2 * M_WIDTH)), full2((1, M_WIDTH)),
                  per_b((SUBLANES, 2 * M_WIDTH)), per_b((M_HEADS, dh, dh)), per_b((M_HEADS, dh)),
                  per_b((SUBLANES, LANES))],
        out_specs=[row(M_WIDTH), per_b((M_HEADS, dh, dh)), per_b((M_HEADS, dh)), per_b((SUBLANES, LANES))],
        out_shape=[jax.ShapeDtypeStruct((n, M_WIDTH), F32),
                   jax.ShapeDtypeStruct((batch, M_HEADS, dh, dh), F32),
                   jax.ShapeDtypeStruct((batch, M_HEADS, dh), F32),
                   jax.ShapeDtypeStruct((batch, SUBLANES, LANES), F32)],
        scratch_shapes=[pltpu.VMEM((SUBLANES + chunk, 2 * M_WIDTH), F32),
                        pltpu.VMEM((M_HEADS, dh, dh), F32),
                        pltpu.VMEM((M_HEADS, dh), F32),
                        pltpu.VMEM((SUBLANES, LANES), F32)],
        compiler_params=_cparams("parallel", "arbitrary"),
        name="mlstm",
    )(qk, v, og, gc, conv_w, conv_b, m_norm, conv0, c0, n0, m0)


def _out_kernel(att_ref, mo_ref, x_ref, g1_ref, sc_ref, sh_ref, gn_ref, wa_ref, wm_ref, wq_ref,
                xo_ref, h2_ref, qp_ref):
    mix = (jnp.dot(att_ref[...].astype(BF16), wa_ref[...], preferred_element_type=F32)
           + jnp.dot(mo_ref[...].astype(BF16), wm_ref[...], preferred_element_type=F32))
    x = x_ref[...] + g1_ref[...] * mix
    xo_ref[...] = x
    y = x * lax.rsqrt(jnp.mean(x * x, axis=-1, keepdims=True) + EPS) * gn_ref[...]
    h2 = y * (1.0 + sc_ref[...]) + sh_ref[...]
    qp_ref[...] = jnp.dot(h2.astype(BF16), wq_ref[...], preferred_element_type=F32)
    h2_ref[...] = _pack_words(h2)


def _out_call(att, mo, x, g1, sc, sh, gnorm, wa, wm, wq, per_token, tokens_per_req):
    n = x.shape[0]
    tm = min(256, n if per_token else tokens_per_req)
    mod = _mod_spec(per_token, tm, tokens_per_req)
    full = lambda shape: pl.BlockSpec(shape, lambda i: (0,) * len(shape))
    row = lambda w: pl.BlockSpec((tm, w), lambda i: (i, 0))
    nq = wq.shape[1]
    return pl.pallas_call(
        _out_kernel,
        grid=(n // tm,),
        in_specs=[row(A_WIDTH), row(M_WIDTH), row(D_MODEL), mod, mod, mod, full((1, D_MODEL)),
                  full(wa.shape), full(wm.shape), full(wq.shape)],
        out_specs=[row(D_MODEL), row(D_MODEL // 2), row(nq)],
        out_shape=[jax.ShapeDtypeStruct((n, D_MODEL), F32), jax.ShapeDtypeStruct((n, D_MODEL // 2), jnp.int32),
                   jax.ShapeDtypeStruct((n, nq), F32)],
        compiler_params=_cparams("parallel"),
        name="out_proj",
    )(att, mo, x, g1, sc, sh, gnorm, wa, wm, wq)


def _pk_cells():
    return [(a, b) for a in range(P_TOPK) for b in range(P_TOPK) if (a + 1) * (b + 1) <= P_TOPK]


PK_CELL_ROWS = 64


def _pk_expand_mats():
    cells = _pk_cells()
    e0 = np.zeros((PK_CELL_ROWS, LANES), np.float32)
    e1 = np.zeros((PK_CELL_ROWS, LANES), np.float32)
    for j, (a, b) in enumerate(cells):
        e0[j, a] = 1.0
        e1[j, b] = 1.0
    return e0, e1, len(cells)


def _top_rows(s, rowf, rounds):
    n_rows = s.shape[0]
    vals, idxs = [], []
    for _ in range(rounds):
        m = jnp.max(s, axis=0, keepdims=True)
        i = jnp.min(jnp.where(s == m, rowf, float(n_rows)), axis=0, keepdims=True)
        vals.append(m)
        idxs.append(i)
        s = jnp.where(rowf == i, -jnp.inf, s)
    return jnp.concatenate(vals, axis=0), jnp.concatenate(idxs, axis=0)


def _select_kernel(n_cells, qp_ref, keys_ref, e0_ref, e1_ref, idx_ref, gw_ref, idx_s, gw_s):
    tm = qp_ref.shape[0]
    keyf = lax.broadcasted_iota(jnp.int32, (N_KEYS, tm), 0).astype(F32)
    cellf = lax.broadcasted_iota(jnp.int32, (PK_CELL_ROWS, tm), 0).astype(F32)
    e0 = e0_ref[...]
    e1 = e1_ref[...]
    pad = jnp.zeros((LANES - P_TOPK, tm), F32)

    def head(h, carry):
        h = jnp.asarray(h, jnp.int32)
        sub = []
        for c in range(2):
            col = pl.multiple_of((h * 2 + c) * P_HALF, P_HALF)
            s = _bdot_nt(keys_ref[h, c], qp_ref[:, pl.ds(col, P_HALF)])
            sub.append(_top_rows(s, keyf, P_TOPK))
        (v0, i0), (v1, i1) = sub
        expand = lambda e, x: jnp.dot(e, jnp.concatenate([x, pad], axis=0), precision=HIGHEST,
                                      preferred_element_type=F32)
        cand = expand(e0, v0) + expand(e1, v1)
        cidx = expand(e0, i0 * float(N_KEYS)) + expand(e1, i1)
        cand = jnp.where(cellf < n_cells, cand, -jnp.inf)
        best, eidx = [], []
        for _ in range(P_TOPK):
            m = jnp.max(cand, axis=0, keepdims=True)
            j = jnp.min(jnp.where(cand == m, cellf, float(PK_CELL_ROWS)), axis=0, keepdims=True)
            hit = cellf == j
            eidx.append(jnp.max(jnp.where(hit, cidx, -1.0), axis=0, keepdims=True))
            best.append(m)
            cand = jnp.where(hit, -jnp.inf, cand)
        best = jnp.concatenate(best, axis=0)
        e = jnp.exp(best - best[0:1, :])
        row0 = pl.multiple_of(h * P_TOPK, P_TOPK)
        gw_s[pl.ds(row0, P_TOPK), :] = e / jnp.sum(e, axis=0, keepdims=True)
        idx_s[pl.ds(row0, P_TOPK), :] = jnp.concatenate(eidx, axis=0)
        return carry

    lax.fori_loop(0, P_HEADS, head, 0)
    idx_ref[...] = idx_s[...].T.astype(jnp.int32)
    gw_ref[...] = gw_s[...].T


def _select_call(qp, keys_bf16):
    n = qp.shape[0]
    tm = min(LANES, n)
    e0, e1, n_cells = _pk_expand_mats()
    full = lambda shape: pl.BlockSpec(shape, lambda i: (0,) * len(shape))
    return pl.pallas_call(
        functools.partial(_select_kernel, n_cells),
        grid=(n // tm,),
        in_specs=[pl.BlockSpec((tm, qp.shape[1]), lambda i: (i, 0)), full(keys_bf16.shape),
                  full((PK_CELL_ROWS, LANES)), full((PK_CELL_ROWS, LANES))],
        out_specs=[pl.BlockSpec((tm, LANES), lambda i: (i, 0)), pl.BlockSpec((tm, LANES), lambda i: (i, 0))],
        out_shape=[jax.ShapeDtypeStruct((n, LANES), jnp.int32), jax.ShapeDtypeStruct((n, LANES), F32)],
        scratch_shapes=[pltpu.VMEM((P_HEADS * P_TOPK, tm), F32), pltpu.VMEM((P_HEADS * P_TOPK, tm), F32)],
        compiler_params=_cparams("parallel"),
        name="peer_select",
    )(qp, keys_bf16, jnp.asarray(e0), jnp.asarray(e1))


N_SEL = P_HEADS * P_TOPK

SC_CORES = 2
SC_SUBCORES = 16
SC_LANES = 16
SC_TOK_BLOCK = 8
SC_ROWS = 32
SC_ROW_BLOCK = 16
SC_ACC_CHAINS = 4
SC_ACC_ROWS = 16
SC_NBUF = 4
N_WCOL = D_MODEL // (2 * SC_LANES)


def _pack_words(x):
    bits = pltpu.bitcast(x.astype(BF16).astype(F32), jnp.uint32)
    half = x.shape[1] // 2
    lo = lax.shift_right_logical(bits[:, :half], jnp.uint32(16))
    hi = bits[:, half:] & jnp.uint32(0xFFFF0000)
    return pltpu.bitcast(lo | hi, jnp.int32)


def _pack_kernel(t_ref, o_ref):
    o_ref[...] = _pack_words(t_ref[...])


def _pack_table(t):
    e, dcol = t.shape
    tm = min(1024, e)
    return pl.pallas_call(
        _pack_kernel,
        grid=(e // tm,),
        in_specs=[pl.BlockSpec((tm, dcol), lambda i: (i, 0))],
        out_specs=pl.BlockSpec((tm, dcol // 2), lambda i: (i, 0)),
        out_shape=jax.ShapeDtypeStruct((e, dcol // 2), jnp.int32),
        compiler_params=_cparams("parallel"),
        name="pack_table",
    )(t)


def _sc_gelu(x):
    z = math.sqrt(2.0 / math.pi) * (x + 0.044715 * (x * x * x))
    t = 1.0 - 2.0 / (jnp.exp(2.0 * z) + 1.0)
    return x * (0.5 * (1.0 + t))


def _sc_expert_body(tokens_per_worker, idx_hbm, gw_hbm, h2_hbm, u_hbm, v_hbm, y_hbm,
                    idx_v, gw_v, x_v, o_v, buf, coef_v, tr_v, sem, in_sem, out_sem):
    wid = lax.axis_index("s") * SC_CORES + lax.axis_index("c")
    base = wid * tokens_per_worker
    lane = lax.iota(jnp.int32, SC_LANES)
    n_gather = N_SEL // SC_ROWS
    zero = jnp.zeros((SC_LANES,), F32)

    n_steps = 2 * n_gather
    assert n_steps % SC_NBUF == 0

    def gather(p, tt, i):
        table = u_hbm if i < n_gather else v_hbm
        j = i % n_gather
        slot = i % SC_NBUF
        return pltpu.make_async_copy(table.at[idx_v.at[p, tt, pl.ds(j * SC_ROWS, SC_ROWS)]], buf.at[slot],
                                     sem.at[slot])

    def unpack(w):
        lo = lax.bitcast_convert_type(lax.shift_left(w, jnp.full((SC_LANES,), 16, jnp.int32)), F32)
        hi = lax.bitcast_convert_type(w & jnp.full((SC_LANES,), -65536, jnp.int32), F32)
        return lo, hi

    def packed(w):
        return plsc.bitcast(w, BF16)

    def unpack_sum(s):
        return unpack(plsc.bitcast(s, jnp.int32))

    def act_chunk(p, tt, j, slot):
        @pl.loop(0, SC_ROWS // SC_LANES)
        def _(half):
            for rb in range(SC_LANES // SC_ROW_BLOCK):
                r0 = half * SC_LANES + rb * SC_ROW_BLOCK

                def col(c, accs):
                    w0 = pl.multiple_of(c * (2 * SC_LANES), 2 * SC_LANES)
                    xa = packed(x_v[p, tt, pl.ds(w0, SC_LANES)])
                    xb = packed(x_v[p, tt, pl.ds(w0 + SC_LANES, SC_LANES)])
                    out = []
                    for r, a in enumerate(accs):
                        ua = packed(buf[slot, r0 + r, pl.ds(w0, SC_LANES)])
                        ub = packed(buf[slot, r0 + r, pl.ds(w0 + SC_LANES, SC_LANES)])
                        lo, hi = unpack_sum(ua * xa + ub * xb)
                        out.append(a + lo + hi)
                    return tuple(out)

                accs = lax.fori_loop(0, N_WCOL // 2, col, (zero,) * SC_ROW_BLOCK)
                for r in range(SC_ROW_BLOCK):
                    tr_v[pl.ds((rb * SC_ROW_BLOCK + r) * SC_LANES, SC_LANES)] = accs[r]
            tot = zero
            for jj in range(SC_LANES):
                tot = tot + plsc.load_gather(tr_v, [lane * SC_LANES + jj])
            k0 = pl.multiple_of(j * SC_ROWS + half * SC_LANES, SC_LANES)
            coef_v[pl.ds(k0, SC_LANES)] = gw_v[p, tt, pl.ds(k0, SC_LANES)] * _sc_gelu(tot)

    def acc_chunk(p, tt, j, slot, first):
        def tree_sum(parts):
            while len(parts) > 1:
                parts = [parts[i] + parts[i + 1] for i in range(0, len(parts), 2)]
            return parts[0]

        for rb in range(SC_ROWS // SC_ACC_ROWS):
            rows = list(range(rb * SC_ACC_ROWS, (rb + 1) * SC_ACC_ROWS))
            splat = {}
            for r in rows:
                c16 = plsc.load_gather(coef_v, [jnp.full((SC_LANES,), j * SC_ROWS + r, jnp.int32)])
                splat[r] = plsc.pack(c16, c16, format=plsc.PackFormat.INTERLEAVED)
            fresh = first and rb == 0

            @plsc.parallel_loop(0, N_WCOL)
            def _(c):
                woff = pl.multiple_of(c * SC_LANES, SC_LANES)
                hoff = pl.multiple_of(c * SC_LANES + D_MODEL // 2, SC_LANES)
                pa, pb = [], []
                for n in range(0, SC_ACC_ROWS, 2):
                    r0, r1 = rows[n], rows[n + 1]
                    s = (splat[r0] * packed(buf[slot, r0, pl.ds(woff, SC_LANES)])
                         + splat[r1] * packed(buf[slot, r1, pl.ds(woff, SC_LANES)]))
                    lo, hi = unpack_sum(s)
                    if n // 2 < SC_ACC_CHAINS:
                        pa.append(lo)
                        pb.append(hi)
                    else:
                        pa[(n // 2) % SC_ACC_CHAINS] = pa[(n // 2) % SC_ACC_CHAINS] + lo
                        pb[(n // 2) % SC_ACC_CHAINS] = pb[(n // 2) % SC_ACC_CHAINS] + hi
                sa, sb = tree_sum(pa), tree_sum(pb)
                if not fresh:
                    sa = sa + o_v[p, tt, pl.ds(woff, SC_LANES)]
                    sb = sb + o_v[p, tt, pl.ds(hoff, SC_LANES)]
                o_v[p, tt, pl.ds(woff, SC_LANES)] = sa
                o_v[p, tt, pl.ds(hoff, SC_LANES)] = sb

    n_blocks = tokens_per_worker // SC_TOK_BLOCK

    def block_start(blk):
        return pl.multiple_of(base + blk * SC_TOK_BLOCK, SC_TOK_BLOCK)

    def in_copies(blk, p):
        rows = pl.ds(block_start(blk), SC_TOK_BLOCK)
        return [pltpu.make_async_copy(idx_hbm.at[rows], idx_v.at[p], in_sem.at[p]),
                pltpu.make_async_copy(gw_hbm.at[rows], gw_v.at[p], in_sem.at[p]),
                pltpu.make_async_copy(h2_hbm.at[rows], x_v.at[p], in_sem.at[p])]

    def out_copy(blk, p):
        return pltpu.make_async_copy(o_v.at[p], y_hbm.at[pl.ds(block_start(blk), SC_TOK_BLOCK)], out_sem.at[p])

    for cp in in_copies(0, 0):
        cp.start()

    @pl.loop(0, n_blocks)
    def _(blk):
        p = lax.rem(blk, 2)
        for cp in in_copies(blk, p):
            cp.wait()

        @pl.when(blk + 1 < n_blocks)
        def _():
            for cp in in_copies(blk + 1, 1 - p):
                cp.start()

        @pl.when(blk >= 2)
        def _():
            out_copy(blk - 2, p).wait()

        ahead = SC_NBUF - 1
        for i in range(ahead):
            gather(p, 0, i).start()

        @pl.loop(0, SC_TOK_BLOCK)
        def _(tt):
            for i in range(n_steps):
                if i + ahead < n_steps:
                    gather(p, tt, i + ahead).start()
                else:
                    @pl.when(tt + 1 < SC_TOK_BLOCK)
                    def _():
                        gather(p, tt + 1, i + ahead - n_steps).start()
                gather(p, tt, i).wait()
                if i < n_gather:
                    act_chunk(p, tt, i, i % SC_NBUF)
                else:
                    acc_chunk(p, tt, i - n_gather, i % SC_NBUF, i == n_gather)

        out_copy(blk, p).start()

    for blk in (n_blocks - 2, n_blocks - 1):
        out_copy(blk, blk % 2).wait()


def _sc_expert_call(idx, gw, xw, u, v):
    n = idx.shape[0]
    workers = SC_CORES * SC_SUBCORES
    assert n % (workers * SC_TOK_BLOCK) == 0 and n // (workers * SC_TOK_BLOCK) >= 2
    mesh = plsc.VectorSubcoreMesh(core_axis_name="c", subcore_axis_name="s")
    return pl.kernel(
        functools.partial(_sc_expert_body, n // workers),
        out_type=jax.ShapeDtypeStruct((n, D_MODEL), F32),
        mesh=mesh,
        scratch_types=[pltpu.VMEM((2, SC_TOK_BLOCK, N_SEL), jnp.int32),
                       pltpu.VMEM((2, SC_TOK_BLOCK, N_SEL), F32),
                       pltpu.VMEM((2, SC_TOK_BLOCK, D_MODEL // 2), jnp.int32),
                       pltpu.VMEM((2, SC_TOK_BLOCK, D_MODEL), F32),
                       pltpu.VMEM((SC_NBUF, SC_ROWS, D_MODEL // 2), jnp.int32),
                       pltpu.VMEM((N_SEL,), F32),
                       pltpu.VMEM((SC_LANES * SC_LANES,), F32),
                       pltpu.SemaphoreType.DMA((SC_NBUF,)), pltpu.SemaphoreType.DMA((2,)),
                       pltpu.SemaphoreType.DMA((2,))],
        compiler_params=pltpu.CompilerParams(needs_layout_passes=False),
        name="peer_experts_sc",
    )(idx, gw, xw, u, v)


def _resid_kernel(x_ref, y_ref, g_ref, o_ref):
    o_ref[...] = x_ref[...] + g_ref[...] * y_ref[...]


def _resid_call(x, y, g2, per_token, tokens_per_req):
    n = x.shape[0]
    tm = min(512, n if per_token else tokens_per_req)
    row = pl.BlockSpec((tm, D_MODEL), lambda i: (i, 0))
    return pl.pallas_call(
        _resid_kernel,
        grid=(n // tm,),
        in_specs=[row, row, _mod_spec(per_token, tm, tokens_per_req)],
        out_specs=row,
        out_shape=jax.ShapeDtypeStruct((n, D_MODEL), F32),
        compiler_params=_cparams("parallel"),
        name="peer_residual",
    )(x, y, g2)


def _final_kernel(x_ref, g_ref, o_ref):
    x = x_ref[...]
    o_ref[...] = x * lax.rsqrt(jnp.mean(x * x, axis=-1, keepdims=True) + EPS) * g_ref[...]


def _final_call(x, g):
    n = x.shape[0]
    tm = min(512, n)
    return pl.pallas_call(
        _final_kernel,
        grid=(n // tm,),
        in_specs=[pl.BlockSpec((tm, D_MODEL), lambda i: (i, 0)), pl.BlockSpec((1, D_MODEL), lambda i: (0, 0))],
        out_specs=pl.BlockSpec((tm, D_MODEL), lambda i: (i, 0)),
        out_shape=jax.ShapeDtypeStruct((n, D_MODEL), F32),
        compiler_params=_cparams("parallel"),
        name="final_norm",
    )(x, g)


def _split_w_in(w_in_l, gate_b_l):
    cuts = np.cumsum([A_WIDTH + 2 * A_KV_WIDTH, 2 * M_WIDTH, M_WIDTH, M_WIDTH]).tolist()
    wa = w_in_l[:, :cuts[0]].astype(BF16)
    wqk = w_in_l[:, cuts[0]:cuts[1]].astype(BF16)
    wv = w_in_l[:, cuts[1]:cuts[2]].astype(BF16)
    wo = w_in_l[:, cuts[2]:cuts[3]].astype(BF16)
    ng = 2 * M_HEADS
    wg = jnp.pad(w_in_l[:, cuts[3]:], ((0, 0), (0, LANES - ng))).astype(BF16)
    gb = jnp.pad(gate_b_l.astype(F32), (0, LANES - ng)).reshape(1, LANES)
    return wa, wqk, wv, wo, wg, gb


def _layer(x, mods, per_token, batch, seq, t_valid, lw, bias_p, bias_c, bias_n, kv_cache, conv0, state,
           after=None):
    (norm_mix, norm_ffn, w_in, conv_w, conv_b, gate_b, sinks, m_norm, w_out, peer_query, peer_keys,
     peer_u, peer_v) = lw
    if after is not None:
        x, _ = lax.optimization_barrier((x, after))
    sh1, sc1, g1, sh2, sc2, g2 = mods
    wa, wqk, wv, wo, wg, gb = _split_w_in(w_in, gate_b)
    qkv, qkm, vm, om, gc = _in_call(x, sc1, sh1, norm_mix.reshape(1, -1), wa, wqk, wv, wo, wg, gb,
                                    per_token, seq)
    if kv_cache is None:
        att = _attn_p_call(qkv, bias_p, sinks, batch, seq)
        kv3 = qkv.reshape(batch, seq, -1)
        new_k = kv3[:, seq - WINDOW:, A_WIDTH:A_WIDTH + A_KV_WIDTH]
        new_v = kv3[:, seq - WINDOW:, A_WIDTH + A_KV_WIDTH:]
        chunk = M_CHUNK
    else:
        att, new_k, new_v = _attn_s_call(qkv, kv_cache[0], kv_cache[1], bias_c, bias_n, sinks, t_valid)
        chunk = seq
    c0, n0, m0 = state
    mo, c_new, n_new, m_new = _mlstm_call(qkm, vm, om, gc, conv_w, conv_b.reshape(1, -1),
                                          m_norm.reshape(1, -1), conv0, c0, n0, m0,
                                          batch, seq, chunk, min(t_valid, chunk))
    new_conv = qkm.reshape(batch, seq, -1)[:, t_valid - (CONV_W - 1):t_valid]
    x_mid, xw, qp = _out_call(att, mo, x, g1, sc2, sh2, norm_ffn.reshape(1, -1),
                              w_out[:A_WIDTH].astype(BF16), w_out[A_WIDTH:].astype(BF16),
                              peer_query.astype(BF16), per_token, seq)
    idx, gw = _select_call(qp, peer_keys.astype(BF16))
    y = _sc_expert_call(idx, gw, xw, peer_u, peer_v)
    x_new = _resid_call(x_mid, y, g2, per_token, seq)
    new_k = new_k.reshape(batch, WINDOW, A_KV_HEADS, A_HEAD_DIM)
    new_v = new_v.reshape(batch, WINDOW, A_KV_HEADS, A_HEAD_DIM)
    return x_new, (new_k, new_v, new_conv, c_new, n_new, m_new[:, :M_HEADS, 0]), idx


def _prompt_group_sizes(n_req):
    if n_req < PROMPT_GROUPS:
        return [1] * n_req
    mid, n_mid = n_req - 2, PROMPT_GROUPS - 2
    weights = [i + 2 for i in range(n_mid)]
    sizes = [max(1, mid * w // sum(weights)) for w in weights]
    for i in range(mid - sum(sizes)):
        sizes[n_mid - 1 - i % n_mid] += 1
    return [1] + sizes + [1]


def kernel(x_prompt, x_sample, c_prompt, c_sample, cache_k, cache_v, state_conv, state_C, state_n, state_m, rel_bias, w_ada, b_ada, norm_mix, norm_ffn, w_in, conv_w, conv_b, gate_b, attn_sinks, m_norm, w_out, peer_query, peer_keys, peer_u, peer_v, norm_final):
    depth = w_ada.shape[0]
    bp, tp, d = x_prompt.shape
    bs, ts, _ = x_sample.shape
    assert tp % WINDOW == 0 and tp % M_CHUNK == 0 and ts <= SAMPLE_PAD and ts >= CONV_W - 1

    mod_all = _ada_call(jnp.concatenate([c_prompt, c_sample], axis=0), w_ada, b_ada)

    qi = np.arange(WINDOW)[:, None]
    bias_p = _bias_call(rel_bias, qi + WINDOW - np.arange(2 * WINDOW)[None, :])
    qs = np.arange(SAMPLE_PAD)[:, None]
    bias_c = _bias_call(rel_bias, qs + WINDOW - np.arange(WINDOW)[None, :])
    bias_n = _bias_call(rel_bias, qs - np.arange(SAMPLE_PAD)[None, :])

    sizes = _prompt_group_sizes(bp)
    starts = np.cumsum([0] + sizes).tolist()
    xg = [x_prompt[starts[g]:starts[g + 1]].reshape(sizes[g] * tp, d) for g in range(len(sizes))]
    xs = jnp.pad(x_sample, ((0, 0), (0, SAMPLE_PAD - ts), (0, 0))).reshape(bs * SAMPLE_PAD, d)
    halo_pad = ((0, 0), (SUBLANES - (CONV_W - 1), 0), (0, 0))

    st_p, st_s = [], []
    for l in range(depth):
        lw = (norm_mix[l], norm_ffn[l], w_in[l], conv_w[l], conv_b[l], gate_b[l], attn_sinks[l], m_norm[l],
              w_out[l], peer_query[l], peer_keys[l], _pack_table(peer_u[l]), _pack_table(peer_v[l]))
        mod_s = [jnp.repeat(m, SAMPLE_PAD, axis=0) for m in jnp.split(mod_all[l, bp:], 6, axis=-1)]
        sp_groups = []
        for g, bg in enumerate(sizes):
            mod_g = [m.reshape(bg, 1, d) for m in jnp.split(mod_all[l, starts[g]:starts[g + 1]], 6, axis=-1)]
            zero_state = (jnp.zeros((bg, M_HEADS, M_HEAD_DIM, M_HEAD_DIM), F32),
                          jnp.zeros((bg, M_HEADS, M_HEAD_DIM), F32),
                          jnp.zeros((bg, SUBLANES, LANES), F32))
            zero_conv = jnp.zeros((bg, SUBLANES, 2 * M_WIDTH), F32)
            xg[g], sp, last_idx = _layer(xg[g], mod_g, False, bg, tp, tp, lw, bias_p, None, None, None,
                                         zero_conv, zero_state)
            sp_groups.append(sp)
        st_p.append([jnp.concatenate([sp[i] for sp in sp_groups], axis=0) for i in range(6)])
        state_s = (state_C[l].astype(F32), state_n[l].astype(F32),
                   jnp.broadcast_to(jnp.pad(state_m[l].astype(F32), ((0, 0), (0, SUBLANES - M_HEADS)))[:, :, None],
                                    (bs, SUBLANES, LANES)))
        kv_cache = (cache_k[l].reshape(bs, WINDOW, A_KV_WIDTH), cache_v[l].reshape(bs, WINDOW, A_KV_WIDTH))
        xs, ss, _ = _layer(xs, mod_s, True, bs, SAMPLE_PAD, ts, lw, None, bias_c, bias_n, kv_cache,
                           jnp.pad(state_conv[l].astype(F32), halo_pad), state_s, after=last_idx)
        st_s.append(ss)

    gfin = norm_final.reshape(1, d)
    y_prompt = jnp.concatenate([_final_call(x, gfin).reshape(bg, tp, d) for x, bg in zip(xg, sizes)], axis=0)
    y_sample = _final_call(xs, gfin).reshape(bs, SAMPLE_PAD, d)[:, :ts]
    outs_p = [jnp.stack([s[i] for s in st_p]) for i in range(6)]
    outs_s = [jnp.stack([s[i] for s in st_s]) for i in range(6)]
    return (y_prompt, y_sample, *outs_p, *outs_s)
```

```python
import functools
import math

import numpy as np
import jax
import jax.numpy as jnp
from jax import lax
from jax.experimental import pallas as pl
from jax.experimental.pallas import tpu as pltpu
from jax.experimental.pallas import tpu_sc as plsc

F32 = jnp.float32
BF16 = jnp.bfloat16
HIGHEST = lax.Precision.HIGHEST

D_MODEL = 1024
A_HEADS = 8
A_KV_HEADS = 2
A_GROUP = A_HEADS // A_KV_HEADS
A_HEAD_DIM = 64
A_WIDTH = A_HEADS * A_HEAD_DIM
A_KV_WIDTH = A_KV_HEADS * A_HEAD_DIM
WINDOW = 128
ATT_SCALE = A_HEAD_DIM ** -0.5
N_BUCKETS = 32
MAX_DISTANCE = WINDOW
M_HEADS = 4
M_HEAD_DIM = 128
M_WIDTH = M_HEADS * M_HEAD_DIM
CONV_W = 4
M_CHUNK = 64
N_KEYS = 128
P_HEADS = 8
P_TOPK = 16
P_KEY_DIM = 256
P_HALF = P_KEY_DIM // 2
EPS = 1e-6
NEG_INF = -1e30

LANES = 128
SUBLANES = 8
SAMPLE_PAD = SUBLANES
VMEM_LIMIT = 48 * 1024 * 1024
PROMPT_GROUPS = 6

NT_DIMS = (((1,), (1,)), ((), ()))
TN_DIMS = (((0,), (0,)), ((), ()))


def _cparams(*sem):
    return pltpu.CompilerParams(dimension_semantics=sem, vmem_limit_bytes=VMEM_LIMIT)


def _bdot(a, b):
    return jnp.dot(a.astype(BF16), b.astype(BF16), preferred_element_type=F32)


def _bdot_nt(a, b):
    return lax.dot_general(a.astype(BF16), b.astype(BF16), NT_DIMS, preferred_element_type=F32)


def _sigmoid(x):
    return 1.0 / (1.0 + jnp.exp(-x))


def _log_sigmoid(x):
    return jnp.minimum(x, 0.0) - jnp.log1p(jnp.exp(-jnp.abs(x)))


def _gelu_tanh(x):
    c = math.sqrt(2.0 / math.pi)
    return x * (0.5 * (1.0 + jnp.tanh(c * (x + 0.044715 * (x * x * x)))))


def _ada_kernel(c_ref, w_ref, b_ref, o_ref):
    c = c_ref[...]
    s = c * _sigmoid(c)
    o_ref[...] = jnp.dot(s, w_ref[...], precision=HIGHEST, preferred_element_type=F32) + b_ref[...]


def _ada_call(c_all, w_ada, b_ada):
    depth, d, n6 = w_ada.shape
    rows = c_all.shape[0]
    bn = 1024
    return pl.pallas_call(
        _ada_kernel,
        grid=(depth, n6 // bn),
        in_specs=[
            pl.BlockSpec((rows, d), lambda l, j: (0, 0)),
            pl.BlockSpec((None, d, bn), lambda l, j: (l, 0, j)),
            pl.BlockSpec((None, 1, bn), lambda l, j: (l, 0, j)),
        ],
        out_specs=pl.BlockSpec((None, rows, bn), lambda l, j: (l, 0, j)),
        out_shape=jax.ShapeDtypeStruct((depth, rows, n6), F32),
        compiler_params=_cparams("parallel", "parallel"),
        name="ada_mod",
    )(c_all, w_ada, b_ada.reshape(depth, 1, n6))


def _mod_spec(per_token, tm, tokens_per_req):
    if per_token:
        return pl.BlockSpec((tm, D_MODEL), lambda i: (i, 0))
    tiles = tokens_per_req // tm
    return pl.BlockSpec((None, 1, D_MODEL), lambda i: (i // tiles, 0, 0))


def _in_kernel(x_ref, sc_ref, sh_ref, g_ref, wa_ref, wqk_ref, wv_ref, wo_ref, wg_ref, gb_ref,
               qkv_ref, qkm_ref, v_ref, o_ref, gc_ref):
    x = x_ref[...]
    y = x * lax.rsqrt(jnp.mean(x * x, axis=-1, keepdims=True) + EPS) * g_ref[...]
    h = (y * (1.0 + sc_ref[...]) + sh_ref[...]).astype(BF16)
    qkv_ref[...] = jnp.dot(h, wa_ref[...], preferred_element_type=F32)
    qkm_ref[...] = jnp.dot(h, wqk_ref[...], preferred_element_type=F32)
    v_ref[...] = jnp.dot(h, wv_ref[...], preferred_element_type=F32)
    o_ref[...] = jnp.dot(h, wo_ref[...], preferred_element_type=F32)
    g = jnp.dot(h, wg_ref[...], preferred_element_type=F32) + gb_ref[...]
    lane = lax.broadcasted_iota(jnp.int32, g.shape, 1)
    gc_ref[...] = jnp.where(lane < M_HEADS, g, jnp.where(lane < 2 * M_HEADS, _log_sigmoid(g), 0.0))


def _in_call(x, sc, sh, gnorm, wa, wqk, wv, wo, wg, gb, per_token, tokens_per_req):
    n = x.shape[0]
    tm = min(512, n if per_token else tokens_per_req)
    mod = _mod_spec(per_token, tm, tokens_per_req)
    full = lambda shape: pl.BlockSpec(shape, lambda i: (0,) * len(shape))
    row = lambda w: pl.BlockSpec((tm, w), lambda i: (i, 0))
    return pl.pallas_call(
        _in_kernel,
        grid=(n // tm,),
        in_specs=[row(D_MODEL), mod, mod, full((1, D_MODEL)), full(wa.shape), full(wqk.shape),
                  full(wv.shape), full(wo.shape), full(wg.shape), full((1, LANES))],
        out_specs=[row(wa.shape[1]), row(wqk.shape[1]), row(wv.shape[1]), row(wo.shape[1]), row(LANES)],
        out_shape=[jax.ShapeDtypeStruct((n, w), F32)
                   for w in (wa.shape[1], wqk.shape[1], wv.shape[1], wo.shape[1], LANES)],
        compiler_params=_cparams("parallel"),
        name="in_proj",
    )(x, sc, sh, gnorm, wa, wqk, wv, wo, wg, gb)


def _t5_bucket_np(dist):
    n = np.maximum(dist, 0)
    max_exact = N_BUCKETS // 2
    nf = np.maximum(n, 1).astype(np.float64)
    large = max_exact + (np.log(nf / max_exact) / math.log(MAX_DISTANCE / max_exact)
                         * (N_BUCKETS - max_exact)).astype(np.int32)
    return np.where(n < max_exact, n, np.minimum(large, N_BUCKETS - 1)).astype(np.int32)


def _bias_kernel(bucket_ref, rel_ref, o_ref):
    bucket = bucket_ref[...]
    for h in range(A_HEADS):
        acc = jnp.zeros(bucket.shape, F32)
        for b in range(N_BUCKETS):
            acc = jnp.where(bucket == b, rel_ref[b, h], acc)
        o_ref[h] = acc


def _bias_call(rel_bias, dist):
    bucket = jnp.asarray(_t5_bucket_np(dist))
    nq, nk = dist.shape
    return pl.pallas_call(
        _bias_kernel,
        in_specs=[pl.BlockSpec((nq, nk), lambda: (0, 0)),
                  pl.BlockSpec(memory_space=pltpu.SMEM)],
        out_specs=pl.BlockSpec((A_HEADS, nq, nk), lambda: (0, 0, 0)),
        out_shape=jax.ShapeDtypeStruct((A_HEADS, nq, nk), F32),
        name="t5_bias",
    )(bucket, rel_bias)


def _attn_p_kernel(q_ref, kp_ref, kc_ref, vp_ref, vc_ref, bias_ref, sink_ref, o_ref):
    i = pl.program_id(1)
    qi = lax.broadcasted_iota(jnp.int32, (WINDOW, WINDOW), 0)
    kj = lax.broadcasted_iota(jnp.int32, (WINDOW, WINDOW), 1)
    valid_prev = jnp.logical_and(kj > qi, i > 0)
    valid_cur = kj <= qi
    q = q_ref[...]
    for h in range(A_HEADS):
        kv = h // A_GROUP
        qh = q[:, h * A_HEAD_DIM:(h + 1) * A_HEAD_DIM]
        sl = slice(kv * A_HEAD_DIM, (kv + 1) * A_HEAD_DIM)
        bias = bias_ref[h]
        sp = _bdot_nt(qh, kp_ref[:, sl]) * ATT_SCALE + bias[:, :WINDOW]
        sc = _bdot_nt(qh, kc_ref[:, sl]) * ATT_SCALE + bias[:, WINDOW:]
        sp = jnp.where(valid_prev, sp, NEG_INF)
        sc = jnp.where(valid_cur, sc, NEG_INF)
        sink = sink_ref[0, h]
        mx = jnp.maximum(jnp.maximum(jnp.max(sp, axis=-1, keepdims=True),
                                     jnp.max(sc, axis=-1, keepdims=True)), sink)
        ep = jnp.exp(sp - mx)
        ec = jnp.exp(sc - mx)
        den = (jnp.sum(ep, axis=-1, keepdims=True) + jnp.sum(ec, axis=-1, keepdims=True)
               + jnp.exp(sink - mx))
        o = _bdot(ep / den, vp_ref[:, sl]) + _bdot(ec / den, vc_ref[:, sl])
        o_ref[:, h * A_HEAD_DIM:(h + 1) * A_HEAD_DIM] = o


def _attn_p_call(qkv, bias, sinks, batch, seq):
    nb = seq // WINDOW
    n = batch * seq
    kcol = A_WIDTH // A_KV_WIDTH
    vcol = kcol + 1
    cur = lambda col: pl.BlockSpec((WINDOW, A_KV_WIDTH), lambda b, i: (b * nb + i, col))
    prev = lambda col: pl.BlockSpec((WINDOW, A_KV_WIDTH),
                                    lambda b, i: (b * nb + jnp.maximum(i - 1, 0), col))
    return pl.pallas_call(
        _attn_p_kernel,
        grid=(batch, nb),
        in_specs=[pl.BlockSpec((WINDOW, A_WIDTH), lambda b, i: (b * nb + i, 0)),
                  prev(kcol), cur(kcol), prev(vcol), cur(vcol),
                  pl.BlockSpec((A_HEADS, WINDOW, 2 * WINDOW), lambda b, i: (0, 0, 0)),
                  pl.BlockSpec(memory_space=pltpu.SMEM)],
        out_specs=pl.BlockSpec((WINDOW, A_WIDTH), lambda b, i: (b * nb + i, 0)),
        out_shape=jax.ShapeDtypeStruct((n, A_WIDTH), F32),
        compiler_params=_cparams("parallel", "parallel"),
        name="swa_prompt",
    )(qkv, qkv, qkv, qkv, qkv, bias, sinks.reshape(1, A_HEADS))


def _attn_s_kernel(n_new, qkv_ref, ck_ref, cv_ref, bc_ref, bn_ref, sink_ref,
                   o_ref, nk_ref, nv_ref, kk_s, vv_s):
    qkv = qkv_ref[...]
    knew = qkv[:, A_WIDTH:A_WIDTH + A_KV_WIDTH]
    vnew = qkv[:, A_WIDTH + A_KV_WIDTH:A_WIDTH + 2 * A_KV_WIDTH]
    ck = ck_ref[...]
    cv = cv_ref[...]
    rows = A_GROUP * SAMPLE_PAD
    qi = lax.broadcasted_iota(jnp.int32, (rows, WINDOW), 0) % SAMPLE_PAD
    kj = lax.broadcasted_iota(jnp.int32, (rows, WINDOW), 1)
    valid_c = kj > qi
    rcol = lax.broadcasted_iota(jnp.int32, (rows, 1), 0)
    qcol = rcol % SAMPLE_PAD
    for kv in range(A_KV_HEADS):
        heads = range(kv * A_GROUP, (kv + 1) * A_GROUP)
        sl = slice(kv * A_HEAD_DIM, (kv + 1) * A_HEAD_DIM)
        qs = jnp.concatenate([qkv[:, h * A_HEAD_DIM:(h + 1) * A_HEAD_DIM] for h in heads], axis=0)
        bias_c = jnp.concatenate([bc_ref[h] for h in heads], axis=0)
        bias_n = jnp.concatenate([bn_ref[h] for h in heads], axis=0)
        sink = jnp.zeros((rows, 1), F32)
        for g, h in enumerate(heads):
            sink = jnp.where(rcol // SAMPLE_PAD == g, sink_ref[0, h], sink)
        s_c = _bdot_nt(qs, ck[:, sl]) * ATT_SCALE + bias_c
        s_c = jnp.where(valid_c, s_c, NEG_INF)
        s_n = []
        for j in range(n_new):
            sj = jnp.sum(qs * knew[j:j + 1, sl], axis=-1, keepdims=True) * ATT_SCALE + bias_n[:, j:j + 1]
            s_n.append(jnp.where(qcol >= j, sj, NEG_INF))
        mx = jnp.maximum(jnp.max(s_c, axis=-1, keepdims=True), sink)
        for sj in s_n:
            mx = jnp.maximum(mx, sj)
        e_c = jnp.exp(s_c - mx)
        den = jnp.sum(e_c, axis=-1, keepdims=True) + jnp.exp(sink - mx)
        o = _bdot(e_c, cv[:, sl])
        for j, sj in enumerate(s_n):
            ej = jnp.exp(sj - mx)
            den = den + ej
            o = o + ej * vnew[j:j + 1, sl]
        o = o / den
        for g, h in enumerate(heads):
            o_ref[:, h * A_HEAD_DIM:(h + 1) * A_HEAD_DIM] = o[g * SAMPLE_PAD:(g + 1) * SAMPLE_PAD, :]
    kk_s[0:WINDOW, :] = ck
    kk_s[WINDOW:WINDOW + SAMPLE_PAD, :] = knew
    vv_s[0:WINDOW, :] = cv
    vv_s[WINDOW:WINDOW + SAMPLE_PAD, :] = vnew
    nk_ref[...] = kk_s[n_new:n_new + WINDOW, :]
    nv_ref[...] = vv_s[n_new:n_new + WINDOW, :]


def _attn_s_call(qkv, ck, cv, bias_c, bias_n, sinks, n_new):
    nreq = ck.shape[0]
    wq = qkv.shape[1]
    full3 = lambda shape: pl.BlockSpec(shape, lambda b: (0, 0, 0))
    cache = pl.BlockSpec((None, WINDOW, A_KV_WIDTH), lambda b: (b, 0, 0))
    return pl.pallas_call(
        functools.partial(_attn_s_kernel, n_new),
        grid=(nreq,),
        in_specs=[pl.BlockSpec((SAMPLE_PAD, wq), lambda b: (b, 0)), cache, cache,
                  full3(bias_c.shape), full3(bias_n.shape),
                  pl.BlockSpec(memory_space=pltpu.SMEM)],
        out_specs=[pl.BlockSpec((SAMPLE_PAD, A_WIDTH), lambda b: (b, 0)), cache, cache],
        out_shape=[jax.ShapeDtypeStruct((nreq * SAMPLE_PAD, A_WIDTH), F32),
                   jax.ShapeDtypeStruct(ck.shape, F32), jax.ShapeDtypeStruct(cv.shape, F32)],
        scratch_shapes=[pltpu.VMEM((WINDOW + SAMPLE_PAD, A_KV_WIDTH), F32),
                        pltpu.VMEM((WINDOW + SAMPLE_PAD, A_KV_WIDTH), F32)],
        compiler_params=_cparams("parallel"),
        name="swa_sample",
    )(qkv, ck, cv, bias_c, bias_n, sinks.reshape(1, A_HEADS))


def _mlstm_kernel(chunk, t_valid, qk_ref, v_ref, og_ref, gc_ref, cw_ref, cb_ref, mn_ref,
                  conv0_ref, c0_ref, n0_ref, m0_ref,
                  out_ref, cout_ref, nout_ref, mout_ref,
                  xp_s, c_s, n_s, m_s):
    step = pl.program_id(1)
    halo = SUBLANES

    @pl.when(step == 0)
    def _():
        xp_s[0:halo, :] = conv0_ref[...]
        c_s[...] = c0_ref[...]
        n_s[...] = n0_ref[...]
        m_s[...] = m0_ref[...]

    xp_s[halo:halo + chunk, :] = qk_ref[...]
    cw = cw_ref[...]
    y = cb_ref[...]
    for i in range(CONV_W):
        off = halo - (CONV_W - 1) + i
        y = y + xp_s[off:off + chunk, :] * cw[i:i + 1, :]
    xp_s[0:halo, :] = xp_s[chunk:chunk + halo, :]
    y = y * _sigmoid(y)
    q_all = y[:, :M_WIDTH]
    k_all = y[:, M_WIDTH:] * (M_HEAD_DIM ** -0.5)

    g = gc_ref[...]
    if t_valid < chunk:
        row = lax.broadcasted_iota(jnp.int32, g.shape, 0)
        lane = lax.broadcasted_iota(jnp.int32, g.shape, 1)
        g = jnp.where(row < t_valid, g, jnp.where(lane < M_HEADS, NEG_INF, 0.0))
    tr = lax.broadcasted_iota(jnp.int32, (chunk, chunk), 0)
    tc = lax.broadcasted_iota(jnp.int32, (chunk, chunk), 1)
    causal = tr >= tc
    tri = causal.astype(F32)
    bcol = jnp.dot(tri, g, precision=HIGHEST, preferred_element_type=F32)
    er = lax.broadcasted_iota(jnp.int32, (SUBLANES, LANES), 0)
    ec = lax.broadcasted_iota(jnp.int32, (SUBLANES, LANES), 1)
    eye = (er == ec).astype(F32)
    g_rows = lax.dot_general(eye, g, NT_DIMS, precision=HIGHEST, preferred_element_type=F32)
    b_rows = lax.dot_general(eye, bcol, NT_DIMS, precision=HIGHEST, preferred_element_type=F32)

    for h in range(M_HEADS):
        hs = slice(h * M_HEAD_DIM, (h + 1) * M_HEAD_DIM)
        b_c = bcol[:, M_HEADS + h:M_HEADS + h + 1]
        ig_c = g[:, h:h + 1]
        b_r = b_rows[M_HEADS + h:M_HEADS + h + 1, :]
        ig_r = g_rows[h:h + 1, :]
        m_prev = m_s[h:h + 1, 0:1]
        logw = jnp.where(causal, b_c - b_r + ig_r, -jnp.inf)
        inter = b_c + m_prev
        m_t = jnp.maximum(inter, jnp.max(logw, axis=-1, keepdims=True))
        w = jnp.exp(logw - m_t)
        a = jnp.exp(inter - m_t)
        q = q_all[:, hs]
        k = k_all[:, hs]
        v = v_ref[:, hs]
        cmat = c_s[h]
        nvec = n_s[h:h + 1, :]
        wqk = w * _bdot_nt(q, k)
        num = _bdot(wqk, v) + a * _bdot_nt(q, cmat)
        den = jnp.sum(wqk, axis=-1, keepdims=True) + a * jnp.sum(q * nvec, axis=-1, keepdims=True)
        hh = num / jnp.maximum(jnp.abs(den), jnp.exp(-m_t))
        m_new = m_t[chunk - 1:chunk, :]
        b_last = b_c[chunk - 1:chunk, :]
        wl = jnp.exp(b_last - b_c + ig_c - m_new)
        al = jnp.exp(b_last + m_prev - m_new)
        c_s[h] = al * cmat + lax.dot_general((v * wl).astype(BF16), k.astype(BF16), TN_DIMS,
                                             preferred_element_type=F32)
        n_s[h:h + 1, :] = al * nvec + jnp.sum(wl * k, axis=0, keepdims=True)
        m_s[h:h + 1, :] = jnp.broadcast_to(m_new, (1, LANES))
        hn = hh * lax.rsqrt(jnp.mean(hh * hh, axis=-1, keepdims=True) + EPS) * mn_ref[:, hs]
        out_ref[:, hs] = _sigmoid(og_ref[:, hs]) * hn

    @pl.when(step == pl.num_programs(1) - 1)
    def _():
        cout_ref[...] = c_s[...]
        nout_ref[...] = n_s[...]
        mout_ref[...] = m_s[...]


def _mlstm_call(qk, v, og, gc, conv_w, conv_b, m_norm, conv0, c0, n0, m0, batch, seq, chunk, t_valid):
    nc = seq // chunk
    n = batch * seq
    row = lambda w: pl.BlockSpec((chunk, w), lambda b, c: (b * nc + c, 0))
    full2 = lambda shape: pl.BlockSpec(shape, lambda b, c: (0, 0))
    per_b = lambda shape: pl.BlockSpec((None,) + shape, lambda b, c: (b,) + (0,) * len(shape))
    dh = M_HEAD_DIM
    return pl.pallas_call(
        functools.partial(_mlstm_kernel, chunk, t_valid),
        grid=(batch, nc),
        in_specs=[row(2 * M_WIDTH), row(M_WIDTH), row(M_WIDTH), row(LANES),
                  full2((CONV_W, 2 * M_WIDTH)), full2((1, 2 * M_WIDTH)), full2((1, M_WIDTH)),
                  per_b((SUBLANES, 2 * M_WIDTH)), per_b((M_HEADS, dh, dh)), per_b((M_HEADS, dh)),
                  per_b((SUBLANES, LANES))],
        out_specs=[row(M_WIDTH), per_b((M_HEADS, dh, dh)), per_b((M_HEADS, dh)), per_b((SUBLANES, LANES))],
        out_shape=[jax.ShapeDtypeStruct((n, M_WIDTH), F32),
                   jax.ShapeDtypeStruct((batch, M_HEADS, dh, dh), F32),
                   jax.ShapeDtypeStruct((batch, M_HEADS, dh), F32),
                   jax.ShapeDtypeStruct((batch, SUBLANES, LANES), F32)],
        scratch_shapes=[pltpu.VMEM((SUBLANES + chunk, 2 * M_WIDTH), F32),
                        pltpu.VMEM((M_HEADS, dh, dh), F32),
                        pltpu.VMEM((M_HEADS, dh), F32),
                        pltpu.VMEM((SUBLANES, LANES), F32)],
        compiler_params=_cparams("parallel", "arbitrary"),
        name="mlstm",
    )(qk, v, og, gc, conv_w, conv_b, m_norm, conv0, c0, n0, m0)


def _out_kernel(att_ref, mo_ref, x_ref, g1_ref, sc_ref, sh_ref, gn_ref, wa_ref, wm_ref, wq_ref,
                xo_ref, h2_ref, qp_ref):
    mix = (jnp.dot(att_ref[...].astype(BF16), wa_ref[...], preferred_element_type=F32)
           + jnp.dot(mo_ref[...].astype(BF16), wm_ref[...], preferred_element_type=F32))
    x = x_ref[...] + g1_ref[...] * mix
    xo_ref[...] = x
    y = x * lax.rsqrt(jnp.mean(x * x, axis=-1, keepdims=True) + EPS) * gn_ref[...]
    h2 = y * (1.0 + sc_ref[...]) + sh_ref[...]
    qp_ref[...] = jnp.dot(h2.astype(BF16), wq_ref[...], preferred_element_type=F32)
    h2_ref[...] = _pack_words(h2)


def _out_call(att, mo, x, g1, sc, sh, gnorm, wa, wm, wq, per_token, tokens_per_req):
    n = x.shape[0]
    tm = min(256, n if per_token else tokens_per_req)
    mod = _mod_spec(per_token, tm, tokens_per_req)
    full = lambda shape: pl.BlockSpec(shape, lambda i: (0,) * len(shape))
    row = lambda w: pl.BlockSpec((tm, w), lambda i: (i, 0))
    nq = wq.shape[1]
    return pl.pallas_call(
        _out_kernel,
        grid=(n // tm,),
        in_specs=[row(A_WIDTH), row(M_WIDTH), row(D_MODEL), mod, mod, mod, full((1, D_MODEL)),
                  full(wa.shape), full(wm.shape), full(wq.shape)],
        out_specs=[row(D_MODEL), row(D_MODEL // 2), row(nq)],
        out_shape=[jax.ShapeDtypeStruct((n, D_MODEL), F32), jax.ShapeDtypeStruct((n, D_MODEL // 2), jnp.int32),
                   jax.ShapeDtypeStruct((n, nq), F32)],
        compiler_params=_cparams("parallel"),
        name="out_proj",
    )(att, mo, x, g1, sc, sh, gnorm, wa, wm, wq)


def _pk_cells():
    return [(a, b) for a in range(P_TOPK) for b in range(P_TOPK) if (a + 1) * (b + 1) <= P_TOPK]


PK_CELL_ROWS = 64


def _pk_expand_mats():
    cells = _pk_cells()
    e0 = np.zeros((PK_CELL_ROWS, LANES), np.float32)
    e1 = np.zeros((PK_CELL_ROWS, LANES), np.float32)
    for j, (a, b) in enumerate(cells):
        e0[j, a] = 1.0
        e1[j, b] = 1.0
    return e0, e1, len(cells)


def _top_rows(s, rowf, rounds):
    n_rows = s.shape[0]
    vals, idxs = [], []
    for _ in range(rounds):
        m = jnp.max(s, axis=0, keepdims=True)
        i = jnp.min(jnp.where(s == m, rowf, float(n_rows)), axis=0, keepdims=True)
        vals.append(m)
        idxs.append(i)
        s = jnp.where(rowf == i, -jnp.inf, s)
    return jnp.concatenate(vals, axis=0), jnp.concatenate(idxs, axis=0)


def _select_kernel(n_cells, qp_ref, keys_ref, e0_ref, e1_ref, idx_ref, gw_ref, idx_s, gw_s):
    tm = qp_ref.shape[0]
    keyf = lax.broadcasted_iota(jnp.int32, (N_KEYS, tm), 0).astype(F32)
    cellf = lax.broadcasted_iota(jnp.int32, (PK_CELL_ROWS, tm), 0).astype(F32)
    e0 = e0_ref[...]
    e1 = e1_ref[...]
    pad = jnp.zeros((LANES - P_TOPK, tm), F32)

    def head(h, carry):
        h = jnp.asarray(h, jnp.int32)
        sub = []
        for c in range(2):
            col = pl.multiple_of((h * 2 + c) * P_HALF, P_HALF)
            s = _bdot_nt(keys_ref[h, c], qp_ref[:, pl.ds(col, P_HALF)])
            sub.append(_top_rows(s, keyf, P_TOPK))
        (v0, i0), (v1, i1) = sub
        expand = lambda e, x: jnp.dot(e, jnp.concatenate([x, pad], axis=0), precision=HIGHEST,
                                      preferred_element_type=F32)
        cand = expand(e0, v0) + expand(e1, v1)
        cidx = expand(e0, i0 * float(N_KEYS)) + expand(e1, i1)
        cand = jnp.where(cellf < n_cells, cand, -jnp.inf)
        best, eidx = [], []
        for _ in range(P_TOPK):
            m = jnp.max(cand, axis=0, keepdims=True)
            j = jnp.min(jnp.where(cand == m, cellf, float(PK_CELL_ROWS)), axis=0, keepdims=True)
            hit = cellf == j
            eidx.append(jnp.max(jnp.where(hit, cidx, -1.0), axis=0, keepdims=True))
            best.append(m)
            cand = jnp.where(hit, -jnp.inf, cand)
        best = jnp.concatenate(best, axis=0)
        e = jnp.exp(best - best[0:1, :])
        row0 = pl.multiple_of(h * P_TOPK, P_TOPK)
        gw_s[pl.ds(row0, P_TOPK), :] = e / jnp.sum(e, axis=0, keepdims=True)
        idx_s[pl.ds(row0, P_TOPK), :] = jnp.concatenate(eidx, axis=0)
        return carry

    lax.fori_loop(0, P_HEADS, head, 0)
    idx_ref[...] = idx_s[...].T.astype(jnp.int32)
    gw_ref[...] = gw_s[...].T


def _select_call(qp, keys_bf16):
    n = qp.shape[0]
    tm = min(LANES, n)
    e0, e1, n_cells = _pk_expand_mats()
    full = lambda shape: pl.BlockSpec(shape, lambda i: (0,) * len(shape))
    return pl.pallas_call(
        functools.partial(_select_kernel, n_cells),
        grid=(n // tm,),
        in_specs=[pl.BlockSpec((tm, qp.shape[1]), lambda i: (i, 0)), full(keys_bf16.shape),
                  full((PK_CELL_ROWS, LANES)), full((PK_CELL_ROWS, LANES))],
        out_specs=[pl.BlockSpec((tm, LANES), lambda i: (i, 0)), pl.BlockSpec((tm, LANES), lambda i: (i, 0))],
        out_shape=[jax.ShapeDtypeStruct((n, LANES), jnp.int32), jax.ShapeDtypeStruct((n, LANES), F32)],
        scratch_shapes=[pltpu.VMEM((P_HEADS * P_TOPK, tm), F32), pltpu.VMEM((P_HEADS * P_TOPK, tm), F32)],
        compiler_params=_cparams("parallel"),
        name="peer_select",
    )(qp, keys_bf16, jnp.asarray(e0), jnp.asarray(e1))


N_SEL = P_HEADS * P_TOPK

SC_CORES = 2
SC_SUBCORES = 16
SC_LANES = 16
SC_TOK_BLOCK = 8
SC_ROWS = 16
SC_ROW_BLOCK = 16
SC_ACC_CHAINS = 4
SC_ACC_ROWS = 16
SC_NBUF = 4
N_WCOL = D_MODEL // (2 * SC_LANES)


def _pack_words(x):
    bits = pltpu.bitcast(x.astype(BF16).astype(F32), jnp.uint32)
    half = x.shape[1] // 2
    lo = lax.shift_right_logical(bits[:, :half], jnp.uint32(16))
    hi = bits[:, half:] & jnp.uint32(0xFFFF0000)
    return pltpu.bitcast(lo | hi, jnp.int32)


def _pack_kernel(u_ref, v_ref, o_ref):
    half = o_ref.shape[1] // 2
    o_ref[:, :half] = _pack_words(u_ref[...])
    o_ref[:, half:] = _pack_words(v_ref[...])


def _pack_tables(u, v):
    e, dcol = u.shape
    tm = min(1024, e)
    return pl.pallas_call(
        _pack_kernel,
        grid=(e // tm,),
        in_specs=[pl.BlockSpec((tm, dcol), lambda i: (i, 0)), pl.BlockSpec((tm, dcol), lambda i: (i, 0))],
        out_specs=pl.BlockSpec((tm, dcol), lambda i: (i, 0)),
        out_shape=jax.ShapeDtypeStruct((e, dcol), jnp.int32),
        compiler_params=_cparams("parallel"),
        name="pack_tables",
    )(u, v)


def _sc_gelu(x):
    z = math.sqrt(2.0 / math.pi) * (x + 0.044715 * (x * x * x))
    t = 1.0 - 2.0 / (jnp.exp(2.0 * z) + 1.0)
    return x * (0.5 * (1.0 + t))


def _sc_expert_body(tokens_per_worker, idx_hbm, gw_hbm, h2_hbm, uv_hbm, y_hbm,
                    idx_v, gw_v, x_v, o_v, buf, coef_v, tr_v, sem, in_sem, out_sem):
    wid = lax.axis_index("s") * SC_CORES + lax.axis_index("c")
    base = wid * tokens_per_worker
    lane = lax.iota(jnp.int32, SC_LANES)
    zero = jnp.zeros((SC_LANES,), F32)
    half_w = D_MODEL // 2

    n_steps = N_SEL // SC_ROWS
    assert n_steps % SC_NBUF == 0
    assert SC_ROWS == SC_LANES == SC_ROW_BLOCK == SC_ACC_ROWS

    def gather(p, tt, i):
        slot = i % SC_NBUF
        return pltpu.make_async_copy(uv_hbm.at[idx_v.at[p, tt, pl.ds(i * SC_ROWS, SC_ROWS)]], buf.at[slot],
                                     sem.at[slot])

    def unpack(w):
        lo = lax.bitcast_convert_type(lax.shift_left(w, jnp.full((SC_LANES,), 16, jnp.int32)), F32)
        hi = lax.bitcast_convert_type(w & jnp.full((SC_LANES,), -65536, jnp.int32), F32)
        return lo, hi

    def packed(w):
        return plsc.bitcast(w, BF16)

    def unpack_sum(s):
        return unpack(plsc.bitcast(s, jnp.int32))

    def act_chunk(p, tt, j, slot):
        def col(c, accs):
            w0 = pl.multiple_of(c * (2 * SC_LANES), 2 * SC_LANES)
            xa = packed(x_v[p, tt, pl.ds(w0, SC_LANES)])
            xb = packed(x_v[p, tt, pl.ds(w0 + SC_LANES, SC_LANES)])
            out = []
            for r, a in enumerate(accs):
                ua = packed(buf[slot, r, pl.ds(w0, SC_LANES)])
                ub = packed(buf[slot, r, pl.ds(w0 + SC_LANES, SC_LANES)])
                lo, hi = unpack_sum(ua * xa + ub * xb)
                out.append(a + lo + hi)
            return tuple(out)

        accs = lax.fori_loop(0, N_WCOL // 2, col, (zero,) * SC_ROWS)
        for r in range(SC_ROWS):
            tr_v[pl.ds(r * SC_LANES, SC_LANES)] = accs[r]
        tot = zero
        for jj in range(SC_LANES):
            tot = tot + plsc.load_gather(tr_v, [lane * SC_LANES + jj])
        k0 = j * SC_ROWS
        coef_v[pl.ds(k0, SC_LANES)] = gw_v[p, tt, pl.ds(k0, SC_LANES)] * _sc_gelu(tot)

    def acc_chunk(p, tt, j, slot, fresh):
        def tree_sum(parts):
            while len(parts) > 1:
                parts = [parts[i] + parts[i + 1] for i in range(0, len(parts), 2)]
            return parts[0]

        splat = []
        for r in range(SC_ROWS):
            c16 = plsc.load_gather(coef_v, [jnp.full((SC_LANES,), j * SC_ROWS + r, jnp.int32)])
            splat.append(plsc.pack(c16, c16, format=plsc.PackFormat.INTERLEAVED))

        @plsc.parallel_loop(0, N_WCOL)
        def _(c):
            woff = pl.multiple_of(c * SC_LANES, SC_LANES)
            voff = pl.multiple_of(c * SC_LANES + half_w, SC_LANES)
            hoff = pl.multiple_of(c * SC_LANES + D_MODEL // 2, SC_LANES)
            pa, pb = [], []
            for n in range(0, SC_ROWS, 2):
                s = (splat[n] * packed(buf[slot, n, pl.ds(voff, SC_LANES)])
                     + splat[n + 1] * packed(buf[slot, n + 1, pl.ds(voff, SC_LANES)]))
                lo, hi = unpack_sum(s)
                if n // 2 < SC_ACC_CHAINS:
                    pa.append(lo)
                    pb.append(hi)
                else:
                    pa[(n // 2) % SC_ACC_CHAINS] = pa[(n // 2) % SC_ACC_CHAINS] + lo
                    pb[(n // 2) % SC_ACC_CHAINS] = pb[(n // 2) % SC_ACC_CHAINS] + hi
            sa, sb = tree_sum(pa), tree_sum(pb)
            if not fresh:
                sa = sa + o_v[p, tt, pl.ds(woff, SC_LANES)]
                sb = sb + o_v[p, tt, pl.ds(hoff, SC_LANES)]
            o_v[p, tt, pl.ds(woff, SC_LANES)] = sa
            o_v[p, tt, pl.ds(hoff, SC_LANES)] = sb

    n_blocks = tokens_per_worker // SC_TOK_BLOCK

    def block_start(blk):
        return pl.multiple_of(base + blk * SC_TOK_BLOCK, SC_TOK_BLOCK)

    def in_copies(blk, p):
        rows = pl.ds(block_start(blk), SC_TOK_BLOCK)
        return [pltpu.make_async_copy(idx_hbm.at[rows], idx_v.at[p], in_sem.at[p]),
                pltpu.make_async_copy(gw_hbm.at[rows], gw_v.at[p], in_sem.at[p]),
                pltpu.make_async_copy(h2_hbm.at[rows], x_v.at[p], in_sem.at[p])]

    def out_copy(blk, p):
        return pltpu.make_async_copy(o_v.at[p], y_hbm.at[pl.ds(block_start(blk), SC_TOK_BLOCK)], out_sem.at[p])

    for cp in in_copies(0, 0):
        cp.start()

    @pl.loop(0, n_blocks)
    def _(blk):
        p = lax.rem(blk, 2)
        for cp in in_copies(blk, p):
            cp.wait()

        @pl.when(blk + 1 < n_blocks)
        def _():
            for cp in in_copies(blk + 1, 1 - p):
                cp.start()

        @pl.when(blk >= 2)
        def _():
            out_copy(blk - 2, p).wait()

        ahead = SC_NBUF - 1
        for i in range(ahead):
            gather(p, 0, i).start()

        @pl.loop(0, SC_TOK_BLOCK)
        def _(tt):
            for i in range(n_steps):
                if i + ahead < n_steps:
                    gather(p, tt, i + ahead).start()
                else:
                    @pl.when(tt + 1 < SC_TOK_BLOCK)
                    def _():
                        gather(p, tt + 1, i + ahead - n_steps).start()
                gather(p, tt, i).wait()
                act_chunk(p, tt, i, i % SC_NBUF)
                acc_chunk(p, tt, i, i % SC_NBUF, i == 0)

        out_copy(blk, p).start()

    for blk in (n_blocks - 2, n_blocks - 1):
        out_copy(blk, blk % 2).wait()


def _sc_expert_call(idx, gw, xw, uv):
    n = idx.shape[0]
    workers = SC_CORES * SC_SUBCORES
    assert n % (workers * SC_TOK_BLOCK) == 0 and n // (workers * SC_TOK_BLOCK) >= 2
    mesh = plsc.VectorSubcoreMesh(core_axis_name="c", subcore_axis_name="s")
    return pl.kernel(
        functools.partial(_sc_expert_body, n // workers),
        out_type=jax.ShapeDtypeStruct((n, D_MODEL), F32),
        mesh=mesh,
        scratch_types=[pltpu.VMEM((2, SC_TOK_BLOCK, N_SEL), jnp.int32),
                       pltpu.VMEM((2, SC_TOK_BLOCK, N_SEL), F32),
                       pltpu.VMEM((2, SC_TOK_BLOCK, D_MODEL // 2), jnp.int32),
                       pltpu.VMEM((2, SC_TOK_BLOCK, D_MODEL), F32),
                       pltpu.VMEM((SC_NBUF, SC_ROWS, D_MODEL), jnp.int32),
                       pltpu.VMEM((N_SEL,), F32),
                       pltpu.VMEM((SC_LANES * SC_LANES,), F32),
                       pltpu.SemaphoreType.DMA((SC_NBUF,)), pltpu.SemaphoreType.DMA((2,)),
                       pltpu.SemaphoreType.DMA((2,))],
        compiler_params=pltpu.CompilerParams(needs_layout_passes=False),
        name="peer_experts_sc",
    )(idx, gw, xw, uv)


def _resid_kernel(x_ref, y_ref, g_ref, o_ref):
    o_ref[...] = x_ref[...] + g_ref[...] * y_ref[...]


def _resid_call(x, y, g2, per_token, tokens_per_req):
    n = x.shape[0]
    tm = min(512, n if per_token else tokens_per_req)
    row = pl.BlockSpec((tm, D_MODEL), lambda i: (i, 0))
    return pl.pallas_call(
        _resid_kernel,
        grid=(n // tm,),
        in_specs=[row, row, _mod_spec(per_token, tm, tokens_per_req)],
        out_specs=row,
        out_shape=jax.ShapeDtypeStruct((n, D_MODEL), F32),
        compiler_params=_cparams("parallel"),
        name="peer_residual",
    )(x, y, g2)


def _final_kernel(x_ref, g_ref, o_ref):
    x = x_ref[...]
    o_ref[...] = x * lax.rsqrt(jnp.mean(x * x, axis=-1, keepdims=True) + EPS) * g_ref[...]


def _final_call(x, g):
    n = x.shape[0]
    tm = min(512, n)
    return pl.pallas_call(
        _final_kernel,
        grid=(n // tm,),
        in_specs=[pl.BlockSpec((tm, D_MODEL), lambda i: (i, 0)), pl.BlockSpec((1, D_MODEL), lambda i: (0, 0))],
        out_specs=pl.BlockSpec((tm, D_MODEL), lambda i: (i, 0)),
        out_shape=jax.ShapeDtypeStruct((n, D_MODEL), F32),
        compiler_params=_cparams("parallel"),
        name="final_norm",
    )(x, g)


def _split_w_in(w_in_l, gate_b_l):
    cuts = np.cumsum([A_WIDTH + 2 * A_KV_WIDTH, 2 * M_WIDTH, M_WIDTH, M_WIDTH]).tolist()
    wa = w_in_l[:, :cuts[0]].astype(BF16)
    wqk = w_in_l[:, cuts[0]:cuts[1]].astype(BF16)
    wv = w_in_l[:, cuts[1]:cuts[2]].astype(BF16)
    wo = w_in_l[:, cuts[2]:cuts[3]].astype(BF16)
    ng = 2 * M_HEADS
    wg = jnp.pad(w_in_l[:, cuts[3]:], ((0, 0), (0, LANES - ng))).astype(BF16)
    gb = jnp.pad(gate_b_l.astype(F32), (0, LANES - ng)).reshape(1, LANES)
    return wa, wqk, wv, wo, wg, gb


def _layer(x, mods, per_token, batch, seq, t_valid, lw, bias_p, bias_c, bias_n, kv_cache, conv0, state,
           after=None):
    (norm_mix, norm_ffn, w_in, conv_w, conv_b, gate_b, sinks, m_norm, w_out, peer_query, peer_keys,
     peer_uv) = lw
    if after is not None:
        x, _ = lax.optimization_barrier((x, after))
    sh1, sc1, g1, sh2, sc2, g2 = mods
    wa, wqk, wv, wo, wg, gb = _split_w_in(w_in, gate_b)
    qkv, qkm, vm, om, gc = _in_call(x, sc1, sh1, norm_mix.reshape(1, -1), wa, wqk, wv, wo, wg, gb,
                                    per_token, seq)
    if kv_cache is None:
        att = _attn_p_call(qkv, bias_p, sinks, batch, seq)
        kv3 = qkv.reshape(batch, seq, -1)
        new_k = kv3[:, seq - WINDOW:, A_WIDTH:A_WIDTH + A_KV_WIDTH]
        new_v = kv3[:, seq - WINDOW:, A_WIDTH + A_KV_WIDTH:]
        chunk = M_CHUNK
    else:
        att, new_k, new_v = _attn_s_call(qkv, kv_cache[0], kv_cache[1], bias_c, bias_n, sinks, t_valid)
        chunk = seq
    c0, n0, m0 = state
    mo, c_new, n_new, m_new = _mlstm_call(qkm, vm, om, gc, conv_w, conv_b.reshape(1, -1),
                                          m_norm.reshape(1, -1), conv0, c0, n0, m0,
                                          batch, seq, chunk, min(t_valid, chunk))
    new_conv = qkm.reshape(batch, seq, -1)[:, t_valid - (CONV_W - 1):t_valid]
    x_mid, xw, qp = _out_call(att, mo, x, g1, sc2, sh2, norm_ffn.reshape(1, -1),
                              w_out[:A_WIDTH].astype(BF16), w_out[A_WIDTH:].astype(BF16),
                              peer_query.astype(BF16), per_token, seq)
    idx, gw = _select_call(qp, peer_keys.astype(BF16))
    y = _sc_expert_call(idx, gw, xw, peer_uv)
    x_new = _resid_call(x_mid, y, g2, per_token, seq)
    new_k = new_k.reshape(batch, WINDOW, A_KV_HEADS, A_HEAD_DIM)
    new_v = new_v.reshape(batch, WINDOW, A_KV_HEADS, A_HEAD_DIM)
    return x_new, (new_k, new_v, new_conv, c_new, n_new, m_new[:, :M_HEADS, 0]), idx


def _prompt_group_sizes(n_req):
    if n_req < PROMPT_GROUPS:
        return [1] * n_req
    mid, n_mid = n_req - 2, PROMPT_GROUPS - 2
    weights = [i + 2 for i in range(n_mid)]
    sizes = [max(1, mid * w // sum(weights)) for w in weights]
    for i in range(mid - sum(sizes)):
        sizes[n_mid - 1 - i % n_mid] += 1
    return [1] + sizes + [1]


def kernel(x_prompt, x_sample, c_prompt, c_sample, cache_k, cache_v, state_conv, state_C, state_n, state_m, rel_bias, w_ada, b_ada, norm_mix, norm_ffn, w_in, conv_w, conv_b, gate_b, attn_sinks, m_norm, w_out, peer_query, peer_keys, peer_u, peer_v, norm_final):
    depth = w_ada.shape[0]
    bp, tp, d = x_prompt.shape
    bs, ts, _ = x_sample.shape
    assert tp % WINDOW == 0 and tp % M_CHUNK == 0 and ts <= SAMPLE_PAD and ts >= CONV_W - 1

    mod_all = _ada_call(jnp.concatenate([c_prompt, c_sample], axis=0), w_ada, b_ada)

    qi = np.arange(WINDOW)[:, None]
    bias_p = _bias_call(rel_bias, qi + WINDOW - np.arange(2 * WINDOW)[None, :])
    qs = np.arange(SAMPLE_PAD)[:, None]
    bias_c = _bias_call(rel_bias, qs + WINDOW - np.arange(WINDOW)[None, :])
    bias_n = _bias_call(rel_bias, qs - np.arange(SAMPLE_PAD)[None, :])

    sizes = _prompt_group_sizes(bp)
    starts = np.cumsum([0] + sizes).tolist()
    xg = [x_prompt[starts[g]:starts[g + 1]].reshape(sizes[g] * tp, d) for g in range(len(sizes))]
    xs = jnp.pad(x_sample, ((0, 0), (0, SAMPLE_PAD - ts), (0, 0))).reshape(bs * SAMPLE_PAD, d)
    halo_pad = ((0, 0), (SUBLANES - (CONV_W - 1), 0), (0, 0))

    st_p, st_s = [], []
    for l in range(depth):
        lw = (norm_mix[l], norm_ffn[l], w_in[l], conv_w[l], conv_b[l], gate_b[l], attn_sinks[l], m_norm[l],
              w_out[l], peer_query[l], peer_keys[l], _pack_tables(peer_u[l], peer_v[l]))
        mod_s = [jnp.repeat(m, SAMPLE_PAD, axis=0) for m in jnp.split(mod_all[l, bp:], 6, axis=-1)]
        sp_groups = []
        for g, bg in enumerate(sizes):
            mod_g = [m.reshape(bg, 1, d) for m in jnp.split(mod_all[l, starts[g]:starts[g + 1]], 6, axis=-1)]
            zero_state = (jnp.zeros((bg, M_HEADS, M_HEAD_DIM, M_HEAD_DIM), F32),
                          jnp.zeros((bg, M_HEADS, M_HEAD_DIM), F32),
                          jnp.zeros((bg, SUBLANES, LANES), F32))
            zero_conv = jnp.zeros((bg, SUBLANES, 2 * M_WIDTH), F32)
            xg[g], sp, last_idx = _layer(xg[g], mod_g, False, bg, tp, tp, lw, bias_p, None, None, None,
                                         zero_conv, zero_state)
            sp_groups.append(sp)
        st_p.append([jnp.concatenate([sp[i] for sp in sp_groups], axis=0) for i in range(6)])
        state_s = (state_C[l].astype(F32), state_n[l].astype(F32),
                   jnp.broadcast_to(jnp.pad(state_m[l].astype(F32), ((0, 0), (0, SUBLANES - M_HEADS)))[:, :, None],
                                    (bs, SUBLANES, LANES)))
        kv_cache = (cache_k[l].reshape(bs, WINDOW, A_KV_WIDTH), cache_v[l].reshape(bs, WINDOW, A_KV_WIDTH))
        xs, ss, _ = _layer(xs, mod_s, True, bs, SAMPLE_PAD, ts, lw, None, bias_c, bias_n, kv_cache,
                           jnp.pad(state_conv[l].astype(F32), halo_pad), state_s, after=last_idx)
        st_s.append(ss)

    gfin = norm_final.reshape(1, d)
    y_prompt = jnp.concatenate([_final_call(x, gfin).reshape(bg, tp, d) for x, bg in zip(xg, sizes)], axis=0)
    y_sample = _final_call(xs, gfin).reshape(bs, SAMPLE_PAD, d)[:, :ts]
    outs_p = [jnp.stack([s[i] for s in st_p]) for i in range(6)]
    outs_s = [jnp.stack([s[i] for s in st_s]) for i in range(6)]
    return (y_prompt, y_sample, *outs_p, *outs_s)
```

```python
import functools
import math

import numpy as np
import jax
import jax.numpy as jnp
from jax import lax
from jax.experimental import pallas as pl
from jax.experimental.pallas import tpu as pltpu
from jax.experimental.pallas import tpu_sc as plsc

F32 = jnp.float32
BF16 = jnp.bfloat16
HIGHEST = lax.Precision.HIGHEST

D_MODEL = 1024
A_HEADS = 8
A_KV_HEADS = 2
A_GROUP = A_HEADS // A_KV_HEADS
A_HEAD_DIM = 64
A_WIDTH = A_HEADS * A_HEAD_DIM
A_KV_WIDTH = A_KV_HEADS * A_HEAD_DIM
WINDOW = 128
ATT_SCALE = A_HEAD_DIM ** -0.5
N_BUCKETS = 32
MAX_DISTANCE = WINDOW
M_HEADS = 4
M_HEAD_DIM = 128
M_WIDTH = M_HEADS * M_HEAD_DIM
CONV_W = 4
M_CHUNK = 64
N_KEYS = 128
P_HEADS = 8
P_TOPK = 16
P_KEY_DIM = 256
P_HALF = P_KEY_DIM // 2
EPS = 1e-6
NEG_INF = -1e30

LANES = 128
SUBLANES = 8
SAMPLE_PAD = SUBLANES
VMEM_LIMIT = 48 * 1024 * 1024
PROMPT_GROUPS = 6

NT_DIMS = (((1,), (1,)), ((), ()))
TN_DIMS = (((0,), (0,)), ((), ()))


def _cparams(*sem):
    return pltpu.CompilerParams(dimension_semantics=sem, vmem_limit_bytes=VMEM_LIMIT)


def _bdot(a, b):
    return jnp.dot(a.astype(BF16), b.astype(BF16), preferred_element_type=F32)


def _bdot_nt(a, b):
    return lax.dot_general(a.astype(BF16), b.astype(BF16), NT_DIMS, preferred_element_type=F32)


def _sigmoid(x):
    return 1.0 / (1.0 + jnp.exp(-x))


def _log_sigmoid(x):
    return jnp.minimum(x, 0.0) - jnp.log1p(jnp.exp(-jnp.abs(x)))


def _ada_kernel(c_ref, w_ref, b_ref, o_ref):
    c = c_ref[...]
    s = c * _sigmoid(c)
    o_ref[...] = jnp.dot(s, w_ref[...], precision=HIGHEST, preferred_element_type=F32) + b_ref[...]


def _ada_call(c_all, w_ada, b_ada):
    depth, d, n6 = w_ada.shape
    rows = c_all.shape[0]
    bn = 1024
    return pl.pallas_call(
        _ada_kernel,
        grid=(depth, n6 // bn),
        in_specs=[
            pl.BlockSpec((rows, d), lambda l, j: (0, 0)),
            pl.BlockSpec((None, d, bn), lambda l, j: (l, 0, j)),
            pl.BlockSpec((None, 1, bn), lambda l, j: (l, 0, j)),
        ],
        out_specs=pl.BlockSpec((None, rows, bn), lambda l, j: (l, 0, j)),
        out_shape=jax.ShapeDtypeStruct((depth, rows, n6), F32),
        compiler_params=_cparams("parallel", "parallel"),
        name="ada_mod",
    )(c_all, w_ada, b_ada.reshape(depth, 1, n6))


def _mod_spec(per_token, tm, tokens_per_req):
    if per_token:
        return pl.BlockSpec((tm, D_MODEL), lambda i: (i, 0))
    tiles = tokens_per_req // tm
    return pl.BlockSpec((None, 1, D_MODEL), lambda i: (i // tiles, 0, 0))


def _in_kernel(x_ref, sc_ref, sh_ref, g_ref, wa_ref, wqk_ref, wv_ref, wo_ref, wg_ref, gb_ref,
               qkv_ref, qkm_ref, v_ref, o_ref, gc_ref):
    x = x_ref[...]
    y = x * lax.rsqrt(jnp.mean(x * x, axis=-1, keepdims=True) + EPS) * g_ref[...]
    h = (y * (1.0 + sc_ref[...]) + sh_ref[...]).astype(BF16)
    qkv_ref[...] = jnp.dot(h, wa_ref[...], preferred_element_type=F32)
    qkm_ref[...] = jnp.dot(h, wqk_ref[...], preferred_element_type=F32)
    v_ref[...] = jnp.dot(h, wv_ref[...], preferred_element_type=F32)
    o_ref[...] = jnp.dot(h, wo_ref[...], preferred_element_type=F32)
    g = jnp.dot(h, wg_ref[...], preferred_element_type=F32) + gb_ref[...]
    lane = lax.broadcasted_iota(jnp.int32, g.shape, 1)
    gc_ref[...] = jnp.where(lane < M_HEADS, g, jnp.where(lane < 2 * M_HEADS, _log_sigmoid(g), 0.0))


def _in_call(x, sc, sh, gnorm, wa, wqk, wv, wo, wg, gb, per_token, tokens_per_req):
    n = x.shape[0]
    tm = min(512, n if per_token else tokens_per_req)
    mod = _mod_spec(per_token, tm, tokens_per_req)
    full = lambda shape: pl.BlockSpec(shape, lambda i: (0,) * len(shape))
    row = lambda w: pl.BlockSpec((tm, w), lambda i: (i, 0))
    return pl.pallas_call(
        _in_kernel,
        grid=(n // tm,),
        in_specs=[row(D_MODEL), mod, mod, full((1, D_MODEL)), full(wa.shape), full(wqk.shape),
                  full(wv.shape), full(wo.shape), full(wg.shape), full((1, LANES))],
        out_specs=[row(wa.shape[1]), row(wqk.shape[1]), row(wv.shape[1]), row(wo.shape[1]), row(LANES)],
        out_shape=[jax.ShapeDtypeStruct((n, w), F32)
                   for w in (wa.shape[1], wqk.shape[1], wv.shape[1], wo.shape[1], LANES)],
        compiler_params=_cparams("parallel"),
        name="in_proj",
    )(x, sc, sh, gnorm, wa, wqk, wv, wo, wg, gb)


def _t5_bucket_np(dist):
    n = np.maximum(dist, 0)
    max_exact = N_BUCKETS // 2
    nf = np.maximum(n, 1).astype(np.float64)
    large = max_exact + (np.log(nf / max_exact) / math.log(MAX_DISTANCE / max_exact)
                         * (N_BUCKETS - max_exact)).astype(np.int32)
    return np.where(n < max_exact, n, np.minimum(large, N_BUCKETS - 1)).astype(np.int32)


def _bias_kernel(bucket_ref, rel_ref, o_ref):
    bucket = bucket_ref[...]
    for h in range(A_HEADS):
        acc = jnp.zeros(bucket.shape, F32)
        for b in range(N_BUCKETS):
            acc = jnp.where(bucket == b, rel_ref[b, h], acc)
        o_ref[h] = acc


def _bias_call(rel_bias, dist):
    bucket = jnp.asarray(_t5_bucket_np(dist))
    nq, nk = dist.shape
    return pl.pallas_call(
        _bias_kernel,
        in_specs=[pl.BlockSpec((nq, nk), lambda: (0, 0)),
                  pl.BlockSpec(memory_space=pltpu.SMEM)],
        out_specs=pl.BlockSpec((A_HEADS, nq, nk), lambda: (0, 0, 0)),
        out_shape=jax.ShapeDtypeStruct((A_HEADS, nq, nk), F32),
        name="t5_bias",
    )(bucket, rel_bias)


def _attn_p_kernel(q_ref, kp_ref, kc_ref, vp_ref, vc_ref, bias_ref, sink_ref, o_ref):
    i = pl.program_id(1)
    qi = lax.broadcasted_iota(jnp.int32, (WINDOW, WINDOW), 0)
    kj = lax.broadcasted_iota(jnp.int32, (WINDOW, WINDOW), 1)
    valid_prev = jnp.logical_and(kj > qi, i > 0)
    valid_cur = kj <= qi
    q = q_ref[...]
    for h in range(A_HEADS):
        kv = h // A_GROUP
        qh = q[:, h * A_HEAD_DIM:(h + 1) * A_HEAD_DIM]
        sl = slice(kv * A_HEAD_DIM, (kv + 1) * A_HEAD_DIM)
        bias = bias_ref[h]
        sp = _bdot_nt(qh, kp_ref[:, sl]) * ATT_SCALE + bias[:, :WINDOW]
        sc = _bdot_nt(qh, kc_ref[:, sl]) * ATT_SCALE + bias[:, WINDOW:]
        sp = jnp.where(valid_prev, sp, NEG_INF)
        sc = jnp.where(valid_cur, sc, NEG_INF)
        sink = sink_ref[0, h]
        mx = jnp.maximum(jnp.maximum(jnp.max(sp, axis=-1, keepdims=True),
                                     jnp.max(sc, axis=-1, keepdims=True)), sink)
        ep = jnp.exp(sp - mx)
        ec = jnp.exp(sc - mx)
        den = (jnp.sum(ep, axis=-1, keepdims=True) + jnp.sum(ec, axis=-1, keepdims=True)
               + jnp.exp(sink - mx))
        o = _bdot(ep / den, vp_ref[:, sl]) + _bdot(ec / den, vc_ref[:, sl])
        o_ref[:, h * A_HEAD_DIM:(h + 1) * A_HEAD_DIM] = o


def _attn_p_call(qkv, bias, sinks, batch, seq):
    nb = seq // WINDOW
    n = batch * seq
    kcol = A_WIDTH // A_KV_WIDTH
    vcol = kcol + 1
    cur = lambda col: pl.BlockSpec((WINDOW, A_KV_WIDTH), lambda b, i: (b * nb + i, col))
    prev = lambda col: pl.BlockSpec((WINDOW, A_KV_WIDTH),
                                    lambda b, i: (b * nb + jnp.maximum(i - 1, 0), col))
    return pl.pallas_call(
        _attn_p_kernel,
        grid=(batch, nb),
        in_specs=[pl.BlockSpec((WINDOW, A_WIDTH), lambda b, i: (b * nb + i, 0)),
                  prev(kcol), cur(kcol), prev(vcol), cur(vcol),
                  pl.BlockSpec((A_HEADS, WINDOW, 2 * WINDOW), lambda b, i: (0, 0, 0)),
                  pl.BlockSpec(memory_space=pltpu.SMEM)],
        out_specs=pl.BlockSpec((WINDOW, A_WIDTH), lambda b, i: (b * nb + i, 0)),
        out_shape=jax.ShapeDtypeStruct((n, A_WIDTH), F32),
        compiler_params=_cparams("parallel", "parallel"),
        name="swa_prompt",
    )(qkv, qkv, qkv, qkv, qkv, bias, sinks.reshape(1, A_HEADS))


def _attn_s_kernel(n_new, qkv_ref, ck_ref, cv_ref, bc_ref, bn_ref, sink_ref,
                   o_ref, nk_ref, nv_ref, kk_s, vv_s):
    qkv = qkv_ref[...]
    knew = qkv[:, A_WIDTH:A_WIDTH + A_KV_WIDTH]
    vnew = qkv[:, A_WIDTH + A_KV_WIDTH:A_WIDTH + 2 * A_KV_WIDTH]
    ck = ck_ref[...]
    cv = cv_ref[...]
    rows = A_GROUP * SAMPLE_PAD
    qi = lax.broadcasted_iota(jnp.int32, (rows, WINDOW), 0) % SAMPLE_PAD
    kj = lax.broadcasted_iota(jnp.int32, (rows, WINDOW), 1)
    valid_c = kj > qi
    rcol = lax.broadcasted_iota(jnp.int32, (rows, 1), 0)
    qcol = rcol % SAMPLE_PAD
    for kv in range(A_KV_HEADS):
        heads = range(kv * A_GROUP, (kv + 1) * A_GROUP)
        sl = slice(kv * A_HEAD_DIM, (kv + 1) * A_HEAD_DIM)
        qs = jnp.concatenate([qkv[:, h * A_HEAD_DIM:(h + 1) * A_HEAD_DIM] for h in heads], axis=0)
        bias_c = jnp.concatenate([bc_ref[h] for h in heads], axis=0)
        bias_n = jnp.concatenate([bn_ref[h] for h in heads], axis=0)
        sink = jnp.zeros((rows, 1), F32)
        for g, h in enumerate(heads):
            sink = jnp.where(rcol // SAMPLE_PAD == g, sink_ref[0, h], sink)
        s_c = _bdot_nt(qs, ck[:, sl]) * ATT_SCALE + bias_c
        s_c = jnp.where(valid_c, s_c, NEG_INF)
        s_n = []
        for j in range(n_new):
            sj = jnp.sum(qs * knew[j:j + 1, sl], axis=-1, keepdims=True) * ATT_SCALE + bias_n[:, j:j + 1]
            s_n.append(jnp.where(qcol >= j, sj, NEG_INF))
        mx = jnp.maximum(jnp.max(s_c, axis=-1, keepdims=True), sink)
        for sj in s_n:
            mx = jnp.maximum(mx, sj)
        e_c = jnp.exp(s_c - mx)
        den = jnp.sum(e_c, axis=-1, keepdims=True) + jnp.exp(sink - mx)
        o = _bdot(e_c, cv[:, sl])
        for j, sj in enumerate(s_n):
            ej = jnp.exp(sj - mx)
            den = den + ej
            o = o + ej * vnew[j:j + 1, sl]
        o = o / den
        for g, h in enumerate(heads):
            o_ref[:, h * A_HEAD_DIM:(h + 1) * A_HEAD_DIM] = o[g * SAMPLE_PAD:(g + 1) * SAMPLE_PAD, :]
    kk_s[0:WINDOW, :] = ck
    kk_s[WINDOW:WINDOW + SAMPLE_PAD, :] = knew
    vv_s[0:WINDOW, :] = cv
    vv_s[WINDOW:WINDOW + SAMPLE_PAD, :] = vnew
    nk_ref[...] = kk_s[n_new:n_new + WINDOW, :]
    nv_ref[...] = vv_s[n_new:n_new + WINDOW, :]


def _attn_s_call(qkv, ck, cv, bias_c, bias_n, sinks, n_new):
    nreq = ck.shape[0]
    wq = qkv.shape[1]
    full3 = lambda shape: pl.BlockSpec(shape, lambda b: (0, 0, 0))
    cache = pl.BlockSpec((None, WINDOW, A_KV_WIDTH), lambda b: (b, 0, 0))
    return pl.pallas_call(
        functools.partial(_attn_s_kernel, n_new),
        grid=(nreq,),
        in_specs=[pl.BlockSpec((SAMPLE_PAD, wq), lambda b: (b, 0)), cache, cache,
                  full3(bias_c.shape), full3(bias_n.shape),
                  pl.BlockSpec(memory_space=pltpu.SMEM)],
        out_specs=[pl.BlockSpec((SAMPLE_PAD, A_WIDTH), lambda b: (b, 0)), cache, cache],
        out_shape=[jax.ShapeDtypeStruct((nreq * SAMPLE_PAD, A_WIDTH), F32),
                   jax.ShapeDtypeStruct(ck.shape, F32), jax.ShapeDtypeStruct(cv.shape, F32)],
        scratch_shapes=[pltpu.VMEM((WINDOW + SAMPLE_PAD, A_KV_WIDTH), F32),
                        pltpu.VMEM((WINDOW + SAMPLE_PAD, A_KV_WIDTH), F32)],
        compiler_params=_cparams("parallel"),
        name="swa_sample",
    )(qkv, ck, cv, bias_c, bias_n, sinks.reshape(1, A_HEADS))


def _mlstm_kernel(chunk, t_valid, qk_ref, v_ref, og_ref, gc_ref, cw_ref, cb_ref, mn_ref,
                  conv0_ref, c0_ref, n0_ref, m0_ref,
                  out_ref, cout_ref, nout_ref, mout_ref,
                  xp_s, c_s, n_s, m_s):
    step = pl.program_id(1)
    halo = SUBLANES

    @pl.when(step == 0)
    def _():
        xp_s[0:halo, :] = conv0_ref[...]
        c_s[...] = c0_ref[...]
        n_s[...] = n0_ref[...]
        m_s[...] = m0_ref[...]

    xp_s[halo:halo + chunk, :] = qk_ref[...]
    cw = cw_ref[...]
    y = cb_ref[...]
    for i in range(CONV_W):
        off = halo - (CONV_W - 1) + i
        y = y + xp_s[off:off + chunk, :] * cw[i:i + 1, :]
    xp_s[0:halo, :] = xp_s[chunk:chunk + halo, :]
    y = y * _sigmoid(y)
    q_all = y[:, :M_WIDTH]
    k_all = y[:, M_WIDTH:] * (M_HEAD_DIM ** -0.5)

    g = gc_ref[...]
    if t_valid < chunk:
        row = lax.broadcasted_iota(jnp.int32, g.shape, 0)
        lane = lax.broadcasted_iota(jnp.int32, g.shape, 1)
        g = jnp.where(row < t_valid, g, jnp.where(lane < M_HEADS, NEG_INF, 0.0))
    tr = lax.broadcasted_iota(jnp.int32, (chunk, chunk), 0)
    tc = lax.broadcasted_iota(jnp.int32, (chunk, chunk), 1)
    causal = tr >= tc
    tri = causal.astype(F32)
    bcol = jnp.dot(tri, g, precision=HIGHEST, preferred_element_type=F32)
    er = lax.broadcasted_iota(jnp.int32, (SUBLANES, LANES), 0)
    ec = lax.broadcasted_iota(jnp.int32, (SUBLANES, LANES), 1)
    eye = (er == ec).astype(F32)
    g_rows = lax.dot_general(eye, g, NT_DIMS, precision=HIGHEST, preferred_element_type=F32)
    b_rows = lax.dot_general(eye, bcol, NT_DIMS, precision=HIGHEST, preferred_element_type=F32)

    for h in range(M_HEADS):
        hs = slice(h * M_HEAD_DIM, (h + 1) * M_HEAD_DIM)
        b_c = bcol[:, M_HEADS + h:M_HEADS + h + 1]
        ig_c = g[:, h:h + 1]
        b_r = b_rows[M_HEADS + h:M_HEADS + h + 1, :]
        ig_r = g_rows[h:h + 1, :]
        m_prev = m_s[h:h + 1, 0:1]
        logw = jnp.where(causal, b_c - b_r + ig_r, -jnp.inf)
        inter = b_c + m_prev
        m_t = jnp.maximum(inter, jnp.max(logw, axis=-1, keepdims=True))
        w = jnp.exp(logw - m_t)
        a = jnp.exp(inter - m_t)
        q = q_all[:, hs]
        k = k_all[:, hs]
        v = v_ref[:, hs]
        cmat = c_s[h]
        nvec = n_s[h:h + 1, :]
        wqk = w * _bdot_nt(q, k)
        num = _bdot(wqk, v) + a * _bdot_nt(q, cmat)
        den = jnp.sum(wqk, axis=-1, keepdims=True) + a * jnp.sum(q * nvec, axis=-1, keepdims=True)
        hh = num / jnp.maximum(jnp.abs(den), jnp.exp(-m_t))
        m_new = m_t[chunk - 1:chunk, :]
        b_last = b_c[chunk - 1:chunk, :]
        wl = jnp.exp(b_last - b_c + ig_c - m_new)
        al = jnp.exp(b_last + m_prev - m_new)
        c_s[h] = al * cmat + lax.dot_general((v * wl).astype(BF16), k.astype(BF16), TN_DIMS,
                                             preferred_element_type=F32)
        n_s[h:h + 1, :] = al * nvec + jnp.sum(wl * k, axis=0, keepdims=True)
        m_s[h:h + 1, :] = jnp.broadcast_to(m_new, (1, LANES))
        hn = hh * lax.rsqrt(jnp.mean(hh * hh, axis=-1, keepdims=True) + EPS) * mn_ref[:, hs]
        out_ref[:, hs] = _sigmoid(og_ref[:, hs]) * hn

    @pl.when(step == pl.num_programs(1) - 1)
    def _():
        cout_ref[...] = c_s[...]
        nout_ref[...] = n_s[...]
        mout_ref[...] = m_s[...]


def _mlstm_call(qk, v, og, gc, conv_w, conv_b, m_norm, conv0, c0, n0, m0, batch, seq, chunk, t_valid):
    nc = seq // chunk
    n = batch * seq
    row = lambda w: pl.BlockSpec((chunk, w), lambda b, c: (b * nc + c, 0))
    full2 = lambda shape: pl.BlockSpec(shape, lambda b, c: (0, 0))
    per_b = lambda shape: pl.BlockSpec((None,) + shape, lambda b, c: (b,) + (0,) * len(shape))
    dh = M_HEAD_DIM
    return pl.pallas_call(
        functools.partial(_mlstm_kernel, chunk, t_valid),
        grid=(batch, nc),
        in_specs=[row(2 * M_WIDTH), row(M_WIDTH), row(M_WIDTH), row(LANES),
                  full2((CONV_W, 2 * M_WIDTH)), full2((1, 2 * M_WIDTH)), full2((1, M_WIDTH)),
                  per_b((SUBLANES, 2 * M_WIDTH)), per_b((M_HEADS, dh, dh)), per_b((M_HEADS, dh)),
                  per_b((SUBLANES, LANES))],
        out_specs=[row(M_WIDTH), per_b((M_HEADS, dh, dh)), per_b((M_HEADS, dh)), per_b((SUBLANES, LANES))],
        out_shape=[jax.ShapeDtypeStruct((n, M_WIDTH), F32),
                   jax.ShapeDtypeStruct((batch, M_HEADS, dh, dh), F32),
                   jax.ShapeDtypeStruct((batch, M_HEADS, dh), F32),
                   jax.ShapeDtypeStruct((batch, SUBLANES, LANES), F32)],
        scratch_shapes=[pltpu.VMEM((SUBLANES + chunk, 2 * M_WIDTH), F32),
                        pltpu.VMEM((M_HEADS, dh, dh), F32),
                        pltpu.VMEM((M_HEADS, dh), F32),
                        pltpu.VMEM((SUBLANES, LANES), F32)],
        compiler_params=_cparams("parallel", "arbitrary"),
        name="mlstm",
    )(qk, v, og, gc, conv_w, conv_b, m_norm, conv0, c0, n0, m0)


def _pack_words(x):
    bits = pltpu.bitcast(x.astype(BF16).astype(F32), jnp.uint32)
    half = x.shape[1] // 2
    lo = lax.shift_right_logical(bits[:, :half], jnp.uint32(16))
    hi = bits[:, half:] & jnp.uint32(0xFFFF0000)
    return pltpu.bitcast(lo | hi, jnp.int32)


def _out_kernel(att_ref, mo_ref, x_ref, g1_ref, sc_ref, sh_ref, gn_ref, wa_ref, wm_ref, wq_ref,
                xo_ref, h2_ref, qp_ref):
    mix = (jnp.dot(att_ref[...].astype(BF16), wa_ref[...], preferred_element_type=F32)
           + jnp.dot(mo_ref[...].astype(BF16), wm_ref[...], preferred_element_type=F32))
    x = x_ref[...] + g1_ref[...] * mix
    xo_ref[...] = x
    y = x * lax.rsqrt(jnp.mean(x * x, axis=-1, keepdims=True) + EPS) * gn_ref[...]
    h2 = y * (1.0 + sc_ref[...]) + sh_ref[...]
    qp_ref[...] = jnp.dot(h2.astype(BF16), wq_ref[...], preferred_element_type=F32)
    h2_ref[...] = _pack_words(h2)


def _out_call(att, mo, x, g1, sc, sh, gnorm, wa, wm, wq, per_token, tokens_per_req):
    n = x.shape[0]
    tm = min(256, n if per_token else tokens_per_req)
    mod = _mod_spec(per_token, tm, tokens_per_req)
    full = lambda shape: pl.BlockSpec(shape, lambda i: (0,) * len(shape))
    row = lambda w: pl.BlockSpec((tm, w), lambda i: (i, 0))
    nq = wq.shape[1]
    return pl.pallas_call(
        _out_kernel,
        grid=(n // tm,),
        in_specs=[row(A_WIDTH), row(M_WIDTH), row(D_MODEL), mod, mod, mod, full((1, D_MODEL)),
                  full(wa.shape), full(wm.shape), full(wq.shape)],
        out_specs=[row(D_MODEL), row(D_MODEL // 2), row(nq)],
        out_shape=[jax.ShapeDtypeStruct((n, D_MODEL), F32), jax.ShapeDtypeStruct((n, D_MODEL // 2), jnp.int32),
                   jax.ShapeDtypeStruct((n, nq), F32)],
        compiler_params=_cparams("parallel"),
        name="out_proj",
    )(att, mo, x, g1, sc, sh, gnorm, wa, wm, wq)


def _pk_cells():
    return [(a, b) for a in range(P_TOPK) for b in range(P_TOPK) if (a + 1) * (b + 1) <= P_TOPK]


PK_CELL_ROWS = 64


def _pk_expand_mats():
    cells = _pk_cells()
    e0 = np.zeros((PK_CELL_ROWS, LANES), np.float32)
    e1 = np.zeros((PK_CELL_ROWS, LANES), np.float32)
    for j, (a, b) in enumerate(cells):
        e0[j, a] = 1.0
        e1[j, b] = 1.0
    return e0, e1, len(cells)


def _top_rows(s, rowf, rounds):
    n_rows = s.shape[0]
    vals, idxs = [], []
    for _ in range(rounds):
        m = jnp.max(s, axis=0, keepdims=True)
        i = jnp.min(jnp.where(s == m, rowf, float(n_rows)), axis=0, keepdims=True)
        vals.append(m)
        idxs.append(i)
        s = jnp.where(rowf == i, -jnp.inf, s)
    return jnp.concatenate(vals, axis=0), jnp.concatenate(idxs, axis=0)


def _select_kernel(n_cells, qp_ref, keys_ref, e0_ref, e1_ref, idx_ref, gw_ref, idx_s, gw_s):
    tm = qp_ref.shape[0]
    keyf = lax.broadcasted_iota(jnp.int32, (N_KEYS, tm), 0).astype(F32)
    cellf = lax.broadcasted_iota(jnp.int32, (PK_CELL_ROWS, tm), 0).astype(F32)
    e0 = e0_ref[...]
    e1 = e1_ref[...]
    pad = jnp.zeros((LANES - P_TOPK, tm), F32)

    def head(h, carry):
        h = jnp.asarray(h, jnp.int32)
        sub = []
        for c in range(2):
            col = pl.multiple_of((h * 2 + c) * P_HALF, P_HALF)
            s = _bdot_nt(keys_ref[h, c], qp_ref[:, pl.ds(col, P_HALF)])
            sub.append(_top_rows(s, keyf, P_TOPK))
        (v0, i0), (v1, i1) = sub
        expand = lambda e, x: jnp.dot(e, jnp.concatenate([x, pad], axis=0), precision=HIGHEST,
                                      preferred_element_type=F32)
        cand = expand(e0, v0) + expand(e1, v1)
        cidx = expand(e0, i0 * float(N_KEYS)) + expand(e1, i1)
        cand = jnp.where(cellf < n_cells, cand, -jnp.inf)
        best, eidx = [], []
        for _ in range(P_TOPK):
            m = jnp.max(cand, axis=0, keepdims=True)
            j = jnp.min(jnp.where(cand == m, cellf, float(PK_CELL_ROWS)), axis=0, keepdims=True)
            hit = cellf == j
            eidx.append(jnp.max(jnp.where(hit, cidx, -1.0), axis=0, keepdims=True))
            best.append(m)
            cand = jnp.where(hit, -jnp.inf, cand)
        best = jnp.concatenate(best, axis=0)
        e = jnp.exp(best - best[0:1, :])
        row0 = pl.multiple_of(h * P_TOPK, P_TOPK)
        gw_s[pl.ds(row0, P_TOPK), :] = e / jnp.sum(e, axis=0, keepdims=True)
        idx_s[pl.ds(row0, P_TOPK), :] = jnp.concatenate(eidx, axis=0)
        return carry

    lax.fori_loop(0, P_HEADS, head, 0)
    idx_ref[...] = idx_s[...].T.astype(jnp.int32)
    gw_ref[...] = gw_s[...].T


def _select_call(qp, keys_bf16):
    n = qp.shape[0]
    tm = min(LANES, n)
    e0, e1, n_cells = _pk_expand_mats()
    full = lambda shape: pl.BlockSpec(shape, lambda i: (0,) * len(shape))
    return pl.pallas_call(
        functools.partial(_select_kernel, n_cells),
        grid=(n // tm,),
        in_specs=[pl.BlockSpec((tm, qp.shape[1]), lambda i: (i, 0)), full(keys_bf16.shape),
                  full((PK_CELL_ROWS, LANES)), full((PK_CELL_ROWS, LANES))],
        out_specs=[pl.BlockSpec((tm, LANES), lambda i: (i, 0)), pl.BlockSpec((tm, LANES), lambda i: (i, 0))],
        out_shape=[jax.ShapeDtypeStruct((n, LANES), jnp.int32), jax.ShapeDtypeStruct((n, LANES), F32)],
        scratch_shapes=[pltpu.VMEM((P_HEADS * P_TOPK, tm), F32), pltpu.VMEM((P_HEADS * P_TOPK, tm), F32)],
        compiler_params=_cparams("parallel"),
        name="peer_select",
    )(qp, keys_bf16, jnp.asarray(e0), jnp.asarray(e1))


N_SEL = P_HEADS * P_TOPK

SC_CORES = 2
SC_SUBCORES = 16
SC_LANES = 16
SC_TOK_BLOCK = 8
SC_ROWS = 32
SC_ROW_BLOCK = 16
SC_ACC_CHAINS = 4
SC_ACC_ROWS = 16
SC_NBUF = 4
N_WCOL = D_MODEL // (2 * SC_LANES)


def _pack_kernel(t_ref, o_ref):
    o_ref[...] = _pack_words(t_ref[...])


def _pack_table(t):
    e, dcol = t.shape
    tm = min(1024, e)
    return pl.pallas_call(
        _pack_kernel,
        grid=(e // tm,),
        in_specs=[pl.BlockSpec((tm, dcol), lambda i: (i, 0))],
        out_specs=pl.BlockSpec((tm, dcol // 2), lambda i: (i, 0)),
        out_shape=jax.ShapeDtypeStruct((e, dcol // 2), jnp.int32),
        compiler_params=_cparams("parallel"),
        name="pack_table",
    )(t)


def _sc_gelu(x):
    z = math.sqrt(2.0 / math.pi) * (x + 0.044715 * (x * x * x))
    t = 1.0 - 2.0 / (jnp.exp(2.0 * z) + 1.0)
    return x * (0.5 * (1.0 + t))


def _sc_expert_body(tokens_per_worker, idx_hbm, gw_hbm, h2_hbm, u_hbm, v_hbm, y_hbm,
                    idx_v, gw_v, x_v, o_v, buf, coef_v, tr_v, sem, in_sem, out_sem):
    wid = lax.axis_index("s") * SC_CORES + lax.axis_index("c")
    base = wid * tokens_per_worker
    lane = lax.iota(jnp.int32, SC_LANES)
    n_gather = N_SEL // SC_ROWS
    zero = jnp.zeros((SC_LANES,), F32)

    n_steps = 2 * n_gather
    assert n_steps % SC_NBUF == 0

    def gather(p, tt, i):
        table = u_hbm if i < n_gather else v_hbm
        j = i % n_gather
        slot = i % SC_NBUF
        return pltpu.make_async_copy(table.at[idx_v.at[p, tt, pl.ds(j * SC_ROWS, SC_ROWS)]], buf.at[slot],
                                     sem.at[slot])

    def unpack(w):
        lo = lax.bitcast_convert_type(lax.shift_left(w, jnp.full((SC_LANES,), 16, jnp.int32)), F32)
        hi = lax.bitcast_convert_type(w & jnp.full((SC_LANES,), -65536, jnp.int32), F32)
        return lo, hi

    def packed(w):
        return plsc.bitcast(w, BF16)

    def unpack_sum(s):
        return unpack(plsc.bitcast(s, jnp.int32))

    def act_chunk(p, tt, j, slot):
        @pl.loop(0, SC_ROWS // SC_LANES)
        def _(half):
            for rb in range(SC_LANES // SC_ROW_BLOCK):
                r0 = half * SC_LANES + rb * SC_ROW_BLOCK

                def col(c, accs):
                    w0 = pl.multiple_of(c * (2 * SC_LANES), 2 * SC_LANES)
                    xa = packed(x_v[p, tt, pl.ds(w0, SC_LANES)])
                    xb = packed(x_v[p, tt, pl.ds(w0 + SC_LANES, SC_LANES)])
                    out = []
                    for r, a in enumerate(accs):
                        ua = packed(buf[slot, r0 + r, pl.ds(w0, SC_LANES)])
                        ub = packed(buf[slot, r0 + r, pl.ds(w0 + SC_LANES, SC_LANES)])
                        lo, hi = unpack_sum(ua * xa + ub * xb)
                        out.append(a + lo + hi)
                    return tuple(out)

                accs = lax.fori_loop(0, N_WCOL // 2, col, (zero,) * SC_ROW_BLOCK)
                for r in range(SC_ROW_BLOCK):
                    tr_v[pl.ds((rb * SC_ROW_BLOCK + r) * SC_LANES, SC_LANES)] = accs[r]
            tot = zero
            for jj in range(SC_LANES):
                tot = tot + plsc.load_gather(tr_v, [lane * SC_LANES + jj])
            k0 = pl.multiple_of(j * SC_ROWS + half * SC_LANES, SC_LANES)
            coef_v[pl.ds(k0, SC_LANES)] = gw_v[p, tt, pl.ds(k0, SC_LANES)] * _sc_gelu(tot)

    def acc_chunk(p, tt, j, slot, first):
        def tree_sum(parts):
            while len(parts) > 1:
                parts = [parts[i] + parts[i + 1] for i in range(0, len(parts), 2)]
            return parts[0]

        for rb in range(SC_ROWS // SC_ACC_ROWS):
            rows = list(range(rb * SC_ACC_ROWS, (rb + 1) * SC_ACC_ROWS))
            splat = {}
            for r in rows:
                c16 = plsc.load_gather(coef_v, [jnp.full((SC_LANES,), j * SC_ROWS + r, jnp.int32)])
                splat[r] = plsc.pack(c16, c16, format=plsc.PackFormat.INTERLEAVED)
            fresh = first and rb == 0

            @plsc.parallel_loop(0, N_WCOL)
            def _(c):
                woff = pl.multiple_of(c * SC_LANES, SC_LANES)
                hoff = pl.multiple_of(c * SC_LANES + D_MODEL // 2, SC_LANES)
                pa, pb = [], []
                for n in range(0, SC_ACC_ROWS, 2):
                    r0, r1 = rows[n], rows[n + 1]
                    s = (splat[r0] * packed(buf[slot, r0, pl.ds(woff, SC_LANES)])
                         + splat[r1] * packed(buf[slot, r1, pl.ds(woff, SC_LANES)]))
                    lo, hi = unpack_sum(s)
                    if n // 2 < SC_ACC_CHAINS:
                        pa.append(lo)
                        pb.append(hi)
                    else:
                        pa[(n // 2) % SC_ACC_CHAINS] = pa[(n // 2) % SC_ACC_CHAINS] + lo
                        pb[(n // 2) % SC_ACC_CHAINS] = pb[(n // 2) % SC_ACC_CHAINS] + hi
                sa, sb = tree_sum(pa), tree_sum(pb)
                if not fresh:
                    sa = sa + o_v[p, tt, pl.ds(woff, SC_LANES)]
                    sb = sb + o_v[p, tt, pl.ds(hoff, SC_LANES)]
                o_v[p, tt, pl.ds(woff, SC_LANES)] = sa
                o_v[p, tt, pl.ds(hoff, SC_LANES)] = sb

    n_blocks = tokens_per_worker // SC_TOK_BLOCK

    def block_start(blk):
        return pl.multiple_of(base + blk * SC_TOK_BLOCK, SC_TOK_BLOCK)

    def in_copies(blk, p):
        rows = pl.ds(block_start(blk), SC_TOK_BLOCK)
        return [pltpu.make_async_copy(idx_hbm.at[rows], idx_v.at[p], in_sem.at[p]),
                pltpu.make_async_copy(gw_hbm.at[rows], gw_v.at[p], in_sem.at[p]),
                pltpu.make_async_copy(h2_hbm.at[rows], x_v.at[p], in_sem.at[p])]

    def out_copy(blk, p):
        return pltpu.make_async_copy(o_v.at[p], y_hbm.at[pl.ds(block_start(blk), SC_TOK_BLOCK)], out_sem.at[p])

    for cp in in_copies(0, 0):
        cp.start()

    @pl.loop(0, n_blocks)
    def _(blk):
        p = lax.rem(blk, 2)
        for cp in in_copies(blk, p):
            cp.wait()

        @pl.when(blk + 1 < n_blocks)
        def _():
            for cp in in_copies(blk + 1, 1 - p):
                cp.start()

        @pl.when(blk >= 2)
        def _():
            out_copy(blk - 2, p).wait()

        ahead = SC_NBUF - 1
        for i in range(ahead):
            gather(p, 0, i).start()

        @pl.loop(0, SC_TOK_BLOCK)
        def _(tt):
            for i in range(n_steps):
                if i + ahead < n_steps:
                    gather(p, tt, i + ahead).start()
                else:
                    @pl.when(tt + 1 < SC_TOK_BLOCK)
                    def _():
                        gather(p, tt + 1, i + ahead - n_steps).start()
                gather(p, tt, i).wait()
                if i < n_gather:
                    act_chunk(p, tt, i, i % SC_NBUF)
                else:
                    acc_chunk(p, tt, i - n_gather, i % SC_NBUF, i == n_gather)

        out_copy(blk, p).start()

    for blk in (n_blocks - 2, n_blocks - 1):
        out_copy(blk, blk % 2).wait()


def _sc_expert_call(idx, gw, xw, u, v):
    n = idx.shape[0]
    workers = SC_CORES * SC_SUBCORES
    assert n % (workers * SC_TOK_BLOCK) == 0 and n // (workers * SC_TOK_BLOCK) >= 2
    mesh = plsc.VectorSubcoreMesh(core_axis_name="c", subcore_axis_name="s")
    return pl.kernel(
        functools.partial(_sc_expert_body, n // workers),
        out_type=jax.ShapeDtypeStruct((n, D_MODEL), F32),
        mesh=mesh,
        scratch_types=[pltpu.VMEM((2, SC_TOK_BLOCK, N_SEL), jnp.int32),
                       pltpu.VMEM((2, SC_TOK_BLOCK, N_SEL), F32),
                       pltpu.VMEM((2, SC_TOK_BLOCK, D_MODEL // 2), jnp.int32),
                       pltpu.VMEM((2, SC_TOK_BLOCK, D_MODEL), F32),
                       pltpu.VMEM((SC_NBUF, SC_ROWS, D_MODEL // 2), jnp.int32),
                       pltpu.VMEM((N_SEL,), F32),
                       pltpu.VMEM((SC_LANES * SC_LANES,), F32),
                       pltpu.SemaphoreType.DMA((SC_NBUF,)), pltpu.SemaphoreType.DMA((2,)),
                       pltpu.SemaphoreType.DMA((2,))],
        compiler_params=pltpu.CompilerParams(needs_layout_passes=False),
        name="peer_experts_sc",
    )(idx, gw, xw, u, v)


def _resid_kernel(x_ref, y_ref, g_ref, o_ref):
    o_ref[...] = x_ref[...] + g_ref[...] * y_ref[...]


def _resid_call(x, y, g2, per_token, tokens_per_req):
    n = x.shape[0]
    tm = min(512, n if per_token else tokens_per_req)
    row = pl.BlockSpec((tm, D_MODEL), lambda i: (i, 0))
    return pl.pallas_call(
        _resid_kernel,
        grid=(n // tm,),
        in_specs=[row, row, _mod_spec(per_token, tm, tokens_per_req)],
        out_specs=row,
        out_shape=jax.ShapeDtypeStruct((n, D_MODEL), F32),
        compiler_params=_cparams("parallel"),
        name="peer_residual",
    )(x, y, g2)


def _final_kernel(x_ref, g_ref, o_ref):
    x = x_ref[...]
    o_ref[...] = x * lax.rsqrt(jnp.mean(x * x, axis=-1, keepdims=True) + EPS) * g_ref[...]


def _final_call(x, g):
    n = x.shape[0]
    tm = min(512, n)
    return pl.pallas_call(
        _final_kernel,
        grid=(n // tm,),
        in_specs=[pl.BlockSpec((tm, D_MODEL), lambda i: (i, 0)), pl.BlockSpec((1, D_MODEL), lambda i: (0, 0))],
        out_specs=pl.BlockSpec((tm, D_MODEL), lambda i: (i, 0)),
        out_shape=jax.ShapeDtypeStruct((n, D_MODEL), F32),
        compiler_params=_cparams("parallel"),
        name="final_norm",
    )(x, g)


def _split_w_in(w_in_l, gate_b_l):
    cuts = np.cumsum([A_WIDTH + 2 * A_KV_WIDTH, 2 * M_WIDTH, M_WIDTH, M_WIDTH]).tolist()
    wa = w_in_l[:, :cuts[0]].astype(BF16)
    wqk = w_in_l[:, cuts[0]:cuts[1]].astype(BF16)
    wv = w_in_l[:, cuts[1]:cuts[2]].astype(BF16)
    wo = w_in_l[:, cuts[2]:cuts[3]].astype(BF16)
    ng = 2 * M_HEADS
    wg = jnp.pad(w_in_l[:, cuts[3]:], ((0, 0), (0, LANES - ng))).astype(BF16)
    gb = jnp.pad(gate_b_l.astype(F32), (0, LANES - ng)).reshape(1, LANES)
    return wa, wqk, wv, wo, wg, gb


def _layer(x, mods, per_token, batch, seq, t_valid, lw, bias_p, bias_c, bias_n, kv_cache, conv0, state,
           after=None):
    (norm_mix, norm_ffn, w_in, conv_w, conv_b, gate_b, sinks, m_norm, w_out, peer_query, peer_keys,
     peer_u, peer_v) = lw
    if after is not None:
        x, _ = lax.optimization_barrier((x, after))
    sh1, sc1, g1, sh2, sc2, g2 = mods
    wa, wqk, wv, wo, wg, gb = _split_w_in(w_in, gate_b)
    qkv, qkm, vm, om, gc = _in_call(x, sc1, sh1, norm_mix.reshape(1, -1), wa, wqk, wv, wo, wg, gb,
                                    per_token, seq)
    if kv_cache is None:
        att = _attn_p_call(qkv, bias_p, sinks, batch, seq)
        kv3 = qkv.reshape(batch, seq, -1)
        new_k = kv3[:, seq - WINDOW:, A_WIDTH:A_WIDTH + A_KV_WIDTH]
        new_v = kv3[:, seq - WINDOW:, A_WIDTH + A_KV_WIDTH:]
        chunk = M_CHUNK
    else:
        att, new_k, new_v = _attn_s_call(qkv, kv_cache[0], kv_cache[1], bias_c, bias_n, sinks, t_valid)
        chunk = seq
    c0, n0, m0 = state
    mo, c_new, n_new, m_new = _mlstm_call(qkm, vm, om, gc, conv_w, conv_b.reshape(1, -1),
                                          m_norm.reshape(1, -1), conv0, c0, n0, m0,
                                          batch, seq, chunk, min(t_valid, chunk))
    new_conv = qkm.reshape(batch, seq, -1)[:, t_valid - (CONV_W - 1):t_valid]
    pad_rows = seq - t_valid
    if pad_rows:
        keep = lambda a: a.reshape(batch, seq, -1)[:, :t_valid].reshape(batch * t_valid, -1)
        att, mo, x, g1, sc2, sh2, g2 = (keep(a) for a in (att, mo, x, g1, sc2, sh2, g2))
    x_mid, xw, qp = _out_call(att, mo, x, g1, sc2, sh2, norm_ffn.reshape(1, -1),
                              w_out[:A_WIDTH].astype(BF16), w_out[A_WIDTH:].astype(BF16),
                              peer_query.astype(BF16), per_token, t_valid)
    idx, gw = _select_call(qp, peer_keys.astype(BF16))
    y = _sc_expert_call(idx, gw, xw, peer_u, peer_v)
    x_new = _resid_call(x_mid, y, g2, per_token, t_valid)
    if pad_rows:
        x_new = jnp.pad(x_new.reshape(batch, t_valid, -1), ((0, 0), (0, pad_rows), (0, 0))).reshape(batch * seq, -1)
    new_k = new_k.reshape(batch, WINDOW, A_KV_HEADS, A_HEAD_DIM)
    new_v = new_v.reshape(batch, WINDOW, A_KV_HEADS, A_HEAD_DIM)
    return x_new, (new_k, new_v, new_conv, c_new, n_new, m_new[:, :M_HEADS, 0]), idx


def _prompt_group_sizes(n_req):
    if n_req < PROMPT_GROUPS:
        return [1] * n_req
    mid, n_mid = n_req - 2, PROMPT_GROUPS - 2
    weights = [i + 2 for i in range(n_mid)]
    sizes = [max(1, mid * w // sum(weights)) for w in weights]
    for i in range(mid - sum(sizes)):
        sizes[n_mid - 1 - i % n_mid] += 1
    return [1] + sizes + [1]


def kernel(x_prompt, x_sample, c_prompt, c_sample, cache_k, cache_v, state_conv, state_C, state_n, state_m, rel_bias, w_ada, b_ada, norm_mix, norm_ffn, w_in, conv_w, conv_b, gate_b, attn_sinks, m_norm, w_out, peer_query, peer_keys, peer_u, peer_v, norm_final):
    depth = w_ada.shape[0]
    bp, tp, d = x_prompt.shape
    bs, ts, _ = x_sample.shape
    assert tp % WINDOW == 0 and tp % M_CHUNK == 0 and ts <= SAMPLE_PAD and ts >= CONV_W - 1

    mod_all = _ada_call(jnp.concatenate([c_prompt, c_sample], axis=0), w_ada, b_ada)

    qi = np.arange(WINDOW)[:, None]
    bias_p = _bias_call(rel_bias, qi + WINDOW - np.arange(2 * WINDOW)[None, :])
    qs = np.arange(SAMPLE_PAD)[:, None]
    bias_c = _bias_call(rel_bias, qs + WINDOW - np.arange(WINDOW)[None, :])
    bias_n = _bias_call(rel_bias, qs - np.arange(SAMPLE_PAD)[None, :])

    sizes = _prompt_group_sizes(bp)
    starts = np.cumsum([0] + sizes).tolist()
    xg = [x_prompt[starts[g]:starts[g + 1]].reshape(sizes[g] * tp, d) for g in range(len(sizes))]
    xs = jnp.pad(x_sample, ((0, 0), (0, SAMPLE_PAD - ts), (0, 0))).reshape(bs * SAMPLE_PAD, d)
    halo_pad = ((0, 0), (SUBLANES - (CONV_W - 1), 0), (0, 0))

    st_p, st_s = [], []
    for l in range(depth):
        lw = (norm_mix[l], norm_ffn[l], w_in[l], conv_w[l], conv_b[l], gate_b[l], attn_sinks[l], m_norm[l],
              w_out[l], peer_query[l], peer_keys[l], _pack_table(peer_u[l]), _pack_table(peer_v[l]))
        mod_s = [jnp.repeat(m, SAMPLE_PAD, axis=0) for m in jnp.split(mod_all[l, bp:], 6, axis=-1)]
        sp_groups = []
        for g, bg in enumerate(sizes):
            mod_g = [m.reshape(bg, 1, d) for m in jnp.split(mod_all[l, starts[g]:starts[g + 1]], 6, axis=-1)]
            zero_state = (jnp.zeros((bg, M_HEADS, M_HEAD_DIM, M_HEAD_DIM), F32),
                          jnp.zeros((bg, M_HEADS, M_HEAD_DIM), F32),
                          jnp.zeros((bg, SUBLANES, LANES), F32))
            zero_conv = jnp.zeros((bg, SUBLANES, 2 * M_WIDTH), F32)
            xg[g], sp, last_idx = _layer(xg[g], mod_g, False, bg, tp, tp, lw, bias_p, None, None, None,
                                         zero_conv, zero_state)
            sp_groups.append(sp)
        st_p.append([jnp.concatenate([sp[i] for sp in sp_groups], axis=0) for i in range(6)])
        state_s = (state_C[l].astype(F32), state_n[l].astype(F32),
                   jnp.broadcast_to(jnp.pad(state_m[l].astype(F32), ((0, 0), (0, SUBLANES - M_HEADS)))[:, :, None],
                                    (bs, SUBLANES, LANES)))
        kv_cache = (cache_k[l].reshape(bs, WINDOW, A_KV_WIDTH), cache_v[l].reshape(bs, WINDOW, A_KV_WIDTH))
        xs, ss, _ = _layer(xs, mod_s, True, bs, SAMPLE_PAD, ts, lw, None, bias_c, bias_n, kv_cache,
                           jnp.pad(state_conv[l].astype(F32), halo_pad), state_s, after=last_idx)
        st_s.append(ss)

    gfin = norm_final.reshape(1, d)
    y_prompt = jnp.concatenate([_final_call(x, gfin).reshape(bg, tp, d) for x, bg in zip(xg, sizes)], axis=0)
    y_sample = _final_call(xs, gfin).reshape(bs, SAMPLE_PAD, d)[:, :ts]
    outs_p = [jnp.stack([s[i] for s in st_p]) for i in range(6)]
    outs_s = [jnp.stack([s[i] for s in st_s]) for i in range(6)]
    return (y_prompt, y_sample, *outs_p, *outs_s)
```

```python
import functools
import math

import numpy as np
import jax
import jax.numpy as jnp
from jax import lax
from jax.experimental import pallas as pl
from jax.experimental.pallas import tpu as pltpu
from jax.experimental.pallas import tpu_sc as plsc

F32 = jnp.float32
BF16 = jnp.bfloat16
HIGHEST = lax.Precision.HIGHEST

D_MODEL = 1024
A_HEADS = 8
A_KV_HEADS = 2
A_GROUP = A_HEADS // A_KV_HEADS
A_HEAD_DIM = 64
A_WIDTH = A_HEADS * A_HEAD_DIM
A_KV_WIDTH = A_KV_HEADS * A_HEAD_DIM
WINDOW = 128
ATT_SCALE = A_HEAD_DIM ** -0.5
N_BUCKETS = 32
MAX_DISTANCE = WINDOW
M_HEADS = 4
M_HEAD_DIM = 128
M_WIDTH = M_HEADS * M_HEAD_DIM
CONV_W = 4
M_CHUNK = 64
N_KEYS = 128
P_HEADS = 8
P_TOPK = 16
P_KEY_DIM = 256
P_HALF = P_KEY_DIM // 2
EPS = 1e-6
NEG_INF = -1e30

LANES = 128
SUBLANES = 8
SAMPLE_PAD = SUBLANES
VMEM_LIMIT = 48 * 1024 * 1024
PROMPT_GROUPS = 6

NT_DIMS = (((1,), (1,)), ((), ()))
TN_DIMS = (((0,), (0,)), ((), ()))


def _cparams(*sem):
    return pltpu.CompilerParams(dimension_semantics=sem, vmem_limit_bytes=VMEM_LIMIT)


def _bdot(a, b):
    return jnp.dot(a.astype(BF16), b.astype(BF16), preferred_element_type=F32)


def _bdot_nt(a, b):
    return lax.dot_general(a.astype(BF16), b.astype(BF16), NT_DIMS, preferred_element_type=F32)


def _sigmoid(x):
    return 1.0 / (1.0 + jnp.exp(-x))


def _log_sigmoid(x):
    return jnp.minimum(x, 0.0) - jnp.log1p(jnp.exp(-jnp.abs(x)))


def _ada_kernel(c_ref, w_ref, b_ref, o_ref):
    c = c_ref[...]
    s = c * _sigmoid(c)
    o_ref[...] = jnp.dot(s, w_ref[...], precision=HIGHEST, preferred_element_type=F32) + b_ref[...]


def _ada_call(c_all, w_ada, b_ada):
    depth, d, n6 = w_ada.shape
    rows = c_all.shape[0]
    bn = 1024
    return pl.pallas_call(
        _ada_kernel,
        grid=(depth, n6 // bn),
        in_specs=[
            pl.BlockSpec((rows, d), lambda l, j: (0, 0)),
            pl.BlockSpec((None, d, bn), lambda l, j: (l, 0, j)),
            pl.BlockSpec((None, 1, bn), lambda l, j: (l, 0, j)),
        ],
        out_specs=pl.BlockSpec((None, rows, bn), lambda l, j: (l, 0, j)),
        out_shape=jax.ShapeDtypeStruct((depth, rows, n6), F32),
        compiler_params=_cparams("parallel", "parallel"),
        name="ada_mod",
    )(c_all, w_ada, b_ada.reshape(depth, 1, n6))


def _mod_spec(per_token, tm, tokens_per_req):
    if per_token:
        return pl.BlockSpec((tm, D_MODEL), lambda i: (i, 0))
    tiles = tokens_per_req // tm
    return pl.BlockSpec((None, 1, D_MODEL), lambda i: (i // tiles, 0, 0))


def _in_kernel(x_ref, sc_ref, sh_ref, g_ref, wa_ref, wqk_ref, wv_ref, wo_ref, wg_ref, gb_ref,
               qkv_ref, qkm_ref, v_ref, o_ref, gc_ref):
    x = x_ref[...]
    y = x * lax.rsqrt(jnp.mean(x * x, axis=-1, keepdims=True) + EPS) * g_ref[...]
    h = (y * (1.0 + sc_ref[...]) + sh_ref[...]).astype(BF16)
    qkv_ref[...] = jnp.dot(h, wa_ref[...], preferred_element_type=F32)
    qkm_ref[...] = jnp.dot(h, wqk_ref[...], preferred_element_type=F32)
    v_ref[...] = jnp.dot(h, wv_ref[...], preferred_element_type=F32)
    o_ref[...] = jnp.dot(h, wo_ref[...], preferred_element_type=F32)
    g = jnp.dot(h, wg_ref[...], preferred_element_type=F32) + gb_ref[...]
    lane = lax.broadcasted_iota(jnp.int32, g.shape, 1)
    gc_ref[...] = jnp.where(lane < M_HEADS, g, jnp.where(lane < 2 * M_HEADS, _log_sigmoid(g), 0.0))


def _in_call(x, sc, sh, gnorm, wa, wqk, wv, wo, wg, gb, per_token, tokens_per_req):
    n = x.shape[0]
    tm = min(512, n if per_token else tokens_per_req)
    mod = _mod_spec(per_token, tm, tokens_per_req)
    full = lambda shape: pl.BlockSpec(shape, lambda i: (0,) * len(shape))
    row = lambda w: pl.BlockSpec((tm, w), lambda i: (i, 0))
    return pl.pallas_call(
        _in_kernel,
        grid=(n // tm,),
        in_specs=[row(D_MODEL), mod, mod, full((1, D_MODEL)), full(wa.shape), full(wqk.shape),
                  full(wv.shape), full(wo.shape), full(wg.shape), full((1, LANES))],
        out_specs=[row(wa.shape[1]), row(wqk.shape[1]), row(wv.shape[1]), row(wo.shape[1]), row(LANES)],
        out_shape=[jax.ShapeDtypeStruct((n, w), F32)
                   for w in (wa.shape[1], wqk.shape[1], wv.shape[1], wo.shape[1], LANES)],
        compiler_params=_cparams("parallel"),
        name="in_proj",
    )(x, sc, sh, gnorm, wa, wqk, wv, wo, wg, gb)


def _t5_bucket_np(dist):
    n = np.maximum(dist, 0)
    max_exact = N_BUCKETS // 2
    nf = np.maximum(n, 1).astype(np.float64)
    large = max_exact + (np.log(nf / max_exact) / math.log(MAX_DISTANCE / max_exact)
                         * (N_BUCKETS - max_exact)).astype(np.int32)
    return np.where(n < max_exact, n, np.minimum(large, N_BUCKETS - 1)).astype(np.int32)


def _bias_kernel(bucket_ref, rel_ref, o_ref):
    bucket = bucket_ref[...]
    for h in range(A_HEADS):
        acc = jnp.zeros(bucket.shape, F32)
        for b in range(N_BUCKETS):
            acc = jnp.where(bucket == b, rel_ref[b, h], acc)
        o_ref[h] = acc


def _bias_call(rel_bias, dist):
    bucket = jnp.asarray(_t5_bucket_np(dist))
    nq, nk = dist.shape
    return pl.pallas_call(
        _bias_kernel,
        in_specs=[pl.BlockSpec((nq, nk), lambda: (0, 0)),
                  pl.BlockSpec(memory_space=pltpu.SMEM)],
        out_specs=pl.BlockSpec((A_HEADS, nq, nk), lambda: (0, 0, 0)),
        out_shape=jax.ShapeDtypeStruct((A_HEADS, nq, nk), F32),
        name="t5_bias",
    )(bucket, rel_bias)


def _attn_p_kernel(q_ref, kp_ref, kc_ref, vp_ref, vc_ref, bias_ref, sink_ref, o_ref):
    i = pl.program_id(1)
    qi = lax.broadcasted_iota(jnp.int32, (WINDOW, WINDOW), 0)
    kj = lax.broadcasted_iota(jnp.int32, (WINDOW, WINDOW), 1)
    valid_prev = jnp.logical_and(kj > qi, i > 0)
    valid_cur = kj <= qi
    q = q_ref[...]
    for h in range(A_HEADS):
        kv = h // A_GROUP
        qh = q[:, h * A_HEAD_DIM:(h + 1) * A_HEAD_DIM]
        sl = slice(kv * A_HEAD_DIM, (kv + 1) * A_HEAD_DIM)
        bias = bias_ref[h]
        sp = _bdot_nt(qh, kp_ref[:, sl]) * ATT_SCALE + bias[:, :WINDOW]
        sc = _bdot_nt(qh, kc_ref[:, sl]) * ATT_SCALE + bias[:, WINDOW:]
        sp = jnp.where(valid_prev, sp, NEG_INF)
        sc = jnp.where(valid_cur, sc, NEG_INF)
        sink = sink_ref[0, h]
        mx = jnp.maximum(jnp.maximum(jnp.max(sp, axis=-1, keepdims=True),
                                     jnp.max(sc, axis=-1, keepdims=True)), sink)
        ep = jnp.exp(sp - mx)
        ec = jnp.exp(sc - mx)
        den = (jnp.sum(ep, axis=-1, keepdims=True) + jnp.sum(ec, axis=-1, keepdims=True)
               + jnp.exp(sink - mx))
        o = _bdot(ep / den, vp_ref[:, sl]) + _bdot(ec / den, vc_ref[:, sl])
        o_ref[:, h * A_HEAD_DIM:(h + 1) * A_HEAD_DIM] = o


def _attn_p_call(qkv, bias, sinks, batch, seq):
    nb = seq // WINDOW
    n = batch * seq
    kcol = A_WIDTH // A_KV_WIDTH
    vcol = kcol + 1
    cur = lambda col: pl.BlockSpec((WINDOW, A_KV_WIDTH), lambda b, i: (b * nb + i, col))
    prev = lambda col: pl.BlockSpec((WINDOW, A_KV_WIDTH),
                                    lambda b, i: (b * nb + jnp.maximum(i - 1, 0), col))
    return pl.pallas_call(
        _attn_p_kernel,
        grid=(batch, nb),
        in_specs=[pl.BlockSpec((WINDOW, A_WIDTH), lambda b, i: (b * nb + i, 0)),
                  prev(kcol), cur(kcol), prev(vcol), cur(vcol),
                  pl.BlockSpec((A_HEADS, WINDOW, 2 * WINDOW), lambda b, i: (0, 0, 0)),
                  pl.BlockSpec(memory_space=pltpu.SMEM)],
        out_specs=pl.BlockSpec((WINDOW, A_WIDTH), lambda b, i: (b * nb + i, 0)),
        out_shape=jax.ShapeDtypeStruct((n, A_WIDTH), F32),
        compiler_params=_cparams("parallel", "parallel"),
        name="swa_prompt",
    )(qkv, qkv, qkv, qkv, qkv, bias, sinks.reshape(1, A_HEADS))


def _attn_s_kernel(n_new, qkv_ref, ck_ref, cv_ref, bc_ref, bn_ref, sink_ref,
                   o_ref, nk_ref, nv_ref, kk_s, vv_s):
    qkv = qkv_ref[...]
    knew = qkv[:, A_WIDTH:A_WIDTH + A_KV_WIDTH]
    vnew = qkv[:, A_WIDTH + A_KV_WIDTH:A_WIDTH + 2 * A_KV_WIDTH]
    ck = ck_ref[...]
    cv = cv_ref[...]
    rows = A_GROUP * SAMPLE_PAD
    qi = lax.broadcasted_iota(jnp.int32, (rows, WINDOW), 0) % SAMPLE_PAD
    kj = lax.broadcasted_iota(jnp.int32, (rows, WINDOW), 1)
    valid_c = kj > qi
    rcol = lax.broadcasted_iota(jnp.int32, (rows, 1), 0)
    qcol = rcol % SAMPLE_PAD
    for kv in range(A_KV_HEADS):
        heads = range(kv * A_GROUP, (kv + 1) * A_GROUP)
        sl = slice(kv * A_HEAD_DIM, (kv + 1) * A_HEAD_DIM)
        qs = jnp.concatenate([qkv[:, h * A_HEAD_DIM:(h + 1) * A_HEAD_DIM] for h in heads], axis=0)
        bias_c = jnp.concatenate([bc_ref[h] for h in heads], axis=0)
        bias_n = jnp.concatenate([bn_ref[h] for h in heads], axis=0)
        sink = jnp.zeros((rows, 1), F32)
        for g, h in enumerate(heads):
            sink = jnp.where(rcol // SAMPLE_PAD == g, sink_ref[0, h], sink)
        s_c = _bdot_nt(qs, ck[:, sl]) * ATT_SCALE + bias_c
        s_c = jnp.where(valid_c, s_c, NEG_INF)
        s_n = []
        for j in range(n_new):
            sj = jnp.sum(qs * knew[j:j + 1, sl], axis=-1, keepdims=True) * ATT_SCALE + bias_n[:, j:j + 1]
            s_n.append(jnp.where(qcol >= j, sj, NEG_INF))
        mx = jnp.maximum(jnp.max(s_c, axis=-1, keepdims=True), sink)
        for sj in s_n:
            mx = jnp.maximum(mx, sj)
        e_c = jnp.exp(s_c - mx)
        den = jnp.sum(e_c, axis=-1, keepdims=True) + jnp.exp(sink - mx)
        o = _bdot(e_c, cv[:, sl])
        for j, sj in enumerate(s_n):
            ej = jnp.exp(sj - mx)
            den = den + ej
            o = o + ej * vnew[j:j + 1, sl]
        o = o / den
        for g, h in enumerate(heads):
            o_ref[:, h * A_HEAD_DIM:(h + 1) * A_HEAD_DIM] = o[g * SAMPLE_PAD:(g + 1) * SAMPLE_PAD, :]
    kk_s[0:WINDOW, :] = ck
    kk_s[WINDOW:WINDOW + SAMPLE_PAD, :] = knew
    vv_s[0:WINDOW, :] = cv
    vv_s[WINDOW:WINDOW + SAMPLE_PAD, :] = vnew
    nk_ref[...] = kk_s[n_new:n_new + WINDOW, :]
    nv_ref[...] = vv_s[n_new:n_new + WINDOW, :]


def _attn_s_call(qkv, ck, cv, bias_c, bias_n, sinks, n_new):
    nreq = ck.shape[0]
    wq = qkv.shape[1]
    full3 = lambda shape: pl.BlockSpec(shape, lambda b: (0, 0, 0))
    cache = pl.BlockSpec((None, WINDOW, A_KV_WIDTH), lambda b: (b, 0, 0))
    return pl.pallas_call(
        functools.partial(_attn_s_kernel, n_new),
        grid=(nreq,),
        in_specs=[pl.BlockSpec((SAMPLE_PAD, wq), lambda b: (b, 0)), cache, cache,
                  full3(bias_c.shape), full3(bias_n.shape),
                  pl.BlockSpec(memory_space=pltpu.SMEM)],
        out_specs=[pl.BlockSpec((SAMPLE_PAD, A_WIDTH), lambda b: (b, 0)), cache, cache],
        out_shape=[jax.ShapeDtypeStruct((nreq * SAMPLE_PAD, A_WIDTH), F32),
                   jax.ShapeDtypeStruct(ck.shape, F32), jax.ShapeDtypeStruct(cv.shape, F32)],
        scratch_shapes=[pltpu.VMEM((WINDOW + SAMPLE_PAD, A_KV_WIDTH), F32),
                        pltpu.VMEM((WINDOW + SAMPLE_PAD, A_KV_WIDTH), F32)],
        compiler_params=_cparams("parallel"),
        name="swa_sample",
    )(qkv, ck, cv, bias_c, bias_n, sinks.reshape(1, A_HEADS))


def _mlstm_kernel(chunk, t_valid, qk_ref, v_ref, og_ref, gc_ref, cw_ref, cb_ref, mn_ref,
                  conv0_ref, c0_ref, n0_ref, m0_ref,
                  out_ref, cout_ref, nout_ref, mout_ref,
                  xp_s, c_s, n_s, m_s):
    step = pl.program_id(1)
    halo = SUBLANES

    @pl.when(step == 0)
    def _():
        xp_s[0:halo, :] = conv0_ref[...]
        c_s[...] = c0_ref[...]
        n_s[...] = n0_ref[...]
        m_s[...] = m0_ref[...]

    xp_s[halo:halo + chunk, :] = qk_ref[...]
    cw = cw_ref[...]
    y = cb_ref[...]
    for i in range(CONV_W):
        off = halo - (CONV_W - 1) + i
        y = y + xp_s[off:off + chunk, :] * cw[i:i + 1, :]
    xp_s[0:halo, :] = xp_s[chunk:chunk + halo, :]
    y = y * _sigmoid(y)
    q_all = y[:, :M_WIDTH]
    k_all = y[:, M_WIDTH:] * (M_HEAD_DIM ** -0.5)

    g = gc_ref[...]
    if t_valid < chunk:
        row = lax.broadcasted_iota(jnp.int32, g.shape, 0)
        lane = lax.broadcasted_iota(jnp.int32, g.shape, 1)
        g = jnp.where(row < t_valid, g, jnp.where(lane < M_HEADS, NEG_INF, 0.0))
    tr = lax.broadcasted_iota(jnp.int32, (chunk, chunk), 0)
    tc = lax.broadcasted_iota(jnp.int32, (chunk, chunk), 1)
    causal = tr >= tc
    tri = causal.astype(F32)
    bcol = jnp.dot(tri, g, precision=HIGHEST, preferred_element_type=F32)
    er = lax.broadcasted_iota(jnp.int32, (SUBLANES, LANES), 0)
    ec = lax.broadcasted_iota(jnp.int32, (SUBLANES, LANES), 1)
    eye = (er == ec).astype(F32)
    g_rows = lax.dot_general(eye, g, NT_DIMS, precision=HIGHEST, preferred_element_type=F32)
    b_rows = lax.dot_general(eye, bcol, NT_DIMS, precision=HIGHEST, preferred_element_type=F32)

    for h in range(M_HEADS):
        hs = slice(h * M_HEAD_DIM, (h + 1) * M_HEAD_DIM)
        b_c = bcol[:, M_HEADS + h:M_HEADS + h + 1]
        ig_c = g[:, h:h + 1]
        b_r = b_rows[M_HEADS + h:M_HEADS + h + 1, :]
        ig_r = g_rows[h:h + 1, :]
        m_prev = m_s[h:h + 1, 0:1]
        logw = jnp.where(causal, b_c - b_r + ig_r, -jnp.inf)
        inter = b_c + m_prev
        m_t = jnp.maximum(inter, jnp.max(logw, axis=-1, keepdims=True))
        w = jnp.exp(logw - m_t)
        a = jnp.exp(inter - m_t)
        q = q_all[:, hs]
        k = k_all[:, hs]
        v = v_ref[:, hs]
        cmat = c_s[h]
        nvec = n_s[h:h + 1, :]
        wqk = w * _bdot_nt(q, k)
        num = _bdot(wqk, v) + a * _bdot_nt(q, cmat)
        den = jnp.sum(wqk, axis=-1, keepdims=True) + a * jnp.sum(q * nvec, axis=-1, keepdims=True)
        hh = num / jnp.maximum(jnp.abs(den), jnp.exp(-m_t))
        m_new = m_t[chunk - 1:chunk, :]
        b_last = b_c[chunk - 1:chunk, :]
        wl = jnp.exp(b_last - b_c + ig_c - m_new)
        al = jnp.exp(b_last + m_prev - m_new)
        c_s[h] = al * cmat + lax.dot_general((v * wl).astype(BF16), k.astype(BF16), TN_DIMS,
                                             preferred_element_type=F32)
        n_s[h:h + 1, :] = al * nvec + jnp.sum(wl * k, axis=0, keepdims=True)
        m_s[h:h + 1, :] = jnp.broadcast_to(m_new, (1, LANES))
        hn = hh * lax.rsqrt(jnp.mean(hh * hh, axis=-1, keepdims=True) + EPS) * mn_ref[:, hs]
        out_ref[:, hs] = _sigmoid(og_ref[:, hs]) * hn

    @pl.when(step == pl.num_programs(1) - 1)
    def _():
        cout_ref[...] = c_s[...]
        nout_ref[...] = n_s[...]
        mout_ref[...] = m_s[...]


def _mlstm_call(qk, v, og, gc, conv_w, conv_b, m_norm, conv0, c0, n0, m0, batch, seq, chunk, t_valid):
    nc = seq // chunk
    n = batch * seq
    row = lambda w: pl.BlockSpec((chunk, w), lambda b, c: (b * nc + c, 0))
    full2 = lambda shape: pl.BlockSpec(shape, lambda b, c: (0, 0))
    per_b = lambda shape: pl.BlockSpec((None,) + shape, lambda b, c: (b,) + (0,) * len(shape))
    dh = M_HEAD_DIM
    return pl.pallas_call(
        functools.partial(_mlstm_kernel, chunk, t_valid),
        grid=(batch, nc),
        in_specs=[row(2 * M_WIDTH), row(M_WIDTH), row(M_WIDTH), row(LANES),
                  full2((CONV_W, 2 * M_WIDTH)), full2((1, 2 * M_WIDTH)), full2((1, M_WIDTH)),
                  per_b((SUBLANES, 2 * M_WIDTH)), per_b((M_HEADS, dh, dh)), per_b((M_HEADS, dh)),
                  per_b((SUBLANES, LANES))],
        out_specs=[row(M_WIDTH), per_b((M_HEADS, dh, dh)), per_b((M_HEADS, dh)), per_b((SUBLANES, LANES))],
        out_shape=[jax.ShapeDtypeStruct((n, M_WIDTH), F32),
                   jax.ShapeDtypeStruct((batch, M_HEADS, dh, dh), F32),
                   jax.ShapeDtypeStruct((batch, M_HEADS, dh), F32),
                   jax.ShapeDtypeStruct((batch, SUBLANES, LANES), F32)],
        scratch_shapes=[pltpu.VMEM((SUBLANES + chunk, 2 * M_WIDTH), F32),
                        pltpu.VMEM((M_HEADS, dh, dh), F32),
                        pltpu.VMEM((M_HEADS, dh), F32),
                        pltpu.VMEM((SUBLANES, LANES), F32)],
        compiler_params=_cparams("parallel", "arbitrary"),
        name="mlstm",
    )(qk, v, og, gc, conv_w, conv_b, m_norm, conv0, c0, n0, m0)


def _pack_words(x):
    bits = pltpu.bitcast(x.astype(BF16).astype(F32), jnp.uint32)
    half = x.shape[1] // 2
    lo = lax.shift_right_logical(bits[:, :half], jnp.uint32(16))
    hi = bits[:, half:] & jnp.uint32(0xFFFF0000)
    return pltpu.bitcast(lo | hi, jnp.int32)


def _out_kernel(att_ref, mo_ref, x_ref, g1_ref, sc_ref, sh_ref, gn_ref, wa_ref, wm_ref, wq_ref,
                xo_ref, h2_ref, qp_ref):
    mix = (jnp.dot(att_ref[...].astype(BF16), wa_ref[...], preferred_element_type=F32)
           + jnp.dot(mo_ref[...].astype(BF16), wm_ref[...], preferred_element_type=F32))
    x = x_ref[...] + g1_ref[...] * mix
    xo_ref[...] = x
    y = x * lax.rsqrt(jnp.mean(x * x, axis=-1, keepdims=True) + EPS) * gn_ref[...]
    h2 = y * (1.0 + sc_ref[...]) + sh_ref[...]
    qp_ref[...] = jnp.dot(h2.astype(BF16), wq_ref[...], preferred_element_type=F32)
    h2_ref[...] = _pack_words(h2)


def _out_call(att, mo, x, g1, sc, sh, gnorm, wa, wm, wq, per_token, tokens_per_req):
    n = x.shape[0]
    tm = min(256, n if per_token else tokens_per_req)
    mod = _mod_spec(per_token, tm, tokens_per_req)
    full = lambda shape: pl.BlockSpec(shape, lambda i: (0,) * len(shape))
    row = lambda w: pl.BlockSpec((tm, w), lambda i: (i, 0))
    nq = wq.shape[1]
    return pl.pallas_call(
        _out_kernel,
        grid=(n // tm,),
        in_specs=[row(A_WIDTH), row(M_WIDTH), row(D_MODEL), mod, mod, mod, full((1, D_MODEL)),
                  full(wa.shape), full(wm.shape), full(wq.shape)],
        out_specs=[row(D_MODEL), row(D_MODEL // 2), row(nq)],
        out_shape=[jax.ShapeDtypeStruct((n, D_MODEL), F32), jax.ShapeDtypeStruct((n, D_MODEL // 2), jnp.int32),
                   jax.ShapeDtypeStruct((n, nq), F32)],
        compiler_params=_cparams("parallel"),
        name="out_proj",
    )(att, mo, x, g1, sc, sh, gnorm, wa, wm, wq)


def _pk_cells():
    return [(a, b) for a in range(P_TOPK) for b in range(P_TOPK) if (a + 1) * (b + 1) <= P_TOPK]


PK_CELL_ROWS = 64


def _pk_expand_mats():
    cells = _pk_cells()
    e0 = np.zeros((PK_CELL_ROWS, LANES), np.float32)
    e1 = np.zeros((PK_CELL_ROWS, LANES), np.float32)
    for j, (a, b) in enumerate(cells):
        e0[j, a] = 1.0
        e1[j, b] = 1.0
    return e0, e1, len(cells)


def _top_rows(s, rowf, rounds):
    n_rows = s.shape[0]
    vals, idxs = [], []
    for _ in range(rounds):
        m = jnp.max(s, axis=0, keepdims=True)
        i = jnp.min(jnp.where(s == m, rowf, float(n_rows)), axis=0, keepdims=True)
        vals.append(m)
        idxs.append(i)
        s = jnp.where(rowf == i, -jnp.inf, s)
    return jnp.concatenate(vals, axis=0), jnp.concatenate(idxs, axis=0)


def _select_kernel(n_cells, qp_ref, keys_ref, e0_ref, e1_ref, idx_ref, gw_ref, idx_s, gw_s):
    tm = qp_ref.shape[0]
    keyf = lax.broadcasted_iota(jnp.int32, (N_KEYS, tm), 0).astype(F32)
    cellf = lax.broadcasted_iota(jnp.int32, (PK_CELL_ROWS, tm), 0).astype(F32)
    e0 = e0_ref[...]
    e1 = e1_ref[...]
    pad = jnp.zeros((LANES - P_TOPK, tm), F32)

    def head(h, carry):
        h = jnp.asarray(h, jnp.int32)
        sub = []
        for c in range(2):
            col = pl.multiple_of((h * 2 + c) * P_HALF, P_HALF)
            s = _bdot_nt(keys_ref[h, c], qp_ref[:, pl.ds(col, P_HALF)])
            sub.append(_top_rows(s, keyf, P_TOPK))
        (v0, i0), (v1, i1) = sub
        expand = lambda e, x: jnp.dot(e, jnp.concatenate([x, pad], axis=0), precision=HIGHEST,
                                      preferred_element_type=F32)
        cand = expand(e0, v0) + expand(e1, v1)
        cidx = expand(e0, i0 * float(N_KEYS)) + expand(e1, i1)
        cand = jnp.where(cellf < n_cells, cand, -jnp.inf)
        best, eidx = [], []
        for _ in range(P_TOPK):
            m = jnp.max(cand, axis=0, keepdims=True)
            j = jnp.min(jnp.where(cand == m, cellf, float(PK_CELL_ROWS)), axis=0, keepdims=True)
            hit = cellf == j
            eidx.append(jnp.max(jnp.where(hit, cidx, -1.0), axis=0, keepdims=True))
            best.append(m)
            cand = jnp.where(hit, -jnp.inf, cand)
        best = jnp.concatenate(best, axis=0)
        e = jnp.exp(best - best[0:1, :])
        row0 = pl.multiple_of(h * P_TOPK, P_TOPK)
        gw_s[pl.ds(row0, P_TOPK), :] = e / jnp.sum(e, axis=0, keepdims=True)
        idx_s[pl.ds(row0, P_TOPK), :] = jnp.concatenate(eidx, axis=0)
        return carry

    lax.fori_loop(0, P_HEADS, head, 0)
    idx_ref[...] = idx_s[...].T.astype(jnp.int32)
    gw_ref[...] = gw_s[...].T


def _select_call(qp, keys_bf16):
    n = qp.shape[0]
    tm = min(LANES, n)
    e0, e1, n_cells = _pk_expand_mats()
    full = lambda shape: pl.BlockSpec(shape, lambda i: (0,) * len(shape))
    return pl.pallas_call(
        functools.partial(_select_kernel, n_cells),
        grid=(n // tm,),
        in_specs=[pl.BlockSpec((tm, qp.shape[1]), lambda i: (i, 0)), full(keys_bf16.shape),
                  full((PK_CELL_ROWS, LANES)), full((PK_CELL_ROWS, LANES))],
        out_specs=[pl.BlockSpec((tm, LANES), lambda i: (i, 0)), pl.BlockSpec((tm, LANES), lambda i: (i, 0))],
        out_shape=[jax.ShapeDtypeStruct((n, LANES), jnp.int32), jax.ShapeDtypeStruct((n, LANES), F32)],
        scratch_shapes=[pltpu.VMEM((P_HEADS * P_TOPK, tm), F32), pltpu.VMEM((P_HEADS * P_TOPK, tm), F32)],
        compiler_params=_cparams("parallel"),
        name="peer_select",
    )(qp, keys_bf16, jnp.asarray(e0), jnp.asarray(e1))


N_SEL = P_HEADS * P_TOPK

SC_CORES = 2
SC_SUBCORES = 16
SC_LANES = 16
SC_TOK_BLOCK = 8
SC_ROWS = 16
SC_ROW_BLOCK = 16
SC_ACC_CHAINS = 4
SC_ACC_ROWS = 16
SC_NBUF = 8
N_WCOL = D_MODEL // (2 * SC_LANES)


def _pack_kernel(t_ref, o_ref):
    o_ref[...] = _pack_words(t_ref[...])


def _pack_table(t):
    e, dcol = t.shape
    tm = min(1024, e)
    return pl.pallas_call(
        _pack_kernel,
        grid=(e // tm,),
        in_specs=[pl.BlockSpec((tm, dcol), lambda i: (i, 0))],
        out_specs=pl.BlockSpec((tm, dcol // 2), lambda i: (i, 0)),
        out_shape=jax.ShapeDtypeStruct((e, dcol // 2), jnp.int32),
        compiler_params=_cparams("parallel"),
        name="pack_table",
    )(t)


def _sc_gelu(x):
    z = math.sqrt(2.0 / math.pi) * (x + 0.044715 * (x * x * x))
    t = 1.0 - 2.0 / (jnp.exp(2.0 * z) + 1.0)
    return x * (0.5 * (1.0 + t))


def _sc_expert_body(tokens_per_worker, idx_hbm, gw_hbm, h2_hbm, u_hbm, v_hbm, y_hbm,
                    idx_v, gw_v, x_v, o_v, buf, coef_v, tr_v, sem, in_sem, out_sem):
    wid = lax.axis_index("s") * SC_CORES + lax.axis_index("c")
    base = wid * tokens_per_worker
    lane = lax.iota(jnp.int32, SC_LANES)
    n_gather = N_SEL // SC_ROWS
    zero = jnp.zeros((SC_LANES,), F32)

    n_steps = 2 * n_gather
    assert n_steps % SC_NBUF == 0

    def gather(p, tt, i):
        table = u_hbm if i < n_gather else v_hbm
        j = i % n_gather
        slot = i % SC_NBUF
        return pltpu.make_async_copy(table.at[idx_v.at[p, tt, pl.ds(j * SC_ROWS, SC_ROWS)]], buf.at[slot],
                                     sem.at[slot])

    def unpack(w):
        lo = lax.bitcast_convert_type(lax.shift_left(w, jnp.full((SC_LANES,), 16, jnp.int32)), F32)
        hi = lax.bitcast_convert_type(w & jnp.full((SC_LANES,), -65536, jnp.int32), F32)
        return lo, hi

    def packed(w):
        return plsc.bitcast(w, BF16)

    def unpack_sum(s):
        return unpack(plsc.bitcast(s, jnp.int32))

    def act_chunk(p, tt, j, slot):
        @pl.loop(0, SC_ROWS // SC_LANES)
        def _(half):
            for rb in range(SC_LANES // SC_ROW_BLOCK):
                r0 = half * SC_LANES + rb * SC_ROW_BLOCK

                def col(c, accs):
                    w0 = pl.multiple_of(c * (2 * SC_LANES), 2 * SC_LANES)
                    xa = packed(x_v[p, tt, pl.ds(w0, SC_LANES)])
                    xb = packed(x_v[p, tt, pl.ds(w0 + SC_LANES, SC_LANES)])
                    out = []
                    for r, a in enumerate(accs):
                        ua = packed(buf[slot, r0 + r, pl.ds(w0, SC_LANES)])
                        ub = packed(buf[slot, r0 + r, pl.ds(w0 + SC_LANES, SC_LANES)])
                        lo, hi = unpack_sum(ua * xa + ub * xb)
                        out.append(a + lo + hi)
                    return tuple(out)

                accs = lax.fori_loop(0, N_WCOL // 2, col, (zero,) * SC_ROW_BLOCK)
                for r in range(SC_ROW_BLOCK):
                    tr_v[pl.ds((rb * SC_ROW_BLOCK + r) * SC_LANES, SC_LANES)] = accs[r]
            tot = zero
            for jj in range(SC_LANES):
                tot = tot + plsc.load_gather(tr_v, [lane * SC_LANES + jj])
            k0 = pl.multiple_of(j * SC_ROWS + half * SC_LANES, SC_LANES)
            coef_v[pl.ds(k0, SC_LANES)] = gw_v[p, tt, pl.ds(k0, SC_LANES)] * _sc_gelu(tot)

    def acc_chunk(p, tt, j, slot, first):
        def tree_sum(parts):
            while len(parts) > 1:
                parts = [parts[i] + parts[i + 1] for i in range(0, len(parts), 2)]
            return parts[0]

        for rb in range(SC_ROWS // SC_ACC_ROWS):
            rows = list(range(rb * SC_ACC_ROWS, (rb + 1) * SC_ACC_ROWS))
            splat = {}
            for r in rows:
                c16 = plsc.load_gather(coef_v, [jnp.full((SC_LANES,), j * SC_ROWS + r, jnp.int32)])
                splat[r] = plsc.pack(c16, c16, format=plsc.PackFormat.INTERLEAVED)
            fresh = first and rb == 0

            @plsc.parallel_loop(0, N_WCOL)
            def _(c):
                woff = pl.multiple_of(c * SC_LANES, SC_LANES)
                hoff = pl.multiple_of(c * SC_LANES + D_MODEL // 2, SC_LANES)
                pa, pb = [], []
                for n in range(0, SC_ACC_ROWS, 2):
                    r0, r1 = rows[n], rows[n + 1]
                    s = (splat[r0] * packed(buf[slot, r0, pl.ds(woff, SC_LANES)])
                         + splat[r1] * packed(buf[slot, r1, pl.ds(woff, SC_LANES)]))
                    lo, hi = unpack_sum(s)
                    if n // 2 < SC_ACC_CHAINS:
                        pa.append(lo)
                        pb.append(hi)
                    else:
                        pa[(n // 2) % SC_ACC_CHAINS] = pa[(n // 2) % SC_ACC_CHAINS] + lo
                        pb[(n // 2) % SC_ACC_CHAINS] = pb[(n // 2) % SC_ACC_CHAINS] + hi
                sa, sb = tree_sum(pa), tree_sum(pb)
                if not fresh:
                    sa = sa + o_v[p, tt, pl.ds(woff, SC_LANES)]
                    sb = sb + o_v[p, tt, pl.ds(hoff, SC_LANES)]
                o_v[p, tt, pl.ds(woff, SC_LANES)] = sa
                o_v[p, tt, pl.ds(hoff, SC_LANES)] = sb

    n_blocks = tokens_per_worker // SC_TOK_BLOCK

    def block_start(blk):
        return pl.multiple_of(base + blk * SC_TOK_BLOCK, SC_TOK_BLOCK)

    def in_copies(blk, p):
        rows = pl.ds(block_start(blk), SC_TOK_BLOCK)
        return [pltpu.make_async_copy(idx_hbm.at[rows], idx_v.at[p], in_sem.at[p]),
                pltpu.make_async_copy(gw_hbm.at[rows], gw_v.at[p], in_sem.at[p]),
                pltpu.make_async_copy(h2_hbm.at[rows], x_v.at[p], in_sem.at[p])]

    def out_copy(blk, p):
        return pltpu.make_async_copy(o_v.at[p], y_hbm.at[pl.ds(block_start(blk), SC_TOK_BLOCK)], out_sem.at[p])

    for cp in in_copies(0, 0):
        cp.start()

    @pl.loop(0, n_blocks)
    def _(blk):
        p = lax.rem(blk, 2)
        for cp in in_copies(blk, p):
            cp.wait()

        @pl.when(blk + 1 < n_blocks)
        def _():
            for cp in in_copies(blk + 1, 1 - p):
                cp.start()

        @pl.when(blk >= 2)
        def _():
            out_copy(blk - 2, p).wait()

        ahead = SC_NBUF - 1
        for i in range(ahead):
            gather(p, 0, i).start()

        @pl.loop(0, SC_TOK_BLOCK)
        def _(tt):
            for i in range(n_steps):
                if i + ahead < n_steps:
                    gather(p, tt, i + ahead).start()
                else:
                    @pl.when(tt + 1 < SC_TOK_BLOCK)
                    def _():
                        gather(p, tt + 1, i + ahead - n_steps).start()
                gather(p, tt, i).wait()
                if i < n_gather:
                    act_chunk(p, tt, i, i % SC_NBUF)
                else:
                    acc_chunk(p, tt, i - n_gather, i % SC_NBUF, i == n_gather)

        out_copy(blk, p).start()

    for blk in (n_blocks - 2, n_blocks - 1):
        out_copy(blk, blk % 2).wait()


def _sc_expert_call(idx, gw, xw, u, v):
    n = idx.shape[0]
    workers = SC_CORES * SC_SUBCORES
    assert n % (workers * SC_TOK_BLOCK) == 0 and n // (workers * SC_TOK_BLOCK) >= 2
    mesh = plsc.VectorSubcoreMesh(core_axis_name="c", subcore_axis_name="s")
    return pl.kernel(
        functools.partial(_sc_expert_body, n // workers),
        out_type=jax.ShapeDtypeStruct((n, D_MODEL), F32),
        mesh=mesh,
        scratch_types=[pltpu.VMEM((2, SC_TOK_BLOCK, N_SEL), jnp.int32),
                       pltpu.VMEM((2, SC_TOK_BLOCK, N_SEL), F32),
                       pltpu.VMEM((2, SC_TOK_BLOCK, D_MODEL // 2), jnp.int32),
                       pltpu.VMEM((2, SC_TOK_BLOCK, D_MODEL), F32),
                       pltpu.VMEM((SC_NBUF, SC_ROWS, D_MODEL // 2), jnp.int32),
                       pltpu.VMEM((N_SEL,), F32),
                       pltpu.VMEM((SC_LANES * SC_LANES,), F32),
                       pltpu.SemaphoreType.DMA((SC_NBUF,)), pltpu.SemaphoreType.DMA((2,)),
                       pltpu.SemaphoreType.DMA((2,))],
        compiler_params=pltpu.CompilerParams(needs_layout_passes=False),
        name="peer_experts_sc",
    )(idx, gw, xw, u, v)


def _resid_kernel(x_ref, y_ref, g_ref, o_ref):
    o_ref[...] = x_ref[...] + g_ref[...] * y_ref[...]


def _resid_call(x, y, g2, per_token, tokens_per_req):
    n = x.shape[0]
    tm = min(512, n if per_token else tokens_per_req)
    row = pl.BlockSpec((tm, D_MODEL), lambda i: (i, 0))
    return pl.pallas_call(
        _resid_kernel,
        grid=(n // tm,),
        in_specs=[row, row, _mod_spec(per_token, tm, tokens_per_req)],
        out_specs=row,
        out_shape=jax.ShapeDtypeStruct((n, D_MODEL), F32),
        compiler_params=_cparams("parallel"),
        name="peer_residual",
    )(x, y, g2)


def _final_kernel(x_ref, g_ref, o_ref):
    x = x_ref[...]
    o_ref[...] = x * lax.rsqrt(jnp.mean(x * x, axis=-1, keepdims=True) + EPS) * g_ref[...]


def _final_call(x, g):
    n = x.shape[0]
    tm = min(512, n)
    return pl.pallas_call(
        _final_kernel,
        grid=(n // tm,),
        in_specs=[pl.BlockSpec((tm, D_MODEL), lambda i: (i, 0)), pl.BlockSpec((1, D_MODEL), lambda i: (0, 0))],
        out_specs=pl.BlockSpec((tm, D_MODEL), lambda i: (i, 0)),
        out_shape=jax.ShapeDtypeStruct((n, D_MODEL), F32),
        compiler_params=_cparams("parallel"),
        name="final_norm",
    )(x, g)


def _split_w_in(w_in_l, gate_b_l):
    cuts = np.cumsum([A_WIDTH + 2 * A_KV_WIDTH, 2 * M_WIDTH, M_WIDTH, M_WIDTH]).tolist()
    wa = w_in_l[:, :cuts[0]].astype(BF16)
    wqk = w_in_l[:, cuts[0]:cuts[1]].astype(BF16)
    wv = w_in_l[:, cuts[1]:cuts[2]].astype(BF16)
    wo = w_in_l[:, cuts[2]:cuts[3]].astype(BF16)
    ng = 2 * M_HEADS
    wg = jnp.pad(w_in_l[:, cuts[3]:], ((0, 0), (0, LANES - ng))).astype(BF16)
    gb = jnp.pad(gate_b_l.astype(F32), (0, LANES - ng)).reshape(1, LANES)
    return wa, wqk, wv, wo, wg, gb


def _layer(x, mods, per_token, batch, seq, t_valid, lw, bias_p, bias_c, bias_n, kv_cache, conv0, state,
           after=None):
    (norm_mix, norm_ffn, w_in, conv_w, conv_b, gate_b, sinks, m_norm, w_out, peer_query, peer_keys,
     peer_u, peer_v) = lw
    if after is not None:
        x, _ = lax.optimization_barrier((x, after))
    sh1, sc1, g1, sh2, sc2, g2 = mods
    wa, wqk, wv, wo, wg, gb = _split_w_in(w_in, gate_b)
    qkv, qkm, vm, om, gc = _in_call(x, sc1, sh1, norm_mix.reshape(1, -1), wa, wqk, wv, wo, wg, gb,
                                    per_token, seq)
    if kv_cache is None:
        att = _attn_p_call(qkv, bias_p, sinks, batch, seq)
        kv3 = qkv.reshape(batch, seq, -1)
        new_k = kv3[:, seq - WINDOW:, A_WIDTH:A_WIDTH + A_KV_WIDTH]
        new_v = kv3[:, seq - WINDOW:, A_WIDTH + A_KV_WIDTH:]
        chunk = M_CHUNK
    else:
        att, new_k, new_v = _attn_s_call(qkv, kv_cache[0], kv_cache[1], bias_c, bias_n, sinks, t_valid)
        chunk = seq
    c0, n0, m0 = state
    mo, c_new, n_new, m_new = _mlstm_call(qkm, vm, om, gc, conv_w, conv_b.reshape(1, -1),
                                          m_norm.reshape(1, -1), conv0, c0, n0, m0,
                                          batch, seq, chunk, min(t_valid, chunk))
    new_conv = qkm.reshape(batch, seq, -1)[:, t_valid - (CONV_W - 1):t_valid]
    pad_rows = seq - t_valid
    if pad_rows:
        keep = lambda a: a.reshape(batch, seq, -1)[:, :t_valid].reshape(batch * t_valid, -1)
        att, mo, x, g1, sc2, sh2, g2 = (keep(a) for a in (att, mo, x, g1, sc2, sh2, g2))
    x_mid, xw, qp = _out_call(att, mo, x, g1, sc2, sh2, norm_ffn.reshape(1, -1),
                              w_out[:A_WIDTH].astype(BF16), w_out[A_WIDTH:].astype(BF16),
                              peer_query.astype(BF16), per_token, t_valid)
    idx, gw = _select_call(qp, peer_keys.astype(BF16))
    y = _sc_expert_call(idx, gw, xw, peer_u, peer_v)
    x_new = _resid_call(x_mid, y, g2, per_token, t_valid)
    if pad_rows:
        x_new = jnp.pad(x_new.reshape(batch, t_valid, -1), ((0, 0), (0, pad_rows), (0, 0))).reshape(batch * seq, -1)
    new_k = new_k.reshape(batch, WINDOW, A_KV_HEADS, A_HEAD_DIM)
    new_v = new_v.reshape(batch, WINDOW, A_KV_HEADS, A_HEAD_DIM)
    return x_new, (new_k, new_v, new_conv, c_new, n_new, m_new[:, :M_HEADS, 0]), idx


def _prompt_group_sizes(n_req):
    if n_req < PROMPT_GROUPS:
        return [1] * n_req
    mid, n_mid = n_req - 2, PROMPT_GROUPS - 2
    weights = [i + 2 for i in range(n_mid)]
    sizes = [max(1, mid * w // sum(weights)) for w in weights]
    for i in range(mid - sum(sizes)):
        sizes[n_mid - 1 - i % n_mid] += 1
    return [1] + sizes + [1]


def kernel(x_prompt, x_sample, c_prompt, c_sample, cache_k, cache_v, state_conv, state_C, state_n, state_m, rel_bias, w_ada, b_ada, norm_mix, norm_ffn, w_in, conv_w, conv_b, gate_b, attn_sinks, m_norm, w_out, peer_query, peer_keys, peer_u, peer_v, norm_final):
    depth = w_ada.shape[0]
    bp, tp, d = x_prompt.shape
    bs, ts, _ = x_sample.shape
    assert tp % WINDOW == 0 and tp % M_CHUNK == 0 and ts <= SAMPLE_PAD and ts >= CONV_W - 1

    mod_all = _ada_call(jnp.concatenate([c_prompt, c_sample], axis=0), w_ada, b_ada)

    qi = np.arange(WINDOW)[:, None]
    bias_p = _bias_call(rel_bias, qi + WINDOW - np.arange(2 * WINDOW)[None, :])
    qs = np.arange(SAMPLE_PAD)[:, None]
    bias_c = _bias_call(rel_bias, qs + WINDOW - np.arange(WINDOW)[None, :])
    bias_n = _bias_call(rel_bias, qs - np.arange(SAMPLE_PAD)[None, :])

    sizes = _prompt_group_sizes(bp)
    starts = np.cumsum([0] + sizes).tolist()
    xg = [x_prompt[starts[g]:starts[g + 1]].reshape(sizes[g] * tp, d) for g in range(len(sizes))]
    xs = jnp.pad(x_sample, ((0, 0), (0, SAMPLE_PAD - ts), (0, 0))).reshape(bs * SAMPLE_PAD, d)
    halo_pad = ((0, 0), (SUBLANES - (CONV_W - 1), 0), (0, 0))

    st_p, st_s = [], []
    for l in range(depth):
        lw = (norm_mix[l], norm_ffn[l], w_in[l], conv_w[l], conv_b[l], gate_b[l], attn_sinks[l], m_norm[l],
              w_out[l], peer_query[l], peer_keys[l], _pack_table(peer_u[l]), _pack_table(peer_v[l]))
        mod_s = [jnp.repeat(m, SAMPLE_PAD, axis=0) for m in jnp.split(mod_all[l, bp:], 6, axis=-1)]
        sp_groups = []
        for g, bg in enumerate(sizes):
            mod_g = [m.reshape(bg, 1, d) for m in jnp.split(mod_all[l, starts[g]:starts[g + 1]], 6, axis=-1)]
            zero_state = (jnp.zeros((bg, M_HEADS, M_HEAD_DIM, M_HEAD_DIM), F32),
                          jnp.zeros((bg, M_HEADS, M_HEAD_DIM), F32),
                          jnp.zeros((bg, SUBLANES, LANES), F32))
            zero_conv = jnp.zeros((bg, SUBLANES, 2 * M_WIDTH), F32)
            xg[g], sp, last_idx = _layer(xg[g], mod_g, False, bg, tp, tp, lw, bias_p, None, None, None,
                                         zero_conv, zero_state)
            sp_groups.append(sp)
        st_p.append([jnp.concatenate([sp[i] for sp in sp_groups], axis=0) for i in range(6)])
        state_s = (state_C[l].astype(F32), state_n[l].astype(F32),
                   jnp.broadcast_to(jnp.pad(state_m[l].astype(F32), ((0, 0), (0, SUBLANES - M_HEADS)))[:, :, None],
                                    (bs, SUBLANES, LANES)))
        kv_cache = (cache_k[l].reshape(bs, WINDOW, A_KV_WIDTH), cache_v[l].reshape(bs, WINDOW, A_KV_WIDTH))
        xs, ss, _ = _layer(xs, mod_s, True, bs, SAMPLE_PAD, ts, lw, None, bias_c, bias_n, kv_cache,
                           jnp.pad(state_conv[l].astype(F32), halo_pad), state_s, after=last_idx)
        st_s.append(ss)

    gfin = norm_final.reshape(1, d)
    y_prompt = jnp.concatenate([_final_call(x, gfin).reshape(bg, tp, d) for x, bg in zip(xg, sizes)], axis=0)
    y_sample = _final_call(xs, gfin).reshape(bs, SAMPLE_PAD, d)[:, :ts]
    outs_p = [jnp.stack([s[i] for s in st_p]) for i in range(6)]
    outs_s = [jnp.stack([s[i] for s in st_s]) for i in range(6)]
    return (y_prompt, y_sample, *outs_p, *outs_s)
```

```python
import functools
import math

import numpy as np
import jax
import jax.numpy as jnp
from jax import lax
from jax.experimental import pallas as pl
from jax.experimental.pallas import tpu as pltpu
from jax.experimental.pallas import tpu_sc as plsc

F32 = jnp.float32
BF16 = jnp.bfloat16
HIGHEST = lax.Precision.HIGHEST

D_MODEL = 1024
A_HEADS = 8
A_KV_HEADS = 2
A_GROUP = A_HEADS // A_KV_HEADS
A_HEAD_DIM = 64
A_WIDTH = A_HEADS * A_HEAD_DIM
A_KV_WIDTH = A_KV_HEADS * A_HEAD_DIM
WINDOW = 128
ATT_SCALE = A_HEAD_DIM ** -0.5
N_BUCKETS = 32
MAX_DISTANCE = WINDOW
M_HEADS = 4
M_HEAD_DIM = 128
M_WIDTH = M_HEADS * M_HEAD_DIM
CONV_W = 4
M_CHUNK = 64
N_KEYS = 128
P_HEADS = 8
P_TOPK = 16
P_KEY_DIM = 256
P_HALF = P_KEY_DIM // 2
EPS = 1e-6
NEG_INF = -1e30

LANES = 128
SUBLANES = 8
SAMPLE_PAD = SUBLANES
VMEM_LIMIT = 48 * 1024 * 1024
PROMPT_GROUPS = 6

NT_DIMS = (((1,), (1,)), ((), ()))
TN_DIMS = (((0,), (0,)), ((), ()))


def _cparams(*sem):
    return pltpu.CompilerParams(dimension_semantics=sem, vmem_limit_bytes=VMEM_LIMIT)


def _bdot(a, b):
    return jnp.dot(a.astype(BF16), b.astype(BF16), preferred_element_type=F32)


def _bdot_nt(a, b):
    return lax.dot_general(a.astype(BF16), b.astype(BF16), NT_DIMS, preferred_element_type=F32)


def _sigmoid(x):
    return 1.0 / (1.0 + jnp.exp(-x))


def _log_sigmoid(x):
    return jnp.minimum(x, 0.0) - jnp.log1p(jnp.exp(-jnp.abs(x)))


def _ada_kernel(c_ref, w_ref, b_ref, o_ref):
    c = c_ref[...]
    s = c * _sigmoid(c)
    o_ref[...] = jnp.dot(s, w_ref[...], precision=HIGHEST, preferred_element_type=F32) + b_ref[...]


def _ada_call(c_all, w_ada, b_ada):
    depth, d, n6 = w_ada.shape
    rows = c_all.shape[0]
    bn = 1024
    return pl.pallas_call(
        _ada_kernel,
        grid=(depth, n6 // bn),
        in_specs=[
            pl.BlockSpec((rows, d), lambda l, j: (0, 0)),
            pl.BlockSpec((None, d, bn), lambda l, j: (l, 0, j)),
            pl.BlockSpec((None, 1, bn), lambda l, j: (l, 0, j)),
        ],
        out_specs=pl.BlockSpec((None, rows, bn), lambda l, j: (l, 0, j)),
        out_shape=jax.ShapeDtypeStruct((depth, rows, n6), F32),
        compiler_params=_cparams("parallel", "parallel"),
        name="ada_mod",
    )(c_all, w_ada, b_ada.reshape(depth, 1, n6))


def _mod_spec(per_token, tm, tokens_per_req):
    if per_token:
        return pl.BlockSpec((tm, D_MODEL), lambda i: (i, 0))
    tiles = tokens_per_req // tm
    return pl.BlockSpec((None, 1, D_MODEL), lambda i: (i // tiles, 0, 0))


def _in_kernel(x_ref, sc_ref, sh_ref, g_ref, wa_ref, wqk_ref, wv_ref, wo_ref, wg_ref, gb_ref,
               qkv_ref, qkm_ref, v_ref, o_ref, gc_ref):
    x = x_ref[...]
    y = x * lax.rsqrt(jnp.mean(x * x, axis=-1, keepdims=True) + EPS) * g_ref[...]
    h = (y * (1.0 + sc_ref[...]) + sh_ref[...]).astype(BF16)
    qkv_ref[...] = jnp.dot(h, wa_ref[...], preferred_element_type=F32)
    qkm_ref[...] = jnp.dot(h, wqk_ref[...], preferred_element_type=F32)
    v_ref[...] = jnp.dot(h, wv_ref[...], preferred_element_type=F32)
    o_ref[...] = jnp.dot(h, wo_ref[...], preferred_element_type=F32)
    g = jnp.dot(h, wg_ref[...], preferred_element_type=F32) + gb_ref[...]
    lane = lax.broadcasted_iota(jnp.int32, g.shape, 1)
    gc_ref[...] = jnp.where(lane < M_HEADS, g, jnp.where(lane < 2 * M_HEADS, _log_sigmoid(g), 0.0))


def _in_call(x, sc, sh, gnorm, wa, wqk, wv, wo, wg, gb, per_token, tokens_per_req):
    n = x.shape[0]
    tm = min(512, n if per_token else tokens_per_req)
    mod = _mod_spec(per_token, tm, tokens_per_req)
    full = lambda shape: pl.BlockSpec(shape, lambda i: (0,) * len(shape))
    row = lambda w: pl.BlockSpec((tm, w), lambda i: (i, 0))
    return pl.pallas_call(
        _in_kernel,
        grid=(n // tm,),
        in_specs=[row(D_MODEL), mod, mod, full((1, D_MODEL)), full(wa.shape), full(wqk.shape),
                  full(wv.shape), full(wo.shape), full(wg.shape), full((1, LANES))],
        out_specs=[row(wa.shape[1]), row(wqk.shape[1]), row(wv.shape[1]), row(wo.shape[1]), row(LANES)],
        out_shape=[jax.ShapeDtypeStruct((n, w), F32)
                   for w in (wa.shape[1], wqk.shape[1], wv.shape[1], wo.shape[1], LANES)],
        compiler_params=_cparams("parallel"),
        name="in_proj",
    )(x, sc, sh, gnorm, wa, wqk, wv, wo, wg, gb)


def _t5_bucket_np(dist):
    n = np.maximum(dist, 0)
    max_exact = N_BUCKETS // 2
    nf = np.maximum(n, 1).astype(np.float64)
    large = max_exact + (np.log(nf / max_exact) / math.log(MAX_DISTANCE / max_exact)
                         * (N_BUCKETS - max_exact)).astype(np.int32)
    return np.where(n < max_exact, n, np.minimum(large, N_BUCKETS - 1)).astype(np.int32)


def _bias_kernel(bucket_ref, rel_ref, o_ref):
    bucket = bucket_ref[...]
    for h in range(A_HEADS):
        acc = jnp.zeros(bucket.shape, F32)
        for b in range(N_BUCKETS):
            acc = jnp.where(bucket == b, rel_ref[b, h], acc)
        o_ref[h] = acc


def _bias_call(rel_bias, dist):
    bucket = jnp.asarray(_t5_bucket_np(dist))
    nq, nk = dist.shape
    return pl.pallas_call(
        _bias_kernel,
        in_specs=[pl.BlockSpec((nq, nk), lambda: (0, 0)),
                  pl.BlockSpec(memory_space=pltpu.SMEM)],
        out_specs=pl.BlockSpec((A_HEADS, nq, nk), lambda: (0, 0, 0)),
        out_shape=jax.ShapeDtypeStruct((A_HEADS, nq, nk), F32),
        name="t5_bias",
    )(bucket, rel_bias)


def _attn_p_kernel(q_ref, kp_ref, kc_ref, vp_ref, vc_ref, bias_ref, sink_ref, o_ref):
    i = pl.program_id(1)
    qi = lax.broadcasted_iota(jnp.int32, (WINDOW, WINDOW), 0)
    kj = lax.broadcasted_iota(jnp.int32, (WINDOW, WINDOW), 1)
    valid_prev = jnp.logical_and(kj > qi, i > 0)
    valid_cur = kj <= qi
    q = q_ref[...]
    for h in range(A_HEADS):
        kv = h // A_GROUP
        qh = q[:, h * A_HEAD_DIM:(h + 1) * A_HEAD_DIM]
        sl = slice(kv * A_HEAD_DIM, (kv + 1) * A_HEAD_DIM)
        bias = bias_ref[h]
        sp = _bdot_nt(qh, kp_ref[:, sl]) * ATT_SCALE + bias[:, :WINDOW]
        sc = _bdot_nt(qh, kc_ref[:, sl]) * ATT_SCALE + bias[:, WINDOW:]
        sp = jnp.where(valid_prev, sp, NEG_INF)
        sc = jnp.where(valid_cur, sc, NEG_INF)
        sink = sink_ref[0, h]
        mx = jnp.maximum(jnp.maximum(jnp.max(sp, axis=-1, keepdims=True),
                                     jnp.max(sc, axis=-1, keepdims=True)), sink)
        ep = jnp.exp(sp - mx)
        ec = jnp.exp(sc - mx)
        den = (jnp.sum(ep, axis=-1, keepdims=True) + jnp.sum(ec, axis=-1, keepdims=True)
               + jnp.exp(sink - mx))
        o = _bdot(ep / den, vp_ref[:, sl]) + _bdot(ec / den, vc_ref[:, sl])
        o_ref[:, h * A_HEAD_DIM:(h + 1) * A_HEAD_DIM] = o


def _attn_p_call(qkv, bias, sinks, batch, seq):
    nb = seq // WINDOW
    n = batch * seq
    kcol = A_WIDTH // A_KV_WIDTH
    vcol = kcol + 1
    cur = lambda col: pl.BlockSpec((WINDOW, A_KV_WIDTH), lambda b, i: (b * nb + i, col))
    prev = lambda col: pl.BlockSpec((WINDOW, A_KV_WIDTH),
                                    lambda b, i: (b * nb + jnp.maximum(i - 1, 0), col))
    return pl.pallas_call(
        _attn_p_kernel,
        grid=(batch, nb),
        in_specs=[pl.BlockSpec((WINDOW, A_WIDTH), lambda b, i: (b * nb + i, 0)),
                  prev(kcol), cur(kcol), prev(vcol), cur(vcol),
                  pl.BlockSpec((A_HEADS, WINDOW, 2 * WINDOW), lambda b, i: (0, 0, 0)),
                  pl.BlockSpec(memory_space=pltpu.SMEM)],
        out_specs=pl.BlockSpec((WINDOW, A_WIDTH), lambda b, i: (b * nb + i, 0)),
        out_shape=jax.ShapeDtypeStruct((n, A_WIDTH), F32),
        compiler_params=_cparams("parallel", "parallel"),
        name="swa_prompt",
    )(qkv, qkv, qkv, qkv, qkv, bias, sinks.reshape(1, A_HEADS))


def _attn_s_kernel(n_new, qkv_ref, ck_ref, cv_ref, bc_ref, bn_ref, sink_ref,
                   o_ref, nk_ref, nv_ref, kk_s, vv_s):
    qkv = qkv_ref[...]
    knew = qkv[:, A_WIDTH:A_WIDTH + A_KV_WIDTH]
    vnew = qkv[:, A_WIDTH + A_KV_WIDTH:A_WIDTH + 2 * A_KV_WIDTH]
    ck = ck_ref[...]
    cv = cv_ref[...]
    rows = A_GROUP * SAMPLE_PAD
    qi = lax.broadcasted_iota(jnp.int32, (rows, WINDOW), 0) % SAMPLE_PAD
    kj = lax.broadcasted_iota(jnp.int32, (rows, WINDOW), 1)
    valid_c = kj > qi
    rcol = lax.broadcasted_iota(jnp.int32, (rows, 1), 0)
    qcol = rcol % SAMPLE_PAD
    for kv in range(A_KV_HEADS):
        heads = range(kv * A_GROUP, (kv + 1) * A_GROUP)
        sl = slice(kv * A_HEAD_DIM, (kv + 1) * A_HEAD_DIM)
        qs = jnp.concatenate([qkv[:, h * A_HEAD_DIM:(h + 1) * A_HEAD_DIM] for h in heads], axis=0)
        bias_c = jnp.concatenate([bc_ref[h] for h in heads], axis=0)
        bias_n = jnp.concatenate([bn_ref[h] for h in heads], axis=0)
        sink = jnp.zeros((rows, 1), F32)
        for g, h in enumerate(heads):
            sink = jnp.where(rcol // SAMPLE_PAD == g, sink_ref[0, h], sink)
        s_c = _bdot_nt(qs, ck[:, sl]) * ATT_SCALE + bias_c
        s_c = jnp.where(valid_c, s_c, NEG_INF)
        s_n = []
        for j in range(n_new):
            sj = jnp.sum(qs * knew[j:j + 1, sl], axis=-1, keepdims=True) * ATT_SCALE + bias_n[:, j:j + 1]
            s_n.append(jnp.where(qcol >= j, sj, NEG_INF))
        mx = jnp.maximum(jnp.max(s_c, axis=-1, keepdims=True), sink)
        for sj in s_n:
            mx = jnp.maximum(mx, sj)
        e_c = jnp.exp(s_c - mx)
        den = jnp.sum(e_c, axis=-1, keepdims=True) + jnp.exp(sink - mx)
        o = _bdot(e_c, cv[:, sl])
        for j, sj in enumerate(s_n):
            ej = jnp.exp(sj - mx)
            den = den + ej
            o = o + ej * vnew[j:j + 1, sl]
        o = o / den
        for g, h in enumerate(heads):
            o_ref[:, h * A_HEAD_DIM:(h + 1) * A_HEAD_DIM] = o[g * SAMPLE_PAD:(g + 1) * SAMPLE_PAD, :]
    kk_s[0:WINDOW, :] = ck
    kk_s[WINDOW:WINDOW + SAMPLE_PAD, :] = knew
    vv_s[0:WINDOW, :] = cv
    vv_s[WINDOW:WINDOW + SAMPLE_PAD, :] = vnew
    nk_ref[...] = kk_s[n_new:n_new + WINDOW, :]
    nv_ref[...] = vv_s[n_new:n_new + WINDOW, :]


def _attn_s_call(qkv, ck, cv, bias_c, bias_n, sinks, n_new):
    nreq = ck.shape[0]
    wq = qkv.shape[1]
    full3 = lambda shape: pl.BlockSpec(shape, lambda b: (0, 0, 0))
    cache = pl.BlockSpec((None, WINDOW, A_KV_WIDTH), lambda b: (b, 0, 0))
    return pl.pallas_call(
        functools.partial(_attn_s_kernel, n_new),
        grid=(nreq,),
        in_specs=[pl.BlockSpec((SAMPLE_PAD, wq), lambda b: (b, 0)), cache, cache,
                  full3(bias_c.shape), full3(bias_n.shape),
                  pl.BlockSpec(memory_space=pltpu.SMEM)],
        out_specs=[pl.BlockSpec((SAMPLE_PAD, A_WIDTH), lambda b: (b, 0)), cache, cache],
        out_shape=[jax.ShapeDtypeStruct((nreq * SAMPLE_PAD, A_WIDTH), F32),
                   jax.ShapeDtypeStruct(ck.shape, F32), jax.ShapeDtypeStruct(cv.shape, F32)],
        scratch_shapes=[pltpu.VMEM((WINDOW + SAMPLE_PAD, A_KV_WIDTH), F32),
                        pltpu.VMEM((WINDOW + SAMPLE_PAD, A_KV_WIDTH), F32)],
        compiler_params=_cparams("parallel"),
        name="swa_sample",
    )(qkv, ck, cv, bias_c, bias_n, sinks.reshape(1, A_HEADS))


def _mlstm_kernel(chunk, t_valid, qk_ref, v_ref, og_ref, gc_ref, cw_ref, cb_ref, mn_ref,
                  conv0_ref, c0_ref, n0_ref, m0_ref,
                  out_ref, cout_ref, nout_ref, mout_ref,
                  xp_s, c_s, n_s, m_s):
    step = pl.program_id(1)
    halo = SUBLANES

    @pl.when(step == 0)
    def _():
        xp_s[0:halo, :] = conv0_ref[...]
        c_s[...] = c0_ref[...]
        n_s[...] = n0_ref[...]
        m_s[...] = m0_ref[...]

    xp_s[halo:halo + chunk, :] = qk_ref[...]
    cw = cw_ref[...]
    y = cb_ref[...]
    for i in range(CONV_W):
        off = halo - (CONV_W - 1) + i
        y = y + xp_s[off:off + chunk, :] * cw[i:i + 1, :]
    xp_s[0:halo, :] = xp_s[chunk:chunk + halo, :]
    y = y * _sigmoid(y)
    q_all = y[:, :M_WIDTH]
    k_all = y[:, M_WIDTH:] * (M_HEAD_DIM ** -0.5)

    g = gc_ref[...]
    if t_valid < chunk:
        row = lax.broadcasted_iota(jnp.int32, g.shape, 0)
        lane = lax.broadcasted_iota(jnp.int32, g.shape, 1)
        g = jnp.where(row < t_valid, g, jnp.where(lane < M_HEADS, NEG_INF, 0.0))
    tr = lax.broadcasted_iota(jnp.int32, (chunk, chunk), 0)
    tc = lax.broadcasted_iota(jnp.int32, (chunk, chunk), 1)
    causal = tr >= tc
    tri = causal.astype(F32)
    bcol = jnp.dot(tri, g, precision=HIGHEST, preferred_element_type=F32)
    er = lax.broadcasted_iota(jnp.int32, (SUBLANES, LANES), 0)
    ec = lax.broadcasted_iota(jnp.int32, (SUBLANES, LANES), 1)
    eye = (er == ec).astype(F32)
    g_rows = lax.dot_general(eye, g, NT_DIMS, precision=HIGHEST, preferred_element_type=F32)
    b_rows = lax.dot_general(eye, bcol, NT_DIMS, precision=HIGHEST, preferred_element_type=F32)

    for h in range(M_HEADS):
        hs = slice(h * M_HEAD_DIM, (h + 1) * M_HEAD_DIM)
        b_c = bcol[:, M_HEADS + h:M_HEADS + h + 1]
        ig_c = g[:, h:h + 1]
        b_r = b_rows[M_HEADS + h:M_HEADS + h + 1, :]
        ig_r = g_rows[h:h + 1, :]
        m_prev = m_s[h:h + 1, 0:1]
        logw = jnp.where(causal, b_c - b_r + ig_r, -jnp.inf)
        inter = b_c + m_prev
        m_t = jnp.maximum(inter, jnp.max(logw, axis=-1, keepdims=True))
        w = jnp.exp(logw - m_t)
        a = jnp.exp(inter - m_t)
        q = q_all[:, hs]
        k = k_all[:, hs]
        v = v_ref[:, hs]
        cmat = c_s[h]
        nvec = n_s[h:h + 1, :]
        wqk = w * _bdot_nt(q, k)
        num = _bdot(wqk, v) + a * _bdot_nt(q, cmat)
        den = jnp.sum(wqk, axis=-1, keepdims=True) + a * jnp.sum(q * nvec, axis=-1, keepdims=True)
        hh = num / jnp.maximum(jnp.abs(den), jnp.exp(-m_t))
        m_new = m_t[chunk - 1:chunk, :]
        b_last = b_c[chunk - 1:chunk, :]
        wl = jnp.exp(b_last - b_c + ig_c - m_new)
        al = jnp.exp(b_last + m_prev - m_new)
        c_s[h] = al * cmat + lax.dot_general((v * wl).astype(BF16), k.astype(BF16), TN_DIMS,
                                             preferred_element_type=F32)
        n_s[h:h + 1, :] = al * nvec + jnp.sum(wl * k, axis=0, keepdims=True)
        m_s[h:h + 1, :] = jnp.broadcast_to(m_new, (1, LANES))
        hn = hh * lax.rsqrt(jnp.mean(hh * hh, axis=-1, keepdims=True) + EPS) * mn_ref[:, hs]
        out_ref[:, hs] = _sigmoid(og_ref[:, hs]) * hn

    @pl.when(step == pl.num_programs(1) - 1)
    def _():
        cout_ref[...] = c_s[...]
        nout_ref[...] = n_s[...]
        mout_ref[...] = m_s[...]


def _mlstm_call(qk, v, og, gc, conv_w, conv_b, m_norm, conv0, c0, n0, m0, batch, seq, chunk, t_valid):
    nc = seq // chunk
    n = batch * seq
    row = lambda w: pl.BlockSpec((chunk, w), lambda b, c: (b * nc + c, 0))
    full2 = lambda shape: pl.BlockSpec(shape, lambda b, c: (0, 0))
    per_b = lambda shape: pl.BlockSpec((None,) + shape, lambda b, c: (b,) + (0,) * len(shape))
    dh = M_HEAD_DIM
    return pl.pallas_call(
        functools.partial(_mlstm_kernel, chunk, t_valid),
        grid=(batch, nc),
        in_specs=[row(2 * M_WIDTH), row(M_WIDTH), row(M_WIDTH), row(LANES),
                  full2((CONV_W, 2 * M_WIDTH)), full2((1, 2 * M_WIDTH)), full2((1, M_WIDTH)),
                  per_b((SUBLANES, 2 * M_WIDTH)), per_b((M_HEADS, dh, dh)), per_b((M_HEADS, dh)),
                  per_b((SUBLANES, LANES))],
        out_specs=[row(M_WIDTH), per_b((M_HEADS, dh, dh)), per_b((M_HEADS, dh)), per_b((SUBLANES, LANES))],
        out_shape=[jax.ShapeDtypeStruct((n, M_WIDTH), F32),
                   jax.ShapeDtypeStruct((batch, M_HEADS, dh, dh), F32),
                   jax.ShapeDtypeStruct((batch, M_HEADS, dh), F32),
                   jax.ShapeDtypeStruct((batch, SUBLANES, LANES), F32)],
        scratch_shapes=[pltpu.VMEM((SUBLANES + chunk, 2 * M_WIDTH), F32),
                        pltpu.VMEM((M_HEADS, dh, dh), F32),
                        pltpu.VMEM((M_HEADS, dh), F32),
                        pltpu.VMEM((SUBLANES, LANES), F32)],
        compiler_params=_cparams("parallel", "arbitrary"),
        name="mlstm",
    )(qk, v, og, gc, conv_w, conv_b, m_norm, conv0, c0, n0, m0)


def _pack_words(x):
    bits = pltpu.bitcast(x.astype(BF16).astype(F32), jnp.uint32)
    half = x.shape[1] // 2
    lo = lax.shift_right_logical(bits[:, :half], jnp.uint32(16))
    hi = bits[:, half:] & jnp.uint32(0xFFFF0000)
    return pltpu.bitcast(lo | hi, jnp.int32)


def _out_kernel(att_ref, mo_ref, x_ref, g1_ref, sc_ref, sh_ref, gn_ref, wa_ref, wm_ref, wq_ref,
                xo_ref, h2_ref, qp_ref):
    mix = (jnp.dot(att_ref[...].astype(BF16), wa_ref[...], preferred_element_type=F32)
           + jnp.dot(mo_ref[...].astype(BF16), wm_ref[...], preferred_element_type=F32))
    x = x_ref[...] + g1_ref[...] * mix
    xo_ref[...] = x
    y = x * lax.rsqrt(jnp.mean(x * x, axis=-1, keepdims=True) + EPS) * gn_ref[...]
    h2 = y * (1.0 + sc_ref[...]) + sh_ref[...]
    qp_ref[...] = jnp.dot(h2.astype(BF16), wq_ref[...], preferred_element_type=F32)
    h2_ref[...] = _pack_words(h2)


def _out_call(att, mo, x, g1, sc, sh, gnorm, wa, wm, wq, per_token, tokens_per_req):
    n = x.shape[0]
    tm = min(256, n if per_token else tokens_per_req)
    mod = _mod_spec(per_token, tm, tokens_per_req)
    full = lambda shape: pl.BlockSpec(shape, lambda i: (0,) * len(shape))
    row = lambda w: pl.BlockSpec((tm, w), lambda i: (i, 0))
    nq = wq.shape[1]
    return pl.pallas_call(
        _out_kernel,
        grid=(n // tm,),
        in_specs=[row(A_WIDTH), row(M_WIDTH), row(D_MODEL), mod, mod, mod, full((1, D_MODEL)),
                  full(wa.shape), full(wm.shape), full(wq.shape)],
        out_specs=[row(D_MODEL), row(D_MODEL // 2), row(nq)],
        out_shape=[jax.ShapeDtypeStruct((n, D_MODEL), F32), jax.ShapeDtypeStruct((n, D_MODEL // 2), jnp.int32),
                   jax.ShapeDtypeStruct((n, nq), F32)],
        compiler_params=_cparams("parallel"),
        name="out_proj",
    )(att, mo, x, g1, sc, sh, gnorm, wa, wm, wq)


def _pk_cells():
    return [(a, b) for a in range(P_TOPK) for b in range(P_TOPK) if (a + 1) * (b + 1) <= P_TOPK]


PK_CELL_ROWS = 64


def _pk_expand_mats():
    cells = _pk_cells()
    e0 = np.zeros((PK_CELL_ROWS, LANES), np.float32)
    e1 = np.zeros((PK_CELL_ROWS, LANES), np.float32)
    for j, (a, b) in enumerate(cells):
        e0[j, a] = 1.0
        e1[j, b] = 1.0
    return e0, e1, len(cells)


def _top_rows(s, rowf, rounds):
    n_rows = s.shape[0]
    vals, idxs = [], []
    for _ in range(rounds):
        m = jnp.max(s, axis=0, keepdims=True)
        i = jnp.min(jnp.where(s == m, rowf, float(n_rows)), axis=0, keepdims=True)
        vals.append(m)
        idxs.append(i)
        s = jnp.where(rowf == i, -jnp.inf, s)
    return jnp.concatenate(vals, axis=0), jnp.concatenate(idxs, axis=0)


def _select_kernel(n_cells, qp_ref, keys_ref, e0_ref, e1_ref, idx_ref, gw_ref, idx_s, gw_s):
    tm = qp_ref.shape[0]
    keyf = lax.broadcasted_iota(jnp.int32, (N_KEYS, tm), 0).astype(F32)
    cellf = lax.broadcasted_iota(jnp.int32, (PK_CELL_ROWS, tm), 0).astype(F32)
    e0 = e0_ref[...]
    e1 = e1_ref[...]
    pad = jnp.zeros((LANES - P_TOPK, tm), F32)

    def head(h, carry):
        h = jnp.asarray(h, jnp.int32)
        sub = []
        for c in range(2):
            col = pl.multiple_of((h * 2 + c) * P_HALF, P_HALF)
            s = _bdot_nt(keys_ref[h, c], qp_ref[:, pl.ds(col, P_HALF)])
            sub.append(_top_rows(s, keyf, P_TOPK))
        (v0, i0), (v1, i1) = sub
        expand = lambda e, x: jnp.dot(e, jnp.concatenate([x, pad], axis=0), precision=HIGHEST,
                                      preferred_element_type=F32)
        cand = expand(e0, v0) + expand(e1, v1)
        cidx = expand(e0, i0 * float(N_KEYS)) + expand(e1, i1)
        cand = jnp.where(cellf < n_cells, cand, -jnp.inf)
        best, eidx = [], []
        for _ in range(P_TOPK):
            m = jnp.max(cand, axis=0, keepdims=True)
            j = jnp.min(jnp.where(cand == m, cellf, float(PK_CELL_ROWS)), axis=0, keepdims=True)
            hit = cellf == j
            eidx.append(jnp.max(jnp.where(hit, cidx, -1.0), axis=0, keepdims=True))
            best.append(m)
            cand = jnp.where(hit, -jnp.inf, cand)
        best = jnp.concatenate(best, axis=0)
        e = jnp.exp(best - best[0:1, :])
        row0 = pl.multiple_of(h * P_TOPK, P_TOPK)
        gw_s[pl.ds(row0, P_TOPK), :] = e / jnp.sum(e, axis=0, keepdims=True)
        idx_s[pl.ds(row0, P_TOPK), :] = jnp.concatenate(eidx, axis=0)
        return carry

    lax.fori_loop(0, P_HEADS, head, 0)
    idx_ref[...] = idx_s[...].T.astype(jnp.int32)
    gw_ref[...] = gw_s[...].T


def _select_call(qp, keys_bf16):
    n = qp.shape[0]
    tm = min(LANES, n)
    e0, e1, n_cells = _pk_expand_mats()
    full = lambda shape: pl.BlockSpec(shape, lambda i: (0,) * len(shape))
    return pl.pallas_call(
        functools.partial(_select_kernel, n_cells),
        grid=(n // tm,),
        in_specs=[pl.BlockSpec((tm, qp.shape[1]), lambda i: (i, 0)), full(keys_bf16.shape),
                  full((PK_CELL_ROWS, LANES)), full((PK_CELL_ROWS, LANES))],
        out_specs=[pl.BlockSpec((tm, LANES), lambda i: (i, 0)), pl.BlockSpec((tm, LANES), lambda i: (i, 0))],
        out_shape=[jax.ShapeDtypeStruct((n, LANES), jnp.int32), jax.ShapeDtypeStruct((n, LANES), F32)],
        scratch_shapes=[pltpu.VMEM((P_HEADS * P_TOPK, tm), F32), pltpu.VMEM((P_HEADS * P_TOPK, tm), F32)],
        compiler_params=_cparams("parallel"),
        name="peer_select",
    )(qp, keys_bf16, jnp.asarray(e0), jnp.asarray(e1))


N_SEL = P_HEADS * P_TOPK

SC_CORES = 2
SC_SUBCORES = 16
SC_LANES = 16
SC_TOK_BLOCK = 8
SC_ROWS = 32
SC_ROW_BLOCK = 16
SC_ACC_CHAINS = 4
SC_ACC_ROWS = 16
SC_NBUF = 4
N_WCOL = D_MODEL // (2 * SC_LANES)


def _pack_kernel(t_ref, o_ref):
    o_ref[...] = _pack_words(t_ref[...])


def _pack_table(tables, layer):
    _, e, dcol = tables.shape
    tm = min(1024, e)
    return pl.pallas_call(
        _pack_kernel,
        grid=(e // tm,),
        in_specs=[pl.BlockSpec((None, tm, dcol), lambda i: (layer, i, 0))],
        out_specs=pl.BlockSpec((tm, dcol // 2), lambda i: (i, 0)),
        out_shape=jax.ShapeDtypeStruct((e, dcol // 2), jnp.int32),
        compiler_params=_cparams("parallel"),
        name="pack_table",
    )(tables)


def _sc_gelu(x):
    z = math.sqrt(2.0 / math.pi) * (x + 0.044715 * (x * x * x))
    t = 1.0 - 2.0 / (jnp.exp(2.0 * z) + 1.0)
    return x * (0.5 * (1.0 + t))


def _sc_expert_body(tokens_per_worker, idx_hbm, gw_hbm, h2_hbm, u_hbm, v_hbm, y_hbm,
                    idx_v, gw_v, x_v, o_v, buf, coef_v, tr_v, sem, in_sem, out_sem):
    wid = lax.axis_index("s") * SC_CORES + lax.axis_index("c")
    base = wid * tokens_per_worker
    lane = lax.iota(jnp.int32, SC_LANES)
    n_gather = N_SEL // SC_ROWS
    zero = jnp.zeros((SC_LANES,), F32)

    n_steps = 2 * n_gather
    assert n_steps % SC_NBUF == 0

    def gather(p, tt, i):
        table = u_hbm if i < n_gather else v_hbm
        j = i % n_gather
        slot = i % SC_NBUF
        return pltpu.make_async_copy(table.at[idx_v.at[p, tt, pl.ds(j * SC_ROWS, SC_ROWS)]], buf.at[slot],
                                     sem.at[slot])

    def unpack(w):
        lo = lax.bitcast_convert_type(lax.shift_left(w, jnp.full((SC_LANES,), 16, jnp.int32)), F32)
        hi = lax.bitcast_convert_type(w & jnp.full((SC_LANES,), -65536, jnp.int32), F32)
        return lo, hi

    def packed(w):
        return plsc.bitcast(w, BF16)

    def unpack_sum(s):
        return unpack(plsc.bitcast(s, jnp.int32))

    def act_chunk(p, tt, j, slot):
        @pl.loop(0, SC_ROWS // SC_LANES)
        def _(half):
            for rb in range(SC_LANES // SC_ROW_BLOCK):
                r0 = half * SC_LANES + rb * SC_ROW_BLOCK

                def col(c, accs):
                    w0 = pl.multiple_of(c * (2 * SC_LANES), 2 * SC_LANES)
                    xa = packed(x_v[p, tt, pl.ds(w0, SC_LANES)])
                    xb = packed(x_v[p, tt, pl.ds(w0 + SC_LANES, SC_LANES)])
                    out = []
                    for r, a in enumerate(accs):
                        ua = packed(buf[slot, r0 + r, pl.ds(w0, SC_LANES)])
                        ub = packed(buf[slot, r0 + r, pl.ds(w0 + SC_LANES, SC_LANES)])
                        lo, hi = unpack_sum(ua * xa + ub * xb)
                        out.append(a + lo + hi)
                    return tuple(out)

                accs = lax.fori_loop(0, N_WCOL // 2, col, (zero,) * SC_ROW_BLOCK)
                for r in range(SC_ROW_BLOCK):
                    tr_v[pl.ds((rb * SC_ROW_BLOCK + r) * SC_LANES, SC_LANES)] = accs[r]
            tot = zero
            for jj in range(SC_LANES):
                tot = tot + plsc.load_gather(tr_v, [lane * SC_LANES + jj])
            k0 = pl.multiple_of(j * SC_ROWS + half * SC_LANES, SC_LANES)
            coef_v[pl.ds(k0, SC_LANES)] = gw_v[p, tt, pl.ds(k0, SC_LANES)] * _sc_gelu(tot)

    def acc_chunk(p, tt, j, slot, first):
        def tree_sum(parts):
            while len(parts) > 1:
                parts = [parts[i] + parts[i + 1] for i in range(0, len(parts), 2)]
            return parts[0]

        for rb in range(SC_ROWS // SC_ACC_ROWS):
            rows = list(range(rb * SC_ACC_ROWS, (rb + 1) * SC_ACC_ROWS))
            splat = {}
            for r in rows:
                c16 = plsc.load_gather(coef_v, [jnp.full((SC_LANES,), j * SC_ROWS + r, jnp.int32)])
                splat[r] = plsc.pack(c16, c16, format=plsc.PackFormat.INTERLEAVED)
            fresh = first and rb == 0

            @plsc.parallel_loop(0, N_WCOL)
            def _(c):
                woff = pl.multiple_of(c * SC_LANES, SC_LANES)
                hoff = pl.multiple_of(c * SC_LANES + D_MODEL // 2, SC_LANES)
                pa, pb = [], []
                for n in range(0, SC_ACC_ROWS, 2):
                    r0, r1 = rows[n], rows[n + 1]
                    s = (splat[r0] * packed(buf[slot, r0, pl.ds(woff, SC_LANES)])
                         + splat[r1] * packed(buf[slot, r1, pl.ds(woff, SC_LANES)]))
                    lo, hi = unpack_sum(s)
                    if n // 2 < SC_ACC_CHAINS:
                        pa.append(lo)
                        pb.append(hi)
                    else:
                        pa[(n // 2) % SC_ACC_CHAINS] = pa[(n // 2) % SC_ACC_CHAINS] + lo
                        pb[(n // 2) % SC_ACC_CHAINS] = pb[(n // 2) % SC_ACC_CHAINS] + hi
                sa, sb = tree_sum(pa), tree_sum(pb)
                if not fresh:
                    sa = sa + o_v[p, tt, pl.ds(woff, SC_LANES)]
                    sb = sb + o_v[p, tt, pl.ds(hoff, SC_LANES)]
                o_v[p, tt, pl.ds(woff, SC_LANES)] = sa
                o_v[p, tt, pl.ds(hoff, SC_LANES)] = sb

    n_blocks = tokens_per_worker // SC_TOK_BLOCK

    def block_start(blk):
        return pl.multiple_of(base + blk * SC_TOK_BLOCK, SC_TOK_BLOCK)

    def in_copies(blk, p):
        rows = pl.ds(block_start(blk), SC_TOK_BLOCK)
        return [pltpu.make_async_copy(idx_hbm.at[rows], idx_v.at[p], in_sem.at[p]),
                pltpu.make_async_copy(gw_hbm.at[rows], gw_v.at[p], in_sem.at[p]),
                pltpu.make_async_copy(h2_hbm.at[rows], x_v.at[p], in_sem.at[p])]

    def out_copy(blk, p):
        return pltpu.make_async_copy(o_v.at[p], y_hbm.at[pl.ds(block_start(blk), SC_TOK_BLOCK)], out_sem.at[p])

    for cp in in_copies(0, 0):
        cp.start()

    @pl.loop(0, n_blocks)
    def _(blk):
        p = lax.rem(blk, 2)
        for cp in in_copies(blk, p):
            cp.wait()

        @pl.when(blk + 1 < n_blocks)
        def _():
            for cp in in_copies(blk + 1, 1 - p):
                cp.start()

        @pl.when(blk >= 2)
        def _():
            out_copy(blk - 2, p).wait()

        ahead = SC_NBUF - 1
        for i in range(ahead):
            gather(p, 0, i).start()

        @pl.loop(0, SC_TOK_BLOCK)
        def _(tt):
            for i in range(n_steps):
                if i + ahead < n_steps:
                    gather(p, tt, i + ahead).start()
                else:
                    @pl.when(tt + 1 < SC_TOK_BLOCK)
                    def _():
                        gather(p, tt + 1, i + ahead - n_steps).start()
                gather(p, tt, i).wait()
                if i < n_gather:
                    act_chunk(p, tt, i, i % SC_NBUF)
                else:
                    acc_chunk(p, tt, i - n_gather, i % SC_NBUF, i == n_gather)

        out_copy(blk, p).start()

    for blk in (n_blocks - 2, n_blocks - 1):
        out_copy(blk, blk % 2).wait()


def _sc_expert_call(idx, gw, xw, u, v):
    n = idx.shape[0]
    workers = SC_CORES * SC_SUBCORES
    assert n % (workers * SC_TOK_BLOCK) == 0 and n // (workers * SC_TOK_BLOCK) >= 2
    mesh = plsc.VectorSubcoreMesh(core_axis_name="c", subcore_axis_name="s")
    return pl.kernel(
        functools.partial(_sc_expert_body, n // workers),
        out_type=jax.ShapeDtypeStruct((n, D_MODEL), F32),
        mesh=mesh,
        scratch_types=[pltpu.VMEM((2, SC_TOK_BLOCK, N_SEL), jnp.int32),
                       pltpu.VMEM((2, SC_TOK_BLOCK, N_SEL), F32),
                       pltpu.VMEM((2, SC_TOK_BLOCK, D_MODEL // 2), jnp.int32),
                       pltpu.VMEM((2, SC_TOK_BLOCK, D_MODEL), F32),
                       pltpu.VMEM((SC_NBUF, SC_ROWS, D_MODEL // 2), jnp.int32),
                       pltpu.VMEM((N_SEL,), F32),
                       pltpu.VMEM((SC_LANES * SC_LANES,), F32),
                       pltpu.SemaphoreType.DMA((SC_NBUF,)), pltpu.SemaphoreType.DMA((2,)),
                       pltpu.SemaphoreType.DMA((2,))],
        compiler_params=pltpu.CompilerParams(needs_layout_passes=False),
        name="peer_experts_sc",
    )(idx, gw, xw, u, v)


def _resid_kernel(x_ref, y_ref, g_ref, o_ref):
    o_ref[...] = x_ref[...] + g_ref[...] * y_ref[...]


def _resid_call(x, y, g2, per_token, tokens_per_req):
    n = x.shape[0]
    tm = min(512, n if per_token else tokens_per_req)
    row = pl.BlockSpec((tm, D_MODEL), lambda i: (i, 0))
    return pl.pallas_call(
        _resid_kernel,
        grid=(n // tm,),
        in_specs=[row, row, _mod_spec(per_token, tm, tokens_per_req)],
        out_specs=row,
        out_shape=jax.ShapeDtypeStruct((n, D_MODEL), F32),
        compiler_params=_cparams("parallel"),
        name="peer_residual",
    )(x, y, g2)


def _final_kernel(x_ref, g_ref, o_ref):
    x = x_ref[...]
    o_ref[...] = x * lax.rsqrt(jnp.mean(x * x, axis=-1, keepdims=True) + EPS) * g_ref[...]


def _final_call(x, g):
    n = x.shape[0]
    tm = min(512, n)
    return pl.pallas_call(
        _final_kernel,
        grid=(n // tm,),
        in_specs=[pl.BlockSpec((tm, D_MODEL), lambda i: (i, 0)), pl.BlockSpec((1, D_MODEL), lambda i: (0, 0))],
        out_specs=pl.BlockSpec((tm, D_MODEL), lambda i: (i, 0)),
        out_shape=jax.ShapeDtypeStruct((n, D_MODEL), F32),
        compiler_params=_cparams("parallel"),
        name="final_norm",
    )(x, g)


def _split_w_in(w_in_l, gate_b_l):
    cuts = np.cumsum([A_WIDTH + 2 * A_KV_WIDTH, 2 * M_WIDTH, M_WIDTH, M_WIDTH]).tolist()
    wa = w_in_l[:, :cuts[0]].astype(BF16)
    wqk = w_in_l[:, cuts[0]:cuts[1]].astype(BF16)
    wv = w_in_l[:, cuts[1]:cuts[2]].astype(BF16)
    wo = w_in_l[:, cuts[2]:cuts[3]].astype(BF16)
    ng = 2 * M_HEADS
    wg = jnp.pad(w_in_l[:, cuts[3]:], ((0, 0), (0, LANES - ng))).astype(BF16)
    gb = jnp.pad(gate_b_l.astype(F32), (0, LANES - ng)).reshape(1, LANES)
    return wa, wqk, wv, wo, wg, gb


def _layer(x, mods, per_token, batch, seq, t_valid, lw, bias_p, bias_c, bias_n, kv_cache, conv0, state,
           after=None):
    (norm_mix, norm_ffn, w_in, conv_w, conv_b, gate_b, sinks, m_norm, w_out, peer_query, peer_keys,
     peer_u, peer_v) = lw
    if after is not None:
        x, _ = lax.optimization_barrier((x, after))
    sh1, sc1, g1, sh2, sc2, g2 = mods
    wa, wqk, wv, wo, wg, gb = _split_w_in(w_in, gate_b)
    qkv, qkm, vm, om, gc = _in_call(x, sc1, sh1, norm_mix.reshape(1, -1), wa, wqk, wv, wo, wg, gb,
                                    per_token, seq)
    if kv_cache is None:
        att = _attn_p_call(qkv, bias_p, sinks, batch, seq)
        kv3 = qkv.reshape(batch, seq, -1)
        new_k = kv3[:, seq - WINDOW:, A_WIDTH:A_WIDTH + A_KV_WIDTH]
        new_v = kv3[:, seq - WINDOW:, A_WIDTH + A_KV_WIDTH:]
        chunk = M_CHUNK
    else:
        att, new_k, new_v = _attn_s_call(qkv, kv_cache[0], kv_cache[1], bias_c, bias_n, sinks, t_valid)
        chunk = seq
    c0, n0, m0 = state
    mo, c_new, n_new, m_new = _mlstm_call(qkm, vm, om, gc, conv_w, conv_b.reshape(1, -1),
                                          m_norm.reshape(1, -1), conv0, c0, n0, m0,
                                          batch, seq, chunk, min(t_valid, chunk))
    new_conv = qkm.reshape(batch, seq, -1)[:, t_valid - (CONV_W - 1):t_valid]
    pad_rows = seq - t_valid
    if pad_rows:
        keep = lambda a: a.reshape(batch, seq, -1)[:, :t_valid].reshape(batch * t_valid, -1)
        att, mo, x, g1, sc2, sh2, g2 = (keep(a) for a in (att, mo, x, g1, sc2, sh2, g2))
    x_mid, xw, qp = _out_call(att, mo, x, g1, sc2, sh2, norm_ffn.reshape(1, -1),
                              w_out[:A_WIDTH].astype(BF16), w_out[A_WIDTH:].astype(BF16),
                              peer_query.astype(BF16), per_token, t_valid)
    idx, gw = _select_call(qp, peer_keys.astype(BF16))
    y = _sc_expert_call(idx, gw, xw, peer_u, peer_v)
    x_new = _resid_call(x_mid, y, g2, per_token, t_valid)
    if pad_rows:
        x_new = jnp.pad(x_new.reshape(batch, t_valid, -1), ((0, 0), (0, pad_rows), (0, 0))).reshape(batch * seq, -1)
    new_k = new_k.reshape(batch, WINDOW, A_KV_HEADS, A_HEAD_DIM)
    new_v = new_v.reshape(batch, WINDOW, A_KV_HEADS, A_HEAD_DIM)
    return x_new, (new_k, new_v, new_conv, c_new, n_new, m_new[:, :M_HEADS, 0]), idx


def _prompt_group_sizes(n_req):
    if n_req < PROMPT_GROUPS:
        return [1] * n_req
    mid, n_mid = n_req - 2, PROMPT_GROUPS - 2
    weights = [i + 2 for i in range(n_mid)]
    sizes = [max(1, mid * w // sum(weights)) for w in weights]
    for i in range(mid - sum(sizes)):
        sizes[n_mid - 1 - i % n_mid] += 1
    return [1] + sizes + [1]


def kernel(x_prompt, x_sample, c_prompt, c_sample, cache_k, cache_v, state_conv, state_C, state_n, state_m, rel_bias, w_ada, b_ada, norm_mix, norm_ffn, w_in, conv_w, conv_b, gate_b, attn_sinks, m_norm, w_out, peer_query, peer_keys, peer_u, peer_v, norm_final):
    depth = w_ada.shape[0]
    bp, tp, d = x_prompt.shape
    bs, ts, _ = x_sample.shape
    assert tp % WINDOW == 0 and tp % M_CHUNK == 0 and ts <= SAMPLE_PAD and ts >= CONV_W - 1

    mod_all = _ada_call(jnp.concatenate([c_prompt, c_sample], axis=0), w_ada, b_ada)

    qi = np.arange(WINDOW)[:, None]
    bias_p = _bias_call(rel_bias, qi + WINDOW - np.arange(2 * WINDOW)[None, :])
    qs = np.arange(SAMPLE_PAD)[:, None]
    bias_c = _bias_call(rel_bias, qs + WINDOW - np.arange(WINDOW)[None, :])
    bias_n = _bias_call(rel_bias, qs - np.arange(SAMPLE_PAD)[None, :])

    sizes = _prompt_group_sizes(bp)
    starts = np.cumsum([0] + sizes).tolist()
    xg = [x_prompt[starts[g]:starts[g + 1]].reshape(sizes[g] * tp, d) for g in range(len(sizes))]
    xs = jnp.pad(x_sample, ((0, 0), (0, SAMPLE_PAD - ts), (0, 0))).reshape(bs * SAMPLE_PAD, d)
    halo_pad = ((0, 0), (SUBLANES - (CONV_W - 1), 0), (0, 0))

    st_p, st_s = [], []
    for l in range(depth):
        lw = (norm_mix[l], norm_ffn[l], w_in[l], conv_w[l], conv_b[l], gate_b[l], attn_sinks[l], m_norm[l],
              w_out[l], peer_query[l], peer_keys[l], _pack_table(peer_u, l), _pack_table(peer_v, l))
        mod_s = [jnp.repeat(m, SAMPLE_PAD, axis=0) for m in jnp.split(mod_all[l, bp:], 6, axis=-1)]
        sp_groups = []
        for g, bg in enumerate(sizes):
            mod_g = [m.reshape(bg, 1, d) for m in jnp.split(mod_all[l, starts[g]:starts[g + 1]], 6, axis=-1)]
            zero_state = (jnp.zeros((bg, M_HEADS, M_HEAD_DIM, M_HEAD_DIM), F32),
                          jnp.zeros((bg, M_HEADS, M_HEAD_DIM), F32),
                          jnp.zeros((bg, SUBLANES, LANES), F32))
            zero_conv = jnp.zeros((bg, SUBLANES, 2 * M_WIDTH), F32)
            xg[g], sp, last_idx = _layer(xg[g], mod_g, False, bg, tp, tp, lw, bias_p, None, None, None,
                                         zero_conv, zero_state)
            sp_groups.append(sp)
        st_p.append([jnp.concatenate([sp[i] for sp in sp_groups], axis=0) for i in range(6)])
        state_s = (state_C[l].astype(F32), state_n[l].astype(F32),
                   jnp.broadcast_to(jnp.pad(state_m[l].astype(F32), ((0, 0), (0, SUBLANES - M_HEADS)))[:, :, None],
                                    (bs, SUBLANES, LANES)))
        kv_cache = (cache_k[l].reshape(bs, WINDOW, A_KV_WIDTH), cache_v[l].reshape(bs, WINDOW, A_KV_WIDTH))
        xs, ss, _ = _layer(xs, mod_s, True, bs, SAMPLE_PAD, ts, lw, None, bias_c, bias_n, kv_cache,
                           jnp.pad(state_conv[l].astype(F32), halo_pad), state_s, after=last_idx)
        st_s.append(ss)

    gfin = norm_final.reshape(1, d)
    y_prompt = jnp.concatenate([_final_call(x, gfin).reshape(bg, tp, d) for x, bg in zip(xg, sizes)], axis=0)
    y_sample = _final_call(xs, gfin).reshape(bs, SAMPLE_PAD, d)[:, :ts]
    outs_p = [jnp.stack([s[i] for s in st_p]) for i in range(6)]
    outs_s = [jnp.stack([s[i] for s in st_s]) for i in range(6)]
    return (y_prompt, y_sample, *outs_p, *outs_s)
```

```python
import functools
import math

import numpy as np
import jax
import jax.numpy as jnp
from jax import lax
from jax.experimental import pallas as pl
from jax.experimental.pallas import tpu as pltpu
from jax.experimental.pallas import tpu_sc as plsc

F32 = jnp.float32
BF16 = jnp.bfloat16
HIGHEST = lax.Precision.HIGHEST

D_MODEL = 1024
A_HEADS = 8
A_KV_HEADS = 2
A_GROUP = A_HEADS // A_KV_HEADS
A_HEAD_DIM = 64
A_WIDTH = A_HEADS * A_HEAD_DIM
A_KV_WIDTH = A_KV_HEADS * A_HEAD_DIM
WINDOW = 128
ATT_SCALE = A_HEAD_DIM ** -0.5
N_BUCKETS = 32
MAX_DISTANCE = WINDOW
M_HEADS = 4
M_HEAD_DIM = 128
M_WIDTH = M_HEADS * M_HEAD_DIM
CONV_W = 4
M_CHUNK = 64
N_KEYS = 128
P_HEADS = 8
P_TOPK = 16
P_KEY_DIM = 256
P_HALF = P_KEY_DIM // 2
EPS = 1e-6
NEG_INF = -1e30

LANES = 128
SUBLANES = 8
SAMPLE_PAD = SUBLANES
VMEM_LIMIT = 48 * 1024 * 1024
PROMPT_GROUPS = 7

NT_DIMS = (((1,), (1,)), ((), ()))
TN_DIMS = (((0,), (0,)), ((), ()))


def _cparams(*sem):
    return pltpu.CompilerParams(dimension_semantics=sem, vmem_limit_bytes=VMEM_LIMIT)


def _bdot(a, b):
    return jnp.dot(a.astype(BF16), b.astype(BF16), preferred_element_type=F32)


def _bdot_nt(a, b):
    return lax.dot_general(a.astype(BF16), b.astype(BF16), NT_DIMS, preferred_element_type=F32)


def _sigmoid(x):
    return 1.0 / (1.0 + jnp.exp(-x))


def _log_sigmoid(x):
    return jnp.minimum(x, 0.0) - jnp.log1p(jnp.exp(-jnp.abs(x)))


def _ada_kernel(c_ref, w_ref, b_ref, o_ref):
    c = c_ref[...]
    s = c * _sigmoid(c)
    o_ref[...] = jnp.dot(s, w_ref[...], precision=HIGHEST, preferred_element_type=F32) + b_ref[...]


def _ada_call(c_all, w_ada, b_ada):
    depth, d, n6 = w_ada.shape
    rows = c_all.shape[0]
    bn = 1024
    return pl.pallas_call(
        _ada_kernel,
        grid=(depth, n6 // bn),
        in_specs=[
            pl.BlockSpec((rows, d), lambda l, j: (0, 0)),
            pl.BlockSpec((None, d, bn), lambda l, j: (l, 0, j)),
            pl.BlockSpec((None, 1, bn), lambda l, j: (l, 0, j)),
        ],
        out_specs=pl.BlockSpec((None, rows, bn), lambda l, j: (l, 0, j)),
        out_shape=jax.ShapeDtypeStruct((depth, rows, n6), F32),
        compiler_params=_cparams("parallel", "parallel"),
        name="ada_mod",
    )(c_all, w_ada, b_ada.reshape(depth, 1, n6))


def _mod_spec(per_token, tm, tokens_per_req):
    if per_token:
        return pl.BlockSpec((tm, D_MODEL), lambda i: (i, 0))
    tiles = tokens_per_req // tm
    return pl.BlockSpec((None, 1, D_MODEL), lambda i: (i // tiles, 0, 0))


def _in_kernel(x_ref, sc_ref, sh_ref, g_ref, wa_ref, wqk_ref, wv_ref, wo_ref, wg_ref, gb_ref,
               qkv_ref, qkm_ref, v_ref, o_ref, gc_ref):
    x = x_ref[...]
    y = x * lax.rsqrt(jnp.mean(x * x, axis=-1, keepdims=True) + EPS) * g_ref[...]
    h = (y * (1.0 + sc_ref[...]) + sh_ref[...]).astype(BF16)
    qkv_ref[...] = jnp.dot(h, wa_ref[...], preferred_element_type=F32)
    qkm_ref[...] = jnp.dot(h, wqk_ref[...], preferred_element_type=F32)
    v_ref[...] = jnp.dot(h, wv_ref[...], preferred_element_type=F32)
    o_ref[...] = jnp.dot(h, wo_ref[...], preferred_element_type=F32)
    g = jnp.dot(h, wg_ref[...], preferred_element_type=F32) + gb_ref[...]
    lane = lax.broadcasted_iota(jnp.int32, g.shape, 1)
    gc_ref[...] = jnp.where(lane < M_HEADS, g, jnp.where(lane < 2 * M_HEADS, _log_sigmoid(g), 0.0))


def _in_call(x, sc, sh, gnorm, wa, wqk, wv, wo, wg, gb, per_token, tokens_per_req):
    n = x.shape[0]
    tm = min(512, n if per_token else tokens_per_req)
    mod = _mod_spec(per_token, tm, tokens_per_req)
    full = lambda shape: pl.BlockSpec(shape, lambda i: (0,) * len(shape))
    row = lambda w: pl.BlockSpec((tm, w), lambda i: (i, 0))
    return pl.pallas_call(
        _in_kernel,
        grid=(n // tm,),
        in_specs=[row(D_MODEL), mod, mod, full((1, D_MODEL)), full(wa.shape), full(wqk.shape),
                  full(wv.shape), full(wo.shape), full(wg.shape), full((1, LANES))],
        out_specs=[row(wa.shape[1]), row(wqk.shape[1]), row(wv.shape[1]), row(wo.shape[1]), row(LANES)],
        out_shape=[jax.ShapeDtypeStruct((n, w), F32)
                   for w in (wa.shape[1], wqk.shape[1], wv.shape[1], wo.shape[1], LANES)],
        compiler_params=_cparams("parallel"),
        name="in_proj",
    )(x, sc, sh, gnorm, wa, wqk, wv, wo, wg, gb)


def _t5_bucket_np(dist):
    n = np.maximum(dist, 0)
    max_exact = N_BUCKETS // 2
    nf = np.maximum(n, 1).astype(np.float64)
    large = max_exact + (np.log(nf / max_exact) / math.log(MAX_DISTANCE / max_exact)
                         * (N_BUCKETS - max_exact)).astype(np.int32)
    return np.where(n < max_exact, n, np.minimum(large, N_BUCKETS - 1)).astype(np.int32)


def _bias_kernel(bucket_ref, rel_ref, o_ref):
    bucket = bucket_ref[...]
    for h in range(A_HEADS):
        acc = jnp.zeros(bucket.shape, F32)
        for b in range(N_BUCKETS):
            acc = jnp.where(bucket == b, rel_ref[b, h], acc)
        o_ref[h] = acc


def _bias_call(rel_bias, dist):
    bucket = jnp.asarray(_t5_bucket_np(dist))
    nq, nk = dist.shape
    return pl.pallas_call(
        _bias_kernel,
        in_specs=[pl.BlockSpec((nq, nk), lambda: (0, 0)),
                  pl.BlockSpec(memory_space=pltpu.SMEM)],
        out_specs=pl.BlockSpec((A_HEADS, nq, nk), lambda: (0, 0, 0)),
        out_shape=jax.ShapeDtypeStruct((A_HEADS, nq, nk), F32),
        name="t5_bias",
    )(bucket, rel_bias)


def _attn_p_kernel(q_ref, kp_ref, kc_ref, vp_ref, vc_ref, bias_ref, sink_ref, o_ref):
    i = pl.program_id(1)
    qi = lax.broadcasted_iota(jnp.int32, (WINDOW, WINDOW), 0)
    kj = lax.broadcasted_iota(jnp.int32, (WINDOW, WINDOW), 1)
    valid_prev = jnp.logical_and(kj > qi, i > 0)
    valid_cur = kj <= qi
    q = q_ref[...]
    for h in range(A_HEADS):
        kv = h // A_GROUP
        qh = q[:, h * A_HEAD_DIM:(h + 1) * A_HEAD_DIM]
        sl = slice(kv * A_HEAD_DIM, (kv + 1) * A_HEAD_DIM)
        bias = bias_ref[h]
        sp = _bdot_nt(qh, kp_ref[:, sl]) * ATT_SCALE + bias[:, :WINDOW]
        sc = _bdot_nt(qh, kc_ref[:, sl]) * ATT_SCALE + bias[:, WINDOW:]
        sp = jnp.where(valid_prev, sp, NEG_INF)
        sc = jnp.where(valid_cur, sc, NEG_INF)
        sink = sink_ref[0, h]
        mx = jnp.maximum(jnp.maximum(jnp.max(sp, axis=-1, keepdims=True),
                                     jnp.max(sc, axis=-1, keepdims=True)), sink)
        ep = jnp.exp(sp - mx)
        ec = jnp.exp(sc - mx)
        den = (jnp.sum(ep, axis=-1, keepdims=True) + jnp.sum(ec, axis=-1, keepdims=True)
               + jnp.exp(sink - mx))
        o = _bdot(ep / den, vp_ref[:, sl]) + _bdot(ec / den, vc_ref[:, sl])
        o_ref[:, h * A_HEAD_DIM:(h + 1) * A_HEAD_DIM] = o


def _attn_p_call(qkv, bias, sinks, batch, seq):
    nb = seq // WINDOW
    n = batch * seq
    kcol = A_WIDTH // A_KV_WIDTH
    vcol = kcol + 1
    cur = lambda col: pl.BlockSpec((WINDOW, A_KV_WIDTH), lambda b, i: (b * nb + i, col))
    prev = lambda col: pl.BlockSpec((WINDOW, A_KV_WIDTH),
                                    lambda b, i: (b * nb + jnp.maximum(i - 1, 0), col))
    return pl.pallas_call(
        _attn_p_kernel,
        grid=(batch, nb),
        in_specs=[pl.BlockSpec((WINDOW, A_WIDTH), lambda b, i: (b * nb + i, 0)),
                  prev(kcol), cur(kcol), prev(vcol), cur(vcol),
                  pl.BlockSpec((A_HEADS, WINDOW, 2 * WINDOW), lambda b, i: (0, 0, 0)),
                  pl.BlockSpec(memory_space=pltpu.SMEM)],
        out_specs=pl.BlockSpec((WINDOW, A_WIDTH), lambda b, i: (b * nb + i, 0)),
        out_shape=jax.ShapeDtypeStruct((n, A_WIDTH), F32),
        compiler_params=_cparams("parallel", "parallel"),
        name="swa_prompt",
    )(qkv, qkv, qkv, qkv, qkv, bias, sinks.reshape(1, A_HEADS))


def _attn_s_kernel(n_new, qkv_ref, ck_ref, cv_ref, bc_ref, bn_ref, sink_ref,
                   o_ref, nk_ref, nv_ref, kk_s, vv_s):
    qkv = qkv_ref[...]
    knew = qkv[:, A_WIDTH:A_WIDTH + A_KV_WIDTH]
    vnew = qkv[:, A_WIDTH + A_KV_WIDTH:A_WIDTH + 2 * A_KV_WIDTH]
    ck = ck_ref[...]
    cv = cv_ref[...]
    rows = A_GROUP * SAMPLE_PAD
    qi = lax.broadcasted_iota(jnp.int32, (rows, WINDOW), 0) % SAMPLE_PAD
    kj = lax.broadcasted_iota(jnp.int32, (rows, WINDOW), 1)
    valid_c = kj > qi
    rcol = lax.broadcasted_iota(jnp.int32, (rows, 1), 0)
    qcol = rcol % SAMPLE_PAD
    for kv in range(A_KV_HEADS):
        heads = range(kv * A_GROUP, (kv + 1) * A_GROUP)
        sl = slice(kv * A_HEAD_DIM, (kv + 1) * A_HEAD_DIM)
        qs = jnp.concatenate([qkv[:, h * A_HEAD_DIM:(h + 1) * A_HEAD_DIM] for h in heads], axis=0)
        bias_c = jnp.concatenate([bc_ref[h] for h in heads], axis=0)
        bias_n = jnp.concatenate([bn_ref[h] for h in heads], axis=0)
        sink = jnp.zeros((rows, 1), F32)
        for g, h in enumerate(heads):
            sink = jnp.where(rcol // SAMPLE_PAD == g, sink_ref[0, h], sink)
        s_c = _bdot_nt(qs, ck[:, sl]) * ATT_SCALE + bias_c
        s_c = jnp.where(valid_c, s_c, NEG_INF)
        s_n = []
        for j in range(n_new):
            sj = jnp.sum(qs * knew[j:j + 1, sl], axis=-1, keepdims=True) * ATT_SCALE + bias_n[:, j:j + 1]
            s_n.append(jnp.where(qcol >= j, sj, NEG_INF))
        mx = jnp.maximum(jnp.max(s_c, axis=-1, keepdims=True), sink)
        for sj in s_n:
            mx = jnp.maximum(mx, sj)
        e_c = jnp.exp(s_c - mx)
        den = jnp.sum(e_c, axis=-1, keepdims=True) + jnp.exp(sink - mx)
        o = _bdot(e_c, cv[:, sl])
        for j, sj in enumerate(s_n):
            ej = jnp.exp(sj - mx)
            den = den + ej
            o = o + ej * vnew[j:j + 1, sl]
        o = o / den
        for g, h in enumerate(heads):
            o_ref[:, h * A_HEAD_DIM:(h + 1) * A_HEAD_DIM] = o[g * SAMPLE_PAD:(g + 1) * SAMPLE_PAD, :]
    kk_s[0:WINDOW, :] = ck
    kk_s[WINDOW:WINDOW + SAMPLE_PAD, :] = knew
    vv_s[0:WINDOW, :] = cv
    vv_s[WINDOW:WINDOW + SAMPLE_PAD, :] = vnew
    nk_ref[...] = kk_s[n_new:n_new + WINDOW, :]
    nv_ref[...] = vv_s[n_new:n_new + WINDOW, :]


def _attn_s_call(qkv, ck, cv, bias_c, bias_n, sinks, n_new):
    nreq = ck.shape[0]
    wq = qkv.shape[1]
    full3 = lambda shape: pl.BlockSpec(shape, lambda b: (0, 0, 0))
    cache = pl.BlockSpec((None, WINDOW, A_KV_WIDTH), lambda b: (b, 0, 0))
    return pl.pallas_call(
        functools.partial(_attn_s_kernel, n_new),
        grid=(nreq,),
        in_specs=[pl.BlockSpec((SAMPLE_PAD, wq), lambda b: (b, 0)), cache, cache,
                  full3(bias_c.shape), full3(bias_n.shape),
                  pl.BlockSpec(memory_space=pltpu.SMEM)],
        out_specs=[pl.BlockSpec((SAMPLE_PAD, A_WIDTH), lambda b: (b, 0)), cache, cache],
        out_shape=[jax.ShapeDtypeStruct((nreq * SAMPLE_PAD, A_WIDTH), F32),
                   jax.ShapeDtypeStruct(ck.shape, F32), jax.ShapeDtypeStruct(cv.shape, F32)],
        scratch_shapes=[pltpu.VMEM((WINDOW + SAMPLE_PAD, A_KV_WIDTH), F32),
                        pltpu.VMEM((WINDOW + SAMPLE_PAD, A_KV_WIDTH), F32)],
        compiler_params=_cparams("parallel"),
        name="swa_sample",
    )(qkv, ck, cv, bias_c, bias_n, sinks.reshape(1, A_HEADS))


def _mlstm_kernel(chunk, t_valid, qk_ref, v_ref, og_ref, gc_ref, cw_ref, cb_ref, mn_ref,
                  conv0_ref, c0_ref, n0_ref, m0_ref,
                  out_ref, cout_ref, nout_ref, mout_ref,
                  xp_s, c_s, n_s, m_s):
    step = pl.program_id(1)
    halo = SUBLANES

    @pl.when(step == 0)
    def _():
        xp_s[0:halo, :] = conv0_ref[...]
        c_s[...] = c0_ref[...]
        n_s[...] = n0_ref[...]
        m_s[...] = m0_ref[...]

    xp_s[halo:halo + chunk, :] = qk_ref[...]
    cw = cw_ref[...]
    y = cb_ref[...]
    for i in range(CONV_W):
        off = halo - (CONV_W - 1) + i
        y = y + xp_s[off:off + chunk, :] * cw[i:i + 1, :]
    xp_s[0:halo, :] = xp_s[chunk:chunk + halo, :]
    y = y * _sigmoid(y)
    q_all = y[:, :M_WIDTH]
    k_all = y[:, M_WIDTH:] * (M_HEAD_DIM ** -0.5)

    g = gc_ref[...]
    if t_valid < chunk:
        row = lax.broadcasted_iota(jnp.int32, g.shape, 0)
        lane = lax.broadcasted_iota(jnp.int32, g.shape, 1)
        g = jnp.where(row < t_valid, g, jnp.where(lane < M_HEADS, NEG_INF, 0.0))
    tr = lax.broadcasted_iota(jnp.int32, (chunk, chunk), 0)
    tc = lax.broadcasted_iota(jnp.int32, (chunk, chunk), 1)
    causal = tr >= tc
    tri = causal.astype(F32)
    bcol = jnp.dot(tri, g, precision=HIGHEST, preferred_element_type=F32)
    er = lax.broadcasted_iota(jnp.int32, (SUBLANES, LANES), 0)
    ec = lax.broadcasted_iota(jnp.int32, (SUBLANES, LANES), 1)
    eye = (er == ec).astype(F32)
    g_rows = lax.dot_general(eye, g, NT_DIMS, precision=HIGHEST, preferred_element_type=F32)
    b_rows = lax.dot_general(eye, bcol, NT_DIMS, precision=HIGHEST, preferred_element_type=F32)

    for h in range(M_HEADS):
        hs = slice(h * M_HEAD_DIM, (h + 1) * M_HEAD_DIM)
        b_c = bcol[:, M_HEADS + h:M_HEADS + h + 1]
        ig_c = g[:, h:h + 1]
        b_r = b_rows[M_HEADS + h:M_HEADS + h + 1, :]
        ig_r = g_rows[h:h + 1, :]
        m_prev = m_s[h:h + 1, 0:1]
        logw = jnp.where(causal, b_c - b_r + ig_r, -jnp.inf)
        inter = b_c + m_prev
        m_t = jnp.maximum(inter, jnp.max(logw, axis=-1, keepdims=True))
        w = jnp.exp(logw - m_t)
        a = jnp.exp(inter - m_t)
        q = q_all[:, hs]
        k = k_all[:, hs]
        v = v_ref[:, hs]
        cmat = c_s[h]
        nvec = n_s[h:h + 1, :]
        wqk = w * _bdot_nt(q, k)
        num = _bdot(wqk, v) + a * _bdot_nt(q, cmat)
        den = jnp.sum(wqk, axis=-1, keepdims=True) + a * jnp.sum(q * nvec, axis=-1, keepdims=True)
        hh = num / jnp.maximum(jnp.abs(den), jnp.exp(-m_t))
        m_new = m_t[chunk - 1:chunk, :]
        b_last = b_c[chunk - 1:chunk, :]
        wl = jnp.exp(b_last - b_c + ig_c - m_new)
        al = jnp.exp(b_last + m_prev - m_new)
        c_s[h] = al * cmat + lax.dot_general((v * wl).astype(BF16), k.astype(BF16), TN_DIMS,
                                             preferred_element_type=F32)
        n_s[h:h + 1, :] = al * nvec + jnp.sum(wl * k, axis=0, keepdims=True)
        m_s[h:h + 1, :] = jnp.broadcast_to(m_new, (1, LANES))
        hn = hh * lax.rsqrt(jnp.mean(hh * hh, axis=-1, keepdims=True) + EPS) * mn_ref[:, hs]
        out_ref[:, hs] = _sigmoid(og_ref[:, hs]) * hn

    @pl.when(step == pl.num_programs(1) - 1)
    def _():
        cout_ref[...] = c_s[...]
        nout_ref[...] = n_s[...]
        mout_ref[...] = m_s[...]


def _mlstm_call(qk, v, og, gc, conv_w, conv_b, m_norm, conv0, c0, n0, m0, batch, seq, chunk, t_valid):
    nc = seq // chunk
    n = batch * seq
    row = lambda w: pl.BlockSpec((chunk, w), lambda b, c: (b * nc + c, 0))
    full2 = lambda shape: pl.BlockSpec(shape, lambda b, c: (0, 0))
    per_b = lambda shape: pl.BlockSpec((None,) + shape, lambda b, c: (b,) + (0,) * len(shape))
    dh = M_HEAD_DIM
    return pl.pallas_call(
        functools.partial(_mlstm_kernel, chunk, t_valid),
        grid=(batch, nc),
        in_specs=[row(2 * M_WIDTH), row(M_WIDTH), row(M_WIDTH), row(LANES),
                  full2((CONV_W, 2 * M_WIDTH)), full2((1, 2 * M_WIDTH)), full2((1, M_WIDTH)),
                  per_b((SUBLANES, 2 * M_WIDTH)), per_b((M_HEADS, dh, dh)), per_b((M_HEADS, dh)),
                  per_b((SUBLANES, LANES))],
        out_specs=[row(M_WIDTH), per_b((M_HEADS, dh, dh)), per_b((M_HEADS, dh)), per_b((SUBLANES, LANES))],
        out_shape=[jax.ShapeDtypeStruct((n, M_WIDTH), F32),
                   jax.ShapeDtypeStruct((batch, M_HEADS, dh, dh), F32),
                   jax.ShapeDtypeStruct((batch, M_HEADS, dh), F32),
                   jax.ShapeDtypeStruct((batch, SUBLANES, LANES), F32)],
        scratch_shapes=[pltpu.VMEM((SUBLANES + chunk, 2 * M_WIDTH), F32),
                        pltpu.VMEM((M_HEADS, dh, dh), F32),
                        pltpu.VMEM((M_HEADS, dh), F32),
                        pltpu.VMEM((SUBLANES, LANES), F32)],
        compiler_params=_cparams("parallel", "arbitrary"),
        name="mlstm",
    )(qk, v, og, gc, conv_w, conv_b, m_norm, conv0, c0, n0, m0)


def _pack_words(x):
    bits = pltpu.bitcast(x.astype(BF16).astype(F32), jnp.uint32)
    half = x.shape[1] // 2
    lo = lax.shift_right_logical(bits[:, :half], jnp.uint32(16))
    hi = bits[:, half:] & jnp.uint32(0xFFFF0000)
    return pltpu.bitcast(lo | hi, jnp.int32)


def _out_kernel(att_ref, mo_ref, x_ref, g1_ref, sc_ref, sh_ref, gn_ref, wa_ref, wm_ref, wq_ref,
                xo_ref, h2_ref, qp_ref):
    mix = (jnp.dot(att_ref[...].astype(BF16), wa_ref[...], preferred_element_type=F32)
           + jnp.dot(mo_ref[...].astype(BF16), wm_ref[...], preferred_element_type=F32))
    x = x_ref[...] + g1_ref[...] * mix
    xo_ref[...] = x
    y = x * lax.rsqrt(jnp.mean(x * x, axis=-1, keepdims=True) + EPS) * gn_ref[...]
    h2 = y * (1.0 + sc_ref[...]) + sh_ref[...]
    qp_ref[...] = jnp.dot(h2.astype(BF16), wq_ref[...], preferred_element_type=F32)
    h2_ref[...] = _pack_words(h2)


def _out_call(att, mo, x, g1, sc, sh, gnorm, wa, wm, wq, per_token, tokens_per_req):
    n = x.shape[0]
    tm = min(256, n if per_token else tokens_per_req)
    mod = _mod_spec(per_token, tm, tokens_per_req)
    full = lambda shape: pl.BlockSpec(shape, lambda i: (0,) * len(shape))
    row = lambda w: pl.BlockSpec((tm, w), lambda i: (i, 0))
    nq = wq.shape[1]
    return pl.pallas_call(
        _out_kernel,
        grid=(n // tm,),
        in_specs=[row(A_WIDTH), row(M_WIDTH), row(D_MODEL), mod, mod, mod, full((1, D_MODEL)),
                  full(wa.shape), full(wm.shape), full(wq.shape)],
        out_specs=[row(D_MODEL), row(D_MODEL // 2), row(nq)],
        out_shape=[jax.ShapeDtypeStruct((n, D_MODEL), F32), jax.ShapeDtypeStruct((n, D_MODEL // 2), jnp.int32),
                   jax.ShapeDtypeStruct((n, nq), F32)],
        compiler_params=_cparams("parallel"),
        name="out_proj",
    )(att, mo, x, g1, sc, sh, gnorm, wa, wm, wq)


def _pk_cells():
    return [(a, b) for a in range(P_TOPK) for b in range(P_TOPK) if (a + 1) * (b + 1) <= P_TOPK]


PK_CELL_ROWS = 64


def _pk_expand_mats():
    cells = _pk_cells()
    e0 = np.zeros((PK_CELL_ROWS, LANES), np.float32)
    e1 = np.zeros((PK_CELL_ROWS, LANES), np.float32)
    for j, (a, b) in enumerate(cells):
        e0[j, a] = 1.0
        e1[j, b] = 1.0
    return e0, e1, len(cells)


def _top_rows(s, rowf, rounds):
    n_rows = s.shape[0]
    vals, idxs = [], []
    for _ in range(rounds):
        m = jnp.max(s, axis=0, keepdims=True)
        i = jnp.min(jnp.where(s == m, rowf, float(n_rows)), axis=0, keepdims=True)
        vals.append(m)
        idxs.append(i)
        s = jnp.where(rowf == i, -jnp.inf, s)
    return jnp.concatenate(vals, axis=0), jnp.concatenate(idxs, axis=0)


def _select_kernel(n_cells, qp_ref, keys_ref, e0_ref, e1_ref, idx_ref, gw_ref, idx_s, gw_s):
    tm = qp_ref.shape[0]
    keyf = lax.broadcasted_iota(jnp.int32, (N_KEYS, tm), 0).astype(F32)
    cellf = lax.broadcasted_iota(jnp.int32, (PK_CELL_ROWS, tm), 0).astype(F32)
    e0 = e0_ref[...]
    e1 = e1_ref[...]
    pad = jnp.zeros((LANES - P_TOPK, tm), F32)

    def head(h, carry):
        h = jnp.asarray(h, jnp.int32)
        sub = []
        for c in range(2):
            col = pl.multiple_of((h * 2 + c) * P_HALF, P_HALF)
            s = _bdot_nt(keys_ref[h, c], qp_ref[:, pl.ds(col, P_HALF)])
            sub.append(_top_rows(s, keyf, P_TOPK))
        (v0, i0), (v1, i1) = sub
        expand = lambda e, x: jnp.dot(e, jnp.concatenate([x, pad], axis=0), precision=HIGHEST,
                                      preferred_element_type=F32)
        cand = expand(e0, v0) + expand(e1, v1)
        cidx = expand(e0, i0 * float(N_KEYS)) + expand(e1, i1)
        cand = jnp.where(cellf < n_cells, cand, -jnp.inf)
        best, eidx = [], []
        for _ in range(P_TOPK):
            m = jnp.max(cand, axis=0, keepdims=True)
            j = jnp.min(jnp.where(cand == m, cellf, float(PK_CELL_ROWS)), axis=0, keepdims=True)
            hit = cellf == j
            eidx.append(jnp.max(jnp.where(hit, cidx, -1.0), axis=0, keepdims=True))
            best.append(m)
            cand = jnp.where(hit, -jnp.inf, cand)
        best = jnp.concatenate(best, axis=0)
        e = jnp.exp(best - best[0:1, :])
        row0 = pl.multiple_of(h * P_TOPK, P_TOPK)
        gw_s[pl.ds(row0, P_TOPK), :] = e / jnp.sum(e, axis=0, keepdims=True)
        idx_s[pl.ds(row0, P_TOPK), :] = jnp.concatenate(eidx, axis=0)
        return carry

    lax.fori_loop(0, P_HEADS, head, 0)
    idx_ref[...] = idx_s[...].T.astype(jnp.int32)
    gw_ref[...] = gw_s[...].T


def _select_call(qp, keys_bf16):
    n = qp.shape[0]
    tm = min(LANES, n)
    e0, e1, n_cells = _pk_expand_mats()
    full = lambda shape: pl.BlockSpec(shape, lambda i: (0,) * len(shape))
    return pl.pallas_call(
        functools.partial(_select_kernel, n_cells),
        grid=(n // tm,),
        in_specs=[pl.BlockSpec((tm, qp.shape[1]), lambda i: (i, 0)), full(keys_bf16.shape),
                  full((PK_CELL_ROWS, LANES)), full((PK_CELL_ROWS, LANES))],
        out_specs=[pl.BlockSpec((tm, LANES), lambda i: (i, 0)), pl.BlockSpec((tm, LANES), lambda i: (i, 0))],
        out_shape=[jax.ShapeDtypeStruct((n, LANES), jnp.int32), jax.ShapeDtypeStruct((n, LANES), F32)],
        scratch_shapes=[pltpu.VMEM((P_HEADS * P_TOPK, tm), F32), pltpu.VMEM((P_HEADS * P_TOPK, tm), F32)],
        compiler_params=_cparams("parallel"),
        name="peer_select",
    )(qp, keys_bf16, jnp.asarray(e0), jnp.asarray(e1))


N_SEL = P_HEADS * P_TOPK

SC_CORES = 2
SC_SUBCORES = 16
SC_LANES = 16
SC_TOK_BLOCK = 8
SC_ROWS = 32
SC_ROW_BLOCK = 16
SC_ACC_CHAINS = 4
SC_ACC_ROWS = 16
SC_NBUF = 4
N_WCOL = D_MODEL // (2 * SC_LANES)


def _pack_kernel(t_ref, o_ref):
    o_ref[...] = _pack_words(t_ref[...])


def _pack_table(tables, layer):
    _, e, dcol = tables.shape
    tm = min(1024, e)
    return pl.pallas_call(
        _pack_kernel,
        grid=(e // tm,),
        in_specs=[pl.BlockSpec((None, tm, dcol), lambda i: (layer, i, 0))],
        out_specs=pl.BlockSpec((tm, dcol // 2), lambda i: (i, 0)),
        out_shape=jax.ShapeDtypeStruct((e, dcol // 2), jnp.int32),
        compiler_params=_cparams("parallel"),
        name="pack_table",
    )(tables)


def _sc_gelu(x):
    z = math.sqrt(2.0 / math.pi) * (x + 0.044715 * (x * x * x))
    t = 1.0 - 2.0 / (jnp.exp(2.0 * z) + 1.0)
    return x * (0.5 * (1.0 + t))


def _sc_expert_body(tokens_per_worker, idx_hbm, gw_hbm, h2_hbm, u_hbm, v_hbm, y_hbm,
                    idx_v, gw_v, x_v, o_v, buf, coef_v, tr_v, sem, in_sem, out_sem):
    wid = lax.axis_index("s") * SC_CORES + lax.axis_index("c")
    base = wid * tokens_per_worker
    lane = lax.iota(jnp.int32, SC_LANES)
    n_gather = N_SEL // SC_ROWS
    zero = jnp.zeros((SC_LANES,), F32)

    n_steps = 2 * n_gather
    assert n_steps % SC_NBUF == 0

    def gather(p, tt, i):
        table = u_hbm if i < n_gather else v_hbm
        j = i % n_gather
        slot = i % SC_NBUF
        return pltpu.make_async_copy(table.at[idx_v.at[p, tt, pl.ds(j * SC_ROWS, SC_ROWS)]], buf.at[slot],
                                     sem.at[slot])

    def unpack(w):
        lo = lax.bitcast_convert_type(lax.shift_left(w, jnp.full((SC_LANES,), 16, jnp.int32)), F32)
        hi = lax.bitcast_convert_type(w & jnp.full((SC_LANES,), -65536, jnp.int32), F32)
        return lo, hi

    def packed(w):
        return plsc.bitcast(w, BF16)

    def unpack_sum(s):
        return unpack(plsc.bitcast(s, jnp.int32))

    def act_chunk(p, tt, j, slot):
        @pl.loop(0, SC_ROWS // SC_LANES)
        def _(half):
            for rb in range(SC_LANES // SC_ROW_BLOCK):
                r0 = half * SC_LANES + rb * SC_ROW_BLOCK

                def col(c, accs):
                    w0 = pl.multiple_of(c * (2 * SC_LANES), 2 * SC_LANES)
                    xa = packed(x_v[p, tt, pl.ds(w0, SC_LANES)])
                    xb = packed(x_v[p, tt, pl.ds(w0 + SC_LANES, SC_LANES)])
                    out = []
                    for r, a in enumerate(accs):
                        ua = packed(buf[slot, r0 + r, pl.ds(w0, SC_LANES)])
                        ub = packed(buf[slot, r0 + r, pl.ds(w0 + SC_LANES, SC_LANES)])
                        lo, hi = unpack_sum(ua * xa + ub * xb)
                        out.append(a + lo + hi)
                    return tuple(out)

                accs = lax.fori_loop(0, N_WCOL // 2, col, (zero,) * SC_ROW_BLOCK)
                for r in range(SC_ROW_BLOCK):
                    tr_v[pl.ds((rb * SC_ROW_BLOCK + r) * SC_LANES, SC_LANES)] = accs[r]
            tot = zero
            for jj in range(SC_LANES):
                tot = tot + plsc.load_gather(tr_v, [lane * SC_LANES + jj])
            k0 = pl.multiple_of(j * SC_ROWS + half * SC_LANES, SC_LANES)
            coef_v[pl.ds(k0, SC_LANES)] = gw_v[p, tt, pl.ds(k0, SC_LANES)] * _sc_gelu(tot)

    def acc_chunk(p, tt, j, slot, first):
        def tree_sum(parts):
            while len(parts) > 1:
                parts = [parts[i] + parts[i + 1] for i in range(0, len(parts), 2)]
            return parts[0]

        for rb in range(SC_ROWS // SC_ACC_ROWS):
            rows = list(range(rb * SC_ACC_ROWS, (rb + 1) * SC_ACC_ROWS))
            splat = {}
            for r in rows:
                c16 = plsc.load_gather(coef_v, [jnp.full((SC_LANES,), j * SC_ROWS + r, jnp.int32)])
                splat[r] = plsc.pack(c16, c16, format=plsc.PackFormat.INTERLEAVED)
            fresh = first and rb == 0

            @plsc.parallel_loop(0, N_WCOL)
            def _(c):
                woff = pl.multiple_of(c * SC_LANES, SC_LANES)
                hoff = pl.multiple_of(c * SC_LANES + D_MODEL // 2, SC_LANES)
                pa, pb = [], []
                for n in range(0, SC_ACC_ROWS, 2):
                    r0, r1 = rows[n], rows[n + 1]
                    s = (splat[r0] * packed(buf[slot, r0, pl.ds(woff, SC_LANES)])
                         + splat[r1] * packed(buf[slot, r1, pl.ds(woff, SC_LANES)]))
                    lo, hi = unpack_sum(s)
                    if n // 2 < SC_ACC_CHAINS:
                        pa.append(lo)
                        pb.append(hi)
                    else:
                        pa[(n // 2) % SC_ACC_CHAINS] = pa[(n // 2) % SC_ACC_CHAINS] + lo
                        pb[(n // 2) % SC_ACC_CHAINS] = pb[(n // 2) % SC_ACC_CHAINS] + hi
                sa, sb = tree_sum(pa), tree_sum(pb)
                if not fresh:
                    sa = sa + o_v[p, tt, pl.ds(woff, SC_LANES)]
                    sb = sb + o_v[p, tt, pl.ds(hoff, SC_LANES)]
                o_v[p, tt, pl.ds(woff, SC_LANES)] = sa
                o_v[p, tt, pl.ds(hoff, SC_LANES)] = sb

    n_blocks = tokens_per_worker // SC_TOK_BLOCK

    def block_start(blk):
        return pl.multiple_of(base + blk * SC_TOK_BLOCK, SC_TOK_BLOCK)

    def in_copies(blk, p):
        rows = pl.ds(block_start(blk), SC_TOK_BLOCK)
        return [pltpu.make_async_copy(idx_hbm.at[rows], idx_v.at[p], in_sem.at[p]),
                pltpu.make_async_copy(gw_hbm.at[rows], gw_v.at[p], in_sem.at[p]),
                pltpu.make_async_copy(h2_hbm.at[rows], x_v.at[p], in_sem.at[p])]

    def out_copy(blk, p):
        return pltpu.make_async_copy(o_v.at[p], y_hbm.at[pl.ds(block_start(blk), SC_TOK_BLOCK)], out_sem.at[p])

    ahead = SC_NBUF - 1
    for cp in in_copies(0, 0):
        cp.start()
    for cp in in_copies(0, 0):
        cp.wait()
    for cp in in_copies(1, 1):
        cp.start()
    for i in range(ahead):
        gather(0, 0, i).start()

    @pl.loop(0, n_blocks)
    def _(blk):
        p = lax.rem(blk, 2)

        @pl.when(blk >= 2)
        def _():
            out_copy(blk - 2, p).wait()

        @pl.loop(0, SC_TOK_BLOCK)
        def _(tt):
            for i in range(n_steps):
                if i + ahead < n_steps:
                    gather(p, tt, i + ahead).start()
                else:
                    nxt = i + ahead - n_steps

                    @pl.when(tt + 1 < SC_TOK_BLOCK)
                    def _():
                        gather(p, tt + 1, nxt).start()

                    @pl.when(jnp.logical_and(tt + 1 == SC_TOK_BLOCK, blk + 1 < n_blocks))
                    def _():
                        if nxt == 0:
                            for cp in in_copies(blk + 1, 1 - p):
                                cp.wait()
                        gather(1 - p, 0, nxt).start()
                gather(p, tt, i).wait()
                if i < n_gather:
                    act_chunk(p, tt, i, i % SC_NBUF)
                else:
                    acc_chunk(p, tt, i - n_gather, i % SC_NBUF, i == n_gather)

        out_copy(blk, p).start()

        @pl.when(blk + 2 < n_blocks)
        def _():
            for cp in in_copies(blk + 2, p):
                cp.start()

    for blk in (n_blocks - 2, n_blocks - 1):
        out_copy(blk, blk % 2).wait()


def _sc_expert_call(idx, gw, xw, u, v):
    n = idx.shape[0]
    workers = SC_CORES * SC_SUBCORES
    assert n % (workers * SC_TOK_BLOCK) == 0 and n // (workers * SC_TOK_BLOCK) >= 2
    mesh = plsc.VectorSubcoreMesh(core_axis_name="c", subcore_axis_name="s")
    return pl.kernel(
        functools.partial(_sc_expert_body, n // workers),
        out_type=jax.ShapeDtypeStruct((n, D_MODEL), F32),
        mesh=mesh,
        scratch_types=[pltpu.VMEM((2, SC_TOK_BLOCK, N_SEL), jnp.int32),
                       pltpu.VMEM((2, SC_TOK_BLOCK, N_SEL), F32),
                       pltpu.VMEM((2, SC_TOK_BLOCK, D_MODEL // 2), jnp.int32),
                       pltpu.VMEM((2, SC_TOK_BLOCK, D_MODEL), F32),
                       pltpu.VMEM((SC_NBUF, SC_ROWS, D_MODEL // 2), jnp.int32),
                       pltpu.VMEM((N_SEL,), F32),
                       pltpu.VMEM((SC_LANES * SC_LANES,), F32),
                       pltpu.SemaphoreType.DMA((SC_NBUF,)), pltpu.SemaphoreType.DMA((2,)),
                       pltpu.SemaphoreType.DMA((2,))],
        compiler_params=pltpu.CompilerParams(needs_layout_passes=False),
        name="peer_experts_sc",
    )(idx, gw, xw, u, v)


def _resid_kernel(x_ref, y_ref, g_ref, o_ref):
    o_ref[...] = x_ref[...] + g_ref[...] * y_ref[...]


def _resid_call(x, y, g2, per_token, tokens_per_req):
    n = x.shape[0]
    tm = min(512, n if per_token else tokens_per_req)
    row = pl.BlockSpec((tm, D_MODEL), lambda i: (i, 0))
    return pl.pallas_call(
        _resid_kernel,
        grid=(n // tm,),
        in_specs=[row, row, _mod_spec(per_token, tm, tokens_per_req)],
        out_specs=row,
        out_shape=jax.ShapeDtypeStruct((n, D_MODEL), F32),
        compiler_params=_cparams("parallel"),
        name="peer_residual",
    )(x, y, g2)


def _final_kernel(x_ref, g_ref, o_ref):
    x = x_ref[...]
    o_ref[...] = x * lax.rsqrt(jnp.mean(x * x, axis=-1, keepdims=True) + EPS) * g_ref[...]


def _final_call(x, g):
    n = x.shape[0]
    tm = min(512, n)
    return pl.pallas_call(
        _final_kernel,
        grid=(n // tm,),
        in_specs=[pl.BlockSpec((tm, D_MODEL), lambda i: (i, 0)), pl.BlockSpec((1, D_MODEL), lambda i: (0, 0))],
        out_specs=pl.BlockSpec((tm, D_MODEL), lambda i: (i, 0)),
        out_shape=jax.ShapeDtypeStruct((n, D_MODEL), F32),
        compiler_params=_cparams("parallel"),
        name="final_norm",
    )(x, g)


def _split_w_in(w_in_l, gate_b_l):
    cuts = np.cumsum([A_WIDTH + 2 * A_KV_WIDTH, 2 * M_WIDTH, M_WIDTH, M_WIDTH]).tolist()
    wa = w_in_l[:, :cuts[0]].astype(BF16)
    wqk = w_in_l[:, cuts[0]:cuts[1]].astype(BF16)
    wv = w_in_l[:, cuts[1]:cuts[2]].astype(BF16)
    wo = w_in_l[:, cuts[2]:cuts[3]].astype(BF16)
    ng = 2 * M_HEADS
    wg = jnp.pad(w_in_l[:, cuts[3]:], ((0, 0), (0, LANES - ng))).astype(BF16)
    gb = jnp.pad(gate_b_l.astype(F32), (0, LANES - ng)).reshape(1, LANES)
    return wa, wqk, wv, wo, wg, gb


def _layer(x, mods, per_token, batch, seq, t_valid, lw, bias_p, bias_c, bias_n, kv_cache, conv0, state,
           after=None):
    (norm_mix, norm_ffn, w_in, conv_w, conv_b, gate_b, sinks, m_norm, w_out, peer_query, peer_keys,
     peer_u, peer_v) = lw
    if after is not None:
        x, _ = lax.optimization_barrier((x, after))
    sh1, sc1, g1, sh2, sc2, g2 = mods
    wa, wqk, wv, wo, wg, gb = _split_w_in(w_in, gate_b)
    qkv, qkm, vm, om, gc = _in_call(x, sc1, sh1, norm_mix.reshape(1, -1), wa, wqk, wv, wo, wg, gb,
                                    per_token, seq)
    if kv_cache is None:
        att = _attn_p_call(qkv, bias_p, sinks, batch, seq)
        kv3 = qkv.reshape(batch, seq, -1)
        new_k = kv3[:, seq - WINDOW:, A_WIDTH:A_WIDTH + A_KV_WIDTH]
        new_v = kv3[:, seq - WINDOW:, A_WIDTH + A_KV_WIDTH:]
        chunk = M_CHUNK
    else:
        att, new_k, new_v = _attn_s_call(qkv, kv_cache[0], kv_cache[1], bias_c, bias_n, sinks, t_valid)
        chunk = seq
    c0, n0, m0 = state
    mo, c_new, n_new, m_new = _mlstm_call(qkm, vm, om, gc, conv_w, conv_b.reshape(1, -1),
                                          m_norm.reshape(1, -1), conv0, c0, n0, m0,
                                          batch, seq, chunk, min(t_valid, chunk))
    new_conv = qkm.reshape(batch, seq, -1)[:, t_valid - (CONV_W - 1):t_valid]
    pad_rows = seq - t_valid
    if pad_rows:
        keep = lambda a: a.reshape(batch, seq, -1)[:, :t_valid].reshape(batch * t_valid, -1)
        att, mo, x, g1, sc2, sh2, g2 = (keep(a) for a in (att, mo, x, g1, sc2, sh2, g2))
    x_mid, xw, qp = _out_call(att, mo, x, g1, sc2, sh2, norm_ffn.reshape(1, -1),
                              w_out[:A_WIDTH].astype(BF16), w_out[A_WIDTH:].astype(BF16),
                              peer_query.astype(BF16), per_token, t_valid)
    idx, gw = _select_call(qp, peer_keys.astype(BF16))
    y = _sc_expert_call(idx, gw, xw, peer_u, peer_v)
    x_new = _resid_call(x_mid, y, g2, per_token, t_valid)
    if pad_rows:
        x_new = jnp.pad(x_new.reshape(batch, t_valid, -1), ((0, 0), (0, pad_rows), (0, 0))).reshape(batch * seq, -1)
    new_k = new_k.reshape(batch, WINDOW, A_KV_HEADS, A_HEAD_DIM)
    new_v = new_v.reshape(batch, WINDOW, A_KV_HEADS, A_HEAD_DIM)
    return x_new, (new_k, new_v, new_conv, c_new, n_new, m_new[:, :M_HEADS, 0]), idx


def _prompt_group_sizes(n_req):
    if n_req < PROMPT_GROUPS:
        return [1] * n_req
    mid, n_mid = n_req - 2, PROMPT_GROUPS - 2
    weights = [i + 2 for i in range(n_mid)]
    sizes = [max(1, mid * w // sum(weights)) for w in weights]
    for i in range(mid - sum(sizes)):
        sizes[n_mid - 1 - i % n_mid] += 1
    return [1] + sizes + [1]


def kernel(x_prompt, x_sample, c_prompt, c_sample, cache_k, cache_v, state_conv, state_C, state_n, state_m, rel_bias, w_ada, b_ada, norm_mix, norm_ffn, w_in, conv_w, conv_b, gate_b, attn_sinks, m_norm, w_out, peer_query, peer_keys, peer_u, peer_v, norm_final):
    depth = w_ada.shape[0]
    bp, tp, d = x_prompt.shape
    bs, ts, _ = x_sample.shape
    assert tp % WINDOW == 0 and tp % M_CHUNK == 0 and ts <= SAMPLE_PAD and ts >= CONV_W - 1

    mod_all = _ada_call(jnp.concatenate([c_prompt, c_sample], axis=0), w_ada, b_ada)

    qi = np.arange(WINDOW)[:, None]
    bias_p = _bias_call(rel_bias, qi + WINDOW - np.arange(2 * WINDOW)[None, :])
    qs = np.arange(SAMPLE_PAD)[:, None]
    bias_c = _bias_call(rel_bias, qs + WINDOW - np.arange(WINDOW)[None, :])
    bias_n = _bias_call(rel_bias, qs - np.arange(SAMPLE_PAD)[None, :])

    sizes = _prompt_group_sizes(bp)
    starts = np.cumsum([0] + sizes).tolist()
    xg = [x_prompt[starts[g]:starts[g + 1]].reshape(sizes[g] * tp, d) for g in range(len(sizes))]
    xs = jnp.pad(x_sample, ((0, 0), (0, SAMPLE_PAD - ts), (0, 0))).reshape(bs * SAMPLE_PAD, d)
    halo_pad = ((0, 0), (SUBLANES - (CONV_W - 1), 0), (0, 0))

    st_p, st_s = [], []
    for l in range(depth):
        lw = (norm_mix[l], norm_ffn[l], w_in[l], conv_w[l], conv_b[l], gate_b[l], attn_sinks[l], m_norm[l],
              w_out[l], peer_query[l], peer_keys[l], _pack_table(peer_u, l), _pack_table(peer_v, l))
        mod_s = [jnp.repeat(m, SAMPLE_PAD, axis=0) for m in jnp.split(mod_all[l, bp:], 6, axis=-1)]
        sp_groups = []
        for g, bg in enumerate(sizes):
            mod_g = [m.reshape(bg, 1, d) for m in jnp.split(mod_all[l, starts[g]:starts[g + 1]], 6, axis=-1)]
            zero_state = (jnp.zeros((bg, M_HEADS, M_HEAD_DIM, M_HEAD_DIM), F32),
                          jnp.zeros((bg, M_HEADS, M_HEAD_DIM), F32),
                          jnp.zeros((bg, SUBLANES, LANES), F32))
            zero_conv = jnp.zeros((bg, SUBLANES, 2 * M_WIDTH), F32)
            xg[g], sp, last_idx = _layer(xg[g], mod_g, False, bg, tp, tp, lw, bias_p, None, None, None,
                                         zero_conv, zero_state)
            sp_groups.append(sp)
        st_p.append([jnp.concatenate([sp[i] for sp in sp_groups], axis=0) for i in range(6)])
        state_s = (state_C[l].astype(F32), state_n[l].astype(F32),
                   jnp.broadcast_to(jnp.pad(state_m[l].astype(F32), ((0, 0), (0, SUBLANES - M_HEADS)))[:, :, None],
                                    (bs, SUBLANES, LANES)))
        kv_cache = (cache_k[l].reshape(bs, WINDOW, A_KV_WIDTH), cache_v[l].reshape(bs, WINDOW, A_KV_WIDTH))
        xs, ss, _ = _layer(xs, mod_s, True, bs, SAMPLE_PAD, ts, lw, None, bias_c, bias_n, kv_cache,
                           jnp.pad(state_conv[l].astype(F32), halo_pad), state_s, after=last_idx)
        st_s.append(ss)

    gfin = norm_final.reshape(1, d)
    y_prompt = jnp.concatenate([_final_call(x, gfin).reshape(bg, tp, d) for x, bg in zip(xg, sizes)], axis=0)
    y_sample = _final_call(xs, gfin).reshape(bs, SAMPLE_PAD, d)[:, :ts]
    outs_p = [jnp.stack([s[i] for s in st_p]) for i in range(6)]
    outs_s = [jnp.stack([s[i] for s in st_s]) for i in range(6)]
    return (y_prompt, y_sample, *outs_p, *outs_s)
```

```python
import functools
import math

import numpy as np
import jax
import jax.numpy as jnp
from jax import lax
from jax.experimental import pallas as pl
from jax.experimental.pallas import tpu as pltpu
from jax.experimental.pallas import tpu_sc as plsc

F32 = jnp.float32
BF16 = jnp.bfloat16
HIGHEST = lax.Precision.HIGHEST

D_MODEL = 1024
A_HEADS = 8
A_KV_HEADS = 2
A_GROUP = A_HEADS // A_KV_HEADS
A_HEAD_DIM = 64
A_WIDTH = A_HEADS * A_HEAD_DIM
A_KV_WIDTH = A_KV_HEADS * A_HEAD_DIM
WINDOW = 128
ATT_SCALE = A_HEAD_DIM ** -0.5
N_BUCKETS = 32
MAX_DISTANCE = WINDOW
M_HEADS = 4
M_HEAD_DIM = 128
M_WIDTH = M_HEADS * M_HEAD_DIM
CONV_W = 4
M_CHUNK = 64
N_KEYS = 128
P_HEADS = 8
P_TOPK = 16
P_KEY_DIM = 256
P_HALF = P_KEY_DIM // 2
EPS = 1e-6
NEG_INF = -1e30

LANES = 128
SUBLANES = 8
SAMPLE_PAD = SUBLANES
VMEM_LIMIT = 48 * 1024 * 1024
PROMPT_GROUPS = 7

NT_DIMS = (((1,), (1,)), ((), ()))
TN_DIMS = (((0,), (0,)), ((), ()))


def _cparams(*sem):
    return pltpu.CompilerParams(dimension_semantics=sem, vmem_limit_bytes=VMEM_LIMIT)


def _bdot(a, b):
    return jnp.dot(a.astype(BF16), b.astype(BF16), preferred_element_type=F32)


def _bdot_nt(a, b):
    return lax.dot_general(a.astype(BF16), b.astype(BF16), NT_DIMS, preferred_element_type=F32)


def _sigmoid(x):
    return 1.0 / (1.0 + jnp.exp(-x))


def _log_sigmoid(x):
    return jnp.minimum(x, 0.0) - jnp.log1p(jnp.exp(-jnp.abs(x)))


def _ada_kernel(c_ref, w_ref, b_ref, o_ref):
    c = c_ref[...]
    s = c * _sigmoid(c)
    o_ref[...] = jnp.dot(s, w_ref[...], precision=HIGHEST, preferred_element_type=F32) + b_ref[...]


def _ada_call(c_all, w_ada, b_ada):
    depth, d, n6 = w_ada.shape
    rows = c_all.shape[0]
    bn = 1024
    return pl.pallas_call(
        _ada_kernel,
        grid=(depth, n6 // bn),
        in_specs=[
            pl.BlockSpec((rows, d), lambda l, j: (0, 0)),
            pl.BlockSpec((None, d, bn), lambda l, j: (l, 0, j)),
            pl.BlockSpec((None, 1, bn), lambda l, j: (l, 0, j)),
        ],
        out_specs=pl.BlockSpec((None, rows, bn), lambda l, j: (l, 0, j)),
        out_shape=jax.ShapeDtypeStruct((depth, rows, n6), F32),
        compiler_params=_cparams("parallel", "parallel"),
        name="ada_mod",
    )(c_all, w_ada, b_ada.reshape(depth, 1, n6))


def _mod_spec(per_token, tm, tokens_per_req):
    if per_token:
        return pl.BlockSpec((tm, D_MODEL), lambda i: (i, 0))
    tiles = tokens_per_req // tm
    return pl.BlockSpec((None, 1, D_MODEL), lambda i: (i // tiles, 0, 0))


def _in_kernel(x_ref, sc_ref, sh_ref, g_ref, wa_ref, wqk_ref, wv_ref, wo_ref, wg_ref, gb_ref,
               qkv_ref, qkm_ref, v_ref, o_ref, gc_ref):
    x = x_ref[...]
    y = x * lax.rsqrt(jnp.mean(x * x, axis=-1, keepdims=True) + EPS) * g_ref[...]
    h = (y * (1.0 + sc_ref[...]) + sh_ref[...]).astype(BF16)
    qkv_ref[...] = jnp.dot(h, wa_ref[...], preferred_element_type=F32)
    qkm_ref[...] = jnp.dot(h, wqk_ref[...], preferred_element_type=F32)
    v_ref[...] = jnp.dot(h, wv_ref[...], preferred_element_type=F32)
    o_ref[...] = jnp.dot(h, wo_ref[...], preferred_element_type=F32)
    g = jnp.dot(h, wg_ref[...], preferred_element_type=F32) + gb_ref[...]
    lane = lax.broadcasted_iota(jnp.int32, g.shape, 1)
    gc_ref[...] = jnp.where(lane < M_HEADS, g, jnp.where(lane < 2 * M_HEADS, _log_sigmoid(g), 0.0))


def _in_call(x, sc, sh, gnorm, wa, wqk, wv, wo, wg, gb, per_token, tokens_per_req):
    n = x.shape[0]
    tm = min(512, n if per_token else tokens_per_req)
    mod = _mod_spec(per_token, tm, tokens_per_req)
    full = lambda shape: pl.BlockSpec(shape, lambda i: (0,) * len(shape))
    row = lambda w: pl.BlockSpec((tm, w), lambda i: (i, 0))
    return pl.pallas_call(
        _in_kernel,
        grid=(n // tm,),
        in_specs=[row(D_MODEL), mod, mod, full((1, D_MODEL)), full(wa.shape), full(wqk.shape),
                  full(wv.shape), full(wo.shape), full(wg.shape), full((1, LANES))],
        out_specs=[row(wa.shape[1]), row(wqk.shape[1]), row(wv.shape[1]), row(wo.shape[1]), row(LANES)],
        out_shape=[jax.ShapeDtypeStruct((n, w), F32)
                   for w in (wa.shape[1], wqk.shape[1], wv.shape[1], wo.shape[1], LANES)],
        compiler_params=_cparams("parallel"),
        name="in_proj",
    )(x, sc, sh, gnorm, wa, wqk, wv, wo, wg, gb)


def _t5_bucket_np(dist):
    n = np.maximum(dist, 0)
    max_exact = N_BUCKETS // 2
    nf = np.maximum(n, 1).astype(np.float64)
    large = max_exact + (np.log(nf / max_exact) / math.log(MAX_DISTANCE / max_exact)
                         * (N_BUCKETS - max_exact)).astype(np.int32)
    return np.where(n < max_exact, n, np.minimum(large, N_BUCKETS - 1)).astype(np.int32)


def _bias_kernel(bucket_ref, rel_ref, o_ref):
    bucket = bucket_ref[...]
    for h in range(A_HEADS):
        acc = jnp.zeros(bucket.shape, F32)
        for b in range(N_BUCKETS):
            acc = jnp.where(bucket == b, rel_ref[b, h], acc)
        o_ref[h] = acc


def _bias_call(rel_bias, dist):
    bucket = jnp.asarray(_t5_bucket_np(dist))
    nq, nk = dist.shape
    return pl.pallas_call(
        _bias_kernel,
        in_specs=[pl.BlockSpec((nq, nk), lambda: (0, 0)),
                  pl.BlockSpec(memory_space=pltpu.SMEM)],
        out_specs=pl.BlockSpec((A_HEADS, nq, nk), lambda: (0, 0, 0)),
        out_shape=jax.ShapeDtypeStruct((A_HEADS, nq, nk), F32),
        name="t5_bias",
    )(bucket, rel_bias)


def _attn_p_kernel(q_ref, kp_ref, kc_ref, vp_ref, vc_ref, bias_ref, sink_ref, o_ref):
    i = pl.program_id(1)
    qi = lax.broadcasted_iota(jnp.int32, (WINDOW, WINDOW), 0)
    kj = lax.broadcasted_iota(jnp.int32, (WINDOW, WINDOW), 1)
    valid_prev = jnp.logical_and(kj > qi, i > 0)
    valid_cur = kj <= qi
    q = q_ref[...]
    for h in range(A_HEADS):
        kv = h // A_GROUP
        qh = q[:, h * A_HEAD_DIM:(h + 1) * A_HEAD_DIM]
        sl = slice(kv * A_HEAD_DIM, (kv + 1) * A_HEAD_DIM)
        bias = bias_ref[h]
        sp = _bdot_nt(qh, kp_ref[:, sl]) * ATT_SCALE + bias[:, :WINDOW]
        sc = _bdot_nt(qh, kc_ref[:, sl]) * ATT_SCALE + bias[:, WINDOW:]
        sp = jnp.where(valid_prev, sp, NEG_INF)
        sc = jnp.where(valid_cur, sc, NEG_INF)
        sink = sink_ref[0, h]
        mx = jnp.maximum(jnp.maximum(jnp.max(sp, axis=-1, keepdims=True),
                                     jnp.max(sc, axis=-1, keepdims=True)), sink)
        ep = jnp.exp(sp - mx)
        ec = jnp.exp(sc - mx)
        den = (jnp.sum(ep, axis=-1, keepdims=True) + jnp.sum(ec, axis=-1, keepdims=True)
               + jnp.exp(sink - mx))
        o = _bdot(ep / den, vp_ref[:, sl]) + _bdot(ec / den, vc_ref[:, sl])
        o_ref[:, h * A_HEAD_DIM:(h + 1) * A_HEAD_DIM] = o


def _attn_p_call(qkv, bias, sinks, batch, seq):
    nb = seq // WINDOW
    n = batch * seq
    kcol = A_WIDTH // A_KV_WIDTH
    vcol = kcol + 1
    cur = lambda col: pl.BlockSpec((WINDOW, A_KV_WIDTH), lambda b, i: (b * nb + i, col))
    prev = lambda col: pl.BlockSpec((WINDOW, A_KV_WIDTH),
                                    lambda b, i: (b * nb + jnp.maximum(i - 1, 0), col))
    return pl.pallas_call(
        _attn_p_kernel,
        grid=(batch, nb),
        in_specs=[pl.BlockSpec((WINDOW, A_WIDTH), lambda b, i: (b * nb + i, 0)),
                  prev(kcol), cur(kcol), prev(vcol), cur(vcol),
                  pl.BlockSpec((A_HEADS, WINDOW, 2 * WINDOW), lambda b, i: (0, 0, 0)),
                  pl.BlockSpec(memory_space=pltpu.SMEM)],
        out_specs=pl.BlockSpec((WINDOW, A_WIDTH), lambda b, i: (b * nb + i, 0)),
        out_shape=jax.ShapeDtypeStruct((n, A_WIDTH), F32),
        compiler_params=_cparams("parallel", "parallel"),
        name="swa_prompt",
    )(qkv, qkv, qkv, qkv, qkv, bias, sinks.reshape(1, A_HEADS))


def _attn_s_kernel(n_new, qkv_ref, ck_ref, cv_ref, bc_ref, bn_ref, sink_ref,
                   o_ref, nk_ref, nv_ref, kk_s, vv_s):
    qkv = qkv_ref[...]
    knew = qkv[:, A_WIDTH:A_WIDTH + A_KV_WIDTH]
    vnew = qkv[:, A_WIDTH + A_KV_WIDTH:A_WIDTH + 2 * A_KV_WIDTH]
    ck = ck_ref[...]
    cv = cv_ref[...]
    rows = A_GROUP * SAMPLE_PAD
    qi = lax.broadcasted_iota(jnp.int32, (rows, WINDOW), 0) % SAMPLE_PAD
    kj = lax.broadcasted_iota(jnp.int32, (rows, WINDOW), 1)
    valid_c = kj > qi
    rcol = lax.broadcasted_iota(jnp.int32, (rows, 1), 0)
    qcol = rcol % SAMPLE_PAD
    for kv in range(A_KV_HEADS):
        heads = range(kv * A_GROUP, (kv + 1) * A_GROUP)
        sl = slice(kv * A_HEAD_DIM, (kv + 1) * A_HEAD_DIM)
        qs = jnp.concatenate([qkv[:, h * A_HEAD_DIM:(h + 1) * A_HEAD_DIM] for h in heads], axis=0)
        bias_c = jnp.concatenate([bc_ref[h] for h in heads], axis=0)
        bias_n = jnp.concatenate([bn_ref[h] for h in heads], axis=0)
        sink = jnp.zeros((rows, 1), F32)
        for g, h in enumerate(heads):
            sink = jnp.where(rcol // SAMPLE_PAD == g, sink_ref[0, h], sink)
        s_c = _bdot_nt(qs, ck[:, sl]) * ATT_SCALE + bias_c
        s_c = jnp.where(valid_c, s_c, NEG_INF)
        s_n = []
        for j in range(n_new):
            sj = jnp.sum(qs * knew[j:j + 1, sl], axis=-1, keepdims=True) * ATT_SCALE + bias_n[:, j:j + 1]
            s_n.append(jnp.where(qcol >= j, sj, NEG_INF))
        mx = jnp.maximum(jnp.max(s_c, axis=-1, keepdims=True), sink)
        for sj in s_n:
            mx = jnp.maximum(mx, sj)
        e_c = jnp.exp(s_c - mx)
        den = jnp.sum(e_c, axis=-1, keepdims=True) + jnp.exp(sink - mx)
        o = _bdot(e_c, cv[:, sl])
        for j, sj in enumerate(s_n):
            ej = jnp.exp(sj - mx)
            den = den + ej
            o = o + ej * vnew[j:j + 1, sl]
        o = o / den
        for g, h in enumerate(heads):
            o_ref[:, h * A_HEAD_DIM:(h + 1) * A_HEAD_DIM] = o[g * SAMPLE_PAD:(g + 1) * SAMPLE_PAD, :]
    kk_s[0:WINDOW, :] = ck
    kk_s[WINDOW:WINDOW + SAMPLE_PAD, :] = knew
    vv_s[0:WINDOW, :] = cv
    vv_s[WINDOW:WINDOW + SAMPLE_PAD, :] = vnew
    nk_ref[...] = kk_s[n_new:n_new + WINDOW, :]
    nv_ref[...] = vv_s[n_new:n_new + WINDOW, :]


def _attn_s_call(qkv, ck, cv, bias_c, bias_n, sinks, n_new):
    nreq = ck.shape[0]
    wq = qkv.shape[1]
    full3 = lambda shape: pl.BlockSpec(shape, lambda b: (0, 0, 0))
    cache = pl.BlockSpec((None, WINDOW, A_KV_WIDTH), lambda b: (b, 0, 0))
    return pl.pallas_call(
        functools.partial(_attn_s_kernel, n_new),
        grid=(nreq,),
        in_specs=[pl.BlockSpec((SAMPLE_PAD, wq), lambda b: (b, 0)), cache, cache,
                  full3(bias_c.shape), full3(bias_n.shape),
                  pl.BlockSpec(memory_space=pltpu.SMEM)],
        out_specs=[pl.BlockSpec((SAMPLE_PAD, A_WIDTH), lambda b: (b, 0)), cache, cache],
        out_shape=[jax.ShapeDtypeStruct((nreq * SAMPLE_PAD, A_WIDTH), F32),
                   jax.ShapeDtypeStruct(ck.shape, F32), jax.ShapeDtypeStruct(cv.shape, F32)],
        scratch_shapes=[pltpu.VMEM((WINDOW + SAMPLE_PAD, A_KV_WIDTH), F32),
                        pltpu.VMEM((WINDOW + SAMPLE_PAD, A_KV_WIDTH), F32)],
        compiler_params=_cparams("parallel"),
        name="swa_sample",
    )(qkv, ck, cv, bias_c, bias_n, sinks.reshape(1, A_HEADS))


def _mlstm_kernel(chunk, t_valid, qk_ref, v_ref, og_ref, gc_ref, cw_ref, cb_ref, mn_ref,
                  conv0_ref, c0_ref, n0_ref, m0_ref,
                  out_ref, cout_ref, nout_ref, mout_ref,
                  xp_s, c_s, n_s, m_s):
    step = pl.program_id(1)
    halo = SUBLANES

    @pl.when(step == 0)
    def _():
        xp_s[0:halo, :] = conv0_ref[...]
        c_s[...] = c0_ref[...]
        n_s[...] = n0_ref[...]
        m_s[...] = m0_ref[...]

    xp_s[halo:halo + chunk, :] = qk_ref[...]
    cw = cw_ref[...]
    y = cb_ref[...]
    for i in range(CONV_W):
        off = halo - (CONV_W - 1) + i
        y = y + xp_s[off:off + chunk, :] * cw[i:i + 1, :]
    xp_s[0:halo, :] = xp_s[chunk:chunk + halo, :]
    y = y * _sigmoid(y)
    q_all = y[:, :M_WIDTH]
    k_all = y[:, M_WIDTH:] * (M_HEAD_DIM ** -0.5)

    g = gc_ref[...]
    if t_valid < chunk:
        row = lax.broadcasted_iota(jnp.int32, g.shape, 0)
        lane = lax.broadcasted_iota(jnp.int32, g.shape, 1)
        g = jnp.where(row < t_valid, g, jnp.where(lane < M_HEADS, NEG_INF, 0.0))
    tr = lax.broadcasted_iota(jnp.int32, (chunk, chunk), 0)
    tc = lax.broadcasted_iota(jnp.int32, (chunk, chunk), 1)
    causal = tr >= tc
    tri = causal.astype(F32)
    bcol = jnp.dot(tri, g, precision=HIGHEST, preferred_element_type=F32)
    er = lax.broadcasted_iota(jnp.int32, (SUBLANES, LANES), 0)
    ec = lax.broadcasted_iota(jnp.int32, (SUBLANES, LANES), 1)
    eye = (er == ec).astype(F32)
    g_rows = lax.dot_general(eye, g, NT_DIMS, precision=HIGHEST, preferred_element_type=F32)
    b_rows = lax.dot_general(eye, bcol, NT_DIMS, precision=HIGHEST, preferred_element_type=F32)

    for h in range(M_HEADS):
        hs = slice(h * M_HEAD_DIM, (h + 1) * M_HEAD_DIM)
        b_c = bcol[:, M_HEADS + h:M_HEADS + h + 1]
        ig_c = g[:, h:h + 1]
        b_r = b_rows[M_HEADS + h:M_HEADS + h + 1, :]
        ig_r = g_rows[h:h + 1, :]
        m_prev = m_s[h:h + 1, 0:1]
        logw = jnp.where(causal, b_c - b_r + ig_r, -jnp.inf)
        inter = b_c + m_prev
        m_t = jnp.maximum(inter, jnp.max(logw, axis=-1, keepdims=True))
        w = jnp.exp(logw - m_t)
        a = jnp.exp(inter - m_t)
        q = q_all[:, hs]
        k = k_all[:, hs]
        v = v_ref[:, hs]
        cmat = c_s[h]
        nvec = n_s[h:h + 1, :]
        wqk = w * _bdot_nt(q, k)
        num = _bdot(wqk, v) + a * _bdot_nt(q, cmat)
        den = jnp.sum(wqk, axis=-1, keepdims=True) + a * jnp.sum(q * nvec, axis=-1, keepdims=True)
        hh = num / jnp.maximum(jnp.abs(den), jnp.exp(-m_t))
        m_new = m_t[chunk - 1:chunk, :]
        b_last = b_c[chunk - 1:chunk, :]
        wl = jnp.exp(b_last - b_c + ig_c - m_new)
        al = jnp.exp(b_last + m_prev - m_new)
        c_s[h] = al * cmat + lax.dot_general((v * wl).astype(BF16), k.astype(BF16), TN_DIMS,
                                             preferred_element_type=F32)
        n_s[h:h + 1, :] = al * nvec + jnp.sum(wl * k, axis=0, keepdims=True)
        m_s[h:h + 1, :] = jnp.broadcast_to(m_new, (1, LANES))
        hn = hh * lax.rsqrt(jnp.mean(hh * hh, axis=-1, keepdims=True) + EPS) * mn_ref[:, hs]
        out_ref[:, hs] = _sigmoid(og_ref[:, hs]) * hn

    @pl.when(step == pl.num_programs(1) - 1)
    def _():
        cout_ref[...] = c_s[...]
        nout_ref[...] = n_s[...]
        mout_ref[...] = m_s[...]


def _mlstm_call(qk, v, og, gc, conv_w, conv_b, m_norm, conv0, c0, n0, m0, batch, seq, chunk, t_valid):
    nc = seq // chunk
    n = batch * seq
    row = lambda w: pl.BlockSpec((chunk, w), lambda b, c: (b * nc + c, 0))
    full2 = lambda shape: pl.BlockSpec(shape, lambda b, c: (0, 0))
    per_b = lambda shape: pl.BlockSpec((None,) + shape, lambda b, c: (b,) + (0,) * len(shape))
    dh = M_HEAD_DIM
    return pl.pallas_call(
        functools.partial(_mlstm_kernel, chunk, t_valid),
        grid=(batch, nc),
        in_specs=[row(2 * M_WIDTH), row(M_WIDTH), row(M_WIDTH), row(LANES),
                  full2((CONV_W, 2 * M_WIDTH)), full2((1, 2 * M_WIDTH)), full2((1, M_WIDTH)),
                  per_b((SUBLANES, 2 * M_WIDTH)), per_b((M_HEADS, dh, dh)), per_b((M_HEADS, dh)),
                  per_b((SUBLANES, LANES))],
        out_specs=[row(M_WIDTH), per_b((M_HEADS, dh, dh)), per_b((M_HEADS, dh)), per_b((SUBLANES, LANES))],
        out_shape=[jax.ShapeDtypeStruct((n, M_WIDTH), F32),
                   jax.ShapeDtypeStruct((batch, M_HEADS, dh, dh), F32),
                   jax.ShapeDtypeStruct((batch, M_HEADS, dh), F32),
                   jax.ShapeDtypeStruct((batch, SUBLANES, LANES), F32)],
        scratch_shapes=[pltpu.VMEM((SUBLANES + chunk, 2 * M_WIDTH), F32),
                        pltpu.VMEM((M_HEADS, dh, dh), F32),
                        pltpu.VMEM((M_HEADS, dh), F32),
                        pltpu.VMEM((SUBLANES, LANES), F32)],
        compiler_params=_cparams("parallel", "arbitrary"),
        name="mlstm",
    )(qk, v, og, gc, conv_w, conv_b, m_norm, conv0, c0, n0, m0)


def _pack_words(x):
    bits = pltpu.bitcast(x.astype(BF16).astype(F32), jnp.uint32)
    half = x.shape[1] // 2
    lo = lax.shift_right_logical(bits[:, :half], jnp.uint32(16))
    hi = bits[:, half:] & jnp.uint32(0xFFFF0000)
    return pltpu.bitcast(lo | hi, jnp.int32)


def _out_kernel(att_ref, mo_ref, x_ref, g1_ref, sc_ref, sh_ref, gn_ref, wa_ref, wm_ref, wq_ref,
                xo_ref, h2_ref, qp_ref):
    mix = (jnp.dot(att_ref[...].astype(BF16), wa_ref[...], preferred_element_type=F32)
           + jnp.dot(mo_ref[...].astype(BF16), wm_ref[...], preferred_element_type=F32))
    x = x_ref[...] + g1_ref[...] * mix
    xo_ref[...] = x
    y = x * lax.rsqrt(jnp.mean(x * x, axis=-1, keepdims=True) + EPS) * gn_ref[...]
    h2 = y * (1.0 + sc_ref[...]) + sh_ref[...]
    qp_ref[...] = jnp.dot(h2.astype(BF16), wq_ref[...], preferred_element_type=F32)
    h2_ref[...] = _pack_words(h2)


def _out_call(att, mo, x, g1, sc, sh, gnorm, wa, wm, wq, per_token, tokens_per_req):
    n = x.shape[0]
    tm = min(256, n if per_token else tokens_per_req)
    mod = _mod_spec(per_token, tm, tokens_per_req)
    full = lambda shape: pl.BlockSpec(shape, lambda i: (0,) * len(shape))
    row = lambda w: pl.BlockSpec((tm, w), lambda i: (i, 0))
    nq = wq.shape[1]
    return pl.pallas_call(
        _out_kernel,
        grid=(n // tm,),
        in_specs=[row(A_WIDTH), row(M_WIDTH), row(D_MODEL), mod, mod, mod, full((1, D_MODEL)),
                  full(wa.shape), full(wm.shape), full(wq.shape)],
        out_specs=[row(D_MODEL), row(D_MODEL // 2), row(nq)],
        out_shape=[jax.ShapeDtypeStruct((n, D_MODEL), F32), jax.ShapeDtypeStruct((n, D_MODEL // 2), jnp.int32),
                   jax.ShapeDtypeStruct((n, nq), F32)],
        compiler_params=_cparams("parallel"),
        name="out_proj",
    )(att, mo, x, g1, sc, sh, gnorm, wa, wm, wq)


def _pk_cells():
    return [(a, b) for a in range(P_TOPK) for b in range(P_TOPK) if (a + 1) * (b + 1) <= P_TOPK]


PK_CELL_ROWS = 64


def _pk_expand_mats():
    cells = _pk_cells()
    e0 = np.zeros((PK_CELL_ROWS, LANES), np.float32)
    e1 = np.zeros((PK_CELL_ROWS, LANES), np.float32)
    for j, (a, b) in enumerate(cells):
        e0[j, a] = 1.0
        e1[j, b] = 1.0
    return e0, e1, len(cells)


def _top_rows(s, rowf, rounds):
    n_rows = s.shape[0]
    vals, idxs = [], []
    for _ in range(rounds):
        m = jnp.max(s, axis=0, keepdims=True)
        i = jnp.min(jnp.where(s == m, rowf, float(n_rows)), axis=0, keepdims=True)
        vals.append(m)
        idxs.append(i)
        s = jnp.where(rowf == i, -jnp.inf, s)
    return jnp.concatenate(vals, axis=0), jnp.concatenate(idxs, axis=0)


def _select_kernel(n_cells, qp_ref, keys_ref, e0_ref, e1_ref, idx_ref, gw_ref, idx_s, gw_s):
    tm = qp_ref.shape[0]
    keyf = lax.broadcasted_iota(jnp.int32, (N_KEYS, tm), 0).astype(F32)
    cellf = lax.broadcasted_iota(jnp.int32, (PK_CELL_ROWS, tm), 0).astype(F32)
    e0 = e0_ref[...]
    e1 = e1_ref[...]
    pad = jnp.zeros((LANES - P_TOPK, tm), F32)

    def head(h, carry):
        h = jnp.asarray(h, jnp.int32)
        sub = []
        for c in range(2):
            col = pl.multiple_of((h * 2 + c) * P_HALF, P_HALF)
            s = _bdot_nt(keys_ref[h, c], qp_ref[:, pl.ds(col, P_HALF)])
            sub.append(_top_rows(s, keyf, P_TOPK))
        (v0, i0), (v1, i1) = sub
        expand = lambda e, x: jnp.dot(e, jnp.concatenate([x, pad], axis=0), precision=HIGHEST,
                                      preferred_element_type=F32)
        cand = expand(e0, v0) + expand(e1, v1)
        cidx = expand(e0, i0 * float(N_KEYS)) + expand(e1, i1)
        cand = jnp.where(cellf < n_cells, cand, -jnp.inf)
        best, eidx = [], []
        for _ in range(P_TOPK):
            m = jnp.max(cand, axis=0, keepdims=True)
            j = jnp.min(jnp.where(cand == m, cellf, float(PK_CELL_ROWS)), axis=0, keepdims=True)
            hit = cellf == j
            eidx.append(jnp.max(jnp.where(hit, cidx, -1.0), axis=0, keepdims=True))
            best.append(m)
            cand = jnp.where(hit, -jnp.inf, cand)
        best = jnp.concatenate(best, axis=0)
        e = jnp.exp(best - best[0:1, :])
        row0 = pl.multiple_of(h * P_TOPK, P_TOPK)
        gw_s[pl.ds(row0, P_TOPK), :] = e / jnp.sum(e, axis=0, keepdims=True)
        idx_s[pl.ds(row0, P_TOPK), :] = jnp.concatenate(eidx, axis=0)
        return carry

    lax.fori_loop(0, P_HEADS, head, 0)
    idx_ref[...] = idx_s[...].T.astype(jnp.int32)
    gw_ref[...] = gw_s[...].T


def _select_call(qp, keys_bf16):
    n = qp.shape[0]
    tm = min(LANES, n)
    e0, e1, n_cells = _pk_expand_mats()
    full = lambda shape: pl.BlockSpec(shape, lambda i: (0,) * len(shape))
    return pl.pallas_call(
        functools.partial(_select_kernel, n_cells),
        grid=(n // tm,),
        in_specs=[pl.BlockSpec((tm, qp.shape[1]), lambda i: (i, 0)), full(keys_bf16.shape),
                  full((PK_CELL_ROWS, LANES)), full((PK_CELL_ROWS, LANES))],
        out_specs=[pl.BlockSpec((tm, LANES), lambda i: (i, 0)), pl.BlockSpec((tm, LANES), lambda i: (i, 0))],
        out_shape=[jax.ShapeDtypeStruct((n, LANES), jnp.int32), jax.ShapeDtypeStruct((n, LANES), F32)],
        scratch_shapes=[pltpu.VMEM((P_HEADS * P_TOPK, tm), F32), pltpu.VMEM((P_HEADS * P_TOPK, tm), F32)],
        compiler_params=_cparams("parallel"),
        name="peer_select",
    )(qp, keys_bf16, jnp.asarray(e0), jnp.asarray(e1))


N_SEL = P_HEADS * P_TOPK

SC_CORES = 2
SC_SUBCORES = 16
SC_LANES = 16
SC_TOK_BLOCK = 8
SC_ROWS = 32
SC_ROW_BLOCK = 16
SC_ACC_CHAINS = 4
SC_ACC_ROWS = 16
SC_NBUF = 4
N_WCOL = D_MODEL // (2 * SC_LANES)


def _pack_kernel(t_ref, o_ref):
    o_ref[...] = _pack_words(t_ref[...])


def _pack_table(tables, layer):
    _, e, dcol = tables.shape
    tm = min(1024, e)
    return pl.pallas_call(
        _pack_kernel,
        grid=(e // tm,),
        in_specs=[pl.BlockSpec((None, tm, dcol), lambda i: (layer, i, 0))],
        out_specs=pl.BlockSpec((tm, dcol // 2), lambda i: (i, 0)),
        out_shape=jax.ShapeDtypeStruct((e, dcol // 2), jnp.int32),
        compiler_params=_cparams("parallel"),
        name="pack_table",
    )(tables)


def _sc_gelu(x):
    z = math.sqrt(2.0 / math.pi) * (x + 0.044715 * (x * x * x))
    t = 1.0 - 2.0 / (jnp.exp(2.0 * z) + 1.0)
    return x * (0.5 * (1.0 + t))


def _sc_expert_body(tokens_per_worker, idx_hbm, gw_hbm, h2_hbm, u_hbm, v_hbm, y_hbm,
                    idx_v, gw_v, x_v, o_v, buf, coef_v, tr_v, sem, in_sem, out_sem):
    wid = lax.axis_index("s") * SC_CORES + lax.axis_index("c")
    base = wid * tokens_per_worker
    lane = lax.iota(jnp.int32, SC_LANES)
    n_gather = N_SEL // SC_ROWS
    zero = jnp.zeros((SC_LANES,), F32)

    n_steps = 2 * n_gather
    assert n_steps % SC_NBUF == 0

    def gather(p, tt, i):
        table = u_hbm if i < n_gather else v_hbm
        j = i % n_gather
        slot = i % SC_NBUF
        return pltpu.make_async_copy(table.at[idx_v.at[p, tt, pl.ds(j * SC_ROWS, SC_ROWS)]], buf.at[slot],
                                     sem.at[slot])

    def unpack(w):
        lo = lax.bitcast_convert_type(lax.shift_left(w, jnp.full((SC_LANES,), 16, jnp.int32)), F32)
        hi = lax.bitcast_convert_type(w & jnp.full((SC_LANES,), -65536, jnp.int32), F32)
        return lo, hi

    def packed(w):
        return plsc.bitcast(w, BF16)

    def unpack_sum(s):
        return unpack(plsc.bitcast(s, jnp.int32))

    def act_chunk(p, tt, j, slot):
        @pl.loop(0, SC_ROWS // SC_LANES)
        def _(half):
            for rb in range(SC_LANES // SC_ROW_BLOCK):
                r0 = half * SC_LANES + rb * SC_ROW_BLOCK

                def col(c, accs):
                    w0 = pl.multiple_of(c * (2 * SC_LANES), 2 * SC_LANES)
                    xa = packed(x_v[p, tt, pl.ds(w0, SC_LANES)])
                    xb = packed(x_v[p, tt, pl.ds(w0 + SC_LANES, SC_LANES)])
                    out = []
                    for r, a in enumerate(accs):
                        ua = packed(buf[slot, r0 + r, pl.ds(w0, SC_LANES)])
                        ub = packed(buf[slot, r0 + r, pl.ds(w0 + SC_LANES, SC_LANES)])
                        lo, hi = unpack_sum(ua * xa + ub * xb)
                        out.append(a + lo + hi)
                    return tuple(out)

                accs = lax.fori_loop(0, N_WCOL // 2, col, (zero,) * SC_ROW_BLOCK)
                for r in range(SC_ROW_BLOCK):
                    tr_v[pl.ds((rb * SC_ROW_BLOCK + r) * SC_LANES, SC_LANES)] = accs[r]
            tot = zero
            for jj in range(SC_LANES):
                tot = tot + plsc.load_gather(tr_v, [lane * SC_LANES + jj])
            k0 = pl.multiple_of(j * SC_ROWS + half * SC_LANES, SC_LANES)
            coef_v[pl.ds(k0, SC_LANES)] = gw_v[p, tt, pl.ds(k0, SC_LANES)] * _sc_gelu(tot)

    def acc_chunk(p, tt, j, slot, first):
        def tree_sum(parts):
            while len(parts) > 1:
                parts = [parts[i] + parts[i + 1] for i in range(0, len(parts), 2)]
            return parts[0]

        for rb in range(SC_ROWS // SC_ACC_ROWS):
            rows = list(range(rb * SC_ACC_ROWS, (rb + 1) * SC_ACC_ROWS))
            splat = {}
            for r in rows:
                c16 = plsc.load_gather(coef_v, [jnp.full((SC_LANES,), j * SC_ROWS + r, jnp.int32)])
                splat[r] = plsc.pack(c16, c16, format=plsc.PackFormat.INTERLEAVED)
            fresh = first and rb == 0

            @plsc.parallel_loop(0, N_WCOL)
            def _(c):
                woff = pl.multiple_of(c * SC_LANES, SC_LANES)
                hoff = pl.multiple_of(c * SC_LANES + D_MODEL // 2, SC_LANES)
                pa, pb = [], []
                for n in range(0, SC_ACC_ROWS, 2):
                    r0, r1 = rows[n], rows[n + 1]
                    s = (splat[r0] * packed(buf[slot, r0, pl.ds(woff, SC_LANES)])
                         + splat[r1] * packed(buf[slot, r1, pl.ds(woff, SC_LANES)]))
                    lo, hi = unpack_sum(s)
                    if n // 2 < SC_ACC_CHAINS:
                        pa.append(lo)
                        pb.append(hi)
                    else:
                        pa[(n // 2) % SC_ACC_CHAINS] = pa[(n // 2) % SC_ACC_CHAINS] + lo
                        pb[(n // 2) % SC_ACC_CHAINS] = pb[(n // 2) % SC_ACC_CHAINS] + hi
                sa, sb = tree_sum(pa), tree_sum(pb)
                if not fresh:
                    sa = sa + o_v[p, tt, pl.ds(woff, SC_LANES)]
                    sb = sb + o_v[p, tt, pl.ds(hoff, SC_LANES)]
                o_v[p, tt, pl.ds(woff, SC_LANES)] = sa
                o_v[p, tt, pl.ds(hoff, SC_LANES)] = sb

    n_blocks = tokens_per_worker // SC_TOK_BLOCK

    def block_start(blk):
        return pl.multiple_of(base + blk * SC_TOK_BLOCK, SC_TOK_BLOCK)

    def in_copies(blk, p):
        rows = pl.ds(block_start(blk), SC_TOK_BLOCK)
        return [pltpu.make_async_copy(idx_hbm.at[rows], idx_v.at[p], in_sem.at[p]),
                pltpu.make_async_copy(gw_hbm.at[rows], gw_v.at[p], in_sem.at[p]),
                pltpu.make_async_copy(h2_hbm.at[rows], x_v.at[p], in_sem.at[p])]

    def out_copy(blk, p):
        return pltpu.make_async_copy(o_v.at[p], y_hbm.at[pl.ds(block_start(blk), SC_TOK_BLOCK)], out_sem.at[p])

    for cp in in_copies(0, 0):
        cp.start()

    @pl.loop(0, n_blocks)
    def _(blk):
        p = lax.rem(blk, 2)
        for cp in in_copies(blk, p):
            cp.wait()

        @pl.when(blk + 1 < n_blocks)
        def _():
            for cp in in_copies(blk + 1, 1 - p):
                cp.start()

        @pl.when(blk >= 2)
        def _():
            out_copy(blk - 2, p).wait()

        ahead = SC_NBUF - 1
        for i in range(ahead):
            gather(p, 0, i).start()

        @pl.loop(0, SC_TOK_BLOCK)
        def _(tt):
            for i in range(n_steps):
                if i + ahead < n_steps:
                    gather(p, tt, i + ahead).start()
                else:
                    @pl.when(tt + 1 < SC_TOK_BLOCK)
                    def _():
                        gather(p, tt + 1, i + ahead - n_steps).start()
                gather(p, tt, i).wait()
                if i < n_gather:
                    act_chunk(p, tt, i, i % SC_NBUF)
                else:
                    acc_chunk(p, tt, i - n_gather, i % SC_NBUF, i == n_gather)

        out_copy(blk, p).start()

    for blk in (n_blocks - 2, n_blocks - 1):
        out_copy(blk, blk % 2).wait()


def _sc_expert_call(idx, gw, xw, u, v):
    n = idx.shape[0]
    workers = SC_CORES * SC_SUBCORES
    assert n % (workers * SC_TOK_BLOCK) == 0 and n // (workers * SC_TOK_BLOCK) >= 2
    mesh = plsc.VectorSubcoreMesh(core_axis_name="c", subcore_axis_name="s")
    return pl.kernel(
        functools.partial(_sc_expert_body, n // workers),
        out_type=jax.ShapeDtypeStruct((n, D_MODEL), F32),
        mesh=mesh,
        scratch_types=[pltpu.VMEM((2, SC_TOK_BLOCK, N_SEL), jnp.int32),
                       pltpu.VMEM((2, SC_TOK_BLOCK, N_SEL), F32),
                       pltpu.VMEM((2, SC_TOK_BLOCK, D_MODEL // 2), jnp.int32),
                       pltpu.VMEM((2, SC_TOK_BLOCK, D_MODEL), F32),
                       pltpu.VMEM((SC_NBUF, SC_ROWS, D_MODEL // 2), jnp.int32),
                       pltpu.VMEM((N_SEL,), F32),
                       pltpu.VMEM((SC_LANES * SC_LANES,), F32),
                       pltpu.SemaphoreType.DMA((SC_NBUF,)), pltpu.SemaphoreType.DMA((2,)),
                       pltpu.SemaphoreType.DMA((2,))],
        compiler_params=pltpu.CompilerParams(needs_layout_passes=False),
        name="peer_experts_sc",
    )(idx, gw, xw, u, v)


def _resid_kernel(x_ref, y_ref, g_ref, o_ref):
    o_ref[...] = x_ref[...] + g_ref[...] * y_ref[...]


def _resid_call(x, y, g2, per_token, tokens_per_req):
    n = x.shape[0]
    tm = min(512, n if per_token else tokens_per_req)
    row = pl.BlockSpec((tm, D_MODEL), lambda i: (i, 0))
    return pl.pallas_call(
        _resid_kernel,
        grid=(n // tm,),
        in_specs=[row, row, _mod_spec(per_token, tm, tokens_per_req)],
        out_specs=row,
        out_shape=jax.ShapeDtypeStruct((n, D_MODEL), F32),
        compiler_params=_cparams("parallel"),
        name="peer_residual",
    )(x, y, g2)


def _final_kernel(x_ref, g_ref, o_ref):
    x = x_ref[...]
    o_ref[...] = x * lax.rsqrt(jnp.mean(x * x, axis=-1, keepdims=True) + EPS) * g_ref[...]


def _final_call(x, g):
    n = x.shape[0]
    tm = min(512, n)
    return pl.pallas_call(
        _final_kernel,
        grid=(n // tm,),
        in_specs=[pl.BlockSpec((tm, D_MODEL), lambda i: (i, 0)), pl.BlockSpec((1, D_MODEL), lambda i: (0, 0))],
        out_specs=pl.BlockSpec((tm, D_MODEL), lambda i: (i, 0)),
        out_shape=jax.ShapeDtypeStruct((n, D_MODEL), F32),
        compiler_params=_cparams("parallel"),
        name="final_norm",
    )(x, g)


def _split_w_in(w_in_l, gate_b_l):
    cuts = np.cumsum([A_WIDTH + 2 * A_KV_WIDTH, 2 * M_WIDTH, M_WIDTH, M_WIDTH]).tolist()
    wa = w_in_l[:, :cuts[0]].astype(BF16)
    wqk = w_in_l[:, cuts[0]:cuts[1]].astype(BF16)
    wv = w_in_l[:, cuts[1]:cuts[2]].astype(BF16)
    wo = w_in_l[:, cuts[2]:cuts[3]].astype(BF16)
    ng = 2 * M_HEADS
    wg = jnp.pad(w_in_l[:, cuts[3]:], ((0, 0), (0, LANES - ng))).astype(BF16)
    gb = jnp.pad(gate_b_l.astype(F32), (0, LANES - ng)).reshape(1, LANES)
    return wa, wqk, wv, wo, wg, gb


def _layer(x, mods, per_token, batch, seq, t_valid, lw, bias_p, bias_c, bias_n, kv_cache, conv0, state,
           after=None):
    (norm_mix, norm_ffn, w_in, conv_w, conv_b, gate_b, sinks, m_norm, w_out, peer_query, peer_keys,
     peer_u, peer_v) = lw
    if after is not None:
        x, _ = lax.optimization_barrier((x, after))
    sh1, sc1, g1, sh2, sc2, g2 = mods
    wa, wqk, wv, wo, wg, gb = _split_w_in(w_in, gate_b)
    qkv, qkm, vm, om, gc = _in_call(x, sc1, sh1, norm_mix.reshape(1, -1), wa, wqk, wv, wo, wg, gb,
                                    per_token, seq)
    if kv_cache is None:
        att = _attn_p_call(qkv, bias_p, sinks, batch, seq)
        kv3 = qkv.reshape(batch, seq, -1)
        new_k = kv3[:, seq - WINDOW:, A_WIDTH:A_WIDTH + A_KV_WIDTH]
        new_v = kv3[:, seq - WINDOW:, A_WIDTH + A_KV_WIDTH:]
        chunk = M_CHUNK
    else:
        att, new_k, new_v = _attn_s_call(qkv, kv_cache[0], kv_cache[1], bias_c, bias_n, sinks, t_valid)
        chunk = seq
    c0, n0, m0 = state
    mo, c_new, n_new, m_new = _mlstm_call(qkm, vm, om, gc, conv_w, conv_b.reshape(1, -1),
                                          m_norm.reshape(1, -1), conv0, c0, n0, m0,
                                          batch, seq, chunk, min(t_valid, chunk))
    new_conv = qkm.reshape(batch, seq, -1)[:, t_valid - (CONV_W - 1):t_valid]
    pad_rows = seq - t_valid
    if pad_rows:
        keep = lambda a: a.reshape(batch, seq, -1)[:, :t_valid].reshape(batch * t_valid, -1)
        att, mo, x, g1, sc2, sh2, g2 = (keep(a) for a in (att, mo, x, g1, sc2, sh2, g2))
    x_mid, xw, qp = _out_call(att, mo, x, g1, sc2, sh2, norm_ffn.reshape(1, -1),
                              w_out[:A_WIDTH].astype(BF16), w_out[A_WIDTH:].astype(BF16),
                              peer_query.astype(BF16), per_token, t_valid)
    idx, gw = _select_call(qp, peer_keys.astype(BF16))
    y = _sc_expert_call(idx, gw, xw, peer_u, peer_v)
    x_new = _resid_call(x_mid, y, g2, per_token, t_valid)
    if pad_rows:
        x_new = jnp.pad(x_new.reshape(batch, t_valid, -1), ((0, 0), (0, pad_rows), (0, 0))).reshape(batch * seq, -1)
    new_k = new_k.reshape(batch, WINDOW, A_KV_HEADS, A_HEAD_DIM)
    new_v = new_v.reshape(batch, WINDOW, A_KV_HEADS, A_HEAD_DIM)
    return x_new, (new_k, new_v, new_conv, c_new, n_new, m_new[:, :M_HEADS, 0]), idx


def _prompt_group_sizes(n_req):
    if n_req < PROMPT_GROUPS:
        return [1] * n_req
    mid, n_mid = n_req - 2, PROMPT_GROUPS - 2
    weights = [i + 2 for i in range(n_mid)]
    sizes = [max(1, mid * w // sum(weights)) for w in weights]
    for i in range(mid - sum(sizes)):
        sizes[n_mid - 1 - i % n_mid] += 1
    return [1] + sizes + [1]


def kernel(x_prompt, x_sample, c_prompt, c_sample, cache_k, cache_v, state_conv, state_C, state_n, state_m, rel_bias, w_ada, b_ada, norm_mix, norm_ffn, w_in, conv_w, conv_b, gate_b, attn_sinks, m_norm, w_out, peer_query, peer_keys, peer_u, peer_v, norm_final):
    depth = w_ada.shape[0]
    bp, tp, d = x_prompt.shape
    bs, ts, _ = x_sample.shape
    assert tp % WINDOW == 0 and tp % M_CHUNK == 0 and ts <= SAMPLE_PAD and ts >= CONV_W - 1

    mod_all = _ada_call(jnp.concatenate([c_prompt, c_sample], axis=0), w_ada, b_ada)

    qi = np.arange(WINDOW)[:, None]
    bias_p = _bias_call(rel_bias, qi + WINDOW - np.arange(2 * WINDOW)[None, :])
    qs = np.arange(SAMPLE_PAD)[:, None]
    bias_c = _bias_call(rel_bias, qs + WINDOW - np.arange(WINDOW)[None, :])
    bias_n = _bias_call(rel_bias, qs - np.arange(SAMPLE_PAD)[None, :])

    sizes = _prompt_group_sizes(bp)
    starts = np.cumsum([0] + sizes).tolist()
    xg = [x_prompt[starts[g]:starts[g + 1]].reshape(sizes[g] * tp, d) for g in range(len(sizes))]
    xs = jnp.pad(x_sample, ((0, 0), (0, SAMPLE_PAD - ts), (0, 0))).reshape(bs * SAMPLE_PAD, d)
    halo_pad = ((0, 0), (SUBLANES - (CONV_W - 1), 0), (0, 0))

    st_p, st_s = [], []
    for l in range(depth):
        lw = (norm_mix[l], norm_ffn[l], w_in[l], conv_w[l], conv_b[l], gate_b[l], attn_sinks[l], m_norm[l],
              w_out[l], peer_query[l], peer_keys[l], _pack_table(peer_u, l), _pack_table(peer_v, l))
        mod_s = [jnp.repeat(m, SAMPLE_PAD, axis=0) for m in jnp.split(mod_all[l, bp:], 6, axis=-1)]
        sp_groups = []
        last_idx = None
        for g, bg in enumerate(sizes):
            mod_g = [m.reshape(bg, 1, d) for m in jnp.split(mod_all[l, starts[g]:starts[g + 1]], 6, axis=-1)]
            zero_state = (jnp.zeros((bg, M_HEADS, M_HEAD_DIM, M_HEAD_DIM), F32),
                          jnp.zeros((bg, M_HEADS, M_HEAD_DIM), F32),
                          jnp.zeros((bg, SUBLANES, LANES), F32))
            zero_conv = jnp.zeros((bg, SUBLANES, 2 * M_WIDTH), F32)
            xg[g], sp, last_idx = _layer(xg[g], mod_g, False, bg, tp, tp, lw, bias_p, None, None, None,
                                         zero_conv, zero_state, after=last_idx if l == 0 else None)
            sp_groups.append(sp)
        st_p.append([jnp.concatenate([sp[i] for sp in sp_groups], axis=0) for i in range(6)])
        state_s = (state_C[l].astype(F32), state_n[l].astype(F32),
                   jnp.broadcast_to(jnp.pad(state_m[l].astype(F32), ((0, 0), (0, SUBLANES - M_HEADS)))[:, :, None],
                                    (bs, SUBLANES, LANES)))
        kv_cache = (cache_k[l].reshape(bs, WINDOW, A_KV_WIDTH), cache_v[l].reshape(bs, WINDOW, A_KV_WIDTH))
        xs, ss, _ = _layer(xs, mod_s, True, bs, SAMPLE_PAD, ts, lw, None, bias_c, bias_n, kv_cache,
                           jnp.pad(state_conv[l].astype(F32), halo_pad), state_s, after=last_idx)
        st_s.append(ss)

    gfin = norm_final.reshape(1, d)
    y_prompt = jnp.concatenate([_final_call(x, gfin).reshape(bg, tp, d) for x, bg in zip(xg, sizes)], axis=0)
    y_sample = _final_call(xs, gfin).reshape(bs, SAMPLE_PAD, d)[:, :ts]
    outs_p = [jnp.stack([s[i] for s in st_p]) for i in range(6)]
    outs_s = [jnp.stack([s[i] for s in st_s]) for i in range(6)]
    return (y_prompt, y_sample, *outs_p, *outs_s)
```

```python
import functools
import math

import numpy as np
import jax
import jax.numpy as jnp
from jax import lax
from jax.experimental import pallas as pl
from jax.experimental.pallas import tpu as pltpu
from jax.experimental.pallas import tpu_sc as plsc

F32 = jnp.float32
BF16 = jnp.bfloat16
HIGHEST = lax.Precision.HIGHEST

D_MODEL = 1024
A_HEADS = 8
A_KV_HEADS = 2
A_GROUP = A_HEADS // A_KV_HEADS
A_HEAD_DIM = 64
A_WIDTH = A_HEADS * A_HEAD_DIM
A_KV_WIDTH = A_KV_HEADS * A_HEAD_DIM
WINDOW = 128
ATT_SCALE = A_HEAD_DIM ** -0.5
N_BUCKETS = 32
MAX_DISTANCE = WINDOW
M_HEADS = 4
M_HEAD_DIM = 128
M_WIDTH = M_HEADS * M_HEAD_DIM
CONV_W = 4
M_CHUNK = 64
N_KEYS = 128
P_HEADS = 8
P_TOPK = 16
P_KEY_DIM = 256
P_HALF = P_KEY_DIM // 2
EPS = 1e-6
NEG_INF = -1e30

LANES = 128
SUBLANES = 8
SAMPLE_PAD = SUBLANES
VMEM_LIMIT = 48 * 1024 * 1024
PROMPT_GROUPS = 9

NT_DIMS = (((1,), (1,)), ((), ()))
TN_DIMS = (((0,), (0,)), ((), ()))


def _cparams(*sem):
    return pltpu.CompilerParams(dimension_semantics=sem, vmem_limit_bytes=VMEM_LIMIT)


def _bdot(a, b):
    return jnp.dot(a.astype(BF16), b.astype(BF16), preferred_element_type=F32)


def _bdot_nt(a, b):
    return lax.dot_general(a.astype(BF16), b.astype(BF16), NT_DIMS, preferred_element_type=F32)


def _sigmoid(x):
    return 1.0 / (1.0 + jnp.exp(-x))


def _log_sigmoid(x):
    return jnp.minimum(x, 0.0) - jnp.log1p(jnp.exp(-jnp.abs(x)))


def _ada_kernel(c_ref, w_ref, b_ref, o_ref):
    c = c_ref[...]
    s = c * _sigmoid(c)
    o_ref[...] = jnp.dot(s, w_ref[...], precision=HIGHEST, preferred_element_type=F32) + b_ref[...]


def _ada_call(c_all, w_ada, b_ada):
    depth, d, n6 = w_ada.shape
    rows = c_all.shape[0]
    bn = 1024
    return pl.pallas_call(
        _ada_kernel,
        grid=(depth, n6 // bn),
        in_specs=[
            pl.BlockSpec((rows, d), lambda l, j: (0, 0)),
            pl.BlockSpec((None, d, bn), lambda l, j: (l, 0, j)),
            pl.BlockSpec((None, 1, bn), lambda l, j: (l, 0, j)),
        ],
        out_specs=pl.BlockSpec((None, rows, bn), lambda l, j: (l, 0, j)),
        out_shape=jax.ShapeDtypeStruct((depth, rows, n6), F32),
        compiler_params=_cparams("parallel", "parallel"),
        name="ada_mod",
    )(c_all, w_ada, b_ada.reshape(depth, 1, n6))


def _mod_spec(per_token, tm, tokens_per_req):
    if per_token:
        return pl.BlockSpec((tm, D_MODEL), lambda i: (i, 0))
    tiles = tokens_per_req // tm
    return pl.BlockSpec((None, 1, D_MODEL), lambda i: (i // tiles, 0, 0))


def _in_kernel(x_ref, sc_ref, sh_ref, g_ref, wa_ref, wqk_ref, wv_ref, wo_ref, wg_ref, gb_ref,
               qkv_ref, qkm_ref, v_ref, o_ref, gc_ref):
    x = x_ref[...]
    y = x * lax.rsqrt(jnp.mean(x * x, axis=-1, keepdims=True) + EPS) * g_ref[...]
    h = (y * (1.0 + sc_ref[...]) + sh_ref[...]).astype(BF16)
    qkv_ref[...] = jnp.dot(h, wa_ref[...], preferred_element_type=F32)
    qkm_ref[...] = jnp.dot(h, wqk_ref[...], preferred_element_type=F32)
    v_ref[...] = jnp.dot(h, wv_ref[...], preferred_element_type=F32)
    o_ref[...] = jnp.dot(h, wo_ref[...], preferred_element_type=F32)
    g = jnp.dot(h, wg_ref[...], preferred_element_type=F32) + gb_ref[...]
    lane = lax.broadcasted_iota(jnp.int32, g.shape, 1)
    gc_ref[...] = jnp.where(lane < M_HEADS, g, jnp.where(lane < 2 * M_HEADS, _log_sigmoid(g), 0.0))


def _in_call(x, sc, sh, gnorm, wa, wqk, wv, wo, wg, gb, per_token, tokens_per_req):
    n = x.shape[0]
    tm = min(512, n if per_token else tokens_per_req)
    mod = _mod_spec(per_token, tm, tokens_per_req)
    full = lambda shape: pl.BlockSpec(shape, lambda i: (0,) * len(shape))
    row = lambda w: pl.BlockSpec((tm, w), lambda i: (i, 0))
    return pl.pallas_call(
        _in_kernel,
        grid=(n // tm,),
        in_specs=[row(D_MODEL), mod, mod, full((1, D_MODEL)), full(wa.shape), full(wqk.shape),
                  full(wv.shape), full(wo.shape), full(wg.shape), full((1, LANES))],
        out_specs=[row(wa.shape[1]), row(wqk.shape[1]), row(wv.shape[1]), row(wo.shape[1]), row(LANES)],
        out_shape=[jax.ShapeDtypeStruct((n, w), F32)
                   for w in (wa.shape[1], wqk.shape[1], wv.shape[1], wo.shape[1], LANES)],
        compiler_params=_cparams("parallel"),
        name="in_proj",
    )(x, sc, sh, gnorm, wa, wqk, wv, wo, wg, gb)


def _t5_bucket_np(dist):
    n = np.maximum(dist, 0)
    max_exact = N_BUCKETS // 2
    nf = np.maximum(n, 1).astype(np.float64)
    large = max_exact + (np.log(nf / max_exact) / math.log(MAX_DISTANCE / max_exact)
                         * (N_BUCKETS - max_exact)).astype(np.int32)
    return np.where(n < max_exact, n, np.minimum(large, N_BUCKETS - 1)).astype(np.int32)


def _bias_kernel(bucket_ref, rel_ref, o_ref):
    bucket = bucket_ref[...]
    for h in range(A_HEADS):
        acc = jnp.zeros(bucket.shape, F32)
        for b in range(N_BUCKETS):
            acc = jnp.where(bucket == b, rel_ref[b, h], acc)
        o_ref[h] = acc


def _bias_call(rel_bias, dist):
    bucket = jnp.asarray(_t5_bucket_np(dist))
    nq, nk = dist.shape
    return pl.pallas_call(
        _bias_kernel,
        in_specs=[pl.BlockSpec((nq, nk), lambda: (0, 0)),
                  pl.BlockSpec(memory_space=pltpu.SMEM)],
        out_specs=pl.BlockSpec((A_HEADS, nq, nk), lambda: (0, 0, 0)),
        out_shape=jax.ShapeDtypeStruct((A_HEADS, nq, nk), F32),
        name="t5_bias",
    )(bucket, rel_bias)


def _attn_p_kernel(q_ref, kp_ref, kc_ref, vp_ref, vc_ref, bias_ref, sink_ref, o_ref):
    i = pl.program_id(1)
    qi = lax.broadcasted_iota(jnp.int32, (WINDOW, WINDOW), 0)
    kj = lax.broadcasted_iota(jnp.int32, (WINDOW, WINDOW), 1)
    valid_prev = jnp.logical_and(kj > qi, i > 0)
    valid_cur = kj <= qi
    q = q_ref[...]
    for h in range(A_HEADS):
        kv = h // A_GROUP
        qh = q[:, h * A_HEAD_DIM:(h + 1) * A_HEAD_DIM]
        sl = slice(kv * A_HEAD_DIM, (kv + 1) * A_HEAD_DIM)
        bias = bias_ref[h]
        sp = _bdot_nt(qh, kp_ref[:, sl]) * ATT_SCALE + bias[:, :WINDOW]
        sc = _bdot_nt(qh, kc_ref[:, sl]) * ATT_SCALE + bias[:, WINDOW:]
        sp = jnp.where(valid_prev, sp, NEG_INF)
        sc = jnp.where(valid_cur, sc, NEG_INF)
        sink = sink_ref[0, h]
        mx = jnp.maximum(jnp.maximum(jnp.max(sp, axis=-1, keepdims=True),
                                     jnp.max(sc, axis=-1, keepdims=True)), sink)
        ep = jnp.exp(sp - mx)
        ec = jnp.exp(sc - mx)
        den = (jnp.sum(ep, axis=-1, keepdims=True) + jnp.sum(ec, axis=-1, keepdims=True)
               + jnp.exp(sink - mx))
        o = _bdot(ep / den, vp_ref[:, sl]) + _bdot(ec / den, vc_ref[:, sl])
        o_ref[:, h * A_HEAD_DIM:(h + 1) * A_HEAD_DIM] = o


def _attn_p_call(qkv, bias, sinks, batch, seq):
    nb = seq // WINDOW
    n = batch * seq
    kcol = A_WIDTH // A_KV_WIDTH
    vcol = kcol + 1
    cur = lambda col: pl.BlockSpec((WINDOW, A_KV_WIDTH), lambda b, i: (b * nb + i, col))
    prev = lambda col: pl.BlockSpec((WINDOW, A_KV_WIDTH),
                                    lambda b, i: (b * nb + jnp.maximum(i - 1, 0), col))
    return pl.pallas_call(
        _attn_p_kernel,
        grid=(batch, nb),
        in_specs=[pl.BlockSpec((WINDOW, A_WIDTH), lambda b, i: (b * nb + i, 0)),
                  prev(kcol), cur(kcol), prev(vcol), cur(vcol),
                  pl.BlockSpec((A_HEADS, WINDOW, 2 * WINDOW), lambda b, i: (0, 0, 0)),
                  pl.BlockSpec(memory_space=pltpu.SMEM)],
        out_specs=pl.BlockSpec((WINDOW, A_WIDTH), lambda b, i: (b * nb + i, 0)),
        out_shape=jax.ShapeDtypeStruct((n, A_WIDTH), F32),
        compiler_params=_cparams("parallel", "parallel"),
        name="swa_prompt",
    )(qkv, qkv, qkv, qkv, qkv, bias, sinks.reshape(1, A_HEADS))


def _attn_s_kernel(n_new, qkv_ref, ck_ref, cv_ref, bc_ref, bn_ref, sink_ref,
                   o_ref, nk_ref, nv_ref, kk_s, vv_s):
    qkv = qkv_ref[...]
    knew = qkv[:, A_WIDTH:A_WIDTH + A_KV_WIDTH]
    vnew = qkv[:, A_WIDTH + A_KV_WIDTH:A_WIDTH + 2 * A_KV_WIDTH]
    ck = ck_ref[...]
    cv = cv_ref[...]
    rows = A_GROUP * SAMPLE_PAD
    qi = lax.broadcasted_iota(jnp.int32, (rows, WINDOW), 0) % SAMPLE_PAD
    kj = lax.broadcasted_iota(jnp.int32, (rows, WINDOW), 1)
    valid_c = kj > qi
    rcol = lax.broadcasted_iota(jnp.int32, (rows, 1), 0)
    qcol = rcol % SAMPLE_PAD
    for kv in range(A_KV_HEADS):
        heads = range(kv * A_GROUP, (kv + 1) * A_GROUP)
        sl = slice(kv * A_HEAD_DIM, (kv + 1) * A_HEAD_DIM)
        qs = jnp.concatenate([qkv[:, h * A_HEAD_DIM:(h + 1) * A_HEAD_DIM] for h in heads], axis=0)
        bias_c = jnp.concatenate([bc_ref[h] for h in heads], axis=0)
        bias_n = jnp.concatenate([bn_ref[h] for h in heads], axis=0)
        sink = jnp.zeros((rows, 1), F32)
        for g, h in enumerate(heads):
            sink = jnp.where(rcol // SAMPLE_PAD == g, sink_ref[0, h], sink)
        s_c = _bdot_nt(qs, ck[:, sl]) * ATT_SCALE + bias_c
        s_c = jnp.where(valid_c, s_c, NEG_INF)
        s_n = []
        for j in range(n_new):
            sj = jnp.sum(qs * knew[j:j + 1, sl], axis=-1, keepdims=True) * ATT_SCALE + bias_n[:, j:j + 1]
            s_n.append(jnp.where(qcol >= j, sj, NEG_INF))
        mx = jnp.maximum(jnp.max(s_c, axis=-1, keepdims=True), sink)
        for sj in s_n:
            mx = jnp.maximum(mx, sj)
        e_c = jnp.exp(s_c - mx)
        den = jnp.sum(e_c, axis=-1, keepdims=True) + jnp.exp(sink - mx)
        o = _bdot(e_c, cv[:, sl])
        for j, sj in enumerate(s_n):
            ej = jnp.exp(sj - mx)
            den = den + ej
            o = o + ej * vnew[j:j + 1, sl]
        o = o / den
        for g, h in enumerate(heads):
            o_ref[:, h * A_HEAD_DIM:(h + 1) * A_HEAD_DIM] = o[g * SAMPLE_PAD:(g + 1) * SAMPLE_PAD, :]
    kk_s[0:WINDOW, :] = ck
    kk_s[WINDOW:WINDOW + SAMPLE_PAD, :] = knew
    vv_s[0:WINDOW, :] = cv
    vv_s[WINDOW:WINDOW + SAMPLE_PAD, :] = vnew
    nk_ref[...] = kk_s[n_new:n_new + WINDOW, :]
    nv_ref[...] = vv_s[n_new:n_new + WINDOW, :]


def _attn_s_call(qkv, ck, cv, bias_c, bias_n, sinks, n_new):
    nreq = ck.shape[0]
    wq = qkv.shape[1]
    full3 = lambda shape: pl.BlockSpec(shape, lambda b: (0, 0, 0))
    cache = pl.BlockSpec((None, WINDOW, A_KV_WIDTH), lambda b: (b, 0, 0))
    return pl.pallas_call(
        functools.partial(_attn_s_kernel, n_new),
        grid=(nreq,),
        in_specs=[pl.BlockSpec((SAMPLE_PAD, wq), lambda b: (b, 0)), cache, cache,
                  full3(bias_c.shape), full3(bias_n.shape),
                  pl.BlockSpec(memory_space=pltpu.SMEM)],
        out_specs=[pl.BlockSpec((SAMPLE_PAD, A_WIDTH), lambda b: (b, 0)), cache, cache],
        out_shape=[jax.ShapeDtypeStruct((nreq * SAMPLE_PAD, A_WIDTH), F32),
                   jax.ShapeDtypeStruct(ck.shape, F32), jax.ShapeDtypeStruct(cv.shape, F32)],
        scratch_shapes=[pltpu.VMEM((WINDOW + SAMPLE_PAD, A_KV_WIDTH), F32),
                        pltpu.VMEM((WINDOW + SAMPLE_PAD, A_KV_WIDTH), F32)],
        compiler_params=_cparams("parallel"),
        name="swa_sample",
    )(qkv, ck, cv, bias_c, bias_n, sinks.reshape(1, A_HEADS))


def _mlstm_kernel(chunk, t_valid, qk_ref, v_ref, og_ref, gc_ref, cw_ref, cb_ref, mn_ref,
                  conv0_ref, c0_ref, n0_ref, m0_ref,
                  out_ref, cout_ref, nout_ref, mout_ref,
                  xp_s, c_s, n_s, m_s):
    step = pl.program_id(1)
    halo = SUBLANES

    @pl.when(step == 0)
    def _():
        xp_s[0:halo, :] = conv0_ref[...]
        c_s[...] = c0_ref[...]
        n_s[...] = n0_ref[...]
        m_s[...] = m0_ref[...]

    xp_s[halo:halo + chunk, :] = qk_ref[...]
    cw = cw_ref[...]
    y = cb_ref[...]
    for i in range(CONV_W):
        off = halo - (CONV_W - 1) + i
        y = y + xp_s[off:off + chunk, :] * cw[i:i + 1, :]
    xp_s[0:halo, :] = xp_s[chunk:chunk + halo, :]
    y = y * _sigmoid(y)
    q_all = y[:, :M_WIDTH]
    k_all = y[:, M_WIDTH:] * (M_HEAD_DIM ** -0.5)

    g = gc_ref[...]
    if t_valid < chunk:
        row = lax.broadcasted_iota(jnp.int32, g.shape, 0)
        lane = lax.broadcasted_iota(jnp.int32, g.shape, 1)
        g = jnp.where(row < t_valid, g, jnp.where(lane < M_HEADS, NEG_INF, 0.0))
    tr = lax.broadcasted_iota(jnp.int32, (chunk, chunk), 0)
    tc = lax.broadcasted_iota(jnp.int32, (chunk, chunk), 1)
    causal = tr >= tc
    tri = causal.astype(F32)
    bcol = jnp.dot(tri, g, precision=HIGHEST, preferred_element_type=F32)
    er = lax.broadcasted_iota(jnp.int32, (SUBLANES, LANES), 0)
    ec = lax.broadcasted_iota(jnp.int32, (SUBLANES, LANES), 1)
    eye = (er == ec).astype(F32)
    g_rows = lax.dot_general(eye, g, NT_DIMS, precision=HIGHEST, preferred_element_type=F32)
    b_rows = lax.dot_general(eye, bcol, NT_DIMS, precision=HIGHEST, preferred_element_type=F32)

    for h in range(M_HEADS):
        hs = slice(h * M_HEAD_DIM, (h + 1) * M_HEAD_DIM)
        b_c = bcol[:, M_HEADS + h:M_HEADS + h + 1]
        ig_c = g[:, h:h + 1]
        b_r = b_rows[M_HEADS + h:M_HEADS + h + 1, :]
        ig_r = g_rows[h:h + 1, :]
        m_prev = m_s[h:h + 1, 0:1]
        logw = jnp.where(causal, b_c - b_r + ig_r, -jnp.inf)
        inter = b_c + m_prev
        m_t = jnp.maximum(inter, jnp.max(logw, axis=-1, keepdims=True))
        w = jnp.exp(logw - m_t)
        a = jnp.exp(inter - m_t)
        q = q_all[:, hs]
        k = k_all[:, hs]
        v = v_ref[:, hs]
        cmat = c_s[h]
        nvec = n_s[h:h + 1, :]
        wqk = w * _bdot_nt(q, k)
        num = _bdot(wqk, v) + a * _bdot_nt(q, cmat)
        den = jnp.sum(wqk, axis=-1, keepdims=True) + a * jnp.sum(q * nvec, axis=-1, keepdims=True)
        hh = num / jnp.maximum(jnp.abs(den), jnp.exp(-m_t))
        m_new = m_t[chunk - 1:chunk, :]
        b_last = b_c[chunk - 1:chunk, :]
        wl = jnp.exp(b_last - b_c + ig_c - m_new)
        al = jnp.exp(b_last + m_prev - m_new)
        c_s[h] = al * cmat + lax.dot_general((v * wl).astype(BF16), k.astype(BF16), TN_DIMS,
                                             preferred_element_type=F32)
        n_s[h:h + 1, :] = al * nvec + jnp.sum(wl * k, axis=0, keepdims=True)
        m_s[h:h + 1, :] = jnp.broadcast_to(m_new, (1, LANES))
        hn = hh * lax.rsqrt(jnp.mean(hh * hh, axis=-1, keepdims=True) + EPS) * mn_ref[:, hs]
        out_ref[:, hs] = _sigmoid(og_ref[:, hs]) * hn

    @pl.when(step == pl.num_programs(1) - 1)
    def _():
        cout_ref[...] = c_s[...]
        nout_ref[...] = n_s[...]
        mout_ref[...] = m_s[...]


def _mlstm_call(qk, v, og, gc, conv_w, conv_b, m_norm, conv0, c0, n0, m0, batch, seq, chunk, t_valid):
    nc = seq // chunk
    n = batch * seq
    row = lambda w: pl.BlockSpec((chunk, w), lambda b, c: (b * nc + c, 0))
    full2 = lambda shape: pl.BlockSpec(shape, lambda b, c: (0, 0))
    per_b = lambda shape: pl.BlockSpec((None,) + shape, lambda b, c: (b,) + (0,) * len(shape))
    dh = M_HEAD_DIM
    return pl.pallas_call(
        functools.partial(_mlstm_kernel, chunk, t_valid),
        grid=(batch, nc),
        in_specs=[row(2 * M_WIDTH), row(M_WIDTH), row(M_WIDTH), row(LANES),
                  full2((CONV_W, 2 * M_WIDTH)), full2((1, 2 * M_WIDTH)), full2((1, M_WIDTH)),
                  per_b((SUBLANES, 2 * M_WIDTH)), per_b((M_HEADS, dh, dh)), per_b((M_HEADS, dh)),
                  per_b((SUBLANES, LANES))],
        out_specs=[row(M_WIDTH), per_b((M_HEADS, dh, dh)), per_b((M_HEADS, dh)), per_b((SUBLANES, LANES))],
        out_shape=[jax.ShapeDtypeStruct((n, M_WIDTH), F32),
                   jax.ShapeDtypeStruct((batch, M_HEADS, dh, dh), F32),
                   jax.ShapeDtypeStruct((batch, M_HEADS, dh), F32),
                   jax.ShapeDtypeStruct((batch, SUBLANES, LANES), F32)],
        scratch_shapes=[pltpu.VMEM((SUBLANES + chunk, 2 * M_WIDTH), F32),
                        pltpu.VMEM((M_HEADS, dh, dh), F32),
                        pltpu.VMEM((M_HEADS, dh), F32),
                        pltpu.VMEM((SUBLANES, LANES), F32)],
        compiler_params=_cparams("parallel", "arbitrary"),
        name="mlstm",
    )(qk, v, og, gc, conv_w, conv_b, m_norm, conv0, c0, n0, m0)


def _pack_words(x):
    bits = pltpu.bitcast(x.astype(BF16).astype(F32), jnp.uint32)
    half = x.shape[1] // 2
    lo = lax.shift_right_logical(bits[:, :half], jnp.uint32(16))
    hi = bits[:, half:] & jnp.uint32(0xFFFF0000)
    return pltpu.bitcast(lo | hi, jnp.int32)


def _out_kernel(att_ref, mo_ref, x_ref, g1_ref, sc_ref, sh_ref, gn_ref, wa_ref, wm_ref, wq_ref,
                xo_ref, h2_ref, qp_ref):
    mix = (jnp.dot(att_ref[...].astype(BF16), wa_ref[...], preferred_element_type=F32)
           + jnp.dot(mo_ref[...].astype(BF16), wm_ref[...], preferred_element_type=F32))
    x = x_ref[...] + g1_ref[...] * mix
    xo_ref[...] = x
    y = x * lax.rsqrt(jnp.mean(x * x, axis=-1, keepdims=True) + EPS) * gn_ref[...]
    h2 = y * (1.0 + sc_ref[...]) + sh_ref[...]
    qp_ref[...] = jnp.dot(h2.astype(BF16), wq_ref[...], preferred_element_type=F32)
    h2_ref[...] = _pack_words(h2)


def _out_call(att, mo, x, g1, sc, sh, gnorm, wa, wm, wq, per_token, tokens_per_req):
    n = x.shape[0]
    tm = min(256, n if per_token else tokens_per_req)
    mod = _mod_spec(per_token, tm, tokens_per_req)
    full = lambda shape: pl.BlockSpec(shape, lambda i: (0,) * len(shape))
    row = lambda w: pl.BlockSpec((tm, w), lambda i: (i, 0))
    nq = wq.shape[1]
    return pl.pallas_call(
        _out_kernel,
        grid=(n // tm,),
        in_specs=[row(A_WIDTH), row(M_WIDTH), row(D_MODEL), mod, mod, mod, full((1, D_MODEL)),
                  full(wa.shape), full(wm.shape), full(wq.shape)],
        out_specs=[row(D_MODEL), row(D_MODEL // 2), row(nq)],
        out_shape=[jax.ShapeDtypeStruct((n, D_MODEL), F32), jax.ShapeDtypeStruct((n, D_MODEL // 2), jnp.int32),
                   jax.ShapeDtypeStruct((n, nq), F32)],
        compiler_params=_cparams("parallel"),
        name="out_proj",
    )(att, mo, x, g1, sc, sh, gnorm, wa, wm, wq)


def _pk_cells():
    return [(a, b) for a in range(P_TOPK) for b in range(P_TOPK) if (a + 1) * (b + 1) <= P_TOPK]


PK_CELL_ROWS = 64


def _pk_expand_mats():
    cells = _pk_cells()
    e0 = np.zeros((PK_CELL_ROWS, LANES), np.float32)
    e1 = np.zeros((PK_CELL_ROWS, LANES), np.float32)
    for j, (a, b) in enumerate(cells):
        e0[j, a] = 1.0
        e1[j, b] = 1.0
    return e0, e1, len(cells)


def _top_rows(s, rowf, rounds):
    n_rows = s.shape[0]
    vals, idxs = [], []
    for _ in range(rounds):
        m = jnp.max(s, axis=0, keepdims=True)
        i = jnp.min(jnp.where(s == m, rowf, float(n_rows)), axis=0, keepdims=True)
        vals.append(m)
        idxs.append(i)
        s = jnp.where(rowf == i, -jnp.inf, s)
    return jnp.concatenate(vals, axis=0), jnp.concatenate(idxs, axis=0)


def _select_kernel(n_cells, qp_ref, keys_ref, e0_ref, e1_ref, idx_ref, gw_ref, idx_s, gw_s):
    tm = qp_ref.shape[0]
    keyf = lax.broadcasted_iota(jnp.int32, (N_KEYS, tm), 0).astype(F32)
    cellf = lax.broadcasted_iota(jnp.int32, (PK_CELL_ROWS, tm), 0).astype(F32)
    e0 = e0_ref[...]
    e1 = e1_ref[...]
    pad = jnp.zeros((LANES - P_TOPK, tm), F32)

    def head(h, carry):
        h = jnp.asarray(h, jnp.int32)
        sub = []
        for c in range(2):
            col = pl.multiple_of((h * 2 + c) * P_HALF, P_HALF)
            s = _bdot_nt(keys_ref[h, c], qp_ref[:, pl.ds(col, P_HALF)])
            sub.append(_top_rows(s, keyf, P_TOPK))
        (v0, i0), (v1, i1) = sub
        expand = lambda e, x: jnp.dot(e, jnp.concatenate([x, pad], axis=0), precision=HIGHEST,
                                      preferred_element_type=F32)
        cand = expand(e0, v0) + expand(e1, v1)
        cidx = expand(e0, i0 * float(N_KEYS)) + expand(e1, i1)
        cand = jnp.where(cellf < n_cells, cand, -jnp.inf)
        best, eidx = [], []
        for _ in range(P_TOPK):
            m = jnp.max(cand, axis=0, keepdims=True)
            j = jnp.min(jnp.where(cand == m, cellf, float(PK_CELL_ROWS)), axis=0, keepdims=True)
            hit = cellf == j
            eidx.append(jnp.max(jnp.where(hit, cidx, -1.0), axis=0, keepdims=True))
            best.append(m)
            cand = jnp.where(hit, -jnp.inf, cand)
        best = jnp.concatenate(best, axis=0)
        e = jnp.exp(best - best[0:1, :])
        row0 = pl.multiple_of(h * P_TOPK, P_TOPK)
        gw_s[pl.ds(row0, P_TOPK), :] = e / jnp.sum(e, axis=0, keepdims=True)
        idx_s[pl.ds(row0, P_TOPK), :] = jnp.concatenate(eidx, axis=0)
        return carry

    lax.fori_loop(0, P_HEADS, head, 0)
    idx_ref[...] = idx_s[...].T.astype(jnp.int32)
    gw_ref[...] = gw_s[...].T


def _select_call(qp, keys_bf16):
    n = qp.shape[0]
    tm = min(LANES, n)
    e0, e1, n_cells = _pk_expand_mats()
    full = lambda shape: pl.BlockSpec(shape, lambda i: (0,) * len(shape))
    return pl.pallas_call(
        functools.partial(_select_kernel, n_cells),
        grid=(n // tm,),
        in_specs=[pl.BlockSpec((tm, qp.shape[1]), lambda i: (i, 0)), full(keys_bf16.shape),
                  full((PK_CELL_ROWS, LANES)), full((PK_CELL_ROWS, LANES))],
        out_specs=[pl.BlockSpec((tm, LANES), lambda i: (i, 0)), pl.BlockSpec((tm, LANES), lambda i: (i, 0))],
        out_shape=[jax.ShapeDtypeStruct((n, LANES), jnp.int32), jax.ShapeDtypeStruct((n, LANES), F32)],
        scratch_shapes=[pltpu.VMEM((P_HEADS * P_TOPK, tm), F32), pltpu.VMEM((P_HEADS * P_TOPK, tm), F32)],
        compiler_params=_cparams("parallel"),
        name="peer_select",
    )(qp, keys_bf16, jnp.asarray(e0), jnp.asarray(e1))


N_SEL = P_HEADS * P_TOPK

SC_CORES = 2
SC_SUBCORES = 16
SC_LANES = 16
SC_TOK_BLOCK = 8
SC_ROWS = 32
SC_ROW_BLOCK = 16
SC_ACC_CHAINS = 4
SC_ACC_ROWS = 16
SC_NBUF = 4
N_WCOL = D_MODEL // (2 * SC_LANES)


def _pack_kernel(t_ref, o_ref):
    o_ref[...] = _pack_words(t_ref[...])


def _pack_table(tables, layer):
    _, e, dcol = tables.shape
    tm = min(1024, e)
    return pl.pallas_call(
        _pack_kernel,
        grid=(e // tm,),
        in_specs=[pl.BlockSpec((None, tm, dcol), lambda i: (layer, i, 0))],
        out_specs=pl.BlockSpec((tm, dcol // 2), lambda i: (i, 0)),
        out_shape=jax.ShapeDtypeStruct((e, dcol // 2), jnp.int32),
        compiler_params=_cparams("parallel"),
        name="pack_table",
    )(tables)


def _sc_gelu(x):
    z = math.sqrt(2.0 / math.pi) * (x + 0.044715 * (x * x * x))
    t = 1.0 - 2.0 / (jnp.exp(2.0 * z) + 1.0)
    return x * (0.5 * (1.0 + t))


def _sc_expert_body(tokens_per_worker, idx_hbm, gw_hbm, h2_hbm, u_hbm, v_hbm, y_hbm,
                    idx_v, gw_v, x_v, o_v, buf, coef_v, tr_v, sem, in_sem, out_sem):
    wid = lax.axis_index("s") * SC_CORES + lax.axis_index("c")
    base = wid * tokens_per_worker
    lane = lax.iota(jnp.int32, SC_LANES)
    n_gather = N_SEL // SC_ROWS
    zero = jnp.zeros((SC_LANES,), F32)

    n_steps = 2 * n_gather
    assert n_steps % SC_NBUF == 0

    def gather(p, tt, i):
        table = u_hbm if i < n_gather else v_hbm
        j = i % n_gather
        slot = i % SC_NBUF
        return pltpu.make_async_copy(table.at[idx_v.at[p, tt, pl.ds(j * SC_ROWS, SC_ROWS)]], buf.at[slot],
                                     sem.at[slot])

    def unpack(w):
        lo = lax.bitcast_convert_type(lax.shift_left(w, jnp.full((SC_LANES,), 16, jnp.int32)), F32)
        hi = lax.bitcast_convert_type(w & jnp.full((SC_LANES,), -65536, jnp.int32), F32)
        return lo, hi

    def packed(w):
        return plsc.bitcast(w, BF16)

    def unpack_sum(s):
        return unpack(plsc.bitcast(s, jnp.int32))

    def act_chunk(p, tt, j, slot):
        @pl.loop(0, SC_ROWS // SC_LANES)
        def _(half):
            for rb in range(SC_LANES // SC_ROW_BLOCK):
                r0 = half * SC_LANES + rb * SC_ROW_BLOCK

                def col(c, accs):
                    w0 = pl.multiple_of(c * (2 * SC_LANES), 2 * SC_LANES)
                    xa = packed(x_v[p, tt, pl.ds(w0, SC_LANES)])
                    xb = packed(x_v[p, tt, pl.ds(w0 + SC_LANES, SC_LANES)])
                    out = []
                    for r, a in enumerate(accs):
                        ua = packed(buf[slot, r0 + r, pl.ds(w0, SC_LANES)])
                        ub = packed(buf[slot, r0 + r, pl.ds(w0 + SC_LANES, SC_LANES)])
                        lo, hi = unpack_sum(ua * xa + ub * xb)
                        out.append(a + lo + hi)
                    return tuple(out)

                accs = lax.fori_loop(0, N_WCOL // 2, col, (zero,) * SC_ROW_BLOCK)
                for r in range(SC_ROW_BLOCK):
                    tr_v[pl.ds((rb * SC_ROW_BLOCK + r) * SC_LANES, SC_LANES)] = accs[r]
            tot = zero
            for jj in range(SC_LANES):
                tot = tot + plsc.load_gather(tr_v, [lane * SC_LANES + jj])
            k0 = pl.multiple_of(j * SC_ROWS + half * SC_LANES, SC_LANES)
            coef_v[pl.ds(k0, SC_LANES)] = gw_v[p, tt, pl.ds(k0, SC_LANES)] * _sc_gelu(tot)

    def acc_chunk(p, tt, j, slot, first):
        def tree_sum(parts):
            while len(parts) > 1:
                parts = [parts[i] + parts[i + 1] for i in range(0, len(parts), 2)]
            return parts[0]

        for rb in range(SC_ROWS // SC_ACC_ROWS):
            rows = list(range(rb * SC_ACC_ROWS, (rb + 1) * SC_ACC_ROWS))
            splat = {}
            for r in rows:
                c16 = plsc.load_gather(coef_v, [jnp.full((SC_LANES,), j * SC_ROWS + r, jnp.int32)])
                splat[r] = plsc.pack(c16, c16, format=plsc.PackFormat.INTERLEAVED)
            fresh = first and rb == 0

            @plsc.parallel_loop(0, N_WCOL)
            def _(c):
                woff = pl.multiple_of(c * SC_LANES, SC_LANES)
                hoff = pl.multiple_of(c * SC_LANES + D_MODEL // 2, SC_LANES)
                pa, pb = [], []
                for n in range(0, SC_ACC_ROWS, 2):
                    r0, r1 = rows[n], rows[n + 1]
                    s = (splat[r0] * packed(buf[slot, r0, pl.ds(woff, SC_LANES)])
                         + splat[r1] * packed(buf[slot, r1, pl.ds(woff, SC_LANES)]))
                    lo, hi = unpack_sum(s)
                    if n // 2 < SC_ACC_CHAINS:
                        pa.append(lo)
                        pb.append(hi)
                    else:
                        pa[(n // 2) % SC_ACC_CHAINS] = pa[(n // 2) % SC_ACC_CHAINS] + lo
                        pb[(n // 2) % SC_ACC_CHAINS] = pb[(n // 2) % SC_ACC_CHAINS] + hi
                sa, sb = tree_sum(pa), tree_sum(pb)
                if not fresh:
                    sa = sa + o_v[p, tt, pl.ds(woff, SC_LANES)]
                    sb = sb + o_v[p, tt, pl.ds(hoff, SC_LANES)]
                o_v[p, tt, pl.ds(woff, SC_LANES)] = sa
                o_v[p, tt, pl.ds(hoff, SC_LANES)] = sb

    n_blocks = tokens_per_worker // SC_TOK_BLOCK

    def block_start(blk):
        return pl.multiple_of(base + blk * SC_TOK_BLOCK, SC_TOK_BLOCK)

    def in_copies(blk, p):
        rows = pl.ds(block_start(blk), SC_TOK_BLOCK)
        return [pltpu.make_async_copy(idx_hbm.at[rows], idx_v.at[p], in_sem.at[p]),
                pltpu.make_async_copy(gw_hbm.at[rows], gw_v.at[p], in_sem.at[p]),
                pltpu.make_async_copy(h2_hbm.at[rows], x_v.at[p], in_sem.at[p])]

    def out_copy(blk, p):
        return pltpu.make_async_copy(o_v.at[p], y_hbm.at[pl.ds(block_start(blk), SC_TOK_BLOCK)], out_sem.at[p])

    ahead = SC_NBUF - 1
    for cp in in_copies(0, 0):
        cp.start()
    for cp in in_copies(0, 0):
        cp.wait()
    for cp in in_copies(1, 1):
        cp.start()
    for i in range(ahead):
        gather(0, 0, i).start()

    @pl.loop(0, n_blocks)
    def _(blk):
        p = lax.rem(blk, 2)

        @pl.when(blk >= 2)
        def _():
            out_copy(blk - 2, p).wait()

        @pl.loop(0, SC_TOK_BLOCK)
        def _(tt):
            for i in range(n_steps):
                if i + ahead < n_steps:
                    gather(p, tt, i + ahead).start()
                else:
                    nxt = i + ahead - n_steps

                    @pl.when(tt + 1 < SC_TOK_BLOCK)
                    def _():
                        gather(p, tt + 1, nxt).start()

                    @pl.when(jnp.logical_and(tt + 1 == SC_TOK_BLOCK, blk + 1 < n_blocks))
                    def _():
                        if nxt == 0:
                            for cp in in_copies(blk + 1, 1 - p):
                                cp.wait()
                        gather(1 - p, 0, nxt).start()
                gather(p, tt, i).wait()
                if i < n_gather:
                    act_chunk(p, tt, i, i % SC_NBUF)
                else:
                    acc_chunk(p, tt, i - n_gather, i % SC_NBUF, i == n_gather)

        out_copy(blk, p).start()

        @pl.when(blk + 2 < n_blocks)
        def _():
            for cp in in_copies(blk + 2, p):
                cp.start()

    for blk in (n_blocks - 2, n_blocks - 1):
        out_copy(blk, blk % 2).wait()


def _sc_expert_call(idx, gw, xw, u, v):
    n = idx.shape[0]
    workers = SC_CORES * SC_SUBCORES
    assert n % (workers * SC_TOK_BLOCK) == 0 and n // (workers * SC_TOK_BLOCK) >= 2
    mesh = plsc.VectorSubcoreMesh(core_axis_name="c", subcore_axis_name="s")
    return pl.kernel(
        functools.partial(_sc_expert_body, n // workers),
        out_type=jax.ShapeDtypeStruct((n, D_MODEL), F32),
        mesh=mesh,
        scratch_types=[pltpu.VMEM((2, SC_TOK_BLOCK, N_SEL), jnp.int32),
                       pltpu.VMEM((2, SC_TOK_BLOCK, N_SEL), F32),
                       pltpu.VMEM((2, SC_TOK_BLOCK, D_MODEL // 2), jnp.int32),
                       pltpu.VMEM((2, SC_TOK_BLOCK, D_MODEL), F32),
                       pltpu.VMEM((SC_NBUF, SC_ROWS, D_MODEL // 2), jnp.int32),
                       pltpu.VMEM((N_SEL,), F32),
                       pltpu.VMEM((SC_LANES * SC_LANES,), F32),
                       pltpu.SemaphoreType.DMA((SC_NBUF,)), pltpu.SemaphoreType.DMA((2,)),
                       pltpu.SemaphoreType.DMA((2,))],
        compiler_params=pltpu.CompilerParams(needs_layout_passes=False),
        name="peer_experts_sc",
    )(idx, gw, xw, u, v)


def _resid_kernel(x_ref, y_ref, g_ref, o_ref):
    o_ref[...] = x_ref[...] + g_ref[...] * y_ref[...]


def _resid_call(x, y, g2, per_token, tokens_per_req):
    n = x.shape[0]
    tm = min(512, n if per_token else tokens_per_req)
    row = pl.BlockSpec((tm, D_MODEL), lambda i: (i, 0))
    return pl.pallas_call(
        _resid_kernel,
        grid=(n // tm,),
        in_specs=[row, row, _mod_spec(per_token, tm, tokens_per_req)],
        out_specs=row,
        out_shape=jax.ShapeDtypeStruct((n, D_MODEL), F32),
        compiler_params=_cparams("parallel"),
        name="peer_residual",
    )(x, y, g2)


def _final_kernel(x_ref, g_ref, o_ref):
    x = x_ref[...]
    o_ref[...] = x * lax.rsqrt(jnp.mean(x * x, axis=-1, keepdims=True) + EPS) * g_ref[...]


def _final_call(x, g):
    n = x.shape[0]
    tm = min(512, n)
    return pl.pallas_call(
        _final_kernel,
        grid=(n // tm,),
        in_specs=[pl.BlockSpec((tm, D_MODEL), lambda i: (i, 0)), pl.BlockSpec((1, D_MODEL), lambda i: (0, 0))],
        out_specs=pl.BlockSpec((tm, D_MODEL), lambda i: (i, 0)),
        out_shape=jax.ShapeDtypeStruct((n, D_MODEL), F32),
        compiler_params=_cparams("parallel"),
        name="final_norm",
    )(x, g)


def _split_w_in(w_in_l, gate_b_l):
    cuts = np.cumsum([A_WIDTH + 2 * A_KV_WIDTH, 2 * M_WIDTH, M_WIDTH, M_WIDTH]).tolist()
    wa = w_in_l[:, :cuts[0]].astype(BF16)
    wqk = w_in_l[:, cuts[0]:cuts[1]].astype(BF16)
    wv = w_in_l[:, cuts[1]:cuts[2]].astype(BF16)
    wo = w_in_l[:, cuts[2]:cuts[3]].astype(BF16)
    ng = 2 * M_HEADS
    wg = jnp.pad(w_in_l[:, cuts[3]:], ((0, 0), (0, LANES - ng))).astype(BF16)
    gb = jnp.pad(gate_b_l.astype(F32), (0, LANES - ng)).reshape(1, LANES)
    return wa, wqk, wv, wo, wg, gb


def _layer(x, mods, per_token, batch, seq, t_valid, lw, bias_p, bias_c, bias_n, kv_cache, conv0, state,
           after=None):
    (norm_mix, norm_ffn, w_in, conv_w, conv_b, gate_b, sinks, m_norm, w_out, peer_query, peer_keys,
     peer_u, peer_v) = lw
    if after is not None:
        x, _ = lax.optimization_barrier((x, after))
    sh1, sc1, g1, sh2, sc2, g2 = mods
    wa, wqk, wv, wo, wg, gb = _split_w_in(w_in, gate_b)
    qkv, qkm, vm, om, gc = _in_call(x, sc1, sh1, norm_mix.reshape(1, -1), wa, wqk, wv, wo, wg, gb,
                                    per_token, seq)
    if kv_cache is None:
        att = _attn_p_call(qkv, bias_p, sinks, batch, seq)
        kv3 = qkv.reshape(batch, seq, -1)
        new_k = kv3[:, seq - WINDOW:, A_WIDTH:A_WIDTH + A_KV_WIDTH]
        new_v = kv3[:, seq - WINDOW:, A_WIDTH + A_KV_WIDTH:]
        chunk = M_CHUNK
    else:
        att, new_k, new_v = _attn_s_call(qkv, kv_cache[0], kv_cache[1], bias_c, bias_n, sinks, t_valid)
        chunk = seq
    c0, n0, m0 = state
    mo, c_new, n_new, m_new = _mlstm_call(qkm, vm, om, gc, conv_w, conv_b.reshape(1, -1),
                                          m_norm.reshape(1, -1), conv0, c0, n0, m0,
                                          batch, seq, chunk, min(t_valid, chunk))
    new_conv = qkm.reshape(batch, seq, -1)[:, t_valid - (CONV_W - 1):t_valid]
    pad_rows = seq - t_valid
    if pad_rows:
        keep = lambda a: a.reshape(batch, seq, -1)[:, :t_valid].reshape(batch * t_valid, -1)
        att, mo, x, g1, sc2, sh2, g2 = (keep(a) for a in (att, mo, x, g1, sc2, sh2, g2))
    x_mid, xw, qp = _out_call(att, mo, x, g1, sc2, sh2, norm_ffn.reshape(1, -1),
                              w_out[:A_WIDTH].astype(BF16), w_out[A_WIDTH:].astype(BF16),
                              peer_query.astype(BF16), per_token, t_valid)
    idx, gw = _select_call(qp, peer_keys.astype(BF16))
    y = _sc_expert_call(idx, gw, xw, peer_u, peer_v)
    x_new = _resid_call(x_mid, y, g2, per_token, t_valid)
    if pad_rows:
        x_new = jnp.pad(x_new.reshape(batch, t_valid, -1), ((0, 0), (0, pad_rows), (0, 0))).reshape(batch * seq, -1)
    new_k = new_k.reshape(batch, WINDOW, A_KV_HEADS, A_HEAD_DIM)
    new_v = new_v.reshape(batch, WINDOW, A_KV_HEADS, A_HEAD_DIM)
    return x_new, (new_k, new_v, new_conv, c_new, n_new, m_new[:, :M_HEADS, 0]), idx


def _prompt_group_sizes(n_req):
    if n_req < PROMPT_GROUPS:
        return [1] * n_req
    mid, n_mid = n_req - 2, PROMPT_GROUPS - 2
    weights = [i + 2 for i in range(n_mid)]
    sizes = [max(1, mid * w // sum(weights)) for w in weights]
    for i in range(mid - sum(sizes)):
        sizes[n_mid - 1 - i % n_mid] += 1
    return [1] + sizes + [1]


def kernel(x_prompt, x_sample, c_prompt, c_sample, cache_k, cache_v, state_conv, state_C, state_n, state_m, rel_bias, w_ada, b_ada, norm_mix, norm_ffn, w_in, conv_w, conv_b, gate_b, attn_sinks, m_norm, w_out, peer_query, peer_keys, peer_u, peer_v, norm_final):
    depth = w_ada.shape[0]
    bp, tp, d = x_prompt.shape
    bs, ts, _ = x_sample.shape
    assert tp % WINDOW == 0 and tp % M_CHUNK == 0 and ts <= SAMPLE_PAD and ts >= CONV_W - 1

    mod_all = _ada_call(jnp.concatenate([c_prompt, c_sample], axis=0), w_ada, b_ada)

    qi = np.arange(WINDOW)[:, None]
    bias_p = _bias_call(rel_bias, qi + WINDOW - np.arange(2 * WINDOW)[None, :])
    qs = np.arange(SAMPLE_PAD)[:, None]
    bias_c = _bias_call(rel_bias, qs + WINDOW - np.arange(WINDOW)[None, :])
    bias_n = _bias_call(rel_bias, qs - np.arange(SAMPLE_PAD)[None, :])

    sizes = _prompt_group_sizes(bp)
    starts = np.cumsum([0] + sizes).tolist()
    xg = [x_prompt[starts[g]:starts[g + 1]].reshape(sizes[g] * tp, d) for g in range(len(sizes))]
    xs = jnp.pad(x_sample, ((0, 0), (0, SAMPLE_PAD - ts), (0, 0))).reshape(bs * SAMPLE_PAD, d)
    halo_pad = ((0, 0), (SUBLANES - (CONV_W - 1), 0), (0, 0))

    st_p, st_s = [], []
    for l in range(depth):
        lw = (norm_mix[l], norm_ffn[l], w_in[l], conv_w[l], conv_b[l], gate_b[l], attn_sinks[l], m_norm[l],
              w_out[l], peer_query[l], peer_keys[l], _pack_table(peer_u, l), _pack_table(peer_v, l))
        mod_s = [jnp.repeat(m, SAMPLE_PAD, axis=0) for m in jnp.split(mod_all[l, bp:], 6, axis=-1)]
        sp_groups = []
        for g, bg in enumerate(sizes):
            mod_g = [m.reshape(bg, 1, d) for m in jnp.split(mod_all[l, starts[g]:starts[g + 1]], 6, axis=-1)]
            zero_state = (jnp.zeros((bg, M_HEADS, M_HEAD_DIM, M_HEAD_DIM), F32),
                          jnp.zeros((bg, M_HEADS, M_HEAD_DIM), F32),
                          jnp.zeros((bg, SUBLANES, LANES), F32))
            zero_conv = jnp.zeros((bg, SUBLANES, 2 * M_WIDTH), F32)
            xg[g], sp, last_idx = _layer(xg[g], mod_g, False, bg, tp, tp, lw, bias_p, None, None, None,
                                         zero_conv, zero_state)
            sp_groups.append(sp)
        st_p.append([jnp.concatenate([sp[i] for sp in sp_groups], axis=0) for i in range(6)])
        state_s = (state_C[l].astype(F32), state_n[l].astype(F32),
                   jnp.broadcast_to(jnp.pad(state_m[l].astype(F32), ((0, 0), (0, SUBLANES - M_HEADS)))[:, :, None],
                                    (bs, SUBLANES, LANES)))
        kv_cache = (cache_k[l].reshape(bs, WINDOW, A_KV_WIDTH), cache_v[l].reshape(bs, WINDOW, A_KV_WIDTH))
        xs, ss, _ = _layer(xs, mod_s, True, bs, SAMPLE_PAD, ts, lw, None, bias_c, bias_n, kv_cache,
                           jnp.pad(state_conv[l].astype(F32), halo_pad), state_s, after=last_idx)
        st_s.append(ss)

    gfin = norm_final.reshape(1, d)
    y_prompt = jnp.concatenate([_final_call(x, gfin).reshape(bg, tp, d) for x, bg in zip(xg, sizes)], axis=0)
    y_sample = _final_call(xs, gfin).reshape(bs, SAMPLE_PAD, d)[:, :ts]
    outs_p = [jnp.stack([s[i] for s in st_p]) for i in range(6)]
    outs_s = [jnp.stack([s[i] for s in st_s]) for i in range(6)]
    return (y_prompt, y_sample, *outs_p, *outs_s)
```

```python
import functools
import math

import numpy as np
import jax
import jax.numpy as jnp
from jax import lax
from jax.experimental import pallas as pl
from jax.experimental.pallas import tpu as pltpu
from jax.experimental.pallas import tpu_sc as plsc

F32 = jnp.float32
BF16 = jnp.bfloat16
HIGHEST = lax.Precision.HIGHEST

D_MODEL = 1024
A_HEADS = 8
A_KV_HEADS = 2
A_GROUP = A_HEADS // A_KV_HEADS
A_HEAD_DIM = 64
A_WIDTH = A_HEADS * A_HEAD_DIM
A_KV_WIDTH = A_KV_HEADS * A_HEAD_DIM
WINDOW = 128
ATT_SCALE = A_HEAD_DIM ** -0.5
N_BUCKETS = 32
MAX_DISTANCE = WINDOW
M_HEADS = 4
M_HEAD_DIM = 128
M_WIDTH = M_HEADS * M_HEAD_DIM
CONV_W = 4
M_CHUNK = 64
N_KEYS = 128
P_HEADS = 8
P_TOPK = 16
P_KEY_DIM = 256
P_HALF = P_KEY_DIM // 2
EPS = 1e-6
NEG_INF = -1e30

LANES = 128
SUBLANES = 8
SAMPLE_PAD = SUBLANES
VMEM_LIMIT = 48 * 1024 * 1024
PROMPT_GROUPS = 9

NT_DIMS = (((1,), (1,)), ((), ()))
TN_DIMS = (((0,), (0,)), ((), ()))


def _cparams(*sem):
    return pltpu.CompilerParams(dimension_semantics=sem, vmem_limit_bytes=VMEM_LIMIT)


def _bdot(a, b):
    return jnp.dot(a.astype(BF16), b.astype(BF16), preferred_element_type=F32)


def _bdot_nt(a, b):
    return lax.dot_general(a.astype(BF16), b.astype(BF16), NT_DIMS, preferred_element_type=F32)


def _sigmoid(x):
    return 1.0 / (1.0 + jnp.exp(-x))


def _log_sigmoid(x):
    return jnp.minimum(x, 0.0) - jnp.log1p(jnp.exp(-jnp.abs(x)))


def _ada_kernel(c_ref, w_ref, b_ref, o_ref):
    c = c_ref[...]
    s = c * _sigmoid(c)
    o_ref[...] = jnp.dot(s, w_ref[...], precision=HIGHEST, preferred_element_type=F32) + b_ref[...]


def _ada_call(c_all, w_ada, b_ada):
    depth, d, n6 = w_ada.shape
    rows = c_all.shape[0]
    bn = 1024
    return pl.pallas_call(
        _ada_kernel,
        grid=(depth, n6 // bn),
        in_specs=[
            pl.BlockSpec((rows, d), lambda l, j: (0, 0)),
            pl.BlockSpec((None, d, bn), lambda l, j: (l, 0, j)),
            pl.BlockSpec((None, 1, bn), lambda l, j: (l, 0, j)),
        ],
        out_specs=pl.BlockSpec((None, rows, bn), lambda l, j: (l, 0, j)),
        out_shape=jax.ShapeDtypeStruct((depth, rows, n6), F32),
        compiler_params=_cparams("parallel", "parallel"),
        name="ada_mod",
    )(c_all, w_ada, b_ada.reshape(depth, 1, n6))


def _mod_spec(per_token, tm, tokens_per_req):
    if per_token:
        return pl.BlockSpec((tm, D_MODEL), lambda i: (i, 0))
    tiles = tokens_per_req // tm
    return pl.BlockSpec((None, 1, D_MODEL), lambda i: (i // tiles, 0, 0))


def _in_kernel(x_ref, sc_ref, sh_ref, g_ref, wa_ref, wqk_ref, wv_ref, wo_ref, wg_ref, gb_ref,
               qkv_ref, qkm_ref, v_ref, o_ref, gc_ref):
    x = x_ref[...]
    y = x * lax.rsqrt(jnp.mean(x * x, axis=-1, keepdims=True) + EPS) * g_ref[...]
    h = (y * (1.0 + sc_ref[...]) + sh_ref[...]).astype(BF16)
    qkv_ref[...] = jnp.dot(h, wa_ref[...], preferred_element_type=F32)
    qkm_ref[...] = jnp.dot(h, wqk_ref[...], preferred_element_type=F32)
    v_ref[...] = jnp.dot(h, wv_ref[...], preferred_element_type=F32)
    o_ref[...] = jnp.dot(h, wo_ref[...], preferred_element_type=F32)
    g = jnp.dot(h, wg_ref[...], preferred_element_type=F32) + gb_ref[...]
    lane = lax.broadcasted_iota(jnp.int32, g.shape, 1)
    gc_ref[...] = jnp.where(lane < M_HEADS, g, jnp.where(lane < 2 * M_HEADS, _log_sigmoid(g), 0.0))


def _in_call(x, sc, sh, gnorm, wa, wqk, wv, wo, wg, gb, per_token, tokens_per_req):
    n = x.shape[0]
    tm = min(512, n if per_token else tokens_per_req)
    mod = _mod_spec(per_token, tm, tokens_per_req)
    full = lambda shape: pl.BlockSpec(shape, lambda i: (0,) * len(shape))
    row = lambda w: pl.BlockSpec((tm, w), lambda i: (i, 0))
    return pl.pallas_call(
        _in_kernel,
        grid=(n // tm,),
        in_specs=[row(D_MODEL), mod, mod, full((1, D_MODEL)), full(wa.shape), full(wqk.shape),
                  full(wv.shape), full(wo.shape), full(wg.shape), full((1, LANES))],
        out_specs=[row(wa.shape[1]), row(wqk.shape[1]), row(wv.shape[1]), row(wo.shape[1]), row(LANES)],
        out_shape=[jax.ShapeDtypeStruct((n, w), F32)
                   for w in (wa.shape[1], wqk.shape[1], wv.shape[1], wo.shape[1], LANES)],
        compiler_params=_cparams("parallel"),
        name="in_proj",
    )(x, sc, sh, gnorm, wa, wqk, wv, wo, wg, gb)


def _t5_bucket_np(dist):
    n = np.maximum(dist, 0)
    max_exact = N_BUCKETS // 2
    nf = np.maximum(n, 1).astype(np.float64)
    large = max_exact + (np.log(nf / max_exact) / math.log(MAX_DISTANCE / max_exact)
                         * (N_BUCKETS - max_exact)).astype(np.int32)
    return np.where(n < max_exact, n, np.minimum(large, N_BUCKETS - 1)).astype(np.int32)


def _bias_kernel(bucket_ref, rel_ref, o_ref):
    bucket = bucket_ref[...]
    for h in range(A_HEADS):
        acc = jnp.zeros(bucket.shape, F32)
        for b in range(N_BUCKETS):
            acc = jnp.where(bucket == b, rel_ref[b, h], acc)
        o_ref[h] = acc


def _bias_call(rel_bias, dist):
    bucket = jnp.asarray(_t5_bucket_np(dist))
    nq, nk = dist.shape
    return pl.pallas_call(
        _bias_kernel,
        in_specs=[pl.BlockSpec((nq, nk), lambda: (0, 0)),
                  pl.BlockSpec(memory_space=pltpu.SMEM)],
        out_specs=pl.BlockSpec((A_HEADS, nq, nk), lambda: (0, 0, 0)),
        out_shape=jax.ShapeDtypeStruct((A_HEADS, nq, nk), F32),
        name="t5_bias",
    )(bucket, rel_bias)


def _attn_p_kernel(q_ref, kp_ref, kc_ref, vp_ref, vc_ref, bias_ref, sink_ref, o_ref):
    i = pl.program_id(1)
    qi = lax.broadcasted_iota(jnp.int32, (WINDOW, WINDOW), 0)
    kj = lax.broadcasted_iota(jnp.int32, (WINDOW, WINDOW), 1)
    valid_prev = jnp.logical_and(kj > qi, i > 0)
    valid_cur = kj <= qi
    q = q_ref[...]
    for h in range(A_HEADS):
        kv = h // A_GROUP
        qh = q[:, h * A_HEAD_DIM:(h + 1) * A_HEAD_DIM]
        sl = slice(kv * A_HEAD_DIM, (kv + 1) * A_HEAD_DIM)
        bias = bias_ref[h]
        sp = _bdot_nt(qh, kp_ref[:, sl]) * ATT_SCALE + bias[:, :WINDOW]
        sc = _bdot_nt(qh, kc_ref[:, sl]) * ATT_SCALE + bias[:, WINDOW:]
        sp = jnp.where(valid_prev, sp, NEG_INF)
        sc = jnp.where(valid_cur, sc, NEG_INF)
        sink = sink_ref[0, h]
        mx = jnp.maximum(jnp.maximum(jnp.max(sp, axis=-1, keepdims=True),
                                     jnp.max(sc, axis=-1, keepdims=True)), sink)
        ep = jnp.exp(sp - mx)
        ec = jnp.exp(sc - mx)
        den = (jnp.sum(ep, axis=-1, keepdims=True) + jnp.sum(ec, axis=-1, keepdims=True)
               + jnp.exp(sink - mx))
        o = _bdot(ep / den, vp_ref[:, sl]) + _bdot(ec / den, vc_ref[:, sl])
        o_ref[:, h * A_HEAD_DIM:(h + 1) * A_HEAD_DIM] = o.astype(o_ref.dtype)


def _attn_p_call(qkv, bias, sinks, batch, seq):
    nb = seq // WINDOW
    n = batch * seq
    kcol = A_WIDTH // A_KV_WIDTH
    vcol = kcol + 1
    cur = lambda col: pl.BlockSpec((WINDOW, A_KV_WIDTH), lambda b, i: (b * nb + i, col))
    prev = lambda col: pl.BlockSpec((WINDOW, A_KV_WIDTH),
                                    lambda b, i: (b * nb + jnp.maximum(i - 1, 0), col))
    return pl.pallas_call(
        _attn_p_kernel,
        grid=(batch, nb),
        in_specs=[pl.BlockSpec((WINDOW, A_WIDTH), lambda b, i: (b * nb + i, 0)),
                  prev(kcol), cur(kcol), prev(vcol), cur(vcol),
                  pl.BlockSpec((A_HEADS, WINDOW, 2 * WINDOW), lambda b, i: (0, 0, 0)),
                  pl.BlockSpec(memory_space=pltpu.SMEM)],
        out_specs=pl.BlockSpec((WINDOW, A_WIDTH), lambda b, i: (b * nb + i, 0)),
        out_shape=jax.ShapeDtypeStruct((n, A_WIDTH), BF16),
        compiler_params=_cparams("parallel", "parallel"),
        name="swa_prompt",
    )(qkv, qkv, qkv, qkv, qkv, bias, sinks.reshape(1, A_HEADS))


def _attn_s_kernel(n_new, qkv_ref, ck_ref, cv_ref, bc_ref, bn_ref, sink_ref,
                   o_ref, nk_ref, nv_ref, kk_s, vv_s):
    qkv = qkv_ref[...]
    knew = qkv[:, A_WIDTH:A_WIDTH + A_KV_WIDTH]
    vnew = qkv[:, A_WIDTH + A_KV_WIDTH:A_WIDTH + 2 * A_KV_WIDTH]
    ck = ck_ref[...]
    cv = cv_ref[...]
    rows = A_GROUP * SAMPLE_PAD
    qi = lax.broadcasted_iota(jnp.int32, (rows, WINDOW), 0) % SAMPLE_PAD
    kj = lax.broadcasted_iota(jnp.int32, (rows, WINDOW), 1)
    valid_c = kj > qi
    rcol = lax.broadcasted_iota(jnp.int32, (rows, 1), 0)
    qcol = rcol % SAMPLE_PAD
    for kv in range(A_KV_HEADS):
        heads = range(kv * A_GROUP, (kv + 1) * A_GROUP)
        sl = slice(kv * A_HEAD_DIM, (kv + 1) * A_HEAD_DIM)
        qs = jnp.concatenate([qkv[:, h * A_HEAD_DIM:(h + 1) * A_HEAD_DIM] for h in heads], axis=0)
        bias_c = jnp.concatenate([bc_ref[h] for h in heads], axis=0)
        bias_n = jnp.concatenate([bn_ref[h] for h in heads], axis=0)
        sink = jnp.zeros((rows, 1), F32)
        for g, h in enumerate(heads):
            sink = jnp.where(rcol // SAMPLE_PAD == g, sink_ref[0, h], sink)
        s_c = _bdot_nt(qs, ck[:, sl]) * ATT_SCALE + bias_c
        s_c = jnp.where(valid_c, s_c, NEG_INF)
        s_n = []
        for j in range(n_new):
            sj = jnp.sum(qs * knew[j:j + 1, sl], axis=-1, keepdims=True) * ATT_SCALE + bias_n[:, j:j + 1]
            s_n.append(jnp.where(qcol >= j, sj, NEG_INF))
        mx = jnp.maximum(jnp.max(s_c, axis=-1, keepdims=True), sink)
        for sj in s_n:
            mx = jnp.maximum(mx, sj)
        e_c = jnp.exp(s_c - mx)
        den = jnp.sum(e_c, axis=-1, keepdims=True) + jnp.exp(sink - mx)
        o = _bdot(e_c, cv[:, sl])
        for j, sj in enumerate(s_n):
            ej = jnp.exp(sj - mx)
            den = den + ej
            o = o + ej * vnew[j:j + 1, sl]
        o = o / den
        for g, h in enumerate(heads):
            o_ref[:, h * A_HEAD_DIM:(h + 1) * A_HEAD_DIM] = o[g * SAMPLE_PAD:(g + 1) * SAMPLE_PAD, :]
    kk_s[0:WINDOW, :] = ck
    kk_s[WINDOW:WINDOW + SAMPLE_PAD, :] = knew
    vv_s[0:WINDOW, :] = cv
    vv_s[WINDOW:WINDOW + SAMPLE_PAD, :] = vnew
    nk_ref[...] = kk_s[n_new:n_new + WINDOW, :]
    nv_ref[...] = vv_s[n_new:n_new + WINDOW, :]


def _attn_s_call(qkv, ck, cv, bias_c, bias_n, sinks, n_new):
    nreq = ck.shape[0]
    wq = qkv.shape[1]
    full3 = lambda shape: pl.BlockSpec(shape, lambda b: (0, 0, 0))
    cache = pl.BlockSpec((None, WINDOW, A_KV_WIDTH), lambda b: (b, 0, 0))
    return pl.pallas_call(
        functools.partial(_attn_s_kernel, n_new),
        grid=(nreq,),
        in_specs=[pl.BlockSpec((SAMPLE_PAD, wq), lambda b: (b, 0)), cache, cache,
                  full3(bias_c.shape), full3(bias_n.shape),
                  pl.BlockSpec(memory_space=pltpu.SMEM)],
        out_specs=[pl.BlockSpec((SAMPLE_PAD, A_WIDTH), lambda b: (b, 0)), cache, cache],
        out_shape=[jax.ShapeDtypeStruct((nreq * SAMPLE_PAD, A_WIDTH), F32),
                   jax.ShapeDtypeStruct(ck.shape, F32), jax.ShapeDtypeStruct(cv.shape, F32)],
        scratch_shapes=[pltpu.VMEM((WINDOW + SAMPLE_PAD, A_KV_WIDTH), F32),
                        pltpu.VMEM((WINDOW + SAMPLE_PAD, A_KV_WIDTH), F32)],
        compiler_params=_cparams("parallel"),
        name="swa_sample",
    )(qkv, ck, cv, bias_c, bias_n, sinks.reshape(1, A_HEADS))


def _mlstm_kernel(chunk, t_valid, qk_ref, v_ref, og_ref, gc_ref, cw_ref, cb_ref, mn_ref,
                  conv0_ref, c0_ref, n0_ref, m0_ref,
                  out_ref, cout_ref, nout_ref, mout_ref,
                  xp_s, c_s, n_s, m_s):
    step = pl.program_id(1)
    halo = SUBLANES

    @pl.when(step == 0)
    def _():
        xp_s[0:halo, :] = conv0_ref[...]
        c_s[...] = c0_ref[...]
        n_s[...] = n0_ref[...]
        m_s[...] = m0_ref[...]

    xp_s[halo:halo + chunk, :] = qk_ref[...]
    cw = cw_ref[...]
    y = cb_ref[...]
    for i in range(CONV_W):
        off = halo - (CONV_W - 1) + i
        y = y + xp_s[off:off + chunk, :] * cw[i:i + 1, :]
    xp_s[0:halo, :] = xp_s[chunk:chunk + halo, :]
    y = y * _sigmoid(y)
    q_all = y[:, :M_WIDTH]
    k_all = y[:, M_WIDTH:] * (M_HEAD_DIM ** -0.5)

    g = gc_ref[...]
    if t_valid < chunk:
        row = lax.broadcasted_iota(jnp.int32, g.shape, 0)
        lane = lax.broadcasted_iota(jnp.int32, g.shape, 1)
        g = jnp.where(row < t_valid, g, jnp.where(lane < M_HEADS, NEG_INF, 0.0))
    tr = lax.broadcasted_iota(jnp.int32, (chunk, chunk), 0)
    tc = lax.broadcasted_iota(jnp.int32, (chunk, chunk), 1)
    causal = tr >= tc
    tri = causal.astype(F32)
    bcol = jnp.dot(tri, g, precision=HIGHEST, preferred_element_type=F32)
    er = lax.broadcasted_iota(jnp.int32, (SUBLANES, LANES), 0)
    ec = lax.broadcasted_iota(jnp.int32, (SUBLANES, LANES), 1)
    eye = (er == ec).astype(F32)
    g_rows = lax.dot_general(eye, g, NT_DIMS, precision=HIGHEST, preferred_element_type=F32)
    b_rows = lax.dot_general(eye, bcol, NT_DIMS, precision=HIGHEST, preferred_element_type=F32)

    for h in range(M_HEADS):
        hs = slice(h * M_HEAD_DIM, (h + 1) * M_HEAD_DIM)
        b_c = bcol[:, M_HEADS + h:M_HEADS + h + 1]
        ig_c = g[:, h:h + 1]
        b_r = b_rows[M_HEADS + h:M_HEADS + h + 1, :]
        ig_r = g_rows[h:h + 1, :]
        m_prev = m_s[h:h + 1, 0:1]
        logw = jnp.where(causal, b_c - b_r + ig_r, -jnp.inf)
        inter = b_c + m_prev
        m_t = jnp.maximum(inter, jnp.max(logw, axis=-1, keepdims=True))
        w = jnp.exp(logw - m_t)
        a = jnp.exp(inter - m_t)
        q = q_all[:, hs]
        k = k_all[:, hs]
        v = v_ref[:, hs]
        cmat = c_s[h]
        nvec = n_s[h:h + 1, :]
        wqk = w * _bdot_nt(q, k)
        num = _bdot(wqk, v) + a * _bdot_nt(q, cmat)
        den = jnp.sum(wqk, axis=-1, keepdims=True) + a * jnp.sum(q * nvec, axis=-1, keepdims=True)
        hh = num / jnp.maximum(jnp.abs(den), jnp.exp(-m_t))
        m_new = m_t[chunk - 1:chunk, :]
        b_last = b_c[chunk - 1:chunk, :]
        wl = jnp.exp(b_last - b_c + ig_c - m_new)
        al = jnp.exp(b_last + m_prev - m_new)
        c_s[h] = al * cmat + lax.dot_general((v * wl).astype(BF16), k.astype(BF16), TN_DIMS,
                                             preferred_element_type=F32)
        n_s[h:h + 1, :] = al * nvec + jnp.sum(wl * k, axis=0, keepdims=True)
        m_s[h:h + 1, :] = jnp.broadcast_to(m_new, (1, LANES))
        hn = hh * lax.rsqrt(jnp.mean(hh * hh, axis=-1, keepdims=True) + EPS) * mn_ref[:, hs]
        out_ref[:, hs] = (_sigmoid(og_ref[:, hs]) * hn).astype(out_ref.dtype)

    @pl.when(step == pl.num_programs(1) - 1)
    def _():
        cout_ref[...] = c_s[...]
        nout_ref[...] = n_s[...]
        mout_ref[...] = m_s[...]


def _mlstm_call(qk, v, og, gc, conv_w, conv_b, m_norm, conv0, c0, n0, m0, batch, seq, chunk, t_valid):
    nc = seq // chunk
    n = batch * seq
    row = lambda w: pl.BlockSpec((chunk, w), lambda b, c: (b * nc + c, 0))
    full2 = lambda shape: pl.BlockSpec(shape, lambda b, c: (0, 0))
    per_b = lambda shape: pl.BlockSpec((None,) + shape, lambda b, c: (b,) + (0,) * len(shape))
    dh = M_HEAD_DIM
    return pl.pallas_call(
        functools.partial(_mlstm_kernel, chunk, t_valid),
        grid=(batch, nc),
        in_specs=[row(2 * M_WIDTH), row(M_WIDTH), row(M_WIDTH), row(LANES),
                  full2((CONV_W, 2 * M_WIDTH)), full2((1, 2 * M_WIDTH)), full2((1, M_WIDTH)),
                  per_b((SUBLANES, 2 * M_WIDTH)), per_b((M_HEADS, dh, dh)), per_b((M_HEADS, dh)),
                  per_b((SUBLANES, LANES))],
        out_specs=[row(M_WIDTH), per_b((M_HEADS, dh, dh)), per_b((M_HEADS, dh)), per_b((SUBLANES, LANES))],
        out_shape=[jax.ShapeDtypeStruct((n, M_WIDTH), BF16 if chunk % (2 * SUBLANES) == 0 else F32),
                   jax.ShapeDtypeStruct((batch, M_HEADS, dh, dh), F32),
                   jax.ShapeDtypeStruct((batch, M_HEADS, dh), F32),
                   jax.ShapeDtypeStruct((batch, SUBLANES, LANES), F32)],
        scratch_shapes=[pltpu.VMEM((SUBLANES + chunk, 2 * M_WIDTH), F32),
                        pltpu.VMEM((M_HEADS, dh, dh), F32),
                        pltpu.VMEM((M_HEADS, dh), F32),
                        pltpu.VMEM((SUBLANES, LANES), F32)],
        compiler_params=_cparams("parallel", "arbitrary"),
        name="mlstm",
    )(qk, v, og, gc, conv_w, conv_b, m_norm, conv0, c0, n0, m0)


def _pack_words(x):
    bits = pltpu.bitcast(x.astype(BF16).astype(F32), jnp.uint32)
    half = x.shape[1] // 2
    lo = lax.shift_right_logical(bits[:, :half], jnp.uint32(16))
    hi = bits[:, half:] & jnp.uint32(0xFFFF0000)
    return pltpu.bitcast(lo | hi, jnp.int32)


def _out_kernel(att_ref, mo_ref, x_ref, g1_ref, sc_ref, sh_ref, gn_ref, wa_ref, wm_ref, wq_ref,
                xo_ref, h2_ref, qp_ref):
    mix = (jnp.dot(att_ref[...].astype(BF16), wa_ref[...], preferred_element_type=F32)
           + jnp.dot(mo_ref[...].astype(BF16), wm_ref[...], preferred_element_type=F32))
    x = x_ref[...] + g1_ref[...] * mix
    xo_ref[...] = x
    y = x * lax.rsqrt(jnp.mean(x * x, axis=-1, keepdims=True) + EPS) * gn_ref[...]
    h2 = y * (1.0 + sc_ref[...]) + sh_ref[...]
    qp_ref[...] = jnp.dot(h2.astype(BF16), wq_ref[...], preferred_element_type=F32).astype(qp_ref.dtype)
    h2_ref[...] = _pack_words(h2)


def _out_call(att, mo, x, g1, sc, sh, gnorm, wa, wm, wq, per_token, tokens_per_req):
    n = x.shape[0]
    tm = min(256, n if per_token else tokens_per_req)
    mod = _mod_spec(per_token, tm, tokens_per_req)
    full = lambda shape: pl.BlockSpec(shape, lambda i: (0,) * len(shape))
    row = lambda w: pl.BlockSpec((tm, w), lambda i: (i, 0))
    nq = wq.shape[1]
    return pl.pallas_call(
        _out_kernel,
        grid=(n // tm,),
        in_specs=[row(A_WIDTH), row(M_WIDTH), row(D_MODEL), mod, mod, mod, full((1, D_MODEL)),
                  full(wa.shape), full(wm.shape), full(wq.shape)],
        out_specs=[row(D_MODEL), row(D_MODEL // 2), row(nq)],
        out_shape=[jax.ShapeDtypeStruct((n, D_MODEL), F32), jax.ShapeDtypeStruct((n, D_MODEL // 2), jnp.int32),
                   jax.ShapeDtypeStruct((n, nq), BF16)],
        compiler_params=_cparams("parallel"),
        name="out_proj",
    )(att, mo, x, g1, sc, sh, gnorm, wa, wm, wq)


def _pk_cells():
    return [(a, b) for a in range(P_TOPK) for b in range(P_TOPK) if (a + 1) * (b + 1) <= P_TOPK]


PK_CELL_ROWS = 64


def _pk_expand_mats():
    cells = _pk_cells()
    e0 = np.zeros((PK_CELL_ROWS, LANES), np.float32)
    e1 = np.zeros((PK_CELL_ROWS, LANES), np.float32)
    for j, (a, b) in enumerate(cells):
        e0[j, a] = 1.0
        e1[j, b] = 1.0
    return e0, e1, len(cells)


def _top_rows(s, rowf, rounds):
    n_rows = s.shape[0]
    vals, idxs = [], []
    for _ in range(rounds):
        m = jnp.max(s, axis=0, keepdims=True)
        i = jnp.min(jnp.where(s == m, rowf, float(n_rows)), axis=0, keepdims=True)
        vals.append(m)
        idxs.append(i)
        s = jnp.where(rowf == i, -jnp.inf, s)
    return jnp.concatenate(vals, axis=0), jnp.concatenate(idxs, axis=0)


def _select_kernel(n_cells, qp_ref, keys_ref, e0_ref, e1_ref, idx_ref, gw_ref, idx_s, gw_s):
    tm = qp_ref.shape[0]
    keyf = lax.broadcasted_iota(jnp.int32, (N_KEYS, tm), 0).astype(F32)
    cellf = lax.broadcasted_iota(jnp.int32, (PK_CELL_ROWS, tm), 0).astype(F32)
    e0 = e0_ref[...]
    e1 = e1_ref[...]
    pad = jnp.zeros((LANES - P_TOPK, tm), F32)

    def head(h, carry):
        h = jnp.asarray(h, jnp.int32)
        sub = []
        for c in range(2):
            col = pl.multiple_of((h * 2 + c) * P_HALF, P_HALF)
            s = _bdot_nt(keys_ref[h, c], qp_ref[:, pl.ds(col, P_HALF)])
            sub.append(_top_rows(s, keyf, P_TOPK))
        (v0, i0), (v1, i1) = sub
        expand = lambda e, x: jnp.dot(e, jnp.concatenate([x, pad], axis=0), precision=HIGHEST,
                                      preferred_element_type=F32)
        cand = expand(e0, v0) + expand(e1, v1)
        cidx = expand(e0, i0 * float(N_KEYS)) + expand(e1, i1)
        cand = jnp.where(cellf < n_cells, cand, -jnp.inf)
        best, eidx = [], []
        for _ in range(P_TOPK):
            m = jnp.max(cand, axis=0, keepdims=True)
            j = jnp.min(jnp.where(cand == m, cellf, float(PK_CELL_ROWS)), axis=0, keepdims=True)
            hit = cellf == j
            eidx.append(jnp.max(jnp.where(hit, cidx, -1.0), axis=0, keepdims=True))
            best.append(m)
            cand = jnp.where(hit, -jnp.inf, cand)
        best = jnp.concatenate(best, axis=0)
        e = jnp.exp(best - best[0:1, :])
        row0 = pl.multiple_of(h * P_TOPK, P_TOPK)
        gw_s[pl.ds(row0, P_TOPK), :] = e / jnp.sum(e, axis=0, keepdims=True)
        idx_s[pl.ds(row0, P_TOPK), :] = jnp.concatenate(eidx, axis=0)
        return carry

    lax.fori_loop(0, P_HEADS, head, 0)
    idx_ref[...] = idx_s[...].T.astype(jnp.int32)
    gw_ref[...] = gw_s[...].T


def _select_call(qp, keys_bf16):
    n = qp.shape[0]
    tm = min(LANES, n)
    e0, e1, n_cells = _pk_expand_mats()
    full = lambda shape: pl.BlockSpec(shape, lambda i: (0,) * len(shape))
    return pl.pallas_call(
        functools.partial(_select_kernel, n_cells),
        grid=(n // tm,),
        in_specs=[pl.BlockSpec((tm, qp.shape[1]), lambda i: (i, 0)), full(keys_bf16.shape),
                  full((PK_CELL_ROWS, LANES)), full((PK_CELL_ROWS, LANES))],
        out_specs=[pl.BlockSpec((tm, LANES), lambda i: (i, 0)), pl.BlockSpec((tm, LANES), lambda i: (i, 0))],
        out_shape=[jax.ShapeDtypeStruct((n, LANES), jnp.int32), jax.ShapeDtypeStruct((n, LANES), F32)],
        scratch_shapes=[pltpu.VMEM((P_HEADS * P_TOPK, tm), F32), pltpu.VMEM((P_HEADS * P_TOPK, tm), F32)],
        compiler_params=_cparams("parallel"),
        name="peer_select",
    )(qp, keys_bf16, jnp.asarray(e0), jnp.asarray(e1))


N_SEL = P_HEADS * P_TOPK

SC_CORES = 2
SC_SUBCORES = 16
SC_LANES = 16
SC_TOK_BLOCK = 8
SC_ROWS = 32
SC_ROW_BLOCK = 16
SC_ACC_CHAINS = 4
SC_ACC_ROWS = 16
SC_NBUF = 4
N_WCOL = D_MODEL // (2 * SC_LANES)


def _pack_kernel(t_ref, o_ref):
    o_ref[...] = _pack_words(t_ref[...])


def _pack_table(tables, layer):
    _, e, dcol = tables.shape
    tm = min(1024, e)
    return pl.pallas_call(
        _pack_kernel,
        grid=(e // tm,),
        in_specs=[pl.BlockSpec((None, tm, dcol), lambda i: (layer, i, 0))],
        out_specs=pl.BlockSpec((tm, dcol // 2), lambda i: (i, 0)),
        out_shape=jax.ShapeDtypeStruct((e, dcol // 2), jnp.int32),
        compiler_params=_cparams("parallel"),
        name="pack_table",
    )(tables)


def _sc_gelu(x):
    z = math.sqrt(2.0 / math.pi) * (x + 0.044715 * (x * x * x))
    t = 1.0 - 2.0 / (jnp.exp(2.0 * z) + 1.0)
    return x * (0.5 * (1.0 + t))


def _sc_expert_body(tokens_per_worker, idx_hbm, gw_hbm, h2_hbm, u_hbm, v_hbm, y_hbm,
                    idx_v, gw_v, x_v, o_v, buf, coef_v, tr_v, sem, in_sem, out_sem):
    wid = lax.axis_index("s") * SC_CORES + lax.axis_index("c")
    base = wid * tokens_per_worker
    lane = lax.iota(jnp.int32, SC_LANES)
    n_gather = N_SEL // SC_ROWS
    zero = jnp.zeros((SC_LANES,), F32)

    n_steps = 2 * n_gather
    assert n_steps % SC_NBUF == 0

    def gather(p, tt, i):
        table = u_hbm if i < n_gather else v_hbm
        j = i % n_gather
        slot = i % SC_NBUF
        return pltpu.make_async_copy(table.at[idx_v.at[p, tt, pl.ds(j * SC_ROWS, SC_ROWS)]], buf.at[slot],
                                     sem.at[slot])

    def unpack(w):
        lo = lax.bitcast_convert_type(lax.shift_left(w, jnp.full((SC_LANES,), 16, jnp.int32)), F32)
        hi = lax.bitcast_convert_type(w & jnp.full((SC_LANES,), -65536, jnp.int32), F32)
        return lo, hi

    def packed(w):
        return plsc.bitcast(w, BF16)

    def unpack_sum(s):
        return unpack(plsc.bitcast(s, jnp.int32))

    def act_chunk(p, tt, j, slot):
        @pl.loop(0, SC_ROWS // SC_LANES)
        def _(half):
            for rb in range(SC_LANES // SC_ROW_BLOCK):
                r0 = half * SC_LANES + rb * SC_ROW_BLOCK

                def col(c, accs):
                    w0 = pl.multiple_of(c * (2 * SC_LANES), 2 * SC_LANES)
                    xa = packed(x_v[p, tt, pl.ds(w0, SC_LANES)])
                    xb = packed(x_v[p, tt, pl.ds(w0 + SC_LANES, SC_LANES)])
                    out = []
                    for r, a in enumerate(accs):
                        ua = packed(buf[slot, r0 + r, pl.ds(w0, SC_LANES)])
                        ub = packed(buf[slot, r0 + r, pl.ds(w0 + SC_LANES, SC_LANES)])
                        lo, hi = unpack_sum(ua * xa + ub * xb)
                        out.append(a + lo + hi)
                    return tuple(out)

                accs = lax.fori_loop(0, N_WCOL // 2, col, (zero,) * SC_ROW_BLOCK)
                for r in range(SC_ROW_BLOCK):
                    tr_v[pl.ds((rb * SC_ROW_BLOCK + r) * SC_LANES, SC_LANES)] = accs[r]
            tot = zero
            for jj in range(SC_LANES):
                tot = tot + plsc.load_gather(tr_v, [lane * SC_LANES + jj])
            k0 = pl.multiple_of(j * SC_ROWS + half * SC_LANES, SC_LANES)
            coef_v[pl.ds(k0, SC_LANES)] = gw_v[p, tt, pl.ds(k0, SC_LANES)] * _sc_gelu(tot)

    def acc_chunk(p, tt, j, slot, first):
        def tree_sum(parts):
            while len(parts) > 1:
                parts = [parts[i] + parts[i + 1] for i in range(0, len(parts), 2)]
            return parts[0]

        for rb in range(SC_ROWS // SC_ACC_ROWS):
            rows = list(range(rb * SC_ACC_ROWS, (rb + 1) * SC_ACC_ROWS))
            splat = {}
            for r in rows:
                c16 = plsc.load_gather(coef_v, [jnp.full((SC_LANES,), j * SC_ROWS + r, jnp.int32)])
                splat[r] = plsc.pack(c16, c16, format=plsc.PackFormat.INTERLEAVED)
            fresh = first and rb == 0

            @plsc.parallel_loop(0, N_WCOL)
            def _(c):
                woff = pl.multiple_of(c * SC_LANES, SC_LANES)
                hoff = pl.multiple_of(c * SC_LANES + D_MODEL // 2, SC_LANES)
                pa, pb = [], []
                for n in range(0, SC_ACC_ROWS, 2):
                    r0, r1 = rows[n], rows[n + 1]
                    s = (splat[r0] * packed(buf[slot, r0, pl.ds(woff, SC_LANES)])
                         + splat[r1] * packed(buf[slot, r1, pl.ds(woff, SC_LANES)]))
                    lo, hi = unpack_sum(s)
                    if n // 2 < SC_ACC_CHAINS:
                        pa.append(lo)
                        pb.append(hi)
                    else:
                        pa[(n // 2) % SC_ACC_CHAINS] = pa[(n // 2) % SC_ACC_CHAINS] + lo
                        pb[(n // 2) % SC_ACC_CHAINS] = pb[(n // 2) % SC_ACC_CHAINS] + hi
                sa, sb = tree_sum(pa), tree_sum(pb)
                if not fresh:
                    sa = sa + o_v[p, tt, pl.ds(woff, SC_LANES)]
                    sb = sb + o_v[p, tt, pl.ds(hoff, SC_LANES)]
                o_v[p, tt, pl.ds(woff, SC_LANES)] = sa
                o_v[p, tt, pl.ds(hoff, SC_LANES)] = sb

    n_blocks = tokens_per_worker // SC_TOK_BLOCK

    def block_start(blk):
        return pl.multiple_of(base + blk * SC_TOK_BLOCK, SC_TOK_BLOCK)

    def in_copies(blk, p):
        rows = pl.ds(block_start(blk), SC_TOK_BLOCK)
        return [pltpu.make_async_copy(idx_hbm.at[rows], idx_v.at[p], in_sem.at[p]),
                pltpu.make_async_copy(gw_hbm.at[rows], gw_v.at[p], in_sem.at[p]),
                pltpu.make_async_copy(h2_hbm.at[rows], x_v.at[p], in_sem.at[p])]

    def out_copy(blk, p):
        return pltpu.make_async_copy(o_v.at[p], y_hbm.at[pl.ds(block_start(blk), SC_TOK_BLOCK)], out_sem.at[p])

    ahead = SC_NBUF - 1
    for cp in in_copies(0, 0):
        cp.start()
    for cp in in_copies(0, 0):
        cp.wait()
    for cp in in_copies(1, 1):
        cp.start()
    for i in range(ahead):
        gather(0, 0, i).start()

    @pl.loop(0, n_blocks)
    def _(blk):
        p = lax.rem(blk, 2)

        @pl.when(blk >= 2)
        def _():
            out_copy(blk - 2, p).wait()

        @pl.loop(0, SC_TOK_BLOCK)
        def _(tt):
            for i in range(n_steps):
                if i + ahead < n_steps:
                    gather(p, tt, i + ahead).start()
                else:
                    nxt = i + ahead - n_steps

                    @pl.when(tt + 1 < SC_TOK_BLOCK)
                    def _():
                        gather(p, tt + 1, nxt).start()

                    @pl.when(jnp.logical_and(tt + 1 == SC_TOK_BLOCK, blk + 1 < n_blocks))
                    def _():
                        if nxt == 0:
                            for cp in in_copies(blk + 1, 1 - p):
                                cp.wait()
                        gather(1 - p, 0, nxt).start()
                gather(p, tt, i).wait()
                if i < n_gather:
                    act_chunk(p, tt, i, i % SC_NBUF)
                else:
                    acc_chunk(p, tt, i - n_gather, i % SC_NBUF, i == n_gather)

        out_copy(blk, p).start()

        @pl.when(blk + 2 < n_blocks)
        def _():
            for cp in in_copies(blk + 2, p):
                cp.start()

    for blk in (n_blocks - 2, n_blocks - 1):
        out_copy(blk, blk % 2).wait()


def _sc_expert_call(idx, gw, xw, u, v):
    n = idx.shape[0]
    workers = SC_CORES * SC_SUBCORES
    assert n % (workers * SC_TOK_BLOCK) == 0 and n // (workers * SC_TOK_BLOCK) >= 2
    mesh = plsc.VectorSubcoreMesh(core_axis_name="c", subcore_axis_name="s")
    return pl.kernel(
        functools.partial(_sc_expert_body, n // workers),
        out_type=jax.ShapeDtypeStruct((n, D_MODEL), F32),
        mesh=mesh,
        scratch_types=[pltpu.VMEM((2, SC_TOK_BLOCK, N_SEL), jnp.int32),
                       pltpu.VMEM((2, SC_TOK_BLOCK, N_SEL), F32),
                       pltpu.VMEM((2, SC_TOK_BLOCK, D_MODEL // 2), jnp.int32),
                       pltpu.VMEM((2, SC_TOK_BLOCK, D_MODEL), F32),
                       pltpu.VMEM((SC_NBUF, SC_ROWS, D_MODEL // 2), jnp.int32),
                       pltpu.VMEM((N_SEL,), F32),
                       pltpu.VMEM((SC_LANES * SC_LANES,), F32),
                       pltpu.SemaphoreType.DMA((SC_NBUF,)), pltpu.SemaphoreType.DMA((2,)),
                       pltpu.SemaphoreType.DMA((2,))],
        compiler_params=pltpu.CompilerParams(needs_layout_passes=False),
        name="peer_experts_sc",
    )(idx, gw, xw, u, v)


def _resid_kernel(x_ref, y_ref, g_ref, o_ref):
    o_ref[...] = x_ref[...] + g_ref[...] * y_ref[...]


def _resid_call(x, y, g2, per_token, tokens_per_req):
    n = x.shape[0]
    tm = min(512, n if per_token else tokens_per_req)
    row = pl.BlockSpec((tm, D_MODEL), lambda i: (i, 0))
    return pl.pallas_call(
        _resid_kernel,
        grid=(n // tm,),
        in_specs=[row, row, _mod_spec(per_token, tm, tokens_per_req)],
        out_specs=row,
        out_shape=jax.ShapeDtypeStruct((n, D_MODEL), F32),
        compiler_params=_cparams("parallel"),
        name="peer_residual",
    )(x, y, g2)


def _final_kernel(x_ref, g_ref, o_ref):
    x = x_ref[...]
    o_ref[...] = x * lax.rsqrt(jnp.mean(x * x, axis=-1, keepdims=True) + EPS) * g_ref[...]


def _final_call(x, g):
    n = x.shape[0]
    tm = min(512, n)
    return pl.pallas_call(
        _final_kernel,
        grid=(n // tm,),
        in_specs=[pl.BlockSpec((tm, D_MODEL), lambda i: (i, 0)), pl.BlockSpec((1, D_MODEL), lambda i: (0, 0))],
        out_specs=pl.BlockSpec((tm, D_MODEL), lambda i: (i, 0)),
        out_shape=jax.ShapeDtypeStruct((n, D_MODEL), F32),
        compiler_params=_cparams("parallel"),
        name="final_norm",
    )(x, g)


def _split_w_in(w_in_l, gate_b_l):
    cuts = np.cumsum([A_WIDTH + 2 * A_KV_WIDTH, 2 * M_WIDTH, M_WIDTH, M_WIDTH]).tolist()
    wa = w_in_l[:, :cuts[0]].astype(BF16)
    wqk = w_in_l[:, cuts[0]:cuts[1]].astype(BF16)
    wv = w_in_l[:, cuts[1]:cuts[2]].astype(BF16)
    wo = w_in_l[:, cuts[2]:cuts[3]].astype(BF16)
    ng = 2 * M_HEADS
    wg = jnp.pad(w_in_l[:, cuts[3]:], ((0, 0), (0, LANES - ng))).astype(BF16)
    gb = jnp.pad(gate_b_l.astype(F32), (0, LANES - ng)).reshape(1, LANES)
    return wa, wqk, wv, wo, wg, gb


def _layer(x, mods, per_token, batch, seq, t_valid, lw, bias_p, bias_c, bias_n, kv_cache, conv0, state,
           after=None):
    (norm_mix, norm_ffn, w_in, conv_w, conv_b, gate_b, sinks, m_norm, w_out, peer_query, peer_keys,
     peer_u, peer_v) = lw
    if after is not None:
        x, _ = lax.optimization_barrier((x, after))
    sh1, sc1, g1, sh2, sc2, g2 = mods
    wa, wqk, wv, wo, wg, gb = _split_w_in(w_in, gate_b)
    qkv, qkm, vm, om, gc = _in_call(x, sc1, sh1, norm_mix.reshape(1, -1), wa, wqk, wv, wo, wg, gb,
                                    per_token, seq)
    if kv_cache is None:
        att = _attn_p_call(qkv, bias_p, sinks, batch, seq)
        kv3 = qkv.reshape(batch, seq, -1)
        new_k = kv3[:, seq - WINDOW:, A_WIDTH:A_WIDTH + A_KV_WIDTH]
        new_v = kv3[:, seq - WINDOW:, A_WIDTH + A_KV_WIDTH:]
        chunk = M_CHUNK
    else:
        att, new_k, new_v = _attn_s_call(qkv, kv_cache[0], kv_cache[1], bias_c, bias_n, sinks, t_valid)
        chunk = seq
    c0, n0, m0 = state
    mo, c_new, n_new, m_new = _mlstm_call(qkm, vm, om, gc, conv_w, conv_b.reshape(1, -1),
                                          m_norm.reshape(1, -1), conv0, c0, n0, m0,
                                          batch, seq, chunk, min(t_valid, chunk))
    new_conv = qkm.reshape(batch, seq, -1)[:, t_valid - (CONV_W - 1):t_valid]
    pad_rows = seq - t_valid
    if pad_rows:
        keep = lambda a: a.reshape(batch, seq, -1)[:, :t_valid].reshape(batch * t_valid, -1)
        att, mo, x, g1, sc2, sh2, g2 = (keep(a) for a in (att, mo, x, g1, sc2, sh2, g2))
    x_mid, xw, qp = _out_call(att, mo, x, g1, sc2, sh2, norm_ffn.reshape(1, -1),
                              w_out[:A_WIDTH].astype(BF16), w_out[A_WIDTH:].astype(BF16),
                              peer_query.astype(BF16), per_token, t_valid)
    idx, gw = _select_call(qp, peer_keys.astype(BF16))
    y = _sc_expert_call(idx, gw, xw, peer_u, peer_v)
    x_new = _resid_call(x_mid, y, g2, per_token, t_valid)
    if pad_rows:
        x_new = jnp.pad(x_new.reshape(batch, t_valid, -1), ((0, 0), (0, pad_rows), (0, 0))).reshape(batch * seq, -1)
    new_k = new_k.reshape(batch, WINDOW, A_KV_HEADS, A_HEAD_DIM)
    new_v = new_v.reshape(batch, WINDOW, A_KV_HEADS, A_HEAD_DIM)
    return x_new, (new_k, new_v, new_conv, c_new, n_new, m_new[:, :M_HEADS, 0]), idx


def _prompt_group_sizes(n_req):
    if n_req < PROMPT_GROUPS:
        return [1] * n_req
    mid, n_mid = n_req - 2, PROMPT_GROUPS - 2
    weights = [i + 2 for i in range(n_mid)]
    sizes = [max(1, mid * w // sum(weights)) for w in weights]
    for i in range(mid - sum(sizes)):
        sizes[n_mid - 1 - i % n_mid] += 1
    return [1] + sizes + [1]


def kernel(x_prompt, x_sample, c_prompt, c_sample, cache_k, cache_v, state_conv, state_C, state_n, state_m, rel_bias, w_ada, b_ada, norm_mix, norm_ffn, w_in, conv_w, conv_b, gate_b, attn_sinks, m_norm, w_out, peer_query, peer_keys, peer_u, peer_v, norm_final):
    depth = w_ada.shape[0]
    bp, tp, d = x_prompt.shape
    bs, ts, _ = x_sample.shape
    assert tp % WINDOW == 0 and tp % M_CHUNK == 0 and ts <= SAMPLE_PAD and ts >= CONV_W - 1

    mod_all = _ada_call(jnp.concatenate([c_prompt, c_sample], axis=0), w_ada, b_ada)

    qi = np.arange(WINDOW)[:, None]
    bias_p = _bias_call(rel_bias, qi + WINDOW - np.arange(2 * WINDOW)[None, :])
    qs = np.arange(SAMPLE_PAD)[:, None]
    bias_c = _bias_call(rel_bias, qs + WINDOW - np.arange(WINDOW)[None, :])
    bias_n = _bias_call(rel_bias, qs - np.arange(SAMPLE_PAD)[None, :])

    sizes = _prompt_group_sizes(bp)
    starts = np.cumsum([0] + sizes).tolist()
    xg = [x_prompt[starts[g]:starts[g + 1]].reshape(sizes[g] * tp, d) for g in range(len(sizes))]
    xs = jnp.pad(x_sample, ((0, 0), (0, SAMPLE_PAD - ts), (0, 0))).reshape(bs * SAMPLE_PAD, d)
    halo_pad = ((0, 0), (SUBLANES - (CONV_W - 1), 0), (0, 0))

    st_p, st_s = [], []
    for l in range(depth):
        lw = (norm_mix[l], norm_ffn[l], w_in[l], conv_w[l], conv_b[l], gate_b[l], attn_sinks[l], m_norm[l],
              w_out[l], peer_query[l], peer_keys[l], _pack_table(peer_u, l), _pack_table(peer_v, l))
        mod_s = [jnp.repeat(m, SAMPLE_PAD, axis=0) for m in jnp.split(mod_all[l, bp:], 6, axis=-1)]
        sp_groups = []
        for g, bg in enumerate(sizes):
            mod_g = [m.reshape(bg, 1, d) for m in jnp.split(mod_all[l, starts[g]:starts[g + 1]], 6, axis=-1)]
            zero_state = (jnp.zeros((bg, M_HEADS, M_HEAD_DIM, M_HEAD_DIM), F32),
                          jnp.zeros((bg, M_HEADS, M_HEAD_DIM), F32),
                          jnp.zeros((bg, SUBLANES, LANES), F32))
            zero_conv = jnp.zeros((bg, SUBLANES, 2 * M_WIDTH), F32)
            xg[g], sp, last_idx = _layer(xg[g], mod_g, False, bg, tp, tp, lw, bias_p, None, None, None,
                                         zero_conv, zero_state)
            sp_groups.append(sp)
        st_p.append([jnp.concatenate([sp[i] for sp in sp_groups], axis=0) for i in range(6)])
        state_s = (state_C[l].astype(F32), state_n[l].astype(F32),
                   jnp.broadcast_to(jnp.pad(state_m[l].astype(F32), ((0, 0), (0, SUBLANES - M_HEADS)))[:, :, None],
                                    (bs, SUBLANES, LANES)))
        kv_cache = (cache_k[l].reshape(bs, WINDOW, A_KV_WIDTH), cache_v[l].reshape(bs, WINDOW, A_KV_WIDTH))
        xs, ss, _ = _layer(xs, mod_s, True, bs, SAMPLE_PAD, ts, lw, None, bias_c, bias_n, kv_cache,
                           jnp.pad(state_conv[l].astype(F32), halo_pad), state_s, after=last_idx)
        st_s.append(ss)

    gfin = norm_final.reshape(1, d)
    y_prompt = jnp.concatenate([_final_call(x, gfin).reshape(bg, tp, d) for x, bg in zip(xg, sizes)], axis=0)
    y_sample = _final_call(xs, gfin).reshape(bs, SAMPLE_PAD, d)[:, :ts]
    outs_p = [jnp.stack([s[i] for s in st_p]) for i in range(6)]
    outs_s = [jnp.stack([s[i] for s in st_s]) for i in range(6)]
    return (y_prompt, y_sample, *outs_p, *outs_s)
```

```python
import functools
import math

import numpy as np
import jax
import jax.numpy as jnp
from jax import lax
from jax.experimental import pallas as pl
from jax.experimental.pallas import tpu as pltpu
from jax.experimental.pallas import tpu_sc as plsc

F32 = jnp.float32
BF16 = jnp.bfloat16
HIGHEST = lax.Precision.HIGHEST

D_MODEL = 1024
A_HEADS = 8
A_KV_HEADS = 2
A_GROUP = A_HEADS // A_KV_HEADS
A_HEAD_DIM = 64
A_WIDTH = A_HEADS * A_HEAD_DIM
A_KV_WIDTH = A_KV_HEADS * A_HEAD_DIM
WINDOW = 128
ATT_SCALE = A_HEAD_DIM ** -0.5
N_BUCKETS = 32
MAX_DISTANCE = WINDOW
M_HEADS = 4
M_HEAD_DIM = 128
M_WIDTH = M_HEADS * M_HEAD_DIM
CONV_W = 4
M_CHUNK = 64
N_KEYS = 128
P_HEADS = 8
P_TOPK = 16
P_KEY_DIM = 256
P_HALF = P_KEY_DIM // 2
EPS = 1e-6
NEG_INF = -1e30

LANES = 128
SUBLANES = 8
SAMPLE_PAD = SUBLANES
VMEM_LIMIT = 48 * 1024 * 1024
PROMPT_GROUPS = 9

NT_DIMS = (((1,), (1,)), ((), ()))
TN_DIMS = (((0,), (0,)), ((), ()))


def _cparams(*sem):
    return pltpu.CompilerParams(dimension_semantics=sem, vmem_limit_bytes=VMEM_LIMIT)


def _bdot(a, b):
    return jnp.dot(a.astype(BF16), b.astype(BF16), preferred_element_type=F32)


def _bdot_nt(a, b):
    return lax.dot_general(a.astype(BF16), b.astype(BF16), NT_DIMS, preferred_element_type=F32)


def _sigmoid(x):
    return 1.0 / (1.0 + jnp.exp(-x))


def _log_sigmoid(x):
    return jnp.minimum(x, 0.0) - jnp.log1p(jnp.exp(-jnp.abs(x)))


def _ada_kernel(c_ref, w_ref, b_ref, o_ref):
    c = c_ref[...]
    s = c * _sigmoid(c)
    o_ref[...] = jnp.dot(s, w_ref[...], precision=HIGHEST, preferred_element_type=F32) + b_ref[...]


def _ada_call(c_all, w_ada, b_ada):
    depth, d, n6 = w_ada.shape
    rows = c_all.shape[0]
    bn = 1024
    return pl.pallas_call(
        _ada_kernel,
        grid=(depth, n6 // bn),
        in_specs=[
            pl.BlockSpec((rows, d), lambda l, j: (0, 0)),
            pl.BlockSpec((None, d, bn), lambda l, j: (l, 0, j)),
            pl.BlockSpec((None, 1, bn), lambda l, j: (l, 0, j)),
        ],
        out_specs=pl.BlockSpec((None, rows, bn), lambda l, j: (l, 0, j)),
        out_shape=jax.ShapeDtypeStruct((depth, rows, n6), F32),
        compiler_params=_cparams("parallel", "parallel"),
        name="ada_mod",
    )(c_all, w_ada, b_ada.reshape(depth, 1, n6))


def _mod_spec(per_token, tm, tokens_per_req):
    if per_token:
        return pl.BlockSpec((tm, D_MODEL), lambda i: (i, 0))
    tiles = tokens_per_req // tm
    return pl.BlockSpec((None, 1, D_MODEL), lambda i: (i // tiles, 0, 0))


def _in_kernel(x_ref, sc_ref, sh_ref, g_ref, wa_ref, wqk_ref, wv_ref, wo_ref, wg_ref, gb_ref,
               qkv_ref, qkm_ref, v_ref, o_ref, gc_ref):
    x = x_ref[...]
    y = x * lax.rsqrt(jnp.mean(x * x, axis=-1, keepdims=True) + EPS) * g_ref[...]
    h = (y * (1.0 + sc_ref[...]) + sh_ref[...]).astype(BF16)
    qkv_ref[...] = jnp.dot(h, wa_ref[...], preferred_element_type=F32)
    qkm_ref[...] = jnp.dot(h, wqk_ref[...], preferred_element_type=F32)
    v_ref[...] = jnp.dot(h, wv_ref[...], preferred_element_type=F32)
    o_ref[...] = jnp.dot(h, wo_ref[...], preferred_element_type=F32)
    g = jnp.dot(h, wg_ref[...], preferred_element_type=F32) + gb_ref[...]
    lane = lax.broadcasted_iota(jnp.int32, g.shape, 1)
    gc_ref[...] = jnp.where(lane < M_HEADS, g, jnp.where(lane < 2 * M_HEADS, _log_sigmoid(g), 0.0))


def _in_call(x, sc, sh, gnorm, wa, wqk, wv, wo, wg, gb, per_token, tokens_per_req):
    n = x.shape[0]
    tm = min(512, n if per_token else tokens_per_req)
    mod = _mod_spec(per_token, tm, tokens_per_req)
    full = lambda shape: pl.BlockSpec(shape, lambda i: (0,) * len(shape))
    row = lambda w: pl.BlockSpec((tm, w), lambda i: (i, 0))
    return pl.pallas_call(
        _in_kernel,
        grid=(n // tm,),
        in_specs=[row(D_MODEL), mod, mod, full((1, D_MODEL)), full(wa.shape), full(wqk.shape),
                  full(wv.shape), full(wo.shape), full(wg.shape), full((1, LANES))],
        out_specs=[row(wa.shape[1]), row(wqk.shape[1]), row(wv.shape[1]), row(wo.shape[1]), row(LANES)],
        out_shape=[jax.ShapeDtypeStruct((n, w), F32)
                   for w in (wa.shape[1], wqk.shape[1], wv.shape[1], wo.shape[1], LANES)],
        compiler_params=_cparams("parallel"),
        name="in_proj",
    )(x, sc, sh, gnorm, wa, wqk, wv, wo, wg, gb)


def _t5_bucket_np(dist):
    n = np.maximum(dist, 0)
    max_exact = N_BUCKETS // 2
    nf = np.maximum(n, 1).astype(np.float64)
    large = max_exact + (np.log(nf / max_exact) / math.log(MAX_DISTANCE / max_exact)
                         * (N_BUCKETS - max_exact)).astype(np.int32)
    return np.where(n < max_exact, n, np.minimum(large, N_BUCKETS - 1)).astype(np.int32)


def _bias_kernel(bucket_ref, rel_ref, o_ref):
    bucket = bucket_ref[...]
    for h in range(A_HEADS):
        acc = jnp.zeros(bucket.shape, F32)
        for b in range(N_BUCKETS):
            acc = jnp.where(bucket == b, rel_ref[b, h], acc)
        o_ref[h] = acc


def _bias_call(rel_bias, dist):
    bucket = jnp.asarray(_t5_bucket_np(dist))
    nq, nk = dist.shape
    return pl.pallas_call(
        _bias_kernel,
        in_specs=[pl.BlockSpec((nq, nk), lambda: (0, 0)),
                  pl.BlockSpec(memory_space=pltpu.SMEM)],
        out_specs=pl.BlockSpec((A_HEADS, nq, nk), lambda: (0, 0, 0)),
        out_shape=jax.ShapeDtypeStruct((A_HEADS, nq, nk), F32),
        name="t5_bias",
    )(bucket, rel_bias)


def _attn_p_kernel(q_ref, kp_ref, kc_ref, vp_ref, vc_ref, bias_ref, sink_ref, o_ref):
    i = pl.program_id(1)
    qi = lax.broadcasted_iota(jnp.int32, (WINDOW, WINDOW), 0)
    kj = lax.broadcasted_iota(jnp.int32, (WINDOW, WINDOW), 1)
    valid_prev = jnp.logical_and(kj > qi, i > 0)
    valid_cur = kj <= qi
    q = q_ref[...]
    for h in range(A_HEADS):
        kv = h // A_GROUP
        qh = q[:, h * A_HEAD_DIM:(h + 1) * A_HEAD_DIM]
        sl = slice(kv * A_HEAD_DIM, (kv + 1) * A_HEAD_DIM)
        bias = bias_ref[h]
        sp = _bdot_nt(qh, kp_ref[:, sl]) * ATT_SCALE + bias[:, :WINDOW]
        sc = _bdot_nt(qh, kc_ref[:, sl]) * ATT_SCALE + bias[:, WINDOW:]
        sp = jnp.where(valid_prev, sp, NEG_INF)
        sc = jnp.where(valid_cur, sc, NEG_INF)
        sink = sink_ref[0, h]
        mx = jnp.maximum(jnp.maximum(jnp.max(sp, axis=-1, keepdims=True),
                                     jnp.max(sc, axis=-1, keepdims=True)), sink)
        ep = jnp.exp(sp - mx)
        ec = jnp.exp(sc - mx)
        den = (jnp.sum(ep, axis=-1, keepdims=True) + jnp.sum(ec, axis=-1, keepdims=True)
               + jnp.exp(sink - mx))
        o = _bdot(ep / den, vp_ref[:, sl]) + _bdot(ec / den, vc_ref[:, sl])
        o_ref[:, h * A_HEAD_DIM:(h + 1) * A_HEAD_DIM] = o.astype(o_ref.dtype)


def _attn_p_call(qkv, bias, sinks, batch, seq):
    nb = seq // WINDOW
    n = batch * seq
    kcol = A_WIDTH // A_KV_WIDTH
    vcol = kcol + 1
    cur = lambda col: pl.BlockSpec((WINDOW, A_KV_WIDTH), lambda b, i: (b * nb + i, col))
    prev = lambda col: pl.BlockSpec((WINDOW, A_KV_WIDTH),
                                    lambda b, i: (b * nb + jnp.maximum(i - 1, 0), col))
    return pl.pallas_call(
        _attn_p_kernel,
        grid=(batch, nb),
        in_specs=[pl.BlockSpec((WINDOW, A_WIDTH), lambda b, i: (b * nb + i, 0)),
                  prev(kcol), cur(kcol), prev(vcol), cur(vcol),
                  pl.BlockSpec((A_HEADS, WINDOW, 2 * WINDOW), lambda b, i: (0, 0, 0)),
                  pl.BlockSpec(memory_space=pltpu.SMEM)],
        out_specs=pl.BlockSpec((WINDOW, A_WIDTH), lambda b, i: (b * nb + i, 0)),
        out_shape=jax.ShapeDtypeStruct((n, A_WIDTH), BF16),
        compiler_params=_cparams("parallel", "parallel"),
        name="swa_prompt",
    )(qkv, qkv, qkv, qkv, qkv, bias, sinks.reshape(1, A_HEADS))


def _attn_s_kernel(n_new, qkv_ref, ck_ref, cv_ref, bc_ref, bn_ref, sink_ref,
                   o_ref, nk_ref, nv_ref, kk_s, vv_s):
    qkv = qkv_ref[...]
    knew = qkv[:, A_WIDTH:A_WIDTH + A_KV_WIDTH]
    vnew = qkv[:, A_WIDTH + A_KV_WIDTH:A_WIDTH + 2 * A_KV_WIDTH]
    ck = ck_ref[...]
    cv = cv_ref[...]
    rows = A_GROUP * SAMPLE_PAD
    qi = lax.broadcasted_iota(jnp.int32, (rows, WINDOW), 0) % SAMPLE_PAD
    kj = lax.broadcasted_iota(jnp.int32, (rows, WINDOW), 1)
    valid_c = kj > qi
    rcol = lax.broadcasted_iota(jnp.int32, (rows, 1), 0)
    qcol = rcol % SAMPLE_PAD
    for kv in range(A_KV_HEADS):
        heads = range(kv * A_GROUP, (kv + 1) * A_GROUP)
        sl = slice(kv * A_HEAD_DIM, (kv + 1) * A_HEAD_DIM)
        qs = jnp.concatenate([qkv[:, h * A_HEAD_DIM:(h + 1) * A_HEAD_DIM] for h in heads], axis=0)
        bias_c = jnp.concatenate([bc_ref[h] for h in heads], axis=0)
        bias_n = jnp.concatenate([bn_ref[h] for h in heads], axis=0)
        sink = jnp.zeros((rows, 1), F32)
        for g, h in enumerate(heads):
            sink = jnp.where(rcol // SAMPLE_PAD == g, sink_ref[0, h], sink)
        s_c = _bdot_nt(qs, ck[:, sl]) * ATT_SCALE + bias_c
        s_c = jnp.where(valid_c, s_c, NEG_INF)
        s_n = []
        for j in range(n_new):
            sj = jnp.sum(qs * knew[j:j + 1, sl], axis=-1, keepdims=True) * ATT_SCALE + bias_n[:, j:j + 1]
            s_n.append(jnp.where(qcol >= j, sj, NEG_INF))
        mx = jnp.maximum(jnp.max(s_c, axis=-1, keepdims=True), sink)
        for sj in s_n:
            mx = jnp.maximum(mx, sj)
        e_c = jnp.exp(s_c - mx)
        den = jnp.sum(e_c, axis=-1, keepdims=True) + jnp.exp(sink - mx)
        o = _bdot(e_c, cv[:, sl])
        for j, sj in enumerate(s_n):
            ej = jnp.exp(sj - mx)
            den = den + ej
            o = o + ej * vnew[j:j + 1, sl]
        o = o / den
        for g, h in enumerate(heads):
            o_ref[:, h * A_HEAD_DIM:(h + 1) * A_HEAD_DIM] = o[g * SAMPLE_PAD:(g + 1) * SAMPLE_PAD, :]
    kk_s[0:WINDOW, :] = ck
    kk_s[WINDOW:WINDOW + SAMPLE_PAD, :] = knew
    vv_s[0:WINDOW, :] = cv
    vv_s[WINDOW:WINDOW + SAMPLE_PAD, :] = vnew
    nk_ref[...] = kk_s[n_new:n_new + WINDOW, :]
    nv_ref[...] = vv_s[n_new:n_new + WINDOW, :]


def _attn_s_call(qkv, ck, cv, bias_c, bias_n, sinks, n_new):
    nreq = ck.shape[0]
    wq = qkv.shape[1]
    full3 = lambda shape: pl.BlockSpec(shape, lambda b: (0, 0, 0))
    cache = pl.BlockSpec((None, WINDOW, A_KV_WIDTH), lambda b: (b, 0, 0))
    return pl.pallas_call(
        functools.partial(_attn_s_kernel, n_new),
        grid=(nreq,),
        in_specs=[pl.BlockSpec((SAMPLE_PAD, wq), lambda b: (b, 0)), cache, cache,
                  full3(bias_c.shape), full3(bias_n.shape),
                  pl.BlockSpec(memory_space=pltpu.SMEM)],
        out_specs=[pl.BlockSpec((SAMPLE_PAD, A_WIDTH), lambda b: (b, 0)), cache, cache],
        out_shape=[jax.ShapeDtypeStruct((nreq * SAMPLE_PAD, A_WIDTH), F32),
                   jax.ShapeDtypeStruct(ck.shape, F32), jax.ShapeDtypeStruct(cv.shape, F32)],
        scratch_shapes=[pltpu.VMEM((WINDOW + SAMPLE_PAD, A_KV_WIDTH), F32),
                        pltpu.VMEM((WINDOW + SAMPLE_PAD, A_KV_WIDTH), F32)],
        compiler_params=_cparams("parallel"),
        name="swa_sample",
    )(qkv, ck, cv, bias_c, bias_n, sinks.reshape(1, A_HEADS))


def _mlstm_kernel(chunk, t_valid, qk_ref, v_ref, og_ref, gc_ref, cw_ref, cb_ref, mn_ref,
                  conv0_ref, c0_ref, n0_ref, m0_ref,
                  out_ref, cout_ref, nout_ref, mout_ref,
                  xp_s, c_s, n_s, m_s):
    step = pl.program_id(1)
    halo = SUBLANES

    @pl.when(step == 0)
    def _():
        xp_s[0:halo, :] = conv0_ref[...]
        c_s[...] = c0_ref[...]
        n_s[...] = n0_ref[...]
        m_s[...] = m0_ref[...]

    xp_s[halo:halo + chunk, :] = qk_ref[...]
    cw = cw_ref[...]
    y = cb_ref[...]
    for i in range(CONV_W):
        off = halo - (CONV_W - 1) + i
        y = y + xp_s[off:off + chunk, :] * cw[i:i + 1, :]
    xp_s[0:halo, :] = xp_s[chunk:chunk + halo, :]
    y = y * _sigmoid(y)
    q_all = y[:, :M_WIDTH]
    k_all = y[:, M_WIDTH:] * (M_HEAD_DIM ** -0.5)

    g = gc_ref[...]
    if t_valid < chunk:
        row = lax.broadcasted_iota(jnp.int32, g.shape, 0)
        lane = lax.broadcasted_iota(jnp.int32, g.shape, 1)
        g = jnp.where(row < t_valid, g, jnp.where(lane < M_HEADS, NEG_INF, 0.0))
    tr = lax.broadcasted_iota(jnp.int32, (chunk, chunk), 0)
    tc = lax.broadcasted_iota(jnp.int32, (chunk, chunk), 1)
    causal = tr >= tc
    tri = causal.astype(F32)
    bcol = jnp.dot(tri, g, precision=HIGHEST, preferred_element_type=F32)
    er = lax.broadcasted_iota(jnp.int32, (SUBLANES, LANES), 0)
    ec = lax.broadcasted_iota(jnp.int32, (SUBLANES, LANES), 1)
    eye = (er == ec).astype(F32)
    g_rows = lax.dot_general(eye, g, NT_DIMS, precision=HIGHEST, preferred_element_type=F32)
    b_rows = lax.dot_general(eye, bcol, NT_DIMS, precision=HIGHEST, preferred_element_type=F32)

    for h in range(M_HEADS):
        hs = slice(h * M_HEAD_DIM, (h + 1) * M_HEAD_DIM)
        b_c = bcol[:, M_HEADS + h:M_HEADS + h + 1]
        ig_c = g[:, h:h + 1]
        b_r = b_rows[M_HEADS + h:M_HEADS + h + 1, :]
        ig_r = g_rows[h:h + 1, :]
        m_prev = m_s[h:h + 1, 0:1]
        logw = jnp.where(causal, b_c - b_r + ig_r, -jnp.inf)
        inter = b_c + m_prev
        m_t = jnp.maximum(inter, jnp.max(logw, axis=-1, keepdims=True))
        w = jnp.exp(logw - m_t)
        a = jnp.exp(inter - m_t)
        q = q_all[:, hs]
        k = k_all[:, hs]
        v = v_ref[:, hs]
        cmat = c_s[h]
        nvec = n_s[h:h + 1, :]
        wqk = w * _bdot_nt(q, k)
        num = _bdot(wqk, v) + a * _bdot_nt(q, cmat)
        den = jnp.sum(wqk, axis=-1, keepdims=True) + a * jnp.sum(q * nvec, axis=-1, keepdims=True)
        hh = num / jnp.maximum(jnp.abs(den), jnp.exp(-m_t))
        m_new = m_t[chunk - 1:chunk, :]
        b_last = b_c[chunk - 1:chunk, :]
        wl = jnp.exp(b_last - b_c + ig_c - m_new)
        al = jnp.exp(b_last + m_prev - m_new)
        c_s[h] = al * cmat + lax.dot_general((v * wl).astype(BF16), k.astype(BF16), TN_DIMS,
                                             preferred_element_type=F32)
        n_s[h:h + 1, :] = al * nvec + jnp.sum(wl * k, axis=0, keepdims=True)
        m_s[h:h + 1, :] = jnp.broadcast_to(m_new, (1, LANES))
        hn = hh * lax.rsqrt(jnp.mean(hh * hh, axis=-1, keepdims=True) + EPS) * mn_ref[:, hs]
        out_ref[:, hs] = (_sigmoid(og_ref[:, hs]) * hn).astype(out_ref.dtype)

    @pl.when(step == pl.num_programs(1) - 1)
    def _():
        cout_ref[...] = c_s[...]
        nout_ref[...] = n_s[...]
        mout_ref[...] = m_s[...]


def _mlstm_call(qk, v, og, gc, conv_w, conv_b, m_norm, conv0, c0, n0, m0, batch, seq, chunk, t_valid):
    nc = seq // chunk
    n = batch * seq
    row = lambda w: pl.BlockSpec((chunk, w), lambda b, c: (b * nc + c, 0))
    full2 = lambda shape: pl.BlockSpec(shape, lambda b, c: (0, 0))
    per_b = lambda shape: pl.BlockSpec((None,) + shape, lambda b, c: (b,) + (0,) * len(shape))
    dh = M_HEAD_DIM
    return pl.pallas_call(
        functools.partial(_mlstm_kernel, chunk, t_valid),
        grid=(batch, nc),
        in_specs=[row(2 * M_WIDTH), row(M_WIDTH), row(M_WIDTH), row(LANES),
                  full2((CONV_W, 2 * M_WIDTH)), full2((1, 2 * M_WIDTH)), full2((1, M_WIDTH)),
                  per_b((SUBLANES, 2 * M_WIDTH)), per_b((M_HEADS, dh, dh)), per_b((M_HEADS, dh)),
                  per_b((SUBLANES, LANES))],
        out_specs=[row(M_WIDTH), per_b((M_HEADS, dh, dh)), per_b((M_HEADS, dh)), per_b((SUBLANES, LANES))],
        out_shape=[jax.ShapeDtypeStruct((n, M_WIDTH), BF16 if chunk % (2 * SUBLANES) == 0 else F32),
                   jax.ShapeDtypeStruct((batch, M_HEADS, dh, dh), F32),
                   jax.ShapeDtypeStruct((batch, M_HEADS, dh), F32),
                   jax.ShapeDtypeStruct((batch, SUBLANES, LANES), F32)],
        scratch_shapes=[pltpu.VMEM((SUBLANES + chunk, 2 * M_WIDTH), F32),
                        pltpu.VMEM((M_HEADS, dh, dh), F32),
                        pltpu.VMEM((M_HEADS, dh), F32),
                        pltpu.VMEM((SUBLANES, LANES), F32)],
        compiler_params=_cparams("parallel", "arbitrary"),
        name="mlstm",
    )(qk, v, og, gc, conv_w, conv_b, m_norm, conv0, c0, n0, m0)


def _pack_words(x):
    bits = pltpu.bitcast(x.astype(BF16).astype(F32), jnp.uint32)
    half = x.shape[1] // 2
    lo = lax.shift_right_logical(bits[:, :half], jnp.uint32(16))
    hi = bits[:, half:] & jnp.uint32(0xFFFF0000)
    return pltpu.bitcast(lo | hi, jnp.int32)


def _out_kernel(att_ref, mo_ref, x_ref, g1_ref, sc_ref, sh_ref, gn_ref, wa_ref, wm_ref, wq_ref,
                xo_ref, h2_ref, qp_ref):
    mix = (jnp.dot(att_ref[...].astype(BF16), wa_ref[...], preferred_element_type=F32)
           + jnp.dot(mo_ref[...].astype(BF16), wm_ref[...], preferred_element_type=F32))
    x = x_ref[...] + g1_ref[...] * mix
    xo_ref[...] = x
    y = x * lax.rsqrt(jnp.mean(x * x, axis=-1, keepdims=True) + EPS) * gn_ref[...]
    h2 = y * (1.0 + sc_ref[...]) + sh_ref[...]
    qp_ref[...] = jnp.dot(h2.astype(BF16), wq_ref[...], preferred_element_type=F32).astype(qp_ref.dtype)
    h2_ref[...] = _pack_words(h2)


def _out_call(att, mo, x, g1, sc, sh, gnorm, wa, wm, wq, per_token, tokens_per_req):
    n = x.shape[0]
    tm = min(256, n if per_token else tokens_per_req)
    mod = _mod_spec(per_token, tm, tokens_per_req)
    full = lambda shape: pl.BlockSpec(shape, lambda i: (0,) * len(shape))
    row = lambda w: pl.BlockSpec((tm, w), lambda i: (i, 0))
    nq = wq.shape[1]
    return pl.pallas_call(
        _out_kernel,
        grid=(n // tm,),
        in_specs=[row(A_WIDTH), row(M_WIDTH), row(D_MODEL), mod, mod, mod, full((1, D_MODEL)),
                  full(wa.shape), full(wm.shape), full(wq.shape)],
        out_specs=[row(D_MODEL), row(D_MODEL // 2), row(nq)],
        out_shape=[jax.ShapeDtypeStruct((n, D_MODEL), F32), jax.ShapeDtypeStruct((n, D_MODEL // 2), jnp.int32),
                   jax.ShapeDtypeStruct((n, nq), BF16)],
        compiler_params=_cparams("parallel"),
        name="out_proj",
    )(att, mo, x, g1, sc, sh, gnorm, wa, wm, wq)


def _pk_cells():
    return [(a, b) for a in range(P_TOPK) for b in range(P_TOPK) if (a + 1) * (b + 1) <= P_TOPK]


PK_CELL_ROWS = 64


def _pk_expand_mats():
    cells = _pk_cells()
    e0 = np.zeros((PK_CELL_ROWS, LANES), np.float32)
    e1 = np.zeros((PK_CELL_ROWS, LANES), np.float32)
    for j, (a, b) in enumerate(cells):
        e0[j, a] = 1.0
        e1[j, b] = 1.0
    return e0, e1, len(cells)


def _top_rows(s, rowf, rounds):
    n_rows = s.shape[0]
    vals, idxs = [], []
    for _ in range(rounds):
        m = jnp.max(s, axis=0, keepdims=True)
        i = jnp.min(jnp.where(s == m, rowf, float(n_rows)), axis=0, keepdims=True)
        vals.append(m)
        idxs.append(i)
        s = jnp.where(rowf == i, -jnp.inf, s)
    return jnp.concatenate(vals, axis=0), jnp.concatenate(idxs, axis=0)


def _select_kernel(n_cells, qp_ref, keys_ref, e0_ref, e1_ref, idx_ref, gw_ref, idx_s, gw_s):
    tm = qp_ref.shape[0]
    keyf = lax.broadcasted_iota(jnp.int32, (N_KEYS, tm), 0).astype(F32)
    cellf = lax.broadcasted_iota(jnp.int32, (PK_CELL_ROWS, tm), 0).astype(F32)
    e0 = e0_ref[...]
    e1 = e1_ref[...]
    pad = jnp.zeros((LANES - P_TOPK, tm), F32)

    def head(h, carry):
        h = jnp.asarray(h, jnp.int32)
        sub = []
        for c in range(2):
            col = pl.multiple_of((h * 2 + c) * P_HALF, P_HALF)
            s = _bdot_nt(keys_ref[h, c], qp_ref[:, pl.ds(col, P_HALF)])
            sub.append(_top_rows(s, keyf, P_TOPK))
        (v0, i0), (v1, i1) = sub
        expand = lambda e, x: jnp.dot(e, jnp.concatenate([x, pad], axis=0), precision=HIGHEST,
                                      preferred_element_type=F32)
        cand = expand(e0, v0) + expand(e1, v1)
        cidx = expand(e0, i0 * float(N_KEYS)) + expand(e1, i1)
        cand = jnp.where(cellf < n_cells, cand, -jnp.inf)
        best, eidx = [], []
        for _ in range(P_TOPK):
            m = jnp.max(cand, axis=0, keepdims=True)
            j = jnp.min(jnp.where(cand == m, cellf, float(PK_CELL_ROWS)), axis=0, keepdims=True)
            hit = cellf == j
            eidx.append(jnp.max(jnp.where(hit, cidx, -1.0), axis=0, keepdims=True))
            best.append(m)
            cand = jnp.where(hit, -jnp.inf, cand)
        best = jnp.concatenate(best, axis=0)
        e = jnp.exp(best - best[0:1, :])
        row0 = pl.multiple_of(h * P_TOPK, P_TOPK)
        gw_s[pl.ds(row0, P_TOPK), :] = e / jnp.sum(e, axis=0, keepdims=True)
        idx_s[pl.ds(row0, P_TOPK), :] = jnp.concatenate(eidx, axis=0)
        return carry

    lax.fori_loop(0, P_HEADS, head, 0)
    idx_ref[...] = idx_s[...].T.astype(jnp.int32)
    gw_ref[...] = gw_s[...].T


def _select_call(qp, keys_bf16):
    n = qp.shape[0]
    tm = min(LANES, n)
    e0, e1, n_cells = _pk_expand_mats()
    full = lambda shape: pl.BlockSpec(shape, lambda i: (0,) * len(shape))
    return pl.pallas_call(
        functools.partial(_select_kernel, n_cells),
        grid=(n // tm,),
        in_specs=[pl.BlockSpec((tm, qp.shape[1]), lambda i: (i, 0)), full(keys_bf16.shape),
                  full((PK_CELL_ROWS, LANES)), full((PK_CELL_ROWS, LANES))],
        out_specs=[pl.BlockSpec((tm, LANES), lambda i: (i, 0)), pl.BlockSpec((tm, LANES), lambda i: (i, 0))],
        out_shape=[jax.ShapeDtypeStruct((n, LANES), jnp.int32), jax.ShapeDtypeStruct((n, LANES), F32)],
        scratch_shapes=[pltpu.VMEM((P_HEADS * P_TOPK, tm), F32), pltpu.VMEM((P_HEADS * P_TOPK, tm), F32)],
        compiler_params=_cparams("parallel"),
        name="peer_select",
    )(qp, keys_bf16, jnp.asarray(e0), jnp.asarray(e1))


N_SEL = P_HEADS * P_TOPK

SC_CORES = 2
SC_SUBCORES = 16
SC_LANES = 16
SC_TOK_BLOCK = 8
SC_ROWS = 32
SC_ROW_BLOCK = 16
SC_ACC_CHAINS = 4
SC_ACC_ROWS = 16
SC_NBUF = 4
N_WCOL = D_MODEL // (2 * SC_LANES)


def _pack_kernel(t_ref, o_ref):
    o_ref[...] = _pack_words(t_ref[...])


def _pack_table(tables, layer):
    _, e, dcol = tables.shape
    tm = min(1024, e)
    return pl.pallas_call(
        _pack_kernel,
        grid=(e // tm,),
        in_specs=[pl.BlockSpec((None, tm, dcol), lambda i: (layer, i, 0))],
        out_specs=pl.BlockSpec((tm, dcol // 2), lambda i: (i, 0)),
        out_shape=jax.ShapeDtypeStruct((e, dcol // 2), jnp.int32),
        compiler_params=_cparams("parallel"),
        name="pack_table",
    )(tables)


def _sc_gelu(x):
    z = math.sqrt(2.0 / math.pi) * (x + 0.044715 * (x * x * x))
    t = 1.0 - 2.0 / (jnp.exp(2.0 * z) + 1.0)
    return x * (0.5 * (1.0 + t))


def _sc_expert_body(tokens_per_worker, idx_hbm, gw_hbm, h2_hbm, u_hbm, v_hbm, y_hbm,
                    idx_v, gw_v, x_v, o_v, buf, coef_v, tr_v, sem, in_sem, out_sem):
    wid = lax.axis_index("s") * SC_CORES + lax.axis_index("c")
    base = wid * tokens_per_worker
    lane = lax.iota(jnp.int32, SC_LANES)
    n_gather = N_SEL // SC_ROWS
    zero = jnp.zeros((SC_LANES,), F32)

    n_steps = 2 * n_gather
    assert n_steps % SC_NBUF == 0

    def gather(p, tt, i):
        table = u_hbm if i < n_gather else v_hbm
        j = i % n_gather
        slot = i % SC_NBUF
        return pltpu.make_async_copy(table.at[idx_v.at[p, tt, pl.ds(j * SC_ROWS, SC_ROWS)]], buf.at[slot],
                                     sem.at[slot])

    def unpack(w):
        lo = lax.bitcast_convert_type(lax.shift_left(w, jnp.full((SC_LANES,), 16, jnp.int32)), F32)
        hi = lax.bitcast_convert_type(w & jnp.full((SC_LANES,), -65536, jnp.int32), F32)
        return lo, hi

    def packed(w):
        return plsc.bitcast(w, BF16)

    def unpack_sum(s):
        return unpack(plsc.bitcast(s, jnp.int32))

    def act_chunk(p, tt, j, slot):
        @pl.loop(0, SC_ROWS // SC_LANES)
        def _(half):
            for rb in range(SC_LANES // SC_ROW_BLOCK):
                r0 = half * SC_LANES + rb * SC_ROW_BLOCK

                def col(c, accs):
                    w0 = pl.multiple_of(c * (4 * SC_LANES), 4 * SC_LANES)
                    xs = [packed(x_v[p, tt, pl.ds(w0 + q * SC_LANES, SC_LANES)]) for q in range(4)]
                    out = []
                    for r, a in enumerate(accs):
                        us = [packed(buf[slot, r0 + r, pl.ds(w0 + q * SC_LANES, SC_LANES)]) for q in range(4)]
                        lo, hi = unpack_sum((us[0] * xs[0] + us[1] * xs[1]) + (us[2] * xs[2] + us[3] * xs[3]))
                        out.append(a + lo + hi)
                    return tuple(out)

                accs = lax.fori_loop(0, N_WCOL // 4, col, (zero,) * SC_ROW_BLOCK)
                for r in range(SC_ROW_BLOCK):
                    tr_v[pl.ds((rb * SC_ROW_BLOCK + r) * SC_LANES, SC_LANES)] = accs[r]
            tot = zero
            for jj in range(SC_LANES):
                tot = tot + plsc.load_gather(tr_v, [lane * SC_LANES + jj])
            k0 = pl.multiple_of(j * SC_ROWS + half * SC_LANES, SC_LANES)
            coef_v[pl.ds(k0, SC_LANES)] = gw_v[p, tt, pl.ds(k0, SC_LANES)] * _sc_gelu(tot)

    def acc_chunk(p, tt, j, slot, first):
        def tree_sum(parts):
            while len(parts) > 1:
                parts = [parts[i] + parts[i + 1] for i in range(0, len(parts), 2)]
            return parts[0]

        for rb in range(SC_ROWS // SC_ACC_ROWS):
            rows = list(range(rb * SC_ACC_ROWS, (rb + 1) * SC_ACC_ROWS))
            splat = {}
            for r in rows:
                c16 = plsc.load_gather(coef_v, [jnp.full((SC_LANES,), j * SC_ROWS + r, jnp.int32)])
                splat[r] = plsc.pack(c16, c16, format=plsc.PackFormat.INTERLEAVED)
            fresh = first and rb == 0

            @plsc.parallel_loop(0, N_WCOL)
            def _(c):
                woff = pl.multiple_of(c * SC_LANES, SC_LANES)
                hoff = pl.multiple_of(c * SC_LANES + D_MODEL // 2, SC_LANES)
                pa, pb = [], []
                for n in range(0, SC_ACC_ROWS, 2):
                    r0, r1 = rows[n], rows[n + 1]
                    s = (splat[r0] * packed(buf[slot, r0, pl.ds(woff, SC_LANES)])
                         + splat[r1] * packed(buf[slot, r1, pl.ds(woff, SC_LANES)]))
                    lo, hi = unpack_sum(s)
                    if n // 2 < SC_ACC_CHAINS:
                        pa.append(lo)
                        pb.append(hi)
                    else:
                        pa[(n // 2) % SC_ACC_CHAINS] = pa[(n // 2) % SC_ACC_CHAINS] + lo
                        pb[(n // 2) % SC_ACC_CHAINS] = pb[(n // 2) % SC_ACC_CHAINS] + hi
                sa, sb = tree_sum(pa), tree_sum(pb)
                if not fresh:
                    sa = sa + o_v[p, tt, pl.ds(woff, SC_LANES)]
                    sb = sb + o_v[p, tt, pl.ds(hoff, SC_LANES)]
                o_v[p, tt, pl.ds(woff, SC_LANES)] = sa
                o_v[p, tt, pl.ds(hoff, SC_LANES)] = sb

    n_blocks = tokens_per_worker // SC_TOK_BLOCK

    def block_start(blk):
        return pl.multiple_of(base + blk * SC_TOK_BLOCK, SC_TOK_BLOCK)

    def in_copies(blk, p):
        rows = pl.ds(block_start(blk), SC_TOK_BLOCK)
        return [pltpu.make_async_copy(idx_hbm.at[rows], idx_v.at[p], in_sem.at[p]),
                pltpu.make_async_copy(gw_hbm.at[rows], gw_v.at[p], in_sem.at[p]),
                pltpu.make_async_copy(h2_hbm.at[rows], x_v.at[p], in_sem.at[p])]

    def out_copy(blk, p):
        return pltpu.make_async_copy(o_v.at[p], y_hbm.at[pl.ds(block_start(blk), SC_TOK_BLOCK)], out_sem.at[p])

    ahead = SC_NBUF - 1
    for cp in in_copies(0, 0):
        cp.start()
    for cp in in_copies(0, 0):
        cp.wait()
    for cp in in_copies(1, 1):
        cp.start()
    for i in range(ahead):
        gather(0, 0, i).start()

    @pl.loop(0, n_blocks)
    def _(blk):
        p = lax.rem(blk, 2)

        @pl.when(blk >= 2)
        def _():
            out_copy(blk - 2, p).wait()

        @pl.loop(0, SC_TOK_BLOCK)
        def _(tt):
            for i in range(n_steps):
                if i + ahead < n_steps:
                    gather(p, tt, i + ahead).start()
                else:
                    nxt = i + ahead - n_steps

                    @pl.when(tt + 1 < SC_TOK_BLOCK)
                    def _():
                        gather(p, tt + 1, nxt).start()

                    @pl.when(jnp.logical_and(tt + 1 == SC_TOK_BLOCK, blk + 1 < n_blocks))
                    def _():
                        if nxt == 0:
                            for cp in in_copies(blk + 1, 1 - p):
                                cp.wait()
                        gather(1 - p, 0, nxt).start()
                gather(p, tt, i).wait()
                if i < n_gather:
                    act_chunk(p, tt, i, i % SC_NBUF)
                else:
                    acc_chunk(p, tt, i - n_gather, i % SC_NBUF, i == n_gather)

        out_copy(blk, p).start()

        @pl.when(blk + 2 < n_blocks)
        def _():
            for cp in in_copies(blk + 2, p):
                cp.start()

    for blk in (n_blocks - 2, n_blocks - 1):
        out_copy(blk, blk % 2).wait()


def _sc_expert_call(idx, gw, xw, u, v):
    n = idx.shape[0]
    workers = SC_CORES * SC_SUBCORES
    assert n % (workers * SC_TOK_BLOCK) == 0 and n // (workers * SC_TOK_BLOCK) >= 2
    mesh = plsc.VectorSubcoreMesh(core_axis_name="c", subcore_axis_name="s")
    return pl.kernel(
        functools.partial(_sc_expert_body, n // workers),
        out_type=jax.ShapeDtypeStruct((n, D_MODEL), F32),
        mesh=mesh,
        scratch_types=[pltpu.VMEM((2, SC_TOK_BLOCK, N_SEL), jnp.int32),
                       pltpu.VMEM((2, SC_TOK_BLOCK, N_SEL), F32),
                       pltpu.VMEM((2, SC_TOK_BLOCK, D_MODEL // 2), jnp.int32),
                       pltpu.VMEM((2, SC_TOK_BLOCK, D_MODEL), F32),
                       pltpu.VMEM((SC_NBUF, SC_ROWS, D_MODEL // 2), jnp.int32),
                       pltpu.VMEM((N_SEL,), F32),
                       pltpu.VMEM((SC_LANES * SC_LANES,), F32),
                       pltpu.SemaphoreType.DMA((SC_NBUF,)), pltpu.SemaphoreType.DMA((2,)),
                       pltpu.SemaphoreType.DMA((2,))],
        compiler_params=pltpu.CompilerParams(needs_layout_passes=False),
        name="peer_experts_sc",
    )(idx, gw, xw, u, v)


def _resid_kernel(x_ref, y_ref, g_ref, o_ref):
    o_ref[...] = x_ref[...] + g_ref[...] * y_ref[...]


def _resid_call(x, y, g2, per_token, tokens_per_req):
    n = x.shape[0]
    tm = min(512, n if per_token else tokens_per_req)
    row = pl.BlockSpec((tm, D_MODEL), lambda i: (i, 0))
    return pl.pallas_call(
        _resid_kernel,
        grid=(n // tm,),
        in_specs=[row, row, _mod_spec(per_token, tm, tokens_per_req)],
        out_specs=row,
        out_shape=jax.ShapeDtypeStruct((n, D_MODEL), F32),
        compiler_params=_cparams("parallel"),
        name="peer_residual",
    )(x, y, g2)


def _final_kernel(x_ref, g_ref, o_ref):
    x = x_ref[...]
    o_ref[...] = x * lax.rsqrt(jnp.mean(x * x, axis=-1, keepdims=True) + EPS) * g_ref[...]


def _final_call(x, g):
    n = x.shape[0]
    tm = min(512, n)
    return pl.pallas_call(
        _final_kernel,
        grid=(n // tm,),
        in_specs=[pl.BlockSpec((tm, D_MODEL), lambda i: (i, 0)), pl.BlockSpec((1, D_MODEL), lambda i: (0, 0))],
        out_specs=pl.BlockSpec((tm, D_MODEL), lambda i: (i, 0)),
        out_shape=jax.ShapeDtypeStruct((n, D_MODEL), F32),
        compiler_params=_cparams("parallel"),
        name="final_norm",
    )(x, g)


def _split_w_in(w_in_l, gate_b_l):
    cuts = np.cumsum([A_WIDTH + 2 * A_KV_WIDTH, 2 * M_WIDTH, M_WIDTH, M_WIDTH]).tolist()
    wa = w_in_l[:, :cuts[0]].astype(BF16)
    wqk = w_in_l[:, cuts[0]:cuts[1]].astype(BF16)
    wv = w_in_l[:, cuts[1]:cuts[2]].astype(BF16)
    wo = w_in_l[:, cuts[2]:cuts[3]].astype(BF16)
    ng = 2 * M_HEADS
    wg = jnp.pad(w_in_l[:, cuts[3]:], ((0, 0), (0, LANES - ng))).astype(BF16)
    gb = jnp.pad(gate_b_l.astype(F32), (0, LANES - ng)).reshape(1, LANES)
    return wa, wqk, wv, wo, wg, gb


def _layer(x, mods, per_token, batch, seq, t_valid, lw, bias_p, bias_c, bias_n, kv_cache, conv0, state,
           after=None):
    (norm_mix, norm_ffn, w_in, conv_w, conv_b, gate_b, sinks, m_norm, w_out, peer_query, peer_keys,
     peer_u, peer_v) = lw
    if after is not None:
        x, _ = lax.optimization_barrier((x, after))
    sh1, sc1, g1, sh2, sc2, g2 = mods
    wa, wqk, wv, wo, wg, gb = _split_w_in(w_in, gate_b)
    qkv, qkm, vm, om, gc = _in_call(x, sc1, sh1, norm_mix.reshape(1, -1), wa, wqk, wv, wo, wg, gb,
                                    per_token, seq)
    if kv_cache is None:
        att = _attn_p_call(qkv, bias_p, sinks, batch, seq)
        kv3 = qkv.reshape(batch, seq, -1)
        new_k = kv3[:, seq - WINDOW:, A_WIDTH:A_WIDTH + A_KV_WIDTH]
        new_v = kv3[:, seq - WINDOW:, A_WIDTH + A_KV_WIDTH:]
        chunk = M_CHUNK
    else:
        att, new_k, new_v = _attn_s_call(qkv, kv_cache[0], kv_cache[1], bias_c, bias_n, sinks, t_valid)
        chunk = seq
    c0, n0, m0 = state
    mo, c_new, n_new, m_new = _mlstm_call(qkm, vm, om, gc, conv_w, conv_b.reshape(1, -1),
                                          m_norm.reshape(1, -1), conv0, c0, n0, m0,
                                          batch, seq, chunk, min(t_valid, chunk))
    new_conv = qkm.reshape(batch, seq, -1)[:, t_valid - (CONV_W - 1):t_valid]
    pad_rows = seq - t_valid
    if pad_rows:
        keep = lambda a: a.reshape(batch, seq, -1)[:, :t_valid].reshape(batch * t_valid, -1)
        att, mo, x, g1, sc2, sh2, g2 = (keep(a) for a in (att, mo, x, g1, sc2, sh2, g2))
    x_mid, xw, qp = _out_call(att, mo, x, g1, sc2, sh2, norm_ffn.reshape(1, -1),
                              w_out[:A_WIDTH].astype(BF16), w_out[A_WIDTH:].astype(BF16),
                              peer_query.astype(BF16), per_token, t_valid)
    idx, gw = _select_call(qp, peer_keys.astype(BF16))
    y = _sc_expert_call(idx, gw, xw, peer_u, peer_v)
    x_new = _resid_call(x_mid, y, g2, per_token, t_valid)
    if pad_rows:
        x_new = jnp.pad(x_new.reshape(batch, t_valid, -1), ((0, 0), (0, pad_rows), (0, 0))).reshape(batch * seq, -1)
    new_k = new_k.reshape(batch, WINDOW, A_KV_HEADS, A_HEAD_DIM)
    new_v = new_v.reshape(batch, WINDOW, A_KV_HEADS, A_HEAD_DIM)
    return x_new, (new_k, new_v, new_conv, c_new, n_new, m_new[:, :M_HEADS, 0]), idx


def _prompt_group_sizes(n_req):
    if n_req < PROMPT_GROUPS:
        return [1] * n_req
    mid, n_mid = n_req - 2, PROMPT_GROUPS - 2
    weights = [i + 2 for i in range(n_mid)]
    sizes = [max(1, mid * w // sum(weights)) for w in weights]
    for i in range(mid - sum(sizes)):
        sizes[n_mid - 1 - i % n_mid] += 1
    return [1] + sizes + [1]


def kernel(x_prompt, x_sample, c_prompt, c_sample, cache_k, cache_v, state_conv, state_C, state_n, state_m, rel_bias, w_ada, b_ada, norm_mix, norm_ffn, w_in, conv_w, conv_b, gate_b, attn_sinks, m_norm, w_out, peer_query, peer_keys, peer_u, peer_v, norm_final):
    depth = w_ada.shape[0]
    bp, tp, d = x_prompt.shape
    bs, ts, _ = x_sample.shape
    assert tp % WINDOW == 0 and tp % M_CHUNK == 0 and ts <= SAMPLE_PAD and ts >= CONV_W - 1

    mod_all = _ada_call(jnp.concatenate([c_prompt, c_sample], axis=0), w_ada, b_ada)

    qi = np.arange(WINDOW)[:, None]
    bias_p = _bias_call(rel_bias, qi + WINDOW - np.arange(2 * WINDOW)[None, :])
    qs = np.arange(SAMPLE_PAD)[:, None]
    bias_c = _bias_call(rel_bias, qs + WINDOW - np.arange(WINDOW)[None, :])
    bias_n = _bias_call(rel_bias, qs - np.arange(SAMPLE_PAD)[None, :])

    sizes = _prompt_group_sizes(bp)
    starts = np.cumsum([0] + sizes).tolist()
    xg = [x_prompt[starts[g]:starts[g + 1]].reshape(sizes[g] * tp, d) for g in range(len(sizes))]
    xs = jnp.pad(x_sample, ((0, 0), (0, SAMPLE_PAD - ts), (0, 0))).reshape(bs * SAMPLE_PAD, d)
    halo_pad = ((0, 0), (SUBLANES - (CONV_W - 1), 0), (0, 0))

    st_p, st_s = [], []
    for l in range(depth):
        lw = (norm_mix[l], norm_ffn[l], w_in[l], conv_w[l], conv_b[l], gate_b[l], attn_sinks[l], m_norm[l],
              w_out[l], peer_query[l], peer_keys[l], _pack_table(peer_u, l), _pack_table(peer_v, l))
        mod_s = [jnp.repeat(m, SAMPLE_PAD, axis=0) for m in jnp.split(mod_all[l, bp:], 6, axis=-1)]
        sp_groups = []
        for g, bg in enumerate(sizes):
            mod_g = [m.reshape(bg, 1, d) for m in jnp.split(mod_all[l, starts[g]:starts[g + 1]], 6, axis=-1)]
            zero_state = (jnp.zeros((bg, M_HEADS, M_HEAD_DIM, M_HEAD_DIM), F32),
                          jnp.zeros((bg, M_HEADS, M_HEAD_DIM), F32),
                          jnp.zeros((bg, SUBLANES, LANES), F32))
            zero_conv = jnp.zeros((bg, SUBLANES, 2 * M_WIDTH), F32)
            xg[g], sp, last_idx = _layer(xg[g], mod_g, False, bg, tp, tp, lw, bias_p, None, None, None,
                                         zero_conv, zero_state)
            sp_groups.append(sp)
        st_p.append([jnp.concatenate([sp[i] for sp in sp_groups], axis=0) for i in range(6)])
        state_s = (state_C[l].astype(F32), state_n[l].astype(F32),
                   jnp.broadcast_to(jnp.pad(state_m[l].astype(F32), ((0, 0), (0, SUBLANES - M_HEADS)))[:, :, None],
                                    (bs, SUBLANES, LANES)))
        kv_cache = (cache_k[l].reshape(bs, WINDOW, A_KV_WIDTH), cache_v[l].reshape(bs, WINDOW, A_KV_WIDTH))
        xs, ss, _ = _layer(xs, mod_s, True, bs, SAMPLE_PAD, ts, lw, None, bias_c, bias_n, kv_cache,
                           jnp.pad(state_conv[l].astype(F32), halo_pad), state_s, after=last_idx)
        st_s.append(ss)

    gfin = norm_final.reshape(1, d)
    y_prompt = jnp.concatenate([_final_call(x, gfin).reshape(bg, tp, d) for x, bg in zip(xg, sizes)], axis=0)
    y_sample = _final_call(xs, gfin).reshape(bs, SAMPLE_PAD, d)[:, :ts]
    outs_p = [jnp.stack([s[i] for s in st_p]) for i in range(6)]
    outs_s = [jnp.stack([s[i] for s in st_s]) for i in range(6)]
    return (y_prompt, y_sample, *outs_p, *outs_s)
```

```python
import functools
import math

import numpy as np
import jax
import jax.numpy as jnp
from jax import lax
from jax.experimental import pallas as pl
from jax.experimental.pallas import tpu as pltpu
from jax.experimental.pallas import tpu_sc as plsc

F32 = jnp.float32
BF16 = jnp.bfloat16
HIGHEST = lax.Precision.HIGHEST

D_MODEL = 1024
A_HEADS = 8
A_KV_HEADS = 2
A_GROUP = A_HEADS // A_KV_HEADS
A_HEAD_DIM = 64
A_WIDTH = A_HEADS * A_HEAD_DIM
A_KV_WIDTH = A_KV_HEADS * A_HEAD_DIM
WINDOW = 128
ATT_SCALE = A_HEAD_DIM ** -0.5
N_BUCKETS = 32
MAX_DISTANCE = WINDOW
M_HEADS = 4
M_HEAD_DIM = 128
M_WIDTH = M_HEADS * M_HEAD_DIM
CONV_W = 4
M_CHUNK = 64
N_KEYS = 128
P_HEADS = 8
P_TOPK = 16
P_KEY_DIM = 256
P_HALF = P_KEY_DIM // 2
EPS = 1e-6
NEG_INF = -1e30

LANES = 128
SUBLANES = 8
SAMPLE_PAD = SUBLANES
VMEM_LIMIT = 48 * 1024 * 1024
PROMPT_GROUPS = 9

NT_DIMS = (((1,), (1,)), ((), ()))
TN_DIMS = (((0,), (0,)), ((), ()))


def _cparams(*sem):
    return pltpu.CompilerParams(dimension_semantics=sem, vmem_limit_bytes=VMEM_LIMIT)


def _bdot(a, b):
    return jnp.dot(a.astype(BF16), b.astype(BF16), preferred_element_type=F32)


def _bdot_nt(a, b):
    return lax.dot_general(a.astype(BF16), b.astype(BF16), NT_DIMS, preferred_element_type=F32)


def _sigmoid(x):
    return 1.0 / (1.0 + jnp.exp(-x))


def _log_sigmoid(x):
    return jnp.minimum(x, 0.0) - jnp.log1p(jnp.exp(-jnp.abs(x)))


def _ada_kernel(c_ref, w_ref, b_ref, o_ref):
    c = c_ref[...]
    s = c * _sigmoid(c)
    o_ref[...] = jnp.dot(s, w_ref[...], precision=HIGHEST, preferred_element_type=F32) + b_ref[...]


def _ada_call(c_all, w_ada, b_ada):
    depth, d, n6 = w_ada.shape
    rows = c_all.shape[0]
    bn = 1024
    return pl.pallas_call(
        _ada_kernel,
        grid=(depth, n6 // bn),
        in_specs=[
            pl.BlockSpec((rows, d), lambda l, j: (0, 0)),
            pl.BlockSpec((None, d, bn), lambda l, j: (l, 0, j)),
            pl.BlockSpec((None, 1, bn), lambda l, j: (l, 0, j)),
        ],
        out_specs=pl.BlockSpec((None, rows, bn), lambda l, j: (l, 0, j)),
        out_shape=jax.ShapeDtypeStruct((depth, rows, n6), F32),
        compiler_params=_cparams("parallel", "parallel"),
        name="ada_mod",
    )(c_all, w_ada, b_ada.reshape(depth, 1, n6))


def _mod_spec(per_token, tm, tokens_per_req):
    if per_token:
        return pl.BlockSpec((tm, D_MODEL), lambda i: (i, 0))
    tiles = tokens_per_req // tm
    return pl.BlockSpec((None, 1, D_MODEL), lambda i: (i // tiles, 0, 0))


def _in_kernel(x_ref, sc_ref, sh_ref, g_ref, wa_ref, wqk_ref, wv_ref, wo_ref, wg_ref, gb_ref,
               qkv_ref, qkm_ref, v_ref, o_ref, gc_ref):
    x = x_ref[...]
    y = x * lax.rsqrt(jnp.mean(x * x, axis=-1, keepdims=True) + EPS) * g_ref[...]
    h = (y * (1.0 + sc_ref[...]) + sh_ref[...]).astype(BF16)
    qkv_ref[...] = jnp.dot(h, wa_ref[...], preferred_element_type=F32)
    qkm_ref[...] = jnp.dot(h, wqk_ref[...], preferred_element_type=F32)
    v_ref[...] = jnp.dot(h, wv_ref[...], preferred_element_type=F32)
    o_ref[...] = jnp.dot(h, wo_ref[...], preferred_element_type=F32)
    g = jnp.dot(h, wg_ref[...], preferred_element_type=F32) + gb_ref[...]
    lane = lax.broadcasted_iota(jnp.int32, g.shape, 1)
    gc_ref[...] = jnp.where(lane < M_HEADS, g, jnp.where(lane < 2 * M_HEADS, _log_sigmoid(g), 0.0))


def _in_call(x, sc, sh, gnorm, wa, wqk, wv, wo, wg, gb, per_token, tokens_per_req):
    n = x.shape[0]
    tm = min(512, n if per_token else tokens_per_req)
    mod = _mod_spec(per_token, tm, tokens_per_req)
    full = lambda shape: pl.BlockSpec(shape, lambda i: (0,) * len(shape))
    row = lambda w: pl.BlockSpec((tm, w), lambda i: (i, 0))
    return pl.pallas_call(
        _in_kernel,
        grid=(n // tm,),
        in_specs=[row(D_MODEL), mod, mod, full((1, D_MODEL)), full(wa.shape), full(wqk.shape),
                  full(wv.shape), full(wo.shape), full(wg.shape), full((1, LANES))],
        out_specs=[row(wa.shape[1]), row(wqk.shape[1]), row(wv.shape[1]), row(wo.shape[1]), row(LANES)],
        out_shape=[jax.ShapeDtypeStruct((n, w), F32)
                   for w in (wa.shape[1], wqk.shape[1], wv.shape[1], wo.shape[1], LANES)],
        compiler_params=_cparams("parallel"),
        name="in_proj",
    )(x, sc, sh, gnorm, wa, wqk, wv, wo, wg, gb)


def _t5_bucket_np(dist):
    n = np.maximum(dist, 0)
    max_exact = N_BUCKETS // 2
    nf = np.maximum(n, 1).astype(np.float64)
    large = max_exact + (np.log(nf / max_exact) / math.log(MAX_DISTANCE / max_exact)
                         * (N_BUCKETS - max_exact)).astype(np.int32)
    return np.where(n < max_exact, n, np.minimum(large, N_BUCKETS - 1)).astype(np.int32)


def _bias_kernel(bucket_ref, rel_ref, o_ref):
    bucket = bucket_ref[...]
    for h in range(A_HEADS):
        acc = jnp.zeros(bucket.shape, F32)
        for b in range(N_BUCKETS):
            acc = jnp.where(bucket == b, rel_ref[b, h], acc)
        o_ref[h] = acc


def _bias_call(rel_bias, dist):
    bucket = jnp.asarray(_t5_bucket_np(dist))
    nq, nk = dist.shape
    return pl.pallas_call(
        _bias_kernel,
        in_specs=[pl.BlockSpec((nq, nk), lambda: (0, 0)),
                  pl.BlockSpec(memory_space=pltpu.SMEM)],
        out_specs=pl.BlockSpec((A_HEADS, nq, nk), lambda: (0, 0, 0)),
        out_shape=jax.ShapeDtypeStruct((A_HEADS, nq, nk), F32),
        name="t5_bias",
    )(bucket, rel_bias)


def _attn_p_kernel(q_ref, kp_ref, kc_ref, vp_ref, vc_ref, bias_ref, sink_ref, o_ref):
    i = pl.program_id(1)
    qi = lax.broadcasted_iota(jnp.int32, (WINDOW, WINDOW), 0)
    kj = lax.broadcasted_iota(jnp.int32, (WINDOW, WINDOW), 1)
    valid_prev = jnp.logical_and(kj > qi, i > 0)
    valid_cur = kj <= qi
    q = q_ref[...]
    for h in range(A_HEADS):
        kv = h // A_GROUP
        qh = q[:, h * A_HEAD_DIM:(h + 1) * A_HEAD_DIM]
        sl = slice(kv * A_HEAD_DIM, (kv + 1) * A_HEAD_DIM)
        bias = bias_ref[h]
        sp = _bdot_nt(qh, kp_ref[:, sl]) * ATT_SCALE + bias[:, :WINDOW]
        sc = _bdot_nt(qh, kc_ref[:, sl]) * ATT_SCALE + bias[:, WINDOW:]
        sp = jnp.where(valid_prev, sp, NEG_INF)
        sc = jnp.where(valid_cur, sc, NEG_INF)
        sink = sink_ref[0, h]
        mx = jnp.maximum(jnp.maximum(jnp.max(sp, axis=-1, keepdims=True),
                                     jnp.max(sc, axis=-1, keepdims=True)), sink)
        ep = jnp.exp(sp - mx)
        ec = jnp.exp(sc - mx)
        den = (jnp.sum(ep, axis=-1, keepdims=True) + jnp.sum(ec, axis=-1, keepdims=True)
               + jnp.exp(sink - mx))
        o = _bdot(ep / den, vp_ref[:, sl]) + _bdot(ec / den, vc_ref[:, sl])
        o_ref[:, h * A_HEAD_DIM:(h + 1) * A_HEAD_DIM] = o.astype(o_ref.dtype)


def _attn_p_call(qkv, bias, sinks, batch, seq):
    nb = seq // WINDOW
    n = batch * seq
    kcol = A_WIDTH // A_KV_WIDTH
    vcol = kcol + 1
    cur = lambda col: pl.BlockSpec((WINDOW, A_KV_WIDTH), lambda b, i: (b * nb + i, col))
    prev = lambda col: pl.BlockSpec((WINDOW, A_KV_WIDTH),
                                    lambda b, i: (b * nb + jnp.maximum(i - 1, 0), col))
    return pl.pallas_call(
        _attn_p_kernel,
        grid=(batch, nb),
        in_specs=[pl.BlockSpec((WINDOW, A_WIDTH), lambda b, i: (b * nb + i, 0)),
                  prev(kcol), cur(kcol), prev(vcol), cur(vcol),
                  pl.BlockSpec((A_HEADS, WINDOW, 2 * WINDOW), lambda b, i: (0, 0, 0)),
                  pl.BlockSpec(memory_space=pltpu.SMEM)],
        out_specs=pl.BlockSpec((WINDOW, A_WIDTH), lambda b, i: (b * nb + i, 0)),
        out_shape=jax.ShapeDtypeStruct((n, A_WIDTH), BF16),
        compiler_params=_cparams("parallel", "parallel"),
        name="swa_prompt",
    )(qkv, qkv, qkv, qkv, qkv, bias, sinks.reshape(1, A_HEADS))


def _attn_s_kernel(n_new, qkv_ref, ck_ref, cv_ref, bc_ref, bn_ref, sink_ref,
                   o_ref, nk_ref, nv_ref, kk_s, vv_s):
    qkv = qkv_ref[...]
    knew = qkv[:, A_WIDTH:A_WIDTH + A_KV_WIDTH]
    vnew = qkv[:, A_WIDTH + A_KV_WIDTH:A_WIDTH + 2 * A_KV_WIDTH]
    ck = ck_ref[...]
    cv = cv_ref[...]
    rows = A_GROUP * SAMPLE_PAD
    qi = lax.broadcasted_iota(jnp.int32, (rows, WINDOW), 0) % SAMPLE_PAD
    kj = lax.broadcasted_iota(jnp.int32, (rows, WINDOW), 1)
    valid_c = kj > qi
    rcol = lax.broadcasted_iota(jnp.int32, (rows, 1), 0)
    qcol = rcol % SAMPLE_PAD
    for kv in range(A_KV_HEADS):
        heads = range(kv * A_GROUP, (kv + 1) * A_GROUP)
        sl = slice(kv * A_HEAD_DIM, (kv + 1) * A_HEAD_DIM)
        qs = jnp.concatenate([qkv[:, h * A_HEAD_DIM:(h + 1) * A_HEAD_DIM] for h in heads], axis=0)
        bias_c = jnp.concatenate([bc_ref[h] for h in heads], axis=0)
        bias_n = jnp.concatenate([bn_ref[h] for h in heads], axis=0)
        sink = jnp.zeros((rows, 1), F32)
        for g, h in enumerate(heads):
            sink = jnp.where(rcol // SAMPLE_PAD == g, sink_ref[0, h], sink)
        s_c = _bdot_nt(qs, ck[:, sl]) * ATT_SCALE + bias_c
        s_c = jnp.where(valid_c, s_c, NEG_INF)
        s_n = []
        for j in range(n_new):
            sj = jnp.sum(qs * knew[j:j + 1, sl], axis=-1, keepdims=True) * ATT_SCALE + bias_n[:, j:j + 1]
            s_n.append(jnp.where(qcol >= j, sj, NEG_INF))
        mx = jnp.maximum(jnp.max(s_c, axis=-1, keepdims=True), sink)
        for sj in s_n:
            mx = jnp.maximum(mx, sj)
        e_c = jnp.exp(s_c - mx)
        den = jnp.sum(e_c, axis=-1, keepdims=True) + jnp.exp(sink - mx)
        o = _bdot(e_c, cv[:, sl])
        for j, sj in enumerate(s_n):
            ej = jnp.exp(sj - mx)
            den = den + ej
            o = o + ej * vnew[j:j + 1, sl]
        o = o / den
        for g, h in enumerate(heads):
            o_ref[:, h * A_HEAD_DIM:(h + 1) * A_HEAD_DIM] = o[g * SAMPLE_PAD:(g + 1) * SAMPLE_PAD, :]
    kk_s[0:WINDOW, :] = ck
    kk_s[WINDOW:WINDOW + SAMPLE_PAD, :] = knew
    vv_s[0:WINDOW, :] = cv
    vv_s[WINDOW:WINDOW + SAMPLE_PAD, :] = vnew
    nk_ref[...] = kk_s[n_new:n_new + WINDOW, :]
    nv_ref[...] = vv_s[n_new:n_new + WINDOW, :]


def _attn_s_call(qkv, ck, cv, bias_c, bias_n, sinks, n_new):
    nreq = ck.shape[0]
    wq = qkv.shape[1]
    full3 = lambda shape: pl.BlockSpec(shape, lambda b: (0, 0, 0))
    cache = pl.BlockSpec((None, WINDOW, A_KV_WIDTH), lambda b: (b, 0, 0))
    return pl.pallas_call(
        functools.partial(_attn_s_kernel, n_new),
        grid=(nreq,),
        in_specs=[pl.BlockSpec((SAMPLE_PAD, wq), lambda b: (b, 0)), cache, cache,
                  full3(bias_c.shape), full3(bias_n.shape),
                  pl.BlockSpec(memory_space=pltpu.SMEM)],
        out_specs=[pl.BlockSpec((SAMPLE_PAD, A_WIDTH), lambda b: (b, 0)), cache, cache],
        out_shape=[jax.ShapeDtypeStruct((nreq * SAMPLE_PAD, A_WIDTH), F32),
                   jax.ShapeDtypeStruct(ck.shape, F32), jax.ShapeDtypeStruct(cv.shape, F32)],
        scratch_shapes=[pltpu.VMEM((WINDOW + SAMPLE_PAD, A_KV_WIDTH), F32),
                        pltpu.VMEM((WINDOW + SAMPLE_PAD, A_KV_WIDTH), F32)],
        compiler_params=_cparams("parallel"),
        name="swa_sample",
    )(qkv, ck, cv, bias_c, bias_n, sinks.reshape(1, A_HEADS))


def _mlstm_kernel(chunk, t_valid, qk_ref, v_ref, og_ref, gc_ref, cw_ref, cb_ref, mn_ref,
                  conv0_ref, c0_ref, n0_ref, m0_ref,
                  out_ref, cout_ref, nout_ref, mout_ref,
                  xp_s, c_s, n_s, m_s):
    step = pl.program_id(1)
    halo = SUBLANES

    @pl.when(step == 0)
    def _():
        xp_s[0:halo, :] = conv0_ref[...]
        c_s[...] = c0_ref[...]
        n_s[...] = n0_ref[...]
        m_s[...] = m0_ref[...]

    xp_s[halo:halo + chunk, :] = qk_ref[...]
    cw = cw_ref[...]
    y = cb_ref[...]
    for i in range(CONV_W):
        off = halo - (CONV_W - 1) + i
        y = y + xp_s[off:off + chunk, :] * cw[i:i + 1, :]
    xp_s[0:halo, :] = xp_s[chunk:chunk + halo, :]
    y = y * _sigmoid(y)
    q_all = y[:, :M_WIDTH]
    k_all = y[:, M_WIDTH:] * (M_HEAD_DIM ** -0.5)

    g = gc_ref[...]
    if t_valid < chunk:
        row = lax.broadcasted_iota(jnp.int32, g.shape, 0)
        lane = lax.broadcasted_iota(jnp.int32, g.shape, 1)
        g = jnp.where(row < t_valid, g, jnp.where(lane < M_HEADS, NEG_INF, 0.0))
    tr = lax.broadcasted_iota(jnp.int32, (chunk, chunk), 0)
    tc = lax.broadcasted_iota(jnp.int32, (chunk, chunk), 1)
    causal = tr >= tc
    tri = causal.astype(F32)
    bcol = jnp.dot(tri, g, precision=HIGHEST, preferred_element_type=F32)
    er = lax.broadcasted_iota(jnp.int32, (SUBLANES, LANES), 0)
    ec = lax.broadcasted_iota(jnp.int32, (SUBLANES, LANES), 1)
    eye = (er == ec).astype(F32)
    g_rows = lax.dot_general(eye, g, NT_DIMS, precision=HIGHEST, preferred_element_type=F32)
    b_rows = lax.dot_general(eye, bcol, NT_DIMS, precision=HIGHEST, preferred_element_type=F32)

    for h in range(M_HEADS):
        hs = slice(h * M_HEAD_DIM, (h + 1) * M_HEAD_DIM)
        b_c = bcol[:, M_HEADS + h:M_HEADS + h + 1]
        ig_c = g[:, h:h + 1]
        b_r = b_rows[M_HEADS + h:M_HEADS + h + 1, :]
        ig_r = g_rows[h:h + 1, :]
        m_prev = m_s[h:h + 1, 0:1]
        logw = jnp.where(causal, b_c - b_r + ig_r, -jnp.inf)
        inter = b_c + m_prev
        m_t = jnp.maximum(inter, jnp.max(logw, axis=-1, keepdims=True))
        w = jnp.exp(logw - m_t)
        a = jnp.exp(inter - m_t)
        q = q_all[:, hs]
        k = k_all[:, hs]
        v = v_ref[:, hs]
        cmat = c_s[h]
        nvec = n_s[h:h + 1, :]
        wqk = w * _bdot_nt(q, k)
        num = _bdot(wqk, v) + a * _bdot_nt(q, cmat)
        den = jnp.sum(wqk, axis=-1, keepdims=True) + a * jnp.sum(q * nvec, axis=-1, keepdims=True)
        hh = num / jnp.maximum(jnp.abs(den), jnp.exp(-m_t))
        m_new = m_t[chunk - 1:chunk, :]
        b_last = b_c[chunk - 1:chunk, :]
        wl = jnp.exp(b_last - b_c + ig_c - m_new)
        al = jnp.exp(b_last + m_prev - m_new)
        c_s[h] = al * cmat + lax.dot_general((v * wl).astype(BF16), k.astype(BF16), TN_DIMS,
                                             preferred_element_type=F32)
        n_s[h:h + 1, :] = al * nvec + jnp.sum(wl * k, axis=0, keepdims=True)
        m_s[h:h + 1, :] = jnp.broadcast_to(m_new, (1, LANES))
        hn = hh * lax.rsqrt(jnp.mean(hh * hh, axis=-1, keepdims=True) + EPS) * mn_ref[:, hs]
        out_ref[:, hs] = (_sigmoid(og_ref[:, hs]) * hn).astype(out_ref.dtype)

    @pl.when(step == pl.num_programs(1) - 1)
    def _():
        cout_ref[...] = c_s[...]
        nout_ref[...] = n_s[...]
        mout_ref[...] = m_s[...]


def _mlstm_call(qk, v, og, gc, conv_w, conv_b, m_norm, conv0, c0, n0, m0, batch, seq, chunk, t_valid):
    nc = seq // chunk
    n = batch * seq
    row = lambda w: pl.BlockSpec((chunk, w), lambda b, c: (b * nc + c, 0))
    full2 = lambda shape: pl.BlockSpec(shape, lambda b, c: (0, 0))
    per_b = lambda shape: pl.BlockSpec((None,) + shape, lambda b, c: (b,) + (0,) * len(shape))
    dh = M_HEAD_DIM
    return pl.pallas_call(
        functools.partial(_mlstm_kernel, chunk, t_valid),
        grid=(batch, nc),
        in_specs=[row(2 * M_WIDTH), row(M_WIDTH), row(M_WIDTH), row(LANES),
                  full2((CONV_W, 2 * M_WIDTH)), full2((1, 2 * M_WIDTH)), full2((1, M_WIDTH)),
                  per_b((SUBLANES, 2 * M_WIDTH)), per_b((M_HEADS, dh, dh)), per_b((M_HEADS, dh)),
                  per_b((SUBLANES, LANES))],
        out_specs=[row(M_WIDTH), per_b((M_HEADS, dh, dh)), per_b((M_HEADS, dh)), per_b((SUBLANES, LANES))],
        out_shape=[jax.ShapeDtypeStruct((n, M_WIDTH), BF16 if chunk % (2 * SUBLANES) == 0 else F32),
                   jax.ShapeDtypeStruct((batch, M_HEADS, dh, dh), F32),
                   jax.ShapeDtypeStruct((batch, M_HEADS, dh), F32),
                   jax.ShapeDtypeStruct((batch, SUBLANES, LANES), F32)],
        scratch_shapes=[pltpu.VMEM((SUBLANES + chunk, 2 * M_WIDTH), F32),
                        pltpu.VMEM((M_HEADS, dh, dh), F32),
                        pltpu.VMEM((M_HEADS, dh), F32),
                        pltpu.VMEM((SUBLANES, LANES), F32)],
        compiler_params=_cparams("parallel", "arbitrary"),
        name="mlstm",
    )(qk, v, og, gc, conv_w, conv_b, m_norm, conv0, c0, n0, m0)


def _pack_words(x):
    bits = pltpu.bitcast(x.astype(BF16).astype(F32), jnp.uint32)
    half = x.shape[1] // 2
    lo = lax.shift_right_logical(bits[:, :half], jnp.uint32(16))
    hi = bits[:, half:] & jnp.uint32(0xFFFF0000)
    return pltpu.bitcast(lo | hi, jnp.int32)


def _out_kernel(att_ref, mo_ref, x_ref, g1_ref, sc_ref, sh_ref, gn_ref, wa_ref, wm_ref, wq_ref,
                xo_ref, h2_ref, qp_ref):
    mix = (jnp.dot(att_ref[...].astype(BF16), wa_ref[...], preferred_element_type=F32)
           + jnp.dot(mo_ref[...].astype(BF16), wm_ref[...], preferred_element_type=F32))
    x = x_ref[...] + g1_ref[...] * mix
    xo_ref[...] = x
    y = x * lax.rsqrt(jnp.mean(x * x, axis=-1, keepdims=True) + EPS) * gn_ref[...]
    h2 = y * (1.0 + sc_ref[...]) + sh_ref[...]
    qp_ref[...] = jnp.dot(h2.astype(BF16), wq_ref[...], preferred_element_type=F32).astype(qp_ref.dtype)
    h2_ref[...] = _pack_words(h2)


def _out_call(att, mo, x, g1, sc, sh, gnorm, wa, wm, wq, per_token, tokens_per_req):
    n = x.shape[0]
    tm = min(256, n if per_token else tokens_per_req)
    mod = _mod_spec(per_token, tm, tokens_per_req)
    full = lambda shape: pl.BlockSpec(shape, lambda i: (0,) * len(shape))
    row = lambda w: pl.BlockSpec((tm, w), lambda i: (i, 0))
    nq = wq.shape[1]
    return pl.pallas_call(
        _out_kernel,
        grid=(n // tm,),
        in_specs=[row(A_WIDTH), row(M_WIDTH), row(D_MODEL), mod, mod, mod, full((1, D_MODEL)),
                  full(wa.shape), full(wm.shape), full(wq.shape)],
        out_specs=[row(D_MODEL), row(D_MODEL // 2), row(nq)],
        out_shape=[jax.ShapeDtypeStruct((n, D_MODEL), F32), jax.ShapeDtypeStruct((n, D_MODEL // 2), jnp.int32),
                   jax.ShapeDtypeStruct((n, nq), BF16)],
        compiler_params=_cparams("parallel"),
        name="out_proj",
    )(att, mo, x, g1, sc, sh, gnorm, wa, wm, wq)


def _pk_cells():
    return [(a, b) for a in range(P_TOPK) for b in range(P_TOPK) if (a + 1) * (b + 1) <= P_TOPK]


PK_CELL_ROWS = 64


def _pk_expand_mats():
    cells = _pk_cells()
    e0 = np.zeros((PK_CELL_ROWS, LANES), np.float32)
    e1 = np.zeros((PK_CELL_ROWS, LANES), np.float32)
    for j, (a, b) in enumerate(cells):
        e0[j, a] = 1.0
        e1[j, b] = 1.0
    return e0, e1, len(cells)


def _top_rows(s, rowf, rounds):
    n_rows = s.shape[0]
    vals, idxs = [], []
    for _ in range(rounds):
        m = jnp.max(s, axis=0, keepdims=True)
        i = jnp.min(jnp.where(s == m, rowf, float(n_rows)), axis=0, keepdims=True)
        vals.append(m)
        idxs.append(i)
        s = jnp.where(rowf == i, -jnp.inf, s)
    return jnp.concatenate(vals, axis=0), jnp.concatenate(idxs, axis=0)


def _select_kernel(n_cells, qp_ref, keys_ref, e0_ref, e1_ref, idx_ref, gw_ref, idx_s, gw_s):
    tm = qp_ref.shape[0]
    keyf = lax.broadcasted_iota(jnp.int32, (N_KEYS, tm), 0).astype(F32)
    cellf = lax.broadcasted_iota(jnp.int32, (PK_CELL_ROWS, tm), 0).astype(F32)
    e0 = e0_ref[...]
    e1 = e1_ref[...]
    pad = jnp.zeros((LANES - P_TOPK, tm), F32)

    def head(h, carry):
        h = jnp.asarray(h, jnp.int32)
        sub = []
        for c in range(2):
            col = pl.multiple_of((h * 2 + c) * P_HALF, P_HALF)
            s = _bdot_nt(keys_ref[h, c], qp_ref[:, pl.ds(col, P_HALF)])
            sub.append(_top_rows(s, keyf, P_TOPK))
        (v0, i0), (v1, i1) = sub
        expand = lambda e, x: jnp.dot(e, jnp.concatenate([x, pad], axis=0), precision=HIGHEST,
                                      preferred_element_type=F32)
        cand = expand(e0, v0) + expand(e1, v1)
        cidx = expand(e0, i0 * float(N_KEYS)) + expand(e1, i1)
        cand = jnp.where(cellf < n_cells, cand, -jnp.inf)
        best, eidx = [], []
        for _ in range(P_TOPK):
            m = jnp.max(cand, axis=0, keepdims=True)
            j = jnp.min(jnp.where(cand == m, cellf, float(PK_CELL_ROWS)), axis=0, keepdims=True)
            hit = cellf == j
            eidx.append(jnp.max(jnp.where(hit, cidx, -1.0), axis=0, keepdims=True))
            best.append(m)
            cand = jnp.where(hit, -jnp.inf, cand)
        best = jnp.concatenate(best, axis=0)
        e = jnp.exp(best - best[0:1, :])
        row0 = pl.multiple_of(h * P_TOPK, P_TOPK)
        gw_s[pl.ds(row0, P_TOPK), :] = e / jnp.sum(e, axis=0, keepdims=True)
        idx_s[pl.ds(row0, P_TOPK), :] = jnp.concatenate(eidx, axis=0)
        return carry

    lax.fori_loop(0, P_HEADS, head, 0)
    idx_ref[...] = idx_s[...].T.astype(jnp.int32)
    gw_ref[...] = gw_s[...].T


def _select_call(qp, keys_bf16):
    n = qp.shape[0]
    tm = min(LANES, n)
    e0, e1, n_cells = _pk_expand_mats()
    full = lambda shape: pl.BlockSpec(shape, lambda i: (0,) * len(shape))
    return pl.pallas_call(
        functools.partial(_select_kernel, n_cells),
        grid=(n // tm,),
        in_specs=[pl.BlockSpec((tm, qp.shape[1]), lambda i: (i, 0)), full(keys_bf16.shape),
                  full((PK_CELL_ROWS, LANES)), full((PK_CELL_ROWS, LANES))],
        out_specs=[pl.BlockSpec((tm, LANES), lambda i: (i, 0)), pl.BlockSpec((tm, LANES), lambda i: (i, 0))],
        out_shape=[jax.ShapeDtypeStruct((n, LANES), jnp.int32), jax.ShapeDtypeStruct((n, LANES), F32)],
        scratch_shapes=[pltpu.VMEM((P_HEADS * P_TOPK, tm), F32), pltpu.VMEM((P_HEADS * P_TOPK, tm), F32)],
        compiler_params=_cparams("parallel"),
        name="peer_select",
    )(qp, keys_bf16, jnp.asarray(e0), jnp.asarray(e1))


N_SEL = P_HEADS * P_TOPK

SC_CORES = 2
SC_SUBCORES = 16
SC_LANES = 16
SC_TOK_BLOCK = 8
SC_ROWS = 32
SC_ROW_BLOCK = 16
SC_ACC_CHAINS = 4
SC_ACC_ROWS = 16
SC_NBUF = 4
N_WCOL = D_MODEL // (2 * SC_LANES)


def _pack_kernel(t_ref, o_ref):
    o_ref[...] = _pack_words(t_ref[...])


def _pack_table(tables, layer):
    _, e, dcol = tables.shape
    tm = min(1024, e)
    return pl.pallas_call(
        _pack_kernel,
        grid=(e // tm,),
        in_specs=[pl.BlockSpec((None, tm, dcol), lambda i: (layer, i, 0))],
        out_specs=pl.BlockSpec((tm, dcol // 2), lambda i: (i, 0)),
        out_shape=jax.ShapeDtypeStruct((e, dcol // 2), jnp.int32),
        compiler_params=_cparams("parallel"),
        name="pack_table",
    )(tables)


def _sc_gelu(x):
    z = math.sqrt(2.0 / math.pi) * (x + 0.044715 * (x * x * x))
    t = 1.0 - 2.0 / (jnp.exp(2.0 * z) + 1.0)
    return x * (0.5 * (1.0 + t))


def _sc_expert_body(tokens_per_worker, idx_hbm, gw_hbm, h2_hbm, u_hbm, v_hbm, y_hbm,
                    idx_v, gw_v, x_v, o_v, buf, coef_v, tr_v, sem, in_sem, out_sem):
    wid = lax.axis_index("s") * SC_CORES + lax.axis_index("c")
    base = wid * tokens_per_worker
    lane = lax.iota(jnp.int32, SC_LANES)
    n_gather = N_SEL // SC_ROWS
    zero = jnp.zeros((SC_LANES,), F32)

    n_steps = 2 * n_gather
    assert n_steps % SC_NBUF == 0

    def gather(p, tt, i):
        table = u_hbm if i < n_gather else v_hbm
        j = i % n_gather
        slot = i % SC_NBUF
        return pltpu.make_async_copy(table.at[idx_v.at[p, tt, pl.ds(j * SC_ROWS, SC_ROWS)]], buf.at[slot],
                                     sem.at[slot])

    def unpack(w):
        lo = lax.bitcast_convert_type(lax.shift_left(w, jnp.full((SC_LANES,), 16, jnp.int32)), F32)
        hi = lax.bitcast_convert_type(w & jnp.full((SC_LANES,), -65536, jnp.int32), F32)
        return lo, hi

    def packed(w):
        return plsc.bitcast(w, BF16)

    def unpack_sum(s):
        return unpack(plsc.bitcast(s, jnp.int32))

    def act_chunk(p, tt, j, slot):
        @pl.loop(0, SC_ROWS // SC_LANES)
        def _(half):
            for rb in range(SC_LANES // SC_ROW_BLOCK):
                r0 = half * SC_LANES + rb * SC_ROW_BLOCK

                def col(c, accs):
                    w0 = pl.multiple_of(c * (4 * SC_LANES), 4 * SC_LANES)
                    xs = [packed(x_v[p, tt, pl.ds(w0 + q * SC_LANES, SC_LANES)]) for q in range(4)]
                    out = []
                    for r, a in enumerate(accs):
                        us = [packed(buf[slot, r0 + r, pl.ds(w0 + q * SC_LANES, SC_LANES)]) for q in range(4)]
                        lo, hi = unpack_sum((us[0] * xs[0] + us[1] * xs[1]) + (us[2] * xs[2] + us[3] * xs[3]))
                        out.append(a + lo + hi)
                    return tuple(out)

                accs = lax.fori_loop(0, N_WCOL // 4, col, (zero,) * SC_ROW_BLOCK)
                for r in range(SC_ROW_BLOCK):
                    tr_v[pl.ds((rb * SC_ROW_BLOCK + r) * SC_LANES, SC_LANES)] = accs[r]
            tot = zero
            for jj in range(SC_LANES):
                tot = tot + plsc.load_gather(tr_v, [lane * SC_LANES + jj])
            k0 = pl.multiple_of(j * SC_ROWS + half * SC_LANES, SC_LANES)
            coef_v[pl.ds(k0, SC_LANES)] = gw_v[p, tt, pl.ds(k0, SC_LANES)] * _sc_gelu(tot)

    def acc_chunk(p, tt, j, slot, first):
        def tree_sum(parts):
            while len(parts) > 1:
                parts = [parts[i] + parts[i + 1] for i in range(0, len(parts), 2)]
            return parts[0]

        for rb in range(SC_ROWS // SC_ACC_ROWS):
            rows = list(range(rb * SC_ACC_ROWS, (rb + 1) * SC_ACC_ROWS))
            splat = {}
            for r in rows:
                c16 = plsc.load_gather(coef_v, [jnp.full((SC_LANES,), j * SC_ROWS + r, jnp.int32)])
                splat[r] = plsc.pack(c16, c16, format=plsc.PackFormat.INTERLEAVED)
            fresh = first and rb == 0

            @plsc.parallel_loop(0, N_WCOL)
            def _(c):
                woff = pl.multiple_of(c * SC_LANES, SC_LANES)
                hoff = pl.multiple_of(c * SC_LANES + D_MODEL // 2, SC_LANES)
                pa, pb = [], []
                for n in range(0, SC_ACC_ROWS, SC_ACC_CHAINS):
                    prod = [splat[r] * packed(buf[slot, r, pl.ds(woff, SC_LANES)])
                            for r in rows[n:n + SC_ACC_CHAINS]]
                    lo, hi = unpack_sum(tree_sum(prod))
                    pa.append(lo)
                    pb.append(hi)
                sa, sb = tree_sum(pa), tree_sum(pb)
                if not fresh:
                    sa = sa + o_v[p, tt, pl.ds(woff, SC_LANES)]
                    sb = sb + o_v[p, tt, pl.ds(hoff, SC_LANES)]
                o_v[p, tt, pl.ds(woff, SC_LANES)] = sa
                o_v[p, tt, pl.ds(hoff, SC_LANES)] = sb

    n_blocks = tokens_per_worker // SC_TOK_BLOCK

    def block_start(blk):
        return pl.multiple_of(base + blk * SC_TOK_BLOCK, SC_TOK_BLOCK)

    def in_copies(blk, p):
        rows = pl.ds(block_start(blk), SC_TOK_BLOCK)
        return [pltpu.make_async_copy(idx_hbm.at[rows], idx_v.at[p], in_sem.at[p]),
                pltpu.make_async_copy(gw_hbm.at[rows], gw_v.at[p], in_sem.at[p]),
                pltpu.make_async_copy(h2_hbm.at[rows], x_v.at[p], in_sem.at[p])]

    def out_copy(blk, p):
        return pltpu.make_async_copy(o_v.at[p], y_hbm.at[pl.ds(block_start(blk), SC_TOK_BLOCK)], out_sem.at[p])

    ahead = SC_NBUF - 1
    for cp in in_copies(0, 0):
        cp.start()
    for cp in in_copies(0, 0):
        cp.wait()
    for cp in in_copies(1, 1):
        cp.start()
    for i in range(ahead):
        gather(0, 0, i).start()

    @pl.loop(0, n_blocks)
    def _(blk):
        p = lax.rem(blk, 2)

        @pl.when(blk >= 2)
        def _():
            out_copy(blk - 2, p).wait()

        @pl.loop(0, SC_TOK_BLOCK)
        def _(tt):
            for i in range(n_steps):
                if i + ahead < n_steps:
                    gather(p, tt, i + ahead).start()
                else:
                    nxt = i + ahead - n_steps

                    @pl.when(tt + 1 < SC_TOK_BLOCK)
                    def _():
                        gather(p, tt + 1, nxt).start()

                    @pl.when(jnp.logical_and(tt + 1 == SC_TOK_BLOCK, blk + 1 < n_blocks))
                    def _():
                        if nxt == 0:
                            for cp in in_copies(blk + 1, 1 - p):
                                cp.wait()
                        gather(1 - p, 0, nxt).start()
                gather(p, tt, i).wait()
                if i < n_gather:
                    act_chunk(p, tt, i, i % SC_NBUF)
                else:
                    acc_chunk(p, tt, i - n_gather, i % SC_NBUF, i == n_gather)

        out_copy(blk, p).start()

        @pl.when(blk + 2 < n_blocks)
        def _():
            for cp in in_copies(blk + 2, p):
                cp.start()

    for blk in (n_blocks - 2, n_blocks - 1):
        out_copy(blk, blk % 2).wait()


def _sc_expert_call(idx, gw, xw, u, v):
    n = idx.shape[0]
    workers = SC_CORES * SC_SUBCORES
    assert n % (workers * SC_TOK_BLOCK) == 0 and n // (workers * SC_TOK_BLOCK) >= 2
    mesh = plsc.VectorSubcoreMesh(core_axis_name="c", subcore_axis_name="s")
    return pl.kernel(
        functools.partial(_sc_expert_body, n // workers),
        out_type=jax.ShapeDtypeStruct((n, D_MODEL), F32),
        mesh=mesh,
        scratch_types=[pltpu.VMEM((2, SC_TOK_BLOCK, N_SEL), jnp.int32),
                       pltpu.VMEM((2, SC_TOK_BLOCK, N_SEL), F32),
                       pltpu.VMEM((2, SC_TOK_BLOCK, D_MODEL // 2), jnp.int32),
                       pltpu.VMEM((2, SC_TOK_BLOCK, D_MODEL), F32),
                       pltpu.VMEM((SC_NBUF, SC_ROWS, D_MODEL // 2), jnp.int32),
                       pltpu.VMEM((N_SEL,), F32),
                       pltpu.VMEM((SC_LANES * SC_LANES,), F32),
                       pltpu.SemaphoreType.DMA((SC_NBUF,)), pltpu.SemaphoreType.DMA((2,)),
                       pltpu.SemaphoreType.DMA((2,))],
        compiler_params=pltpu.CompilerParams(needs_layout_passes=False),
        name="peer_experts_sc",
    )(idx, gw, xw, u, v)


def _resid_kernel(x_ref, y_ref, g_ref, o_ref):
    o_ref[...] = x_ref[...] + g_ref[...] * y_ref[...]


def _resid_call(x, y, g2, per_token, tokens_per_req):
    n = x.shape[0]
    tm = min(512, n if per_token else tokens_per_req)
    row = pl.BlockSpec((tm, D_MODEL), lambda i: (i, 0))
    return pl.pallas_call(
        _resid_kernel,
        grid=(n // tm,),
        in_specs=[row, row, _mod_spec(per_token, tm, tokens_per_req)],
        out_specs=row,
        out_shape=jax.ShapeDtypeStruct((n, D_MODEL), F32),
        compiler_params=_cparams("parallel"),
        name="peer_residual",
    )(x, y, g2)


def _final_kernel(x_ref, g_ref, o_ref):
    x = x_ref[...]
    o_ref[...] = x * lax.rsqrt(jnp.mean(x * x, axis=-1, keepdims=True) + EPS) * g_ref[...]


def _final_call(x, g):
    n = x.shape[0]
    tm = min(512, n)
    return pl.pallas_call(
        _final_kernel,
        grid=(n // tm,),
        in_specs=[pl.BlockSpec((tm, D_MODEL), lambda i: (i, 0)), pl.BlockSpec((1, D_MODEL), lambda i: (0, 0))],
        out_specs=pl.BlockSpec((tm, D_MODEL), lambda i: (i, 0)),
        out_shape=jax.ShapeDtypeStruct((n, D_MODEL), F32),
        compiler_params=_cparams("parallel"),
        name="final_norm",
    )(x, g)


def _split_w_in(w_in_l, gate_b_l):
    cuts = np.cumsum([A_WIDTH + 2 * A_KV_WIDTH, 2 * M_WIDTH, M_WIDTH, M_WIDTH]).tolist()
    wa = w_in_l[:, :cuts[0]].astype(BF16)
    wqk = w_in_l[:, cuts[0]:cuts[1]].astype(BF16)
    wv = w_in_l[:, cuts[1]:cuts[2]].astype(BF16)
    wo = w_in_l[:, cuts[2]:cuts[3]].astype(BF16)
    ng = 2 * M_HEADS
    wg = jnp.pad(w_in_l[:, cuts[3]:], ((0, 0), (0, LANES - ng))).astype(BF16)
    gb = jnp.pad(gate_b_l.astype(F32), (0, LANES - ng)).reshape(1, LANES)
    return wa, wqk, wv, wo, wg, gb


def _layer(x, mods, per_token, batch, seq, t_valid, lw, bias_p, bias_c, bias_n, kv_cache, conv0, state,
           after=None):
    (norm_mix, norm_ffn, w_in, conv_w, conv_b, gate_b, sinks, m_norm, w_out, peer_query, peer_keys,
     peer_u, peer_v) = lw
    if after is not None:
        x, _ = lax.optimization_barrier((x, after))
    sh1, sc1, g1, sh2, sc2, g2 = mods
    wa, wqk, wv, wo, wg, gb = _split_w_in(w_in, gate_b)
    qkv, qkm, vm, om, gc = _in_call(x, sc1, sh1, norm_mix.reshape(1, -1), wa, wqk, wv, wo, wg, gb,
                                    per_token, seq)
    if kv_cache is None:
        att = _attn_p_call(qkv, bias_p, sinks, batch, seq)
        kv3 = qkv.reshape(batch, seq, -1)
        new_k = kv3[:, seq - WINDOW:, A_WIDTH:A_WIDTH + A_KV_WIDTH]
        new_v = kv3[:, seq - WINDOW:, A_WIDTH + A_KV_WIDTH:]
        chunk = M_CHUNK
    else:
        att, new_k, new_v = _attn_s_call(qkv, kv_cache[0], kv_cache[1], bias_c, bias_n, sinks, t_valid)
        chunk = seq
    c0, n0, m0 = state
    mo, c_new, n_new, m_new = _mlstm_call(qkm, vm, om, gc, conv_w, conv_b.reshape(1, -1),
                                          m_norm.reshape(1, -1), conv0, c0, n0, m0,
                                          batch, seq, chunk, min(t_valid, chunk))
    new_conv = qkm.reshape(batch, seq, -1)[:, t_valid - (CONV_W - 1):t_valid]
    pad_rows = seq - t_valid
    if pad_rows:
        keep = lambda a: a.reshape(batch, seq, -1)[:, :t_valid].reshape(batch * t_valid, -1)
        att, mo, x, g1, sc2, sh2, g2 = (keep(a) for a in (att, mo, x, g1, sc2, sh2, g2))
    x_mid, xw, qp = _out_call(att, mo, x, g1, sc2, sh2, norm_ffn.reshape(1, -1),
                              w_out[:A_WIDTH].astype(BF16), w_out[A_WIDTH:].astype(BF16),
                              peer_query.astype(BF16), per_token, t_valid)
    idx, gw = _select_call(qp, peer_keys.astype(BF16))
    y = _sc_expert_call(idx, gw, xw, peer_u, peer_v)
    x_new = _resid_call(x_mid, y, g2, per_token, t_valid)
    if pad_rows:
        x_new = jnp.pad(x_new.reshape(batch, t_valid, -1), ((0, 0), (0, pad_rows), (0, 0))).reshape(batch * seq, -1)
    new_k = new_k.reshape(batch, WINDOW, A_KV_HEADS, A_HEAD_DIM)
    new_v = new_v.reshape(batch, WINDOW, A_KV_HEADS, A_HEAD_DIM)
    return x_new, (new_k, new_v, new_conv, c_new, n_new, m_new[:, :M_HEADS, 0]), idx


def _prompt_group_sizes(n_req):
    if n_req < PROMPT_GROUPS:
        return [1] * n_req
    mid, n_mid = n_req - 2, PROMPT_GROUPS - 2
    weights = [i + 2 for i in range(n_mid)]
    sizes = [max(1, mid * w // sum(weights)) for w in weights]
    for i in range(mid - sum(sizes)):
        sizes[n_mid - 1 - i % n_mid] += 1
    return [1] + sizes + [1]


def kernel(x_prompt, x_sample, c_prompt, c_sample, cache_k, cache_v, state_conv, state_C, state_n, state_m, rel_bias, w_ada, b_ada, norm_mix, norm_ffn, w_in, conv_w, conv_b, gate_b, attn_sinks, m_norm, w_out, peer_query, peer_keys, peer_u, peer_v, norm_final):
    depth = w_ada.shape[0]
    bp, tp, d = x_prompt.shape
    bs, ts, _ = x_sample.shape
    assert tp % WINDOW == 0 and tp % M_CHUNK == 0 and ts <= SAMPLE_PAD and ts >= CONV_W - 1

    mod_all = _ada_call(jnp.concatenate([c_prompt, c_sample], axis=0), w_ada, b_ada)

    qi = np.arange(WINDOW)[:, None]
    bias_p = _bias_call(rel_bias, qi + WINDOW - np.arange(2 * WINDOW)[None, :])
    qs = np.arange(SAMPLE_PAD)[:, None]
    bias_c = _bias_call(rel_bias, qs + WINDOW - np.arange(WINDOW)[None, :])
    bias_n = _bias_call(rel_bias, qs - np.arange(SAMPLE_PAD)[None, :])

    sizes = _prompt_group_sizes(bp)
    starts = np.cumsum([0] + sizes).tolist()
    xg = [x_prompt[starts[g]:starts[g + 1]].reshape(sizes[g] * tp, d) for g in range(len(sizes))]
    xs = jnp.pad(x_sample, ((0, 0), (0, SAMPLE_PAD - ts), (0, 0))).reshape(bs * SAMPLE_PAD, d)
    halo_pad = ((0, 0), (SUBLANES - (CONV_W - 1), 0), (0, 0))

    st_p, st_s = [], []
    for l in range(depth):
        lw = (norm_mix[l], norm_ffn[l], w_in[l], conv_w[l], conv_b[l], gate_b[l], attn_sinks[l], m_norm[l],
              w_out[l], peer_query[l], peer_keys[l], _pack_table(peer_u, l), _pack_table(peer_v, l))
        mod_s = [jnp.repeat(m, SAMPLE_PAD, axis=0) for m in jnp.split(mod_all[l, bp:], 6, axis=-1)]
        sp_groups = []
        for g, bg in enumerate(sizes):
            mod_g = [m.reshape(bg, 1, d) for m in jnp.split(mod_all[l, starts[g]:starts[g + 1]], 6, axis=-1)]
            zero_state = (jnp.zeros((bg, M_HEADS, M_HEAD_DIM, M_HEAD_DIM), F32),
                          jnp.zeros((bg, M_HEADS, M_HEAD_DIM), F32),
                          jnp.zeros((bg, SUBLANES, LANES), F32))
            zero_conv = jnp.zeros((bg, SUBLANES, 2 * M_WIDTH), F32)
            xg[g], sp, last_idx = _layer(xg[g], mod_g, False, bg, tp, tp, lw, bias_p, None, None, None,
                                         zero_conv, zero_state)
            sp_groups.append(sp)
        st_p.append([jnp.concatenate([sp[i] for sp in sp_groups], axis=0) for i in range(6)])
        state_s = (state_C[l].astype(F32), state_n[l].astype(F32),
                   jnp.broadcast_to(jnp.pad(state_m[l].astype(F32), ((0, 0), (0, SUBLANES - M_HEADS)))[:, :, None],
                                    (bs, SUBLANES, LANES)))
        kv_cache = (cache_k[l].reshape(bs, WINDOW, A_KV_WIDTH), cache_v[l].reshape(bs, WINDOW, A_KV_WIDTH))
        xs, ss, _ = _layer(xs, mod_s, True, bs, SAMPLE_PAD, ts, lw, None, bias_c, bias_n, kv_cache,
                           jnp.pad(state_conv[l].astype(F32), halo_pad), state_s, after=last_idx)
        st_s.append(ss)

    gfin = norm_final.reshape(1, d)
    y_prompt = jnp.concatenate([_final_call(x, gfin).reshape(bg, tp, d) for x, bg in zip(xg, sizes)], axis=0)
    y_sample = _final_call(xs, gfin).reshape(bs, SAMPLE_PAD, d)[:, :ts]
    outs_p = [jnp.stack([s[i] for s in st_p]) for i in range(6)]
    outs_s = [jnp.stack([s[i] for s in st_s]) for i in range(6)]
    return (y_prompt, y_sample, *outs_p, *outs_s)
```

```python
import functools
import math

import numpy as np
import jax
import jax.numpy as jnp
from jax import lax
from jax.experimental import pallas as pl
from jax.experimental.pallas import tpu as pltpu
from jax.experimental.pallas import tpu_sc as plsc

F32 = jnp.float32
BF16 = jnp.bfloat16
HIGHEST = lax.Precision.HIGHEST

D_MODEL = 1024
A_HEADS = 8
A_KV_HEADS = 2
A_GROUP = A_HEADS // A_KV_HEADS
A_HEAD_DIM = 64
A_WIDTH = A_HEADS * A_HEAD_DIM
A_KV_WIDTH = A_KV_HEADS * A_HEAD_DIM
WINDOW = 128
ATT_SCALE = A_HEAD_DIM ** -0.5
N_BUCKETS = 32
MAX_DISTANCE = WINDOW
M_HEADS = 4
M_HEAD_DIM = 128
M_WIDTH = M_HEADS * M_HEAD_DIM
CONV_W = 4
M_CHUNK = 64
N_KEYS = 128
P_HEADS = 8
P_TOPK = 16
P_KEY_DIM = 256
P_HALF = P_KEY_DIM // 2
EPS = 1e-6
NEG_INF = -1e30

LANES = 128
SUBLANES = 8
SAMPLE_PAD = SUBLANES
VMEM_LIMIT = 48 * 1024 * 1024
PROMPT_GROUPS = 9

NT_DIMS = (((1,), (1,)), ((), ()))
TN_DIMS = (((0,), (0,)), ((), ()))


def _cparams(*sem):
    return pltpu.CompilerParams(dimension_semantics=sem, vmem_limit_bytes=VMEM_LIMIT)


def _bdot(a, b):
    return jnp.dot(a.astype(BF16), b.astype(BF16), preferred_element_type=F32)


def _bdot_nt(a, b):
    return lax.dot_general(a.astype(BF16), b.astype(BF16), NT_DIMS, preferred_element_type=F32)


def _sigmoid(x):
    return 1.0 / (1.0 + jnp.exp(-x))


def _log_sigmoid(x):
    return jnp.minimum(x, 0.0) - jnp.log1p(jnp.exp(-jnp.abs(x)))


def _ada_kernel(c_ref, w_ref, b_ref, o_ref):
    c = c_ref[...]
    s = c * _sigmoid(c)
    o_ref[...] = jnp.dot(s, w_ref[...], precision=HIGHEST, preferred_element_type=F32) + b_ref[...]


def _ada_call(c_all, w_ada, b_ada):
    depth, d, n6 = w_ada.shape
    rows = c_all.shape[0]
    bn = 1024
    return pl.pallas_call(
        _ada_kernel,
        grid=(depth, n6 // bn),
        in_specs=[
            pl.BlockSpec((rows, d), lambda l, j: (0, 0)),
            pl.BlockSpec((None, d, bn), lambda l, j: (l, 0, j)),
            pl.BlockSpec((None, 1, bn), lambda l, j: (l, 0, j)),
        ],
        out_specs=pl.BlockSpec((None, rows, bn), lambda l, j: (l, 0, j)),
        out_shape=jax.ShapeDtypeStruct((depth, rows, n6), F32),
        compiler_params=_cparams("parallel", "parallel"),
        name="ada_mod",
    )(c_all, w_ada, b_ada.reshape(depth, 1, n6))


def _mod_spec(per_token, tm, tokens_per_req):
    if per_token:
        return pl.BlockSpec((tm, D_MODEL), lambda i: (i, 0))
    tiles = tokens_per_req // tm
    return pl.BlockSpec((None, 1, D_MODEL), lambda i: (i // tiles, 0, 0))


def _in_kernel(x_ref, sc_ref, sh_ref, g_ref, wa_ref, wqk_ref, wv_ref, wo_ref, wg_ref, gb_ref,
               qkv_ref, qkm_ref, v_ref, o_ref, gc_ref):
    x = x_ref[...]
    y = x * lax.rsqrt(jnp.mean(x * x, axis=-1, keepdims=True) + EPS) * g_ref[...]
    h = (y * (1.0 + sc_ref[...]) + sh_ref[...]).astype(BF16)
    qkv_ref[...] = jnp.dot(h, wa_ref[...], preferred_element_type=F32)
    qkm_ref[...] = jnp.dot(h, wqk_ref[...], preferred_element_type=F32)
    v_ref[...] = jnp.dot(h, wv_ref[...], preferred_element_type=F32)
    o_ref[...] = jnp.dot(h, wo_ref[...], preferred_element_type=F32)
    g = jnp.dot(h, wg_ref[...], preferred_element_type=F32) + gb_ref[...]
    lane = lax.broadcasted_iota(jnp.int32, g.shape, 1)
    gc_ref[...] = jnp.where(lane < M_HEADS, g, jnp.where(lane < 2 * M_HEADS, _log_sigmoid(g), 0.0))


def _in_call(x, sc, sh, gnorm, wa, wqk, wv, wo, wg, gb, per_token, tokens_per_req):
    n = x.shape[0]
    tm = min(512, n if per_token else tokens_per_req)
    mod = _mod_spec(per_token, tm, tokens_per_req)
    full = lambda shape: pl.BlockSpec(shape, lambda i: (0,) * len(shape))
    row = lambda w: pl.BlockSpec((tm, w), lambda i: (i, 0))
    return pl.pallas_call(
        _in_kernel,
        grid=(n // tm,),
        in_specs=[row(D_MODEL), mod, mod, full((1, D_MODEL)), full(wa.shape), full(wqk.shape),
                  full(wv.shape), full(wo.shape), full(wg.shape), full((1, LANES))],
        out_specs=[row(wa.shape[1]), row(wqk.shape[1]), row(wv.shape[1]), row(wo.shape[1]), row(LANES)],
        out_shape=[jax.ShapeDtypeStruct((n, w), F32)
                   for w in (wa.shape[1], wqk.shape[1], wv.shape[1], wo.shape[1], LANES)],
        compiler_params=_cparams("parallel"),
        name="in_proj",
    )(x, sc, sh, gnorm, wa, wqk, wv, wo, wg, gb)


def _t5_bucket_np(dist):
    n = np.maximum(dist, 0)
    max_exact = N_BUCKETS // 2
    nf = np.maximum(n, 1).astype(np.float64)
    large = max_exact + (np.log(nf / max_exact) / math.log(MAX_DISTANCE / max_exact)
                         * (N_BUCKETS - max_exact)).astype(np.int32)
    return np.where(n < max_exact, n, np.minimum(large, N_BUCKETS - 1)).astype(np.int32)


def _bias_kernel(bucket_ref, rel_ref, o_ref):
    bucket = bucket_ref[...]
    for h in range(A_HEADS):
        acc = jnp.zeros(bucket.shape, F32)
        for b in range(N_BUCKETS):
            acc = jnp.where(bucket == b, rel_ref[b, h], acc)
        o_ref[h] = acc


def _bias_call(rel_bias, dist):
    bucket = jnp.asarray(_t5_bucket_np(dist))
    nq, nk = dist.shape
    return pl.pallas_call(
        _bias_kernel,
        in_specs=[pl.BlockSpec((nq, nk), lambda: (0, 0)),
                  pl.BlockSpec(memory_space=pltpu.SMEM)],
        out_specs=pl.BlockSpec((A_HEADS, nq, nk), lambda: (0, 0, 0)),
        out_shape=jax.ShapeDtypeStruct((A_HEADS, nq, nk), F32),
        name="t5_bias",
    )(bucket, rel_bias)


def _attn_p_kernel(q_ref, kp_ref, kc_ref, vp_ref, vc_ref, bias_ref, sink_ref, o_ref):
    i = pl.program_id(1)
    qi = lax.broadcasted_iota(jnp.int32, (WINDOW, WINDOW), 0)
    kj = lax.broadcasted_iota(jnp.int32, (WINDOW, WINDOW), 1)
    valid_prev = jnp.logical_and(kj > qi, i > 0)
    valid_cur = kj <= qi
    q = q_ref[...]
    for h in range(A_HEADS):
        kv = h // A_GROUP
        qh = q[:, h * A_HEAD_DIM:(h + 1) * A_HEAD_DIM]
        sl = slice(kv * A_HEAD_DIM, (kv + 1) * A_HEAD_DIM)
        bias = bias_ref[h]
        sp = _bdot_nt(qh, kp_ref[:, sl]) * ATT_SCALE + bias[:, :WINDOW]
        sc = _bdot_nt(qh, kc_ref[:, sl]) * ATT_SCALE + bias[:, WINDOW:]
        sp = jnp.where(valid_prev, sp, NEG_INF)
        sc = jnp.where(valid_cur, sc, NEG_INF)
        sink = sink_ref[0, h]
        mx = jnp.maximum(jnp.maximum(jnp.max(sp, axis=-1, keepdims=True),
                                     jnp.max(sc, axis=-1, keepdims=True)), sink)
        ep = jnp.exp(sp - mx)
        ec = jnp.exp(sc - mx)
        den = (jnp.sum(ep, axis=-1, keepdims=True) + jnp.sum(ec, axis=-1, keepdims=True)
               + jnp.exp(sink - mx))
        o = _bdot(ep / den, vp_ref[:, sl]) + _bdot(ec / den, vc_ref[:, sl])
        o_ref[:, h * A_HEAD_DIM:(h + 1) * A_HEAD_DIM] = o.astype(o_ref.dtype)


def _attn_p_call(qkv, bias, sinks, batch, seq):
    nb = seq // WINDOW
    n = batch * seq
    kcol = A_WIDTH // A_KV_WIDTH
    vcol = kcol + 1
    cur = lambda col: pl.BlockSpec((WINDOW, A_KV_WIDTH), lambda b, i: (b * nb + i, col))
    prev = lambda col: pl.BlockSpec((WINDOW, A_KV_WIDTH),
                                    lambda b, i: (b * nb + jnp.maximum(i - 1, 0), col))
    return pl.pallas_call(
        _attn_p_kernel,
        grid=(batch, nb),
        in_specs=[pl.BlockSpec((WINDOW, A_WIDTH), lambda b, i: (b * nb + i, 0)),
                  prev(kcol), cur(kcol), prev(vcol), cur(vcol),
                  pl.BlockSpec((A_HEADS, WINDOW, 2 * WINDOW), lambda b, i: (0, 0, 0)),
                  pl.BlockSpec(memory_space=pltpu.SMEM)],
        out_specs=pl.BlockSpec((WINDOW, A_WIDTH), lambda b, i: (b * nb + i, 0)),
        out_shape=jax.ShapeDtypeStruct((n, A_WIDTH), BF16),
        compiler_params=_cparams("parallel", "parallel"),
        name="swa_prompt",
    )(qkv, qkv, qkv, qkv, qkv, bias, sinks.reshape(1, A_HEADS))


def _attn_s_kernel(n_new, qkv_ref, ck_ref, cv_ref, bc_ref, bn_ref, sink_ref,
                   o_ref, nk_ref, nv_ref, kk_s, vv_s):
    qkv = qkv_ref[...]
    knew = qkv[:, A_WIDTH:A_WIDTH + A_KV_WIDTH]
    vnew = qkv[:, A_WIDTH + A_KV_WIDTH:A_WIDTH + 2 * A_KV_WIDTH]
    ck = ck_ref[...]
    cv = cv_ref[...]
    rows = A_GROUP * SAMPLE_PAD
    qi = lax.broadcasted_iota(jnp.int32, (rows, WINDOW), 0) % SAMPLE_PAD
    kj = lax.broadcasted_iota(jnp.int32, (rows, WINDOW), 1)
    valid_c = kj > qi
    rcol = lax.broadcasted_iota(jnp.int32, (rows, 1), 0)
    qcol = rcol % SAMPLE_PAD
    for kv in range(A_KV_HEADS):
        heads = range(kv * A_GROUP, (kv + 1) * A_GROUP)
        sl = slice(kv * A_HEAD_DIM, (kv + 1) * A_HEAD_DIM)
        qs = jnp.concatenate([qkv[:, h * A_HEAD_DIM:(h + 1) * A_HEAD_DIM] for h in heads], axis=0)
        bias_c = jnp.concatenate([bc_ref[h] for h in heads], axis=0)
        bias_n = jnp.concatenate([bn_ref[h] for h in heads], axis=0)
        sink = jnp.zeros((rows, 1), F32)
        for g, h in enumerate(heads):
            sink = jnp.where(rcol // SAMPLE_PAD == g, sink_ref[0, h], sink)
        s_c = _bdot_nt(qs, ck[:, sl]) * ATT_SCALE + bias_c
        s_c = jnp.where(valid_c, s_c, NEG_INF)
        s_n = []
        for j in range(n_new):
            sj = jnp.sum(qs * knew[j:j + 1, sl], axis=-1, keepdims=True) * ATT_SCALE + bias_n[:, j:j + 1]
            s_n.append(jnp.where(qcol >= j, sj, NEG_INF))
        mx = jnp.maximum(jnp.max(s_c, axis=-1, keepdims=True), sink)
        for sj in s_n:
            mx = jnp.maximum(mx, sj)
        e_c = jnp.exp(s_c - mx)
        den = jnp.sum(e_c, axis=-1, keepdims=True) + jnp.exp(sink - mx)
        o = _bdot(e_c, cv[:, sl])
        for j, sj in enumerate(s_n):
            ej = jnp.exp(sj - mx)
            den = den + ej
            o = o + ej * vnew[j:j + 1, sl]
        o = o / den
        for g, h in enumerate(heads):
            o_ref[:, h * A_HEAD_DIM:(h + 1) * A_HEAD_DIM] = o[g * SAMPLE_PAD:(g + 1) * SAMPLE_PAD, :]
    kk_s[0:WINDOW, :] = ck
    kk_s[WINDOW:WINDOW + SAMPLE_PAD, :] = knew
    vv_s[0:WINDOW, :] = cv
    vv_s[WINDOW:WINDOW + SAMPLE_PAD, :] = vnew
    nk_ref[...] = kk_s[n_new:n_new + WINDOW, :]
    nv_ref[...] = vv_s[n_new:n_new + WINDOW, :]


def _attn_s_call(qkv, ck, cv, bias_c, bias_n, sinks, n_new):
    nreq = ck.shape[0]
    wq = qkv.shape[1]
    full3 = lambda shape: pl.BlockSpec(shape, lambda b: (0, 0, 0))
    cache = pl.BlockSpec((None, WINDOW, A_KV_WIDTH), lambda b: (b, 0, 0))
    return pl.pallas_call(
        functools.partial(_attn_s_kernel, n_new),
        grid=(nreq,),
        in_specs=[pl.BlockSpec((SAMPLE_PAD, wq), lambda b: (b, 0)), cache, cache,
                  full3(bias_c.shape), full3(bias_n.shape),
                  pl.BlockSpec(memory_space=pltpu.SMEM)],
        out_specs=[pl.BlockSpec((SAMPLE_PAD, A_WIDTH), lambda b: (b, 0)), cache, cache],
        out_shape=[jax.ShapeDtypeStruct((nreq * SAMPLE_PAD, A_WIDTH), F32),
                   jax.ShapeDtypeStruct(ck.shape, F32), jax.ShapeDtypeStruct(cv.shape, F32)],
        scratch_shapes=[pltpu.VMEM((WINDOW + SAMPLE_PAD, A_KV_WIDTH), F32),
                        pltpu.VMEM((WINDOW + SAMPLE_PAD, A_KV_WIDTH), F32)],
        compiler_params=_cparams("parallel"),
        name="swa_sample",
    )(qkv, ck, cv, bias_c, bias_n, sinks.reshape(1, A_HEADS))


def _mlstm_kernel(chunk, t_valid, qk_ref, v_ref, og_ref, gc_ref, cw_ref, cb_ref, mn_ref,
                  conv0_ref, c0_ref, n0_ref, m0_ref,
                  out_ref, cout_ref, nout_ref, mout_ref,
                  xp_s, c_s, n_s, m_s):
    step = pl.program_id(1)
    halo = SUBLANES

    @pl.when(step == 0)
    def _():
        xp_s[0:halo, :] = conv0_ref[...]
        c_s[...] = c0_ref[...]
        n_s[...] = n0_ref[...]
        m_s[...] = m0_ref[...]

    xp_s[halo:halo + chunk, :] = qk_ref[...]
    cw = cw_ref[...]
    y = cb_ref[...]
    for i in range(CONV_W):
        off = halo - (CONV_W - 1) + i
        y = y + xp_s[off:off + chunk, :] * cw[i:i + 1, :]
    xp_s[0:halo, :] = xp_s[chunk:chunk + halo, :]
    y = y * _sigmoid(y)
    q_all = y[:, :M_WIDTH]
    k_all = y[:, M_WIDTH:] * (M_HEAD_DIM ** -0.5)

    g = gc_ref[...]
    if t_valid < chunk:
        row = lax.broadcasted_iota(jnp.int32, g.shape, 0)
        lane = lax.broadcasted_iota(jnp.int32, g.shape, 1)
        g = jnp.where(row < t_valid, g, jnp.where(lane < M_HEADS, NEG_INF, 0.0))
    tr = lax.broadcasted_iota(jnp.int32, (chunk, chunk), 0)
    tc = lax.broadcasted_iota(jnp.int32, (chunk, chunk), 1)
    causal = tr >= tc
    tri = causal.astype(F32)
    bcol = jnp.dot(tri, g, precision=HIGHEST, preferred_element_type=F32)
    er = lax.broadcasted_iota(jnp.int32, (SUBLANES, LANES), 0)
    ec = lax.broadcasted_iota(jnp.int32, (SUBLANES, LANES), 1)
    eye = (er == ec).astype(F32)
    g_rows = lax.dot_general(eye, g, NT_DIMS, precision=HIGHEST, preferred_element_type=F32)
    b_rows = lax.dot_general(eye, bcol, NT_DIMS, precision=HIGHEST, preferred_element_type=F32)

    for h in range(M_HEADS):
        hs = slice(h * M_HEAD_DIM, (h + 1) * M_HEAD_DIM)
        b_c = bcol[:, M_HEADS + h:M_HEADS + h + 1]
        ig_c = g[:, h:h + 1]
        b_r = b_rows[M_HEADS + h:M_HEADS + h + 1, :]
        ig_r = g_rows[h:h + 1, :]
        m_prev = m_s[h:h + 1, 0:1]
        logw = jnp.where(causal, b_c - b_r + ig_r, -jnp.inf)
        inter = b_c + m_prev
        m_t = jnp.maximum(inter, jnp.max(logw, axis=-1, keepdims=True))
        w = jnp.exp(logw - m_t)
        a = jnp.exp(inter - m_t)
        q = q_all[:, hs]
        k = k_all[:, hs]
        v = v_ref[:, hs]
        cmat = c_s[h]
        nvec = n_s[h:h + 1, :]
        wqk = w * _bdot_nt(q, k)
        num = _bdot(wqk, v) + a * _bdot_nt(q, cmat)
        den = jnp.sum(wqk, axis=-1, keepdims=True) + a * jnp.sum(q * nvec, axis=-1, keepdims=True)
        hh = num / jnp.maximum(jnp.abs(den), jnp.exp(-m_t))
        m_new = m_t[chunk - 1:chunk, :]
        b_last = b_c[chunk - 1:chunk, :]
        wl = jnp.exp(b_last - b_c + ig_c - m_new)
        al = jnp.exp(b_last + m_prev - m_new)
        c_s[h] = al * cmat + lax.dot_general((v * wl).astype(BF16), k.astype(BF16), TN_DIMS,
                                             preferred_element_type=F32)
        n_s[h:h + 1, :] = al * nvec + jnp.sum(wl * k, axis=0, keepdims=True)
        m_s[h:h + 1, :] = jnp.broadcast_to(m_new, (1, LANES))
        hn = hh * lax.rsqrt(jnp.mean(hh * hh, axis=-1, keepdims=True) + EPS) * mn_ref[:, hs]
        out_ref[:, hs] = (_sigmoid(og_ref[:, hs]) * hn).astype(out_ref.dtype)

    @pl.when(step == pl.num_programs(1) - 1)
    def _():
        cout_ref[...] = c_s[...]
        nout_ref[...] = n_s[...]
        mout_ref[...] = m_s[...]


def _mlstm_call(qk, v, og, gc, conv_w, conv_b, m_norm, conv0, c0, n0, m0, batch, seq, chunk, t_valid):
    nc = seq // chunk
    n = batch * seq
    row = lambda w: pl.BlockSpec((chunk, w), lambda b, c: (b * nc + c, 0))
    full2 = lambda shape: pl.BlockSpec(shape, lambda b, c: (0, 0))
    per_b = lambda shape: pl.BlockSpec((None,) + shape, lambda b, c: (b,) + (0,) * len(shape))
    dh = M_HEAD_DIM
    return pl.pallas_call(
        functools.partial(_mlstm_kernel, chunk, t_valid),
        grid=(batch, nc),
        in_specs=[row(2 * M_WIDTH), row(M_WIDTH), row(M_WIDTH), row(LANES),
                  full2((CONV_W, 2 * M_WIDTH)), full2((1, 2 * M_WIDTH)), full2((1, M_WIDTH)),
                  per_b((SUBLANES, 2 * M_WIDTH)), per_b((M_HEADS, dh, dh)), per_b((M_HEADS, dh)),
                  per_b((SUBLANES, LANES))],
        out_specs=[row(M_WIDTH), per_b((M_HEADS, dh, dh)), per_b((M_HEADS, dh)), per_b((SUBLANES, LANES))],
        out_shape=[jax.ShapeDtypeStruct((n, M_WIDTH), BF16 if chunk % (2 * SUBLANES) == 0 else F32),
                   jax.ShapeDtypeStruct((batch, M_HEADS, dh, dh), F32),
                   jax.ShapeDtypeStruct((batch, M_HEADS, dh), F32),
                   jax.ShapeDtypeStruct((batch, SUBLANES, LANES), F32)],
        scratch_shapes=[pltpu.VMEM((SUBLANES + chunk, 2 * M_WIDTH), F32),
                        pltpu.VMEM((M_HEADS, dh, dh), F32),
                        pltpu.VMEM((M_HEADS, dh), F32),
                        pltpu.VMEM((SUBLANES, LANES), F32)],
        compiler_params=_cparams("parallel", "arbitrary"),
        name="mlstm",
    )(qk, v, og, gc, conv_w, conv_b, m_norm, conv0, c0, n0, m0)


def _pack_words(x):
    bits = pltpu.bitcast(x.astype(BF16).astype(F32), jnp.uint32)
    half = x.shape[1] // 2
    lo = lax.shift_right_logical(bits[:, :half], jnp.uint32(16))
    hi = bits[:, half:] & jnp.uint32(0xFFFF0000)
    return pltpu.bitcast(lo | hi, jnp.int32)


def _out_kernel(att_ref, mo_ref, x_ref, g1_ref, sc_ref, sh_ref, gn_ref, wa_ref, wm_ref, wq_ref,
                xo_ref, h2_ref, qp_ref):
    mix = (jnp.dot(att_ref[...].astype(BF16), wa_ref[...], preferred_element_type=F32)
           + jnp.dot(mo_ref[...].astype(BF16), wm_ref[...], preferred_element_type=F32))
    x = x_ref[...] + g1_ref[...] * mix
    xo_ref[...] = x
    y = x * lax.rsqrt(jnp.mean(x * x, axis=-1, keepdims=True) + EPS) * gn_ref[...]
    h2 = y * (1.0 + sc_ref[...]) + sh_ref[...]
    qp_ref[...] = jnp.dot(h2.astype(BF16), wq_ref[...], preferred_element_type=F32).astype(qp_ref.dtype)
    h2_ref[...] = _pack_words(h2)


def _out_call(att, mo, x, g1, sc, sh, gnorm, wa, wm, wq, per_token, tokens_per_req):
    n = x.shape[0]
    tm = min(256, n if per_token else tokens_per_req)
    mod = _mod_spec(per_token, tm, tokens_per_req)
    full = lambda shape: pl.BlockSpec(shape, lambda i: (0,) * len(shape))
    row = lambda w: pl.BlockSpec((tm, w), lambda i: (i, 0))
    nq = wq.shape[1]
    return pl.pallas_call(
        _out_kernel,
        grid=(n // tm,),
        in_specs=[row(A_WIDTH), row(M_WIDTH), row(D_MODEL), mod, mod, mod, full((1, D_MODEL)),
                  full(wa.shape), full(wm.shape), full(wq.shape)],
        out_specs=[row(D_MODEL), row(D_MODEL // 2), row(nq)],
        out_shape=[jax.ShapeDtypeStruct((n, D_MODEL), F32), jax.ShapeDtypeStruct((n, D_MODEL // 2), jnp.int32),
                   jax.ShapeDtypeStruct((n, nq), BF16)],
        compiler_params=_cparams("parallel"),
        name="out_proj",
    )(att, mo, x, g1, sc, sh, gnorm, wa, wm, wq)


def _pk_cells():
    return [(a, b) for a in range(P_TOPK) for b in range(P_TOPK) if (a + 1) * (b + 1) <= P_TOPK]


PK_CELL_ROWS = 64


def _pk_expand_mats():
    cells = _pk_cells()
    e0 = np.zeros((PK_CELL_ROWS, LANES), np.float32)
    e1 = np.zeros((PK_CELL_ROWS, LANES), np.float32)
    for j, (a, b) in enumerate(cells):
        e0[j, a] = 1.0
        e1[j, b] = 1.0
    return e0, e1, len(cells)


def _top_rows(s, rowf, rounds):
    n_rows = s.shape[0]
    vals, idxs = [], []
    for _ in range(rounds):
        m = jnp.max(s, axis=0, keepdims=True)
        i = jnp.min(jnp.where(s == m, rowf, float(n_rows)), axis=0, keepdims=True)
        vals.append(m)
        idxs.append(i)
        s = jnp.where(rowf == i, -jnp.inf, s)
    return jnp.concatenate(vals, axis=0), jnp.concatenate(idxs, axis=0)


def _select_kernel(n_cells, qp_ref, keys_ref, e0_ref, e1_ref, idx_ref, gw_ref, idx_s, gw_s):
    tm = qp_ref.shape[0]
    keyf = lax.broadcasted_iota(jnp.int32, (N_KEYS, tm), 0).astype(F32)
    cellf = lax.broadcasted_iota(jnp.int32, (PK_CELL_ROWS, tm), 0).astype(F32)
    e0 = e0_ref[...]
    e1 = e1_ref[...]
    pad = jnp.zeros((LANES - P_TOPK, tm), F32)

    def head(h, carry):
        h = jnp.asarray(h, jnp.int32)
        sub = []
        for c in range(2):
            col = pl.multiple_of((h * 2 + c) * P_HALF, P_HALF)
            s = _bdot_nt(keys_ref[h, c], qp_ref[:, pl.ds(col, P_HALF)])
            sub.append(_top_rows(s, keyf, P_TOPK))
        (v0, i0), (v1, i1) = sub
        expand = lambda e, x: jnp.dot(e, jnp.concatenate([x, pad], axis=0), precision=HIGHEST,
                                      preferred_element_type=F32)
        cand = expand(e0, v0) + expand(e1, v1)
        cidx = expand(e0, i0 * float(N_KEYS)) + expand(e1, i1)
        cand = jnp.where(cellf < n_cells, cand, -jnp.inf)
        best, eidx = [], []
        for _ in range(P_TOPK):
            m = jnp.max(cand, axis=0, keepdims=True)
            j = jnp.min(jnp.where(cand == m, cellf, float(PK_CELL_ROWS)), axis=0, keepdims=True)
            hit = cellf == j
            eidx.append(jnp.max(jnp.where(hit, cidx, -1.0), axis=0, keepdims=True))
            best.append(m)
            cand = jnp.where(hit, -jnp.inf, cand)
        best = jnp.concatenate(best, axis=0)
        e = jnp.exp(best - best[0:1, :])
        row0 = pl.multiple_of(h * P_TOPK, P_TOPK)
        gw_s[pl.ds(row0, P_TOPK), :] = e / jnp.sum(e, axis=0, keepdims=True)
        idx_s[pl.ds(row0, P_TOPK), :] = jnp.concatenate(eidx, axis=0)
        return carry

    lax.fori_loop(0, P_HEADS, head, 0)
    idx_ref[...] = idx_s[...].T.astype(jnp.int32)
    gw_ref[...] = gw_s[...].T


def _select_call(qp, keys_bf16):
    n = qp.shape[0]
    tm = min(LANES, n)
    e0, e1, n_cells = _pk_expand_mats()
    full = lambda shape: pl.BlockSpec(shape, lambda i: (0,) * len(shape))
    return pl.pallas_call(
        functools.partial(_select_kernel, n_cells),
        grid=(n // tm,),
        in_specs=[pl.BlockSpec((tm, qp.shape[1]), lambda i: (i, 0)), full(keys_bf16.shape),
                  full((PK_CELL_ROWS, LANES)), full((PK_CELL_ROWS, LANES))],
        out_specs=[pl.BlockSpec((tm, LANES), lambda i: (i, 0)), pl.BlockSpec((tm, LANES), lambda i: (i, 0))],
        out_shape=[jax.ShapeDtypeStruct((n, LANES), jnp.int32), jax.ShapeDtypeStruct((n, LANES), F32)],
        scratch_shapes=[pltpu.VMEM((P_HEADS * P_TOPK, tm), F32), pltpu.VMEM((P_HEADS * P_TOPK, tm), F32)],
        compiler_params=_cparams("parallel"),
        name="peer_select",
    )(qp, keys_bf16, jnp.asarray(e0), jnp.asarray(e1))


N_SEL = P_HEADS * P_TOPK

SC_CORES = 2
SC_SUBCORES = 16
SC_LANES = 16
SC_TOK_BLOCK = 8
SC_ROWS = 32
SC_ROW_BLOCK = 16
SC_ACC_CHAINS = 4
SC_ACC_ROWS = 32
SC_NBUF = 4
N_WCOL = D_MODEL // (2 * SC_LANES)


def _pack_kernel(t_ref, o_ref):
    o_ref[...] = _pack_words(t_ref[...])


def _pack_table(tables, layer):
    _, e, dcol = tables.shape
    tm = min(1024, e)
    return pl.pallas_call(
        _pack_kernel,
        grid=(e // tm,),
        in_specs=[pl.BlockSpec((None, tm, dcol), lambda i: (layer, i, 0))],
        out_specs=pl.BlockSpec((tm, dcol // 2), lambda i: (i, 0)),
        out_shape=jax.ShapeDtypeStruct((e, dcol // 2), jnp.int32),
        compiler_params=_cparams("parallel"),
        name="pack_table",
    )(tables)


def _sc_gelu(x):
    z = math.sqrt(2.0 / math.pi) * (x + 0.044715 * (x * x * x))
    t = 1.0 - 2.0 / (jnp.exp(2.0 * z) + 1.0)
    return x * (0.5 * (1.0 + t))


def _sc_expert_body(tokens_per_worker, idx_hbm, gw_hbm, h2_hbm, u_hbm, v_hbm, y_hbm,
                    idx_v, gw_v, x_v, o_v, buf, coef_v, tr_v, sem, in_sem, out_sem):
    wid = lax.axis_index("s") * SC_CORES + lax.axis_index("c")
    base = wid * tokens_per_worker
    lane = lax.iota(jnp.int32, SC_LANES)
    n_gather = N_SEL // SC_ROWS
    zero = jnp.zeros((SC_LANES,), F32)

    n_steps = 2 * n_gather
    assert n_steps % SC_NBUF == 0

    def gather(p, tt, i):
        table = u_hbm if i < n_gather else v_hbm
        j = i % n_gather
        slot = i % SC_NBUF
        return pltpu.make_async_copy(table.at[idx_v.at[p, tt, pl.ds(j * SC_ROWS, SC_ROWS)]], buf.at[slot],
                                     sem.at[slot])

    def unpack(w):
        lo = lax.bitcast_convert_type(lax.shift_left(w, jnp.full((SC_LANES,), 16, jnp.int32)), F32)
        hi = lax.bitcast_convert_type(w & jnp.full((SC_LANES,), -65536, jnp.int32), F32)
        return lo, hi

    def packed(w):
        return plsc.bitcast(w, BF16)

    def unpack_sum(s):
        return unpack(plsc.bitcast(s, jnp.int32))

    def act_chunk(p, tt, j, slot):
        @pl.loop(0, SC_ROWS // SC_LANES)
        def _(half):
            for rb in range(SC_LANES // SC_ROW_BLOCK):
                r0 = half * SC_LANES + rb * SC_ROW_BLOCK

                def col(c, accs):
                    w0 = pl.multiple_of(c * (4 * SC_LANES), 4 * SC_LANES)
                    xs = [packed(x_v[p, tt, pl.ds(w0 + q * SC_LANES, SC_LANES)]) for q in range(4)]
                    out = []
                    for r, a in enumerate(accs):
                        us = [packed(buf[slot, r0 + r, pl.ds(w0 + q * SC_LANES, SC_LANES)]) for q in range(4)]
                        lo, hi = unpack_sum((us[0] * xs[0] + us[1] * xs[1]) + (us[2] * xs[2] + us[3] * xs[3]))
                        out.append(a + lo + hi)
                    return tuple(out)

                accs = lax.fori_loop(0, N_WCOL // 4, col, (zero,) * SC_ROW_BLOCK)
                for r in range(SC_ROW_BLOCK):
                    tr_v[pl.ds((rb * SC_ROW_BLOCK + r) * SC_LANES, SC_LANES)] = accs[r]
            tot = zero
            for jj in range(SC_LANES):
                tot = tot + plsc.load_gather(tr_v, [lane * SC_LANES + jj])
            k0 = pl.multiple_of(j * SC_ROWS + half * SC_LANES, SC_LANES)
            coef_v[pl.ds(k0, SC_LANES)] = gw_v[p, tt, pl.ds(k0, SC_LANES)] * _sc_gelu(tot)

    def acc_chunk(p, tt, j, slot, first):
        def tree_sum(parts):
            while len(parts) > 1:
                parts = [parts[i] + parts[i + 1] for i in range(0, len(parts), 2)]
            return parts[0]

        for rb in range(SC_ROWS // SC_ACC_ROWS):
            rows = list(range(rb * SC_ACC_ROWS, (rb + 1) * SC_ACC_ROWS))
            splat = {}
            for r in rows:
                c16 = plsc.load_gather(coef_v, [jnp.full((SC_LANES,), j * SC_ROWS + r, jnp.int32)])
                splat[r] = plsc.pack(c16, c16, format=plsc.PackFormat.INTERLEAVED)
            fresh = first and rb == 0

            @plsc.parallel_loop(0, N_WCOL)
            def _(c):
                woff = pl.multiple_of(c * SC_LANES, SC_LANES)
                hoff = pl.multiple_of(c * SC_LANES + D_MODEL // 2, SC_LANES)
                pa, pb = [], []
                for n in range(0, SC_ACC_ROWS, SC_ACC_CHAINS):
                    prod = [splat[r] * packed(buf[slot, r, pl.ds(woff, SC_LANES)])
                            for r in rows[n:n + SC_ACC_CHAINS]]
                    lo, hi = unpack_sum(tree_sum(prod))
                    pa.append(lo)
                    pb.append(hi)
                sa, sb = tree_sum(pa), tree_sum(pb)
                if not fresh:
                    sa = sa + o_v[p, tt, pl.ds(woff, SC_LANES)]
                    sb = sb + o_v[p, tt, pl.ds(hoff, SC_LANES)]
                o_v[p, tt, pl.ds(woff, SC_LANES)] = sa
                o_v[p, tt, pl.ds(hoff, SC_LANES)] = sb

    n_blocks = tokens_per_worker // SC_TOK_BLOCK

    def block_start(blk):
        return pl.multiple_of(base + blk * SC_TOK_BLOCK, SC_TOK_BLOCK)

    def in_copies(blk, p):
        rows = pl.ds(block_start(blk), SC_TOK_BLOCK)
        return [pltpu.make_async_copy(idx_hbm.at[rows], idx_v.at[p], in_sem.at[p]),
                pltpu.make_async_copy(gw_hbm.at[rows], gw_v.at[p], in_sem.at[p]),
                pltpu.make_async_copy(h2_hbm.at[rows], x_v.at[p], in_sem.at[p])]

    def out_copy(blk, p):
        return pltpu.make_async_copy(o_v.at[p], y_hbm.at[pl.ds(block_start(blk), SC_TOK_BLOCK)], out_sem.at[p])

    ahead = SC_NBUF - 1
    for cp in in_copies(0, 0):
        cp.start()
    for cp in in_copies(0, 0):
        cp.wait()
    for cp in in_copies(1, 1):
        cp.start()
    for i in range(ahead):
        gather(0, 0, i).start()

    @pl.loop(0, n_blocks)
    def _(blk):
        p = lax.rem(blk, 2)

        @pl.when(blk >= 2)
        def _():
            out_copy(blk - 2, p).wait()

        @pl.loop(0, SC_TOK_BLOCK)
        def _(tt):
            for i in range(n_steps):
                if i + ahead < n_steps:
                    gather(p, tt, i + ahead).start()
                else:
                    nxt = i + ahead - n_steps

                    @pl.when(tt + 1 < SC_TOK_BLOCK)
                    def _():
                        gather(p, tt + 1, nxt).start()

                    @pl.when(jnp.logical_and(tt + 1 == SC_TOK_BLOCK, blk + 1 < n_blocks))
                    def _():
                        if nxt == 0:
                            for cp in in_copies(blk + 1, 1 - p):
                                cp.wait()
                        gather(1 - p, 0, nxt).start()
                gather(p, tt, i).wait()
                if i < n_gather:
                    act_chunk(p, tt, i, i % SC_NBUF)
                else:
                    acc_chunk(p, tt, i - n_gather, i % SC_NBUF, i == n_gather)

        out_copy(blk, p).start()

        @pl.when(blk + 2 < n_blocks)
        def _():
            for cp in in_copies(blk + 2, p):
                cp.start()

    for blk in (n_blocks - 2, n_blocks - 1):
        out_copy(blk, blk % 2).wait()


def _sc_expert_call(idx, gw, xw, u, v):
    n = idx.shape[0]
    workers = SC_CORES * SC_SUBCORES
    assert n % (workers * SC_TOK_BLOCK) == 0 and n // (workers * SC_TOK_BLOCK) >= 2
    mesh = plsc.VectorSubcoreMesh(core_axis_name="c", subcore_axis_name="s")
    return pl.kernel(
        functools.partial(_sc_expert_body, n // workers),
        out_type=jax.ShapeDtypeStruct((n, D_MODEL), F32),
        mesh=mesh,
        scratch_types=[pltpu.VMEM((2, SC_TOK_BLOCK, N_SEL), jnp.int32),
                       pltpu.VMEM((2, SC_TOK_BLOCK, N_SEL), F32),
                       pltpu.VMEM((2, SC_TOK_BLOCK, D_MODEL // 2), jnp.int32),
                       pltpu.VMEM((2, SC_TOK_BLOCK, D_MODEL), F32),
                       pltpu.VMEM((SC_NBUF, SC_ROWS, D_MODEL // 2), jnp.int32),
                       pltpu.VMEM((N_SEL,), F32),
                       pltpu.VMEM((SC_LANES * SC_LANES,), F32),
                       pltpu.SemaphoreType.DMA((SC_NBUF,)), pltpu.SemaphoreType.DMA((2,)),
                       pltpu.SemaphoreType.DMA((2,))],
        compiler_params=pltpu.CompilerParams(needs_layout_passes=False),
        name="peer_experts_sc",
    )(idx, gw, xw, u, v)


def _resid_kernel(x_ref, y_ref, g_ref, o_ref):
    o_ref[...] = x_ref[...] + g_ref[...] * y_ref[...]


def _resid_call(x, y, g2, per_token, tokens_per_req):
    n = x.shape[0]
    tm = min(512, n if per_token else tokens_per_req)
    row = pl.BlockSpec((tm, D_MODEL), lambda i: (i, 0))
    return pl.pallas_call(
        _resid_kernel,
        grid=(n // tm,),
        in_specs=[row, row, _mod_spec(per_token, tm, tokens_per_req)],
        out_specs=row,
        out_shape=jax.ShapeDtypeStruct((n, D_MODEL), F32),
        compiler_params=_cparams("parallel"),
        name="peer_residual",
    )(x, y, g2)


def _final_kernel(x_ref, g_ref, o_ref):
    x = x_ref[...]
    o_ref[...] = x * lax.rsqrt(jnp.mean(x * x, axis=-1, keepdims=True) + EPS) * g_ref[...]


def _final_call(x, g):
    n = x.shape[0]
    tm = min(512, n)
    return pl.pallas_call(
        _final_kernel,
        grid=(n // tm,),
        in_specs=[pl.BlockSpec((tm, D_MODEL), lambda i: (i, 0)), pl.BlockSpec((1, D_MODEL), lambda i: (0, 0))],
        out_specs=pl.BlockSpec((tm, D_MODEL), lambda i: (i, 0)),
        out_shape=jax.ShapeDtypeStruct((n, D_MODEL), F32),
        compiler_params=_cparams("parallel"),
        name="final_norm",
    )(x, g)


def _split_w_in(w_in_l, gate_b_l):
    cuts = np.cumsum([A_WIDTH + 2 * A_KV_WIDTH, 2 * M_WIDTH, M_WIDTH, M_WIDTH]).tolist()
    wa = w_in_l[:, :cuts[0]].astype(BF16)
    wqk = w_in_l[:, cuts[0]:cuts[1]].astype(BF16)
    wv = w_in_l[:, cuts[1]:cuts[2]].astype(BF16)
    wo = w_in_l[:, cuts[2]:cuts[3]].astype(BF16)
    ng = 2 * M_HEADS
    wg = jnp.pad(w_in_l[:, cuts[3]:], ((0, 0), (0, LANES - ng))).astype(BF16)
    gb = jnp.pad(gate_b_l.astype(F32), (0, LANES - ng)).reshape(1, LANES)
    return wa, wqk, wv, wo, wg, gb


def _layer(x, mods, per_token, batch, seq, t_valid, lw, bias_p, bias_c, bias_n, kv_cache, conv0, state,
           after=None):
    (norm_mix, norm_ffn, w_in, conv_w, conv_b, gate_b, sinks, m_norm, w_out, peer_query, peer_keys,
     peer_u, peer_v) = lw
    if after is not None:
        x, _ = lax.optimization_barrier((x, after))
    sh1, sc1, g1, sh2, sc2, g2 = mods
    wa, wqk, wv, wo, wg, gb = _split_w_in(w_in, gate_b)
    qkv, qkm, vm, om, gc = _in_call(x, sc1, sh1, norm_mix.reshape(1, -1), wa, wqk, wv, wo, wg, gb,
                                    per_token, seq)
    if kv_cache is None:
        att = _attn_p_call(qkv, bias_p, sinks, batch, seq)
        kv3 = qkv.reshape(batch, seq, -1)
        new_k = kv3[:, seq - WINDOW:, A_WIDTH:A_WIDTH + A_KV_WIDTH]
        new_v = kv3[:, seq - WINDOW:, A_WIDTH + A_KV_WIDTH:]
        chunk = M_CHUNK
    else:
        att, new_k, new_v = _attn_s_call(qkv, kv_cache[0], kv_cache[1], bias_c, bias_n, sinks, t_valid)
        chunk = seq
    c0, n0, m0 = state
    mo, c_new, n_new, m_new = _mlstm_call(qkm, vm, om, gc, conv_w, conv_b.reshape(1, -1),
                                          m_norm.reshape(1, -1), conv0, c0, n0, m0,
                                          batch, seq, chunk, min(t_valid, chunk))
    new_conv = qkm.reshape(batch, seq, -1)[:, t_valid - (CONV_W - 1):t_valid]
    pad_rows = seq - t_valid
    if pad_rows:
        keep = lambda a: a.reshape(batch, seq, -1)[:, :t_valid].reshape(batch * t_valid, -1)
        att, mo, x, g1, sc2, sh2, g2 = (keep(a) for a in (att, mo, x, g1, sc2, sh2, g2))
    x_mid, xw, qp = _out_call(att, mo, x, g1, sc2, sh2, norm_ffn.reshape(1, -1),
                              w_out[:A_WIDTH].astype(BF16), w_out[A_WIDTH:].astype(BF16),
                              peer_query.astype(BF16), per_token, t_valid)
    idx, gw = _select_call(qp, peer_keys.astype(BF16))
    y = _sc_expert_call(idx, gw, xw, peer_u, peer_v)
    x_new = _resid_call(x_mid, y, g2, per_token, t_valid)
    if pad_rows:
        x_new = jnp.pad(x_new.reshape(batch, t_valid, -1), ((0, 0), (0, pad_rows), (0, 0))).reshape(batch * seq, -1)
    new_k = new_k.reshape(batch, WINDOW, A_KV_HEADS, A_HEAD_DIM)
    new_v = new_v.reshape(batch, WINDOW, A_KV_HEADS, A_HEAD_DIM)
    return x_new, (new_k, new_v, new_conv, c_new, n_new, m_new[:, :M_HEADS, 0]), idx


def _prompt_group_sizes(n_req):
    if n_req < PROMPT_GROUPS:
        return [1] * n_req
    mid, n_mid = n_req - 2, PROMPT_GROUPS - 2
    weights = [i + 2 for i in range(n_mid)]
    sizes = [max(1, mid * w // sum(weights)) for w in weights]
    for i in range(mid - sum(sizes)):
        sizes[n_mid - 1 - i % n_mid] += 1
    return [1] + sizes + [1]


def kernel(x_prompt, x_sample, c_prompt, c_sample, cache_k, cache_v, state_conv, state_C, state_n, state_m, rel_bias, w_ada, b_ada, norm_mix, norm_ffn, w_in, conv_w, conv_b, gate_b, attn_sinks, m_norm, w_out, peer_query, peer_keys, peer_u, peer_v, norm_final):
    depth = w_ada.shape[0]
    bp, tp, d = x_prompt.shape
    bs, ts, _ = x_sample.shape
    assert tp % WINDOW == 0 and tp % M_CHUNK == 0 and ts <= SAMPLE_PAD and ts >= CONV_W - 1

    mod_all = _ada_call(jnp.concatenate([c_prompt, c_sample], axis=0), w_ada, b_ada)

    qi = np.arange(WINDOW)[:, None]
    bias_p = _bias_call(rel_bias, qi + WINDOW - np.arange(2 * WINDOW)[None, :])
    qs = np.arange(SAMPLE_PAD)[:, None]
    bias_c = _bias_call(rel_bias, qs + WINDOW - np.arange(WINDOW)[None, :])
    bias_n = _bias_call(rel_bias, qs - np.arange(SAMPLE_PAD)[None, :])

    sizes = _prompt_group_sizes(bp)
    starts = np.cumsum([0] + sizes).tolist()
    xg = [x_prompt[starts[g]:starts[g + 1]].reshape(sizes[g] * tp, d) for g in range(len(sizes))]
    xs = jnp.pad(x_sample, ((0, 0), (0, SAMPLE_PAD - ts), (0, 0))).reshape(bs * SAMPLE_PAD, d)
    halo_pad = ((0, 0), (SUBLANES - (CONV_W - 1), 0), (0, 0))

    st_p, st_s = [], []
    for l in range(depth):
        lw = (norm_mix[l], norm_ffn[l], w_in[l], conv_w[l], conv_b[l], gate_b[l], attn_sinks[l], m_norm[l],
              w_out[l], peer_query[l], peer_keys[l], _pack_table(peer_u, l), _pack_table(peer_v, l))
        mod_s = [jnp.repeat(m, SAMPLE_PAD, axis=0) for m in jnp.split(mod_all[l, bp:], 6, axis=-1)]
        sp_groups = []
        for g, bg in enumerate(sizes):
            mod_g = [m.reshape(bg, 1, d) for m in jnp.split(mod_all[l, starts[g]:starts[g + 1]], 6, axis=-1)]
            zero_state = (jnp.zeros((bg, M_HEADS, M_HEAD_DIM, M_HEAD_DIM), F32),
                          jnp.zeros((bg, M_HEADS, M_HEAD_DIM), F32),
                          jnp.zeros((bg, SUBLANES, LANES), F32))
            zero_conv = jnp.zeros((bg, SUBLANES, 2 * M_WIDTH), F32)
            xg[g], sp, last_idx = _layer(xg[g], mod_g, False, bg, tp, tp, lw, bias_p, None, None, None,
                                         zero_conv, zero_state)
            sp_groups.append(sp)
        st_p.append([jnp.concatenate([sp[i] for sp in sp_groups], axis=0) for i in range(6)])
        state_s = (state_C[l].astype(F32), state_n[l].astype(F32),
                   jnp.broadcast_to(jnp.pad(state_m[l].astype(F32), ((0, 0), (0, SUBLANES - M_HEADS)))[:, :, None],
                                    (bs, SUBLANES, LANES)))
        kv_cache = (cache_k[l].reshape(bs, WINDOW, A_KV_WIDTH), cache_v[l].reshape(bs, WINDOW, A_KV_WIDTH))
        xs, ss, _ = _layer(xs, mod_s, True, bs, SAMPLE_PAD, ts, lw, None, bias_c, bias_n, kv_cache,
                           jnp.pad(state_conv[l].astype(F32), halo_pad), state_s, after=last_idx)
        st_s.append(ss)

    gfin = norm_final.reshape(1, d)
    y_prompt = jnp.concatenate([_final_call(x, gfin).reshape(bg, tp, d) for x, bg in zip(xg, sizes)], axis=0)
    y_sample = _final_call(xs, gfin).reshape(bs, SAMPLE_PAD, d)[:, :ts]
    outs_p = [jnp.stack([s[i] for s in st_p]) for i in range(6)]
    outs_s = [jnp.stack([s[i] for s in st_s]) for i in range(6)]
    return (y_prompt, y_sample, *outs_p, *outs_s)
```
